```python
import math
import functools
import jax
import jax.numpy as jnp
from jax import lax
import numpy as np

D_MODEL = 1024
BATCH = 4
SEQ = 4096
DEPTH = 1
DEC_BATCH = 32
DEC_SEQ = 4
PAST_LEN = 8192
PAGE_SIZE = 128

H_M = 4
D_M = D_MODEL // 2
DK_M = D_M // H_M
DV_M = D_M // H_M
CONV_W = 4
MLSTM_CHUNK = 64
H_A = 8
D_A = D_MODEL - D_M
HD_A = D_A // H_A
BRANCHES = ((128, 1), (512, 4), (2048, 16))
WINDOW_MAX = 2048
QB = 128
N_BUCKETS = 32
MAX_DISTANCE = 2048
N_EXPERTS = 32
TOP_K = 4
D_FF = D_MODEL
SWIGLU_LIMIT = 7.0
SWIGLU_ALPHA = 1.702
MOE_BLOCK = 128
EPS = 1e-6

OFF_QM = 0
OFF_KM = OFF_QM + D_M
OFF_VM = OFF_KM + D_M
OFF_OM = OFF_VM + D_M
OFF_IM = OFF_OM + D_M
OFF_FM = OFF_IM + H_M
OFF_QA = OFF_FM + H_M
OFF_KA = OFF_QA + D_A
OFF_VA = OFF_KA + D_A
D_IN = OFF_VA + D_A
D_MIX = D_M + D_A

kernel_name = 'hymba_mlstm_dilated_attn_moe_step'

F32 = jnp.float32


def rmsnorm(x, g):
    xf = x.astype(F32)
    y = xf * lax.rsqrt(jnp.mean(xf * xf, axis=-1, keepdims=True) + EPS)
    return (y * g.astype(F32)).astype(x.dtype)


def adaln(c, w_ada, b_ada):
    mod = jnp.dot(jax.nn.silu(c), w_ada) + b_ada
    return jnp.split(mod[:, None, :], 6, axis=-1)


def causal_conv(u, buf, w, b):
    full = jnp.concatenate([buf.astype(u.dtype), u], axis=1)
    T = u.shape[1]
    out = b
    for j in range(CONV_W):
        out = out + full[:, j:j + T] * w[j]
    return out, full[:, full.shape[1] - (CONV_W - 1):]


def t5_bucket(dist):
    max_exact = N_BUCKETS // 2
    n = jnp.maximum(dist, 1).astype(F32)
    large = max_exact + (jnp.log(n / max_exact) / math.log(MAX_DISTANCE / max_exact)
                         * (N_BUCKETS - max_exact)).astype(jnp.int32)
    large = jnp.minimum(large, N_BUCKETS - 1)
    return jnp.where(dist < max_exact, dist, large)


def attn_heads(z):
    B, T, _ = z.shape
    q = z[..., OFF_QA:OFF_KA].reshape(B, T, H_A, HD_A)
    k = z[..., OFF_KA:OFF_VA].reshape(B, T, H_A, HD_A)
    v = z[..., OFF_VA:D_IN].reshape(B, T, H_A, HD_A)
    return q, k, v


def dilated_attn_prompt(q, k, v, rel_bias, window, dil):
    B, S, H, E = q.shape
    w_sub = window // dil
    span = dil * QB
    s_pad = -(-S // span) * span
    L = s_pad // dil
    nb = L // QB

    def blocks(t):
        t = jnp.pad(t, ((0, 0), (0, s_pad - S), (0, 0), (0, 0))).reshape(B, L, dil, H, E)
        return t.transpose(0, 2, 3, 1, 4).reshape(B, dil, H, nb, QB, E)

    def with_prev(t):
        prev = jnp.pad(t, ((0, 0), (0, 0), (0, 0), (1, 0), (0, 0), (0, 0)))[:, :, :, :nb]
        return jnp.concatenate([prev, t], axis=-2)

    qb = blocks(q)
    kk = with_prev(blocks(k))
    vv = with_prev(blocks(v)).astype(F32)
    i = jnp.arange(QB)[:, None]
    j = jnp.arange(2 * QB)[None, :]
    dsub = i + QB - j
    band = (dsub >= 0) & (dsub <= w_sub)
    valid = band[None] & ((jnp.arange(nb)[:, None, None] > 0) | (j >= QB)[None])
    bias = rel_bias[t5_bucket(dil * jnp.clip(dsub, 0, w_sub))].transpose(2, 0, 1).astype(F32)
    logits = jnp.einsum('brhnqe,brhnke->brhnqk', qb, kk, preferred_element_type=F32) * (HD_A ** -0.5)
    logits = jnp.where(valid[None, None, None], logits + bias[None, None, :, None], -jnp.inf)
    m = jnp.max(logits, axis=-1)
    p = jnp.exp(logits - m[..., None])
    s = jnp.sum(p, axis=-1)
    o = jnp.einsum('brhnqk,brhnke->brhnqe', p, vv) / s[..., None]

    def unblock(t):
        rest = t.shape[5:]
        t = jnp.moveaxis(t.reshape(B, dil, H, L, *rest), 3, 1)
        return t.reshape(B, s_pad, H, *rest)[:, :S]

    return unblock(o), unblock(m), unblock(s)


def dilated_attn_sample(q, k_all, v_all, rel_bias, window, dil, n_past):
    T = q.shape[1]
    mm = jnp.arange(window // dil + 1)
    idx = n_past + jnp.arange(T)[:, None] - dil * mm[None, :]
    valid = idx >= 0
    idx_c = jnp.maximum(idx, 0)
    kg = k_all[:, idx_c]
    vg = v_all[:, idx_c].astype(F32)
    bias = rel_bias[t5_bucket(dil * mm)].T.astype(F32)
    logits = jnp.einsum('bthe,btkhe->bhtk', q, kg, preferred_element_type=F32) * (HD_A ** -0.5)
    logits = jnp.where(valid[None, None], logits + bias[None, :, None, :], -jnp.inf)
    m = jnp.max(logits, axis=-1)
    p = jnp.exp(logits - m[..., None])
    s = jnp.sum(p, axis=-1)
    o = jnp.einsum('bhtk,btkhe->bthe', p, vg) / s.transpose(0, 2, 1)[..., None]
    return o, m.transpose(0, 2, 1), s.transpose(0, 2, 1)


def combine_branches(outs):
    m_all = outs[0][1]
    for _, m, _ in outs[1:]:
        m_all = jnp.maximum(m_all, m)
    num = 0.0
    den = 0.0
    for o, m, s in outs:
        w = s * jnp.exp(m - m_all)
        num = num + w[..., None] * o
        den = den + w
    return num / den[..., None]


def mlstm_heads(z, qk, b_ig, b_fg):
    B, T, _ = z.shape
    qk = jax.nn.silu(qk)

    def heads(t):
        return t.reshape(B, T, H_M, -1).transpose(0, 2, 1, 3)

    q = heads(qk[..., :D_M])
    k = heads(qk[..., D_M:]) * (DK_M ** -0.5)
    v = heads(z[..., OFF_VM:OFF_OM])
    o_gate = jax.nn.sigmoid(z[..., OFF_OM:OFF_IM])
    log_i = (z[..., OFF_IM:OFF_FM] + b_ig).astype(F32).transpose(0, 2, 1)
    log_f = jax.nn.log_sigmoid((z[..., OFF_FM:OFF_QA] + b_fg).astype(F32)).transpose(0, 2, 1)
    return q, k, v, log_i, log_f, o_gate


def mlstm_chunk(state, xs):
    C0, n0, m0 = state
    q, k, v, log_i, log_f = xs
    q = q.astype(F32)
    k = k.astype(F32)
    v = v.astype(F32)
    L = q.shape[-2]
    b = jnp.cumsum(log_f, axis=-1)
    causal = jnp.tril(jnp.ones((L, L), dtype=bool))
    d_mat = jnp.where(causal, b[..., :, None] - b[..., None, :] + log_i[..., None, :], -jnp.inf)
    g_state = b + m0[..., None]
    m = jnp.maximum(jnp.max(d_mat, axis=-1), g_state)
    w_state = jnp.exp(g_state - m)
    s = jnp.einsum('bhte,bhse->bhts', q, k) * jnp.exp(d_mat - m[..., None])
    num = jnp.einsum('bhts,bhsv->bhtv', s, v) + w_state[..., None] * jnp.einsum('bhte,bhev->bhtv', q, C0)
    den = jnp.sum(s, axis=-1) + w_state * jnp.einsum('bhte,bhe->bht', q, n0)
    h = num / jnp.maximum(jnp.abs(den), jnp.exp(-m))[..., None]
    b_last = b[..., -1]
    g_tok = b_last[..., None] - b + log_i
    m_new = jnp.maximum(b_last + m0, jnp.max(g_tok, axis=-1))
    w_tok = jnp.exp(g_tok - m_new[..., None])
    decay = jnp.exp(b_last + m0 - m_new)
    C_new = decay[..., None, None] * C0 + jnp.einsum('bhs,bhse,bhsv->bhev', w_tok, k, v)
    n_new = decay[..., None] * n0 + jnp.einsum('bhs,bhse->bhe', w_tok, k)
    return (C_new, n_new, m_new), h


def mlstm_readout(h_tilde, o_gate, mh_norm):
    B, _, T, _ = h_tilde.shape
    hn = h_tilde * lax.rsqrt(jnp.mean(h_tilde * h_tilde, axis=-1, keepdims=True) + EPS)
    hn = hn * mh_norm.astype(F32).reshape(H_M, 1, DV_M)
    hn = hn.transpose(0, 2, 1, 3).reshape(B, T, D_M)
    return o_gate * hn.astype(o_gate.dtype)


def mixer_prompt(mix_w, h):
    w_in, b_ig, b_fg, conv_w, conv_b, mh_norm, rel_bias = mix_w
    B, S, _ = h.shape
    z = jnp.dot(h, w_in)
    qa, ka, va = attn_heads(z)
    att = combine_branches([dilated_attn_prompt(qa, ka, va, rel_bias, w, d) for (w, d) in BRANCHES])
    qk, conv_new = causal_conv(z[..., OFF_QM:OFF_VM], jnp.zeros((B, CONV_W - 1, 2 * D_M), z.dtype), conv_w, conv_b)
    q, k, v, log_i, log_f, o_gate = mlstm_heads(z, qk, b_ig, b_fg)
    n_chunks = S // MLSTM_CHUNK

    def to_chunks(t):
        return jnp.moveaxis(t.reshape(B, H_M, n_chunks, MLSTM_CHUNK, *t.shape[3:]), 2, 0)

    init = (jnp.zeros((B, H_M, DK_M, DV_M), F32), jnp.zeros((B, H_M, DK_M), F32), jnp.zeros((B, H_M), F32))
    (C, n, m), hs = lax.scan(mlstm_chunk, init, (to_chunks(q), to_chunks(k), to_chunks(v), to_chunks(log_i), to_chunks(log_f)))
    h_tilde = jnp.moveaxis(hs, 0, 2).reshape(B, H_M, S, DV_M)
    mem = mlstm_readout(h_tilde, o_gate, mh_norm)
    mix = jnp.concatenate([mem, att.reshape(B, S, D_A).astype(h.dtype)], axis=-1)
    keep = min(WINDOW_MAX, S)
    return mix, (ka[:, S - keep:], va[:, S - keep:], conv_new, C.astype(h.dtype), n.astype(h.dtype), m.astype(h.dtype))


def mixer_sample(st, mix_w, h):
    cache_k, cache_v, state_conv, state_C, state_n, state_m = st
    w_in, b_ig, b_fg, conv_w, conv_b, mh_norm, rel_bias = mix_w
    B, T, _ = h.shape
    z = jnp.dot(h, w_in)
    qa, ka, va = attn_heads(z)
    k_all = jnp.concatenate([cache_k.astype(ka.dtype), ka], axis=1)
    v_all = jnp.concatenate([cache_v.astype(va.dtype), va], axis=1)
    n_past = cache_k.shape[1]
    att = combine_branches([dilated_attn_sample(qa, k_all, v_all, rel_bias, w, d, n_past) for (w, d) in BRANCHES])
    qk, conv_new = causal_conv(z[..., OFF_QM:OFF_VM], state_conv, conv_w, conv_b)
    q, k, v, log_i, log_f, o_gate = mlstm_heads(z, qk, b_ig, b_fg)
    init = (state_C.astype(F32), state_n.astype(F32), state_m.astype(F32))
    (C, n, m), h_tilde = mlstm_chunk(init, (q, k, v, log_i, log_f))
    mem = mlstm_readout(h_tilde, o_gate, mh_norm)
    mix = jnp.concatenate([mem, att.reshape(B, T, D_A).astype(h.dtype)], axis=-1)
    return mix, (ka, va, conv_new.astype(state_conv.dtype), C.astype(state_C.dtype), n.astype(state_n.dtype), m.astype(state_m.dtype))


def moe_ffn(h, w_router, b_router, w1, b1, w2, b2):
    T, D = h.shape
    logits = jnp.dot(h, w_router, preferred_element_type=F32) + b_router.astype(F32)
    top_val, top_idx = lax.top_k(logits, TOP_K)
    gates = jax.nn.softmax(top_val, axis=-1)
    A = T * TOP_K
    flat_e = top_idx.reshape(A)
    flat_g = gates.reshape(A)
    flat_tok = jnp.arange(A) // TOP_K
    order = jnp.argsort(flat_e)
    sorted_e = flat_e[order]
    counts = jnp.bincount(flat_e, length=N_EXPERTS)
    padded = (counts + MOE_BLOCK - 1) // MOE_BLOCK * MOE_BLOCK
    pad_end = jnp.cumsum(padded)
    pad_start = pad_end - padded
    grp_start = jnp.cumsum(counts) - counts
    dest = pad_start[sorted_e] + jnp.arange(A) - grp_start[sorted_e]
    n_slots = -(-(A + N_EXPERTS * (MOE_BLOCK - 1)) // MOE_BLOCK) * MOE_BLOCK
    n_blocks = n_slots // MOE_BLOCK
    slot_tok = jnp.zeros((n_slots,), jnp.int32).at[dest].set(flat_tok[order])
    slot_w = jnp.zeros((n_slots,), F32).at[dest].set(flat_g[order])
    block_expert = jnp.minimum(jnp.searchsorted(pad_end, jnp.arange(n_blocks) * MOE_BLOCK, side='right'), N_EXPERTS - 1)
    xs = h[slot_tok].reshape(n_blocks, MOE_BLOCK, D)

    def expert_block(args):
        xb, e = args
        hgu = jnp.dot(xb, w1[e]) + b1[e]
        x_glu = jnp.minimum(hgu[:, :D_FF], SWIGLU_LIMIT)
        x_lin = jnp.clip(hgu[:, D_FF:], -SWIGLU_LIMIT, SWIGLU_LIMIT)
        act = x_glu * jax.nn.sigmoid(SWIGLU_ALPHA * x_glu) * (x_lin + 1)
        return jnp.dot(act, w2[e]) + b2[e]

    ys = lax.map(expert_block, (xs, block_expert)).reshape(n_slots, D)
    out = jnp.zeros((T, D), F32).at[slot_tok].add(ys.astype(F32) * slot_w[:, None])
    return out.astype(h.dtype)


def trunk_layer(x, c, mixer, w_ada, b_ada, g_pre_mix, g_post_mix, g_pre_ffn, g_post_ffn, w_out,
                w_router, b_router, w1, b1, w2, b2):
    shift1, scale1, gate1, shift2, scale2, gate2 = adaln(c, w_ada, b_ada)
    mix, new_state = mixer(rmsnorm(x, g_pre_mix) * (1 + scale1) + shift1)
    x = x + gate1 * rmsnorm(jnp.dot(mix, w_out), g_post_mix)
    h = rmsnorm(x, g_pre_ffn) * (1 + scale2) + shift2
    B, T, D = h.shape
    f = moe_ffn(h.reshape(B * T, D), w_router, b_router, w1, b1, w2, b2).reshape(B, T, D)
    x = x + gate2 * rmsnorm(f, g_post_ffn)
    return x, new_state


def setup_inputs(seed: int = 0) -> dict:
    key = jax.random.key(seed)
    ks = iter(jax.random.split(key, 40))

    def nrm(shape, scale):
        return scale * jax.random.normal(next(ks), shape, F32)

    wbuf = min(WINDOW_MAX, PAST_LEN)
    Ld = DEPTH
    return {
        'x_prompt': nrm((BATCH, SEQ, D_MODEL), 1.0),
        'x_sample': nrm((DEC_BATCH, DEC_SEQ, D_MODEL), 1.0),
        'cache_k': nrm((Ld, DEC_BATCH, wbuf, H_A, HD_A), 1.0),
        'cache_v': nrm((Ld, DEC_BATCH, wbuf, H_A, HD_A), 1.0),
        'state_conv': nrm((Ld, DEC_BATCH, CONV_W - 1, 2 * D_M), 1.0),
        'state_C': nrm((Ld, DEC_BATCH, H_M, DK_M, DV_M), 0.3),
        'state_n': nrm((Ld, DEC_BATCH, H_M, DK_M), 0.3),
        'state_m': nrm((Ld, DEC_BATCH, H_M), 1.0),
        'c_prompt': nrm((BATCH, D_MODEL), 1.0),
        'c_sample': nrm((DEC_BATCH, D_MODEL), 1.0),
        'w_ada': nrm((Ld, D_MODEL, 6 * D_MODEL), 0.5 * D_MODEL ** -0.5),
        'b_ada': nrm((Ld, 6 * D_MODEL), 0.02),
        'g_pre_mix': 1.0 + nrm((Ld, D_MODEL), 0.05),
        'g_post_mix': 1.0 + nrm((Ld, D_MODEL), 0.05),
        'g_pre_ffn': 1.0 + nrm((Ld, D_MODEL), 0.05),
        'g_post_ffn': 1.0 + nrm((Ld, D_MODEL), 0.05),
        'w_in': nrm((Ld, D_MODEL, D_IN), D_MODEL ** -0.5),
        'b_ig': nrm((Ld, H_M), 0.1),
        'b_fg': 3.0 + nrm((Ld, H_M), 0.1),
        'conv_w': nrm((Ld, CONV_W, 2 * D_M), CONV_W ** -0.5),
        'conv_b': nrm((Ld, 2 * D_M), 0.02),
        'mh_norm': 1.0 + nrm((Ld, D_M), 0.05),
        'rel_bias': nrm((N_BUCKETS, H_A), 0.1),
        'w_out': nrm((Ld, D_MIX, D_MODEL), D_MIX ** -0.5),
        'w_router': nrm((Ld, D_MODEL, N_EXPERTS), D_MODEL ** -0.5),
        'b_router': nrm((Ld, N_EXPERTS), 0.01),
        'w1': nrm((Ld, N_EXPERTS, D_MODEL, 2 * D_FF), D_MODEL ** -0.5),
        'b1': nrm((Ld, N_EXPERTS, 2 * D_FF), 0.01),
        'w2': nrm((Ld, N_EXPERTS, D_FF, D_MODEL), D_FF ** -0.5),
        'b2': nrm((Ld, N_EXPERTS, D_MODEL), 0.01),
    }


def reference(x_prompt, x_sample, cache_k, cache_v, state_conv, state_C, state_n, state_m,
              c_prompt, c_sample, w_ada, b_ada, g_pre_mix, g_post_mix, g_pre_ffn, g_post_ffn,
              w_in, b_ig, b_fg, conv_w, conv_b, mh_norm, rel_bias, w_out,
              w_router, b_router, w1, b1, w2, b2):
    y_prompt = x_prompt
    y_sample = x_sample
    states_p = []
    states_s = []
    for l in range(DEPTH):
        mix_w = (w_in[l], b_ig[l], b_fg[l], conv_w[l], conv_b[l], mh_norm[l], rel_bias)
        rest = (w_ada[l], b_ada[l], g_pre_mix[l], g_post_mix[l], g_pre_ffn[l], g_post_ffn[l], w_out[l],
                w_router[l], b_router[l], w1[l], b1[l], w2[l], b2[l])
        st_l = (cache_k[l], cache_v[l], state_conv[l], state_C[l], state_n[l], state_m[l])
        y_prompt, st_p = trunk_layer(y_prompt, c_prompt, functools.partial(mixer_prompt, mix_w), *rest)
        y_sample, st_s = trunk_layer(y_sample, c_sample, functools.partial(mixer_sample, st_l, mix_w), *rest)
        states_p.append(st_p)
        states_s.append(st_s)
    k_p, v_p, conv_p, C_p, n_p, m_p = [jnp.stack(t) for t in zip(*states_p)]
    k_s, v_s, conv_s, C_s, n_s, m_s = [jnp.stack(t) for t in zip(*states_s)]
    return (y_prompt, y_sample, k_p, v_p, conv_p, C_p, n_p, m_p, k_s, v_s, conv_s, C_s, n_s, m_s)
```

```python
import functools
import math

import jax
import jax.numpy as jnp
import numpy as np
from jax import lax
from jax.experimental import pallas as pl
from jax.experimental.pallas import tpu as pltpu

F32 = jnp.float32
BF16 = jnp.bfloat16
NEG_INF = float("-inf")

D_MODEL = 1024
H_M = 4
D_M = 512
DK_M = 128
CONV_W = 4
H_A = 8
D_A = 512
HD_A = 64
BRANCHES = ((128, 1), (512, 4), (2048, 16))
WINDOW_MAX = 2048
QB = 128
N_BUCKETS = 32
MAX_DISTANCE = 2048
N_EXPERTS = 32
TOP_K = 4
D_FF = 1024
SWIGLU_LIMIT = 7.0
SWIGLU_ALPHA = 1.702
EPS = 1e-6

LANES = 128
C_U, C_V, C_O, C_QA, C_KA, C_VA, C_G = 0, 1024, 1536, 2048, 2560, 3072, 3584
D_IN_R = C_G + LANES

MLSTM_L_PROMPT = 256
MLSTM_L_SAMPLE = 128
ATT_QL = 256
MOE_M = 512
GATHER_ROWS = 128
VMEM_LIMIT = 56 * 1024 * 1024


def _params(sem, vmem=None):
    return pltpu.CompilerParams(dimension_semantics=sem, vmem_limit_bytes=vmem or VMEM_LIMIT)


def _rms(x, g):
    return x * lax.rsqrt(jnp.mean(x * x, axis=-1, keepdims=True) + EPS) * g


def _log_sigmoid(t):
    return jnp.minimum(t, 0.0) - jnp.log1p(jnp.exp(-jnp.abs(t)))


def _ada_kernel(c_ref, w_ref, b_ref, o_ref):
    c = c_ref[...]
    a = (c * jax.nn.sigmoid(c)).astype(BF16)
    o_ref[...] = jnp.dot(a, w_ref[...].astype(BF16), preferred_element_type=F32) + b_ref[...]


def _adaln(c_all, w_ada, b_ada):
    rows = c_all.shape[0]
    n = w_ada.shape[1]
    tn = 1024
    return pl.pallas_call(
        _ada_kernel,
        grid=(n // tn,),
        in_specs=[
            pl.BlockSpec((rows, D_MODEL), lambda j: (0, 0)),
            pl.BlockSpec((D_MODEL, tn), lambda j: (0, j)),
            pl.BlockSpec((1, tn), lambda j: (0, j)),
        ],
        out_specs=pl.BlockSpec((rows, tn), lambda j: (0, j)),
        out_shape=jax.ShapeDtypeStruct((rows, n), F32),
        compiler_params=_params(("arbitrary",)),
        name="adaln",
    )(c_all, w_ada, b_ada.reshape(1, n))


def _inproj_kernel(x_ref, g_ref, sc_ref, sh_ref, w_ref, bg_ref,
                   u_ref, v_ref, o_ref, gt_ref, qa_ref, ka_ref, va_ref):
    x = x_ref[0]
    h = _rms(x, g_ref[...]) * (1.0 + sc_ref[0]) + sh_ref[0]
    hb = h.astype(BF16)

    def seg(lo, width):
        return jnp.dot(hb, w_ref[:, lo:lo + width], preferred_element_type=F32)

    u_ref[0] = seg(C_U, 2 * D_M)
    v_ref[0] = seg(C_V, D_M)
    o_ref[0] = jax.nn.sigmoid(seg(C_O, D_M))
    qa_ref[0] = seg(C_QA, D_A) * (HD_A ** -0.5)
    ka_ref[0] = seg(C_KA, D_A)
    va_ref[0] = seg(C_VA, D_A)
    t = seg(C_G, LANES) + bg_ref[...]
    lane = lax.broadcasted_iota(jnp.int32, t.shape, 1)
    gt_ref[0] = jnp.where(lane < H_M, t, _log_sigmoid(t))


def _inproj(x3, g_pre, scale, shift, w_r, bias_g, tm):
    nb, rows, _ = x3.shape
    per_row_mod = scale.shape[1] != 1
    mod_block = (1, tm, D_MODEL) if per_row_mod else (1, 1, D_MODEL)
    mod_map = (lambda b, i: (b, i, 0)) if per_row_mod else (lambda b, i: (b, 0, 0))
    row_map = lambda b, i: (b, i, 0)
    widths = (2 * D_M, D_M, D_M, LANES, D_A, D_A, D_A)
    return pl.pallas_call(
        _inproj_kernel,
        grid=(nb, rows // tm),
        in_specs=[
            pl.BlockSpec((1, tm, D_MODEL), row_map),
            pl.BlockSpec((1, D_MODEL), lambda b, i: (0, 0)),
            pl.BlockSpec(mod_block, mod_map),
            pl.BlockSpec(mod_block, mod_map),
            pl.BlockSpec((D_MODEL, D_IN_R), lambda b, i: (0, 0)),
            pl.BlockSpec((1, LANES), lambda b, i: (0, 0)),
        ],
        out_specs=[pl.BlockSpec((1, tm, w), row_map) for w in widths],
        out_shape=[jax.ShapeDtypeStruct((nb, rows, w), F32) for w in widths],
        compiler_params=_params(("arbitrary", "arbitrary")),
        name="inproj",
    )(x3, g_pre.reshape(1, D_MODEL), scale, shift, w_r, bias_g)


def _mlstm_kernel(u_ref, v_ref, og_ref, gc_ref, gr_ref, tail0_ref, c0_ref, n0_ref, m0_ref,
                  cw_ref, cb_ref, mh_ref,
                  mem_ref, c_out_ref, n_out_ref, m_out_ref,
                  c_s, n_s, m_s, tail_s, uext_s, *, chunk):
    L = chunk
    step = pl.program_id(1)

    @pl.when(step == 0)
    def _():
        c_s[...] = c0_ref[0]
        n_s[...] = n0_ref[0]
        m_s[...] = m0_ref[0]
        tail_s[...] = tail0_ref[0]

    u = u_ref[0]
    uext_s[0:8, :] = tail_s[...]
    uext_s[8:8 + L, :] = u
    acc = cb_ref[...] + uext_s[pl.ds(5, L), :] * cw_ref[0:1, :]
    acc = acc + uext_s[pl.ds(6, L), :] * cw_ref[1:2, :]
    acc = acc + uext_s[pl.ds(7, L), :] * cw_ref[2:3, :]
    acc = acc + u * cw_ref[3:4, :]
    tail_s[...] = uext_s[pl.ds(L, 8), :]
    qk = acc * jax.nn.sigmoid(acc)

    gc = gc_ref[0]
    gr = gr_ref[0]
    row = lax.broadcasted_iota(jnp.int32, (L, L), 0)
    col = lax.broadcasted_iota(jnp.int32, (L, L), 1)
    causal = col <= row
    tri = causal.astype(F32)
    tri_t = (row <= col).astype(F32)
    lane = lax.broadcasted_iota(jnp.int32, gc.shape, 1)
    srow = lax.broadcasted_iota(jnp.int32, gr.shape, 0)
    b_col_all = jnp.dot(tri, jnp.where(lane >= H_M, gc, 0.0), preferred_element_type=F32,
                        precision=lax.Precision.HIGHEST)
    b_row_all = jnp.dot(jnp.where(srow >= H_M, gr, 0.0), tri_t, preferred_element_type=F32,
                        precision=lax.Precision.HIGHEST)

    v_all = v_ref[0]
    og = og_ref[0]
    for h in range(H_M):
        q = qk[:, h * DK_M:(h + 1) * DK_M]
        k = qk[:, D_M + h * DK_M:D_M + (h + 1) * DK_M] * (DK_M ** -0.5)
        v = v_all[:, h * DK_M:(h + 1) * DK_M]
        li_row = gr[h:h + 1, :]
        li_col = gc[:, h:h + 1]
        b_row = b_row_all[H_M + h:H_M + h + 1, :]
        b_col = b_col_all[:, H_M + h:H_M + h + 1]
        m0 = m_s[h:h + 1, 0:1]
        c0 = c_s[h]
        n0 = n_s[h:h + 1, :]

        d = jnp.where(causal, b_col - b_row + li_row, NEG_INF)
        g_state = b_col + m0
        m = jnp.maximum(jnp.max(d, axis=-1, keepdims=True), g_state)
        w_state = jnp.exp(g_state - m)
        qb = q.astype(BF16)
        kb = k.astype(BF16)
        vb = v.astype(BF16)
        s = lax.dot_general(qb, kb, (((1,), (1,)), ((), ())), preferred_element_type=F32)
        s = s * jnp.exp(d - m)
        num = jnp.dot(s.astype(BF16), vb, preferred_element_type=F32)
        num = num + w_state * jnp.dot(qb, c0.astype(BF16), preferred_element_type=F32)
        den = jnp.sum(s, axis=-1, keepdims=True) + w_state * jnp.sum(q * n0, axis=-1, keepdims=True)
        hh = num / jnp.maximum(jnp.abs(den), jnp.exp(-m))

        b_last = b_col[L - 1:L, :]
        g_tok_row = b_last - b_row + li_row
        m_new = jnp.maximum(b_last + m0, jnp.max(g_tok_row, axis=-1, keepdims=True))
        w_tok_col = jnp.exp(b_last - b_col + li_col - m_new)
        decay = jnp.exp(b_last + m0 - m_new)
        kw = k * w_tok_col
        c_s[h] = decay * c0 + jnp.dot(kw.T.astype(BF16), vb, preferred_element_type=F32)
        n_s[h:h + 1, :] = decay * n0 + jnp.sum(kw, axis=0, keepdims=True)
        m_s[h:h + 1, :] = jnp.broadcast_to(m_new, (1, LANES))

        hn = hh * lax.rsqrt(jnp.mean(hh * hh, axis=-1, keepdims=True) + EPS)
        hn = hn * mh_ref[:, h * DK_M:(h + 1) * DK_M]
        mem_ref[0, :, h * DK_M:(h + 1) * DK_M] = og[:, h * DK_M:(h + 1) * DK_M] * hn

    @pl.when(step == pl.num_programs(1) - 1)
    def _():
        c_out_ref[0] = c_s[...]
        n_out_ref[0] = n_s[...]
        m_out_ref[0] = m_s[...]


def _mlstm(u, v, og, gates_col, gates_row, tail0, c0, n0, m0x, conv_w, conv_b, mh_norm, chunk):
    nb, rows, _ = u.shape
    nc = rows // chunk
    row_map = lambda b, c: (b, c, 0)
    bat3 = lambda b, c: (b, 0, 0)
    const2 = lambda b, c: (0, 0)
    return pl.pallas_call(
        functools.partial(_mlstm_kernel, chunk=chunk),
        grid=(nb, nc),
        in_specs=[
            pl.BlockSpec((1, chunk, 2 * D_M), row_map),
            pl.BlockSpec((1, chunk, D_M), row_map),
            pl.BlockSpec((1, chunk, D_M), row_map),
            pl.BlockSpec((1, chunk, LANES), row_map),
            pl.BlockSpec((1, 8, chunk), lambda b, c: (b, 0, c)),
            pl.BlockSpec((1, 8, 2 * D_M), bat3),
            pl.BlockSpec((1, H_M, DK_M, DK_M), lambda b, c: (b, 0, 0, 0)),
            pl.BlockSpec((1, H_M, DK_M), bat3),
            pl.BlockSpec((1, H_M, LANES), bat3),
            pl.BlockSpec((CONV_W, 2 * D_M), const2),
            pl.BlockSpec((1, 2 * D_M), const2),
            pl.BlockSpec((1, D_M), const2),
        ],
        out_specs=[
            pl.BlockSpec((1, chunk, D_M), row_map),
            pl.BlockSpec((1, H_M, DK_M, DK_M), lambda b, c: (b, 0, 0, 0)),
            pl.BlockSpec((1, H_M, DK_M), bat3),
            pl.BlockSpec((1, H_M, LANES), bat3),
        ],
        out_shape=[
            jax.ShapeDtypeStruct((nb, rows, D_M), F32),
            jax.ShapeDtypeStruct((nb, H_M, DK_M, DK_M), F32),
            jax.ShapeDtypeStruct((nb, H_M, DK_M), F32),
            jax.ShapeDtypeStruct((nb, H_M, LANES), F32),
        ],
        scratch_shapes=[
            pltpu.VMEM((H_M, DK_M, DK_M), F32),
            pltpu.VMEM((H_M, DK_M), F32),
            pltpu.VMEM((H_M, LANES), F32),
            pltpu.VMEM((8, 2 * D_M), F32),
            pltpu.VMEM((chunk + 8, 2 * D_M), F32),
        ],
        compiler_params=_params(("arbitrary", "arbitrary")),
        name="mlstm",
    )(u, v, og, gates_col, gates_row, tail0, c0, n0, m0x,
      conv_w, conv_b.reshape(1, 2 * D_M), mh_norm.reshape(1, D_M))


def _attn_kernel(q_ref, kc_ref, vc_ref, kp_ref, vp_ref, bias_ref, o_ref, l_ref, k_s, v_s, *, ql):
    first = pl.program_id(2) == 0
    k_s[0:QB, :] = kp_ref[0].astype(BF16)
    k_s[QB:QB + ql, :] = kc_ref[0].astype(BF16)
    v_s[0:QB, :] = vp_ref[0].astype(BF16)
    v_s[QB:QB + ql, :] = vc_ref[0].astype(BF16)
    col = lax.broadcasted_iota(jnp.int32, (QB, 2 * QB), 1)
    pad_keys = jnp.logical_and(first, col < QB)
    for h in range(H_A):
        lo = h * HD_A
        bias = bias_ref[h]
        for j in range(ql // QB):
            qb = q_ref[0, j * QB:(j + 1) * QB, lo:lo + HD_A].astype(BF16)
            kb = k_s[j * QB:(j + 2) * QB, lo:lo + HD_A]
            vb = v_s[j * QB:(j + 2) * QB, lo:lo + HD_A]
            logits = lax.dot_general(qb, kb, (((1,), (1,)), ((), ())), preferred_element_type=F32)
            logits = logits + bias
            if j == 0:
                logits = jnp.where(pad_keys, NEG_INF, logits)
            m = jnp.max(logits, axis=-1, keepdims=True)
            p = jnp.exp(logits - m)
            s = jnp.sum(p, axis=-1, keepdims=True)
            o = jnp.dot(p.astype(BF16), vb, preferred_element_type=F32) / s
            o_ref[0, j * QB:(j + 1) * QB, lo:lo + HD_A] = o
            l_ref[0, j * QB:(j + 1) * QB, lo:lo + HD_A] = jnp.broadcast_to(m + jnp.log(s), (QB, HD_A))


def _attn_branch(qa, ka, va, bias, dil):
    nb, seq, _ = qa.shape
    ld = seq // dil
    ql = min(ATT_QL, ld)
    view = lambda t: t.reshape(nb, ld, dil * D_A)
    cur = lambda b, r, i: (b, i, r)
    prev = lambda b, r, i: (b, jnp.maximum(i * (ql // QB) - 1, 0), r)
    o, lse = pl.pallas_call(
        functools.partial(_attn_kernel, ql=ql),
        grid=(nb, dil, ld // ql),
        in_specs=[
            pl.BlockSpec((1, ql, D_A), cur),
            pl.BlockSpec((1, ql, D_A), cur),
            pl.BlockSpec((1, ql, D_A), cur),
            pl.BlockSpec((1, QB, D_A), prev),
            pl.BlockSpec((1, QB, D_A), prev),
            pl.BlockSpec((H_A, QB, 2 * QB), lambda b, r, i: (0, 0, 0)),
        ],
        out_specs=[pl.BlockSpec((1, ql, D_A), cur), pl.BlockSpec((1, ql, D_A), cur)],
        out_shape=[jax.ShapeDtypeStruct((nb, ld, dil * D_A), F32)] * 2,
        scratch_shapes=[pltpu.VMEM((QB + ql, D_A), BF16), pltpu.VMEM((QB + ql, D_A), BF16)],
        compiler_params=_params(("arbitrary", "arbitrary", "arbitrary")),
        name=f"attn_d{dil}",
    )(view(qa), view(ka), view(va), view(ka), view(va), bias)
    return o.reshape(nb, seq, D_A), lse.reshape(nb, seq, D_A)


def _t5_bucket(dist):
    max_exact = N_BUCKETS // 2
    n = jnp.maximum(dist, 1).astype(F32)
    large = max_exact + (jnp.log(n / max_exact) / math.log(MAX_DISTANCE / max_exact)
                         * (N_BUCKETS - max_exact)).astype(jnp.int32)
    large = jnp.minimum(large, N_BUCKETS - 1)
    return jnp.where(dist < max_exact, dist, large)


def _prompt_bias(rel_bias, window, dil):
    w_sub = window // dil
    i = jnp.arange(QB)[:, None]
    j = jnp.arange(2 * QB)[None, :]
    dsub = i + QB - j
    band = (dsub >= 0) & (dsub <= w_sub)
    bias = rel_bias[_t5_bucket(dil * jnp.clip(dsub, 0, w_sub))].transpose(2, 0, 1).astype(F32)
    return jnp.where(band[None], bias, NEG_INF)


def _sattn_kernel(q_ref, kn_ref, vn_ref, k1_ref, v1_ref, k4_ref, v4_ref, k16_ref, v16_ref,
                  bc_ref, bn_ref, seg_ref, att_ref, *, n_tok):
    seg = seg_ref[...]
    kn = kn_ref[0]
    vn = vn_ref[0]
    caches = ((k1_ref, v1_ref), (k4_ref, v4_ref), (k16_ref, v16_ref))
    att_ref[0] = jnp.zeros(att_ref.shape[1:], F32)
    for t in range(n_tok):
        q = q_ref[0, t:t + 1, :]
        outs = []
        for bi, (_, dil) in enumerate(BRANCHES):
            kc_ref, vc_ref = caches[bi]
            c0 = (t % dil) * D_A if dil > 1 else 0
            kc = kc_ref[0, :, c0:c0 + D_A]
            vc = vc_ref[0, :, c0:c0 + D_A]
            lc = jnp.dot((kc * q).astype(BF16), seg, preferred_element_type=F32) + bc_ref[bi, t]
            ln = jnp.dot((kn * q).astype(BF16), seg, preferred_element_type=F32) + bn_ref[bi, t]
            m = jnp.maximum(jnp.max(lc, axis=0, keepdims=True), jnp.max(ln, axis=0, keepdims=True))
            pc = jnp.exp(lc - m)
            pn = jnp.exp(ln - m)
            s = jnp.sum(pc, axis=0, keepdims=True) + jnp.sum(pn, axis=0, keepdims=True)
            o = jnp.sum(pc * vc, axis=0, keepdims=True) + jnp.sum(pn * vn, axis=0, keepdims=True)
            outs.append((o / s, m + jnp.log(s)))
        lmax = functools.reduce(jnp.maximum, [l for _, l in outs])
        num = 0.0
        den = 0.0
        for o, l in outs:
            w = jnp.exp(l - lmax)
            num = num + w * o
            den = den + w
        att_ref[0, t:t + 1, :] = num / den


def _sample_attention(qa8, kn8, vn8, cache_k, cache_v, bias_c, bias_n, n_tok):
    nb, n_past = cache_k.shape[0], cache_k.shape[1]
    seg = jnp.asarray(np.kron(np.eye(H_A), np.ones((HD_A, HD_A))), BF16)

    def views(c):
        out = []
        for _, dil in BRANCHES:
            out.append(c.reshape(nb, n_past // dil, dil * D_A))
        return out

    k1, k4, k16 = views(cache_k)
    v1, v4, v16 = views(cache_v)
    tok_map = lambda b: (b, 0, 0)
    specs = []
    for _, dil in BRANCHES:
        width = D_A if dil == 1 else n_tok * D_A
        last = n_past // dil // QB - 1
        spec = pl.BlockSpec((1, QB, width), functools.partial(lambda b, last: (b, last, 0), last=last))
        specs += [spec, spec]
    return pl.pallas_call(
        functools.partial(_sattn_kernel, n_tok=n_tok),
        grid=(nb,),
        in_specs=[pl.BlockSpec((1, 8, D_A), tok_map)] * 3 + specs + [
            pl.BlockSpec(bias_c.shape, lambda b: (0, 0, 0, 0)),
            pl.BlockSpec(bias_n.shape, lambda b: (0, 0, 0, 0)),
            pl.BlockSpec((D_A, D_A), lambda b: (0, 0)),
        ],
        out_specs=pl.BlockSpec((1, 8, D_A), tok_map),
        out_shape=jax.ShapeDtypeStruct((nb, 8, D_A), F32),
        compiler_params=_params(("arbitrary",)),
        name="sample_attn",
    )(qa8, kn8, vn8, k1, v1, k4, v4, k16, v16, bias_c, bias_n, seg)


def _sample_bias(rel_bias, n_tok):
    bc, bn = [], []
    i = jnp.arange(QB)
    j = jnp.arange(8)
    for window, dil in BRANCHES:
        nk = window // dil
        bc_t, bn_t = [], []
        for t in range(n_tok):
            if dil == 1:
                m_c = nk + t - i
                ok_c = i >= t
                m_n = t - j
                ok_n = j <= t
            else:
                m_c = nk - i
                ok_c = i >= 0
                m_n = t - j
                ok_n = j == t
            tab = lambda m: rel_bias[_t5_bucket(dil * jnp.clip(m, 0, nk))].astype(F32)
            bc_t.append(jnp.where(ok_c[:, None], tab(m_c), NEG_INF))
            bn_t.append(jnp.where(ok_n[:, None], tab(m_n), NEG_INF))
        bc.append(jnp.stack(bc_t))
        bn.append(jnp.stack(bn_t))
    expand = lambda x: jnp.repeat(jnp.stack(x), HD_A, axis=-1)
    return expand(bc), expand(bn)


def _postmix_kernel(mem_ref, o1_ref, l1_ref, o2_ref, l2_ref, o3_ref, l3_ref, x_ref,
                    g1_ref, sc2_ref, sh2_ref, gpost_ref, gpre_ref, wo_ref, wr_ref, br_ref,
                    x1_ref, h2_ref, idx_ref, gate_ref):
    l1, l2, l3 = l1_ref[0], l2_ref[0], l3_ref[0]
    lmax = jnp.maximum(jnp.maximum(l1, l2), l3)
    w1 = jnp.exp(l1 - lmax)
    w2 = jnp.exp(l2 - lmax)
    w3 = jnp.exp(l3 - lmax)
    att = (w1 * o1_ref[0] + w2 * o2_ref[0] + w3 * o3_ref[0]) / (w1 + w2 + w3)
    y = jnp.dot(mem_ref[0].astype(BF16), wo_ref[0:D_M, :], preferred_element_type=F32)
    y = y + jnp.dot(att.astype(BF16), wo_ref[D_M:, :], preferred_element_type=F32)
    x1 = x_ref[0] + g1_ref[0] * _rms(y, gpost_ref[...])
    x1_ref[0] = x1
    h2 = _rms(x1, gpre_ref[...]) * (1.0 + sc2_ref[0]) + sh2_ref[0]
    h2_ref[0] = h2
    logits = jnp.dot(h2, wr_ref[...], preferred_element_type=F32,
                     precision=lax.Precision.HIGHEST) + br_ref[...]
    lane = lax.broadcasted_iota(jnp.int32, logits.shape, 1)
    lane_f = lane.astype(F32)
    logits = jnp.where(lane < N_EXPERTS, logits, NEG_INF)
    vals, idxs = [], []
    for _ in range(TOP_K):
        mx = jnp.max(logits, axis=-1, keepdims=True)
        ix = jnp.min(jnp.where(logits == mx, lane_f, float(LANES)), axis=-1, keepdims=True)
        vals.append(mx)
        idxs.append(ix)
        logits = jnp.where(lane_f == ix, NEG_INF, logits)
    es = [jnp.exp(v - vals[0]) for v in vals]
    tot = es[0] + es[1] + es[2] + es[3]
    idx_tile = jnp.zeros(logits.shape, F32)
    gate_tile = jnp.zeros(logits.shape, F32)
    for k in range(TOP_K):
        idx_tile = jnp.where(lane == k, idxs[k], idx_tile)
        gate_tile = jnp.where(lane == k, es[k] / tot, gate_tile)
    idx_ref[0] = idx_tile.astype(jnp.int32)
    gate_ref[0] = gate_tile


def _postmix(mem, branches, x3, gate1, scale2, shift2, g_post, g_pre, w_out_b, w_router_p, b_router_p, tm):
    nb, rows, _ = x3.shape
    per_row_mod = gate1.shape[1] != 1
    mod_block = (1, tm, D_MODEL) if per_row_mod else (1, 1, D_MODEL)
    mod_map = (lambda b, i: (b, i, 0)) if per_row_mod else (lambda b, i: (b, 0, 0))
    row_map = lambda b, i: (b, i, 0)
    const2 = lambda b, i: (0, 0)
    half = pl.BlockSpec((1, tm, D_M), row_map)
    flat = [a for pair in branches for a in pair]
    return pl.pallas_call(
        _postmix_kernel,
        grid=(nb, rows // tm),
        in_specs=[half] * 7 + [
            pl.BlockSpec((1, tm, D_MODEL), row_map),
            pl.BlockSpec(mod_block, mod_map),
            pl.BlockSpec(mod_block, mod_map),
            pl.BlockSpec(mod_block, mod_map),
            pl.BlockSpec((1, D_MODEL), const2),
            pl.BlockSpec((1, D_MODEL), const2),
            pl.BlockSpec((D_MODEL, D_MODEL), const2),
            pl.BlockSpec((D_MODEL, LANES), const2),
            pl.BlockSpec((1, LANES), const2),
        ],
        out_specs=[
            pl.BlockSpec((1, tm, D_MODEL), row_map),
            pl.BlockSpec((1, tm, D_MODEL), row_map),
            pl.BlockSpec((1, tm, LANES), row_map),
            pl.BlockSpec((1, tm, LANES), row_map),
        ],
        out_shape=[
            jax.ShapeDtypeStruct((nb, rows, D_MODEL), F32),
            jax.ShapeDtypeStruct((nb, rows, D_MODEL), F32),
            jax.ShapeDtypeStruct((nb, rows, LANES), jnp.int32),
            jax.ShapeDtypeStruct((nb, rows, LANES), F32),
        ],
        compiler_params=_params(("arbitrary", "arbitrary")),
        name="postmix",
    )(mem, *flat, x3, gate1, scale2, shift2, g_post.reshape(1, D_MODEL), g_pre.reshape(1, D_MODEL),
      w_out_b, w_router_p, b_router_p)


def _gather_kernel(nused_ref, tok_ref, h_hbm, xs_ref, buf, sem, *, rows_per_block):
    b = pl.program_id(0)
    active = b * GATHER_ROWS < nused_ref[0] * rows_per_block

    def row_copy(j):
        return pltpu.make_async_copy(h_hbm.at[pl.ds(tok_ref[0, 0, j], 1), :], buf.at[pl.ds(j, 1), :], sem)

    @pl.when(active)
    def _():
        def start(j, carry):
            row_copy(j).start()
            return carry

        lax.fori_loop(0, GATHER_ROWS, start, 0)

        def wait(j, carry):
            row_copy(j).wait()
            return carry

        lax.fori_loop(0, GATHER_ROWS, wait, 0)
        xs_ref[...] = buf[...].astype(BF16)

    @pl.when(jnp.logical_not(active))
    def _():
        xs_ref[...] = jnp.zeros(xs_ref.shape, BF16)


def _moe_gather(h_all, slot_tok, n_used):
    n_slots = slot_tok.shape[0]
    nblk = n_slots // GATHER_ROWS
    return pl.pallas_call(
        functools.partial(_gather_kernel, rows_per_block=MOE_M),
        grid_spec=pltpu.PrefetchScalarGridSpec(
            num_scalar_prefetch=1,
            grid=(nblk,),
            in_specs=[
                pl.BlockSpec((1, 1, GATHER_ROWS), lambda b, n: (b, 0, 0), memory_space=pltpu.SMEM),
                pl.BlockSpec(memory_space=pl.ANY),
            ],
            out_specs=pl.BlockSpec((GATHER_ROWS, D_MODEL), lambda b, n: (b, 0)),
            scratch_shapes=[pltpu.VMEM((GATHER_ROWS, D_MODEL), F32), pltpu.SemaphoreType.DMA],
        ),
        out_shape=jax.ShapeDtypeStruct((n_slots, D_MODEL), BF16),
        compiler_params=_params(("arbitrary",)),
        name="moe_gather",
    )(n_used, slot_tok.reshape(nblk, 1, GATHER_ROWS), h_all)


def _expert_kernel(be_ref, nused_ref, xs_ref, w1_ref, b1_ref, w2_ref, b2_ref, ys_ref, w1b, w2b):
    b = pl.program_id(0)
    prev = be_ref[jnp.maximum(b - 1, 0)]
    changed = jnp.logical_or(b == 0, be_ref[b] != prev)
    active = b < nused_ref[0]

    @pl.when(jnp.logical_and(changed, active))
    def _():
        w1b[...] = w1_ref[0].astype(BF16)
        w2b[...] = w2_ref[0].astype(BF16)

    @pl.when(active)
    def _():
        sub = MOE_M // 2
        for r in range(2):
            x = xs_ref[r * sub:(r + 1) * sub, :]
            hgu = jnp.dot(x, w1b[...], preferred_element_type=F32) + b1_ref[0]
            x_glu = jnp.minimum(hgu[:, :D_FF], SWIGLU_LIMIT)
            x_lin = jnp.clip(hgu[:, D_FF:], -SWIGLU_LIMIT, SWIGLU_LIMIT)
            act = x_glu * jax.nn.sigmoid(SWIGLU_ALPHA * x_glu) * (x_lin + 1.0)
            y = jnp.dot(act.astype(BF16), w2b[...], preferred_element_type=F32) + b2_ref[0]
            ys_ref[r * sub:(r + 1) * sub, :] = y

    @pl.when(jnp.logical_not(active))
    def _():
        ys_ref[...] = jnp.zeros(ys_ref.shape, F32)


def _moe_experts(xs, block_expert, n_used, w1, b1, w2, b2):
    n_slots = xs.shape[0]
    nblk = n_slots // MOE_M
    return pl.pallas_call(
        _expert_kernel,
        grid_spec=pltpu.PrefetchScalarGridSpec(
            num_scalar_prefetch=2,
            grid=(nblk,),
            in_specs=[
                pl.BlockSpec((MOE_M, D_MODEL), lambda b, be, n: (b, 0)),
                pl.BlockSpec((1, D_MODEL, 2 * D_FF), lambda b, be, n: (be[b], 0, 0)),
                pl.BlockSpec((1, 1, 2 * D_FF), lambda b, be, n: (be[b], 0, 0)),
                pl.BlockSpec((1, D_FF, D_MODEL), lambda b, be, n: (be[b], 0, 0)),
                pl.BlockSpec((1, 1, D_MODEL), lambda b, be, n: (be[b], 0, 0)),
            ],
            out_specs=pl.BlockSpec((MOE_M, D_MODEL), lambda b, be, n: (b, 0)),
            scratch_shapes=[pltpu.VMEM((D_MODEL, 2 * D_FF), BF16), pltpu.VMEM((D_FF, D_MODEL), BF16)],
        ),
        out_shape=jax.ShapeDtypeStruct((n_slots, D_MODEL), F32),
        compiler_params=_params(("arbitrary",)),
        name="moe_experts",
    )(block_expert, n_used, xs, w1, b1.reshape(N_EXPERTS, 1, 2 * D_FF), w2, b2.reshape(N_EXPERTS, 1, D_MODEL))


def _combine_kernel(slot_ref, gate_ref, ys_hbm, f_ref, buf, sem):
    n = GATHER_ROWS * TOP_K

    def row_copy(j):
        return pltpu.make_async_copy(ys_hbm.at[pl.ds(slot_ref[0, 0, j], 1), :], buf.at[pl.ds(j, 1), :], sem)

    def start(j, carry):
        row_copy(j).start()
        return carry

    lax.fori_loop(0, n, start, 0)

    def wait(j, carry):
        row_copy(j).wait()
        return carry

    lax.fori_loop(0, n, wait, 0)
    g = gate_ref[...]
    acc = jnp.zeros((GATHER_ROWS, D_MODEL), F32)
    for k in range(TOP_K):
        acc = acc + buf[k * GATHER_ROWS:(k + 1) * GATHER_ROWS, :] * g[:, k:k + 1]
    f_ref[...] = acc


def _moe_combine(ys, slot_of, gates):
    tiles = slot_of.shape[0]
    return pl.pallas_call(
        _combine_kernel,
        grid=(tiles,),
        in_specs=[
            pl.BlockSpec((1, 1, TOP_K * GATHER_ROWS), lambda i: (i, 0, 0), memory_space=pltpu.SMEM),
            pl.BlockSpec((GATHER_ROWS, LANES), lambda i: (i, 0)),
            pl.BlockSpec(memory_space=pl.ANY),
        ],
        out_specs=pl.BlockSpec((GATHER_ROWS, D_MODEL), lambda i: (i, 0)),
        out_shape=jax.ShapeDtypeStruct((tiles * GATHER_ROWS, D_MODEL), F32),
        scratch_shapes=[pltpu.VMEM((TOP_K * GATHER_ROWS, D_MODEL), F32), pltpu.SemaphoreType.DMA],
        compiler_params=_params(("arbitrary",)),
        name="moe_combine",
    )(slot_of, gates, ys)


def _final_kernel(x1_ref, f_ref, g2_ref, gpost_ref, y_ref):
    y_ref[0] = x1_ref[0] + g2_ref[0] * _rms(f_ref[0], gpost_ref[...])


def _final(x1, f, gate2, g_post, tm):
    nb, rows, _ = x1.shape
    per_row_mod = gate2.shape[1] != 1
    mod_block = (1, tm, D_MODEL) if per_row_mod else (1, 1, D_MODEL)
    mod_map = (lambda b, i: (b, i, 0)) if per_row_mod else (lambda b, i: (b, 0, 0))
    row_map = lambda b, i: (b, i, 0)
    return pl.pallas_call(
        _final_kernel,
        grid=(nb, rows // tm),
        in_specs=[
            pl.BlockSpec((1, tm, D_MODEL), row_map),
            pl.BlockSpec((1, tm, D_MODEL), row_map),
            pl.BlockSpec(mod_block, mod_map),
            pl.BlockSpec((1, D_MODEL), lambda b, i: (0, 0)),
        ],
        out_specs=pl.BlockSpec((1, tm, D_MODEL), row_map),
        out_shape=jax.ShapeDtypeStruct((nb, rows, D_MODEL), F32),
        compiler_params=_params(("arbitrary", "arbitrary")),
        name="final",
    )(x1, f, gate2, g_post.reshape(1, D_MODEL))


def _moe(h_all, top_idx, gates, w1, b1, w2, b2):
    t_all = h_all.shape[0]
    a = t_all * TOP_K
    flat_e = top_idx[:, :TOP_K].reshape(a)
    onehot = (flat_e[:, None] == jnp.arange(N_EXPERTS)[None, :]).astype(jnp.int32)
    csum = jnp.cumsum(onehot, axis=0)
    counts = csum[-1]
    rank = jnp.take_along_axis(csum, flat_e[:, None], axis=1)[:, 0] - 1
    padded = (counts + MOE_M - 1) // MOE_M * MOE_M
    pad_end = jnp.cumsum(padded)
    pad_start = pad_end - padded
    slot_of = (pad_start[flat_e] + rank).astype(jnp.int32)
    n_blocks = -(-(a + N_EXPERTS * (MOE_M - 1)) // MOE_M)
    n_slots = n_blocks * MOE_M
    flat_tok = (jnp.arange(a) // TOP_K).astype(jnp.int32)
    slot_tok = jnp.zeros((n_slots,), jnp.int32).at[slot_of].set(flat_tok)
    n_used = (pad_end[-1] // MOE_M).astype(jnp.int32).reshape(1)
    blk_start = jnp.arange(n_blocks) * MOE_M
    block_expert = jnp.minimum(jnp.searchsorted(pad_end, blk_start, side="right"), N_EXPERTS - 1)
    last_used = block_expert[jnp.maximum(n_used[0] - 1, 0)]
    block_expert = jnp.where(jnp.arange(n_blocks) < n_used[0], block_expert, last_used).astype(jnp.int32)

    xs = _moe_gather(h_all, slot_tok, n_used)
    ys = _moe_experts(xs, block_expert, n_used, w1, b1, w2, b2)
    tiles = t_all // GATHER_ROWS
    slot_tiles = slot_of.reshape(tiles, GATHER_ROWS, TOP_K).transpose(0, 2, 1).reshape(tiles, 1, TOP_K * GATHER_ROWS)
    return _moe_combine(ys, slot_tiles, gates)


def _reorder_w_in(w_in):
    off_om_end = 4 * D_M
    off_g_end = off_om_end + 2 * H_M
    w = jnp.concatenate([
        w_in[:, :off_om_end],
        w_in[:, off_g_end:],
        w_in[:, off_om_end:off_g_end],
        jnp.zeros((D_MODEL, LANES - 2 * H_M), w_in.dtype),
    ], axis=1)
    return w.astype(BF16)


def kernel(x_prompt, x_sample, cache_k, cache_v, state_conv, state_C, state_n, state_m, c_prompt, c_sample, w_ada, b_ada, g_pre_mix, g_post_mix, g_pre_ffn, g_post_ffn, w_in, b_ig, b_fg, conv_w, conv_b, mh_norm, rel_bias, w_out, w_router, b_router, w1, b1, w2, b2):
    depth = w_in.shape[0]
    assert depth == 1
    l = 0
    nb_p, seq, _ = x_prompt.shape
    nb_s, n_tok, _ = x_sample.shape
    rows_s = nb_s * n_tok

    n_c = nb_p + nb_s
    c_pad = -(-n_c // 8) * 8
    c_all = jnp.concatenate([c_prompt, c_sample, jnp.zeros((c_pad - n_c, D_MODEL), F32)], axis=0)
    mod = _adaln(c_all, w_ada[l], b_ada[l])
    mods = [mod[:, i * D_MODEL:(i + 1) * D_MODEL] for i in range(6)]
    mods_p = [m[:nb_p].reshape(nb_p, 1, D_MODEL) for m in mods]
    mods_s = [jnp.repeat(m[nb_p:n_c], n_tok, axis=0).reshape(1, rows_s, D_MODEL) for m in mods]

    w_r = _reorder_w_in(w_in[l])
    bias_g = jnp.concatenate([b_ig[l], b_fg[l], jnp.zeros((LANES - 2 * H_M,), F32)]).reshape(1, LANES)
    w_out_b = w_out[l].astype(BF16)
    w_router_p = jnp.concatenate([w_router[l], jnp.zeros((D_MODEL, LANES - N_EXPERTS), F32)], axis=1)
    b_router_p = jnp.concatenate([b_router[l].astype(F32), jnp.zeros((LANES - N_EXPERTS,), F32)]).reshape(1, LANES)

    shift1, scale1, gate1, shift2, scale2, gate2 = mods_p
    u, v_m, og, gts, qa, ka, va = _inproj(x_prompt, g_pre_mix[l], scale1, shift1, w_r, bias_g, tm=512)
    gates_row = gts[:, :, :8].transpose(0, 2, 1)
    zeros = lambda *s: jnp.zeros(s, F32)
    mem, c_p, n_p, m_p = _mlstm(u, v_m, og, gts, gates_row, zeros(nb_p, 8, 2 * D_M),
                                zeros(nb_p, H_M, DK_M, DK_M), zeros(nb_p, H_M, DK_M), zeros(nb_p, H_M, LANES),
                                conv_w[l], conv_b[l], mh_norm[l], MLSTM_L_PROMPT)
    branches = [_attn_branch(qa, ka, va, _prompt_bias(rel_bias, w, d), d) for (w, d) in BRANCHES]
    x1_p, h2_p, idx_p, gate_p = _postmix(mem, branches, x_prompt, gate1, scale2, shift2,
                                         g_post_mix[l], g_pre_ffn[l], w_out_b, w_router_p, b_router_p, tm=256)
    keep = min(WINDOW_MAX, seq)
    k_p = ka[:, seq - keep:].reshape(nb_p, keep, H_A, HD_A)
    v_p = va[:, seq - keep:].reshape(nb_p, keep, H_A, HD_A)
    conv_p = u[:, seq - (CONV_W - 1):]
    gate2_p = gate2

    shift1, scale1, gate1, shift2, scale2, gate2 = mods_s
    x_s = x_sample.reshape(1, rows_s, D_MODEL)
    u_s, v_s, og_s, gts_s, qa_s, ka_s, va_s = _inproj(x_s, g_pre_mix[l], scale1, shift1, w_r, bias_g, tm=rows_s)
    per_b = lambda t: t.reshape(nb_s, n_tok, t.shape[-1])
    ls = MLSTM_L_SAMPLE
    pad_rows = lambda t, n: jnp.concatenate([t, jnp.zeros((nb_s, n - t.shape[1], t.shape[2]), t.dtype)], axis=1)
    inert = jnp.concatenate([jnp.full((H_M,), NEG_INF, F32), jnp.zeros((LANES - H_M,), F32)])
    gts_pad = jnp.concatenate([per_b(gts_s), jnp.broadcast_to(inert, (nb_s, ls - n_tok, LANES))], axis=1)
    tail0 = jnp.concatenate([zeros(nb_s, 8 - (CONV_W - 1), 2 * D_M), state_conv[l].astype(F32)], axis=1)
    m0x = jnp.broadcast_to(state_m[l].astype(F32)[:, :, None], (nb_s, H_M, LANES))
    mem_s, c_s, n_s, m_s = _mlstm(pad_rows(per_b(u_s), ls), pad_rows(per_b(v_s), ls), pad_rows(per_b(og_s), ls),
                                  gts_pad, gts_pad[:, :, :8].transpose(0, 2, 1), tail0,
                                  state_C[l].astype(F32), state_n[l].astype(F32), m0x,
                                  conv_w[l], conv_b[l], mh_norm[l], ls)
    mem_s = mem_s[:, :n_tok].reshape(1, rows_s, D_M)
    bias_c, bias_n = _sample_bias(rel_bias, n_tok)
    n_past = cache_k.shape[2]
    att_s = _sample_attention(pad_rows(per_b(qa_s), 8), pad_rows(per_b(ka_s), 8), pad_rows(per_b(va_s), 8),
                              cache_k[l].reshape(nb_s, n_past, D_A), cache_v[l].reshape(nb_s, n_past, D_A),
                              bias_c, bias_n, n_tok)
    att_s = att_s[:, :n_tok].reshape(1, rows_s, D_A)
    zero_o = zeros(1, rows_s, D_A)
    ninf = jnp.full((1, rows_s, D_A), NEG_INF, F32)
    branches_s = [(att_s, zero_o), (zero_o, ninf), (zero_o, ninf)]
    x1_s, h2_s, idx_s, gate_s = _postmix(mem_s, branches_s, x_s, gate1, scale2, shift2,
                                         g_post_mix[l], g_pre_ffn[l], w_out_b, w_router_p, b_router_p, tm=rows_s)

    t_p = nb_p * seq
    h_all = jnp.concatenate([h2_p.reshape(t_p, D_MODEL), h2_s.reshape(rows_s, D_MODEL)], axis=0)
    idx_all = jnp.concatenate([idx_p.reshape(t_p, LANES), idx_s.reshape(rows_s, LANES)], axis=0)
    gate_all = jnp.concatenate([gate_p.reshape(t_p, LANES), gate_s.reshape(rows_s, LANES)], axis=0)
    f_all = _moe(h_all, idx_all, gate_all, w1[l], b1[l], w2[l], b2[l])
    y_prompt = _final(x1_p, f_all[:t_p].reshape(nb_p, seq, D_MODEL), gate2_p, g_post_ffn[l], tm=512)
    y_sample = _final(x1_s, f_all[t_p:].reshape(1, rows_s, D_MODEL), gate2, g_post_ffn[l], tm=rows_s)
    y_sample = y_sample.reshape(nb_s, n_tok, D_MODEL)

    st = lambda t: t[None]
    k_s = per_b(ka_s).reshape(nb_s, n_tok, H_A, HD_A)
    v_s4 = per_b(va_s).reshape(nb_s, n_tok, H_A, HD_A)
    conv_s = jnp.concatenate([state_conv[l].astype(F32), per_b(u_s)], axis=1)[:, -(CONV_W - 1):]
    return (y_prompt, y_sample, st(k_p), st(v_p), st(conv_p), st(c_p), st(n_p), st(m_p[:, :, 0]),
            st(k_s), st(v_s4), st(conv_s), st(c_s), st(n_s), st(m_s[:, :, 0]))
```

```python
import functools
import math

import jax
import jax.numpy as jnp
import numpy as np
from jax import lax
from jax.experimental import pallas as pl
from jax.experimental.pallas import tpu as pltpu

F32 = jnp.float32
BF16 = jnp.bfloat16
U32 = jnp.uint32
NEG_INF = float("-inf")

D_MODEL = 1024
H_M = 4
D_M = 512
DK_M = 128
CONV_W = 4
H_A = 8
D_A = 512
HD_A = 64
BRANCHES = ((128, 1), (512, 4), (2048, 16))
WINDOW_MAX = 2048
QB = 128
N_BUCKETS = 32
MAX_DISTANCE = 2048
N_EXPERTS = 32
TOP_K = 4
D_FF = 1024
SWIGLU_LIMIT = 7.0
SWIGLU_ALPHA = 1.702
EPS = 1e-6

LANES = 128
C_U, C_V, C_O, C_QA, C_KA, C_VA, C_G = 0, 1024, 1536, 2048, 2560, 3072, 3584
D_IN_R = C_G + LANES

MLSTM_L_PROMPT = 256
MLSTM_L_SAMPLE = 128
ATT_QL = 256
MOE_M = 512
MOE_SUB = 256
TOK_TILE = 128
PACK_W = D_MODEL // 2
HI_MASK = 0xFFFF0000
VMEM_LIMIT = 56 * 1024 * 1024


def _params(sem, vmem=None):
    return pltpu.CompilerParams(dimension_semantics=sem, vmem_limit_bytes=vmem or VMEM_LIMIT)


def _rms(x, g):
    return x * lax.rsqrt(jnp.mean(x * x, axis=-1, keepdims=True) + EPS) * g


def _log_sigmoid(t):
    return jnp.minimum(t, 0.0) - jnp.log1p(jnp.exp(-jnp.abs(t)))


def _ada_kernel(c_ref, w_ref, b_ref, o_ref):
    c = c_ref[...]
    a = (c * jax.nn.sigmoid(c)).astype(BF16)
    o_ref[...] = jnp.dot(a, w_ref[...].astype(BF16), preferred_element_type=F32) + b_ref[...]


def _adaln(c_all, w_ada, b_ada):
    rows = c_all.shape[0]
    n = w_ada.shape[1]
    tn = 1024
    return pl.pallas_call(
        _ada_kernel,
        grid=(n // tn,),
        in_specs=[
            pl.BlockSpec((rows, D_MODEL), lambda j: (0, 0)),
            pl.BlockSpec((D_MODEL, tn), lambda j: (0, j)),
            pl.BlockSpec((1, tn), lambda j: (0, j)),
        ],
        out_specs=pl.BlockSpec((rows, tn), lambda j: (0, j)),
        out_shape=jax.ShapeDtypeStruct((rows, n), F32),
        compiler_params=_params(("arbitrary",)),
        name="adaln",
    )(c_all, w_ada, b_ada.reshape(1, n))


def _inproj_kernel(x_ref, g_ref, sc_ref, sh_ref, w_ref, bg_ref,
                   u_ref, v_ref, o_ref, gt_ref, qa_ref, ka_ref, va_ref):
    x = x_ref[0]
    h = _rms(x, g_ref[...]) * (1.0 + sc_ref[0]) + sh_ref[0]
    hb = h.astype(BF16)

    def seg(lo, width):
        return jnp.dot(hb, w_ref[:, lo:lo + width], preferred_element_type=F32)

    u_ref[0] = seg(C_U, 2 * D_M)
    v_ref[0] = seg(C_V, D_M)
    o_ref[0] = jax.nn.sigmoid(seg(C_O, D_M))
    qa_ref[0] = seg(C_QA, D_A) * (HD_A ** -0.5)
    ka_ref[0] = seg(C_KA, D_A)
    va_ref[0] = seg(C_VA, D_A)
    t = seg(C_G, LANES) + bg_ref[...]
    lane = lax.broadcasted_iota(jnp.int32, t.shape, 1)
    gt_ref[0] = jnp.where(lane < H_M, t, _log_sigmoid(t))


def _inproj(x3, g_pre, scale, shift, w_r, bias_g, tm):
    nb, rows, _ = x3.shape
    per_row_mod = scale.shape[1] != 1
    mod_block = (1, tm, D_MODEL) if per_row_mod else (1, 1, D_MODEL)
    mod_map = (lambda b, i: (b, i, 0)) if per_row_mod else (lambda b, i: (b, 0, 0))
    row_map = lambda b, i: (b, i, 0)
    widths = (2 * D_M, D_M, D_M, LANES, D_A, D_A, D_A)
    return pl.pallas_call(
        _inproj_kernel,
        grid=(nb, rows // tm),
        in_specs=[
            pl.BlockSpec((1, tm, D_MODEL), row_map),
            pl.BlockSpec((1, D_MODEL), lambda b, i: (0, 0)),
            pl.BlockSpec(mod_block, mod_map),
            pl.BlockSpec(mod_block, mod_map),
            pl.BlockSpec((D_MODEL, D_IN_R), lambda b, i: (0, 0)),
            pl.BlockSpec((1, LANES), lambda b, i: (0, 0)),
        ],
        out_specs=[pl.BlockSpec((1, tm, w), row_map) for w in widths],
        out_shape=[jax.ShapeDtypeStruct((nb, rows, w), F32) for w in widths],
        compiler_params=_params(("arbitrary", "arbitrary")),
        name="inproj",
    )(x3, g_pre.reshape(1, D_MODEL), scale, shift, w_r, bias_g)


def _mlstm_kernel(u_ref, v_ref, og_ref, gc_ref, gr_ref, tail0_ref, c0_ref, n0_ref, m0_ref,
                  cw_ref, cb_ref, mh_ref,
                  mem_ref, c_out_ref, n_out_ref, m_out_ref,
                  c_s, n_s, m_s, tail_s, uext_s, *, chunk):
    L = chunk
    step = pl.program_id(1)

    @pl.when(step == 0)
    def _():
        c_s[...] = c0_ref[0]
        n_s[...] = n0_ref[0]
        m_s[...] = m0_ref[0]
        tail_s[...] = tail0_ref[0]

    u = u_ref[0]
    uext_s[0:8, :] = tail_s[...]
    uext_s[8:8 + L, :] = u
    acc = cb_ref[...] + uext_s[pl.ds(5, L), :] * cw_ref[0:1, :]
    acc = acc + uext_s[pl.ds(6, L), :] * cw_ref[1:2, :]
    acc = acc + uext_s[pl.ds(7, L), :] * cw_ref[2:3, :]
    acc = acc + u * cw_ref[3:4, :]
    tail_s[...] = uext_s[pl.ds(L, 8), :]
    qk = acc * jax.nn.sigmoid(acc)

    gc = gc_ref[0]
    gr = gr_ref[0]
    row = lax.broadcasted_iota(jnp.int32, (L, L), 0)
    col = lax.broadcasted_iota(jnp.int32, (L, L), 1)
    causal = col <= row
    tri = causal.astype(F32)
    tri_t = (row <= col).astype(F32)
    lane = lax.broadcasted_iota(jnp.int32, gc.shape, 1)
    srow = lax.broadcasted_iota(jnp.int32, gr.shape, 0)
    b_col_all = jnp.dot(tri, jnp.where(lane >= H_M, gc, 0.0), preferred_element_type=F32,
                        precision=lax.Precision.HIGHEST)
    b_row_all = jnp.dot(jnp.where(srow >= H_M, gr, 0.0), tri_t, preferred_element_type=F32,
                        precision=lax.Precision.HIGHEST)

    v_all = v_ref[0]
    og = og_ref[0]
    for h in range(H_M):
        q = qk[:, h * DK_M:(h + 1) * DK_M]
        k = qk[:, D_M + h * DK_M:D_M + (h + 1) * DK_M] * (DK_M ** -0.5)
        v = v_all[:, h * DK_M:(h + 1) * DK_M]
        li_row = gr[h:h + 1, :]
        li_col = gc[:, h:h + 1]
        b_row = b_row_all[H_M + h:H_M + h + 1, :]
        b_col = b_col_all[:, H_M + h:H_M + h + 1]
        m0 = m_s[h:h + 1, 0:1]
        c0 = c_s[h]
        n0 = n_s[h:h + 1, :]

        d = jnp.where(causal, b_col - b_row + li_row, NEG_INF)
        g_state = b_col + m0
        m = jnp.maximum(jnp.max(d, axis=-1, keepdims=True), g_state)
        w_state = jnp.exp(g_state - m)
        qb = q.astype(BF16)
        kb = k.astype(BF16)
        vb = v.astype(BF16)
        s = lax.dot_general(qb, kb, (((1,), (1,)), ((), ())), preferred_element_type=F32)
        s = s * jnp.exp(d - m)
        num = jnp.dot(s.astype(BF16), vb, preferred_element_type=F32)
        num = num + w_state * jnp.dot(qb, c0.astype(BF16), preferred_element_type=F32)
        den = jnp.sum(s, axis=-1, keepdims=True) + w_state * jnp.sum(q * n0, axis=-1, keepdims=True)
        hh = num / jnp.maximum(jnp.abs(den), jnp.exp(-m))

        b_last = b_col[L - 1:L, :]
        g_tok_row = b_last - b_row + li_row
        m_new = jnp.maximum(b_last + m0, jnp.max(g_tok_row, axis=-1, keepdims=True))
        w_tok_col = jnp.exp(b_last - b_col + li_col - m_new)
        decay = jnp.exp(b_last + m0 - m_new)
        kw = k * w_tok_col
        c_s[h] = decay * c0 + jnp.dot(kw.T.astype(BF16), vb, preferred_element_type=F32)
        n_s[h:h + 1, :] = decay * n0 + jnp.sum(kw, axis=0, keepdims=True)
        m_s[h:h + 1, :] = jnp.broadcast_to(m_new, (1, LANES))

        hn = hh * lax.rsqrt(jnp.mean(hh * hh, axis=-1, keepdims=True) + EPS)
        hn = hn * mh_ref[:, h * DK_M:(h + 1) * DK_M]
        mem_ref[0, :, h * DK_M:(h + 1) * DK_M] = og[:, h * DK_M:(h + 1) * DK_M] * hn

    @pl.when(step == pl.num_programs(1) - 1)
    def _():
        c_out_ref[0] = c_s[...]
        n_out_ref[0] = n_s[...]
        m_out_ref[0] = m_s[...]


def _mlstm(u, v, og, gates_col, gates_row, tail0, c0, n0, m0x, conv_w, conv_b, mh_norm, chunk):
    nb, rows, _ = u.shape
    nc = rows // chunk
    row_map = lambda b, c: (b, c, 0)
    bat3 = lambda b, c: (b, 0, 0)
    const2 = lambda b, c: (0, 0)
    return pl.pallas_call(
        functools.partial(_mlstm_kernel, chunk=chunk),
        grid=(nb, nc),
        in_specs=[
            pl.BlockSpec((1, chunk, 2 * D_M), row_map),
            pl.BlockSpec((1, chunk, D_M), row_map),
            pl.BlockSpec((1, chunk, D_M), row_map),
            pl.BlockSpec((1, chunk, LANES), row_map),
            pl.BlockSpec((1, 8, chunk), lambda b, c: (b, 0, c)),
            pl.BlockSpec((1, 8, 2 * D_M), bat3),
            pl.BlockSpec((1, H_M, DK_M, DK_M), lambda b, c: (b, 0, 0, 0)),
            pl.BlockSpec((1, H_M, DK_M), bat3),
            pl.BlockSpec((1, H_M, LANES), bat3),
            pl.BlockSpec((CONV_W, 2 * D_M), const2),
            pl.BlockSpec((1, 2 * D_M), const2),
            pl.BlockSpec((1, D_M), const2),
        ],
        out_specs=[
            pl.BlockSpec((1, chunk, D_M), row_map),
            pl.BlockSpec((1, H_M, DK_M, DK_M), lambda b, c: (b, 0, 0, 0)),
            pl.BlockSpec((1, H_M, DK_M), bat3),
            pl.BlockSpec((1, H_M, LANES), bat3),
        ],
        out_shape=[
            jax.ShapeDtypeStruct((nb, rows, D_M), F32),
            jax.ShapeDtypeStruct((nb, H_M, DK_M, DK_M), F32),
            jax.ShapeDtypeStruct((nb, H_M, DK_M), F32),
            jax.ShapeDtypeStruct((nb, H_M, LANES), F32),
        ],
        scratch_shapes=[
            pltpu.VMEM((H_M, DK_M, DK_M), F32),
            pltpu.VMEM((H_M, DK_M), F32),
            pltpu.VMEM((H_M, LANES), F32),
            pltpu.VMEM((8, 2 * D_M), F32),
            pltpu.VMEM((chunk + 8, 2 * D_M), F32),
        ],
        compiler_params=_params(("arbitrary", "arbitrary")),
        name="mlstm",
    )(u, v, og, gates_col, gates_row, tail0, c0, n0, m0x,
      conv_w, conv_b.reshape(1, 2 * D_M), mh_norm.reshape(1, D_M))


def _attn_kernel(q_ref, kc_ref, vc_ref, kp_ref, vp_ref, bias_ref, o_ref, l_ref, k_s, v_s, *, ql):
    first = pl.program_id(2) == 0
    k_s[0:QB, :] = kp_ref[0].astype(BF16)
    k_s[QB:QB + ql, :] = kc_ref[0].astype(BF16)
    v_s[0:QB, :] = vp_ref[0].astype(BF16)
    v_s[QB:QB + ql, :] = vc_ref[0].astype(BF16)
    col = lax.broadcasted_iota(jnp.int32, (QB, 2 * QB), 1)
    pad_keys = jnp.logical_and(first, col < QB)
    for h in range(H_A):
        lo = h * HD_A
        bias = bias_ref[h]
        for j in range(ql // QB):
            qb = q_ref[0, j * QB:(j + 1) * QB, lo:lo + HD_A].astype(BF16)
            kb = k_s[j * QB:(j + 2) * QB, lo:lo + HD_A]
            vb = v_s[j * QB:(j + 2) * QB, lo:lo + HD_A]
            logits = lax.dot_general(qb, kb, (((1,), (1,)), ((), ())), preferred_element_type=F32)
            logits = logits + bias
            if j == 0:
                logits = jnp.where(pad_keys, NEG_INF, logits)
            m = jnp.max(logits, axis=-1, keepdims=True)
            p = jnp.exp(logits - m)
            s = jnp.sum(p, axis=-1, keepdims=True)
            o = jnp.dot(p.astype(BF16), vb, preferred_element_type=F32) / s
            o_ref[0, j * QB:(j + 1) * QB, lo:lo + HD_A] = o
            l_ref[0, j * QB:(j + 1) * QB, lo:lo + HD_A] = jnp.broadcast_to(m + jnp.log(s), (QB, HD_A))


def _attn_branch(qa, ka, va, bias, dil):
    nb, seq, _ = qa.shape
    ld = seq // dil
    ql = min(ATT_QL, ld)
    view = lambda t: t.reshape(nb, ld, dil * D_A)
    cur = lambda b, r, i: (b, i, r)
    prev = lambda b, r, i: (b, jnp.maximum(i * (ql // QB) - 1, 0), r)
    o, lse = pl.pallas_call(
        functools.partial(_attn_kernel, ql=ql),
        grid=(nb, dil, ld // ql),
        in_specs=[
            pl.BlockSpec((1, ql, D_A), cur),
            pl.BlockSpec((1, ql, D_A), cur),
            pl.BlockSpec((1, ql, D_A), cur),
            pl.BlockSpec((1, QB, D_A), prev),
            pl.BlockSpec((1, QB, D_A), prev),
            pl.BlockSpec((H_A, QB, 2 * QB), lambda b, r, i: (0, 0, 0)),
        ],
        out_specs=[pl.BlockSpec((1, ql, D_A), cur), pl.BlockSpec((1, ql, D_A), cur)],
        out_shape=[jax.ShapeDtypeStruct((nb, ld, dil * D_A), F32)] * 2,
        scratch_shapes=[pltpu.VMEM((QB + ql, D_A), BF16), pltpu.VMEM((QB + ql, D_A), BF16)],
        compiler_params=_params(("arbitrary", "arbitrary", "arbitrary")),
        name=f"attn_d{dil}",
    )(view(qa), view(ka), view(va), view(ka), view(va), bias)
    return o.reshape(nb, seq, D_A), lse.reshape(nb, seq, D_A)


def _t5_bucket(dist):
    max_exact = N_BUCKETS // 2
    n = jnp.maximum(dist, 1).astype(F32)
    large = max_exact + (jnp.log(n / max_exact) / math.log(MAX_DISTANCE / max_exact)
                         * (N_BUCKETS - max_exact)).astype(jnp.int32)
    large = jnp.minimum(large, N_BUCKETS - 1)
    return jnp.where(dist < max_exact, dist, large)


def _branch_bias_vec(rel_bias, window, dil):
    return rel_bias[_t5_bucket(dil * jnp.arange(window // dil + 1))].astype(F32)


def _prompt_bias(rel_bias, window, dil):
    assert window // dil == QB
    n = 2 * QB
    vec = _branch_bias_vec(rel_bias, window, dil)
    vecp = jnp.concatenate([vec[::-1], jnp.full((n - QB - 1, H_A), NEG_INF, F32)], axis=0)
    g = jnp.roll(vecp, QB - 1, axis=0)
    skew = jnp.tile(g, (QB + 1, 1))[:QB * (n + 1)].reshape(QB, n + 1, H_A)[:, :n]
    return skew[::-1].transpose(2, 0, 1)


def _sattn_kernel(q_ref, kn_ref, vn_ref, k1_ref, v1_ref, k4_ref, v4_ref, k16_ref, v16_ref,
                  b1_ref, b4_ref, b16_ref, bn_ref, att_ref, *, n_tok):
    kn = kn_ref[0]
    vn = vn_ref[0]
    for t in range(n_tok):
        q = q_ref[0, t]
        parts = []
        for bi, (_, dil) in enumerate(BRANCHES):
            if dil == 1:
                kc, vc, bc = k1_ref[0], v1_ref[0], b1_ref[t]
                k_new, v_new, b_new = kn, vn, bn_ref[t]
            else:
                k_ref, v_ref, b_ref = ((k4_ref, v4_ref, b4_ref) if dil == 4 else (k16_ref, v16_ref, b16_ref))
                kc, vc, bc = k_ref[0, :, t], v_ref[0, :, t], b_ref[...]
                k_new, v_new, b_new = kn[t:t + 1], vn[t:t + 1], bn_ref[0, 0:1]
            lc = jnp.sum(kc * q[None], axis=-1, keepdims=True) + bc
            ln = jnp.sum(k_new * q[None], axis=-1, keepdims=True) + b_new
            m = jnp.maximum(jnp.max(lc, axis=0, keepdims=True), jnp.max(ln, axis=0, keepdims=True))
            pc = jnp.exp(lc - m)
            pn = jnp.exp(ln - m)
            s = jnp.sum(pc, axis=0, keepdims=True) + jnp.sum(pn, axis=0, keepdims=True)
            o = jnp.sum(pc * vc, axis=0, keepdims=True) + jnp.sum(pn * v_new, axis=0, keepdims=True)
            parts.append((o / s, m + jnp.log(s)))
        lmax = functools.reduce(jnp.maximum, [l for _, l in parts])
        num = 0.0
        den = 0.0
        for o, l in parts:
            w = jnp.exp(l - lmax)
            num = num + w * o
            den = den + w
        att_ref[0, t] = (num / den)[0]


def _sample_attention(q4, kn4, vn4, cache_k, cache_v, biases, n_tok):
    nb, n_past = cache_k.shape[0], cache_k.shape[1]
    assert all(n_past % (dil * QB) == 0 and n_tok <= max(dil, n_tok) for _, dil in BRANCHES)
    tok_spec = pl.BlockSpec((1, n_tok, H_A, HD_A), lambda b: (b, 0, 0, 0))
    args, specs = [], []
    for _, dil in BRANCHES:
        last = n_past // dil // QB - 1
        for c in (cache_k, cache_v):
            if dil == 1:
                args.append(c)
                specs.append(pl.BlockSpec((1, QB, H_A, HD_A), functools.partial(lambda b, last: (b, last, 0, 0), last=last)))
            else:
                assert n_tok <= dil
                args.append(c.reshape(nb, n_past // dil, dil, H_A, HD_A))
                specs.append(pl.BlockSpec((1, QB, n_tok, H_A, HD_A),
                                          functools.partial(lambda b, last: (b, last, 0, 0, 0), last=last)))
    b1, b4, b16, bn = biases
    full = lambda a: pl.BlockSpec(a.shape, functools.partial(lambda b, nd: (0,) * nd, nd=a.ndim))
    return pl.pallas_call(
        functools.partial(_sattn_kernel, n_tok=n_tok),
        grid=(nb,),
        in_specs=[tok_spec] * 3 + specs + [full(b1), full(b4), full(b16), full(bn)],
        out_specs=tok_spec,
        out_shape=jax.ShapeDtypeStruct((nb, n_tok, H_A, HD_A), F32),
        compiler_params=_params(("arbitrary",)),
        name="sample_attn",
    )(q4, kn4, vn4, *args, b1, b4, b16, bn)


def _sample_bias(rel_bias, n_tok):
    expand = lambda x: jnp.broadcast_to(x[..., None], x.shape + (HD_A,))
    i = jnp.arange(QB)
    out = []
    for window, dil in BRANCHES:
        nk = window // dil
        assert nk == QB
        vec = _branch_bias_vec(rel_bias, window, dil)
        if dil == 1:
            tabs = []
            for t in range(n_tok):
                m = nk + t - i
                tabs.append(jnp.where((m <= nk)[:, None], vec[jnp.clip(m, 0, nk)], NEG_INF))
            out.append(expand(jnp.stack(tabs)))
        else:
            out.append(expand(vec[nk - i]))
    vec1 = _branch_bias_vec(rel_bias, *BRANCHES[0])
    j = jnp.arange(n_tok)
    new = []
    for t in range(n_tok):
        m = t - j
        new.append(jnp.where((m >= 0)[:, None], vec1[jnp.clip(m, 0, None)], NEG_INF))
    out.append(expand(jnp.stack(new)))
    return out


def _postmix_kernel(mem_ref, o1_ref, l1_ref, o2_ref, l2_ref, o3_ref, l3_ref, x_ref,
                    g1_ref, sc2_ref, sh2_ref, gpost_ref, gpre_ref, wo_ref, wr_ref, br_ref,
                    x1_ref, h2_ref, idx_ref, gate_ref):
    l1, l2, l3 = l1_ref[0], l2_ref[0], l3_ref[0]
    lmax = jnp.maximum(jnp.maximum(l1, l2), l3)
    w1 = jnp.exp(l1 - lmax)
    w2 = jnp.exp(l2 - lmax)
    w3 = jnp.exp(l3 - lmax)
    att = (w1 * o1_ref[0] + w2 * o2_ref[0] + w3 * o3_ref[0]) / (w1 + w2 + w3)
    y = jnp.dot(mem_ref[0].astype(BF16), wo_ref[0:D_M, :], preferred_element_type=F32)
    y = y + jnp.dot(att.astype(BF16), wo_ref[D_M:, :], preferred_element_type=F32)
    x1 = x_ref[0] + g1_ref[0] * _rms(y, gpost_ref[...])
    x1_ref[0] = x1
    h2 = _rms(x1, gpre_ref[...]) * (1.0 + sc2_ref[0]) + sh2_ref[0]
    h2_ref[0] = h2
    logits = jnp.dot(h2, wr_ref[...], preferred_element_type=F32,
                     precision=lax.Precision.HIGHEST) + br_ref[...]
    lane = lax.broadcasted_iota(jnp.int32, logits.shape, 1)
    lane_f = lane.astype(F32)
    logits = jnp.where(lane < N_EXPERTS, logits, NEG_INF)
    vals, idxs = [], []
    for _ in range(TOP_K):
        mx = jnp.max(logits, axis=-1, keepdims=True)
        ix = jnp.min(jnp.where(logits == mx, lane_f, float(LANES)), axis=-1, keepdims=True)
        vals.append(mx)
        idxs.append(ix)
        logits = jnp.where(lane_f == ix, NEG_INF, logits)
    es = [jnp.exp(v - vals[0]) for v in vals]
    tot = es[0] + es[1] + es[2] + es[3]
    idx_tile = jnp.zeros(logits.shape, F32)
    gate_tile = jnp.zeros(logits.shape, F32)
    for k in range(TOP_K):
        idx_tile = jnp.where(lane == k, idxs[k], idx_tile)
        gate_tile = jnp.where(lane == k, es[k] / tot, gate_tile)
    idx_ref[0] = idx_tile.astype(jnp.int32)
    gate_ref[0] = gate_tile


def _postmix(mem, branches, x3, gate1, scale2, shift2, g_post, g_pre, w_out_b, w_router_p, b_router_p, tm):
    nb, rows, _ = x3.shape
    per_row_mod = gate1.shape[1] != 1
    mod_block = (1, tm, D_MODEL) if per_row_mod else (1, 1, D_MODEL)
    mod_map = (lambda b, i: (b, i, 0)) if per_row_mod else (lambda b, i: (b, 0, 0))
    row_map = lambda b, i: (b, i, 0)
    const2 = lambda b, i: (0, 0)
    half = pl.BlockSpec((1, tm, D_M), row_map)
    flat = [a for pair in branches for a in pair]
    return pl.pallas_call(
        _postmix_kernel,
        grid=(nb, rows // tm),
        in_specs=[half] * 7 + [
            pl.BlockSpec((1, tm, D_MODEL), row_map),
            pl.BlockSpec(mod_block, mod_map),
            pl.BlockSpec(mod_block, mod_map),
            pl.BlockSpec(mod_block, mod_map),
            pl.BlockSpec((1, D_MODEL), const2),
            pl.BlockSpec((1, D_MODEL), const2),
            pl.BlockSpec((D_MODEL, D_MODEL), const2),
            pl.BlockSpec((D_MODEL, LANES), const2),
            pl.BlockSpec((1, LANES), const2),
        ],
        out_specs=[
            pl.BlockSpec((1, tm, D_MODEL), row_map),
            pl.BlockSpec((1, tm, D_MODEL), row_map),
            pl.BlockSpec((1, tm, LANES), row_map),
            pl.BlockSpec((1, tm, LANES), row_map),
        ],
        out_shape=[
            jax.ShapeDtypeStruct((nb, rows, D_MODEL), F32),
            jax.ShapeDtypeStruct((nb, rows, D_MODEL), F32),
            jax.ShapeDtypeStruct((nb, rows, LANES), jnp.int32),
            jax.ShapeDtypeStruct((nb, rows, LANES), F32),
        ],
        compiler_params=_params(("arbitrary", "arbitrary")),
        name="postmix",
    )(mem, *flat, x3, gate1, scale2, shift2, g_post.reshape(1, D_MODEL), g_pre.reshape(1, D_MODEL),
      w_out_b, w_router_p, b_router_p)


def _row_wait(src_rows, dst_rows, sem):
    pltpu.make_async_copy(src_rows, dst_rows, sem).wait()


def _scatter_kernel(slot_ref, h_ref, xs_hbm, buf, sem):
    bits = pltpu.bitcast(h_ref[...].astype(BF16).astype(F32), U32)
    buf[...] = (bits[:, :PACK_W] & U32(HI_MASK)) | (bits[:, PACK_W:] >> 16)

    def issue(r, carry):
        src = buf.at[pl.ds(r, 1), :]
        for k in range(TOP_K):
            pltpu.make_async_copy(src, xs_hbm.at[pl.ds(slot_ref[0, 0, r * TOP_K + k], 1), :], sem).start()
        return carry

    lax.fori_loop(0, TOK_TILE, issue, 0, unroll=4)
    for _ in range(TOP_K):
        _row_wait(buf, xs_hbm.at[pl.ds(0, TOK_TILE), :], sem)


def _moe_scatter(h_all, slot_tiles):
    tiles = slot_tiles.shape[0]
    n_rows = tiles * TOK_TILE * TOP_K
    return pl.pallas_call(
        _scatter_kernel,
        grid=(tiles,),
        in_specs=[
            pl.BlockSpec((1, 1, TOK_TILE * TOP_K), lambda i: (i, 0, 0), memory_space=pltpu.SMEM),
            pl.BlockSpec((TOK_TILE, D_MODEL), lambda i: (i, 0)),
        ],
        out_specs=pl.BlockSpec(memory_space=pl.ANY),
        out_shape=jax.ShapeDtypeStruct((n_rows, PACK_W), U32),
        scratch_shapes=[pltpu.VMEM((TOK_TILE, PACK_W), U32), pltpu.SemaphoreType.DMA],
        compiler_params=_params(("arbitrary",)),
        name="moe_scatter",
    )(slot_tiles, h_all)


def _expert_kernel(tile_ref, exp_ref, lo_ref, hi_ref, first_ref, nvis_ref,
                   xs_ref, w1_ref, b1_ref, w2_ref, b2_ref, ys_ref, w1b, w2b):
    v = pl.program_id(0)
    active = v < nvis_ref[0]
    changed = jnp.logical_or(v == 0, exp_ref[v] != exp_ref[jnp.maximum(v - 1, 0)])
    lo = lo_ref[v]
    hi = hi_ref[v]

    @pl.when(jnp.logical_and(active, changed))
    def _():
        w1b[...] = w1_ref[0].astype(BF16)
        w2b[...] = w2_ref[0].astype(BF16)

    @pl.when(jnp.logical_and(active, first_ref[v] == 1))
    def _():
        ys_ref[...] = jnp.zeros(ys_ref.shape, F32)

    for r in range(MOE_M // MOE_SUB):
        r0 = r * MOE_SUB

        @pl.when(jnp.logical_and(active, jnp.logical_and(lo < r0 + MOE_SUB, hi > r0)))
        def _():
            w = xs_ref[r0:r0 + MOE_SUB, :]
            x_hi = pltpu.bitcast(w & U32(HI_MASK), F32).astype(BF16)
            x_lo = pltpu.bitcast(w << 16, F32).astype(BF16)
            hgu = jnp.dot(x_hi, w1b[0:PACK_W, :], preferred_element_type=F32)
            hgu = hgu + jnp.dot(x_lo, w1b[PACK_W:, :], preferred_element_type=F32) + b1_ref[0]
            x_glu = jnp.minimum(hgu[:, :D_FF], SWIGLU_LIMIT)
            x_lin = jnp.clip(hgu[:, D_FF:], -SWIGLU_LIMIT, SWIGLU_LIMIT)
            act = x_glu * jax.nn.sigmoid(SWIGLU_ALPHA * x_glu) * (x_lin + 1.0)
            y = jnp.dot(act.astype(BF16), w2b[...], preferred_element_type=F32) + b2_ref[0]
            rows = r0 + lax.broadcasted_iota(jnp.int32, (MOE_SUB, 1), 0)
            mine = jnp.logical_and(rows >= lo, rows < hi)
            ys_ref[r0:r0 + MOE_SUB, :] = jnp.where(mine, y, ys_ref[r0:r0 + MOE_SUB, :])


def _moe_experts(xs, visits, w1, b1, w2, b2):
    n_rows = xs.shape[0]
    n_vis = visits[0].shape[0]
    tile_map = lambda v, tile, *_: (tile[v], 0)
    exp_map = lambda v, tile, exp, *_: (exp[v], 0, 0)
    return pl.pallas_call(
        _expert_kernel,
        grid_spec=pltpu.PrefetchScalarGridSpec(
            num_scalar_prefetch=6,
            grid=(n_vis,),
            in_specs=[
                pl.BlockSpec((MOE_M, PACK_W), tile_map),
                pl.BlockSpec((1, D_MODEL, 2 * D_FF), exp_map),
                pl.BlockSpec((1, 1, 2 * D_FF), exp_map),
                pl.BlockSpec((1, D_FF, D_MODEL), exp_map),
                pl.BlockSpec((1, 1, D_MODEL), exp_map),
            ],
            out_specs=pl.BlockSpec((MOE_M, D_MODEL), tile_map),
            scratch_shapes=[pltpu.VMEM((D_MODEL, 2 * D_FF), BF16), pltpu.VMEM((D_FF, D_MODEL), BF16)],
        ),
        out_shape=jax.ShapeDtypeStruct((n_rows, D_MODEL), F32),
        compiler_params=_params(("arbitrary",)),
        name="moe_experts",
    )(*visits, xs, w1, b1.reshape(N_EXPERTS, 1, 2 * D_FF), w2, b2.reshape(N_EXPERTS, 1, D_MODEL))


def _combine_kernel(slot_ref, gate_ref, x1p_ref, x1s_ref, g2p_ref, g2s_ref, gpost_ref, ys_hbm,
                    yp_ref, ysm_ref, buf, sem, *, n_prompt_tiles):
    i = pl.program_id(0)

    def issue(r, carry):
        for k in range(TOP_K):
            pltpu.make_async_copy(ys_hbm.at[pl.ds(slot_ref[0, 0, r * TOP_K + k], 1), :],
                                  buf.at[pl.ds(k * TOK_TILE + r, 1), :], sem).start()
        return carry

    lax.fori_loop(0, TOK_TILE, issue, 0, unroll=4)
    for k in range(TOP_K):
        _row_wait(ys_hbm.at[pl.ds(0, TOK_TILE), :], buf.at[pl.ds(k * TOK_TILE, TOK_TILE), :], sem)
    g = gate_ref[...]
    f = jnp.zeros((TOK_TILE, D_MODEL), F32)
    for k in range(TOP_K):
        f = f + buf[k * TOK_TILE:(k + 1) * TOK_TILE, :] * g[:, k:k + 1]
    is_sample = i >= n_prompt_tiles
    x1 = jnp.where(is_sample, x1s_ref[...], x1p_ref[...])
    g2 = jnp.where(is_sample, g2s_ref[...], g2p_ref[0])
    y = x1 + g2 * _rms(f, gpost_ref[...])

    @pl.when(jnp.logical_not(is_sample))
    def _():
        yp_ref[...] = y

    @pl.when(is_sample)
    def _():
        ysm_ref[...] = y


def _moe_combine(ys, slot_tiles, gates, x1_p, x1_s, gate2_p, gate2_s, g_post, rows_per_batch):
    tiles = slot_tiles.shape[0]
    n_p = x1_p.shape[0] // TOK_TILE
    assert tiles == n_p + 1 and x1_s.shape[0] == TOK_TILE
    tiles_per_batch = rows_per_batch // TOK_TILE
    p_tile = lambda i: (jnp.minimum(i, n_p - 1), 0)
    const2 = lambda i: (0, 0)
    return pl.pallas_call(
        functools.partial(_combine_kernel, n_prompt_tiles=n_p),
        grid=(tiles,),
        in_specs=[
            pl.BlockSpec((1, 1, TOK_TILE * TOP_K), lambda i: (i, 0, 0), memory_space=pltpu.SMEM),
            pl.BlockSpec((TOK_TILE, LANES), lambda i: (i, 0)),
            pl.BlockSpec((TOK_TILE, D_MODEL), p_tile),
            pl.BlockSpec((TOK_TILE, D_MODEL), const2),
            pl.BlockSpec((1, 1, D_MODEL), lambda i: (jnp.minimum(i, n_p - 1) // tiles_per_batch, 0, 0)),
            pl.BlockSpec((TOK_TILE, D_MODEL), const2),
            pl.BlockSpec((1, D_MODEL), const2),
            pl.BlockSpec(memory_space=pl.ANY),
        ],
        out_specs=[pl.BlockSpec((TOK_TILE, D_MODEL), p_tile), pl.BlockSpec((TOK_TILE, D_MODEL), const2)],
        out_shape=[jax.ShapeDtypeStruct(x1_p.shape, F32), jax.ShapeDtypeStruct(x1_s.shape, F32)],
        scratch_shapes=[pltpu.VMEM((TOP_K * TOK_TILE, D_MODEL), F32), pltpu.SemaphoreType.DMA],
        compiler_params=_params(("arbitrary",)),
        name="moe_combine",
    )(slot_tiles, gates, x1_p, x1_s, gate2_p, gate2_s, g_post.reshape(1, D_MODEL), ys)


def _moe_plan(top_idx):
    a = top_idx.shape[0] * TOP_K
    assert a % MOE_M == 0
    flat_e = top_idx[:, :TOP_K].reshape(a)
    onehot = (flat_e[:, None] == jnp.arange(N_EXPERTS)[None, :]).astype(jnp.int32)
    csum = jnp.cumsum(onehot, axis=0)
    counts = csum[-1]
    g_end = jnp.cumsum(counts)
    g_start = g_end - counts
    rank = jnp.sum(csum * onehot, axis=1) - 1
    slot_of = (jnp.sum(g_start[None, :] * onehot, axis=1) + rank).astype(jnp.int32)

    n_tiles = a // MOE_M
    n_vis = n_tiles + N_EXPERTS - 1
    first_tile = g_start // MOE_M
    last_tile = jnp.where(counts > 0, (g_end - 1) // MOE_M, first_tile)
    nvis_e = jnp.where(counts > 0, last_tile - first_tile + 1, 0)
    v_end = jnp.cumsum(nvis_e)
    v_start = v_end - nvis_e
    total = v_end[-1]
    v = jnp.arange(n_vis)
    vc = jnp.minimum(v, total - 1)
    e_of = jnp.minimum(jnp.sum((vc[:, None] >= v_end[None, :]).astype(jnp.int32), axis=1), N_EXPERTS - 1)
    tile_of = first_tile[e_of] + (vc - v_start[e_of])
    lo = jnp.maximum(g_start[e_of], tile_of * MOE_M) - tile_of * MOE_M
    hi = jnp.minimum(g_end[e_of], (tile_of + 1) * MOE_M) - tile_of * MOE_M
    prev_tile = jnp.concatenate([jnp.full((1,), -1, tile_of.dtype), tile_of[:-1]])
    first = (tile_of != prev_tile).astype(jnp.int32)
    i32 = lambda t: t.astype(jnp.int32)
    visits = (i32(tile_of), i32(e_of), i32(lo), i32(hi), first, i32(total).reshape(1))
    return slot_of, visits


def _reorder_w_in(w_in):
    off_om_end = 4 * D_M
    off_g_end = off_om_end + 2 * H_M
    w = jnp.concatenate([
        w_in[:, :off_om_end],
        w_in[:, off_g_end:],
        w_in[:, off_om_end:off_g_end],
        jnp.zeros((D_MODEL, LANES - 2 * H_M), w_in.dtype),
    ], axis=1)
    return w.astype(BF16)


def kernel(x_prompt, x_sample, cache_k, cache_v, state_conv, state_C, state_n, state_m, c_prompt, c_sample, w_ada, b_ada, g_pre_mix, g_post_mix, g_pre_ffn, g_post_ffn, w_in, b_ig, b_fg, conv_w, conv_b, mh_norm, rel_bias, w_out, w_router, b_router, w1, b1, w2, b2):
    depth = w_in.shape[0]
    assert depth == 1
    l = 0
    nb_p, seq, _ = x_prompt.shape
    nb_s, n_tok, _ = x_sample.shape
    rows_s = nb_s * n_tok
    assert rows_s == TOK_TILE

    n_c = nb_p + nb_s
    c_pad = -(-n_c // 8) * 8
    c_all = jnp.concatenate([c_prompt, c_sample, jnp.zeros((c_pad - n_c, D_MODEL), F32)], axis=0)
    mod = _adaln(c_all, w_ada[l], b_ada[l])
    mods = [mod[:, i * D_MODEL:(i + 1) * D_MODEL] for i in range(6)]
    mods_p = [m[:nb_p].reshape(nb_p, 1, D_MODEL) for m in mods]
    mods_s = [jnp.repeat(m[nb_p:n_c], n_tok, axis=0).reshape(1, rows_s, D_MODEL) for m in mods]

    w_r = _reorder_w_in(w_in[l])
    bias_g = jnp.concatenate([b_ig[l], b_fg[l], jnp.zeros((LANES - 2 * H_M,), F32)]).reshape(1, LANES)
    w_out_b = w_out[l].astype(BF16)
    w_router_p = jnp.concatenate([w_router[l], jnp.zeros((D_MODEL, LANES - N_EXPERTS), F32)], axis=1)
    b_router_p = jnp.concatenate([b_router[l].astype(F32), jnp.zeros((LANES - N_EXPERTS,), F32)]).reshape(1, LANES)

    shift1, scale1, gate1, shift2, scale2, gate2_p = mods_p
    u, v_m, og, gts, qa, ka, va = _inproj(x_prompt, g_pre_mix[l], scale1, shift1, w_r, bias_g, tm=512)
    gates_row = gts[:, :, :8].transpose(0, 2, 1)
    zeros = lambda *s: jnp.zeros(s, F32)
    mem, c_p, n_p, m_p = _mlstm(u, v_m, og, gts, gates_row, zeros(nb_p, 8, 2 * D_M),
                                zeros(nb_p, H_M, DK_M, DK_M), zeros(nb_p, H_M, DK_M), zeros(nb_p, H_M, LANES),
                                conv_w[l], conv_b[l], mh_norm[l], MLSTM_L_PROMPT)
    branches = [_attn_branch(qa, ka, va, _prompt_bias(rel_bias, w, d), d) for (w, d) in BRANCHES]
    x1_p, h2_p, idx_p, gate_p = _postmix(mem, branches, x_prompt, gate1, scale2, shift2,
                                         g_post_mix[l], g_pre_ffn[l], w_out_b, w_router_p, b_router_p, tm=256)
    keep = min(WINDOW_MAX, seq)
    k_p = ka[:, seq - keep:].reshape(nb_p, keep, H_A, HD_A)
    v_p = va[:, seq - keep:].reshape(nb_p, keep, H_A, HD_A)
    conv_p = u[:, seq - (CONV_W - 1):]

    shift1, scale1, gate1, shift2, scale2, gate2_s = mods_s
    x_s = x_sample.reshape(1, rows_s, D_MODEL)
    u_s, v_s, og_s, gts_s, qa_s, ka_s, va_s = _inproj(x_s, g_pre_mix[l], scale1, shift1, w_r, bias_g, tm=rows_s)
    per_b = lambda t: t.reshape(nb_s, n_tok, t.shape[-1])
    ls = MLSTM_L_SAMPLE
    pad_rows = lambda t, n: jnp.concatenate([t, jnp.zeros((nb_s, n - t.shape[1], t.shape[2]), t.dtype)], axis=1)
    inert = jnp.concatenate([jnp.full((H_M,), NEG_INF, F32), jnp.zeros((LANES - H_M,), F32)])
    gts_pad = jnp.concatenate([per_b(gts_s), jnp.broadcast_to(inert, (nb_s, ls - n_tok, LANES))], axis=1)
    tail0 = jnp.concatenate([zeros(nb_s, 8 - (CONV_W - 1), 2 * D_M), state_conv[l].astype(F32)], axis=1)
    m0x = jnp.broadcast_to(state_m[l].astype(F32)[:, :, None], (nb_s, H_M, LANES))
    mem_s, c_s, n_s, m_s = _mlstm(pad_rows(per_b(u_s), ls), pad_rows(per_b(v_s), ls), pad_rows(per_b(og_s), ls),
                                  gts_pad, gts_pad[:, :, :8].transpose(0, 2, 1), tail0,
                                  state_C[l].astype(F32), state_n[l].astype(F32), m0x,
                                  conv_w[l], conv_b[l], mh_norm[l], ls)
    mem_s = mem_s[:, :n_tok].reshape(1, rows_s, D_M)
    heads = lambda t: t.reshape(nb_s, n_tok, H_A, HD_A)
    k_new, v_new = heads(ka_s), heads(va_s)
    att_s = _sample_attention(heads(qa_s), k_new, v_new, cache_k[l], cache_v[l],
                              _sample_bias(rel_bias, n_tok), n_tok)
    att_s = att_s.reshape(1, rows_s, D_A)
    zero_o = zeros(1, rows_s, D_A)
    ninf = jnp.full((1, rows_s, D_A), NEG_INF, F32)
    branches_s = [(att_s, zero_o), (zero_o, ninf), (zero_o, ninf)]
    x1_s, h2_s, idx_s, gate_s = _postmix(mem_s, branches_s, x_s, gate1, scale2, shift2,
                                         g_post_mix[l], g_pre_ffn[l], w_out_b, w_router_p, b_router_p, tm=rows_s)

    t_p = nb_p * seq
    h_all = jnp.concatenate([h2_p.reshape(t_p, D_MODEL), h2_s.reshape(rows_s, D_MODEL)], axis=0)
    idx_all = jnp.concatenate([idx_p.reshape(t_p, LANES), idx_s.reshape(rows_s, LANES)], axis=0)
    gate_all = jnp.concatenate([gate_p.reshape(t_p, LANES), gate_s.reshape(rows_s, LANES)], axis=0)
    slot_of, visits = _moe_plan(idx_all)
    slot_tiles = slot_of.reshape((t_p + rows_s) // TOK_TILE, 1, TOK_TILE * TOP_K)
    xs = _moe_scatter(h_all, slot_tiles)
    ys = _moe_experts(xs, visits, w1[l], b1[l], w2[l], b2[l])
    y_p, y_s = _moe_combine(ys, slot_tiles, gate_all, x1_p.reshape(t_p, D_MODEL), x1_s.reshape(rows_s, D_MODEL),
                            gate2_p, gate2_s.reshape(rows_s, D_MODEL), g_post_ffn[l], seq)
    y_prompt = y_p.reshape(nb_p, seq, D_MODEL)
    y_sample = y_s.reshape(nb_s, n_tok, D_MODEL)

    st = lambda t: t[None]
    conv_s = jnp.concatenate([state_conv[l].astype(F32), per_b(u_s)], axis=1)[:, -(CONV_W - 1):]
    return (y_prompt, y_sample, st(k_p), st(v_p), st(conv_p), st(c_p), st(n_p), st(m_p[:, :, 0]),
            st(k_new), st(v_new), st(conv_s), st(c_s), st(n_s), st(m_s[:, :, 0]))
```

```python
import functools
import math

import jax
import jax.numpy as jnp
import numpy as np
from jax import lax
from jax.experimental import pallas as pl
from jax.experimental.pallas import tpu as pltpu

F32 = jnp.float32
BF16 = jnp.bfloat16
U32 = jnp.uint32
NEG_INF = float("-inf")

D_MODEL = 1024
H_M = 4
D_M = 512
DK_M = 128
CONV_W = 4
H_A = 8
D_A = 512
HD_A = 64
BRANCHES = ((128, 1), (512, 4), (2048, 16))
WINDOW_MAX = 2048
QB = 128
N_BUCKETS = 32
MAX_DISTANCE = 2048
N_EXPERTS = 32
TOP_K = 4
D_FF = 1024
SWIGLU_LIMIT = 7.0
SWIGLU_ALPHA = 1.702
EPS = 1e-6

LANES = 128
C_U, C_V, C_O, C_QA, C_KA, C_VA, C_G = 0, 1024, 1536, 2048, 2560, 3072, 3584
D_IN_R = C_G + LANES

MLSTM_L_PROMPT = 256
MLSTM_L_SAMPLE = 128
ATT_QL = 256
MOE_M = 512
MOE_SUB = 256
TOK_TILE = 128
PACK_W = D_MODEL // 2
HI_MASK = 0xFFFF0000
VMEM_LIMIT = 56 * 1024 * 1024


def _params(sem, vmem=None):
    return pltpu.CompilerParams(dimension_semantics=sem, vmem_limit_bytes=vmem or VMEM_LIMIT)


def _rms(x, g):
    return x * lax.rsqrt(jnp.mean(x * x, axis=-1, keepdims=True) + EPS) * g


def _log_sigmoid(t):
    return jnp.minimum(t, 0.0) - jnp.log1p(jnp.exp(-jnp.abs(t)))


def _ada_kernel(c_ref, w_ref, b_ref, o_ref):
    c = c_ref[...]
    a = (c * jax.nn.sigmoid(c)).astype(BF16)
    o_ref[...] = jnp.dot(a, w_ref[...].astype(BF16), preferred_element_type=F32) + b_ref[...]


def _adaln(c_all, w_ada, b_ada):
    rows = c_all.shape[0]
    n = w_ada.shape[1]
    tn = 1024
    return pl.pallas_call(
        _ada_kernel,
        grid=(n // tn,),
        in_specs=[
            pl.BlockSpec((rows, D_MODEL), lambda j: (0, 0)),
            pl.BlockSpec((D_MODEL, tn), lambda j: (0, j)),
            pl.BlockSpec((1, tn), lambda j: (0, j)),
        ],
        out_specs=pl.BlockSpec((rows, tn), lambda j: (0, j)),
        out_shape=jax.ShapeDtypeStruct((rows, n), F32),
        compiler_params=_params(("arbitrary",)),
        name="adaln",
    )(c_all, w_ada, b_ada.reshape(1, n))


def _inproj_kernel(x_ref, g_ref, sc_ref, sh_ref, w_ref, bg_ref,
                   u_ref, v_ref, o_ref, gt_ref, qa_ref, ka_ref, va_ref):
    x = x_ref[0]
    h = _rms(x, g_ref[...]) * (1.0 + sc_ref[0]) + sh_ref[0]
    hb = h.astype(BF16)

    def seg(lo, width):
        return jnp.dot(hb, w_ref[:, lo:lo + width], preferred_element_type=F32)

    u_ref[0] = seg(C_U, 2 * D_M)
    v_ref[0] = seg(C_V, D_M)
    o_ref[0] = jax.nn.sigmoid(seg(C_O, D_M))
    qa_ref[0] = seg(C_QA, D_A) * (HD_A ** -0.5)
    ka_ref[0] = seg(C_KA, D_A)
    va_ref[0] = seg(C_VA, D_A)
    t = seg(C_G, LANES) + bg_ref[...]
    lane = lax.broadcasted_iota(jnp.int32, t.shape, 1)
    gt_ref[0] = jnp.where(lane < H_M, t, _log_sigmoid(t))


def _inproj(x3, g_pre, scale, shift, w_r, bias_g, tm):
    nb, rows, _ = x3.shape
    per_row_mod = scale.shape[1] != 1
    mod_block = (1, tm, D_MODEL) if per_row_mod else (1, 1, D_MODEL)
    mod_map = (lambda b, i: (b, i, 0)) if per_row_mod else (lambda b, i: (b, 0, 0))
    row_map = lambda b, i: (b, i, 0)
    widths = (2 * D_M, D_M, D_M, LANES, D_A, D_A, D_A)
    return pl.pallas_call(
        _inproj_kernel,
        grid=(nb, rows // tm),
        in_specs=[
            pl.BlockSpec((1, tm, D_MODEL), row_map),
            pl.BlockSpec((1, D_MODEL), lambda b, i: (0, 0)),
            pl.BlockSpec(mod_block, mod_map),
            pl.BlockSpec(mod_block, mod_map),
            pl.BlockSpec((D_MODEL, D_IN_R), lambda b, i: (0, 0)),
            pl.BlockSpec((1, LANES), lambda b, i: (0, 0)),
        ],
        out_specs=[pl.BlockSpec((1, tm, w), row_map) for w in widths],
        out_shape=[jax.ShapeDtypeStruct((nb, rows, w), F32) for w in widths],
        compiler_params=_params(("arbitrary", "arbitrary")),
        name="inproj",
    )(x3, g_pre.reshape(1, D_MODEL), scale, shift, w_r, bias_g)


def _mlstm_kernel(u_ref, v_ref, og_ref, gc_ref, gr_ref, tail0_ref, c0_ref, n0_ref, m0_ref,
                  cw_ref, cb_ref, mh_ref,
                  mem_ref, c_out_ref, n_out_ref, m_out_ref,
                  c_s, n_s, m_s, tail_s, uext_s, *, chunk):
    L = chunk
    step = pl.program_id(1)

    @pl.when(step == 0)
    def _():
        c_s[...] = c0_ref[0]
        n_s[...] = n0_ref[0]
        m_s[...] = m0_ref[0]
        tail_s[...] = tail0_ref[0]

    u = u_ref[0]
    uext_s[0:8, :] = tail_s[...]
    uext_s[8:8 + L, :] = u
    acc = cb_ref[...] + uext_s[pl.ds(5, L), :] * cw_ref[0:1, :]
    acc = acc + uext_s[pl.ds(6, L), :] * cw_ref[1:2, :]
    acc = acc + uext_s[pl.ds(7, L), :] * cw_ref[2:3, :]
    acc = acc + u * cw_ref[3:4, :]
    tail_s[...] = uext_s[pl.ds(L, 8), :]
    qk = acc * jax.nn.sigmoid(acc)

    gc = gc_ref[0]
    gr = gr_ref[0]
    row = lax.broadcasted_iota(jnp.int32, (L, L), 0)
    col = lax.broadcasted_iota(jnp.int32, (L, L), 1)
    causal = col <= row
    tri = causal.astype(F32)
    tri_t = (row <= col).astype(F32)
    lane = lax.broadcasted_iota(jnp.int32, gc.shape, 1)
    srow = lax.broadcasted_iota(jnp.int32, gr.shape, 0)
    b_col_all = jnp.dot(tri, jnp.where(lane >= H_M, gc, 0.0), preferred_element_type=F32,
                        precision=lax.Precision.HIGHEST)
    b_row_all = jnp.dot(jnp.where(srow >= H_M, gr, 0.0), tri_t, preferred_element_type=F32,
                        precision=lax.Precision.HIGHEST)

    v_all = v_ref[0]
    og = og_ref[0]
    for h in range(H_M):
        q = qk[:, h * DK_M:(h + 1) * DK_M]
        k = qk[:, D_M + h * DK_M:D_M + (h + 1) * DK_M] * (DK_M ** -0.5)
        v = v_all[:, h * DK_M:(h + 1) * DK_M]
        li_row = gr[h:h + 1, :]
        li_col = gc[:, h:h + 1]
        b_row = b_row_all[H_M + h:H_M + h + 1, :]
        b_col = b_col_all[:, H_M + h:H_M + h + 1]
        m0 = m_s[h:h + 1, 0:1]
        c0 = c_s[h]
        n0 = n_s[h:h + 1, :]

        d = jnp.where(causal, b_col - b_row + li_row, NEG_INF)
        g_state = b_col + m0
        m = jnp.maximum(jnp.max(d, axis=-1, keepdims=True), g_state)
        w_state = jnp.exp(g_state - m)
        qb = q.astype(BF16)
        kb = k.astype(BF16)
        vb = v.astype(BF16)
        s = lax.dot_general(qb, kb, (((1,), (1,)), ((), ())), preferred_element_type=F32)
        s = s * jnp.exp(d - m)
        num = jnp.dot(s.astype(BF16), vb, preferred_element_type=F32)
        num = num + w_state * jnp.dot(qb, c0.astype(BF16), preferred_element_type=F32)
        den = jnp.sum(s, axis=-1, keepdims=True) + w_state * jnp.sum(q * n0, axis=-1, keepdims=True)
        hh = num / jnp.maximum(jnp.abs(den), jnp.exp(-m))

        b_last = b_col[L - 1:L, :]
        g_tok_row = b_last - b_row + li_row
        m_new = jnp.maximum(b_last + m0, jnp.max(g_tok_row, axis=-1, keepdims=True))
        w_tok_col = jnp.exp(b_last - b_col + li_col - m_new)
        decay = jnp.exp(b_last + m0 - m_new)
        kw = k * w_tok_col
        c_s[h] = decay * c0 + jnp.dot(kw.T.astype(BF16), vb, preferred_element_type=F32)
        n_s[h:h + 1, :] = decay * n0 + jnp.sum(kw, axis=0, keepdims=True)
        m_s[h:h + 1, :] = jnp.broadcast_to(m_new, (1, LANES))

        hn = hh * lax.rsqrt(jnp.mean(hh * hh, axis=-1, keepdims=True) + EPS)
        hn = hn * mh_ref[:, h * DK_M:(h + 1) * DK_M]
        mem_ref[0, :, h * DK_M:(h + 1) * DK_M] = og[:, h * DK_M:(h + 1) * DK_M] * hn

    @pl.when(step == pl.num_programs(1) - 1)
    def _():
        c_out_ref[0] = c_s[...]
        n_out_ref[0] = n_s[...]
        m_out_ref[0] = m_s[...]


def _mlstm(u, v, og, gates_col, gates_row, tail0, c0, n0, m0x, conv_w, conv_b, mh_norm, chunk):
    nb, rows, _ = u.shape
    nc = rows // chunk
    row_map = lambda b, c: (b, c, 0)
    bat3 = lambda b, c: (b, 0, 0)
    const2 = lambda b, c: (0, 0)
    return pl.pallas_call(
        functools.partial(_mlstm_kernel, chunk=chunk),
        grid=(nb, nc),
        in_specs=[
            pl.BlockSpec((1, chunk, 2 * D_M), row_map),
            pl.BlockSpec((1, chunk, D_M), row_map),
            pl.BlockSpec((1, chunk, D_M), row_map),
            pl.BlockSpec((1, chunk, LANES), row_map),
            pl.BlockSpec((1, 8, chunk), lambda b, c: (b, 0, c)),
            pl.BlockSpec((1, 8, 2 * D_M), bat3),
            pl.BlockSpec((1, H_M, DK_M, DK_M), lambda b, c: (b, 0, 0, 0)),
            pl.BlockSpec((1, H_M, DK_M), bat3),
            pl.BlockSpec((1, H_M, LANES), bat3),
            pl.BlockSpec((CONV_W, 2 * D_M), const2),
            pl.BlockSpec((1, 2 * D_M), const2),
            pl.BlockSpec((1, D_M), const2),
        ],
        out_specs=[
            pl.BlockSpec((1, chunk, D_M), row_map),
            pl.BlockSpec((1, H_M, DK_M, DK_M), lambda b, c: (b, 0, 0, 0)),
            pl.BlockSpec((1, H_M, DK_M), bat3),
            pl.BlockSpec((1, H_M, LANES), bat3),
        ],
        out_shape=[
            jax.ShapeDtypeStruct((nb, rows, D_M), F32),
            jax.ShapeDtypeStruct((nb, H_M, DK_M, DK_M), F32),
            jax.ShapeDtypeStruct((nb, H_M, DK_M), F32),
            jax.ShapeDtypeStruct((nb, H_M, LANES), F32),
        ],
        scratch_shapes=[
            pltpu.VMEM((H_M, DK_M, DK_M), F32),
            pltpu.VMEM((H_M, DK_M), F32),
            pltpu.VMEM((H_M, LANES), F32),
            pltpu.VMEM((8, 2 * D_M), F32),
            pltpu.VMEM((chunk + 8, 2 * D_M), F32),
        ],
        compiler_params=_params(("arbitrary", "arbitrary")),
        name="mlstm",
    )(u, v, og, gates_col, gates_row, tail0, c0, n0, m0x,
      conv_w, conv_b.reshape(1, 2 * D_M), mh_norm.reshape(1, D_M))


def _attn_kernel(q_ref, kc_ref, vc_ref, kp_ref, vp_ref, bias_ref, o_ref, l_ref, k_s, v_s, *, ql):
    first = pl.program_id(2) == 0
    k_s[0:QB, :] = kp_ref[0].astype(BF16)
    k_s[QB:QB + ql, :] = kc_ref[0].astype(BF16)
    v_s[0:QB, :] = vp_ref[0].astype(BF16)
    v_s[QB:QB + ql, :] = vc_ref[0].astype(BF16)
    col = lax.broadcasted_iota(jnp.int32, (QB, 2 * QB), 1)
    pad_keys = jnp.logical_and(first, col < QB)
    for h in range(H_A):
        lo = h * HD_A
        bias = bias_ref[h]
        for j in range(ql // QB):
            qb = q_ref[0, j * QB:(j + 1) * QB, lo:lo + HD_A].astype(BF16)
            kb = k_s[j * QB:(j + 2) * QB, lo:lo + HD_A]
            vb = v_s[j * QB:(j + 2) * QB, lo:lo + HD_A]
            logits = lax.dot_general(qb, kb, (((1,), (1,)), ((), ())), preferred_element_type=F32)
            logits = logits + bias
            if j == 0:
                logits = jnp.where(pad_keys, NEG_INF, logits)
            m = jnp.max(logits, axis=-1, keepdims=True)
            p = jnp.exp(logits - m)
            s = jnp.sum(p, axis=-1, keepdims=True)
            o = jnp.dot(p.astype(BF16), vb, preferred_element_type=F32) / s
            o_ref[0, j * QB:(j + 1) * QB, lo:lo + HD_A] = o
            l_ref[0, j * QB:(j + 1) * QB, lo:lo + HD_A] = jnp.broadcast_to(m + jnp.log(s), (QB, HD_A))


def _attn_branch(qa, ka, va, bias, dil):
    nb, seq, _ = qa.shape
    ld = seq // dil
    ql = min(ATT_QL, ld)
    view = lambda t: t.reshape(nb, ld, dil * D_A)
    cur = lambda b, r, i: (b, i, r)
    prev = lambda b, r, i: (b, jnp.maximum(i * (ql // QB) - 1, 0), r)
    o, lse = pl.pallas_call(
        functools.partial(_attn_kernel, ql=ql),
        grid=(nb, dil, ld // ql),
        in_specs=[
            pl.BlockSpec((1, ql, D_A), cur),
            pl.BlockSpec((1, ql, D_A), cur),
            pl.BlockSpec((1, ql, D_A), cur),
            pl.BlockSpec((1, QB, D_A), prev),
            pl.BlockSpec((1, QB, D_A), prev),
            pl.BlockSpec((H_A, QB, 2 * QB), lambda b, r, i: (0, 0, 0)),
        ],
        out_specs=[pl.BlockSpec((1, ql, D_A), cur), pl.BlockSpec((1, ql, D_A), cur)],
        out_shape=[jax.ShapeDtypeStruct((nb, ld, dil * D_A), F32)] * 2,
        scratch_shapes=[pltpu.VMEM((QB + ql, D_A), BF16), pltpu.VMEM((QB + ql, D_A), BF16)],
        compiler_params=_params(("arbitrary", "arbitrary", "arbitrary")),
        name=f"attn_d{dil}",
    )(view(qa), view(ka), view(va), view(ka), view(va), bias)
    return o.reshape(nb, seq, D_A), lse.reshape(nb, seq, D_A)


def _heads_kernel(k_ref, v_ref, ko_ref, vo_ref):
    ko_ref[0] = pltpu.einshape("m(hd)->mhd", k_ref[0], h=H_A)
    vo_ref[0] = pltpu.einshape("m(hd)->mhd", v_ref[0], h=H_A)


def _window_heads(ka, va, keep, tm=256):
    nb, seq, _ = ka.shape
    first = (seq - keep) // tm
    in_spec = pl.BlockSpec((1, tm, D_A), lambda b, i: (b, first + i, 0))
    out_spec = pl.BlockSpec((1, tm, H_A, HD_A), lambda b, i: (b, i, 0, 0))
    return pl.pallas_call(
        _heads_kernel,
        grid=(nb, keep // tm),
        in_specs=[in_spec, in_spec],
        out_specs=[out_spec, out_spec],
        out_shape=[jax.ShapeDtypeStruct((nb, keep, H_A, HD_A), F32)] * 2,
        compiler_params=_params(("arbitrary", "arbitrary")),
        name="window_heads",
    )(ka, va)


def _t5_bucket(dist):
    max_exact = N_BUCKETS // 2
    n = jnp.maximum(dist, 1).astype(F32)
    large = max_exact + (jnp.log(n / max_exact) / math.log(MAX_DISTANCE / max_exact)
                         * (N_BUCKETS - max_exact)).astype(jnp.int32)
    large = jnp.minimum(large, N_BUCKETS - 1)
    return jnp.where(dist < max_exact, dist, large)


def _branch_bias_vec(rel_bias, window, dil):
    return rel_bias[_t5_bucket(dil * jnp.arange(window // dil + 1))].astype(F32)


def _prompt_bias(rel_bias, window, dil):
    assert window // dil == QB
    n = 2 * QB
    vec = _branch_bias_vec(rel_bias, window, dil)
    vecp = jnp.concatenate([vec[::-1], jnp.full((n - QB - 1, H_A), NEG_INF, F32)], axis=0)
    g = jnp.roll(vecp, QB - 1, axis=0)
    skew = jnp.tile(g, (QB + 1, 1))[:QB * (n + 1)].reshape(QB, n + 1, H_A)[:, :n]
    return skew[::-1].transpose(2, 0, 1)


def _sattn_kernel(q_ref, kn_ref, vn_ref, k1_ref, v1_ref, k4_ref, v4_ref, k16_ref, v16_ref,
                  b1_ref, b4_ref, b16_ref, bn_ref, att_ref, *, n_tok):
    kn = kn_ref[0]
    vn = vn_ref[0]
    for t in range(n_tok):
        q = q_ref[0, t]
        parts = []
        for bi, (_, dil) in enumerate(BRANCHES):
            if dil == 1:
                kc, vc, bc = k1_ref[0], v1_ref[0], b1_ref[t]
                k_new, v_new, b_new = kn, vn, bn_ref[t]
            else:
                k_ref, v_ref, b_ref = ((k4_ref, v4_ref, b4_ref) if dil == 4 else (k16_ref, v16_ref, b16_ref))
                kc, vc, bc = k_ref[0, :, t], v_ref[0, :, t], b_ref[...]
                k_new, v_new, b_new = kn[t:t + 1], vn[t:t + 1], bn_ref[0, 0:1]
            lc = jnp.sum(kc * q[None], axis=-1, keepdims=True) + bc
            ln = jnp.sum(k_new * q[None], axis=-1, keepdims=True) + b_new
            m = jnp.maximum(jnp.max(lc, axis=0, keepdims=True), jnp.max(ln, axis=0, keepdims=True))
            pc = jnp.exp(lc - m)
            pn = jnp.exp(ln - m)
            s = jnp.sum(pc, axis=0, keepdims=True) + jnp.sum(pn, axis=0, keepdims=True)
            o = jnp.sum(pc * vc, axis=0, keepdims=True) + jnp.sum(pn * v_new, axis=0, keepdims=True)
            parts.append((o / s, m + jnp.log(s)))
        lmax = functools.reduce(jnp.maximum, [l for _, l in parts])
        num = 0.0
        den = 0.0
        for o, l in parts:
            w = jnp.exp(l - lmax)
            num = num + w * o
            den = den + w
        att_ref[0, t] = (num / den)[0]


def _sample_attention(q4, kn4, vn4, cache_k, cache_v, biases, n_tok):
    nb, n_past = cache_k.shape[0], cache_k.shape[1]
    assert all(n_past % (dil * QB) == 0 and n_tok <= max(dil, n_tok) for _, dil in BRANCHES)
    tok_spec = pl.BlockSpec((1, n_tok, H_A, HD_A), lambda b: (b, 0, 0, 0))
    args, specs = [], []
    for _, dil in BRANCHES:
        last = n_past // dil // QB - 1
        for c in (cache_k, cache_v):
            if dil == 1:
                args.append(c)
                specs.append(pl.BlockSpec((1, QB, H_A, HD_A), functools.partial(lambda b, last: (b, last, 0, 0), last=last)))
            else:
                assert n_tok <= dil
                args.append(c.reshape(nb, n_past // dil, dil, H_A, HD_A))
                specs.append(pl.BlockSpec((1, QB, n_tok, H_A, HD_A),
                                          functools.partial(lambda b, last: (b, last, 0, 0, 0), last=last)))
    b1, b4, b16, bn = biases
    full = lambda a: pl.BlockSpec(a.shape, functools.partial(lambda b, nd: (0,) * nd, nd=a.ndim))
    return pl.pallas_call(
        functools.partial(_sattn_kernel, n_tok=n_tok),
        grid=(nb,),
        in_specs=[tok_spec] * 3 + specs + [full(b1), full(b4), full(b16), full(bn)],
        out_specs=tok_spec,
        out_shape=jax.ShapeDtypeStruct((nb, n_tok, H_A, HD_A), F32),
        compiler_params=_params(("arbitrary",)),
        name="sample_attn",
    )(q4, kn4, vn4, *args, b1, b4, b16, bn)


def _sample_bias(rel_bias, n_tok):
    expand = lambda x: jnp.broadcast_to(x[..., None], x.shape + (HD_A,))
    i = jnp.arange(QB)
    out = []
    for window, dil in BRANCHES:
        nk = window // dil
        assert nk == QB
        vec = _branch_bias_vec(rel_bias, window, dil)
        if dil == 1:
            tabs = []
            for t in range(n_tok):
                m = nk + t - i
                tabs.append(jnp.where((m <= nk)[:, None], vec[jnp.clip(m, 0, nk)], NEG_INF))
            out.append(expand(jnp.stack(tabs)))
        else:
            out.append(expand(vec[nk - i]))
    vec1 = _branch_bias_vec(rel_bias, *BRANCHES[0])
    j = jnp.arange(n_tok)
    new = []
    for t in range(n_tok):
        m = t - j
        new.append(jnp.where((m >= 0)[:, None], vec1[jnp.clip(m, 0, None)], NEG_INF))
    out.append(expand(jnp.stack(new)))
    return out


def _postmix_kernel(mem_ref, o1_ref, l1_ref, o2_ref, l2_ref, o3_ref, l3_ref, x_ref,
                    g1_ref, sc2_ref, sh2_ref, gpost_ref, gpre_ref, wo_ref, wr_ref, br_ref,
                    x1_ref, h2_ref, idx_ref, gate_ref):
    l1, l2, l3 = l1_ref[0], l2_ref[0], l3_ref[0]
    lmax = jnp.maximum(jnp.maximum(l1, l2), l3)
    w1 = jnp.exp(l1 - lmax)
    w2 = jnp.exp(l2 - lmax)
    w3 = jnp.exp(l3 - lmax)
    att = (w1 * o1_ref[0] + w2 * o2_ref[0] + w3 * o3_ref[0]) / (w1 + w2 + w3)
    y = jnp.dot(mem_ref[0].astype(BF16), wo_ref[0:D_M, :], preferred_element_type=F32)
    y = y + jnp.dot(att.astype(BF16), wo_ref[D_M:, :], preferred_element_type=F32)
    x1 = x_ref[0] + g1_ref[0] * _rms(y, gpost_ref[...])
    x1_ref[0] = x1
    h2 = _rms(x1, gpre_ref[...]) * (1.0 + sc2_ref[0]) + sh2_ref[0]
    h2_ref[0] = h2
    h_hi = h2.astype(BF16)
    h_lo = (h2 - h_hi.astype(F32)).astype(BF16)
    w_hi = wr_ref[0]
    logits = jnp.dot(h_hi, w_hi, preferred_element_type=F32)
    logits = logits + jnp.dot(h_lo, w_hi, preferred_element_type=F32)
    logits = logits + jnp.dot(h_hi, wr_ref[1], preferred_element_type=F32) + br_ref[...]
    lane = lax.broadcasted_iota(jnp.int32, logits.shape, 1)
    lane_f = lane.astype(F32)
    logits = jnp.where(lane < N_EXPERTS, logits, NEG_INF)
    vals, idxs = [], []
    for _ in range(TOP_K):
        mx = jnp.max(logits, axis=-1, keepdims=True)
        ix = jnp.min(jnp.where(logits == mx, lane_f, float(LANES)), axis=-1, keepdims=True)
        vals.append(mx)
        idxs.append(ix)
        logits = jnp.where(lane_f == ix, NEG_INF, logits)
    es = [jnp.exp(v - vals[0]) for v in vals]
    tot = es[0] + es[1] + es[2] + es[3]
    idx_tile = jnp.zeros(logits.shape, F32)
    gate_tile = jnp.zeros(logits.shape, F32)
    for k in range(TOP_K):
        idx_tile = jnp.where(lane == k, idxs[k], idx_tile)
        gate_tile = jnp.where(lane == k, es[k] / tot, gate_tile)
    idx_ref[0] = idx_tile.astype(jnp.int32)
    gate_ref[0] = gate_tile


def _postmix(mem, branches, x3, gate1, scale2, shift2, g_post, g_pre, w_out_b, w_router_p, b_router_p, tm):
    nb, rows, _ = x3.shape
    per_row_mod = gate1.shape[1] != 1
    mod_block = (1, tm, D_MODEL) if per_row_mod else (1, 1, D_MODEL)
    mod_map = (lambda b, i: (b, i, 0)) if per_row_mod else (lambda b, i: (b, 0, 0))
    row_map = lambda b, i: (b, i, 0)
    const2 = lambda b, i: (0, 0)
    half = pl.BlockSpec((1, tm, D_M), row_map)
    flat = [a for pair in branches for a in pair]
    return pl.pallas_call(
        _postmix_kernel,
        grid=(nb, rows // tm),
        in_specs=[half] * 7 + [
            pl.BlockSpec((1, tm, D_MODEL), row_map),
            pl.BlockSpec(mod_block, mod_map),
            pl.BlockSpec(mod_block, mod_map),
            pl.BlockSpec(mod_block, mod_map),
            pl.BlockSpec((1, D_MODEL), const2),
            pl.BlockSpec((1, D_MODEL), const2),
            pl.BlockSpec((D_MODEL, D_MODEL), const2),
            pl.BlockSpec((2, D_MODEL, LANES), lambda b, i: (0, 0, 0)),
            pl.BlockSpec((1, LANES), const2),
        ],
        out_specs=[
            pl.BlockSpec((1, tm, D_MODEL), row_map),
            pl.BlockSpec((1, tm, D_MODEL), row_map),
            pl.BlockSpec((1, tm, LANES), row_map),
            pl.BlockSpec((1, tm, LANES), row_map),
        ],
        out_shape=[
            jax.ShapeDtypeStruct((nb, rows, D_MODEL), F32),
            jax.ShapeDtypeStruct((nb, rows, D_MODEL), F32),
            jax.ShapeDtypeStruct((nb, rows, LANES), jnp.int32),
            jax.ShapeDtypeStruct((nb, rows, LANES), F32),
        ],
        compiler_params=_params(("arbitrary", "arbitrary")),
        name="postmix",
    )(mem, *flat, x3, gate1, scale2, shift2, g_post.reshape(1, D_MODEL), g_pre.reshape(1, D_MODEL),
      w_out_b, w_router_p, b_router_p)


def _row_wait(src_rows, dst_rows, sem):
    pltpu.make_async_copy(src_rows, dst_rows, sem).wait()


def _scatter_kernel(slot_ref, h_ref, xs_hbm, buf, sem):
    bits = pltpu.bitcast(h_ref[...].astype(BF16).astype(F32), U32)
    buf[...] = (bits[:, :PACK_W] & U32(HI_MASK)) | (bits[:, PACK_W:] >> 16)

    def issue(r, carry):
        src = buf.at[pl.ds(r, 1), :]
        for k in range(TOP_K):
            pltpu.make_async_copy(src, xs_hbm.at[pl.ds(slot_ref[0, 0, r * TOP_K + k], 1), :],
                                  sem).start(priority=k % 2)
        return carry

    lax.fori_loop(0, TOK_TILE, issue, 0, unroll=4)
    for _ in range(TOP_K):
        _row_wait(buf, xs_hbm.at[pl.ds(0, TOK_TILE), :], sem)


def _moe_scatter(h_all, slot_tiles):
    tiles = slot_tiles.shape[0]
    n_rows = tiles * TOK_TILE * TOP_K
    return pl.pallas_call(
        _scatter_kernel,
        grid=(tiles,),
        in_specs=[
            pl.BlockSpec((1, 1, TOK_TILE * TOP_K), lambda i: (i, 0, 0), memory_space=pltpu.SMEM),
            pl.BlockSpec((TOK_TILE, D_MODEL), lambda i: (i, 0)),
        ],
        out_specs=pl.BlockSpec(memory_space=pl.ANY),
        out_shape=jax.ShapeDtypeStruct((n_rows, PACK_W), U32),
        scratch_shapes=[pltpu.VMEM((TOK_TILE, PACK_W), U32), pltpu.SemaphoreType.DMA],
        compiler_params=_params(("arbitrary",)),
        name="moe_scatter",
    )(slot_tiles, h_all)


def _expert_kernel(tile_ref, exp_ref, lo_ref, hi_ref, first_ref, nvis_ref,
                   xs_ref, w1_ref, b1_ref, w2_ref, b2_ref, ys_ref, w1b, w2b):
    v = pl.program_id(0)
    active = v < nvis_ref[0]
    changed = jnp.logical_or(v == 0, exp_ref[v] != exp_ref[jnp.maximum(v - 1, 0)])
    lo = lo_ref[v]
    hi = hi_ref[v]

    @pl.when(jnp.logical_and(active, changed))
    def _():
        w1b[...] = w1_ref[0].astype(BF16)
        w2b[...] = w2_ref[0].astype(BF16)

    @pl.when(jnp.logical_and(active, first_ref[v] == 1))
    def _():
        ys_ref[...] = jnp.zeros(ys_ref.shape, F32)

    for r in range(MOE_M // MOE_SUB):
        r0 = r * MOE_SUB

        @pl.when(jnp.logical_and(active, jnp.logical_and(lo < r0 + MOE_SUB, hi > r0)))
        def _():
            w = xs_ref[r0:r0 + MOE_SUB, :]
            x_hi = pltpu.bitcast(w & U32(HI_MASK), F32).astype(BF16)
            x_lo = pltpu.bitcast(w << 16, F32).astype(BF16)
            hgu = jnp.dot(x_hi, w1b[0:PACK_W, :], preferred_element_type=F32)
            hgu = hgu + jnp.dot(x_lo, w1b[PACK_W:, :], preferred_element_type=F32) + b1_ref[0]
            x_glu = jnp.minimum(hgu[:, :D_FF], SWIGLU_LIMIT)
            x_lin = jnp.clip(hgu[:, D_FF:], -SWIGLU_LIMIT, SWIGLU_LIMIT)
            act = x_glu * jax.nn.sigmoid(SWIGLU_ALPHA * x_glu) * (x_lin + 1.0)
            y = jnp.dot(act.astype(BF16), w2b[...], preferred_element_type=F32) + b2_ref[0]
            rows = r0 + lax.broadcasted_iota(jnp.int32, (MOE_SUB, 1), 0)
            mine = jnp.logical_and(rows >= lo, rows < hi)
            ys_ref[r0:r0 + MOE_SUB, :] = jnp.where(mine, y, ys_ref[r0:r0 + MOE_SUB, :])


def _moe_experts(xs, visits, w1, b1, w2, b2):
    n_rows = xs.shape[0]
    n_vis = visits[0].shape[0]
    tile_map = lambda v, tile, *_: (tile[v], 0)
    exp_map = lambda v, tile, exp, *_: (exp[v], 0, 0)
    return pl.pallas_call(
        _expert_kernel,
        grid_spec=pltpu.PrefetchScalarGridSpec(
            num_scalar_prefetch=6,
            grid=(n_vis,),
            in_specs=[
                pl.BlockSpec((MOE_M, PACK_W), tile_map),
                pl.BlockSpec((1, D_MODEL, 2 * D_FF), exp_map),
                pl.BlockSpec((1, 1, 2 * D_FF), exp_map),
                pl.BlockSpec((1, D_FF, D_MODEL), exp_map),
                pl.BlockSpec((1, 1, D_MODEL), exp_map),
            ],
            out_specs=pl.BlockSpec((MOE_M, D_MODEL), tile_map),
            scratch_shapes=[pltpu.VMEM((D_MODEL, 2 * D_FF), BF16), pltpu.VMEM((D_FF, D_MODEL), BF16)],
        ),
        out_shape=jax.ShapeDtypeStruct((n_rows, D_MODEL), F32),
        compiler_params=_params(("arbitrary",)),
        name="moe_experts",
    )(*visits, xs, w1, b1.reshape(N_EXPERTS, 1, 2 * D_FF), w2, b2.reshape(N_EXPERTS, 1, D_MODEL))


def _combine_kernel(slot_ref, gate_ref, x1p_ref, x1s_ref, g2p_ref, g2s_ref, gpost_ref, ys_hbm,
                    yp_ref, ysm_ref, buf, sem, *, n_prompt_tiles):
    i = pl.program_id(0)

    def issue(r, carry):
        for k in range(TOP_K):
            pltpu.make_async_copy(ys_hbm.at[pl.ds(slot_ref[0, 0, r * TOP_K + k], 1), :],
                                  buf.at[pl.ds(k * TOK_TILE + r, 1), :], sem).start(priority=k % 2)
        return carry

    lax.fori_loop(0, TOK_TILE, issue, 0, unroll=4)
    for k in range(TOP_K):
        _row_wait(ys_hbm.at[pl.ds(0, TOK_TILE), :], buf.at[pl.ds(k * TOK_TILE, TOK_TILE), :], sem)
    g = gate_ref[...]
    f = jnp.zeros((TOK_TILE, D_MODEL), F32)
    for k in range(TOP_K):
        f = f + buf[k * TOK_TILE:(k + 1) * TOK_TILE, :] * g[:, k:k + 1]
    is_sample = i >= n_prompt_tiles
    x1 = jnp.where(is_sample, x1s_ref[...], x1p_ref[...])
    g2 = jnp.where(is_sample, g2s_ref[...], g2p_ref[0])
    y = x1 + g2 * _rms(f, gpost_ref[...])

    @pl.when(jnp.logical_not(is_sample))
    def _():
        yp_ref[...] = y

    @pl.when(is_sample)
    def _():
        ysm_ref[...] = y


def _moe_combine(ys, slot_tiles, gates, x1_p, x1_s, gate2_p, gate2_s, g_post, rows_per_batch):
    tiles = slot_tiles.shape[0]
    n_p = x1_p.shape[0] // TOK_TILE
    assert tiles == n_p + 1 and x1_s.shape[0] == TOK_TILE
    tiles_per_batch = rows_per_batch // TOK_TILE
    p_tile = lambda i: (jnp.minimum(i, n_p - 1), 0)
    const2 = lambda i: (0, 0)
    return pl.pallas_call(
        functools.partial(_combine_kernel, n_prompt_tiles=n_p),
        grid=(tiles,),
        in_specs=[
            pl.BlockSpec((1, 1, TOK_TILE * TOP_K), lambda i: (i, 0, 0), memory_space=pltpu.SMEM),
            pl.BlockSpec((TOK_TILE, LANES), lambda i: (i, 0)),
            pl.BlockSpec((TOK_TILE, D_MODEL), p_tile),
            pl.BlockSpec((TOK_TILE, D_MODEL), const2),
            pl.BlockSpec((1, 1, D_MODEL), lambda i: (jnp.minimum(i, n_p - 1) // tiles_per_batch, 0, 0)),
            pl.BlockSpec((TOK_TILE, D_MODEL), const2),
            pl.BlockSpec((1, D_MODEL), const2),
            pl.BlockSpec(memory_space=pl.ANY),
        ],
        out_specs=[pl.BlockSpec((TOK_TILE, D_MODEL), p_tile), pl.BlockSpec((TOK_TILE, D_MODEL), const2)],
        out_shape=[jax.ShapeDtypeStruct(x1_p.shape, F32), jax.ShapeDtypeStruct(x1_s.shape, F32)],
        scratch_shapes=[pltpu.VMEM((TOP_K * TOK_TILE, D_MODEL), F32), pltpu.SemaphoreType.DMA],
        compiler_params=_params(("arbitrary",)),
        name="moe_combine",
    )(slot_tiles, gates, x1_p, x1_s, gate2_p, gate2_s, g_post.reshape(1, D_MODEL), ys)


def _moe_plan(top_idx):
    a = top_idx.shape[0] * TOP_K
    assert a % MOE_M == 0
    flat_e = top_idx[:, :TOP_K].reshape(a)
    onehot = (flat_e[:, None] == jnp.arange(N_EXPERTS)[None, :]).astype(jnp.int32)
    csum = jnp.cumsum(onehot, axis=0)
    counts = csum[-1]
    g_end = jnp.cumsum(counts)
    g_start = g_end - counts
    rank = jnp.sum(csum * onehot, axis=1) - 1
    slot_of = (jnp.sum(g_start[None, :] * onehot, axis=1) + rank).astype(jnp.int32)

    n_tiles = a // MOE_M
    n_vis = n_tiles + N_EXPERTS - 1
    first_tile = g_start // MOE_M
    last_tile = jnp.where(counts > 0, (g_end - 1) // MOE_M, first_tile)
    nvis_e = jnp.where(counts > 0, last_tile - first_tile + 1, 0)
    v_end = jnp.cumsum(nvis_e)
    v_start = v_end - nvis_e
    total = v_end[-1]
    v = jnp.arange(n_vis)
    vc = jnp.minimum(v, total - 1)
    e_of = jnp.minimum(jnp.sum((vc[:, None] >= v_end[None, :]).astype(jnp.int32), axis=1), N_EXPERTS - 1)
    tile_of = first_tile[e_of] + (vc - v_start[e_of])
    lo = jnp.maximum(g_start[e_of], tile_of * MOE_M) - tile_of * MOE_M
    hi = jnp.minimum(g_end[e_of], (tile_of + 1) * MOE_M) - tile_of * MOE_M
    prev_tile = jnp.concatenate([jnp.full((1,), -1, tile_of.dtype), tile_of[:-1]])
    first = (tile_of != prev_tile).astype(jnp.int32)
    i32 = lambda t: t.astype(jnp.int32)
    visits = (i32(tile_of), i32(e_of), i32(lo), i32(hi), first, i32(total).reshape(1))
    return slot_of, visits


def _reorder_w_in(w_in):
    off_om_end = 4 * D_M
    off_g_end = off_om_end + 2 * H_M
    w = jnp.concatenate([
        w_in[:, :off_om_end],
        w_in[:, off_g_end:],
        w_in[:, off_om_end:off_g_end],
        jnp.zeros((D_MODEL, LANES - 2 * H_M), w_in.dtype),
    ], axis=1)
    return w.astype(BF16)


def kernel(x_prompt, x_sample, cache_k, cache_v, state_conv, state_C, state_n, state_m, c_prompt, c_sample, w_ada, b_ada, g_pre_mix, g_post_mix, g_pre_ffn, g_post_ffn, w_in, b_ig, b_fg, conv_w, conv_b, mh_norm, rel_bias, w_out, w_router, b_router, w1, b1, w2, b2):
    depth = w_in.shape[0]
    assert depth == 1
    l = 0
    nb_p, seq, _ = x_prompt.shape
    nb_s, n_tok, _ = x_sample.shape
    rows_s = nb_s * n_tok
    assert rows_s == TOK_TILE

    n_c = nb_p + nb_s
    c_pad = -(-n_c // 8) * 8
    c_all = jnp.concatenate([c_prompt, c_sample, jnp.zeros((c_pad - n_c, D_MODEL), F32)], axis=0)
    mod = _adaln(c_all, w_ada[l], b_ada[l])
    mods = [mod[:, i * D_MODEL:(i + 1) * D_MODEL] for i in range(6)]
    mods_p = [m[:nb_p].reshape(nb_p, 1, D_MODEL) for m in mods]
    mods_s = [jnp.repeat(m[nb_p:n_c], n_tok, axis=0).reshape(1, rows_s, D_MODEL) for m in mods]

    w_r = _reorder_w_in(w_in[l])
    bias_g = jnp.concatenate([b_ig[l], b_fg[l], jnp.zeros((LANES - 2 * H_M,), F32)]).reshape(1, LANES)
    w_out_b = w_out[l].astype(BF16)
    w_router_f = jnp.concatenate([w_router[l], jnp.zeros((D_MODEL, LANES - N_EXPERTS), F32)], axis=1)
    w_router_hi = w_router_f.astype(BF16)
    w_router_p = jnp.stack([w_router_hi, (w_router_f - w_router_hi.astype(F32)).astype(BF16)])
    b_router_p = jnp.concatenate([b_router[l].astype(F32), jnp.zeros((LANES - N_EXPERTS,), F32)]).reshape(1, LANES)

    shift1, scale1, gate1, shift2, scale2, gate2_p = mods_p
    u, v_m, og, gts, qa, ka, va = _inproj(x_prompt, g_pre_mix[l], scale1, shift1, w_r, bias_g, tm=512)
    gates_row = gts[:, :, :8].transpose(0, 2, 1)
    zeros = lambda *s: jnp.zeros(s, F32)
    mem, c_p, n_p, m_p = _mlstm(u, v_m, og, gts, gates_row, zeros(nb_p, 8, 2 * D_M),
                                zeros(nb_p, H_M, DK_M, DK_M), zeros(nb_p, H_M, DK_M), zeros(nb_p, H_M, LANES),
                                conv_w[l], conv_b[l], mh_norm[l], MLSTM_L_PROMPT)
    branches = [_attn_branch(qa, ka, va, _prompt_bias(rel_bias, w, d), d) for (w, d) in BRANCHES]
    x1_p, h2_p, idx_p, gate_p = _postmix(mem, branches, x_prompt, gate1, scale2, shift2,
                                         g_post_mix[l], g_pre_ffn[l], w_out_b, w_router_p, b_router_p, tm=256)
    keep = min(WINDOW_MAX, seq)
    k_p, v_p = _window_heads(ka, va, keep)
    conv_p = u[:, seq - (CONV_W - 1):]

    shift1, scale1, gate1, shift2, scale2, gate2_s = mods_s
    x_s = x_sample.reshape(1, rows_s, D_MODEL)
    u_s, v_s, og_s, gts_s, qa_s, ka_s, va_s = _inproj(x_s, g_pre_mix[l], scale1, shift1, w_r, bias_g, tm=rows_s)
    per_b = lambda t: t.reshape(nb_s, n_tok, t.shape[-1])
    ls = MLSTM_L_SAMPLE
    pad_rows = lambda t, n: jnp.concatenate([t, jnp.zeros((nb_s, n - t.shape[1], t.shape[2]), t.dtype)], axis=1)
    inert = jnp.concatenate([jnp.full((H_M,), NEG_INF, F32), jnp.zeros((LANES - H_M,), F32)])
    gts_pad = jnp.concatenate([per_b(gts_s), jnp.broadcast_to(inert, (nb_s, ls - n_tok, LANES))], axis=1)
    tail0 = jnp.concatenate([zeros(nb_s, 8 - (CONV_W - 1), 2 * D_M), state_conv[l].astype(F32)], axis=1)
    m0x = jnp.broadcast_to(state_m[l].astype(F32)[:, :, None], (nb_s, H_M, LANES))
    mem_s, c_s, n_s, m_s = _mlstm(pad_rows(per_b(u_s), ls), pad_rows(per_b(v_s), ls), pad_rows(per_b(og_s), ls),
                                  gts_pad, gts_pad[:, :, :8].transpose(0, 2, 1), tail0,
                                  state_C[l].astype(F32), state_n[l].astype(F32), m0x,
                                  conv_w[l], conv_b[l], mh_norm[l], ls)
    mem_s = mem_s[:, :n_tok].reshape(1, rows_s, D_M)
    heads = lambda t: t.reshape(nb_s, n_tok, H_A, HD_A)
    k_new, v_new = heads(ka_s), heads(va_s)
    att_s = _sample_attention(heads(qa_s), k_new, v_new, cache_k[l], cache_v[l],
                              _sample_bias(rel_bias, n_tok), n_tok)
    att_s = att_s.reshape(1, rows_s, D_A)
    zero_o = zeros(1, rows_s, D_A)
    ninf = jnp.full((1, rows_s, D_A), NEG_INF, F32)
    branches_s = [(att_s, zero_o), (zero_o, ninf), (zero_o, ninf)]
    x1_s, h2_s, idx_s, gate_s = _postmix(mem_s, branches_s, x_s, gate1, scale2, shift2,
                                         g_post_mix[l], g_pre_ffn[l], w_out_b, w_router_p, b_router_p, tm=rows_s)

    t_p = nb_p * seq
    h_all = jnp.concatenate([h2_p.reshape(t_p, D_MODEL), h2_s.reshape(rows_s, D_MODEL)], axis=0)
    idx_all = jnp.concatenate([idx_p.reshape(t_p, LANES), idx_s.reshape(rows_s, LANES)], axis=0)
    gate_all = jnp.concatenate([gate_p.reshape(t_p, LANES), gate_s.reshape(rows_s, LANES)], axis=0)
    slot_of, visits = _moe_plan(idx_all)
    slot_tiles = slot_of.reshape((t_p + rows_s) // TOK_TILE, 1, TOK_TILE * TOP_K)
    xs = _moe_scatter(h_all, slot_tiles)
    ys = _moe_experts(xs, visits, w1[l], b1[l], w2[l], b2[l])
    y_p, y_s = _moe_combine(ys, slot_tiles, gate_all, x1_p.reshape(t_p, D_MODEL), x1_s.reshape(rows_s, D_MODEL),
                            gate2_p, gate2_s.reshape(rows_s, D_MODEL), g_post_ffn[l], seq)
    y_prompt = y_p.reshape(nb_p, seq, D_MODEL)
    y_sample = y_s.reshape(nb_s, n_tok, D_MODEL)

    st = lambda t: t[None]
    conv_s = jnp.concatenate([state_conv[l].astype(F32), per_b(u_s)], axis=1)[:, -(CONV_W - 1):]
    return (y_prompt, y_sample, st(k_p), st(v_p), st(conv_p), st(c_p), st(n_p), st(m_p[:, :, 0]),
            st(k_new), st(v_new), st(conv_s), st(c_s), st(n_s), st(m_s[:, :, 0]))
```

```python
import functools
import math

import jax
import jax.numpy as jnp
import numpy as np
from jax import lax
from jax.experimental import pallas as pl
from jax.experimental.pallas import tpu as pltpu

F32 = jnp.float32
BF16 = jnp.bfloat16
U32 = jnp.uint32
NEG_INF = float("-inf")

D_MODEL = 1024
H_M = 4
D_M = 512
DK_M = 128
CONV_W = 4
H_A = 8
D_A = 512
HD_A = 64
BRANCHES = ((128, 1), (512, 4), (2048, 16))
WINDOW_MAX = 2048
QB = 128
N_BUCKETS = 32
MAX_DISTANCE = 2048
N_EXPERTS = 32
TOP_K = 4
D_FF = 1024
SWIGLU_LIMIT = 7.0
SWIGLU_ALPHA = 1.702
EPS = 1e-6

LANES = 128
C_U, C_V, C_O, C_QA, C_KA, C_VA, C_G = 0, 1024, 1536, 2048, 2560, 3072, 3584
D_IN_R = C_G + LANES

MLSTM_L_PROMPT = 256
MLSTM_L_SAMPLE = 128
ATT_QL = 256
MOE_M = 512
MOE_SUB = 256
TOK_TILE = 128
PACK_W = D_MODEL // 2
PACK_T = PACK_W // LANES
ROW_T = D_MODEL // LANES
HI_MASK = 0xFFFF0000
VMEM_LIMIT = 56 * 1024 * 1024


def _params(sem, vmem=None):
    return pltpu.CompilerParams(dimension_semantics=sem, vmem_limit_bytes=vmem or VMEM_LIMIT)


def _rms(x, g):
    return x * lax.rsqrt(jnp.mean(x * x, axis=-1, keepdims=True) + EPS) * g


def _log_sigmoid(t):
    return jnp.minimum(t, 0.0) - jnp.log1p(jnp.exp(-jnp.abs(t)))


def _ada_kernel(c_ref, w_ref, b_ref, o_ref):
    c = c_ref[...]
    a = (c * jax.nn.sigmoid(c)).astype(BF16)
    o_ref[...] = jnp.dot(a, w_ref[...].astype(BF16), preferred_element_type=F32) + b_ref[...]


def _adaln(c_all, w_ada, b_ada):
    rows = c_all.shape[0]
    n = w_ada.shape[1]
    tn = 1024
    return pl.pallas_call(
        _ada_kernel,
        grid=(n // tn,),
        in_specs=[
            pl.BlockSpec((rows, D_MODEL), lambda j: (0, 0)),
            pl.BlockSpec((D_MODEL, tn), lambda j: (0, j)),
            pl.BlockSpec((1, tn), lambda j: (0, j)),
        ],
        out_specs=pl.BlockSpec((rows, tn), lambda j: (0, j)),
        out_shape=jax.ShapeDtypeStruct((rows, n), F32),
        compiler_params=_params(("arbitrary",)),
        name="adaln",
    )(c_all, w_ada, b_ada.reshape(1, n))


def _inproj_kernel(x_ref, g_ref, sc_ref, sh_ref, w_ref, bg_ref,
                   u_ref, v_ref, o_ref, gt_ref, qa_ref, ka_ref, va_ref):
    x = x_ref[0]
    h = _rms(x, g_ref[...]) * (1.0 + sc_ref[0]) + sh_ref[0]
    hb = h.astype(BF16)

    def seg(lo, width):
        return jnp.dot(hb, w_ref[:, lo:lo + width], preferred_element_type=F32)

    u_ref[0] = seg(C_U, 2 * D_M)
    v_ref[0] = seg(C_V, D_M)
    o_ref[0] = jax.nn.sigmoid(seg(C_O, D_M))
    qa_ref[0] = seg(C_QA, D_A) * (HD_A ** -0.5)
    ka_ref[0] = seg(C_KA, D_A)
    va_ref[0] = seg(C_VA, D_A)
    t = seg(C_G, LANES) + bg_ref[...]
    lane = lax.broadcasted_iota(jnp.int32, t.shape, 1)
    gt_ref[0] = jnp.where(lane < H_M, t, _log_sigmoid(t))


def _inproj(x3, g_pre, scale, shift, w_r, bias_g, tm):
    nb, rows, _ = x3.shape
    per_row_mod = scale.shape[1] != 1
    mod_block = (1, tm, D_MODEL) if per_row_mod else (1, 1, D_MODEL)
    mod_map = (lambda b, i: (b, i, 0)) if per_row_mod else (lambda b, i: (b, 0, 0))
    row_map = lambda b, i: (b, i, 0)
    widths = (2 * D_M, D_M, D_M, LANES, D_A, D_A, D_A)
    return pl.pallas_call(
        _inproj_kernel,
        grid=(nb, rows // tm),
        in_specs=[
            pl.BlockSpec((1, tm, D_MODEL), row_map),
            pl.BlockSpec((1, D_MODEL), lambda b, i: (0, 0)),
            pl.BlockSpec(mod_block, mod_map),
            pl.BlockSpec(mod_block, mod_map),
            pl.BlockSpec((D_MODEL, D_IN_R), lambda b, i: (0, 0)),
            pl.BlockSpec((1, LANES), lambda b, i: (0, 0)),
        ],
        out_specs=[pl.BlockSpec((1, tm, w), row_map) for w in widths],
        out_shape=[jax.ShapeDtypeStruct((nb, rows, w), F32) for w in widths],
        compiler_params=_params(("arbitrary", "arbitrary")),
        name="inproj",
    )(x3, g_pre.reshape(1, D_MODEL), scale, shift, w_r, bias_g)


def _mlstm_kernel(u_ref, v_ref, og_ref, gc_ref, gr_ref, tail0_ref, c0_ref, n0_ref, m0_ref,
                  cw_ref, cb_ref, mh_ref,
                  mem_ref, c_out_ref, n_out_ref, m_out_ref,
                  c_s, n_s, m_s, tail_s, uext_s, *, chunk):
    L = chunk
    step = pl.program_id(1)

    @pl.when(step == 0)
    def _():
        c_s[...] = c0_ref[0]
        n_s[...] = n0_ref[0]
        m_s[...] = m0_ref[0]
        tail_s[...] = tail0_ref[0]

    u = u_ref[0]
    uext_s[0:8, :] = tail_s[...]
    uext_s[8:8 + L, :] = u
    acc = cb_ref[...] + uext_s[pl.ds(5, L), :] * cw_ref[0:1, :]
    acc = acc + uext_s[pl.ds(6, L), :] * cw_ref[1:2, :]
    acc = acc + uext_s[pl.ds(7, L), :] * cw_ref[2:3, :]
    acc = acc + u * cw_ref[3:4, :]
    tail_s[...] = uext_s[pl.ds(L, 8), :]
    qk = acc * jax.nn.sigmoid(acc)

    gc = gc_ref[0]
    gr = gr_ref[0]
    row = lax.broadcasted_iota(jnp.int32, (L, L), 0)
    col = lax.broadcasted_iota(jnp.int32, (L, L), 1)
    causal = col <= row
    tri = causal.astype(F32)
    tri_t = (row <= col).astype(F32)
    lane = lax.broadcasted_iota(jnp.int32, gc.shape, 1)
    srow = lax.broadcasted_iota(jnp.int32, gr.shape, 0)
    b_col_all = jnp.dot(tri, jnp.where(lane >= H_M, gc, 0.0), preferred_element_type=F32,
                        precision=lax.Precision.HIGHEST)
    b_row_all = jnp.dot(jnp.where(srow >= H_M, gr, 0.0), tri_t, preferred_element_type=F32,
                        precision=lax.Precision.HIGHEST)

    v_all = v_ref[0]
    og = og_ref[0]
    for h in range(H_M):
        q = qk[:, h * DK_M:(h + 1) * DK_M]
        k = qk[:, D_M + h * DK_M:D_M + (h + 1) * DK_M] * (DK_M ** -0.5)
        v = v_all[:, h * DK_M:(h + 1) * DK_M]
        li_row = gr[h:h + 1, :]
        li_col = gc[:, h:h + 1]
        b_row = b_row_all[H_M + h:H_M + h + 1, :]
        b_col = b_col_all[:, H_M + h:H_M + h + 1]
        m0 = m_s[h:h + 1, 0:1]
        c0 = c_s[h]
        n0 = n_s[h:h + 1, :]

        d = jnp.where(causal, b_col - b_row + li_row, NEG_INF)
        g_state = b_col + m0
        m = jnp.maximum(jnp.max(d, axis=-1, keepdims=True), g_state)
        w_state = jnp.exp(g_state - m)
        qb = q.astype(BF16)
        kb = k.astype(BF16)
        vb = v.astype(BF16)
        s = lax.dot_general(qb, kb, (((1,), (1,)), ((), ())), preferred_element_type=F32)
        s = s * jnp.exp(d - m)
        num = jnp.dot(s.astype(BF16), vb, preferred_element_type=F32)
        num = num + w_state * jnp.dot(qb, c0.astype(BF16), preferred_element_type=F32)
        den = jnp.sum(s, axis=-1, keepdims=True) + w_state * jnp.sum(q * n0, axis=-1, keepdims=True)
        hh = num / jnp.maximum(jnp.abs(den), jnp.exp(-m))

        b_last = b_col[L - 1:L, :]
        g_tok_row = b_last - b_row + li_row
        m_new = jnp.maximum(b_last + m0, jnp.max(g_tok_row, axis=-1, keepdims=True))
        w_tok_col = jnp.exp(b_last - b_col + li_col - m_new)
        decay = jnp.exp(b_last + m0 - m_new)
        kw = k * w_tok_col
        c_s[h] = decay * c0 + jnp.dot(kw.T.astype(BF16), vb, preferred_element_type=F32)
        n_s[h:h + 1, :] = decay * n0 + jnp.sum(kw, axis=0, keepdims=True)
        m_s[h:h + 1, :] = jnp.broadcast_to(m_new, (1, LANES))

        hn = hh * lax.rsqrt(jnp.mean(hh * hh, axis=-1, keepdims=True) + EPS)
        hn = hn * mh_ref[:, h * DK_M:(h + 1) * DK_M]
        mem_ref[0, :, h * DK_M:(h + 1) * DK_M] = og[:, h * DK_M:(h + 1) * DK_M] * hn

    @pl.when(step == pl.num_programs(1) - 1)
    def _():
        c_out_ref[0] = c_s[...]
        n_out_ref[0] = n_s[...]
        m_out_ref[0] = m_s[...]


def _mlstm(u, v, og, gates_col, gates_row, tail0, c0, n0, m0x, conv_w, conv_b, mh_norm, chunk):
    nb, rows, _ = u.shape
    nc = rows // chunk
    row_map = lambda b, c: (b, c, 0)
    bat3 = lambda b, c: (b, 0, 0)
    const2 = lambda b, c: (0, 0)
    return pl.pallas_call(
        functools.partial(_mlstm_kernel, chunk=chunk),
        grid=(nb, nc),
        in_specs=[
            pl.BlockSpec((1, chunk, 2 * D_M), row_map),
            pl.BlockSpec((1, chunk, D_M), row_map),
            pl.BlockSpec((1, chunk, D_M), row_map),
            pl.BlockSpec((1, chunk, LANES), row_map),
            pl.BlockSpec((1, 8, chunk), lambda b, c: (b, 0, c)),
            pl.BlockSpec((1, 8, 2 * D_M), bat3),
            pl.BlockSpec((1, H_M, DK_M, DK_M), lambda b, c: (b, 0, 0, 0)),
            pl.BlockSpec((1, H_M, DK_M), bat3),
            pl.BlockSpec((1, H_M, LANES), bat3),
            pl.BlockSpec((CONV_W, 2 * D_M), const2),
            pl.BlockSpec((1, 2 * D_M), const2),
            pl.BlockSpec((1, D_M), const2),
        ],
        out_specs=[
            pl.BlockSpec((1, chunk, D_M), row_map),
            pl.BlockSpec((1, H_M, DK_M, DK_M), lambda b, c: (b, 0, 0, 0)),
            pl.BlockSpec((1, H_M, DK_M), bat3),
            pl.BlockSpec((1, H_M, LANES), bat3),
        ],
        out_shape=[
            jax.ShapeDtypeStruct((nb, rows, D_M), F32),
            jax.ShapeDtypeStruct((nb, H_M, DK_M, DK_M), F32),
            jax.ShapeDtypeStruct((nb, H_M, DK_M), F32),
            jax.ShapeDtypeStruct((nb, H_M, LANES), F32),
        ],
        scratch_shapes=[
            pltpu.VMEM((H_M, DK_M, DK_M), F32),
            pltpu.VMEM((H_M, DK_M), F32),
            pltpu.VMEM((H_M, LANES), F32),
            pltpu.VMEM((8, 2 * D_M), F32),
            pltpu.VMEM((chunk + 8, 2 * D_M), F32),
        ],
        compiler_params=_params(("arbitrary", "arbitrary")),
        name="mlstm",
    )(u, v, og, gates_col, gates_row, tail0, c0, n0, m0x,
      conv_w, conv_b.reshape(1, 2 * D_M), mh_norm.reshape(1, D_M))


def _attn_kernel(q_ref, kc_ref, vc_ref, kp_ref, vp_ref, bias_ref, o_ref, l_ref, k_s, v_s, *, ql):
    first = pl.program_id(2) == 0
    k_s[0:QB, :] = kp_ref[0].astype(BF16)
    k_s[QB:QB + ql, :] = kc_ref[0].astype(BF16)
    v_s[0:QB, :] = vp_ref[0].astype(BF16)
    v_s[QB:QB + ql, :] = vc_ref[0].astype(BF16)
    col = lax.broadcasted_iota(jnp.int32, (QB, 2 * QB), 1)
    pad_keys = jnp.logical_and(first, col < QB)
    for h in range(H_A):
        lo = h * HD_A
        bias = bias_ref[h]
        for j in range(ql // QB):
            qb = q_ref[0, j * QB:(j + 1) * QB, lo:lo + HD_A].astype(BF16)
            kb = k_s[j * QB:(j + 2) * QB, lo:lo + HD_A]
            vb = v_s[j * QB:(j + 2) * QB, lo:lo + HD_A]
            logits = lax.dot_general(qb, kb, (((1,), (1,)), ((), ())), preferred_element_type=F32)
            logits = logits + bias
            if j == 0:
                logits = jnp.where(pad_keys, NEG_INF, logits)
            m = jnp.max(logits, axis=-1, keepdims=True)
            p = jnp.exp(logits - m)
            s = jnp.sum(p, axis=-1, keepdims=True)
            o = jnp.dot(p.astype(BF16), vb, preferred_element_type=F32) / s
            o_ref[0, j * QB:(j + 1) * QB, lo:lo + HD_A] = o
            l_ref[0, j * QB:(j + 1) * QB, lo:lo + HD_A] = jnp.broadcast_to(m + jnp.log(s), (QB, HD_A))


def _attn_branch(qa, ka, va, bias, dil):
    nb, seq, _ = qa.shape
    ld = seq // dil
    ql = min(ATT_QL, ld)
    view = lambda t: t.reshape(nb, ld, dil * D_A)
    cur = lambda b, r, i: (b, i, r)
    prev = lambda b, r, i: (b, jnp.maximum(i * (ql // QB) - 1, 0), r)
    o, lse = pl.pallas_call(
        functools.partial(_attn_kernel, ql=ql),
        grid=(nb, dil, ld // ql),
        in_specs=[
            pl.BlockSpec((1, ql, D_A), cur),
            pl.BlockSpec((1, ql, D_A), cur),
            pl.BlockSpec((1, ql, D_A), cur),
            pl.BlockSpec((1, QB, D_A), prev),
            pl.BlockSpec((1, QB, D_A), prev),
            pl.BlockSpec((H_A, QB, 2 * QB), lambda b, r, i: (0, 0, 0)),
        ],
        out_specs=[pl.BlockSpec((1, ql, D_A), cur), pl.BlockSpec((1, ql, D_A), cur)],
        out_shape=[jax.ShapeDtypeStruct((nb, ld, dil * D_A), F32)] * 2,
        scratch_shapes=[pltpu.VMEM((QB + ql, D_A), BF16), pltpu.VMEM((QB + ql, D_A), BF16)],
        compiler_params=_params(("arbitrary", "arbitrary", "arbitrary")),
        name=f"attn_d{dil}",
    )(view(qa), view(ka), view(va), view(ka), view(va), bias)
    return o.reshape(nb, seq, D_A), lse.reshape(nb, seq, D_A)


def _heads_kernel(k_ref, v_ref, ko_ref, vo_ref):
    for src, dst in ((k_ref, ko_ref), (v_ref, vo_ref)):
        for c in range(D_A // LANES):
            xt = src[0, :, c * LANES:(c + 1) * LANES].T
            dst[0, 2 * c] = xt[:HD_A]
            dst[0, 2 * c + 1] = xt[HD_A:]


def _window_heads(ka, va, keep, tm=512):
    nb, seq, _ = ka.shape
    first = (seq - keep) // tm
    in_spec = pl.BlockSpec((1, tm, D_A), lambda b, i: (b, first + i, 0))
    out_spec = pl.BlockSpec((1, H_A, HD_A, tm), lambda b, i: (b, 0, 0, i))
    kt, vt = pl.pallas_call(
        _heads_kernel,
        grid=(nb, keep // tm),
        in_specs=[in_spec, in_spec],
        out_specs=[out_spec, out_spec],
        out_shape=[jax.ShapeDtypeStruct((nb, H_A, HD_A, keep), F32)] * 2,
        compiler_params=_params(("arbitrary", "arbitrary")),
        name="window_heads",
    )(ka, va)
    return kt.transpose(0, 3, 1, 2), vt.transpose(0, 3, 1, 2)


def _t5_bucket(dist):
    max_exact = N_BUCKETS // 2
    n = jnp.maximum(dist, 1).astype(F32)
    large = max_exact + (jnp.log(n / max_exact) / math.log(MAX_DISTANCE / max_exact)
                         * (N_BUCKETS - max_exact)).astype(jnp.int32)
    large = jnp.minimum(large, N_BUCKETS - 1)
    return jnp.where(dist < max_exact, dist, large)


def _branch_bias_vec(rel_bias, window, dil):
    return rel_bias[_t5_bucket(dil * jnp.arange(window // dil + 1))].astype(F32)


def _prompt_bias(rel_bias, window, dil):
    assert window // dil == QB
    n = 2 * QB
    vec = _branch_bias_vec(rel_bias, window, dil)
    vecp = jnp.concatenate([vec[::-1], jnp.full((n - QB - 1, H_A), NEG_INF, F32)], axis=0)
    g = jnp.roll(vecp, QB - 1, axis=0)
    skew = jnp.tile(g, (QB + 1, 1))[:QB * (n + 1)].reshape(QB, n + 1, H_A)[:, :n]
    return skew[::-1].transpose(2, 0, 1)


def _sattn_kernel(q_ref, kn_ref, vn_ref, k1_ref, v1_ref, k4_ref, v4_ref, k16_ref, v16_ref,
                  b1_ref, b4_ref, b16_ref, bn_ref, att_ref, *, n_tok):
    kn = kn_ref[0]
    vn = vn_ref[0]
    for t in range(n_tok):
        q = q_ref[0, t]
        parts = []
        for bi, (_, dil) in enumerate(BRANCHES):
            if dil == 1:
                kc, vc, bc = k1_ref[0], v1_ref[0], b1_ref[t]
                k_new, v_new, b_new = kn, vn, bn_ref[t]
            else:
                k_ref, v_ref, b_ref = ((k4_ref, v4_ref, b4_ref) if dil == 4 else (k16_ref, v16_ref, b16_ref))
                kc, vc, bc = k_ref[0, :, t], v_ref[0, :, t], b_ref[...]
                k_new, v_new, b_new = kn[t:t + 1], vn[t:t + 1], bn_ref[0, 0:1]
            lc = jnp.sum(kc * q[None], axis=-1, keepdims=True) + bc
            ln = jnp.sum(k_new * q[None], axis=-1, keepdims=True) + b_new
            m = jnp.maximum(jnp.max(lc, axis=0, keepdims=True), jnp.max(ln, axis=0, keepdims=True))
            pc = jnp.exp(lc - m)
            pn = jnp.exp(ln - m)
            s = jnp.sum(pc, axis=0, keepdims=True) + jnp.sum(pn, axis=0, keepdims=True)
            o = jnp.sum(pc * vc, axis=0, keepdims=True) + jnp.sum(pn * v_new, axis=0, keepdims=True)
            parts.append((o / s, m + jnp.log(s)))
        lmax = functools.reduce(jnp.maximum, [l for _, l in parts])
        num = 0.0
        den = 0.0
        for o, l in parts:
            w = jnp.exp(l - lmax)
            num = num + w * o
            den = den + w
        att_ref[0, t] = (num / den)[0]


def _sample_attention(q4, kn4, vn4, cache_k, cache_v, biases, n_tok):
    nb, n_past = cache_k.shape[0], cache_k.shape[1]
    assert all(n_past % (dil * QB) == 0 and n_tok <= max(dil, n_tok) for _, dil in BRANCHES)
    tok_spec = pl.BlockSpec((1, n_tok, H_A, HD_A), lambda b: (b, 0, 0, 0))
    args, specs = [], []
    for _, dil in BRANCHES:
        last = n_past // dil // QB - 1
        for c in (cache_k, cache_v):
            if dil == 1:
                args.append(c)
                specs.append(pl.BlockSpec((1, QB, H_A, HD_A), functools.partial(lambda b, last: (b, last, 0, 0), last=last)))
            else:
                assert n_tok <= dil
                args.append(c.reshape(nb, n_past // dil, dil, H_A, HD_A))
                specs.append(pl.BlockSpec((1, QB, n_tok, H_A, HD_A),
                                          functools.partial(lambda b, last: (b, last, 0, 0, 0), last=last)))
    b1, b4, b16, bn = biases
    full = lambda a: pl.BlockSpec(a.shape, functools.partial(lambda b, nd: (0,) * nd, nd=a.ndim))
    return pl.pallas_call(
        functools.partial(_sattn_kernel, n_tok=n_tok),
        grid=(nb,),
        in_specs=[tok_spec] * 3 + specs + [full(b1), full(b4), full(b16), full(bn)],
        out_specs=tok_spec,
        out_shape=jax.ShapeDtypeStruct((nb, n_tok, H_A, HD_A), F32),
        compiler_params=_params(("arbitrary",)),
        name="sample_attn",
    )(q4, kn4, vn4, *args, b1, b4, b16, bn)


def _sample_bias(rel_bias, n_tok):
    expand = lambda x: jnp.broadcast_to(x[..., None], x.shape + (HD_A,))
    i = jnp.arange(QB)
    out = []
    for window, dil in BRANCHES:
        nk = window // dil
        assert nk == QB
        vec = _branch_bias_vec(rel_bias, window, dil)
        if dil == 1:
            tabs = []
            for t in range(n_tok):
                m = nk + t - i
                tabs.append(jnp.where((m <= nk)[:, None], vec[jnp.clip(m, 0, nk)], NEG_INF))
            out.append(expand(jnp.stack(tabs)))
        else:
            out.append(expand(vec[nk - i]))
    vec1 = _branch_bias_vec(rel_bias, *BRANCHES[0])
    j = jnp.arange(n_tok)
    new = []
    for t in range(n_tok):
        m = t - j
        new.append(jnp.where((m >= 0)[:, None], vec1[jnp.clip(m, 0, None)], NEG_INF))
    out.append(expand(jnp.stack(new)))
    return out


def _postmix_kernel(mem_ref, o1_ref, l1_ref, o2_ref, l2_ref, o3_ref, l3_ref, x_ref,
                    g1_ref, sc2_ref, sh2_ref, gpost_ref, gpre_ref, wo_ref, wr_ref, br_ref,
                    x1_ref, h2_ref, idx_ref, gate_ref):
    l1, l2, l3 = l1_ref[0], l2_ref[0], l3_ref[0]
    lmax = jnp.maximum(jnp.maximum(l1, l2), l3)
    w1 = jnp.exp(l1 - lmax)
    w2 = jnp.exp(l2 - lmax)
    w3 = jnp.exp(l3 - lmax)
    att = (w1 * o1_ref[0] + w2 * o2_ref[0] + w3 * o3_ref[0]) / (w1 + w2 + w3)
    y = jnp.dot(mem_ref[0].astype(BF16), wo_ref[0:D_M, :], preferred_element_type=F32)
    y = y + jnp.dot(att.astype(BF16), wo_ref[D_M:, :], preferred_element_type=F32)
    x1 = x_ref[0] + g1_ref[0] * _rms(y, gpost_ref[...])
    x1_ref[0] = x1
    h2 = _rms(x1, gpre_ref[...]) * (1.0 + sc2_ref[0]) + sh2_ref[0]
    h2_ref[0] = h2
    h_hi = h2.astype(BF16)
    h_lo = (h2 - h_hi.astype(F32)).astype(BF16)
    w_hi = wr_ref[0]
    logits = jnp.dot(h_hi, w_hi, preferred_element_type=F32)
    logits = logits + jnp.dot(h_lo, w_hi, preferred_element_type=F32)
    logits = logits + jnp.dot(h_hi, wr_ref[1], preferred_element_type=F32) + br_ref[...]
    lane = lax.broadcasted_iota(jnp.int32, logits.shape, 1)
    lane_f = lane.astype(F32)
    logits = jnp.where(lane < N_EXPERTS, logits, NEG_INF)
    vals, idxs = [], []
    for _ in range(TOP_K):
        mx = jnp.max(logits, axis=-1, keepdims=True)
        ix = jnp.min(jnp.where(logits == mx, lane_f, float(LANES)), axis=-1, keepdims=True)
        vals.append(mx)
        idxs.append(ix)
        logits = jnp.where(lane_f == ix, NEG_INF, logits)
    es = [jnp.exp(v - vals[0]) for v in vals]
    tot = es[0] + es[1] + es[2] + es[3]
    idx_tile = jnp.zeros(logits.shape, F32)
    gate_tile = jnp.zeros(logits.shape, F32)
    for k in range(TOP_K):
        idx_tile = jnp.where(lane == k, idxs[k], idx_tile)
        gate_tile = jnp.where(lane == k, es[k] / tot, gate_tile)
    idx_ref[0] = idx_tile.astype(jnp.int32)
    gate_ref[0] = gate_tile


def _postmix(mem, branches, x3, gate1, scale2, shift2, g_post, g_pre, w_out_b, w_router_p, b_router_p, tm):
    nb, rows, _ = x3.shape
    per_row_mod = gate1.shape[1] != 1
    mod_block = (1, tm, D_MODEL) if per_row_mod else (1, 1, D_MODEL)
    mod_map = (lambda b, i: (b, i, 0)) if per_row_mod else (lambda b, i: (b, 0, 0))
    row_map = lambda b, i: (b, i, 0)
    const2 = lambda b, i: (0, 0)
    half = pl.BlockSpec((1, tm, D_M), row_map)
    flat = [a for pair in branches for a in pair]
    return pl.pallas_call(
        _postmix_kernel,
        grid=(nb, rows // tm),
        in_specs=[half] * 7 + [
            pl.BlockSpec((1, tm, D_MODEL), row_map),
            pl.BlockSpec(mod_block, mod_map),
            pl.BlockSpec(mod_block, mod_map),
            pl.BlockSpec(mod_block, mod_map),
            pl.BlockSpec((1, D_MODEL), const2),
            pl.BlockSpec((1, D_MODEL), const2),
            pl.BlockSpec((D_MODEL, D_MODEL), const2),
            pl.BlockSpec((2, D_MODEL, LANES), lambda b, i: (0, 0, 0)),
            pl.BlockSpec((1, LANES), const2),
        ],
        out_specs=[
            pl.BlockSpec((1, tm, D_MODEL), row_map),
            pl.BlockSpec((1, tm, D_MODEL), row_map),
            pl.BlockSpec((1, tm, LANES), row_map),
            pl.BlockSpec((1, tm, LANES), row_map),
        ],
        out_shape=[
            jax.ShapeDtypeStruct((nb, rows, D_MODEL), F32),
            jax.ShapeDtypeStruct((nb, rows, D_MODEL), F32),
            jax.ShapeDtypeStruct((nb, rows, LANES), jnp.int32),
            jax.ShapeDtypeStruct((nb, rows, LANES), F32),
        ],
        compiler_params=_params(("arbitrary", "arbitrary")),
        name="postmix",
    )(mem, *flat, x3, gate1, scale2, shift2, g_post.reshape(1, D_MODEL), g_pre.reshape(1, D_MODEL),
      w_out_b, w_router_p, b_router_p)


def _row_wait(src_rows, dst_rows, sem):
    pltpu.make_async_copy(src_rows, dst_rows, sem).wait()


def _scatter_kernel(slot_ref, h_ref, xs_hbm, buf, sem):
    bits = pltpu.bitcast(h_ref[...].astype(BF16).astype(F32), U32)
    packed = (bits[:, :PACK_W] & U32(HI_MASK)) | (bits[:, PACK_W:] >> 16)
    buf[...] = pltpu.einshape("m(jc)->mjc", packed, j=PACK_T)

    def issue(r, carry):
        src = buf.at[pl.ds(r, 1)]
        for k in range(TOP_K):
            pltpu.make_async_copy(src, xs_hbm.at[pl.ds(slot_ref[0, 0, r * TOP_K + k], 1)],
                                  sem).start(priority=k % 2)
        return carry

    lax.fori_loop(0, TOK_TILE, issue, 0, unroll=4)
    for _ in range(TOP_K):
        _row_wait(buf, xs_hbm.at[pl.ds(0, TOK_TILE)], sem)


def _moe_scatter(h_all, slot_tiles):
    tiles = slot_tiles.shape[0]
    n_rows = tiles * TOK_TILE * TOP_K
    return pl.pallas_call(
        _scatter_kernel,
        grid=(tiles,),
        in_specs=[
            pl.BlockSpec((1, 1, TOK_TILE * TOP_K), lambda i: (i, 0, 0), memory_space=pltpu.SMEM),
            pl.BlockSpec((TOK_TILE, D_MODEL), lambda i: (i, 0)),
        ],
        out_specs=pl.BlockSpec(memory_space=pl.ANY),
        out_shape=jax.ShapeDtypeStruct((n_rows, PACK_T, LANES), U32),
        scratch_shapes=[pltpu.VMEM((TOK_TILE, PACK_T, LANES), U32), pltpu.SemaphoreType.DMA],
        compiler_params=_params(("arbitrary",)),
        name="moe_scatter",
    )(slot_tiles, h_all)


def _expert_kernel(tile_ref, exp_ref, lo_ref, hi_ref, first_ref, nvis_ref,
                   xs_ref, w1_ref, b1_ref, w2_ref, b2_ref, ys_ref, w1b, w2b):
    v = pl.program_id(0)
    active = v < nvis_ref[0]
    changed = jnp.logical_or(v == 0, exp_ref[v] != exp_ref[jnp.maximum(v - 1, 0)])
    lo = lo_ref[v]
    hi = hi_ref[v]

    @pl.when(jnp.logical_and(active, changed))
    def _():
        w1b[...] = w1_ref[0].astype(BF16)
        w2b[...] = w2_ref[0].astype(BF16)

    @pl.when(jnp.logical_and(active, first_ref[v] == 1))
    def _():
        ys_ref[...] = jnp.zeros(ys_ref.shape, F32)

    for r in range(MOE_M // MOE_SUB):
        r0 = r * MOE_SUB

        @pl.when(jnp.logical_and(active, jnp.logical_and(lo < r0 + MOE_SUB, hi > r0)))
        def _():
            w = pltpu.einshape("mjc->m(jc)", xs_ref[r0:r0 + MOE_SUB])
            x_hi = pltpu.bitcast(w & U32(HI_MASK), F32).astype(BF16)
            x_lo = pltpu.bitcast(w << 16, F32).astype(BF16)
            hgu = jnp.dot(x_hi, w1b[0:PACK_W, :], preferred_element_type=F32)
            hgu = hgu + jnp.dot(x_lo, w1b[PACK_W:, :], preferred_element_type=F32) + b1_ref[0]
            x_glu = jnp.minimum(hgu[:, :D_FF], SWIGLU_LIMIT)
            x_lin = jnp.clip(hgu[:, D_FF:], -SWIGLU_LIMIT, SWIGLU_LIMIT)
            act = x_glu * jax.nn.sigmoid(SWIGLU_ALPHA * x_glu) * (x_lin + 1.0)
            y = jnp.dot(act.astype(BF16), w2b[...], preferred_element_type=F32) + b2_ref[0]
            rows = r0 + lax.broadcasted_iota(jnp.int32, (MOE_SUB, 1, 1), 0)
            mine = jnp.logical_and(rows >= lo, rows < hi)
            y_rows = pltpu.einshape("m(jc)->mjc", y, j=ROW_T)
            ys_ref[r0:r0 + MOE_SUB] = jnp.where(mine, y_rows, ys_ref[r0:r0 + MOE_SUB])


def _moe_experts(xs, visits, w1, b1, w2, b2):
    n_rows = xs.shape[0]
    n_vis = visits[0].shape[0]
    tile_map = lambda v, tile, *_: (tile[v], 0, 0)
    exp_map = lambda v, tile, exp, *_: (exp[v], 0, 0)
    return pl.pallas_call(
        _expert_kernel,
        grid_spec=pltpu.PrefetchScalarGridSpec(
            num_scalar_prefetch=6,
            grid=(n_vis,),
            in_specs=[
                pl.BlockSpec((MOE_M, PACK_T, LANES), tile_map),
                pl.BlockSpec((1, D_MODEL, 2 * D_FF), exp_map),
                pl.BlockSpec((1, 1, 2 * D_FF), exp_map),
                pl.BlockSpec((1, D_FF, D_MODEL), exp_map),
                pl.BlockSpec((1, 1, D_MODEL), exp_map),
            ],
            out_specs=pl.BlockSpec((MOE_M, ROW_T, LANES), tile_map),
            scratch_shapes=[pltpu.VMEM((D_MODEL, 2 * D_FF), BF16), pltpu.VMEM((D_FF, D_MODEL), BF16)],
        ),
        out_shape=jax.ShapeDtypeStruct((n_rows, ROW_T, LANES), F32),
        compiler_params=_params(("arbitrary",)),
        name="moe_experts",
    )(*visits, xs, w1, b1.reshape(N_EXPERTS, 1, 2 * D_FF), w2, b2.reshape(N_EXPERTS, 1, D_MODEL))


def _combine_kernel(slot_ref, gate_ref, x1p_ref, x1s_ref, g2p_ref, g2s_ref, gpost_ref, ys_hbm,
                    yp_ref, ysm_ref, buf, sem, *, n_prompt_tiles):
    i = pl.program_id(0)

    def issue(r, carry):
        for k in range(TOP_K):
            pltpu.make_async_copy(ys_hbm.at[pl.ds(slot_ref[0, 0, r * TOP_K + k], 1)],
                                  buf.at[pl.ds(k * TOK_TILE + r, 1)], sem).start(priority=k % 2)
        return carry

    lax.fori_loop(0, TOK_TILE, issue, 0, unroll=4)
    for k in range(TOP_K):
        _row_wait(ys_hbm.at[pl.ds(0, TOK_TILE)], buf.at[pl.ds(k * TOK_TILE, TOK_TILE)], sem)
    g = gate_ref[...]
    f = jnp.zeros((TOK_TILE, D_MODEL), F32)
    for k in range(TOP_K):
        rows = pltpu.einshape("mjc->m(jc)", buf[k * TOK_TILE:(k + 1) * TOK_TILE])
        f = f + rows * g[:, k:k + 1]
    is_sample = i >= n_prompt_tiles
    x1 = jnp.where(is_sample, x1s_ref[...], x1p_ref[...])
    g2 = jnp.where(is_sample, g2s_ref[...], g2p_ref[0])
    y = x1 + g2 * _rms(f, gpost_ref[...])

    @pl.when(jnp.logical_not(is_sample))
    def _():
        yp_ref[...] = y

    @pl.when(is_sample)
    def _():
        ysm_ref[...] = y


def _moe_combine(ys, slot_tiles, gates, x1_p, x1_s, gate2_p, gate2_s, g_post, rows_per_batch):
    tiles = slot_tiles.shape[0]
    n_p = x1_p.shape[0] // TOK_TILE
    assert tiles == n_p + 1 and x1_s.shape[0] == TOK_TILE
    tiles_per_batch = rows_per_batch // TOK_TILE
    p_tile = lambda i: (jnp.minimum(i, n_p - 1), 0)
    const2 = lambda i: (0, 0)
    return pl.pallas_call(
        functools.partial(_combine_kernel, n_prompt_tiles=n_p),
        grid=(tiles,),
        in_specs=[
            pl.BlockSpec((1, 1, TOK_TILE * TOP_K), lambda i: (i, 0, 0), memory_space=pltpu.SMEM),
            pl.BlockSpec((TOK_TILE, LANES), lambda i: (i, 0)),
            pl.BlockSpec((TOK_TILE, D_MODEL), p_tile),
            pl.BlockSpec((TOK_TILE, D_MODEL), const2),
            pl.BlockSpec((1, 1, D_MODEL), lambda i: (jnp.minimum(i, n_p - 1) // tiles_per_batch, 0, 0)),
            pl.BlockSpec((TOK_TILE, D_MODEL), const2),
            pl.BlockSpec((1, D_MODEL), const2),
            pl.BlockSpec(memory_space=pl.ANY),
        ],
        out_specs=[pl.BlockSpec((TOK_TILE, D_MODEL), p_tile), pl.BlockSpec((TOK_TILE, D_MODEL), const2)],
        out_shape=[jax.ShapeDtypeStruct(x1_p.shape, F32), jax.ShapeDtypeStruct(x1_s.shape, F32)],
        scratch_shapes=[pltpu.VMEM((TOP_K * TOK_TILE, ROW_T, LANES), F32), pltpu.SemaphoreType.DMA],
        compiler_params=_params(("arbitrary",)),
        name="moe_combine",
    )(slot_tiles, gates, x1_p, x1_s, gate2_p, gate2_s, g_post.reshape(1, D_MODEL), ys)


def _moe_plan(top_idx):
    a = top_idx.shape[0] * TOP_K
    assert a % MOE_M == 0
    flat_e = top_idx[:, :TOP_K].reshape(a)
    onehot = (flat_e[:, None] == jnp.arange(N_EXPERTS)[None, :]).astype(jnp.int32)
    csum = jnp.cumsum(onehot, axis=0)
    counts = csum[-1]
    g_end = jnp.cumsum(counts)
    g_start = g_end - counts
    rank = jnp.sum(csum * onehot, axis=1) - 1
    slot_of = (jnp.sum(g_start[None, :] * onehot, axis=1) + rank).astype(jnp.int32)

    n_tiles = a // MOE_M
    n_vis = n_tiles + N_EXPERTS - 1
    first_tile = g_start // MOE_M
    last_tile = jnp.where(counts > 0, (g_end - 1) // MOE_M, first_tile)
    nvis_e = jnp.where(counts > 0, last_tile - first_tile + 1, 0)
    v_end = jnp.cumsum(nvis_e)
    v_start = v_end - nvis_e
    total = v_end[-1]
    v = jnp.arange(n_vis)
    vc = jnp.minimum(v, total - 1)
    e_of = jnp.minimum(jnp.sum((vc[:, None] >= v_end[None, :]).astype(jnp.int32), axis=1), N_EXPERTS - 1)
    tile_of = first_tile[e_of] + (vc - v_start[e_of])
    lo = jnp.maximum(g_start[e_of], tile_of * MOE_M) - tile_of * MOE_M
    hi = jnp.minimum(g_end[e_of], (tile_of + 1) * MOE_M) - tile_of * MOE_M
    prev_tile = jnp.concatenate([jnp.full((1,), -1, tile_of.dtype), tile_of[:-1]])
    first = (tile_of != prev_tile).astype(jnp.int32)
    i32 = lambda t: t.astype(jnp.int32)
    visits = (i32(tile_of), i32(e_of), i32(lo), i32(hi), first, i32(total).reshape(1))
    return slot_of, visits


def _reorder_w_in(w_in):
    off_om_end = 4 * D_M
    off_g_end = off_om_end + 2 * H_M
    w = jnp.concatenate([
        w_in[:, :off_om_end],
        w_in[:, off_g_end:],
        w_in[:, off_om_end:off_g_end],
        jnp.zeros((D_MODEL, LANES - 2 * H_M), w_in.dtype),
    ], axis=1)
    return w.astype(BF16)


def kernel(x_prompt, x_sample, cache_k, cache_v, state_conv, state_C, state_n, state_m, c_prompt, c_sample, w_ada, b_ada, g_pre_mix, g_post_mix, g_pre_ffn, g_post_ffn, w_in, b_ig, b_fg, conv_w, conv_b, mh_norm, rel_bias, w_out, w_router, b_router, w1, b1, w2, b2):
    depth = w_in.shape[0]
    assert depth == 1
    l = 0
    nb_p, seq, _ = x_prompt.shape
    nb_s, n_tok, _ = x_sample.shape
    rows_s = nb_s * n_tok
    assert rows_s == TOK_TILE

    n_c = nb_p + nb_s
    c_pad = -(-n_c // 8) * 8
    c_all = jnp.concatenate([c_prompt, c_sample, jnp.zeros((c_pad - n_c, D_MODEL), F32)], axis=0)
    mod = _adaln(c_all, w_ada[l], b_ada[l])
    mods = [mod[:, i * D_MODEL:(i + 1) * D_MODEL] for i in range(6)]
    mods_p = [m[:nb_p].reshape(nb_p, 1, D_MODEL) for m in mods]
    mods_s = [jnp.repeat(m[nb_p:n_c], n_tok, axis=0).reshape(1, rows_s, D_MODEL) for m in mods]

    w_r = _reorder_w_in(w_in[l])
    bias_g = jnp.concatenate([b_ig[l], b_fg[l], jnp.zeros((LANES - 2 * H_M,), F32)]).reshape(1, LANES)
    w_out_b = w_out[l].astype(BF16)
    w_router_f = jnp.concatenate([w_router[l], jnp.zeros((D_MODEL, LANES - N_EXPERTS), F32)], axis=1)
    w_router_hi = w_router_f.astype(BF16)
    w_router_p = jnp.stack([w_router_hi, (w_router_f - w_router_hi.astype(F32)).astype(BF16)])
    b_router_p = jnp.concatenate([b_router[l].astype(F32), jnp.zeros((LANES - N_EXPERTS,), F32)]).reshape(1, LANES)

    shift1, scale1, gate1, shift2, scale2, gate2_p = mods_p
    u, v_m, og, gts, qa, ka, va = _inproj(x_prompt, g_pre_mix[l], scale1, shift1, w_r, bias_g, tm=512)
    gates_row = gts[:, :, :8].transpose(0, 2, 1)
    zeros = lambda *s: jnp.zeros(s, F32)
    mem, c_p, n_p, m_p = _mlstm(u, v_m, og, gts, gates_row, zeros(nb_p, 8, 2 * D_M),
                                zeros(nb_p, H_M, DK_M, DK_M), zeros(nb_p, H_M, DK_M), zeros(nb_p, H_M, LANES),
                                conv_w[l], conv_b[l], mh_norm[l], MLSTM_L_PROMPT)
    branches = [_attn_branch(qa, ka, va, _prompt_bias(rel_bias, w, d), d) for (w, d) in BRANCHES]
    x1_p, h2_p, idx_p, gate_p = _postmix(mem, branches, x_prompt, gate1, scale2, shift2,
                                         g_post_mix[l], g_pre_ffn[l], w_out_b, w_router_p, b_router_p, tm=256)
    keep = min(WINDOW_MAX, seq)
    k_p, v_p = _window_heads(ka, va, keep)
    conv_p = u[:, seq - (CONV_W - 1):]

    shift1, scale1, gate1, shift2, scale2, gate2_s = mods_s
    x_s = x_sample.reshape(1, rows_s, D_MODEL)
    u_s, v_s, og_s, gts_s, qa_s, ka_s, va_s = _inproj(x_s, g_pre_mix[l], scale1, shift1, w_r, bias_g, tm=rows_s)
    per_b = lambda t: t.reshape(nb_s, n_tok, t.shape[-1])
    ls = MLSTM_L_SAMPLE
    pad_rows = lambda t, n: jnp.concatenate([t, jnp.zeros((nb_s, n - t.shape[1], t.shape[2]), t.dtype)], axis=1)
    inert = jnp.concatenate([jnp.full((H_M,), NEG_INF, F32), jnp.zeros((LANES - H_M,), F32)])
    gts_pad = jnp.concatenate([per_b(gts_s), jnp.broadcast_to(inert, (nb_s, ls - n_tok, LANES))], axis=1)
    tail0 = jnp.concatenate([zeros(nb_s, 8 - (CONV_W - 1), 2 * D_M), state_conv[l].astype(F32)], axis=1)
    m0x = jnp.broadcast_to(state_m[l].astype(F32)[:, :, None], (nb_s, H_M, LANES))
    mem_s, c_s, n_s, m_s = _mlstm(pad_rows(per_b(u_s), ls), pad_rows(per_b(v_s), ls), pad_rows(per_b(og_s), ls),
                                  gts_pad, gts_pad[:, :, :8].transpose(0, 2, 1), tail0,
                                  state_C[l].astype(F32), state_n[l].astype(F32), m0x,
                                  conv_w[l], conv_b[l], mh_norm[l], ls)
    mem_s = mem_s[:, :n_tok].reshape(1, rows_s, D_M)
    heads = lambda t: t.reshape(nb_s, n_tok, H_A, HD_A)
    k_new, v_new = heads(ka_s), heads(va_s)
    att_s = _sample_attention(heads(qa_s), k_new, v_new, cache_k[l], cache_v[l],
                              _sample_bias(rel_bias, n_tok), n_tok)
    att_s = att_s.reshape(1, rows_s, D_A)
    zero_o = zeros(1, rows_s, D_A)
    ninf = jnp.full((1, rows_s, D_A), NEG_INF, F32)
    branches_s = [(att_s, zero_o), (zero_o, ninf), (zero_o, ninf)]
    x1_s, h2_s, idx_s, gate_s = _postmix(mem_s, branches_s, x_s, gate1, scale2, shift2,
                                         g_post_mix[l], g_pre_ffn[l], w_out_b, w_router_p, b_router_p, tm=rows_s)

    t_p = nb_p * seq
    h_all = jnp.concatenate([h2_p.reshape(t_p, D_MODEL), h2_s.reshape(rows_s, D_MODEL)], axis=0)
    idx_all = jnp.concatenate([idx_p.reshape(t_p, LANES), idx_s.reshape(rows_s, LANES)], axis=0)
    gate_all = jnp.concatenate([gate_p.reshape(t_p, LANES), gate_s.reshape(rows_s, LANES)], axis=0)
    slot_of, visits = _moe_plan(idx_all)
    slot_tiles = slot_of.reshape((t_p + rows_s) // TOK_TILE, 1, TOK_TILE * TOP_K)
    xs = _moe_scatter(h_all, slot_tiles)
    ys = _moe_experts(xs, visits, w1[l], b1[l], w2[l], b2[l])
    y_p, y_s = _moe_combine(ys, slot_tiles, gate_all, x1_p.reshape(t_p, D_MODEL), x1_s.reshape(rows_s, D_MODEL),
                            gate2_p, gate2_s.reshape(rows_s, D_MODEL), g_post_ffn[l], seq)
    y_prompt = y_p.reshape(nb_p, seq, D_MODEL)
    y_sample = y_s.reshape(nb_s, n_tok, D_MODEL)

    st = lambda t: t[None]
    conv_s = jnp.concatenate([state_conv[l].astype(F32), per_b(u_s)], axis=1)[:, -(CONV_W - 1):]
    return (y_prompt, y_sample, st(k_p), st(v_p), st(conv_p), st(c_p), st(n_p), st(m_p[:, :, 0]),
            st(k_new), st(v_new), st(conv_s), st(c_s), st(n_s), st(m_s[:, :, 0]))
```

```python
import functools
import math

import jax
import jax.numpy as jnp
import numpy as np
from jax import lax
from jax.experimental import pallas as pl
from jax.experimental.pallas import tpu as pltpu

F32 = jnp.float32
BF16 = jnp.bfloat16
U32 = jnp.uint32
NEG_INF = float("-inf")

D_MODEL = 1024
H_M = 4
D_M = 512
DK_M = 128
CONV_W = 4
H_A = 8
D_A = 512
HD_A = 64
BRANCHES = ((128, 1), (512, 4), (2048, 16))
WINDOW_MAX = 2048
QB = 128
N_BUCKETS = 32
MAX_DISTANCE = 2048
N_EXPERTS = 32
TOP_K = 4
D_FF = 1024
SWIGLU_LIMIT = 7.0
SWIGLU_ALPHA = 1.702
EPS = 1e-6

LANES = 128
C_U, C_V, C_O, C_QA, C_KA, C_VA, C_G = 0, 1024, 1536, 2048, 2560, 3072, 3584
D_IN_R = C_G + LANES

MLSTM_L_PROMPT = 256
MLSTM_L_SAMPLE = 128
ATT_QL = 256
MOE_M = 512
MOE_SUB = 256
TOK_TILE = 128
PACK_W = D_MODEL // 2
PACK_T = PACK_W // LANES
ROW_T = D_MODEL // LANES
HI_MASK = 0xFFFF0000
VMEM_LIMIT = 56 * 1024 * 1024


def _params(sem, vmem=None):
    return pltpu.CompilerParams(dimension_semantics=sem, vmem_limit_bytes=vmem or VMEM_LIMIT)


def _rms(x, g):
    return x * lax.rsqrt(jnp.mean(x * x, axis=-1, keepdims=True) + EPS) * g


def _log_sigmoid(t):
    return jnp.minimum(t, 0.0) - jnp.log1p(jnp.exp(-jnp.abs(t)))


def _ada_kernel(c_ref, w_ref, b_ref, o_ref):
    c = c_ref[...]
    a = (c * jax.nn.sigmoid(c)).astype(BF16)
    o_ref[...] = jnp.dot(a, w_ref[...].astype(BF16), preferred_element_type=F32) + b_ref[...]


def _adaln(c_all, w_ada, b_ada):
    rows = c_all.shape[0]
    n = w_ada.shape[1]
    tn = 1024
    return pl.pallas_call(
        _ada_kernel,
        grid=(n // tn,),
        in_specs=[
            pl.BlockSpec((rows, D_MODEL), lambda j: (0, 0)),
            pl.BlockSpec((D_MODEL, tn), lambda j: (0, j)),
            pl.BlockSpec((1, tn), lambda j: (0, j)),
        ],
        out_specs=pl.BlockSpec((rows, tn), lambda j: (0, j)),
        out_shape=jax.ShapeDtypeStruct((rows, n), F32),
        compiler_params=_params(("arbitrary",)),
        name="adaln",
    )(c_all, w_ada, b_ada.reshape(1, n))


def _inproj_kernel(x_ref, g_ref, sc_ref, sh_ref, w_ref, bg_ref,
                   u_ref, v_ref, o_ref, gt_ref, qa_ref, ka_ref, va_ref):
    x = x_ref[0]
    h = _rms(x, g_ref[...]) * (1.0 + sc_ref[0]) + sh_ref[0]
    hb = h.astype(BF16)

    def seg(lo, width):
        return jnp.dot(hb, w_ref[:, lo:lo + width], preferred_element_type=F32)

    u_ref[0] = seg(C_U, 2 * D_M)
    v_ref[0] = seg(C_V, D_M)
    o_ref[0] = jax.nn.sigmoid(seg(C_O, D_M))
    qa_ref[0] = seg(C_QA, D_A) * (HD_A ** -0.5)
    ka_ref[0] = seg(C_KA, D_A)
    va_ref[0] = seg(C_VA, D_A)
    t = seg(C_G, LANES) + bg_ref[...]
    lane = lax.broadcasted_iota(jnp.int32, t.shape, 1)
    gt_ref[0] = jnp.where(lane < H_M, t, _log_sigmoid(t))


def _inproj(x3, g_pre, scale, shift, w_r, bias_g, tm):
    nb, rows, _ = x3.shape
    per_row_mod = scale.shape[1] != 1
    mod_block = (1, tm, D_MODEL) if per_row_mod else (1, 1, D_MODEL)
    mod_map = (lambda b, i: (b, i, 0)) if per_row_mod else (lambda b, i: (b, 0, 0))
    row_map = lambda b, i: (b, i, 0)
    widths = (2 * D_M, D_M, D_M, LANES, D_A, D_A, D_A)
    return pl.pallas_call(
        _inproj_kernel,
        grid=(nb, rows // tm),
        in_specs=[
            pl.BlockSpec((1, tm, D_MODEL), row_map),
            pl.BlockSpec((1, D_MODEL), lambda b, i: (0, 0)),
            pl.BlockSpec(mod_block, mod_map),
            pl.BlockSpec(mod_block, mod_map),
            pl.BlockSpec((D_MODEL, D_IN_R), lambda b, i: (0, 0)),
            pl.BlockSpec((1, LANES), lambda b, i: (0, 0)),
        ],
        out_specs=[pl.BlockSpec((1, tm, w), row_map) for w in widths],
        out_shape=[jax.ShapeDtypeStruct((nb, rows, w), F32) for w in widths],
        compiler_params=_params(("arbitrary", "arbitrary")),
        name="inproj",
    )(x3, g_pre.reshape(1, D_MODEL), scale, shift, w_r, bias_g)


def _mlstm_kernel(u_ref, v_ref, og_ref, gc_ref, gr_ref, tail0_ref, c0_ref, n0_ref, m0_ref,
                  cw_ref, cb_ref, mh_ref,
                  mem_ref, c_out_ref, n_out_ref, m_out_ref,
                  c_s, n_s, m_s, tail_s, uext_s, *, chunk):
    L = chunk
    step = pl.program_id(1)

    @pl.when(step == 0)
    def _():
        c_s[...] = c0_ref[0]
        n_s[...] = n0_ref[0]
        m_s[...] = m0_ref[0]
        tail_s[...] = tail0_ref[0]

    u = u_ref[0]
    uext_s[0:8, :] = tail_s[...]
    uext_s[8:8 + L, :] = u
    acc = cb_ref[...] + uext_s[pl.ds(5, L), :] * cw_ref[0:1, :]
    acc = acc + uext_s[pl.ds(6, L), :] * cw_ref[1:2, :]
    acc = acc + uext_s[pl.ds(7, L), :] * cw_ref[2:3, :]
    acc = acc + u * cw_ref[3:4, :]
    tail_s[...] = uext_s[pl.ds(L, 8), :]
    qk = acc * jax.nn.sigmoid(acc)

    gc = gc_ref[0]
    gr = gr_ref[0]
    row = lax.broadcasted_iota(jnp.int32, (L, L), 0)
    col = lax.broadcasted_iota(jnp.int32, (L, L), 1)
    causal = col <= row
    tri = causal.astype(F32)
    tri_t = (row <= col).astype(F32)
    lane = lax.broadcasted_iota(jnp.int32, gc.shape, 1)
    srow = lax.broadcasted_iota(jnp.int32, gr.shape, 0)
    b_col_all = jnp.dot(tri, jnp.where(lane >= H_M, gc, 0.0), preferred_element_type=F32,
                        precision=lax.Precision.HIGHEST)
    b_row_all = jnp.dot(jnp.where(srow >= H_M, gr, 0.0), tri_t, preferred_element_type=F32,
                        precision=lax.Precision.HIGHEST)

    v_all = v_ref[0]
    og = og_ref[0]
    for h in range(H_M):
        q = qk[:, h * DK_M:(h + 1) * DK_M]
        k = qk[:, D_M + h * DK_M:D_M + (h + 1) * DK_M] * (DK_M ** -0.5)
        v = v_all[:, h * DK_M:(h + 1) * DK_M]
        li_row = gr[h:h + 1, :]
        li_col = gc[:, h:h + 1]
        b_row = b_row_all[H_M + h:H_M + h + 1, :]
        b_col = b_col_all[:, H_M + h:H_M + h + 1]
        m0 = m_s[h:h + 1, 0:1]
        c0 = c_s[h]
        n0 = n_s[h:h + 1, :]

        d = jnp.where(causal, b_col - b_row + li_row, NEG_INF)
        g_state = b_col + m0
        m = jnp.maximum(jnp.max(d, axis=-1, keepdims=True), g_state)
        w_state = jnp.exp(g_state - m)
        qb = q.astype(BF16)
        kb = k.astype(BF16)
        vb = v.astype(BF16)
        s = lax.dot_general(qb, kb, (((1,), (1,)), ((), ())), preferred_element_type=F32)
        s = s * jnp.exp(d - m)
        num = jnp.dot(s.astype(BF16), vb, preferred_element_type=F32)
        num = num + w_state * jnp.dot(qb, c0.astype(BF16), preferred_element_type=F32)
        den = jnp.sum(s, axis=-1, keepdims=True) + w_state * jnp.sum(q * n0, axis=-1, keepdims=True)
        hh = num / jnp.maximum(jnp.abs(den), jnp.exp(-m))

        b_last = b_col[L - 1:L, :]
        g_tok_row = b_last - b_row + li_row
        m_new = jnp.maximum(b_last + m0, jnp.max(g_tok_row, axis=-1, keepdims=True))
        w_tok_col = jnp.exp(b_last - b_col + li_col - m_new)
        decay = jnp.exp(b_last + m0 - m_new)
        kw = k * w_tok_col
        c_s[h] = decay * c0 + jnp.dot(kw.T.astype(BF16), vb, preferred_element_type=F32)
        n_s[h:h + 1, :] = decay * n0 + jnp.sum(kw, axis=0, keepdims=True)
        m_s[h:h + 1, :] = jnp.broadcast_to(m_new, (1, LANES))

        hn = hh * lax.rsqrt(jnp.mean(hh * hh, axis=-1, keepdims=True) + EPS)
        hn = hn * mh_ref[:, h * DK_M:(h + 1) * DK_M]
        mem_ref[0, :, h * DK_M:(h + 1) * DK_M] = og[:, h * DK_M:(h + 1) * DK_M] * hn

    @pl.when(step == pl.num_programs(1) - 1)
    def _():
        c_out_ref[0] = c_s[...]
        n_out_ref[0] = n_s[...]
        m_out_ref[0] = m_s[...]


def _mlstm(u, v, og, gates_col, gates_row, tail0, c0, n0, m0x, conv_w, conv_b, mh_norm, chunk):
    nb, rows, _ = u.shape
    nc = rows // chunk
    row_map = lambda b, c: (b, c, 0)
    bat3 = lambda b, c: (b, 0, 0)
    const2 = lambda b, c: (0, 0)
    return pl.pallas_call(
        functools.partial(_mlstm_kernel, chunk=chunk),
        grid=(nb, nc),
        in_specs=[
            pl.BlockSpec((1, chunk, 2 * D_M), row_map),
            pl.BlockSpec((1, chunk, D_M), row_map),
            pl.BlockSpec((1, chunk, D_M), row_map),
            pl.BlockSpec((1, chunk, LANES), row_map),
            pl.BlockSpec((1, 8, chunk), lambda b, c: (b, 0, c)),
            pl.BlockSpec((1, 8, 2 * D_M), bat3),
            pl.BlockSpec((1, H_M, DK_M, DK_M), lambda b, c: (b, 0, 0, 0)),
            pl.BlockSpec((1, H_M, DK_M), bat3),
            pl.BlockSpec((1, H_M, LANES), bat3),
            pl.BlockSpec((CONV_W, 2 * D_M), const2),
            pl.BlockSpec((1, 2 * D_M), const2),
            pl.BlockSpec((1, D_M), const2),
        ],
        out_specs=[
            pl.BlockSpec((1, chunk, D_M), row_map),
            pl.BlockSpec((1, H_M, DK_M, DK_M), lambda b, c: (b, 0, 0, 0)),
            pl.BlockSpec((1, H_M, DK_M), bat3),
            pl.BlockSpec((1, H_M, LANES), bat3),
        ],
        out_shape=[
            jax.ShapeDtypeStruct((nb, rows, D_M), F32),
            jax.ShapeDtypeStruct((nb, H_M, DK_M, DK_M), F32),
            jax.ShapeDtypeStruct((nb, H_M, DK_M), F32),
            jax.ShapeDtypeStruct((nb, H_M, LANES), F32),
        ],
        scratch_shapes=[
            pltpu.VMEM((H_M, DK_M, DK_M), F32),
            pltpu.VMEM((H_M, DK_M), F32),
            pltpu.VMEM((H_M, LANES), F32),
            pltpu.VMEM((8, 2 * D_M), F32),
            pltpu.VMEM((chunk + 8, 2 * D_M), F32),
        ],
        compiler_params=_params(("arbitrary", "arbitrary")),
        name="mlstm",
    )(u, v, og, gates_col, gates_row, tail0, c0, n0, m0x,
      conv_w, conv_b.reshape(1, 2 * D_M), mh_norm.reshape(1, D_M))


def _attn_kernel(q_ref, kc_ref, vc_ref, kp_ref, vp_ref, bias_ref, o_ref, l_ref, k_s, v_s, *, ql):
    first = pl.program_id(2) == 0
    k_s[0:QB, :] = kp_ref[0].astype(BF16)
    k_s[QB:QB + ql, :] = kc_ref[0].astype(BF16)
    v_s[0:QB, :] = vp_ref[0].astype(BF16)
    v_s[QB:QB + ql, :] = vc_ref[0].astype(BF16)
    col = lax.broadcasted_iota(jnp.int32, (QB, 2 * QB), 1)
    pad_keys = jnp.logical_and(first, col < QB)
    for h in range(H_A):
        lo = h * HD_A
        bias = bias_ref[h]
        for j in range(ql // QB):
            qb = q_ref[0, j * QB:(j + 1) * QB, lo:lo + HD_A].astype(BF16)
            kb = k_s[j * QB:(j + 2) * QB, lo:lo + HD_A]
            vb = v_s[j * QB:(j + 2) * QB, lo:lo + HD_A]
            logits = lax.dot_general(qb, kb, (((1,), (1,)), ((), ())), preferred_element_type=F32)
            logits = logits + bias
            if j == 0:
                logits = jnp.where(pad_keys, NEG_INF, logits)
            m = jnp.max(logits, axis=-1, keepdims=True)
            p = jnp.exp(logits - m)
            s = jnp.sum(p, axis=-1, keepdims=True)
            o = jnp.dot(p.astype(BF16), vb, preferred_element_type=F32) / s
            o_ref[0, j * QB:(j + 1) * QB, lo:lo + HD_A] = o
            l_ref[0, j * QB:(j + 1) * QB, lo:lo + HD_A] = jnp.broadcast_to(m + jnp.log(s), (QB, HD_A))


def _attn_branch(qa, ka, va, bias, dil):
    nb, seq, _ = qa.shape
    ld = seq // dil
    ql = min(ATT_QL, ld)
    view = lambda t: t.reshape(nb, ld, dil * D_A)
    cur = lambda b, r, i: (b, i, r)
    prev = lambda b, r, i: (b, jnp.maximum(i * (ql // QB) - 1, 0), r)
    o, lse = pl.pallas_call(
        functools.partial(_attn_kernel, ql=ql),
        grid=(nb, dil, ld // ql),
        in_specs=[
            pl.BlockSpec((1, ql, D_A), cur),
            pl.BlockSpec((1, ql, D_A), cur),
            pl.BlockSpec((1, ql, D_A), cur),
            pl.BlockSpec((1, QB, D_A), prev),
            pl.BlockSpec((1, QB, D_A), prev),
            pl.BlockSpec((H_A, QB, 2 * QB), lambda b, r, i: (0, 0, 0)),
        ],
        out_specs=[pl.BlockSpec((1, ql, D_A), cur), pl.BlockSpec((1, ql, D_A), cur)],
        out_shape=[jax.ShapeDtypeStruct((nb, ld, dil * D_A), F32)] * 2,
        scratch_shapes=[pltpu.VMEM((QB + ql, D_A), BF16), pltpu.VMEM((QB + ql, D_A), BF16)],
        compiler_params=_params(("arbitrary", "arbitrary", "arbitrary")),
        name=f"attn_d{dil}",
    )(view(qa), view(ka), view(va), view(ka), view(va), bias)
    return o.reshape(nb, seq, D_A), lse.reshape(nb, seq, D_A)


def _heads_kernel(k_ref, v_ref, ko_ref, vo_ref):
    for src, dst in ((k_ref, ko_ref), (v_ref, vo_ref)):
        for c in range(D_A // LANES):
            xt = src[0, :, c * LANES:(c + 1) * LANES].T
            dst[0, 2 * c] = xt[:HD_A]
            dst[0, 2 * c + 1] = xt[HD_A:]


def _window_heads(ka, va, keep, tm=512):
    nb, seq, _ = ka.shape
    first = (seq - keep) // tm
    in_spec = pl.BlockSpec((1, tm, D_A), lambda b, i: (b, first + i, 0))
    out_spec = pl.BlockSpec((1, H_A, HD_A, tm), lambda b, i: (b, 0, 0, i))
    kt, vt = pl.pallas_call(
        _heads_kernel,
        grid=(nb, keep // tm),
        in_specs=[in_spec, in_spec],
        out_specs=[out_spec, out_spec],
        out_shape=[jax.ShapeDtypeStruct((nb, H_A, HD_A, keep), F32)] * 2,
        compiler_params=_params(("arbitrary", "arbitrary")),
        name="window_heads",
    )(ka, va)
    return kt.transpose(0, 3, 1, 2), vt.transpose(0, 3, 1, 2)


def _t5_bucket(dist):
    max_exact = N_BUCKETS // 2
    n = jnp.maximum(dist, 1).astype(F32)
    large = max_exact + (jnp.log(n / max_exact) / math.log(MAX_DISTANCE / max_exact)
                         * (N_BUCKETS - max_exact)).astype(jnp.int32)
    large = jnp.minimum(large, N_BUCKETS - 1)
    return jnp.where(dist < max_exact, dist, large)


def _branch_bias_vec(rel_bias, window, dil):
    return rel_bias[_t5_bucket(dil * jnp.arange(window // dil + 1))].astype(F32)


def _prompt_bias(rel_bias, window, dil):
    assert window // dil == QB
    n = 2 * QB
    vec = _branch_bias_vec(rel_bias, window, dil)
    vecp = jnp.concatenate([vec[::-1], jnp.full((n - QB - 1, H_A), NEG_INF, F32)], axis=0)
    g = jnp.roll(vecp, QB - 1, axis=0)
    skew = jnp.tile(g, (QB + 1, 1))[:QB * (n + 1)].reshape(QB, n + 1, H_A)[:, :n]
    return skew[::-1].transpose(2, 0, 1)


TOK_PAD = 8


def _sattn_kernel(q_ref, kt_ref, vt_ref, knt_ref, vnt_ref, bc_ref, bn_ref, att_ref):
    nt = (((1,), (1,)), ((), ()))
    for h in range(H_A):
        q = q_ref[0, h].astype(BF16)
        lc = jnp.dot(q, kt_ref[0, h].astype(BF16), preferred_element_type=F32)
        ln = jnp.dot(q, knt_ref[0, h].astype(BF16), preferred_element_type=F32)
        lcs = [lc + bc_ref[d, h] for d in range(len(BRANCHES))]
        lns = [ln + bn_ref[d, h] for d in range(len(BRANCHES))]
        m = functools.reduce(jnp.maximum, [jnp.max(x, axis=-1, keepdims=True) for x in lcs + lns])
        pc = functools.reduce(jnp.add, [jnp.exp(x - m) for x in lcs])
        pn = functools.reduce(jnp.add, [jnp.exp(x - m) for x in lns])
        s = jnp.sum(pc, axis=-1, keepdims=True) + jnp.sum(pn, axis=-1, keepdims=True)
        o = lax.dot_general(pc.astype(BF16), vt_ref[0, h].astype(BF16), nt, preferred_element_type=F32)
        o = o + lax.dot_general(pn.astype(BF16), vnt_ref[0, h].astype(BF16), nt, preferred_element_type=F32)
        att_ref[0, h] = o / s


def _sample_attention(q_h, kt, vt, knt, vnt, bias_c, bias_n):
    nb, _, _, n_past = kt.shape
    bat = lambda b: (b, 0, 0, 0)
    const = lambda b: (0, 0, 0, 0)
    return pl.pallas_call(
        _sattn_kernel,
        grid=(nb,),
        in_specs=[
            pl.BlockSpec((1, H_A, TOK_PAD, HD_A), bat),
            pl.BlockSpec((1, H_A, HD_A, n_past), bat),
            pl.BlockSpec((1, H_A, HD_A, n_past), bat),
            pl.BlockSpec((1, H_A, HD_A, LANES), bat),
            pl.BlockSpec((1, H_A, HD_A, LANES), bat),
            pl.BlockSpec(bias_c.shape, const),
            pl.BlockSpec(bias_n.shape, const),
        ],
        out_specs=pl.BlockSpec((1, H_A, TOK_PAD, HD_A), bat),
        out_shape=jax.ShapeDtypeStruct((nb, H_A, TOK_PAD, HD_A), F32),
        compiler_params=_params(("arbitrary",)),
        name="sample_attn",
    )(q_h, kt, vt, knt, vnt, bias_c, bias_n)


def _sample_bias(rel_bias, n_tok, n_past):
    def table(key_pos):
        out = []
        for window, dil in BRANCHES:
            nk = window // dil
            vec = _branch_bias_vec(rel_bias, window, dil)
            delta = (n_past + jnp.arange(n_tok))[:, None] - key_pos[None, :]
            ok = (delta >= 0) & (delta % dil == 0) & (delta // dil <= nk)
            b = jnp.where(ok[..., None], vec[jnp.clip(delta // dil, 0, nk)], NEG_INF)
            b = jnp.concatenate([b, jnp.zeros((TOK_PAD - n_tok,) + b.shape[1:], F32)], axis=0)
            out.append(b.transpose(2, 0, 1))
        return jnp.stack(out)

    lane = jnp.arange(LANES)
    beyond = n_past + n_tok + WINDOW_MAX
    return table(jnp.arange(n_past)), table(jnp.where(lane < n_tok, n_past + lane, beyond))


def _postmix_kernel(mem_ref, o1_ref, l1_ref, o2_ref, l2_ref, o3_ref, l3_ref, x_ref,
                    g1_ref, sc2_ref, sh2_ref, gpost_ref, gpre_ref, wo_ref, wr_ref, br_ref,
                    x1_ref, h2_ref, idx_ref, gate_ref):
    l1, l2, l3 = l1_ref[0], l2_ref[0], l3_ref[0]
    lmax = jnp.maximum(jnp.maximum(l1, l2), l3)
    w1 = jnp.exp(l1 - lmax)
    w2 = jnp.exp(l2 - lmax)
    w3 = jnp.exp(l3 - lmax)
    att = (w1 * o1_ref[0] + w2 * o2_ref[0] + w3 * o3_ref[0]) / (w1 + w2 + w3)
    y = jnp.dot(mem_ref[0].astype(BF16), wo_ref[0:D_M, :], preferred_element_type=F32)
    y = y + jnp.dot(att.astype(BF16), wo_ref[D_M:, :], preferred_element_type=F32)
    x1 = x_ref[0] + g1_ref[0] * _rms(y, gpost_ref[...])
    x1_ref[0] = x1
    h2 = _rms(x1, gpre_ref[...]) * (1.0 + sc2_ref[0]) + sh2_ref[0]
    h2_ref[0] = h2
    h_hi = h2.astype(BF16)
    h_lo = (h2 - h_hi.astype(F32)).astype(BF16)
    w_hi = wr_ref[0]
    logits = jnp.dot(h_hi, w_hi, preferred_element_type=F32)
    logits = logits + jnp.dot(h_lo, w_hi, preferred_element_type=F32)
    logits = logits + jnp.dot(h_hi, wr_ref[1], preferred_element_type=F32) + br_ref[...]
    lane = lax.broadcasted_iota(jnp.int32, logits.shape, 1)
    lane_f = lane.astype(F32)
    logits = jnp.where(lane < N_EXPERTS, logits, NEG_INF)
    vals, idxs = [], []
    for _ in range(TOP_K):
        mx = jnp.max(logits, axis=-1, keepdims=True)
        ix = jnp.min(jnp.where(logits == mx, lane_f, float(LANES)), axis=-1, keepdims=True)
        vals.append(mx)
        idxs.append(ix)
        logits = jnp.where(lane_f == ix, NEG_INF, logits)
    es = [jnp.exp(v - vals[0]) for v in vals]
    tot = es[0] + es[1] + es[2] + es[3]
    idx_tile = jnp.zeros(logits.shape, F32)
    gate_tile = jnp.zeros(logits.shape, F32)
    for k in range(TOP_K):
        idx_tile = jnp.where(lane == k, idxs[k], idx_tile)
        gate_tile = jnp.where(lane == k, es[k] / tot, gate_tile)
    idx_ref[0] = idx_tile.astype(jnp.int32)
    gate_ref[0] = gate_tile


def _postmix(mem, branches, x3, gate1, scale2, shift2, g_post, g_pre, w_out_b, w_router_p, b_router_p, tm):
    nb, rows, _ = x3.shape
    per_row_mod = gate1.shape[1] != 1
    mod_block = (1, tm, D_MODEL) if per_row_mod else (1, 1, D_MODEL)
    mod_map = (lambda b, i: (b, i, 0)) if per_row_mod else (lambda b, i: (b, 0, 0))
    row_map = lambda b, i: (b, i, 0)
    const2 = lambda b, i: (0, 0)
    half = pl.BlockSpec((1, tm, D_M), row_map)
    flat = [a for pair in branches for a in pair]
    return pl.pallas_call(
        _postmix_kernel,
        grid=(nb, rows // tm),
        in_specs=[half] * 7 + [
            pl.BlockSpec((1, tm, D_MODEL), row_map),
            pl.BlockSpec(mod_block, mod_map),
            pl.BlockSpec(mod_block, mod_map),
            pl.BlockSpec(mod_block, mod_map),
            pl.BlockSpec((1, D_MODEL), const2),
            pl.BlockSpec((1, D_MODEL), const2),
            pl.BlockSpec((D_MODEL, D_MODEL), const2),
            pl.BlockSpec((2, D_MODEL, LANES), lambda b, i: (0, 0, 0)),
            pl.BlockSpec((1, LANES), const2),
        ],
        out_specs=[
            pl.BlockSpec((1, tm, D_MODEL), row_map),
            pl.BlockSpec((1, tm, D_MODEL), row_map),
            pl.BlockSpec((1, tm, LANES), row_map),
            pl.BlockSpec((1, tm, LANES), row_map),
        ],
        out_shape=[
            jax.ShapeDtypeStruct((nb, rows, D_MODEL), F32),
            jax.ShapeDtypeStruct((nb, rows, D_MODEL), F32),
            jax.ShapeDtypeStruct((nb, rows, LANES), jnp.int32),
            jax.ShapeDtypeStruct((nb, rows, LANES), F32),
        ],
        compiler_params=_params(("arbitrary", "arbitrary")),
        name="postmix",
    )(mem, *flat, x3, gate1, scale2, shift2, g_post.reshape(1, D_MODEL), g_pre.reshape(1, D_MODEL),
      w_out_b, w_router_p, b_router_p)


def _row_wait(src_rows, dst_rows, sem):
    pltpu.make_async_copy(src_rows, dst_rows, sem).wait()


def _scatter_kernel(slot_ref, h_ref, xs_hbm, buf, sem):
    bits = pltpu.bitcast(h_ref[...].astype(BF16).astype(F32), U32)
    packed = (bits[:, :PACK_W] & U32(HI_MASK)) | (bits[:, PACK_W:] >> 16)
    buf[...] = pltpu.einshape("m(jc)->mjc", packed, j=PACK_T)

    def issue(r, carry):
        src = buf.at[pl.ds(r, 1)]
        for k in range(TOP_K):
            pltpu.make_async_copy(src, xs_hbm.at[pl.ds(slot_ref[0, 0, r * TOP_K + k], 1)],
                                  sem).start(priority=k % 2)
        return carry

    lax.fori_loop(0, TOK_TILE, issue, 0, unroll=4)
    for _ in range(TOP_K):
        _row_wait(buf, xs_hbm.at[pl.ds(0, TOK_TILE)], sem)


def _moe_scatter(h_all, slot_tiles):
    tiles = slot_tiles.shape[0]
    n_rows = tiles * TOK_TILE * TOP_K
    return pl.pallas_call(
        _scatter_kernel,
        grid=(tiles,),
        in_specs=[
            pl.BlockSpec((1, 1, TOK_TILE * TOP_K), lambda i: (i, 0, 0), memory_space=pltpu.SMEM),
            pl.BlockSpec((TOK_TILE, D_MODEL), lambda i: (i, 0)),
        ],
        out_specs=pl.BlockSpec(memory_space=pl.ANY),
        out_shape=jax.ShapeDtypeStruct((n_rows, PACK_T, LANES), U32),
        scratch_shapes=[pltpu.VMEM((TOK_TILE, PACK_T, LANES), U32), pltpu.SemaphoreType.DMA],
        compiler_params=_params(("arbitrary",)),
        name="moe_scatter",
    )(slot_tiles, h_all)


def _expert_kernel(tile_ref, exp_ref, lo_ref, hi_ref, first_ref, nvis_ref,
                   xs_ref, w1_ref, b1_ref, w2_ref, b2_ref, ys_ref, w1b, w2b):
    v = pl.program_id(0)
    active = v < nvis_ref[0]
    changed = jnp.logical_or(v == 0, exp_ref[v] != exp_ref[jnp.maximum(v - 1, 0)])
    lo = lo_ref[v]
    hi = hi_ref[v]

    @pl.when(jnp.logical_and(active, changed))
    def _():
        w1b[...] = w1_ref[0].astype(BF16)
        w2b[...] = w2_ref[0].astype(BF16)

    @pl.when(jnp.logical_and(active, first_ref[v] == 1))
    def _():
        ys_ref[...] = jnp.zeros(ys_ref.shape, F32)

    for r in range(MOE_M // MOE_SUB):
        r0 = r * MOE_SUB

        @pl.when(jnp.logical_and(active, jnp.logical_and(lo < r0 + MOE_SUB, hi > r0)))
        def _():
            w = pltpu.einshape("mjc->m(jc)", xs_ref[r0:r0 + MOE_SUB])
            x_hi = pltpu.bitcast(w & U32(HI_MASK), F32).astype(BF16)
            x_lo = pltpu.bitcast(w << 16, F32).astype(BF16)
            hgu = jnp.dot(x_hi, w1b[0:PACK_W, :], preferred_element_type=F32)
            hgu = hgu + jnp.dot(x_lo, w1b[PACK_W:, :], preferred_element_type=F32) + b1_ref[0]
            x_glu = jnp.minimum(hgu[:, :D_FF], SWIGLU_LIMIT)
            x_lin = jnp.clip(hgu[:, D_FF:], -SWIGLU_LIMIT, SWIGLU_LIMIT)
            act = x_glu * jax.nn.sigmoid(SWIGLU_ALPHA * x_glu) * (x_lin + 1.0)
            y = jnp.dot(act.astype(BF16), w2b[...], preferred_element_type=F32) + b2_ref[0]
            rows = r0 + lax.broadcasted_iota(jnp.int32, (MOE_SUB, 1, 1), 0)
            mine = jnp.logical_and(rows >= lo, rows < hi)
            y_rows = pltpu.einshape("m(jc)->mjc", y, j=ROW_T)
            ys_ref[r0:r0 + MOE_SUB] = jnp.where(mine, y_rows, ys_ref[r0:r0 + MOE_SUB])


def _moe_experts(xs, visits, w1, b1, w2, b2):
    n_rows = xs.shape[0]
    n_vis = visits[0].shape[0]
    tile_map = lambda v, tile, *_: (tile[v], 0, 0)
    exp_map = lambda v, tile, exp, *_: (exp[v], 0, 0)
    return pl.pallas_call(
        _expert_kernel,
        grid_spec=pltpu.PrefetchScalarGridSpec(
            num_scalar_prefetch=6,
            grid=(n_vis,),
            in_specs=[
                pl.BlockSpec((MOE_M, PACK_T, LANES), tile_map),
                pl.BlockSpec((1, D_MODEL, 2 * D_FF), exp_map),
                pl.BlockSpec((1, 1, 2 * D_FF), exp_map),
                pl.BlockSpec((1, D_FF, D_MODEL), exp_map),
                pl.BlockSpec((1, 1, D_MODEL), exp_map),
            ],
            out_specs=pl.BlockSpec((MOE_M, ROW_T, LANES), tile_map),
            scratch_shapes=[pltpu.VMEM((D_MODEL, 2 * D_FF), BF16), pltpu.VMEM((D_FF, D_MODEL), BF16)],
        ),
        out_shape=jax.ShapeDtypeStruct((n_rows, ROW_T, LANES), F32),
        compiler_params=_params(("arbitrary",)),
        name="moe_experts",
    )(*visits, xs, w1, b1.reshape(N_EXPERTS, 1, 2 * D_FF), w2, b2.reshape(N_EXPERTS, 1, D_MODEL))


def _combine_kernel(slot_ref, gate_ref, x1p_ref, x1s_ref, g2p_ref, g2s_ref, gpost_ref, ys_hbm,
                    yp_ref, ysm_ref, buf, sem, *, n_prompt_tiles):
    i = pl.program_id(0)

    def issue(r, carry):
        for k in range(TOP_K):
            pltpu.make_async_copy(ys_hbm.at[pl.ds(slot_ref[0, 0, r * TOP_K + k], 1)],
                                  buf.at[pl.ds(k * TOK_TILE + r, 1)], sem).start(priority=k % 2)
        return carry

    lax.fori_loop(0, TOK_TILE, issue, 0, unroll=4)
    for k in range(TOP_K):
        _row_wait(ys_hbm.at[pl.ds(0, TOK_TILE)], buf.at[pl.ds(k * TOK_TILE, TOK_TILE)], sem)
    g = gate_ref[...]
    f = jnp.zeros((TOK_TILE, D_MODEL), F32)
    for k in range(TOP_K):
        rows = pltpu.einshape("mjc->m(jc)", buf[k * TOK_TILE:(k + 1) * TOK_TILE])
        f = f + rows * g[:, k:k + 1]
    is_sample = i >= n_prompt_tiles
    x1 = jnp.where(is_sample, x1s_ref[...], x1p_ref[...])
    g2 = jnp.where(is_sample, g2s_ref[...], g2p_ref[0])
    y = x1 + g2 * _rms(f, gpost_ref[...])

    @pl.when(jnp.logical_not(is_sample))
    def _():
        yp_ref[...] = y

    @pl.when(is_sample)
    def _():
        ysm_ref[...] = y


def _moe_combine(ys, slot_tiles, gates, x1_p, x1_s, gate2_p, gate2_s, g_post, rows_per_batch):
    tiles = slot_tiles.shape[0]
    n_p = x1_p.shape[0] // TOK_TILE
    assert tiles == n_p + 1 and x1_s.shape[0] == TOK_TILE
    tiles_per_batch = rows_per_batch // TOK_TILE
    p_tile = lambda i: (jnp.minimum(i, n_p - 1), 0)
    const2 = lambda i: (0, 0)
    return pl.pallas_call(
        functools.partial(_combine_kernel, n_prompt_tiles=n_p),
        grid=(tiles,),
        in_specs=[
            pl.BlockSpec((1, 1, TOK_TILE * TOP_K), lambda i: (i, 0, 0), memory_space=pltpu.SMEM),
            pl.BlockSpec((TOK_TILE, LANES), lambda i: (i, 0)),
            pl.BlockSpec((TOK_TILE, D_MODEL), p_tile),
            pl.BlockSpec((TOK_TILE, D_MODEL), const2),
            pl.BlockSpec((1, 1, D_MODEL), lambda i: (jnp.minimum(i, n_p - 1) // tiles_per_batch, 0, 0)),
            pl.BlockSpec((TOK_TILE, D_MODEL), const2),
            pl.BlockSpec((1, D_MODEL), const2),
            pl.BlockSpec(memory_space=pl.ANY),
        ],
        out_specs=[pl.BlockSpec((TOK_TILE, D_MODEL), p_tile), pl.BlockSpec((TOK_TILE, D_MODEL), const2)],
        out_shape=[jax.ShapeDtypeStruct(x1_p.shape, F32), jax.ShapeDtypeStruct(x1_s.shape, F32)],
        scratch_shapes=[pltpu.VMEM((TOP_K * TOK_TILE, ROW_T, LANES), F32), pltpu.SemaphoreType.DMA],
        compiler_params=_params(("arbitrary",)),
        name="moe_combine",
    )(slot_tiles, gates, x1_p, x1_s, gate2_p, gate2_s, g_post.reshape(1, D_MODEL), ys)


def _moe_plan(top_idx):
    a = top_idx.shape[0] * TOP_K
    assert a % MOE_M == 0
    flat_e = top_idx[:, :TOP_K].reshape(a)
    onehot = (flat_e[:, None] == jnp.arange(N_EXPERTS)[None, :]).astype(jnp.int32)
    csum = jnp.cumsum(onehot, axis=0)
    counts = csum[-1]
    g_end = jnp.cumsum(counts)
    g_start = g_end - counts
    rank = jnp.sum(csum * onehot, axis=1) - 1
    slot_of = (jnp.sum(g_start[None, :] * onehot, axis=1) + rank).astype(jnp.int32)

    n_tiles = a // MOE_M
    n_vis = n_tiles + N_EXPERTS - 1
    first_tile = g_start // MOE_M
    last_tile = jnp.where(counts > 0, (g_end - 1) // MOE_M, first_tile)
    nvis_e = jnp.where(counts > 0, last_tile - first_tile + 1, 0)
    v_end = jnp.cumsum(nvis_e)
    v_start = v_end - nvis_e
    total = v_end[-1]
    v = jnp.arange(n_vis)
    vc = jnp.minimum(v, total - 1)
    e_of = jnp.minimum(jnp.sum((vc[:, None] >= v_end[None, :]).astype(jnp.int32), axis=1), N_EXPERTS - 1)
    tile_of = first_tile[e_of] + (vc - v_start[e_of])
    lo = jnp.maximum(g_start[e_of], tile_of * MOE_M) - tile_of * MOE_M
    hi = jnp.minimum(g_end[e_of], (tile_of + 1) * MOE_M) - tile_of * MOE_M
    prev_tile = jnp.concatenate([jnp.full((1,), -1, tile_of.dtype), tile_of[:-1]])
    first = (tile_of != prev_tile).astype(jnp.int32)
    i32 = lambda t: t.astype(jnp.int32)
    visits = (i32(tile_of), i32(e_of), i32(lo), i32(hi), first, i32(total).reshape(1))
    return slot_of, visits


def _reorder_w_in(w_in):
    off_om_end = 4 * D_M
    off_g_end = off_om_end + 2 * H_M
    w = jnp.concatenate([
        w_in[:, :off_om_end],
        w_in[:, off_g_end:],
        w_in[:, off_om_end:off_g_end],
        jnp.zeros((D_MODEL, LANES - 2 * H_M), w_in.dtype),
    ], axis=1)
    return w.astype(BF16)


def kernel(x_prompt, x_sample, cache_k, cache_v, state_conv, state_C, state_n, state_m, c_prompt, c_sample, w_ada, b_ada, g_pre_mix, g_post_mix, g_pre_ffn, g_post_ffn, w_in, b_ig, b_fg, conv_w, conv_b, mh_norm, rel_bias, w_out, w_router, b_router, w1, b1, w2, b2):
    depth = w_in.shape[0]
    assert depth == 1
    l = 0
    nb_p, seq, _ = x_prompt.shape
    nb_s, n_tok, _ = x_sample.shape
    rows_s = nb_s * n_tok
    assert rows_s == TOK_TILE

    n_c = nb_p + nb_s
    c_pad = -(-n_c // 8) * 8
    c_all = jnp.concatenate([c_prompt, c_sample, jnp.zeros((c_pad - n_c, D_MODEL), F32)], axis=0)
    mod = _adaln(c_all, w_ada[l], b_ada[l])
    mods = [mod[:, i * D_MODEL:(i + 1) * D_MODEL] for i in range(6)]
    mods_p = [m[:nb_p].reshape(nb_p, 1, D_MODEL) for m in mods]
    mods_s = [jnp.repeat(m[nb_p:n_c], n_tok, axis=0).reshape(1, rows_s, D_MODEL) for m in mods]

    w_r = _reorder_w_in(w_in[l])
    bias_g = jnp.concatenate([b_ig[l], b_fg[l], jnp.zeros((LANES - 2 * H_M,), F32)]).reshape(1, LANES)
    w_out_b = w_out[l].astype(BF16)
    w_router_f = jnp.concatenate([w_router[l], jnp.zeros((D_MODEL, LANES - N_EXPERTS), F32)], axis=1)
    w_router_hi = w_router_f.astype(BF16)
    w_router_p = jnp.stack([w_router_hi, (w_router_f - w_router_hi.astype(F32)).astype(BF16)])
    b_router_p = jnp.concatenate([b_router[l].astype(F32), jnp.zeros((LANES - N_EXPERTS,), F32)]).reshape(1, LANES)

    shift1, scale1, gate1, shift2, scale2, gate2_p = mods_p
    u, v_m, og, gts, qa, ka, va = _inproj(x_prompt, g_pre_mix[l], scale1, shift1, w_r, bias_g, tm=512)
    gates_row = gts[:, :, :8].transpose(0, 2, 1)
    zeros = lambda *s: jnp.zeros(s, F32)
    mem, c_p, n_p, m_p = _mlstm(u, v_m, og, gts, gates_row, zeros(nb_p, 8, 2 * D_M),
                                zeros(nb_p, H_M, DK_M, DK_M), zeros(nb_p, H_M, DK_M), zeros(nb_p, H_M, LANES),
                                conv_w[l], conv_b[l], mh_norm[l], MLSTM_L_PROMPT)
    branches = [_attn_branch(qa, ka, va, _prompt_bias(rel_bias, w, d), d) for (w, d) in BRANCHES]
    x1_p, h2_p, idx_p, gate_p = _postmix(mem, branches, x_prompt, gate1, scale2, shift2,
                                         g_post_mix[l], g_pre_ffn[l], w_out_b, w_router_p, b_router_p, tm=256)
    keep = min(WINDOW_MAX, seq)
    k_p, v_p = _window_heads(ka, va, keep)
    conv_p = u[:, seq - (CONV_W - 1):]

    shift1, scale1, gate1, shift2, scale2, gate2_s = mods_s
    x_s = x_sample.reshape(1, rows_s, D_MODEL)
    u_s, v_s, og_s, gts_s, qa_s, ka_s, va_s = _inproj(x_s, g_pre_mix[l], scale1, shift1, w_r, bias_g, tm=rows_s)
    per_b = lambda t: t.reshape(nb_s, n_tok, t.shape[-1])
    ls = MLSTM_L_SAMPLE
    pad_rows = lambda t, n: jnp.concatenate([t, jnp.zeros((nb_s, n - t.shape[1], t.shape[2]), t.dtype)], axis=1)
    inert = jnp.concatenate([jnp.full((H_M,), NEG_INF, F32), jnp.zeros((LANES - H_M,), F32)])
    gts_pad = jnp.concatenate([per_b(gts_s), jnp.broadcast_to(inert, (nb_s, ls - n_tok, LANES))], axis=1)
    tail0 = jnp.concatenate([zeros(nb_s, 8 - (CONV_W - 1), 2 * D_M), state_conv[l].astype(F32)], axis=1)
    m0x = jnp.broadcast_to(state_m[l].astype(F32)[:, :, None], (nb_s, H_M, LANES))
    mem_s, c_s, n_s, m_s = _mlstm(pad_rows(per_b(u_s), ls), pad_rows(per_b(v_s), ls), pad_rows(per_b(og_s), ls),
                                  gts_pad, gts_pad[:, :, :8].transpose(0, 2, 1), tail0,
                                  state_C[l].astype(F32), state_n[l].astype(F32), m0x,
                                  conv_w[l], conv_b[l], mh_norm[l], ls)
    mem_s = mem_s[:, :n_tok].reshape(1, rows_s, D_M)
    heads = lambda t: t.reshape(nb_s, n_tok, H_A, HD_A)
    k_new, v_new = heads(ka_s), heads(va_s)
    n_past = cache_k.shape[2]
    assert n_tok <= TOK_PAD
    q_h = jnp.pad(heads(qa_s).transpose(0, 2, 1, 3), ((0, 0), (0, 0), (0, TOK_PAD - n_tok), (0, 0)))
    new_t = lambda t: jnp.pad(t.transpose(0, 2, 3, 1), ((0, 0), (0, 0), (0, 0), (0, LANES - n_tok)))
    cache_t = lambda c: c[l].astype(F32).transpose(0, 2, 3, 1)
    att_s = _sample_attention(q_h, cache_t(cache_k), cache_t(cache_v), new_t(k_new), new_t(v_new),
                              *_sample_bias(rel_bias, n_tok, n_past))
    att_s = att_s[:, :, :n_tok].transpose(0, 2, 1, 3).reshape(1, rows_s, D_A)
    zero_o = zeros(1, rows_s, D_A)
    ninf = jnp.full((1, rows_s, D_A), NEG_INF, F32)
    branches_s = [(att_s, zero_o), (zero_o, ninf), (zero_o, ninf)]
    x1_s, h2_s, idx_s, gate_s = _postmix(mem_s, branches_s, x_s, gate1, scale2, shift2,
                                         g_post_mix[l], g_pre_ffn[l], w_out_b, w_router_p, b_router_p, tm=rows_s)

    t_p = nb_p * seq
    h_all = jnp.concatenate([h2_p.reshape(t_p, D_MODEL), h2_s.reshape(rows_s, D_MODEL)], axis=0)
    idx_all = jnp.concatenate([idx_p.reshape(t_p, LANES), idx_s.reshape(rows_s, LANES)], axis=0)
    gate_all = jnp.concatenate([gate_p.reshape(t_p, LANES), gate_s.reshape(rows_s, LANES)], axis=0)
    slot_of, visits = _moe_plan(idx_all)
    slot_tiles = slot_of.reshape((t_p + rows_s) // TOK_TILE, 1, TOK_TILE * TOP_K)
    xs = _moe_scatter(h_all, slot_tiles)
    ys = _moe_experts(xs, visits, w1[l], b1[l], w2[l], b2[l])
    y_p, y_s = _moe_combine(ys, slot_tiles, gate_all, x1_p.reshape(t_p, D_MODEL), x1_s.reshape(rows_s, D_MODEL),
                            gate2_p, gate2_s.reshape(rows_s, D_MODEL), g_post_ffn[l], seq)
    y_prompt = y_p.reshape(nb_p, seq, D_MODEL)
    y_sample = y_s.reshape(nb_s, n_tok, D_MODEL)

    st = lambda t: t[None]
    conv_s = jnp.concatenate([state_conv[l].astype(F32), per_b(u_s)], axis=1)[:, -(CONV_W - 1):]
    return (y_prompt, y_sample, st(k_p), st(v_p), st(conv_p), st(c_p), st(n_p), st(m_p[:, :, 0]),
            st(k_new), st(v_new), st(conv_s), st(c_s), st(n_s), st(m_s[:, :, 0]))
```

```python
import functools
import math

import jax
import jax.numpy as jnp
import numpy as np
from jax import lax
from jax.experimental import pallas as pl
from jax.experimental.pallas import tpu as pltpu

F32 = jnp.float32
BF16 = jnp.bfloat16
U32 = jnp.uint32
NEG_INF = float("-inf")

D_MODEL = 1024
H_M = 4
D_M = 512
DK_M = 128
CONV_W = 4
H_A = 8
D_A = 512
HD_A = 64
BRANCHES = ((128, 1), (512, 4), (2048, 16))
WINDOW_MAX = 2048
QB = 128
N_BUCKETS = 32
MAX_DISTANCE = 2048
N_EXPERTS = 32
TOP_K = 4
D_FF = 1024
SWIGLU_LIMIT = 7.0
SWIGLU_ALPHA = 1.702
EPS = 1e-6

LANES = 128
C_U, C_V, C_O, C_QA, C_KA, C_VA, C_G = 0, 1024, 1536, 2048, 2560, 3072, 3584
D_IN_R = C_G + LANES

MLSTM_L_PROMPT = 256
MLSTM_L_SAMPLE = 128
ATT_QL = 256
MOE_M = 512
MOE_SUB = 256
TOK_TILE = 128
PACK_W = D_MODEL // 2
PACK_T = PACK_W // LANES
ROW_T = D_MODEL // LANES
HI_MASK = 0xFFFF0000
VMEM_LIMIT = 56 * 1024 * 1024


def _params(sem, vmem=None):
    return pltpu.CompilerParams(dimension_semantics=sem, vmem_limit_bytes=vmem or VMEM_LIMIT)


def _rms(x, g):
    return x * lax.rsqrt(jnp.mean(x * x, axis=-1, keepdims=True) + EPS) * g


def _log_sigmoid(t):
    return jnp.minimum(t, 0.0) - jnp.log1p(jnp.exp(-jnp.abs(t)))


def _ada_kernel(c_ref, w_ref, b_ref, o_ref):
    c = c_ref[...]
    a = (c * jax.nn.sigmoid(c)).astype(BF16)
    o_ref[...] = jnp.dot(a, w_ref[...].astype(BF16), preferred_element_type=F32) + b_ref[...]


def _adaln(c_all, w_ada, b_ada):
    rows = c_all.shape[0]
    n = w_ada.shape[1]
    tn = 1024
    return pl.pallas_call(
        _ada_kernel,
        grid=(n // tn,),
        in_specs=[
            pl.BlockSpec((rows, D_MODEL), lambda j: (0, 0)),
            pl.BlockSpec((D_MODEL, tn), lambda j: (0, j)),
            pl.BlockSpec((1, tn), lambda j: (0, j)),
        ],
        out_specs=pl.BlockSpec((rows, tn), lambda j: (0, j)),
        out_shape=jax.ShapeDtypeStruct((rows, n), F32),
        compiler_params=_params(("arbitrary",)),
        name="adaln",
    )(c_all, w_ada, b_ada.reshape(1, n))


def _to_dilated(x, dil):
    return pltpu.einshape("(lr)c->l(rc)", x, r=dil)


def _from_dilated(x, dil):
    return pltpu.einshape("l(rc)->(lr)c", x, r=dil)


def _inproj_kernel(x_ref, g_ref, sc_ref, sh_ref, w_ref, bg_ref,
                   u_ref, v_ref, o_ref, gt_ref, qa_ref, ka_ref, va_ref, *dil_refs, dilations):
    x = x_ref[0]
    h = _rms(x, g_ref[...]) * (1.0 + sc_ref[0]) + sh_ref[0]
    hb = h.astype(BF16)

    def seg(lo, width):
        return jnp.dot(hb, w_ref[:, lo:lo + width], preferred_element_type=F32)

    u_ref[0] = seg(C_U, 2 * D_M)
    v_ref[0] = seg(C_V, D_M)
    o_ref[0] = jax.nn.sigmoid(seg(C_O, D_M))
    qkv = (seg(C_QA, D_A) * (HD_A ** -0.5), seg(C_KA, D_A), seg(C_VA, D_A))
    for ref, val in zip((qa_ref, ka_ref, va_ref), qkv):
        ref[0] = val
    for n, dil in enumerate(dilations):
        for j, val in enumerate(qkv):
            dil_refs[3 * n + j][0] = _to_dilated(val, dil)
    t = seg(C_G, LANES) + bg_ref[...]
    lane = lax.broadcasted_iota(jnp.int32, t.shape, 1)
    gt_ref[0] = jnp.where(lane < H_M, t, _log_sigmoid(t))


def _inproj(x3, g_pre, scale, shift, w_r, bias_g, tm, dilations=()):
    nb, rows, _ = x3.shape
    per_row_mod = scale.shape[1] != 1
    mod_block = (1, tm, D_MODEL) if per_row_mod else (1, 1, D_MODEL)
    mod_map = (lambda b, i: (b, i, 0)) if per_row_mod else (lambda b, i: (b, 0, 0))
    row_map = lambda b, i: (b, i, 0)
    widths = (2 * D_M, D_M, D_M, LANES, D_A, D_A, D_A)
    dil_specs = [pl.BlockSpec((1, tm // d, d * D_A), row_map) for d in dilations for _ in range(3)]
    dil_shapes = [jax.ShapeDtypeStruct((nb, rows // d, d * D_A), F32) for d in dilations for _ in range(3)]
    return pl.pallas_call(
        functools.partial(_inproj_kernel, dilations=tuple(dilations)),
        grid=(nb, rows // tm),
        in_specs=[
            pl.BlockSpec((1, tm, D_MODEL), row_map),
            pl.BlockSpec((1, D_MODEL), lambda b, i: (0, 0)),
            pl.BlockSpec(mod_block, mod_map),
            pl.BlockSpec(mod_block, mod_map),
            pl.BlockSpec((D_MODEL, D_IN_R), lambda b, i: (0, 0)),
            pl.BlockSpec((1, LANES), lambda b, i: (0, 0)),
        ],
        out_specs=[pl.BlockSpec((1, tm, w), row_map) for w in widths] + dil_specs,
        out_shape=[jax.ShapeDtypeStruct((nb, rows, w), F32) for w in widths] + dil_shapes,
        compiler_params=_params(("arbitrary", "arbitrary")),
        name="inproj",
    )(x3, g_pre.reshape(1, D_MODEL), scale, shift, w_r, bias_g)


def _mlstm_kernel(u_ref, v_ref, og_ref, gc_ref, gr_ref, tail0_ref, c0_ref, n0_ref, m0_ref,
                  cw_ref, cb_ref, mh_ref,
                  mem_ref, c_out_ref, n_out_ref, m_out_ref,
                  c_s, n_s, m_s, tail_s, uext_s, *, chunk):
    L = chunk
    step = pl.program_id(1)

    @pl.when(step == 0)
    def _():
        c_s[...] = c0_ref[0]
        n_s[...] = n0_ref[0]
        m_s[...] = m0_ref[0]
        tail_s[...] = tail0_ref[0]

    u = u_ref[0]
    uext_s[0:8, :] = tail_s[...]
    uext_s[8:8 + L, :] = u
    acc = cb_ref[...] + uext_s[pl.ds(5, L), :] * cw_ref[0:1, :]
    acc = acc + uext_s[pl.ds(6, L), :] * cw_ref[1:2, :]
    acc = acc + uext_s[pl.ds(7, L), :] * cw_ref[2:3, :]
    acc = acc + u * cw_ref[3:4, :]
    tail_s[...] = uext_s[pl.ds(L, 8), :]
    qk = acc * jax.nn.sigmoid(acc)

    gc = gc_ref[0]
    gr = gr_ref[0]
    row = lax.broadcasted_iota(jnp.int32, (L, L), 0)
    col = lax.broadcasted_iota(jnp.int32, (L, L), 1)
    causal = col <= row
    tri = causal.astype(F32)
    tri_t = (row <= col).astype(F32)
    lane = lax.broadcasted_iota(jnp.int32, gc.shape, 1)
    srow = lax.broadcasted_iota(jnp.int32, gr.shape, 0)
    b_col_all = jnp.dot(tri, jnp.where(lane >= H_M, gc, 0.0), preferred_element_type=F32,
                        precision=lax.Precision.HIGHEST)
    b_row_all = jnp.dot(jnp.where(srow >= H_M, gr, 0.0), tri_t, preferred_element_type=F32,
                        precision=lax.Precision.HIGHEST)

    v_all = v_ref[0]
    og = og_ref[0]
    for h in range(H_M):
        q = qk[:, h * DK_M:(h + 1) * DK_M]
        k = qk[:, D_M + h * DK_M:D_M + (h + 1) * DK_M] * (DK_M ** -0.5)
        v = v_all[:, h * DK_M:(h + 1) * DK_M]
        li_row = gr[h:h + 1, :]
        li_col = gc[:, h:h + 1]
        b_row = b_row_all[H_M + h:H_M + h + 1, :]
        b_col = b_col_all[:, H_M + h:H_M + h + 1]
        m0 = m_s[h:h + 1, 0:1]
        c0 = c_s[h]
        n0 = n_s[h:h + 1, :]

        d = jnp.where(causal, b_col - b_row + li_row, NEG_INF)
        g_state = b_col + m0
        m = jnp.maximum(jnp.max(d, axis=-1, keepdims=True), g_state)
        w_state = jnp.exp(g_state - m)
        qb = q.astype(BF16)
        kb = k.astype(BF16)
        vb = v.astype(BF16)
        s = lax.dot_general(qb, kb, (((1,), (1,)), ((), ())), preferred_element_type=F32)
        s = s * jnp.exp(d - m)
        num = jnp.dot(s.astype(BF16), vb, preferred_element_type=F32)
        num = num + w_state * jnp.dot(qb, c0.astype(BF16), preferred_element_type=F32)
        den = jnp.sum(s, axis=-1, keepdims=True) + w_state * jnp.sum(q * n0, axis=-1, keepdims=True)
        hh = num / jnp.maximum(jnp.abs(den), jnp.exp(-m))

        b_last = b_col[L - 1:L, :]
        g_tok_row = b_last - b_row + li_row
        m_new = jnp.maximum(b_last + m0, jnp.max(g_tok_row, axis=-1, keepdims=True))
        w_tok_col = jnp.exp(b_last - b_col + li_col - m_new)
        decay = jnp.exp(b_last + m0 - m_new)
        kw = k * w_tok_col
        c_s[h] = decay * c0 + jnp.dot(kw.T.astype(BF16), vb, preferred_element_type=F32)
        n_s[h:h + 1, :] = decay * n0 + jnp.sum(kw, axis=0, keepdims=True)
        m_s[h:h + 1, :] = jnp.broadcast_to(m_new, (1, LANES))

        hn = hh * lax.rsqrt(jnp.mean(hh * hh, axis=-1, keepdims=True) + EPS)
        hn = hn * mh_ref[:, h * DK_M:(h + 1) * DK_M]
        mem_ref[0, :, h * DK_M:(h + 1) * DK_M] = og[:, h * DK_M:(h + 1) * DK_M] * hn

    @pl.when(step == pl.num_programs(1) - 1)
    def _():
        c_out_ref[0] = c_s[...]
        n_out_ref[0] = n_s[...]
        m_out_ref[0] = m_s[...]


def _mlstm(u, v, og, gates_col, gates_row, tail0, c0, n0, m0x, conv_w, conv_b, mh_norm, chunk):
    nb, rows, _ = u.shape
    nc = rows // chunk
    row_map = lambda b, c: (b, c, 0)
    bat3 = lambda b, c: (b, 0, 0)
    const2 = lambda b, c: (0, 0)
    return pl.pallas_call(
        functools.partial(_mlstm_kernel, chunk=chunk),
        grid=(nb, nc),
        in_specs=[
            pl.BlockSpec((1, chunk, 2 * D_M), row_map),
            pl.BlockSpec((1, chunk, D_M), row_map),
            pl.BlockSpec((1, chunk, D_M), row_map),
            pl.BlockSpec((1, chunk, LANES), row_map),
            pl.BlockSpec((1, 8, chunk), lambda b, c: (b, 0, c)),
            pl.BlockSpec((1, 8, 2 * D_M), bat3),
            pl.BlockSpec((1, H_M, DK_M, DK_M), lambda b, c: (b, 0, 0, 0)),
            pl.BlockSpec((1, H_M, DK_M), bat3),
            pl.BlockSpec((1, H_M, LANES), bat3),
            pl.BlockSpec((CONV_W, 2 * D_M), const2),
            pl.BlockSpec((1, 2 * D_M), const2),
            pl.BlockSpec((1, D_M), const2),
        ],
        out_specs=[
            pl.BlockSpec((1, chunk, D_M), row_map),
            pl.BlockSpec((1, H_M, DK_M, DK_M), lambda b, c: (b, 0, 0, 0)),
            pl.BlockSpec((1, H_M, DK_M), bat3),
            pl.BlockSpec((1, H_M, LANES), bat3),
        ],
        out_shape=[
            jax.ShapeDtypeStruct((nb, rows, D_M), F32),
            jax.ShapeDtypeStruct((nb, H_M, DK_M, DK_M), F32),
            jax.ShapeDtypeStruct((nb, H_M, DK_M), F32),
            jax.ShapeDtypeStruct((nb, H_M, LANES), F32),
        ],
        scratch_shapes=[
            pltpu.VMEM((H_M, DK_M, DK_M), F32),
            pltpu.VMEM((H_M, DK_M), F32),
            pltpu.VMEM((H_M, LANES), F32),
            pltpu.VMEM((8, 2 * D_M), F32),
            pltpu.VMEM((chunk + 8, 2 * D_M), F32),
        ],
        compiler_params=_params(("arbitrary", "arbitrary")),
        name="mlstm",
    )(u, v, og, gates_col, gates_row, tail0, c0, n0, m0x,
      conv_w, conv_b.reshape(1, 2 * D_M), mh_norm.reshape(1, D_M))


def _attn_kernel(q_ref, kc_ref, vc_ref, kp_ref, vp_ref, bias_ref, o_ref, l_ref, k_s, v_s, *, ql):
    first = pl.program_id(2) == 0
    k_s[0:QB, :] = kp_ref[0].astype(BF16)
    k_s[QB:QB + ql, :] = kc_ref[0].astype(BF16)
    v_s[0:QB, :] = vp_ref[0].astype(BF16)
    v_s[QB:QB + ql, :] = vc_ref[0].astype(BF16)
    col = lax.broadcasted_iota(jnp.int32, (QB, 2 * QB), 1)
    pad_keys = jnp.logical_and(first, col < QB)
    for h in range(H_A):
        lo = h * HD_A
        bias = bias_ref[h]
        for j in range(ql // QB):
            qb = q_ref[0, j * QB:(j + 1) * QB, lo:lo + HD_A].astype(BF16)
            kb = k_s[j * QB:(j + 2) * QB, lo:lo + HD_A]
            vb = v_s[j * QB:(j + 2) * QB, lo:lo + HD_A]
            logits = lax.dot_general(qb, kb, (((1,), (1,)), ((), ())), preferred_element_type=F32)
            logits = logits + bias
            if j == 0:
                logits = jnp.where(pad_keys, NEG_INF, logits)
            m = jnp.max(logits, axis=-1, keepdims=True)
            p = jnp.exp(logits - m)
            s = jnp.sum(p, axis=-1, keepdims=True)
            o = jnp.dot(p.astype(BF16), vb, preferred_element_type=F32) / s
            o_ref[0, j * QB:(j + 1) * QB, lo:lo + HD_A] = o
            l_ref[0, j * QB:(j + 1) * QB, lo:lo + HD_A] = jnp.broadcast_to(m + jnp.log(s), (QB, HD_A))


def _attn_branch(qa, ka, va, bias, dil):
    nb, ld, _ = qa.shape
    ql = min(ATT_QL, ld)
    cur = lambda b, r, i: (b, i, r)
    prev = lambda b, r, i: (b, jnp.maximum(i * (ql // QB) - 1, 0), r)
    return pl.pallas_call(
        functools.partial(_attn_kernel, ql=ql),
        grid=(nb, dil, ld // ql),
        in_specs=[
            pl.BlockSpec((1, ql, D_A), cur),
            pl.BlockSpec((1, ql, D_A), cur),
            pl.BlockSpec((1, ql, D_A), cur),
            pl.BlockSpec((1, QB, D_A), prev),
            pl.BlockSpec((1, QB, D_A), prev),
            pl.BlockSpec((H_A, QB, 2 * QB), lambda b, r, i: (0, 0, 0)),
        ],
        out_specs=[pl.BlockSpec((1, ql, D_A), cur), pl.BlockSpec((1, ql, D_A), cur)],
        out_shape=[jax.ShapeDtypeStruct((nb, ld, dil * D_A), F32)] * 2,
        scratch_shapes=[pltpu.VMEM((QB + ql, D_A), BF16), pltpu.VMEM((QB + ql, D_A), BF16)],
        compiler_params=_params(("arbitrary", "arbitrary", "arbitrary")),
        name=f"attn_d{dil}",
    )(qa, ka, va, ka, va, bias)


def _heads_kernel(k_ref, v_ref, ko_ref, vo_ref):
    for src, dst in ((k_ref, ko_ref), (v_ref, vo_ref)):
        for c in range(D_A // LANES):
            xt = src[0, :, c * LANES:(c + 1) * LANES].T
            dst[0, 2 * c] = xt[:HD_A]
            dst[0, 2 * c + 1] = xt[HD_A:]


def _window_heads(ka, va, keep, tm=512):
    nb, seq, _ = ka.shape
    first = (seq - keep) // tm
    in_spec = pl.BlockSpec((1, tm, D_A), lambda b, i: (b, first + i, 0))
    out_spec = pl.BlockSpec((1, H_A, HD_A, tm), lambda b, i: (b, 0, 0, i))
    kt, vt = pl.pallas_call(
        _heads_kernel,
        grid=(nb, keep // tm),
        in_specs=[in_spec, in_spec],
        out_specs=[out_spec, out_spec],
        out_shape=[jax.ShapeDtypeStruct((nb, H_A, HD_A, keep), F32)] * 2,
        compiler_params=_params(("arbitrary", "arbitrary")),
        name="window_heads",
    )(ka, va)
    return kt.transpose(0, 3, 1, 2), vt.transpose(0, 3, 1, 2)


def _t5_bucket(dist):
    max_exact = N_BUCKETS // 2
    n = jnp.maximum(dist, 1).astype(F32)
    large = max_exact + (jnp.log(n / max_exact) / math.log(MAX_DISTANCE / max_exact)
                         * (N_BUCKETS - max_exact)).astype(jnp.int32)
    large = jnp.minimum(large, N_BUCKETS - 1)
    return jnp.where(dist < max_exact, dist, large)


def _branch_bias_vec(rel_bias, window, dil):
    return rel_bias[_t5_bucket(dil * jnp.arange(window // dil + 1))].astype(F32)


def _prompt_bias(rel_bias, window, dil):
    assert window // dil == QB
    n = 2 * QB
    vec = _branch_bias_vec(rel_bias, window, dil)
    vecp = jnp.concatenate([vec[::-1], jnp.full((n - QB - 1, H_A), NEG_INF, F32)], axis=0)
    g = jnp.roll(vecp, QB - 1, axis=0)
    skew = jnp.tile(g, (QB + 1, 1))[:QB * (n + 1)].reshape(QB, n + 1, H_A)[:, :n]
    return skew[::-1].transpose(2, 0, 1)


TOK_PAD = 8


def _sattn_kernel(q_ref, kt_ref, vt_ref, knt_ref, vnt_ref, bc_ref, bn_ref, att_ref):
    nt = (((1,), (1,)), ((), ()))
    for h in range(H_A):
        q = q_ref[0, h].astype(BF16)
        lc = jnp.dot(q, kt_ref[0, h].astype(BF16), preferred_element_type=F32)
        ln = jnp.dot(q, knt_ref[0, h].astype(BF16), preferred_element_type=F32)
        lcs = [lc + bc_ref[d, h] for d in range(len(BRANCHES))]
        lns = [ln + bn_ref[d, h] for d in range(len(BRANCHES))]
        m = functools.reduce(jnp.maximum, [jnp.max(x, axis=-1, keepdims=True) for x in lcs + lns])
        pc = functools.reduce(jnp.add, [jnp.exp(x - m) for x in lcs])
        pn = functools.reduce(jnp.add, [jnp.exp(x - m) for x in lns])
        s = jnp.sum(pc, axis=-1, keepdims=True) + jnp.sum(pn, axis=-1, keepdims=True)
        o = lax.dot_general(pc.astype(BF16), vt_ref[0, h].astype(BF16), nt, preferred_element_type=F32)
        o = o + lax.dot_general(pn.astype(BF16), vnt_ref[0, h].astype(BF16), nt, preferred_element_type=F32)
        att_ref[0, h] = o / s


def _sample_attention(q_h, kt, vt, knt, vnt, bias_c, bias_n):
    nb, _, _, n_past = kt.shape
    bat = lambda b: (b, 0, 0, 0)
    const = lambda b: (0, 0, 0, 0)
    return pl.pallas_call(
        _sattn_kernel,
        grid=(nb,),
        in_specs=[
            pl.BlockSpec((1, H_A, TOK_PAD, HD_A), bat),
            pl.BlockSpec((1, H_A, HD_A, n_past), bat),
            pl.BlockSpec((1, H_A, HD_A, n_past), bat),
            pl.BlockSpec((1, H_A, HD_A, LANES), bat),
            pl.BlockSpec((1, H_A, HD_A, LANES), bat),
            pl.BlockSpec(bias_c.shape, const),
            pl.BlockSpec(bias_n.shape, const),
        ],
        out_specs=pl.BlockSpec((1, H_A, TOK_PAD, HD_A), bat),
        out_shape=jax.ShapeDtypeStruct((nb, H_A, TOK_PAD, HD_A), F32),
        compiler_params=_params(("arbitrary",)),
        name="sample_attn",
    )(q_h, kt, vt, knt, vnt, bias_c, bias_n)


def _sample_bias(rel_bias, n_tok, n_past):
    def table(key_pos):
        out = []
        for window, dil in BRANCHES:
            nk = window // dil
            vec = _branch_bias_vec(rel_bias, window, dil)
            delta = (n_past + jnp.arange(n_tok))[:, None] - key_pos[None, :]
            ok = (delta >= 0) & (delta % dil == 0) & (delta // dil <= nk)
            b = jnp.where(ok[..., None], vec[jnp.clip(delta // dil, 0, nk)], NEG_INF)
            b = jnp.concatenate([b, jnp.zeros((TOK_PAD - n_tok,) + b.shape[1:], F32)], axis=0)
            out.append(b.transpose(2, 0, 1))
        return jnp.stack(out)

    lane = jnp.arange(LANES)
    beyond = n_past + n_tok + WINDOW_MAX
    return table(jnp.arange(n_past)), table(jnp.where(lane < n_tok, n_past + lane, beyond))


def _postmix_kernel(mem_ref, o1_ref, l1_ref, o2_ref, l2_ref, o3_ref, l3_ref, x_ref,
                    g1_ref, sc2_ref, sh2_ref, gpost_ref, gpre_ref, wo_ref, wr_ref, br_ref,
                    x1_ref, h2_ref, idx_ref, gate_ref, *, dilations):
    rows = lambda ref, dil: ref[0] if dil == 1 else _from_dilated(ref[0], dil)
    o1, o2, o3 = (rows(r, d) for r, d in zip((o1_ref, o2_ref, o3_ref), dilations))
    l1, l2, l3 = (rows(r, d) for r, d in zip((l1_ref, l2_ref, l3_ref), dilations))
    lmax = jnp.maximum(jnp.maximum(l1, l2), l3)
    w1 = jnp.exp(l1 - lmax)
    w2 = jnp.exp(l2 - lmax)
    w3 = jnp.exp(l3 - lmax)
    att = (w1 * o1 + w2 * o2 + w3 * o3) / (w1 + w2 + w3)
    y = jnp.dot(mem_ref[0].astype(BF16), wo_ref[0:D_M, :], preferred_element_type=F32)
    y = y + jnp.dot(att.astype(BF16), wo_ref[D_M:, :], preferred_element_type=F32)
    x1 = x_ref[0] + g1_ref[0] * _rms(y, gpost_ref[...])
    x1_ref[0] = x1
    h2 = _rms(x1, gpre_ref[...]) * (1.0 + sc2_ref[0]) + sh2_ref[0]
    h2_ref[0] = h2
    h_hi = h2.astype(BF16)
    h_lo = (h2 - h_hi.astype(F32)).astype(BF16)
    w_hi = wr_ref[0]
    logits = jnp.dot(h_hi, w_hi, preferred_element_type=F32)
    logits = logits + jnp.dot(h_lo, w_hi, preferred_element_type=F32)
    logits = logits + jnp.dot(h_hi, wr_ref[1], preferred_element_type=F32) + br_ref[...]
    lane = lax.broadcasted_iota(jnp.int32, logits.shape, 1)
    lane_f = lane.astype(F32)
    logits = jnp.where(lane < N_EXPERTS, logits, NEG_INF)
    vals, idxs = [], []
    for _ in range(TOP_K):
        mx = jnp.max(logits, axis=-1, keepdims=True)
        ix = jnp.min(jnp.where(logits == mx, lane_f, float(LANES)), axis=-1, keepdims=True)
        vals.append(mx)
        idxs.append(ix)
        logits = jnp.where(lane_f == ix, NEG_INF, logits)
    es = [jnp.exp(v - vals[0]) for v in vals]
    tot = es[0] + es[1] + es[2] + es[3]
    idx_tile = jnp.zeros(logits.shape, F32)
    gate_tile = jnp.zeros(logits.shape, F32)
    for k in range(TOP_K):
        idx_tile = jnp.where(lane == k, idxs[k], idx_tile)
        gate_tile = jnp.where(lane == k, es[k] / tot, gate_tile)
    idx_ref[0] = idx_tile.astype(jnp.int32)
    gate_ref[0] = gate_tile


def _postmix(mem, branches, x3, gate1, scale2, shift2, g_post, g_pre, w_out_b, w_router_p, b_router_p, tm,
             dilations=(1, 1, 1)):
    nb, rows, _ = x3.shape
    per_row_mod = gate1.shape[1] != 1
    mod_block = (1, tm, D_MODEL) if per_row_mod else (1, 1, D_MODEL)
    mod_map = (lambda b, i: (b, i, 0)) if per_row_mod else (lambda b, i: (b, 0, 0))
    row_map = lambda b, i: (b, i, 0)
    const2 = lambda b, i: (0, 0)
    half = pl.BlockSpec((1, tm, D_M), row_map)
    flat = [a for pair in branches for a in pair]
    branch_specs = [pl.BlockSpec((1, tm // d, d * D_A), row_map) for d in dilations for _ in range(2)]
    return pl.pallas_call(
        functools.partial(_postmix_kernel, dilations=tuple(dilations)),
        grid=(nb, rows // tm),
        in_specs=[half] + branch_specs + [
            pl.BlockSpec((1, tm, D_MODEL), row_map),
            pl.BlockSpec(mod_block, mod_map),
            pl.BlockSpec(mod_block, mod_map),
            pl.BlockSpec(mod_block, mod_map),
            pl.BlockSpec((1, D_MODEL), const2),
            pl.BlockSpec((1, D_MODEL), const2),
            pl.BlockSpec((D_MODEL, D_MODEL), const2),
            pl.BlockSpec((2, D_MODEL, LANES), lambda b, i: (0, 0, 0)),
            pl.BlockSpec((1, LANES), const2),
        ],
        out_specs=[
            pl.BlockSpec((1, tm, D_MODEL), row_map),
            pl.BlockSpec((1, tm, D_MODEL), row_map),
            pl.BlockSpec((1, tm, LANES), row_map),
            pl.BlockSpec((1, tm, LANES), row_map),
        ],
        out_shape=[
            jax.ShapeDtypeStruct((nb, rows, D_MODEL), F32),
            jax.ShapeDtypeStruct((nb, rows, D_MODEL), F32),
            jax.ShapeDtypeStruct((nb, rows, LANES), jnp.int32),
            jax.ShapeDtypeStruct((nb, rows, LANES), F32),
        ],
        compiler_params=_params(("arbitrary", "arbitrary")),
        name="postmix",
    )(mem, *flat, x3, gate1, scale2, shift2, g_post.reshape(1, D_MODEL), g_pre.reshape(1, D_MODEL),
      w_out_b, w_router_p, b_router_p)


def _row_wait(src_rows, dst_rows, sem):
    pltpu.make_async_copy(src_rows, dst_rows, sem).wait()


def _scatter_kernel(slot_ref, h_ref, xs_hbm, buf, sem):
    bits = pltpu.bitcast(h_ref[...].astype(BF16).astype(F32), U32)
    packed = (bits[:, :PACK_W] & U32(HI_MASK)) | (bits[:, PACK_W:] >> 16)
    buf[...] = pltpu.einshape("m(jc)->mjc", packed, j=PACK_T)

    def issue(r, carry):
        src = buf.at[pl.ds(r, 1)]
        for k in range(TOP_K):
            pltpu.make_async_copy(src, xs_hbm.at[pl.ds(slot_ref[0, 0, r * TOP_K + k], 1)],
                                  sem).start(priority=k % 2)
        return carry

    lax.fori_loop(0, TOK_TILE, issue, 0, unroll=4)
    for _ in range(TOP_K):
        _row_wait(buf, xs_hbm.at[pl.ds(0, TOK_TILE)], sem)


def _moe_scatter(h_all, slot_tiles):
    tiles = slot_tiles.shape[0]
    n_rows = tiles * TOK_TILE * TOP_K
    return pl.pallas_call(
        _scatter_kernel,
        grid=(tiles,),
        in_specs=[
            pl.BlockSpec((1, 1, TOK_TILE * TOP_K), lambda i: (i, 0, 0), memory_space=pltpu.SMEM),
            pl.BlockSpec((TOK_TILE, D_MODEL), lambda i: (i, 0)),
        ],
        out_specs=pl.BlockSpec(memory_space=pl.ANY),
        out_shape=jax.ShapeDtypeStruct((n_rows, PACK_T, LANES), U32),
        scratch_shapes=[pltpu.VMEM((TOK_TILE, PACK_T, LANES), U32), pltpu.SemaphoreType.DMA],
        compiler_params=_params(("arbitrary",)),
        name="moe_scatter",
    )(slot_tiles, h_all)


def _expert_kernel(tile_ref, exp_ref, lo_ref, hi_ref, first_ref, nvis_ref,
                   xs_ref, w1_ref, b1_ref, w2_ref, b2_ref, ys_ref, w1b, w2b):
    v = pl.program_id(0)
    active = v < nvis_ref[0]
    changed = jnp.logical_or(v == 0, exp_ref[v] != exp_ref[jnp.maximum(v - 1, 0)])
    lo = lo_ref[v]
    hi = hi_ref[v]

    @pl.when(jnp.logical_and(active, changed))
    def _():
        w1b[...] = w1_ref[0].astype(BF16)
        w2b[...] = w2_ref[0].astype(BF16)

    @pl.when(jnp.logical_and(active, first_ref[v] == 1))
    def _():
        ys_ref[...] = jnp.zeros(ys_ref.shape, F32)

    for r in range(MOE_M // MOE_SUB):
        r0 = r * MOE_SUB

        @pl.when(jnp.logical_and(active, jnp.logical_and(lo < r0 + MOE_SUB, hi > r0)))
        def _():
            w = pltpu.einshape("mjc->m(jc)", xs_ref[r0:r0 + MOE_SUB])
            x_hi = pltpu.bitcast(w & U32(HI_MASK), F32).astype(BF16)
            x_lo = pltpu.bitcast(w << 16, F32).astype(BF16)
            hgu = jnp.dot(x_hi, w1b[0:PACK_W, :], preferred_element_type=F32)
            hgu = hgu + jnp.dot(x_lo, w1b[PACK_W:, :], preferred_element_type=F32) + b1_ref[0]
            x_glu = jnp.minimum(hgu[:, :D_FF], SWIGLU_LIMIT)
            x_lin = jnp.clip(hgu[:, D_FF:], -SWIGLU_LIMIT, SWIGLU_LIMIT)
            act = x_glu * jax.nn.sigmoid(SWIGLU_ALPHA * x_glu) * (x_lin + 1.0)
            y = jnp.dot(act.astype(BF16), w2b[...], preferred_element_type=F32) + b2_ref[0]
            rows = r0 + lax.broadcasted_iota(jnp.int32, (MOE_SUB, 1, 1), 0)
            mine = jnp.logical_and(rows >= lo, rows < hi)
            y_rows = pltpu.einshape("m(jc)->mjc", y, j=ROW_T)
            ys_ref[r0:r0 + MOE_SUB] = jnp.where(mine, y_rows, ys_ref[r0:r0 + MOE_SUB])


def _moe_experts(xs, visits, w1, b1, w2, b2):
    n_rows = xs.shape[0]
    n_vis = visits[0].shape[0]
    tile_map = lambda v, tile, *_: (tile[v], 0, 0)
    exp_map = lambda v, tile, exp, *_: (exp[v], 0, 0)
    return pl.pallas_call(
        _expert_kernel,
        grid_spec=pltpu.PrefetchScalarGridSpec(
            num_scalar_prefetch=6,
            grid=(n_vis,),
            in_specs=[
                pl.BlockSpec((MOE_M, PACK_T, LANES), tile_map),
                pl.BlockSpec((1, D_MODEL, 2 * D_FF), exp_map),
                pl.BlockSpec((1, 1, 2 * D_FF), exp_map),
                pl.BlockSpec((1, D_FF, D_MODEL), exp_map),
                pl.BlockSpec((1, 1, D_MODEL), exp_map),
            ],
            out_specs=pl.BlockSpec((MOE_M, ROW_T, LANES), tile_map),
            scratch_shapes=[pltpu.VMEM((D_MODEL, 2 * D_FF), BF16), pltpu.VMEM((D_FF, D_MODEL), BF16)],
        ),
        out_shape=jax.ShapeDtypeStruct((n_rows, ROW_T, LANES), F32),
        compiler_params=_params(("arbitrary",)),
        name="moe_experts",
    )(*visits, xs, w1, b1.reshape(N_EXPERTS, 1, 2 * D_FF), w2, b2.reshape(N_EXPERTS, 1, D_MODEL))


def _combine_kernel(slot_ref, gate_ref, x1p_ref, x1s_ref, g2p_ref, g2s_ref, gpost_ref, ys_hbm,
                    yp_ref, ysm_ref, buf, sem, *, n_prompt_tiles):
    i = pl.program_id(0)

    def issue(r, carry):
        for k in range(TOP_K):
            pltpu.make_async_copy(ys_hbm.at[pl.ds(slot_ref[0, 0, r * TOP_K + k], 1)],
                                  buf.at[pl.ds(k * TOK_TILE + r, 1)], sem).start(priority=k % 2)
        return carry

    lax.fori_loop(0, TOK_TILE, issue, 0, unroll=4)
    for k in range(TOP_K):
        _row_wait(ys_hbm.at[pl.ds(0, TOK_TILE)], buf.at[pl.ds(k * TOK_TILE, TOK_TILE)], sem)
    g = gate_ref[...]
    f = jnp.zeros((TOK_TILE, D_MODEL), F32)
    for k in range(TOP_K):
        rows = pltpu.einshape("mjc->m(jc)", buf[k * TOK_TILE:(k + 1) * TOK_TILE])
        f = f + rows * g[:, k:k + 1]
    is_sample = i >= n_prompt_tiles
    x1 = jnp.where(is_sample, x1s_ref[...], x1p_ref[...])
    g2 = jnp.where(is_sample, g2s_ref[...], g2p_ref[0])
    y = x1 + g2 * _rms(f, gpost_ref[...])

    @pl.when(jnp.logical_not(is_sample))
    def _():
        yp_ref[...] = y

    @pl.when(is_sample)
    def _():
        ysm_ref[...] = y


def _moe_combine(ys, slot_tiles, gates, x1_p, x1_s, gate2_p, gate2_s, g_post, rows_per_batch):
    tiles = slot_tiles.shape[0]
    n_p = x1_p.shape[0] // TOK_TILE
    assert tiles == n_p + 1 and x1_s.shape[0] == TOK_TILE
    tiles_per_batch = rows_per_batch // TOK_TILE
    p_tile = lambda i: (jnp.minimum(i, n_p - 1), 0)
    const2 = lambda i: (0, 0)
    return pl.pallas_call(
        functools.partial(_combine_kernel, n_prompt_tiles=n_p),
        grid=(tiles,),
        in_specs=[
            pl.BlockSpec((1, 1, TOK_TILE * TOP_K), lambda i: (i, 0, 0), memory_space=pltpu.SMEM),
            pl.BlockSpec((TOK_TILE, LANES), lambda i: (i, 0)),
            pl.BlockSpec((TOK_TILE, D_MODEL), p_tile),
            pl.BlockSpec((TOK_TILE, D_MODEL), const2),
            pl.BlockSpec((1, 1, D_MODEL), lambda i: (jnp.minimum(i, n_p - 1) // tiles_per_batch, 0, 0)),
            pl.BlockSpec((TOK_TILE, D_MODEL), const2),
            pl.BlockSpec((1, D_MODEL), const2),
            pl.BlockSpec(memory_space=pl.ANY),
        ],
        out_specs=[pl.BlockSpec((TOK_TILE, D_MODEL), p_tile), pl.BlockSpec((TOK_TILE, D_MODEL), const2)],
        out_shape=[jax.ShapeDtypeStruct(x1_p.shape, F32), jax.ShapeDtypeStruct(x1_s.shape, F32)],
        scratch_shapes=[pltpu.VMEM((TOP_K * TOK_TILE, ROW_T, LANES), F32), pltpu.SemaphoreType.DMA],
        compiler_params=_params(("arbitrary",)),
        name="moe_combine",
    )(slot_tiles, gates, x1_p, x1_s, gate2_p, gate2_s, g_post.reshape(1, D_MODEL), ys)


def _moe_plan(top_idx):
    a = top_idx.shape[0] * TOP_K
    assert a % MOE_M == 0
    flat_e = top_idx[:, :TOP_K].reshape(a)
    onehot = (flat_e[:, None] == jnp.arange(N_EXPERTS)[None, :]).astype(jnp.int32)
    csum = jnp.cumsum(onehot, axis=0)
    counts = csum[-1]
    g_end = jnp.cumsum(counts)
    g_start = g_end - counts
    rank = jnp.sum(csum * onehot, axis=1) - 1
    slot_of = (jnp.sum(g_start[None, :] * onehot, axis=1) + rank).astype(jnp.int32)

    n_tiles = a // MOE_M
    n_vis = n_tiles + N_EXPERTS - 1
    first_tile = g_start // MOE_M
    last_tile = jnp.where(counts > 0, (g_end - 1) // MOE_M, first_tile)
    nvis_e = jnp.where(counts > 0, last_tile - first_tile + 1, 0)
    v_end = jnp.cumsum(nvis_e)
    v_start = v_end - nvis_e
    total = v_end[-1]
    v = jnp.arange(n_vis)
    vc = jnp.minimum(v, total - 1)
    e_of = jnp.minimum(jnp.sum((vc[:, None] >= v_end[None, :]).astype(jnp.int32), axis=1), N_EXPERTS - 1)
    tile_of = first_tile[e_of] + (vc - v_start[e_of])
    lo = jnp.maximum(g_start[e_of], tile_of * MOE_M) - tile_of * MOE_M
    hi = jnp.minimum(g_end[e_of], (tile_of + 1) * MOE_M) - tile_of * MOE_M
    prev_tile = jnp.concatenate([jnp.full((1,), -1, tile_of.dtype), tile_of[:-1]])
    first = (tile_of != prev_tile).astype(jnp.int32)
    i32 = lambda t: t.astype(jnp.int32)
    visits = (i32(tile_of), i32(e_of), i32(lo), i32(hi), first, i32(total).reshape(1))
    return slot_of, visits


def _reorder_w_in(w_in):
    off_om_end = 4 * D_M
    off_g_end = off_om_end + 2 * H_M
    w = jnp.concatenate([
        w_in[:, :off_om_end],
        w_in[:, off_g_end:],
        w_in[:, off_om_end:off_g_end],
        jnp.zeros((D_MODEL, LANES - 2 * H_M), w_in.dtype),
    ], axis=1)
    return w.astype(BF16)


def kernel(x_prompt, x_sample, cache_k, cache_v, state_conv, state_C, state_n, state_m, c_prompt, c_sample, w_ada, b_ada, g_pre_mix, g_post_mix, g_pre_ffn, g_post_ffn, w_in, b_ig, b_fg, conv_w, conv_b, mh_norm, rel_bias, w_out, w_router, b_router, w1, b1, w2, b2):
    depth = w_in.shape[0]
    assert depth == 1
    l = 0
    nb_p, seq, _ = x_prompt.shape
    nb_s, n_tok, _ = x_sample.shape
    rows_s = nb_s * n_tok
    assert rows_s == TOK_TILE

    n_c = nb_p + nb_s
    c_pad = -(-n_c // 8) * 8
    c_all = jnp.concatenate([c_prompt, c_sample, jnp.zeros((c_pad - n_c, D_MODEL), F32)], axis=0)
    mod = _adaln(c_all, w_ada[l], b_ada[l])
    mods = [mod[:, i * D_MODEL:(i + 1) * D_MODEL] for i in range(6)]
    mods_p = [m[:nb_p].reshape(nb_p, 1, D_MODEL) for m in mods]
    mods_s = [jnp.repeat(m[nb_p:n_c], n_tok, axis=0).reshape(1, rows_s, D_MODEL) for m in mods]

    w_r = _reorder_w_in(w_in[l])
    bias_g = jnp.concatenate([b_ig[l], b_fg[l], jnp.zeros((LANES - 2 * H_M,), F32)]).reshape(1, LANES)
    w_out_b = w_out[l].astype(BF16)
    w_router_f = jnp.concatenate([w_router[l], jnp.zeros((D_MODEL, LANES - N_EXPERTS), F32)], axis=1)
    w_router_hi = w_router_f.astype(BF16)
    w_router_p = jnp.stack([w_router_hi, (w_router_f - w_router_hi.astype(F32)).astype(BF16)])
    b_router_p = jnp.concatenate([b_router[l].astype(F32), jnp.zeros((LANES - N_EXPERTS,), F32)]).reshape(1, LANES)

    shift1, scale1, gate1, shift2, scale2, gate2_p = mods_p
    dils = tuple(d for _, d in BRANCHES)
    assert dils[0] == 1
    u, v_m, og, gts, qa, ka, va, *dilated = _inproj(x_prompt, g_pre_mix[l], scale1, shift1, w_r, bias_g, tm=512,
                                                   dilations=dils[1:])
    qkv_d = [(qa, ka, va)] + [tuple(dilated[3 * n:3 * n + 3]) for n in range(len(dils) - 1)]
    gates_row = gts[:, :, :8].transpose(0, 2, 1)
    zeros = lambda *s: jnp.zeros(s, F32)
    mem, c_p, n_p, m_p = _mlstm(u, v_m, og, gts, gates_row, zeros(nb_p, 8, 2 * D_M),
                                zeros(nb_p, H_M, DK_M, DK_M), zeros(nb_p, H_M, DK_M), zeros(nb_p, H_M, LANES),
                                conv_w[l], conv_b[l], mh_norm[l], MLSTM_L_PROMPT)
    branches = [_attn_branch(*qkv, _prompt_bias(rel_bias, w, d), d) for qkv, (w, d) in zip(qkv_d, BRANCHES)]
    x1_p, h2_p, idx_p, gate_p = _postmix(mem, branches, x_prompt, gate1, scale2, shift2,
                                         g_post_mix[l], g_pre_ffn[l], w_out_b, w_router_p, b_router_p, tm=256,
                                         dilations=dils)
    keep = min(WINDOW_MAX, seq)
    k_p, v_p = _window_heads(ka, va, keep)
    conv_p = u[:, seq - (CONV_W - 1):]

    shift1, scale1, gate1, shift2, scale2, gate2_s = mods_s
    x_s = x_sample.reshape(1, rows_s, D_MODEL)
    u_s, v_s, og_s, gts_s, qa_s, ka_s, va_s = _inproj(x_s, g_pre_mix[l], scale1, shift1, w_r, bias_g, tm=rows_s)
    per_b = lambda t: t.reshape(nb_s, n_tok, t.shape[-1])
    ls = MLSTM_L_SAMPLE
    pad_rows = lambda t, n: jnp.concatenate([t, jnp.zeros((nb_s, n - t.shape[1], t.shape[2]), t.dtype)], axis=1)
    inert = jnp.concatenate([jnp.full((H_M,), NEG_INF, F32), jnp.zeros((LANES - H_M,), F32)])
    gts_pad = jnp.concatenate([per_b(gts_s), jnp.broadcast_to(inert, (nb_s, ls - n_tok, LANES))], axis=1)
    tail0 = jnp.concatenate([zeros(nb_s, 8 - (CONV_W - 1), 2 * D_M), state_conv[l].astype(F32)], axis=1)
    m0x = jnp.broadcast_to(state_m[l].astype(F32)[:, :, None], (nb_s, H_M, LANES))
    mem_s, c_s, n_s, m_s = _mlstm(pad_rows(per_b(u_s), ls), pad_rows(per_b(v_s), ls), pad_rows(per_b(og_s), ls),
                                  gts_pad, gts_pad[:, :, :8].transpose(0, 2, 1), tail0,
                                  state_C[l].astype(F32), state_n[l].astype(F32), m0x,
                                  conv_w[l], conv_b[l], mh_norm[l], ls)
    mem_s = mem_s[:, :n_tok].reshape(1, rows_s, D_M)
    heads = lambda t: t.reshape(nb_s, n_tok, H_A, HD_A)
    k_new, v_new = heads(ka_s), heads(va_s)
    n_past = cache_k.shape[2]
    assert n_tok <= TOK_PAD
    q_h = jnp.pad(heads(qa_s).transpose(0, 2, 1, 3), ((0, 0), (0, 0), (0, TOK_PAD - n_tok), (0, 0)))
    new_t = lambda t: jnp.pad(t.transpose(0, 2, 3, 1), ((0, 0), (0, 0), (0, 0), (0, LANES - n_tok)))
    cache_t = lambda c: c[l].astype(F32).transpose(0, 2, 3, 1)
    att_s = _sample_attention(q_h, cache_t(cache_k), cache_t(cache_v), new_t(k_new), new_t(v_new),
                              *_sample_bias(rel_bias, n_tok, n_past))
    att_s = att_s[:, :, :n_tok].transpose(0, 2, 1, 3).reshape(1, rows_s, D_A)
    zero_o = zeros(1, rows_s, D_A)
    ninf = jnp.full((1, rows_s, D_A), NEG_INF, F32)
    branches_s = [(att_s, zero_o), (zero_o, ninf), (zero_o, ninf)]
    x1_s, h2_s, idx_s, gate_s = _postmix(mem_s, branches_s, x_s, gate1, scale2, shift2,
                                         g_post_mix[l], g_pre_ffn[l], w_out_b, w_router_p, b_router_p, tm=rows_s)

    t_p = nb_p * seq
    h_all = jnp.concatenate([h2_p.reshape(t_p, D_MODEL), h2_s.reshape(rows_s, D_MODEL)], axis=0)
    idx_all = jnp.concatenate([idx_p.reshape(t_p, LANES), idx_s.reshape(rows_s, LANES)], axis=0)
    gate_all = jnp.concatenate([gate_p.reshape(t_p, LANES), gate_s.reshape(rows_s, LANES)], axis=0)
    slot_of, visits = _moe_plan(idx_all)
    slot_tiles = slot_of.reshape((t_p + rows_s) // TOK_TILE, 1, TOK_TILE * TOP_K)
    xs = _moe_scatter(h_all, slot_tiles)
    ys = _moe_experts(xs, visits, w1[l], b1[l], w2[l], b2[l])
    y_p, y_s = _moe_combine(ys, slot_tiles, gate_all, x1_p.reshape(t_p, D_MODEL), x1_s.reshape(rows_s, D_MODEL),
                            gate2_p, gate2_s.reshape(rows_s, D_MODEL), g_post_ffn[l], seq)
    y_prompt = y_p.reshape(nb_p, seq, D_MODEL)
    y_sample = y_s.reshape(nb_s, n_tok, D_MODEL)

    st = lambda t: t[None]
    conv_s = jnp.concatenate([state_conv[l].astype(F32), per_b(u_s)], axis=1)[:, -(CONV_W - 1):]
    return (y_prompt, y_sample, st(k_p), st(v_p), st(conv_p), st(c_p), st(n_p), st(m_p[:, :, 0]),
            st(k_new), st(v_new), st(conv_s), st(c_s), st(n_s), st(m_s[:, :, 0]))
```

```python
import functools
import math

import jax
import jax.numpy as jnp
import numpy as np
from jax import lax
from jax.experimental import pallas as pl
from jax.experimental.pallas import tpu as pltpu

F32 = jnp.float32
BF16 = jnp.bfloat16
U32 = jnp.uint32
NEG_INF = float("-inf")

D_MODEL = 1024
H_M = 4
D_M = 512
DK_M = 128
CONV_W = 4
H_A = 8
D_A = 512
HD_A = 64
BRANCHES = ((128, 1), (512, 4), (2048, 16))
WINDOW_MAX = 2048
QB = 128
N_BUCKETS = 32
MAX_DISTANCE = 2048
N_EXPERTS = 32
TOP_K = 4
D_FF = 1024
SWIGLU_LIMIT = 7.0
SWIGLU_ALPHA = 1.702
EPS = 1e-6

LANES = 128
C_U, C_V, C_O, C_QA, C_KA, C_VA, C_G = 0, 1024, 1536, 2048, 2560, 3072, 3584
D_IN_R = C_G + LANES

MLSTM_L_PROMPT = 256
MLSTM_L_SAMPLE = 128
ATT_QL = 256
MOE_M = 512
MOE_SUB = 256
TOK_TILE = 128
PACK_W = D_MODEL // 2
PACK_T = PACK_W // LANES
ROW_T = D_MODEL // LANES
HI_MASK = 0xFFFF0000
VMEM_LIMIT = 56 * 1024 * 1024


def _params(sem, vmem=None):
    return pltpu.CompilerParams(dimension_semantics=sem, vmem_limit_bytes=vmem or VMEM_LIMIT)


def _rms(x, g):
    return x * lax.rsqrt(jnp.mean(x * x, axis=-1, keepdims=True) + EPS) * g


def _log_sigmoid(t):
    return jnp.minimum(t, 0.0) - jnp.log1p(jnp.exp(-jnp.abs(t)))


def _ada_kernel(c_ref, w_ref, b_ref, o_ref):
    c = c_ref[...]
    a = (c * jax.nn.sigmoid(c)).astype(BF16)
    o_ref[...] = jnp.dot(a, w_ref[...].astype(BF16), preferred_element_type=F32) + b_ref[...]


def _adaln(c_all, w_ada, b_ada):
    rows = c_all.shape[0]
    n = w_ada.shape[1]
    tn = 1024
    return pl.pallas_call(
        _ada_kernel,
        grid=(n // tn,),
        in_specs=[
            pl.BlockSpec((rows, D_MODEL), lambda j: (0, 0)),
            pl.BlockSpec((D_MODEL, tn), lambda j: (0, j)),
            pl.BlockSpec((1, tn), lambda j: (0, j)),
        ],
        out_specs=pl.BlockSpec((rows, tn), lambda j: (0, j)),
        out_shape=jax.ShapeDtypeStruct((rows, n), F32),
        compiler_params=_params(("arbitrary",)),
        name="adaln",
    )(c_all, w_ada, b_ada.reshape(1, n))


def _to_dilated(x, dil):
    return pltpu.einshape("(lr)c->l(rc)", x, r=dil)


def _from_dilated(x, dil):
    return pltpu.einshape("l(rc)->(lr)c", x, r=dil)


def _inproj_kernel(x_ref, g_ref, sc_ref, sh_ref, w_ref, bg_ref,
                   u_ref, v_ref, o_ref, gt_ref, qa_ref, ka_ref, va_ref, *dil_refs, dilations):
    x = x_ref[0]
    h = _rms(x, g_ref[...]) * (1.0 + sc_ref[0]) + sh_ref[0]
    hb = h.astype(BF16)

    def seg(lo, width):
        return jnp.dot(hb, w_ref[:, lo:lo + width], preferred_element_type=F32)

    u_ref[0] = seg(C_U, 2 * D_M)
    v_ref[0] = seg(C_V, D_M)
    o_ref[0] = jax.nn.sigmoid(seg(C_O, D_M))
    qkv = (seg(C_QA, D_A) * (HD_A ** -0.5), seg(C_KA, D_A), seg(C_VA, D_A))
    for ref, val in zip((qa_ref, ka_ref, va_ref), qkv):
        ref[0] = val
    for n, dil in enumerate(dilations):
        for j, val in enumerate(qkv):
            dil_refs[3 * n + j][0] = _to_dilated(val, dil)
    t = seg(C_G, LANES) + bg_ref[...]
    lane = lax.broadcasted_iota(jnp.int32, t.shape, 1)
    gt_ref[0] = jnp.where(lane < H_M, t, _log_sigmoid(t))


def _inproj(x3, g_pre, scale, shift, w_r, bias_g, tm, dilations=()):
    nb, rows, _ = x3.shape
    per_row_mod = scale.shape[1] != 1
    mod_block = (1, tm, D_MODEL) if per_row_mod else (1, 1, D_MODEL)
    mod_map = (lambda b, i: (b, i, 0)) if per_row_mod else (lambda b, i: (b, 0, 0))
    row_map = lambda b, i: (b, i, 0)
    widths = (2 * D_M, D_M, D_M, LANES, D_A, D_A, D_A)
    dil_specs = [pl.BlockSpec((1, tm // d, d * D_A), row_map) for d in dilations for _ in range(3)]
    dil_shapes = [jax.ShapeDtypeStruct((nb, rows // d, d * D_A), F32) for d in dilations for _ in range(3)]
    return pl.pallas_call(
        functools.partial(_inproj_kernel, dilations=tuple(dilations)),
        grid=(nb, rows // tm),
        in_specs=[
            pl.BlockSpec((1, tm, D_MODEL), row_map),
            pl.BlockSpec((1, D_MODEL), lambda b, i: (0, 0)),
            pl.BlockSpec(mod_block, mod_map),
            pl.BlockSpec(mod_block, mod_map),
            pl.BlockSpec((D_MODEL, D_IN_R), lambda b, i: (0, 0)),
            pl.BlockSpec((1, LANES), lambda b, i: (0, 0)),
        ],
        out_specs=[pl.BlockSpec((1, tm, w), row_map) for w in widths] + dil_specs,
        out_shape=[jax.ShapeDtypeStruct((nb, rows, w), F32) for w in widths] + dil_shapes,
        compiler_params=_params(("arbitrary", "arbitrary")),
        name="inproj",
    )(x3, g_pre.reshape(1, D_MODEL), scale, shift, w_r, bias_g)


def _mlstm_kernel(u_ref, v_ref, og_ref, gc_ref, gr_ref, tail0_ref, c0_ref, n0_ref, m0_ref,
                  cw_ref, cb_ref, mh_ref,
                  mem_ref, c_out_ref, n_out_ref, m_out_ref,
                  c_s, n_s, m_s, tail_s, uext_s, *, chunk):
    L = chunk
    step = pl.program_id(1)

    @pl.when(step == 0)
    def _():
        c_s[...] = c0_ref[0]
        n_s[...] = n0_ref[0]
        m_s[...] = m0_ref[0]
        tail_s[...] = tail0_ref[0]

    u = u_ref[0]
    uext_s[0:8, :] = tail_s[...]
    uext_s[8:8 + L, :] = u
    acc = cb_ref[...] + uext_s[pl.ds(5, L), :] * cw_ref[0:1, :]
    acc = acc + uext_s[pl.ds(6, L), :] * cw_ref[1:2, :]
    acc = acc + uext_s[pl.ds(7, L), :] * cw_ref[2:3, :]
    acc = acc + u * cw_ref[3:4, :]
    tail_s[...] = uext_s[pl.ds(L, 8), :]
    qk = acc * jax.nn.sigmoid(acc)

    gc = gc_ref[0]
    gr = gr_ref[0]
    row = lax.broadcasted_iota(jnp.int32, (L, L), 0)
    col = lax.broadcasted_iota(jnp.int32, (L, L), 1)
    causal = col <= row
    tri = causal.astype(F32)
    tri_t = (row <= col).astype(F32)
    lane = lax.broadcasted_iota(jnp.int32, gc.shape, 1)
    srow = lax.broadcasted_iota(jnp.int32, gr.shape, 0)
    b_col_all = jnp.dot(tri, jnp.where(lane >= H_M, gc, 0.0), preferred_element_type=F32,
                        precision=lax.Precision.HIGHEST)
    b_row_all = jnp.dot(jnp.where(srow >= H_M, gr, 0.0), tri_t, preferred_element_type=F32,
                        precision=lax.Precision.HIGHEST)

    v_all = v_ref[0]
    og = og_ref[0]
    for h in range(H_M):
        q = qk[:, h * DK_M:(h + 1) * DK_M]
        k = qk[:, D_M + h * DK_M:D_M + (h + 1) * DK_M] * (DK_M ** -0.5)
        v = v_all[:, h * DK_M:(h + 1) * DK_M]
        li_row = gr[h:h + 1, :]
        li_col = gc[:, h:h + 1]
        b_row = b_row_all[H_M + h:H_M + h + 1, :]
        b_col = b_col_all[:, H_M + h:H_M + h + 1]
        m0 = m_s[h:h + 1, 0:1]
        c0 = c_s[h]
        n0 = n_s[h:h + 1, :]

        d = jnp.where(causal, b_col - b_row + li_row, NEG_INF)
        g_state = b_col + m0
        m = jnp.maximum(jnp.max(d, axis=-1, keepdims=True), g_state)
        w_state = jnp.exp(g_state - m)
        qb = q.astype(BF16)
        kb = k.astype(BF16)
        vb = v.astype(BF16)
        s = lax.dot_general(qb, kb, (((1,), (1,)), ((), ())), preferred_element_type=F32)
        s = s * jnp.exp(d - m)
        num = jnp.dot(s.astype(BF16), vb, preferred_element_type=F32)
        num = num + w_state * jnp.dot(qb, c0.astype(BF16), preferred_element_type=F32)
        den = jnp.sum(s, axis=-1, keepdims=True) + w_state * jnp.sum(q * n0, axis=-1, keepdims=True)
        hh = num / jnp.maximum(jnp.abs(den), jnp.exp(-m))

        b_last = b_col[L - 1:L, :]
        g_tok_row = b_last - b_row + li_row
        m_new = jnp.maximum(b_last + m0, jnp.max(g_tok_row, axis=-1, keepdims=True))
        w_tok_col = jnp.exp(b_last - b_col + li_col - m_new)
        decay = jnp.exp(b_last + m0 - m_new)
        kw = k * w_tok_col
        c_s[h] = decay * c0 + jnp.dot(kw.T.astype(BF16), vb, preferred_element_type=F32)
        n_s[h:h + 1, :] = decay * n0 + jnp.sum(kw, axis=0, keepdims=True)
        m_s[h:h + 1, :] = jnp.broadcast_to(m_new, (1, LANES))

        hn = hh * lax.rsqrt(jnp.mean(hh * hh, axis=-1, keepdims=True) + EPS)
        hn = hn * mh_ref[:, h * DK_M:(h + 1) * DK_M]
        mem_ref[0, :, h * DK_M:(h + 1) * DK_M] = og[:, h * DK_M:(h + 1) * DK_M] * hn

    @pl.when(step == pl.num_programs(1) - 1)
    def _():
        c_out_ref[0] = c_s[...]
        n_out_ref[0] = n_s[...]
        m_out_ref[0] = m_s[...]


def _mlstm(u, v, og, gates_col, gates_row, tail0, c0, n0, m0x, conv_w, conv_b, mh_norm, chunk):
    nb, rows, _ = u.shape
    nc = rows // chunk
    row_map = lambda b, c: (b, c, 0)
    bat3 = lambda b, c: (b, 0, 0)
    const2 = lambda b, c: (0, 0)
    return pl.pallas_call(
        functools.partial(_mlstm_kernel, chunk=chunk),
        grid=(nb, nc),
        in_specs=[
            pl.BlockSpec((1, chunk, 2 * D_M), row_map),
            pl.BlockSpec((1, chunk, D_M), row_map),
            pl.BlockSpec((1, chunk, D_M), row_map),
            pl.BlockSpec((1, chunk, LANES), row_map),
            pl.BlockSpec((1, 8, chunk), lambda b, c: (b, 0, c)),
            pl.BlockSpec((1, 8, 2 * D_M), bat3),
            pl.BlockSpec((1, H_M, DK_M, DK_M), lambda b, c: (b, 0, 0, 0)),
            pl.BlockSpec((1, H_M, DK_M), bat3),
            pl.BlockSpec((1, H_M, LANES), bat3),
            pl.BlockSpec((CONV_W, 2 * D_M), const2),
            pl.BlockSpec((1, 2 * D_M), const2),
            pl.BlockSpec((1, D_M), const2),
        ],
        out_specs=[
            pl.BlockSpec((1, chunk, D_M), row_map),
            pl.BlockSpec((1, H_M, DK_M, DK_M), lambda b, c: (b, 0, 0, 0)),
            pl.BlockSpec((1, H_M, DK_M), bat3),
            pl.BlockSpec((1, H_M, LANES), bat3),
        ],
        out_shape=[
            jax.ShapeDtypeStruct((nb, rows, D_M), F32),
            jax.ShapeDtypeStruct((nb, H_M, DK_M, DK_M), F32),
            jax.ShapeDtypeStruct((nb, H_M, DK_M), F32),
            jax.ShapeDtypeStruct((nb, H_M, LANES), F32),
        ],
        scratch_shapes=[
            pltpu.VMEM((H_M, DK_M, DK_M), F32),
            pltpu.VMEM((H_M, DK_M), F32),
            pltpu.VMEM((H_M, LANES), F32),
            pltpu.VMEM((8, 2 * D_M), F32),
            pltpu.VMEM((chunk + 8, 2 * D_M), F32),
        ],
        compiler_params=_params(("arbitrary", "arbitrary")),
        name="mlstm",
    )(u, v, og, gates_col, gates_row, tail0, c0, n0, m0x,
      conv_w, conv_b.reshape(1, 2 * D_M), mh_norm.reshape(1, D_M))


def _attn_kernel(q_ref, kc_ref, vc_ref, kp_ref, vp_ref, bias_ref, o_ref, l_ref, k_s, v_s, *, ql):
    first = pl.program_id(2) == 0
    k_s[0:QB, :] = kp_ref[0].astype(BF16)
    k_s[QB:QB + ql, :] = kc_ref[0].astype(BF16)
    v_s[0:QB, :] = vp_ref[0].astype(BF16)
    v_s[QB:QB + ql, :] = vc_ref[0].astype(BF16)
    col = lax.broadcasted_iota(jnp.int32, (QB, 2 * QB), 1)
    pad_keys = jnp.logical_and(first, col < QB)
    for h in range(H_A):
        lo = h * HD_A
        bias = bias_ref[h]
        for j in range(ql // QB):
            qb = q_ref[0, j * QB:(j + 1) * QB, lo:lo + HD_A].astype(BF16)
            kb = k_s[j * QB:(j + 2) * QB, lo:lo + HD_A]
            vb = v_s[j * QB:(j + 2) * QB, lo:lo + HD_A]
            logits = lax.dot_general(qb, kb, (((1,), (1,)), ((), ())), preferred_element_type=F32)
            logits = logits + bias
            if j == 0:
                logits = jnp.where(pad_keys, NEG_INF, logits)
            m = jnp.max(logits, axis=-1, keepdims=True)
            p = jnp.exp(logits - m)
            s = jnp.sum(p, axis=-1, keepdims=True)
            o = jnp.dot(p.astype(BF16), vb, preferred_element_type=F32) / s
            o_ref[0, j * QB:(j + 1) * QB, lo:lo + HD_A] = o
            l_ref[0, j * QB:(j + 1) * QB, lo:lo + HD_A] = jnp.broadcast_to(m + jnp.log(s), (QB, HD_A))


def _attn_branch(qa, ka, va, bias, dil):
    nb, ld, _ = qa.shape
    ql = min(ATT_QL, ld)
    cur = lambda b, r, i: (b, i, r)
    prev = lambda b, r, i: (b, jnp.maximum(i * (ql // QB) - 1, 0), r)
    return pl.pallas_call(
        functools.partial(_attn_kernel, ql=ql),
        grid=(nb, dil, ld // ql),
        in_specs=[
            pl.BlockSpec((1, ql, D_A), cur),
            pl.BlockSpec((1, ql, D_A), cur),
            pl.BlockSpec((1, ql, D_A), cur),
            pl.BlockSpec((1, QB, D_A), prev),
            pl.BlockSpec((1, QB, D_A), prev),
            pl.BlockSpec((H_A, QB, 2 * QB), lambda b, r, i: (0, 0, 0)),
        ],
        out_specs=[pl.BlockSpec((1, ql, D_A), cur), pl.BlockSpec((1, ql, D_A), cur)],
        out_shape=[jax.ShapeDtypeStruct((nb, ld, dil * D_A), F32)] * 2,
        scratch_shapes=[pltpu.VMEM((QB + ql, D_A), BF16), pltpu.VMEM((QB + ql, D_A), BF16)],
        compiler_params=_params(("arbitrary", "arbitrary", "arbitrary")),
        name=f"attn_d{dil}",
    )(qa, ka, va, ka, va, bias)


def _heads_kernel(k_ref, v_ref, ko_ref, vo_ref):
    for src, dst in ((k_ref, ko_ref), (v_ref, vo_ref)):
        for c in range(D_A // LANES):
            xt = src[0, :, c * LANES:(c + 1) * LANES].T
            dst[0, 2 * c] = xt[:HD_A]
            dst[0, 2 * c + 1] = xt[HD_A:]


def _window_heads(ka, va, keep, tm=512):
    nb, seq, _ = ka.shape
    first = (seq - keep) // tm
    in_spec = pl.BlockSpec((1, tm, D_A), lambda b, i: (b, first + i, 0))
    out_spec = pl.BlockSpec((1, H_A, HD_A, tm), lambda b, i: (b, 0, 0, i))
    kt, vt = pl.pallas_call(
        _heads_kernel,
        grid=(nb, keep // tm),
        in_specs=[in_spec, in_spec],
        out_specs=[out_spec, out_spec],
        out_shape=[jax.ShapeDtypeStruct((nb, H_A, HD_A, keep), F32)] * 2,
        compiler_params=_params(("arbitrary", "arbitrary")),
        name="window_heads",
    )(ka, va)
    return kt.transpose(0, 3, 1, 2), vt.transpose(0, 3, 1, 2)


def _t5_bucket(dist):
    max_exact = N_BUCKETS // 2
    n = jnp.maximum(dist, 1).astype(F32)
    large = max_exact + (jnp.log(n / max_exact) / math.log(MAX_DISTANCE / max_exact)
                         * (N_BUCKETS - max_exact)).astype(jnp.int32)
    large = jnp.minimum(large, N_BUCKETS - 1)
    return jnp.where(dist < max_exact, dist, large)


def _branch_bias_vec(rel_bias, window, dil):
    return rel_bias[_t5_bucket(dil * jnp.arange(window // dil + 1))].astype(F32)


def _prompt_bias(rel_bias, window, dil):
    assert window // dil == QB
    n = 2 * QB
    vec = _branch_bias_vec(rel_bias, window, dil)
    vecp = jnp.concatenate([vec[::-1], jnp.full((n - QB - 1, H_A), NEG_INF, F32)], axis=0)
    g = jnp.roll(vecp, QB - 1, axis=0)
    skew = jnp.tile(g, (QB + 1, 1))[:QB * (n + 1)].reshape(QB, n + 1, H_A)[:, :n]
    return skew[::-1].transpose(2, 0, 1)


TOK_PAD = 8


def _sattn_kernel(q_ref, kt_ref, vt_ref, knt_ref, vnt_ref, bc_ref, bn_ref, att_ref):
    nt = (((1,), (1,)), ((), ()))
    for h in range(H_A):
        q = q_ref[0, h].astype(BF16)
        lc = jnp.dot(q, kt_ref[0, h].astype(BF16), preferred_element_type=F32)
        ln = jnp.dot(q, knt_ref[0, h].astype(BF16), preferred_element_type=F32)
        lcs = [lc + bc_ref[d, h] for d in range(len(BRANCHES))]
        lns = [ln + bn_ref[d, h] for d in range(len(BRANCHES))]
        m = functools.reduce(jnp.maximum, [jnp.max(x, axis=-1, keepdims=True) for x in lcs + lns])
        pc = functools.reduce(jnp.add, [jnp.exp(x - m) for x in lcs])
        pn = functools.reduce(jnp.add, [jnp.exp(x - m) for x in lns])
        s = jnp.sum(pc, axis=-1, keepdims=True) + jnp.sum(pn, axis=-1, keepdims=True)
        o = lax.dot_general(pc.astype(BF16), vt_ref[0, h].astype(BF16), nt, preferred_element_type=F32)
        o = o + lax.dot_general(pn.astype(BF16), vnt_ref[0, h].astype(BF16), nt, preferred_element_type=F32)
        att_ref[0, h] = o / s


def _sample_attention(q_h, kt, vt, knt, vnt, bias_c, bias_n):
    nb, _, _, n_past = kt.shape
    bat = lambda b: (b, 0, 0, 0)
    const = lambda b: (0, 0, 0, 0)
    return pl.pallas_call(
        _sattn_kernel,
        grid=(nb,),
        in_specs=[
            pl.BlockSpec((1, H_A, TOK_PAD, HD_A), bat),
            pl.BlockSpec((1, H_A, HD_A, n_past), bat),
            pl.BlockSpec((1, H_A, HD_A, n_past), bat),
            pl.BlockSpec((1, H_A, HD_A, LANES), bat),
            pl.BlockSpec((1, H_A, HD_A, LANES), bat),
            pl.BlockSpec(bias_c.shape, const),
            pl.BlockSpec(bias_n.shape, const),
        ],
        out_specs=pl.BlockSpec((1, H_A, TOK_PAD, HD_A), bat),
        out_shape=jax.ShapeDtypeStruct((nb, H_A, TOK_PAD, HD_A), F32),
        compiler_params=_params(("arbitrary",)),
        name="sample_attn",
    )(q_h, kt, vt, knt, vnt, bias_c, bias_n)


def _sample_bias(rel_bias, n_tok, n_past):
    ninf = lambda n: jnp.full((n, H_A), NEG_INF, F32)
    cached, new = [], []
    for window, dil in BRANCHES:
        nk = window // dil
        rows = n_past // dil
        assert n_past % dil == 0 and rows >= nk
        vec = _branch_bias_vec(rel_bias, window, dil)
        res = jnp.arange(dil)[None, :, None]
        tc, tn = [], []
        for t in range(n_tok):
            near = t // dil + 1
            col = jnp.concatenate([ninf(rows - (nk + 1 - near)), vec[near:nk + 1][::-1]], axis=0)
            tab = jnp.where(res == t % dil, col[:, None, :], NEG_INF).reshape(n_past, H_A)
            tc.append(tab)
            keys = [vec[(t - j) // dil][None] if (t - j) % dil == 0 and 0 <= t - j <= nk * dil else ninf(1)
                    for j in range(n_tok)]
            tn.append(jnp.concatenate(keys + [ninf(LANES - n_tok)], axis=0))
        pad = lambda tabs: jnp.stack(tabs + [jnp.zeros_like(tabs[0])] * (TOK_PAD - n_tok))
        cached.append(pad(tc).transpose(2, 0, 1))
        new.append(pad(tn).transpose(2, 0, 1))
    return jnp.stack(cached), jnp.stack(new)


def _postmix_kernel(mem_ref, o1_ref, l1_ref, o2_ref, l2_ref, o3_ref, l3_ref, x_ref,
                    g1_ref, sc2_ref, sh2_ref, gpost_ref, gpre_ref, wo_ref, wr_ref, br_ref,
                    x1_ref, h2_ref, idx_ref, gate_ref, *, dilations):
    rows = lambda ref, dil: ref[0] if dil == 1 else _from_dilated(ref[0], dil)
    o1, o2, o3 = (rows(r, d) for r, d in zip((o1_ref, o2_ref, o3_ref), dilations))
    l1, l2, l3 = (rows(r, d) for r, d in zip((l1_ref, l2_ref, l3_ref), dilations))
    lmax = jnp.maximum(jnp.maximum(l1, l2), l3)
    w1 = jnp.exp(l1 - lmax)
    w2 = jnp.exp(l2 - lmax)
    w3 = jnp.exp(l3 - lmax)
    att = (w1 * o1 + w2 * o2 + w3 * o3) / (w1 + w2 + w3)
    y = jnp.dot(mem_ref[0].astype(BF16), wo_ref[0:D_M, :], preferred_element_type=F32)
    y = y + jnp.dot(att.astype(BF16), wo_ref[D_M:, :], preferred_element_type=F32)
    x1 = x_ref[0] + g1_ref[0] * _rms(y, gpost_ref[...])
    x1_ref[0] = x1
    h2 = _rms(x1, gpre_ref[...]) * (1.0 + sc2_ref[0]) + sh2_ref[0]
    h2_ref[0] = h2
    h_hi = h2.astype(BF16)
    h_lo = (h2 - h_hi.astype(F32)).astype(BF16)
    w_hi = wr_ref[0]
    logits = jnp.dot(h_hi, w_hi, preferred_element_type=F32)
    logits = logits + jnp.dot(h_lo, w_hi, preferred_element_type=F32)
    logits = logits + jnp.dot(h_hi, wr_ref[1], preferred_element_type=F32) + br_ref[...]
    lane = lax.broadcasted_iota(jnp.int32, logits.shape, 1)
    lane_f = lane.astype(F32)
    logits = jnp.where(lane < N_EXPERTS, logits, NEG_INF)
    vals, idxs = [], []
    for _ in range(TOP_K):
        mx = jnp.max(logits, axis=-1, keepdims=True)
        ix = jnp.min(jnp.where(logits == mx, lane_f, float(LANES)), axis=-1, keepdims=True)
        vals.append(mx)
        idxs.append(ix)
        logits = jnp.where(lane_f == ix, NEG_INF, logits)
    es = [jnp.exp(v - vals[0]) for v in vals]
    tot = es[0] + es[1] + es[2] + es[3]
    idx_tile = jnp.zeros(logits.shape, F32)
    gate_tile = jnp.zeros(logits.shape, F32)
    for k in range(TOP_K):
        idx_tile = jnp.where(lane == k, idxs[k], idx_tile)
        gate_tile = jnp.where(lane == k, es[k] / tot, gate_tile)
    idx_ref[0] = idx_tile.astype(jnp.int32)
    gate_ref[0] = gate_tile


def _postmix(mem, branches, x3, gate1, scale2, shift2, g_post, g_pre, w_out_b, w_router_p, b_router_p, tm,
             dilations=(1, 1, 1)):
    nb, rows, _ = x3.shape
    per_row_mod = gate1.shape[1] != 1
    mod_block = (1, tm, D_MODEL) if per_row_mod else (1, 1, D_MODEL)
    mod_map = (lambda b, i: (b, i, 0)) if per_row_mod else (lambda b, i: (b, 0, 0))
    row_map = lambda b, i: (b, i, 0)
    const2 = lambda b, i: (0, 0)
    half = pl.BlockSpec((1, tm, D_M), row_map)
    flat = [a for pair in branches for a in pair]
    branch_specs = [pl.BlockSpec((1, tm // d, d * D_A), row_map) for d in dilations for _ in range(2)]
    return pl.pallas_call(
        functools.partial(_postmix_kernel, dilations=tuple(dilations)),
        grid=(nb, rows // tm),
        in_specs=[half] + branch_specs + [
            pl.BlockSpec((1, tm, D_MODEL), row_map),
            pl.BlockSpec(mod_block, mod_map),
            pl.BlockSpec(mod_block, mod_map),
            pl.BlockSpec(mod_block, mod_map),
            pl.BlockSpec((1, D_MODEL), const2),
            pl.BlockSpec((1, D_MODEL), const2),
            pl.BlockSpec((D_MODEL, D_MODEL), const2),
            pl.BlockSpec((2, D_MODEL, LANES), lambda b, i: (0, 0, 0)),
            pl.BlockSpec((1, LANES), const2),
        ],
        out_specs=[
            pl.BlockSpec((1, tm, D_MODEL), row_map),
            pl.BlockSpec((1, tm, D_MODEL), row_map),
            pl.BlockSpec((1, tm, LANES), row_map),
            pl.BlockSpec((1, tm, LANES), row_map),
        ],
        out_shape=[
            jax.ShapeDtypeStruct((nb, rows, D_MODEL), F32),
            jax.ShapeDtypeStruct((nb, rows, D_MODEL), F32),
            jax.ShapeDtypeStruct((nb, rows, LANES), jnp.int32),
            jax.ShapeDtypeStruct((nb, rows, LANES), F32),
        ],
        compiler_params=_params(("arbitrary", "arbitrary")),
        name="postmix",
    )(mem, *flat, x3, gate1, scale2, shift2, g_post.reshape(1, D_MODEL), g_pre.reshape(1, D_MODEL),
      w_out_b, w_router_p, b_router_p)


def _row_wait(src_rows, dst_rows, sem):
    pltpu.make_async_copy(src_rows, dst_rows, sem).wait()


def _scatter_kernel(slot_ref, hp_ref, hs_ref, xs_hbm, buf, sem, *, n_prompt_tiles):
    h = jnp.where(pl.program_id(0) >= n_prompt_tiles, hs_ref[...], hp_ref[...])
    bits = pltpu.bitcast(h.astype(BF16).astype(F32), U32)
    packed = (bits[:, :PACK_W] & U32(HI_MASK)) | (bits[:, PACK_W:] >> 16)
    buf[...] = pltpu.einshape("m(jc)->mjc", packed, j=PACK_T)

    def issue(r, carry):
        src = buf.at[pl.ds(r, 1)]
        for k in range(TOP_K):
            pltpu.make_async_copy(src, xs_hbm.at[pl.ds(slot_ref[0, 0, r * TOP_K + k], 1)],
                                  sem).start(priority=k % 2)
        return carry

    lax.fori_loop(0, TOK_TILE, issue, 0, unroll=4)
    for _ in range(TOP_K):
        _row_wait(buf, xs_hbm.at[pl.ds(0, TOK_TILE)], sem)


def _moe_scatter(h_p, h_s, slot_tiles):
    tiles = slot_tiles.shape[0]
    n_p = h_p.shape[0] // TOK_TILE
    assert tiles == n_p + 1 and h_s.shape[0] == TOK_TILE
    n_rows = tiles * TOK_TILE * TOP_K
    return pl.pallas_call(
        functools.partial(_scatter_kernel, n_prompt_tiles=n_p),
        grid=(tiles,),
        in_specs=[
            pl.BlockSpec((1, 1, TOK_TILE * TOP_K), lambda i: (i, 0, 0), memory_space=pltpu.SMEM),
            pl.BlockSpec((TOK_TILE, D_MODEL), lambda i: (jnp.minimum(i, n_p - 1), 0)),
            pl.BlockSpec((TOK_TILE, D_MODEL), lambda i: (0, 0)),
        ],
        out_specs=pl.BlockSpec(memory_space=pl.ANY),
        out_shape=jax.ShapeDtypeStruct((n_rows, PACK_T, LANES), U32),
        scratch_shapes=[pltpu.VMEM((TOK_TILE, PACK_T, LANES), U32), pltpu.SemaphoreType.DMA],
        compiler_params=_params(("arbitrary",)),
        name="moe_scatter",
    )(slot_tiles, h_p, h_s)


def _expert_kernel(tile_ref, exp_ref, lo_ref, hi_ref, first_ref, nvis_ref,
                   xs_ref, w1_ref, b1_ref, w2_ref, b2_ref, ys_ref, w1b, w2b):
    v = pl.program_id(0)
    active = v < nvis_ref[0]
    changed = jnp.logical_or(v == 0, exp_ref[v] != exp_ref[jnp.maximum(v - 1, 0)])
    lo = lo_ref[v]
    hi = hi_ref[v]

    @pl.when(jnp.logical_and(active, changed))
    def _():
        w1b[...] = w1_ref[0].astype(BF16)
        w2b[...] = w2_ref[0].astype(BF16)

    def mlp_rows(r0):
        w = pltpu.einshape("mjc->m(jc)", xs_ref[r0:r0 + MOE_SUB])
        x_hi = pltpu.bitcast(w & U32(HI_MASK), F32).astype(BF16)
        x_lo = pltpu.bitcast(w << 16, F32).astype(BF16)
        hgu = jnp.dot(x_hi, w1b[0:PACK_W, :], preferred_element_type=F32)
        hgu = hgu + jnp.dot(x_lo, w1b[PACK_W:, :], preferred_element_type=F32) + b1_ref[0]
        x_glu = jnp.minimum(hgu[:, :D_FF], SWIGLU_LIMIT)
        x_lin = jnp.clip(hgu[:, D_FF:], -SWIGLU_LIMIT, SWIGLU_LIMIT)
        act = x_glu * jax.nn.sigmoid(SWIGLU_ALPHA * x_glu) * (x_lin + 1.0)
        y = jnp.dot(act.astype(BF16), w2b[...], preferred_element_type=F32) + b2_ref[0]
        return pltpu.einshape("m(jc)->mjc", y, j=ROW_T)

    subs = [r * MOE_SUB for r in range(MOE_M // MOE_SUB)]
    whole = jnp.logical_and(lo == 0, hi == MOE_M)

    @pl.when(jnp.logical_and(active, whole))
    def _():
        for r0 in subs:
            ys_ref[r0:r0 + MOE_SUB] = mlp_rows(r0)

    shared = jnp.logical_and(active, jnp.logical_not(whole))

    @pl.when(jnp.logical_and(shared, first_ref[v] == 1))
    def _():
        ys_ref[...] = jnp.zeros(ys_ref.shape, F32)

    for r0 in subs:
        @pl.when(jnp.logical_and(shared, jnp.logical_and(lo < r0 + MOE_SUB, hi > r0)))
        def _():
            rows = r0 + lax.broadcasted_iota(jnp.int32, (MOE_SUB, 1, 1), 0)
            mine = jnp.logical_and(rows >= lo, rows < hi)
            ys_ref[r0:r0 + MOE_SUB] = jnp.where(mine, mlp_rows(r0), ys_ref[r0:r0 + MOE_SUB])


def _moe_experts(xs, visits, w1, b1, w2, b2):
    n_rows = xs.shape[0]
    n_vis = visits[0].shape[0]
    tile_map = lambda v, tile, *_: (tile[v], 0, 0)
    exp_map = lambda v, tile, exp, *_: (exp[v], 0, 0)
    return pl.pallas_call(
        _expert_kernel,
        grid_spec=pltpu.PrefetchScalarGridSpec(
            num_scalar_prefetch=6,
            grid=(n_vis,),
            in_specs=[
                pl.BlockSpec((MOE_M, PACK_T, LANES), tile_map),
                pl.BlockSpec((1, D_MODEL, 2 * D_FF), exp_map),
                pl.BlockSpec((1, 1, 2 * D_FF), exp_map),
                pl.BlockSpec((1, D_FF, D_MODEL), exp_map),
                pl.BlockSpec((1, 1, D_MODEL), exp_map),
            ],
            out_specs=pl.BlockSpec((MOE_M, ROW_T, LANES), tile_map),
            scratch_shapes=[pltpu.VMEM((D_MODEL, 2 * D_FF), BF16), pltpu.VMEM((D_FF, D_MODEL), BF16)],
        ),
        out_shape=jax.ShapeDtypeStruct((n_rows, ROW_T, LANES), F32),
        compiler_params=_params(("arbitrary",)),
        name="moe_experts",
    )(*visits, xs, w1, b1.reshape(N_EXPERTS, 1, 2 * D_FF), w2, b2.reshape(N_EXPERTS, 1, D_MODEL))


def _combine_kernel(slot_ref, next_ref, gate_ref, x1p_ref, x1s_ref, g2p_ref, g2s_ref, gpost_ref, ys_hbm,
                    yp_ref, ysm_ref, buf, sem, *, n_prompt_tiles):
    i = pl.program_id(0)
    last = pl.num_programs(0) - 1
    cur = lax.rem(i, 2)

    def gather(slots, b):
        def issue(r, carry):
            for k in range(TOP_K):
                pltpu.make_async_copy(ys_hbm.at[pl.ds(slots[0, 0, r * TOP_K + k], 1)],
                                      buf.at[b, pl.ds(k * TOK_TILE + r, 1)], sem.at[b]).start(priority=k % 2)
            return carry

        lax.fori_loop(0, TOK_TILE, issue, 0, unroll=4)

    @pl.when(i == 0)
    def _():
        gather(slot_ref, cur)

    @pl.when(i < last)
    def _():
        gather(next_ref, 1 - cur)

    for k in range(TOP_K):
        _row_wait(ys_hbm.at[pl.ds(0, TOK_TILE)], buf.at[cur, pl.ds(k * TOK_TILE, TOK_TILE)], sem.at[cur])
    g = gate_ref[...]
    f = jnp.zeros((TOK_TILE, D_MODEL), F32)
    for k in range(TOP_K):
        rows = pltpu.einshape("mjc->m(jc)", buf[cur, pl.ds(k * TOK_TILE, TOK_TILE)])
        f = f + rows * g[:, k:k + 1]
    is_sample = i >= n_prompt_tiles
    x1 = jnp.where(is_sample, x1s_ref[...], x1p_ref[...])
    g2 = jnp.where(is_sample, g2s_ref[...], g2p_ref[0])
    y = x1 + g2 * _rms(f, gpost_ref[...])

    @pl.when(jnp.logical_not(is_sample))
    def _():
        yp_ref[...] = y

    @pl.when(is_sample)
    def _():
        ysm_ref[...] = y


def _moe_combine(ys, slot_tiles, gates, x1_p, x1_s, gate2_p, gate2_s, g_post, rows_per_batch):
    tiles = slot_tiles.shape[0]
    n_p = x1_p.shape[0] // TOK_TILE
    assert tiles == n_p + 1 and x1_s.shape[0] == TOK_TILE
    tiles_per_batch = rows_per_batch // TOK_TILE
    p_tile = lambda i: (jnp.minimum(i, n_p - 1), 0)
    const2 = lambda i: (0, 0)
    return pl.pallas_call(
        functools.partial(_combine_kernel, n_prompt_tiles=n_p),
        grid=(tiles,),
        in_specs=[
            pl.BlockSpec((1, 1, TOK_TILE * TOP_K), lambda i: (i, 0, 0), memory_space=pltpu.SMEM),
            pl.BlockSpec((1, 1, TOK_TILE * TOP_K), lambda i: (jnp.minimum(i + 1, tiles - 1), 0, 0),
                         memory_space=pltpu.SMEM),
            pl.BlockSpec((TOK_TILE, LANES), lambda i: (i, 0)),
            pl.BlockSpec((TOK_TILE, D_MODEL), p_tile),
            pl.BlockSpec((TOK_TILE, D_MODEL), const2),
            pl.BlockSpec((1, 1, D_MODEL), lambda i: (jnp.minimum(i, n_p - 1) // tiles_per_batch, 0, 0)),
            pl.BlockSpec((TOK_TILE, D_MODEL), const2),
            pl.BlockSpec((1, D_MODEL), const2),
            pl.BlockSpec(memory_space=pl.ANY),
        ],
        out_specs=[pl.BlockSpec((TOK_TILE, D_MODEL), p_tile), pl.BlockSpec((TOK_TILE, D_MODEL), const2)],
        out_shape=[jax.ShapeDtypeStruct(x1_p.shape, F32), jax.ShapeDtypeStruct(x1_s.shape, F32)],
        scratch_shapes=[pltpu.VMEM((2, TOP_K * TOK_TILE, ROW_T, LANES), F32), pltpu.SemaphoreType.DMA((2,))],
        compiler_params=_params(("arbitrary",)),
        name="moe_combine",
    )(slot_tiles, slot_tiles, gates, x1_p, x1_s, gate2_p, gate2_s, g_post.reshape(1, D_MODEL), ys)


def _moe_plan(top_idx):
    a = top_idx.shape[0] * TOP_K
    assert a % MOE_M == 0
    flat_e = top_idx[:, :TOP_K].reshape(a)
    onehot = (flat_e[:, None] == jnp.arange(N_EXPERTS)[None, :]).astype(jnp.int32)
    csum = jnp.cumsum(onehot, axis=0)
    counts = csum[-1]
    g_end = jnp.cumsum(counts)
    g_start = g_end - counts
    rank = jnp.sum(csum * onehot, axis=1) - 1
    slot_of = (jnp.sum(g_start[None, :] * onehot, axis=1) + rank).astype(jnp.int32)

    n_tiles = a // MOE_M
    n_vis = n_tiles + N_EXPERTS - 1
    first_tile = g_start // MOE_M
    last_tile = jnp.where(counts > 0, (g_end - 1) // MOE_M, first_tile)
    nvis_e = jnp.where(counts > 0, last_tile - first_tile + 1, 0)
    v_end = jnp.cumsum(nvis_e)
    v_start = v_end - nvis_e
    total = v_end[-1]
    v = jnp.arange(n_vis)
    vc = jnp.minimum(v, total - 1)
    e_of = jnp.minimum(jnp.sum((vc[:, None] >= v_end[None, :]).astype(jnp.int32), axis=1), N_EXPERTS - 1)
    tile_of = first_tile[e_of] + (vc - v_start[e_of])
    lo = jnp.maximum(g_start[e_of], tile_of * MOE_M) - tile_of * MOE_M
    hi = jnp.minimum(g_end[e_of], (tile_of + 1) * MOE_M) - tile_of * MOE_M
    prev_tile = jnp.concatenate([jnp.full((1,), -1, tile_of.dtype), tile_of[:-1]])
    first = (tile_of != prev_tile).astype(jnp.int32)
    i32 = lambda t: t.astype(jnp.int32)
    visits = (i32(tile_of), i32(e_of), i32(lo), i32(hi), first, i32(total).reshape(1))
    return slot_of, visits


def _reorder_w_in(w_in):
    off_om_end = 4 * D_M
    off_g_end = off_om_end + 2 * H_M
    w = jnp.concatenate([
        w_in[:, :off_om_end],
        w_in[:, off_g_end:],
        w_in[:, off_om_end:off_g_end],
        jnp.zeros((D_MODEL, LANES - 2 * H_M), w_in.dtype),
    ], axis=1)
    return w.astype(BF16)


def kernel(x_prompt, x_sample, cache_k, cache_v, state_conv, state_C, state_n, state_m, c_prompt, c_sample, w_ada, b_ada, g_pre_mix, g_post_mix, g_pre_ffn, g_post_ffn, w_in, b_ig, b_fg, conv_w, conv_b, mh_norm, rel_bias, w_out, w_router, b_router, w1, b1, w2, b2):
    depth = w_in.shape[0]
    assert depth == 1
    l = 0
    nb_p, seq, _ = x_prompt.shape
    nb_s, n_tok, _ = x_sample.shape
    rows_s = nb_s * n_tok
    assert rows_s == TOK_TILE

    n_c = nb_p + nb_s
    c_pad = -(-n_c // 8) * 8
    c_all = jnp.concatenate([c_prompt, c_sample, jnp.zeros((c_pad - n_c, D_MODEL), F32)], axis=0)
    mod = _adaln(c_all, w_ada[l], b_ada[l])
    mods = [mod[:, i * D_MODEL:(i + 1) * D_MODEL] for i in range(6)]
    mods_p = [m[:nb_p].reshape(nb_p, 1, D_MODEL) for m in mods]
    mods_s = [jnp.repeat(m[nb_p:n_c], n_tok, axis=0).reshape(1, rows_s, D_MODEL) for m in mods]

    w_r = _reorder_w_in(w_in[l])
    bias_g = jnp.concatenate([b_ig[l], b_fg[l], jnp.zeros((LANES - 2 * H_M,), F32)]).reshape(1, LANES)
    w_out_b = w_out[l].astype(BF16)
    w_router_f = jnp.concatenate([w_router[l], jnp.zeros((D_MODEL, LANES - N_EXPERTS), F32)], axis=1)
    w_router_hi = w_router_f.astype(BF16)
    w_router_p = jnp.stack([w_router_hi, (w_router_f - w_router_hi.astype(F32)).astype(BF16)])
    b_router_p = jnp.concatenate([b_router[l].astype(F32), jnp.zeros((LANES - N_EXPERTS,), F32)]).reshape(1, LANES)

    shift1, scale1, gate1, shift2, scale2, gate2_p = mods_p
    dils = tuple(d for _, d in BRANCHES)
    assert dils[0] == 1
    u, v_m, og, gts, qa, ka, va, *dilated = _inproj(x_prompt, g_pre_mix[l], scale1, shift1, w_r, bias_g, tm=512,
                                                   dilations=dils[1:])
    qkv_d = [(qa, ka, va)] + [tuple(dilated[3 * n:3 * n + 3]) for n in range(len(dils) - 1)]
    gates_row = gts[:, :, :8].transpose(0, 2, 1)
    zeros = lambda *s: jnp.zeros(s, F32)
    mem, c_p, n_p, m_p = _mlstm(u, v_m, og, gts, gates_row, zeros(nb_p, 8, 2 * D_M),
                                zeros(nb_p, H_M, DK_M, DK_M), zeros(nb_p, H_M, DK_M), zeros(nb_p, H_M, LANES),
                                conv_w[l], conv_b[l], mh_norm[l], MLSTM_L_PROMPT)
    branches = [_attn_branch(*qkv, _prompt_bias(rel_bias, w, d), d) for qkv, (w, d) in zip(qkv_d, BRANCHES)]
    x1_p, h2_p, idx_p, gate_p = _postmix(mem, branches, x_prompt, gate1, scale2, shift2,
                                         g_post_mix[l], g_pre_ffn[l], w_out_b, w_router_p, b_router_p, tm=256,
                                         dilations=dils)
    keep = min(WINDOW_MAX, seq)
    k_p, v_p = _window_heads(ka, va, keep)
    conv_p = u[:, seq - (CONV_W - 1):]

    shift1, scale1, gate1, shift2, scale2, gate2_s = mods_s
    x_s = x_sample.reshape(1, rows_s, D_MODEL)
    u_s, v_s, og_s, gts_s, qa_s, ka_s, va_s = _inproj(x_s, g_pre_mix[l], scale1, shift1, w_r, bias_g, tm=rows_s)
    per_b = lambda t: t.reshape(nb_s, n_tok, t.shape[-1])
    ls = MLSTM_L_SAMPLE
    pad_rows = lambda t, n: jnp.concatenate([t, jnp.zeros((nb_s, n - t.shape[1], t.shape[2]), t.dtype)], axis=1)
    inert = jnp.concatenate([jnp.full((H_M,), NEG_INF, F32), jnp.zeros((LANES - H_M,), F32)])
    gts_pad = jnp.concatenate([per_b(gts_s), jnp.broadcast_to(inert, (nb_s, ls - n_tok, LANES))], axis=1)
    tail0 = jnp.concatenate([zeros(nb_s, 8 - (CONV_W - 1), 2 * D_M), state_conv[l].astype(F32)], axis=1)
    m0x = jnp.broadcast_to(state_m[l].astype(F32)[:, :, None], (nb_s, H_M, LANES))
    mem_s, c_s, n_s, m_s = _mlstm(pad_rows(per_b(u_s), ls), pad_rows(per_b(v_s), ls), pad_rows(per_b(og_s), ls),
                                  gts_pad, gts_pad[:, :, :8].transpose(0, 2, 1), tail0,
                                  state_C[l].astype(F32), state_n[l].astype(F32), m0x,
                                  conv_w[l], conv_b[l], mh_norm[l], ls)
    mem_s = mem_s[:, :n_tok].reshape(1, rows_s, D_M)
    heads = lambda t: t.reshape(nb_s, n_tok, H_A, HD_A)
    k_new, v_new = heads(ka_s), heads(va_s)
    n_past = cache_k.shape[2]
    assert n_tok <= TOK_PAD
    q_h = jnp.pad(heads(qa_s).transpose(0, 2, 1, 3), ((0, 0), (0, 0), (0, TOK_PAD - n_tok), (0, 0)))
    new_t = lambda t: jnp.pad(t.transpose(0, 2, 3, 1), ((0, 0), (0, 0), (0, 0), (0, LANES - n_tok)))
    cache_t = lambda c: c[l].astype(F32).transpose(0, 2, 3, 1)
    att_s = _sample_attention(q_h, cache_t(cache_k), cache_t(cache_v), new_t(k_new), new_t(v_new),
                              *_sample_bias(rel_bias, n_tok, n_past))
    att_s = att_s[:, :, :n_tok].transpose(0, 2, 1, 3).reshape(1, rows_s, D_A)
    zero_o = zeros(1, rows_s, D_A)
    ninf = jnp.full((1, rows_s, D_A), NEG_INF, F32)
    branches_s = [(att_s, zero_o), (zero_o, ninf), (zero_o, ninf)]
    x1_s, h2_s, idx_s, gate_s = _postmix(mem_s, branches_s, x_s, gate1, scale2, shift2,
                                         g_post_mix[l], g_pre_ffn[l], w_out_b, w_router_p, b_router_p, tm=rows_s)

    t_p = nb_p * seq
    idx_all = jnp.concatenate([idx_p.reshape(t_p, LANES), idx_s.reshape(rows_s, LANES)], axis=0)
    gate_all = jnp.concatenate([gate_p.reshape(t_p, LANES), gate_s.reshape(rows_s, LANES)], axis=0)
    slot_of, visits = _moe_plan(idx_all)
    slot_tiles = slot_of.reshape((t_p + rows_s) // TOK_TILE, 1, TOK_TILE * TOP_K)
    xs = _moe_scatter(h2_p.reshape(t_p, D_MODEL), h2_s.reshape(rows_s, D_MODEL), slot_tiles)
    ys = _moe_experts(xs, visits, w1[l], b1[l], w2[l], b2[l])
    y_p, y_s = _moe_combine(ys, slot_tiles, gate_all, x1_p.reshape(t_p, D_MODEL), x1_s.reshape(rows_s, D_MODEL),
                            gate2_p, gate2_s.reshape(rows_s, D_MODEL), g_post_ffn[l], seq)
    y_prompt = y_p.reshape(nb_p, seq, D_MODEL)
    y_sample = y_s.reshape(nb_s, n_tok, D_MODEL)

    st = lambda t: t[None]
    conv_s = jnp.concatenate([state_conv[l].astype(F32), per_b(u_s)], axis=1)[:, -(CONV_W - 1):]
    return (y_prompt, y_sample, st(k_p), st(v_p), st(conv_p), st(c_p), st(n_p), st(m_p[:, :, 0]),
            st(k_new), st(v_new), st(conv_s), st(c_s), st(n_s), st(m_s[:, :, 0]))
```

```python
import functools
import math

import jax
import jax.numpy as jnp
import numpy as np
from jax import lax
from jax.experimental import pallas as pl
from jax.experimental.pallas import tpu as pltpu

F32 = jnp.float32
BF16 = jnp.bfloat16
U32 = jnp.uint32
NEG_INF = float("-inf")

D_MODEL = 1024
H_M = 4
D_M = 512
DK_M = 128
CONV_W = 4
H_A = 8
D_A = 512
HD_A = 64
BRANCHES = ((128, 1), (512, 4), (2048, 16))
WINDOW_MAX = 2048
QB = 128
N_BUCKETS = 32
MAX_DISTANCE = 2048
N_EXPERTS = 32
TOP_K = 4
D_FF = 1024
SWIGLU_LIMIT = 7.0
SWIGLU_ALPHA = 1.702
EPS = 1e-6

LANES = 128
C_U, C_V, C_O, C_QA, C_KA, C_VA, C_G = 0, 1024, 1536, 2048, 2560, 3072, 3584
D_IN_R = C_G + LANES

MLSTM_L_PROMPT = 256
MLSTM_L_SAMPLE = 32
ATT_QL = 512
MOE_M = 512
MOE_SUB = 256
TOK_TILE = 128
PACK_W = D_MODEL // 2
PACK_T = PACK_W // LANES
ROW_T = D_MODEL // LANES
HI_MASK = 0xFFFF0000
VMEM_LIMIT = 56 * 1024 * 1024


def _params(sem, vmem=None):
    return pltpu.CompilerParams(dimension_semantics=sem, vmem_limit_bytes=vmem or VMEM_LIMIT)


def _rms(x, g):
    return x * lax.rsqrt(jnp.mean(x * x, axis=-1, keepdims=True) + EPS) * g


def _log_sigmoid(t):
    return jnp.minimum(t, 0.0) - jnp.log1p(jnp.exp(-jnp.abs(t)))


def _ada_kernel(c_ref, w_ref, b_ref, o_ref):
    c = c_ref[...]
    a = (c * jax.nn.sigmoid(c)).astype(BF16)
    o_ref[...] = jnp.dot(a, w_ref[...].astype(BF16), preferred_element_type=F32) + b_ref[...]


def _adaln(c_all, w_ada, b_ada):
    rows = c_all.shape[0]
    n = w_ada.shape[1]
    tn = 1024
    return pl.pallas_call(
        _ada_kernel,
        grid=(n // tn,),
        in_specs=[
            pl.BlockSpec((rows, D_MODEL), lambda j: (0, 0)),
            pl.BlockSpec((D_MODEL, tn), lambda j: (0, j)),
            pl.BlockSpec((1, tn), lambda j: (0, j)),
        ],
        out_specs=pl.BlockSpec((rows, tn), lambda j: (0, j)),
        out_shape=jax.ShapeDtypeStruct((rows, n), F32),
        compiler_params=_params(("arbitrary",)),
        name="adaln",
    )(c_all, w_ada, b_ada.reshape(1, n))


def _to_dilated(x, dil):
    return pltpu.einshape("(lr)c->l(rc)", x, r=dil)


def _from_dilated(x, dil):
    return pltpu.einshape("l(rc)->(lr)c", x, r=dil)


def _inproj_kernel(x_ref, g_ref, sc_ref, sh_ref, w_ref, bg_ref,
                   u_ref, v_ref, o_ref, gt_ref, qa_ref, ka_ref, va_ref, *dil_refs, dilations):
    x = x_ref[0]
    h = _rms(x, g_ref[...]) * (1.0 + sc_ref[0]) + sh_ref[0]
    hb = h.astype(BF16)

    def seg(lo, width):
        return jnp.dot(hb, w_ref[:, lo:lo + width], preferred_element_type=F32)

    u_ref[0] = seg(C_U, 2 * D_M)
    v_ref[0] = seg(C_V, D_M)
    o_ref[0] = jax.nn.sigmoid(seg(C_O, D_M))
    qkv = (seg(C_QA, D_A) * (HD_A ** -0.5), seg(C_KA, D_A), seg(C_VA, D_A))
    for ref, val in zip((qa_ref, ka_ref, va_ref), qkv):
        ref[0] = val
    for n, dil in enumerate(dilations):
        for j, val in enumerate(qkv):
            dil_refs[3 * n + j][0] = _to_dilated(val, dil)
    t = seg(C_G, LANES) + bg_ref[...]
    lane = lax.broadcasted_iota(jnp.int32, t.shape, 1)
    gt_ref[0] = jnp.where(lane < H_M, t, _log_sigmoid(t))


def _inproj(x3, g_pre, scale, shift, w_r, bias_g, tm, dilations=()):
    nb, rows, _ = x3.shape
    per_row_mod = scale.shape[1] != 1
    mod_block = (1, tm, D_MODEL) if per_row_mod else (1, 1, D_MODEL)
    mod_map = (lambda b, i: (b, i, 0)) if per_row_mod else (lambda b, i: (b, 0, 0))
    row_map = lambda b, i: (b, i, 0)
    widths = (2 * D_M, D_M, D_M, LANES, D_A, D_A, D_A)
    dil_specs = [pl.BlockSpec((1, tm // d, d * D_A), row_map) for d in dilations for _ in range(3)]
    dil_shapes = [jax.ShapeDtypeStruct((nb, rows // d, d * D_A), F32) for d in dilations for _ in range(3)]
    return pl.pallas_call(
        functools.partial(_inproj_kernel, dilations=tuple(dilations)),
        grid=(nb, rows // tm),
        in_specs=[
            pl.BlockSpec((1, tm, D_MODEL), row_map),
            pl.BlockSpec((1, D_MODEL), lambda b, i: (0, 0)),
            pl.BlockSpec(mod_block, mod_map),
            pl.BlockSpec(mod_block, mod_map),
            pl.BlockSpec((D_MODEL, D_IN_R), lambda b, i: (0, 0)),
            pl.BlockSpec((1, LANES), lambda b, i: (0, 0)),
        ],
        out_specs=[pl.BlockSpec((1, tm, w), row_map) for w in widths] + dil_specs,
        out_shape=[jax.ShapeDtypeStruct((nb, rows, w), F32) for w in widths] + dil_shapes,
        compiler_params=_params(("arbitrary", "arbitrary")),
        name="inproj",
    )(x3, g_pre.reshape(1, D_MODEL), scale, shift, w_r, bias_g)


def _mlstm_kernel(u_ref, v_ref, og_ref, gc_ref, gr_ref, tail0_ref, c0_ref, n0_ref, m0_ref,
                  cw_ref, cb_ref, mh_ref,
                  mem_ref, c_out_ref, n_out_ref, m_out_ref,
                  c_s, n_s, m_s, tail_s, uext_s, *, chunk):
    L = chunk
    step = pl.program_id(1)

    @pl.when(step == 0)
    def _():
        c_s[...] = c0_ref[0]
        n_s[...] = n0_ref[0]
        m_s[...] = m0_ref[0]
        tail_s[...] = tail0_ref[0]

    u = u_ref[0]
    uext_s[0:8, :] = tail_s[...]
    uext_s[8:8 + L, :] = u
    acc = cb_ref[...] + uext_s[pl.ds(5, L), :] * cw_ref[0:1, :]
    acc = acc + uext_s[pl.ds(6, L), :] * cw_ref[1:2, :]
    acc = acc + uext_s[pl.ds(7, L), :] * cw_ref[2:3, :]
    acc = acc + u * cw_ref[3:4, :]
    tail_s[...] = uext_s[pl.ds(L, 8), :]
    qk = acc * jax.nn.sigmoid(acc)

    gc = gc_ref[0]
    gr = gr_ref[0]
    row = lax.broadcasted_iota(jnp.int32, (L, L), 0)
    col = lax.broadcasted_iota(jnp.int32, (L, L), 1)
    causal = col <= row
    tri = causal.astype(F32)
    tri_t = (row <= col).astype(F32)
    lane = lax.broadcasted_iota(jnp.int32, gc.shape, 1)
    srow = lax.broadcasted_iota(jnp.int32, gr.shape, 0)
    b_col_all = jnp.dot(tri, jnp.where(lane >= H_M, gc, 0.0), preferred_element_type=F32,
                        precision=lax.Precision.HIGHEST)
    b_row_all = jnp.dot(jnp.where(srow >= H_M, gr, 0.0), tri_t, preferred_element_type=F32,
                        precision=lax.Precision.HIGHEST)

    v_all = v_ref[0]
    og = og_ref[0]
    for h in range(H_M):
        q = qk[:, h * DK_M:(h + 1) * DK_M]
        k = qk[:, D_M + h * DK_M:D_M + (h + 1) * DK_M] * (DK_M ** -0.5)
        v = v_all[:, h * DK_M:(h + 1) * DK_M]
        li_row = gr[h:h + 1, :]
        li_col = gc[:, h:h + 1]
        b_row = b_row_all[H_M + h:H_M + h + 1, :]
        b_col = b_col_all[:, H_M + h:H_M + h + 1]
        m0 = m_s[h:h + 1, 0:1]
        c0 = c_s[h]
        n0 = n_s[h:h + 1, :]

        d = jnp.where(causal, b_col - b_row + li_row, NEG_INF)
        g_state = b_col + m0
        m = jnp.maximum(jnp.max(d, axis=-1, keepdims=True), g_state)
        w_state = jnp.exp(g_state - m)
        qb = q.astype(BF16)
        kb = k.astype(BF16)
        vb = v.astype(BF16)
        s = lax.dot_general(qb, kb, (((1,), (1,)), ((), ())), preferred_element_type=F32)
        s = s * jnp.exp(d - m)
        num = jnp.dot(s.astype(BF16), vb, preferred_element_type=F32)
        num = num + w_state * jnp.dot(qb, c0.astype(BF16), preferred_element_type=F32)
        den = jnp.sum(s, axis=-1, keepdims=True) + w_state * jnp.sum(q * n0, axis=-1, keepdims=True)
        hh = num / jnp.maximum(jnp.abs(den), jnp.exp(-m))

        b_last = b_col[L - 1:L, :]
        g_tok_row = b_last - b_row + li_row
        m_new = jnp.maximum(b_last + m0, jnp.max(g_tok_row, axis=-1, keepdims=True))
        w_tok_col = jnp.exp(b_last - b_col + li_col - m_new)
        decay = jnp.exp(b_last + m0 - m_new)
        kw = k * w_tok_col
        c_s[h] = decay * c0 + jnp.dot(kw.T.astype(BF16), vb, preferred_element_type=F32)
        n_s[h:h + 1, :] = decay * n0 + jnp.sum(kw, axis=0, keepdims=True)
        m_s[h:h + 1, :] = jnp.broadcast_to(m_new, (1, LANES))

        hn = hh * lax.rsqrt(jnp.mean(hh * hh, axis=-1, keepdims=True) + EPS)
        hn = hn * mh_ref[:, h * DK_M:(h + 1) * DK_M]
        mem_ref[0, :, h * DK_M:(h + 1) * DK_M] = og[:, h * DK_M:(h + 1) * DK_M] * hn

    @pl.when(step == pl.num_programs(1) - 1)
    def _():
        c_out_ref[0] = c_s[...]
        n_out_ref[0] = n_s[...]
        m_out_ref[0] = m_s[...]


def _mlstm(u, v, og, gates_col, gates_row, tail0, c0, n0, m0x, conv_w, conv_b, mh_norm, chunk):
    nb, rows, _ = u.shape
    nc = rows // chunk
    row_map = lambda b, c: (b, c, 0)
    bat3 = lambda b, c: (b, 0, 0)
    const2 = lambda b, c: (0, 0)
    return pl.pallas_call(
        functools.partial(_mlstm_kernel, chunk=chunk),
        grid=(nb, nc),
        in_specs=[
            pl.BlockSpec((1, chunk, 2 * D_M), row_map),
            pl.BlockSpec((1, chunk, D_M), row_map),
            pl.BlockSpec((1, chunk, D_M), row_map),
            pl.BlockSpec((1, chunk, LANES), row_map),
            pl.BlockSpec((1, 8, chunk), lambda b, c: (b, 0, c)),
            pl.BlockSpec((1, 8, 2 * D_M), bat3),
            pl.BlockSpec((1, H_M, DK_M, DK_M), lambda b, c: (b, 0, 0, 0)),
            pl.BlockSpec((1, H_M, DK_M), bat3),
            pl.BlockSpec((1, H_M, LANES), bat3),
            pl.BlockSpec((CONV_W, 2 * D_M), const2),
            pl.BlockSpec((1, 2 * D_M), const2),
            pl.BlockSpec((1, D_M), const2),
        ],
        out_specs=[
            pl.BlockSpec((1, chunk, D_M), row_map),
            pl.BlockSpec((1, H_M, DK_M, DK_M), lambda b, c: (b, 0, 0, 0)),
            pl.BlockSpec((1, H_M, DK_M), bat3),
            pl.BlockSpec((1, H_M, LANES), bat3),
        ],
        out_shape=[
            jax.ShapeDtypeStruct((nb, rows, D_M), F32),
            jax.ShapeDtypeStruct((nb, H_M, DK_M, DK_M), F32),
            jax.ShapeDtypeStruct((nb, H_M, DK_M), F32),
            jax.ShapeDtypeStruct((nb, H_M, LANES), F32),
        ],
        scratch_shapes=[
            pltpu.VMEM((H_M, DK_M, DK_M), F32),
            pltpu.VMEM((H_M, DK_M), F32),
            pltpu.VMEM((H_M, LANES), F32),
            pltpu.VMEM((8, 2 * D_M), F32),
            pltpu.VMEM((chunk + 8, 2 * D_M), F32),
        ],
        compiler_params=_params(("arbitrary", "arbitrary")),
        name="mlstm",
    )(u, v, og, gates_col, gates_row, tail0, c0, n0, m0x,
      conv_w, conv_b.reshape(1, 2 * D_M), mh_norm.reshape(1, D_M))


def _attn_kernel(q_ref, kc_ref, vc_ref, kp_ref, vp_ref, bias_ref, ones_ref, o_ref, l_ref, k_s, v_s, *, ql):
    first = pl.program_id(2) == 0
    k_s[0:QB, :] = kp_ref[0].astype(BF16)
    k_s[QB:QB + ql, :] = kc_ref[0].astype(BF16)
    v_s[0:QB, :] = vp_ref[0].astype(BF16)
    v_s[QB:QB + ql, :] = vc_ref[0].astype(BF16)
    col = lax.broadcasted_iota(jnp.int32, (QB, 2 * QB), 1)
    pad_keys = jnp.logical_and(first, col < QB)
    head0 = lax.broadcasted_iota(jnp.int32, (1, LANES), 1) < HD_A
    nt = (((1,), (1,)), ((), ()))
    for c in range(D_A // LANES):
        sl = slice(c * LANES, (c + 1) * LANES)
        v_all = v_s[:, sl]
        v_lo = jnp.where(head0, v_all, jnp.zeros_like(v_all))
        v_hi = jnp.where(head0, jnp.zeros_like(v_all), v_all)
        for j in range(ql // QB):
            rows = slice(j * QB, (j + 1) * QB)
            keys = slice(j * QB, (j + 2) * QB)
            q = q_ref[0, rows, sl].astype(BF16)
            kb = k_s[keys, sl]
            ps, ms = [], []
            for half in range(2):
                qh = jnp.where(head0, q, jnp.zeros_like(q)) if half == 0 else jnp.where(head0, jnp.zeros_like(q), q)
                logits = lax.dot_general(qh, kb, nt, preferred_element_type=F32) + bias_ref[2 * c + half]
                if j == 0:
                    logits = jnp.where(pad_keys, NEG_INF, logits)
                m = jnp.max(logits, axis=-1, keepdims=True)
                ps.append(jnp.exp(logits - m).astype(BF16))
                ms.append(m)
            p_cat = jnp.concatenate(ps, axis=1)
            v_bd = jnp.concatenate([v_lo[keys], v_hi[keys]], axis=0)
            o = jnp.dot(p_cat, v_bd, preferred_element_type=F32)
            s = jnp.dot(p_cat, ones_ref[...], preferred_element_type=F32)
            o_ref[0, rows, sl] = o / s
            l_ref[0, rows, sl] = jnp.where(head0, ms[0], ms[1]) + jnp.log(s)


def _attn_branch(qa, ka, va, bias, dil):
    nb, ld, _ = qa.shape
    ql = min(ATT_QL, ld)
    cur = lambda b, r, i: (b, i, r)
    prev = lambda b, r, i: (b, jnp.maximum(i * (ql // QB) - 1, 0), r)
    ones_bd = np.zeros((4 * QB, LANES), np.float32)
    ones_bd[:2 * QB, :HD_A] = 1.0
    ones_bd[2 * QB:, HD_A:] = 1.0
    return pl.pallas_call(
        functools.partial(_attn_kernel, ql=ql),
        grid=(nb, dil, ld // ql),
        in_specs=[
            pl.BlockSpec((1, ql, D_A), cur),
            pl.BlockSpec((1, ql, D_A), cur),
            pl.BlockSpec((1, ql, D_A), cur),
            pl.BlockSpec((1, QB, D_A), prev),
            pl.BlockSpec((1, QB, D_A), prev),
            pl.BlockSpec((H_A, QB, 2 * QB), lambda b, r, i: (0, 0, 0)),
            pl.BlockSpec((4 * QB, LANES), lambda b, r, i: (0, 0)),
        ],
        out_specs=[pl.BlockSpec((1, ql, D_A), cur), pl.BlockSpec((1, ql, D_A), cur)],
        out_shape=[jax.ShapeDtypeStruct((nb, ld, dil * D_A), F32)] * 2,
        scratch_shapes=[pltpu.VMEM((QB + ql, D_A), BF16), pltpu.VMEM((QB + ql, D_A), BF16)],
        compiler_params=_params(("arbitrary", "arbitrary", "arbitrary")),
        name=f"attn_d{dil}",
    )(qa, ka, va, ka, va, bias, jnp.asarray(ones_bd, BF16))


def _heads_kernel(k_ref, v_ref, ko_ref, vo_ref):
    for src, dst in ((k_ref, ko_ref), (v_ref, vo_ref)):
        for c in range(D_A // LANES):
            xt = src[0, :, c * LANES:(c + 1) * LANES].T
            dst[0, 2 * c] = xt[:HD_A]
            dst[0, 2 * c + 1] = xt[HD_A:]


def _window_heads(ka, va, keep, tm=512):
    nb, seq, _ = ka.shape
    first = (seq - keep) // tm
    in_spec = pl.BlockSpec((1, tm, D_A), lambda b, i: (b, first + i, 0))
    out_spec = pl.BlockSpec((1, H_A, HD_A, tm), lambda b, i: (b, 0, 0, i))
    kt, vt = pl.pallas_call(
        _heads_kernel,
        grid=(nb, keep // tm),
        in_specs=[in_spec, in_spec],
        out_specs=[out_spec, out_spec],
        out_shape=[jax.ShapeDtypeStruct((nb, H_A, HD_A, keep), F32)] * 2,
        compiler_params=_params(("arbitrary", "arbitrary")),
        name="window_heads",
    )(ka, va)
    return kt.transpose(0, 3, 1, 2), vt.transpose(0, 3, 1, 2)


def _t5_bucket(dist):
    max_exact = N_BUCKETS // 2
    n = jnp.maximum(dist, 1).astype(F32)
    large = max_exact + (jnp.log(n / max_exact) / math.log(MAX_DISTANCE / max_exact)
                         * (N_BUCKETS - max_exact)).astype(jnp.int32)
    large = jnp.minimum(large, N_BUCKETS - 1)
    return jnp.where(dist < max_exact, dist, large)


def _branch_bias_vec(rel_bias, window, dil):
    return rel_bias[_t5_bucket(dil * jnp.arange(window // dil + 1))].astype(F32)


def _prompt_bias(rel_bias, window, dil):
    assert window // dil == QB
    n = 2 * QB
    vec = _branch_bias_vec(rel_bias, window, dil)
    vecp = jnp.concatenate([vec[::-1], jnp.full((n - QB - 1, H_A), NEG_INF, F32)], axis=0)
    g = jnp.roll(vecp, QB - 1, axis=0)
    skew = jnp.tile(g, (QB + 1, 1))[:QB * (n + 1)].reshape(QB, n + 1, H_A)[:, :n]
    return skew[::-1].transpose(2, 0, 1)


TOK_PAD = 8


def _sattn_kernel(q_ref, kt_ref, vt_ref, knt_ref, vnt_ref, bc_ref, bn_ref, att_ref):
    nt = (((1,), (1,)), ((), ()))
    for h in range(H_A):
        q = q_ref[0, h].astype(BF16)
        lc = jnp.dot(q, kt_ref[0, h].astype(BF16), preferred_element_type=F32)
        ln = jnp.dot(q, knt_ref[0, h].astype(BF16), preferred_element_type=F32)
        lcs = [lc + bc_ref[d, h] for d in range(len(BRANCHES))]
        lns = [ln + bn_ref[d, h] for d in range(len(BRANCHES))]
        m = functools.reduce(jnp.maximum, [jnp.max(x, axis=-1, keepdims=True) for x in lcs + lns])
        pc = functools.reduce(jnp.add, [jnp.exp(x - m) for x in lcs])
        pn = functools.reduce(jnp.add, [jnp.exp(x - m) for x in lns])
        s = jnp.sum(pc, axis=-1, keepdims=True) + jnp.sum(pn, axis=-1, keepdims=True)
        o = lax.dot_general(pc.astype(BF16), vt_ref[0, h].astype(BF16), nt, preferred_element_type=F32)
        o = o + lax.dot_general(pn.astype(BF16), vnt_ref[0, h].astype(BF16), nt, preferred_element_type=F32)
        att_ref[0, h] = o / s


def _sample_attention(q_h, kt, vt, knt, vnt, bias_c, bias_n):
    nb, _, _, n_past = kt.shape
    bat = lambda b: (b, 0, 0, 0)
    const = lambda b: (0, 0, 0, 0)
    return pl.pallas_call(
        _sattn_kernel,
        grid=(nb,),
        in_specs=[
            pl.BlockSpec((1, H_A, TOK_PAD, HD_A), bat),
            pl.BlockSpec((1, H_A, HD_A, n_past), bat),
            pl.BlockSpec((1, H_A, HD_A, n_past), bat),
            pl.BlockSpec((1, H_A, HD_A, LANES), bat),
            pl.BlockSpec((1, H_A, HD_A, LANES), bat),
            pl.BlockSpec(bias_c.shape, const),
            pl.BlockSpec(bias_n.shape, const),
        ],
        out_specs=pl.BlockSpec((1, H_A, TOK_PAD, HD_A), bat),
        out_shape=jax.ShapeDtypeStruct((nb, H_A, TOK_PAD, HD_A), F32),
        compiler_params=_params(("arbitrary",)),
        name="sample_attn",
    )(q_h, kt, vt, knt, vnt, bias_c, bias_n)


def _sample_bias(rel_bias, n_tok, n_past):
    ninf = lambda n: jnp.full((n, H_A), NEG_INF, F32)
    cached, new = [], []
    for window, dil in BRANCHES:
        nk = window // dil
        rows = n_past // dil
        assert n_past % dil == 0 and rows >= nk
        vec = _branch_bias_vec(rel_bias, window, dil)
        res = jnp.arange(dil)[None, :, None]
        tc, tn = [], []
        for t in range(n_tok):
            near = t // dil + 1
            col = jnp.concatenate([ninf(rows - (nk + 1 - near)), vec[near:nk + 1][::-1]], axis=0)
            tab = jnp.where(res == t % dil, col[:, None, :], NEG_INF).reshape(n_past, H_A)
            tc.append(tab)
            keys = [vec[(t - j) // dil][None] if (t - j) % dil == 0 and 0 <= t - j <= nk * dil else ninf(1)
                    for j in range(n_tok)]
            tn.append(jnp.concatenate(keys + [ninf(LANES - n_tok)], axis=0))
        pad = lambda tabs: jnp.stack(tabs + [jnp.zeros_like(tabs[0])] * (TOK_PAD - n_tok))
        cached.append(pad(tc).transpose(2, 0, 1))
        new.append(pad(tn).transpose(2, 0, 1))
    return jnp.stack(cached), jnp.stack(new)


def _postmix_kernel(mem_ref, o1_ref, l1_ref, o2_ref, l2_ref, o3_ref, l3_ref, x_ref,
                    g1_ref, sc2_ref, sh2_ref, gpost_ref, gpre_ref, wo_ref, wr_ref, br_ref,
                    x1_ref, h2_ref, idx_ref, gate_ref, *, dilations):
    rows = lambda ref, dil: ref[0] if dil == 1 else _from_dilated(ref[0], dil)
    o1, o2, o3 = (rows(r, d) for r, d in zip((o1_ref, o2_ref, o3_ref), dilations))
    l1, l2, l3 = (rows(r, d) for r, d in zip((l1_ref, l2_ref, l3_ref), dilations))
    lmax = jnp.maximum(jnp.maximum(l1, l2), l3)
    w1 = jnp.exp(l1 - lmax)
    w2 = jnp.exp(l2 - lmax)
    w3 = jnp.exp(l3 - lmax)
    att = (w1 * o1 + w2 * o2 + w3 * o3) / (w1 + w2 + w3)
    y = jnp.dot(mem_ref[0].astype(BF16), wo_ref[0:D_M, :], preferred_element_type=F32)
    y = y + jnp.dot(att.astype(BF16), wo_ref[D_M:, :], preferred_element_type=F32)
    x1 = x_ref[0] + g1_ref[0] * _rms(y, gpost_ref[...])
    x1_ref[0] = x1
    h2 = _rms(x1, gpre_ref[...]) * (1.0 + sc2_ref[0]) + sh2_ref[0]
    h2_ref[0] = h2
    h_hi = h2.astype(BF16)
    h_lo = (h2 - h_hi.astype(F32)).astype(BF16)
    w_hi = wr_ref[0]
    logits = jnp.dot(h_hi, w_hi, preferred_element_type=F32)
    logits = logits + jnp.dot(h_lo, w_hi, preferred_element_type=F32)
    logits = logits + jnp.dot(h_hi, wr_ref[1], preferred_element_type=F32) + br_ref[...]
    lane = lax.broadcasted_iota(jnp.int32, logits.shape, 1)
    lane_f = lane.astype(F32)
    logits = jnp.where(lane < N_EXPERTS, logits, NEG_INF)
    vals, idxs = [], []
    for _ in range(TOP_K):
        mx = jnp.max(logits, axis=-1, keepdims=True)
        ix = jnp.min(jnp.where(logits == mx, lane_f, float(LANES)), axis=-1, keepdims=True)
        vals.append(mx)
        idxs.append(ix)
        logits = jnp.where(lane_f == ix, NEG_INF, logits)
    es = [jnp.exp(v - vals[0]) for v in vals]
    tot = es[0] + es[1] + es[2] + es[3]
    idx_tile = jnp.zeros(logits.shape, F32)
    gate_tile = jnp.zeros(logits.shape, F32)
    for k in range(TOP_K):
        idx_tile = jnp.where(lane == k, idxs[k], idx_tile)
        gate_tile = jnp.where(lane == k, es[k] / tot, gate_tile)
    idx_ref[0] = idx_tile.astype(jnp.int32)
    gate_ref[0] = gate_tile


def _postmix(mem, branches, x3, gate1, scale2, shift2, g_post, g_pre, w_out_b, w_router_p, b_router_p, tm,
             dilations=(1, 1, 1)):
    nb, rows, _ = x3.shape
    per_row_mod = gate1.shape[1] != 1
    mod_block = (1, tm, D_MODEL) if per_row_mod else (1, 1, D_MODEL)
    mod_map = (lambda b, i: (b, i, 0)) if per_row_mod else (lambda b, i: (b, 0, 0))
    row_map = lambda b, i: (b, i, 0)
    const2 = lambda b, i: (0, 0)
    half = pl.BlockSpec((1, tm, D_M), row_map)
    flat = [a for pair in branches for a in pair]
    branch_specs = [pl.BlockSpec((1, tm // d, d * D_A), row_map) for d in dilations for _ in range(2)]
    return pl.pallas_call(
        functools.partial(_postmix_kernel, dilations=tuple(dilations)),
        grid=(nb, rows // tm),
        in_specs=[half] + branch_specs + [
            pl.BlockSpec((1, tm, D_MODEL), row_map),
            pl.BlockSpec(mod_block, mod_map),
            pl.BlockSpec(mod_block, mod_map),
            pl.BlockSpec(mod_block, mod_map),
            pl.BlockSpec((1, D_MODEL), const2),
            pl.BlockSpec((1, D_MODEL), const2),
            pl.BlockSpec((D_MODEL, D_MODEL), const2),
            pl.BlockSpec((2, D_MODEL, LANES), lambda b, i: (0, 0, 0)),
            pl.BlockSpec((1, LANES), const2),
        ],
        out_specs=[
            pl.BlockSpec((1, tm, D_MODEL), row_map),
            pl.BlockSpec((1, tm, D_MODEL), row_map),
            pl.BlockSpec((1, tm, LANES), row_map),
            pl.BlockSpec((1, tm, LANES), row_map),
        ],
        out_shape=[
            jax.ShapeDtypeStruct((nb, rows, D_MODEL), F32),
            jax.ShapeDtypeStruct((nb, rows, D_MODEL), F32),
            jax.ShapeDtypeStruct((nb, rows, LANES), jnp.int32),
            jax.ShapeDtypeStruct((nb, rows, LANES), F32),
        ],
        compiler_params=_params(("arbitrary", "arbitrary")),
        name="postmix",
    )(mem, *flat, x3, gate1, scale2, shift2, g_post.reshape(1, D_MODEL), g_pre.reshape(1, D_MODEL),
      w_out_b, w_router_p, b_router_p)


def _row_wait(src_rows, dst_rows, sem):
    pltpu.make_async_copy(src_rows, dst_rows, sem).wait()


def _scatter_kernel(slot_ref, hp_ref, hs_ref, xs_hbm, buf, sem, *, n_prompt_tiles):
    h = jnp.where(pl.program_id(0) >= n_prompt_tiles, hs_ref[...], hp_ref[...])
    bits = pltpu.bitcast(h.astype(BF16).astype(F32), U32)
    packed = (bits[:, :PACK_W] & U32(HI_MASK)) | (bits[:, PACK_W:] >> 16)
    buf[...] = pltpu.einshape("m(jc)->mjc", packed, j=PACK_T)

    def issue(r, carry):
        src = buf.at[pl.ds(r, 1)]
        for k in range(TOP_K):
            pltpu.make_async_copy(src, xs_hbm.at[pl.ds(slot_ref[0, 0, r * TOP_K + k], 1)],
                                  sem).start(priority=k % 2)
        return carry

    lax.fori_loop(0, TOK_TILE, issue, 0, unroll=4)
    for _ in range(TOP_K):
        _row_wait(buf, xs_hbm.at[pl.ds(0, TOK_TILE)], sem)


def _moe_scatter(h_p, h_s, slot_tiles):
    tiles = slot_tiles.shape[0]
    n_p = h_p.shape[0] // TOK_TILE
    assert tiles == n_p + 1 and h_s.shape[0] == TOK_TILE
    n_rows = tiles * TOK_TILE * TOP_K
    return pl.pallas_call(
        functools.partial(_scatter_kernel, n_prompt_tiles=n_p),
        grid=(tiles,),
        in_specs=[
            pl.BlockSpec((1, 1, TOK_TILE * TOP_K), lambda i: (i, 0, 0), memory_space=pltpu.SMEM),
            pl.BlockSpec((TOK_TILE, D_MODEL), lambda i: (jnp.minimum(i, n_p - 1), 0)),
            pl.BlockSpec((TOK_TILE, D_MODEL), lambda i: (0, 0)),
        ],
        out_specs=pl.BlockSpec(memory_space=pl.ANY),
        out_shape=jax.ShapeDtypeStruct((n_rows, PACK_T, LANES), U32),
        scratch_shapes=[pltpu.VMEM((TOK_TILE, PACK_T, LANES), U32), pltpu.SemaphoreType.DMA],
        compiler_params=_params(("arbitrary",)),
        name="moe_scatter",
    )(slot_tiles, h_p, h_s)


def _expert_kernel(tile_ref, exp_ref, lo_ref, hi_ref, first_ref, nvis_ref,
                   xs_ref, w1_ref, b1_ref, w2_ref, b2_ref, ys_ref, w1b, w2b):
    v = pl.program_id(0)
    active = v < nvis_ref[0]
    changed = jnp.logical_or(v == 0, exp_ref[v] != exp_ref[jnp.maximum(v - 1, 0)])
    lo = lo_ref[v]
    hi = hi_ref[v]

    @pl.when(jnp.logical_and(active, changed))
    def _():
        w1b[...] = w1_ref[0].astype(BF16)
        w2b[...] = w2_ref[0].astype(BF16)

    def mlp_rows(r0):
        w = pltpu.einshape("mjc->m(jc)", xs_ref[r0:r0 + MOE_SUB])
        x_hi = pltpu.bitcast(w & U32(HI_MASK), F32).astype(BF16)
        x_lo = pltpu.bitcast(w << 16, F32).astype(BF16)
        hgu = jnp.dot(x_hi, w1b[0:PACK_W, :], preferred_element_type=F32)
        hgu = hgu + jnp.dot(x_lo, w1b[PACK_W:, :], preferred_element_type=F32) + b1_ref[0]
        x_glu = jnp.minimum(hgu[:, :D_FF], SWIGLU_LIMIT)
        x_lin = jnp.clip(hgu[:, D_FF:], -SWIGLU_LIMIT, SWIGLU_LIMIT)
        act = x_glu * jax.nn.sigmoid(SWIGLU_ALPHA * x_glu) * (x_lin + 1.0)
        y = jnp.dot(act.astype(BF16), w2b[...], preferred_element_type=F32) + b2_ref[0]
        return pltpu.einshape("m(jc)->mjc", y, j=ROW_T)

    subs = [r * MOE_SUB for r in range(MOE_M // MOE_SUB)]
    whole = jnp.logical_and(lo == 0, hi == MOE_M)

    @pl.when(jnp.logical_and(active, whole))
    def _():
        for r0 in subs:
            ys_ref[r0:r0 + MOE_SUB] = mlp_rows(r0)

    shared = jnp.logical_and(active, jnp.logical_not(whole))

    @pl.when(jnp.logical_and(shared, first_ref[v] == 1))
    def _():
        ys_ref[...] = jnp.zeros(ys_ref.shape, F32)

    for r0 in subs:
        @pl.when(jnp.logical_and(shared, jnp.logical_and(lo < r0 + MOE_SUB, hi > r0)))
        def _():
            rows = r0 + lax.broadcasted_iota(jnp.int32, (MOE_SUB, 1, 1), 0)
            mine = jnp.logical_and(rows >= lo, rows < hi)
            ys_ref[r0:r0 + MOE_SUB] = jnp.where(mine, mlp_rows(r0), ys_ref[r0:r0 + MOE_SUB])


def _moe_experts(xs, visits, w1, b1, w2, b2):
    n_rows = xs.shape[0]
    n_vis = visits[0].shape[0]
    tile_map = lambda v, tile, *_: (tile[v], 0, 0)
    exp_map = lambda v, tile, exp, *_: (exp[v], 0, 0)
    return pl.pallas_call(
        _expert_kernel,
        grid_spec=pltpu.PrefetchScalarGridSpec(
            num_scalar_prefetch=6,
            grid=(n_vis,),
            in_specs=[
                pl.BlockSpec((MOE_M, PACK_T, LANES), tile_map),
                pl.BlockSpec((1, D_MODEL, 2 * D_FF), exp_map),
                pl.BlockSpec((1, 1, 2 * D_FF), exp_map),
                pl.BlockSpec((1, D_FF, D_MODEL), exp_map),
                pl.BlockSpec((1, 1, D_MODEL), exp_map),
            ],
            out_specs=pl.BlockSpec((MOE_M, ROW_T, LANES), tile_map),
            scratch_shapes=[pltpu.VMEM((D_MODEL, 2 * D_FF), BF16), pltpu.VMEM((D_FF, D_MODEL), BF16)],
        ),
        out_shape=jax.ShapeDtypeStruct((n_rows, ROW_T, LANES), F32),
        compiler_params=_params(("arbitrary",)),
        name="moe_experts",
    )(*visits, xs, w1, b1.reshape(N_EXPERTS, 1, 2 * D_FF), w2, b2.reshape(N_EXPERTS, 1, D_MODEL))


def _combine_kernel(slot_ref, next_ref, gate_ref, x1p_ref, x1s_ref, g2p_ref, g2s_ref, gpost_ref, ys_hbm,
                    yp_ref, ysm_ref, buf, sem, *, n_prompt_tiles):
    i = pl.program_id(0)
    last = pl.num_programs(0) - 1
    cur = lax.rem(i, 2)

    def gather(slots, b):
        def issue(r, carry):
            for k in range(TOP_K):
                pltpu.make_async_copy(ys_hbm.at[pl.ds(slots[0, 0, r * TOP_K + k], 1)],
                                      buf.at[b, pl.ds(k * TOK_TILE + r, 1)], sem.at[b]).start(priority=k % 2)
            return carry

        lax.fori_loop(0, TOK_TILE, issue, 0, unroll=4)

    @pl.when(i == 0)
    def _():
        gather(slot_ref, cur)

    @pl.when(i < last)
    def _():
        gather(next_ref, 1 - cur)

    for k in range(TOP_K):
        _row_wait(ys_hbm.at[pl.ds(0, TOK_TILE)], buf.at[cur, pl.ds(k * TOK_TILE, TOK_TILE)], sem.at[cur])
    g = gate_ref[...]
    f = jnp.zeros((TOK_TILE, D_MODEL), F32)
    for k in range(TOP_K):
        rows = pltpu.einshape("mjc->m(jc)", buf[cur, pl.ds(k * TOK_TILE, TOK_TILE)])
        f = f + rows * g[:, k:k + 1]
    is_sample = i >= n_prompt_tiles
    x1 = jnp.where(is_sample, x1s_ref[...], x1p_ref[...])
    g2 = jnp.where(is_sample, g2s_ref[...], g2p_ref[0])
    y = x1 + g2 * _rms(f, gpost_ref[...])

    @pl.when(jnp.logical_not(is_sample))
    def _():
        yp_ref[...] = y

    @pl.when(is_sample)
    def _():
        ysm_ref[...] = y


def _moe_combine(ys, slot_tiles, gates, x1_p, x1_s, gate2_p, gate2_s, g_post, rows_per_batch):
    tiles = slot_tiles.shape[0]
    n_p = x1_p.shape[0] // TOK_TILE
    assert tiles == n_p + 1 and x1_s.shape[0] == TOK_TILE
    tiles_per_batch = rows_per_batch // TOK_TILE
    p_tile = lambda i: (jnp.minimum(i, n_p - 1), 0)
    const2 = lambda i: (0, 0)
    return pl.pallas_call(
        functools.partial(_combine_kernel, n_prompt_tiles=n_p),
        grid=(tiles,),
        in_specs=[
            pl.BlockSpec((1, 1, TOK_TILE * TOP_K), lambda i: (i, 0, 0), memory_space=pltpu.SMEM),
            pl.BlockSpec((1, 1, TOK_TILE * TOP_K), lambda i: (jnp.minimum(i + 1, tiles - 1), 0, 0),
                         memory_space=pltpu.SMEM),
            pl.BlockSpec((TOK_TILE, LANES), lambda i: (i, 0)),
            pl.BlockSpec((TOK_TILE, D_MODEL), p_tile),
            pl.BlockSpec((TOK_TILE, D_MODEL), const2),
            pl.BlockSpec((1, 1, D_MODEL), lambda i: (jnp.minimum(i, n_p - 1) // tiles_per_batch, 0, 0)),
            pl.BlockSpec((TOK_TILE, D_MODEL), const2),
            pl.BlockSpec((1, D_MODEL), const2),
            pl.BlockSpec(memory_space=pl.ANY),
        ],
        out_specs=[pl.BlockSpec((TOK_TILE, D_MODEL), p_tile), pl.BlockSpec((TOK_TILE, D_MODEL), const2)],
        out_shape=[jax.ShapeDtypeStruct(x1_p.shape, F32), jax.ShapeDtypeStruct(x1_s.shape, F32)],
        scratch_shapes=[pltpu.VMEM((2, TOP_K * TOK_TILE, ROW_T, LANES), F32), pltpu.SemaphoreType.DMA((2,))],
        compiler_params=_params(("arbitrary",)),
        name="moe_combine",
    )(slot_tiles, slot_tiles, gates, x1_p, x1_s, gate2_p, gate2_s, g_post.reshape(1, D_MODEL), ys)


def _moe_plan(top_idx):
    a = top_idx.shape[0] * TOP_K
    assert a % MOE_M == 0
    flat_e = top_idx[:, :TOP_K].reshape(a)
    onehot = (flat_e[:, None] == jnp.arange(N_EXPERTS)[None, :]).astype(jnp.int32)
    csum = jnp.cumsum(onehot, axis=0)
    counts = csum[-1]
    g_end = jnp.cumsum(counts)
    g_start = g_end - counts
    rank = jnp.sum(csum * onehot, axis=1) - 1
    slot_of = (jnp.sum(g_start[None, :] * onehot, axis=1) + rank).astype(jnp.int32)

    n_tiles = a // MOE_M
    n_vis = n_tiles + N_EXPERTS - 1
    first_tile = g_start // MOE_M
    last_tile = jnp.where(counts > 0, (g_end - 1) // MOE_M, first_tile)
    nvis_e = jnp.where(counts > 0, last_tile - first_tile + 1, 0)
    v_end = jnp.cumsum(nvis_e)
    v_start = v_end - nvis_e
    total = v_end[-1]
    v = jnp.arange(n_vis)
    vc = jnp.minimum(v, total - 1)
    e_of = jnp.minimum(jnp.sum((vc[:, None] >= v_end[None, :]).astype(jnp.int32), axis=1), N_EXPERTS - 1)
    tile_of = first_tile[e_of] + (vc - v_start[e_of])
    lo = jnp.maximum(g_start[e_of], tile_of * MOE_M) - tile_of * MOE_M
    hi = jnp.minimum(g_end[e_of], (tile_of + 1) * MOE_M) - tile_of * MOE_M
    prev_tile = jnp.concatenate([jnp.full((1,), -1, tile_of.dtype), tile_of[:-1]])
    first = (tile_of != prev_tile).astype(jnp.int32)
    i32 = lambda t: t.astype(jnp.int32)
    visits = (i32(tile_of), i32(e_of), i32(lo), i32(hi), first, i32(total).reshape(1))
    return slot_of, visits


def _reorder_w_in(w_in):
    off_om_end = 4 * D_M
    off_g_end = off_om_end + 2 * H_M
    w = jnp.concatenate([
        w_in[:, :off_om_end],
        w_in[:, off_g_end:],
        w_in[:, off_om_end:off_g_end],
        jnp.zeros((D_MODEL, LANES - 2 * H_M), w_in.dtype),
    ], axis=1)
    return w.astype(BF16)


def kernel(x_prompt, x_sample, cache_k, cache_v, state_conv, state_C, state_n, state_m, c_prompt, c_sample, w_ada, b_ada, g_pre_mix, g_post_mix, g_pre_ffn, g_post_ffn, w_in, b_ig, b_fg, conv_w, conv_b, mh_norm, rel_bias, w_out, w_router, b_router, w1, b1, w2, b2):
    depth = w_in.shape[0]
    assert depth == 1
    l = 0
    nb_p, seq, _ = x_prompt.shape
    nb_s, n_tok, _ = x_sample.shape
    rows_s = nb_s * n_tok
    assert rows_s == TOK_TILE

    n_c = nb_p + nb_s
    c_pad = -(-n_c // 8) * 8
    c_all = jnp.concatenate([c_prompt, c_sample, jnp.zeros((c_pad - n_c, D_MODEL), F32)], axis=0)
    mod = _adaln(c_all, w_ada[l], b_ada[l])
    mods = [mod[:, i * D_MODEL:(i + 1) * D_MODEL] for i in range(6)]
    mods_p = [m[:nb_p].reshape(nb_p, 1, D_MODEL) for m in mods]
    mods_s = [jnp.repeat(m[nb_p:n_c], n_tok, axis=0).reshape(1, rows_s, D_MODEL) for m in mods]

    w_r = _reorder_w_in(w_in[l])
    bias_g = jnp.concatenate([b_ig[l], b_fg[l], jnp.zeros((LANES - 2 * H_M,), F32)]).reshape(1, LANES)
    w_out_b = w_out[l].astype(BF16)
    w_router_f = jnp.concatenate([w_router[l], jnp.zeros((D_MODEL, LANES - N_EXPERTS), F32)], axis=1)
    w_router_hi = w_router_f.astype(BF16)
    w_router_p = jnp.stack([w_router_hi, (w_router_f - w_router_hi.astype(F32)).astype(BF16)])
    b_router_p = jnp.concatenate([b_router[l].astype(F32), jnp.zeros((LANES - N_EXPERTS,), F32)]).reshape(1, LANES)

    shift1, scale1, gate1, shift2, scale2, gate2_p = mods_p
    dils = tuple(d for _, d in BRANCHES)
    assert dils[0] == 1
    u, v_m, og, gts, qa, ka, va, *dilated = _inproj(x_prompt, g_pre_mix[l], scale1, shift1, w_r, bias_g, tm=512,
                                                   dilations=dils[1:])
    qkv_d = [(qa, ka, va)] + [tuple(dilated[3 * n:3 * n + 3]) for n in range(len(dils) - 1)]
    gates_row = gts[:, :, :8].transpose(0, 2, 1)
    zeros = lambda *s: jnp.zeros(s, F32)
    mem, c_p, n_p, m_p = _mlstm(u, v_m, og, gts, gates_row, zeros(nb_p, 8, 2 * D_M),
                                zeros(nb_p, H_M, DK_M, DK_M), zeros(nb_p, H_M, DK_M), zeros(nb_p, H_M, LANES),
                                conv_w[l], conv_b[l], mh_norm[l], MLSTM_L_PROMPT)
    branches = [_attn_branch(*qkv, _prompt_bias(rel_bias, w, d), d) for qkv, (w, d) in zip(qkv_d, BRANCHES)]
    x1_p, h2_p, idx_p, gate_p = _postmix(mem, branches, x_prompt, gate1, scale2, shift2,
                                         g_post_mix[l], g_pre_ffn[l], w_out_b, w_router_p, b_router_p, tm=256,
                                         dilations=dils)
    keep = min(WINDOW_MAX, seq)
    k_p, v_p = _window_heads(ka, va, keep)
    conv_p = u[:, seq - (CONV_W - 1):]

    shift1, scale1, gate1, shift2, scale2, gate2_s = mods_s
    x_s = x_sample.reshape(1, rows_s, D_MODEL)
    u_s, v_s, og_s, gts_s, qa_s, ka_s, va_s = _inproj(x_s, g_pre_mix[l], scale1, shift1, w_r, bias_g, tm=rows_s)
    per_b = lambda t: t.reshape(nb_s, n_tok, t.shape[-1])
    ls = MLSTM_L_SAMPLE
    pad_rows = lambda t, n: jnp.concatenate([t, jnp.zeros((nb_s, n - t.shape[1], t.shape[2]), t.dtype)], axis=1)
    inert = jnp.concatenate([jnp.full((H_M,), NEG_INF, F32), jnp.zeros((LANES - H_M,), F32)])
    gts_pad = jnp.concatenate([per_b(gts_s), jnp.broadcast_to(inert, (nb_s, ls - n_tok, LANES))], axis=1)
    tail0 = jnp.concatenate([zeros(nb_s, 8 - (CONV_W - 1), 2 * D_M), state_conv[l].astype(F32)], axis=1)
    m0x = jnp.broadcast_to(state_m[l].astype(F32)[:, :, None], (nb_s, H_M, LANES))
    mem_s, c_s, n_s, m_s = _mlstm(pad_rows(per_b(u_s), ls), pad_rows(per_b(v_s), ls), pad_rows(per_b(og_s), ls),
                                  gts_pad, gts_pad[:, :, :8].transpose(0, 2, 1), tail0,
                                  state_C[l].astype(F32), state_n[l].astype(F32), m0x,
                                  conv_w[l], conv_b[l], mh_norm[l], ls)
    mem_s = mem_s[:, :n_tok].reshape(1, rows_s, D_M)
    heads = lambda t: t.reshape(nb_s, n_tok, H_A, HD_A)
    k_new, v_new = heads(ka_s), heads(va_s)
    n_past = cache_k.shape[2]
    assert n_tok <= TOK_PAD
    q_h = jnp.pad(heads(qa_s).transpose(0, 2, 1, 3), ((0, 0), (0, 0), (0, TOK_PAD - n_tok), (0, 0)))
    new_t = lambda t: jnp.pad(t.transpose(0, 2, 3, 1), ((0, 0), (0, 0), (0, 0), (0, LANES - n_tok)))
    cache_t = lambda c: c[l].astype(F32).transpose(0, 2, 3, 1)
    att_s = _sample_attention(q_h, cache_t(cache_k), cache_t(cache_v), new_t(k_new), new_t(v_new),
                              *_sample_bias(rel_bias, n_tok, n_past))
    att_s = att_s[:, :, :n_tok].transpose(0, 2, 1, 3).reshape(1, rows_s, D_A)
    zero_o = zeros(1, rows_s, D_A)
    ninf = jnp.full((1, rows_s, D_A), NEG_INF, F32)
    branches_s = [(att_s, zero_o), (zero_o, ninf), (zero_o, ninf)]
    x1_s, h2_s, idx_s, gate_s = _postmix(mem_s, branches_s, x_s, gate1, scale2, shift2,
                                         g_post_mix[l], g_pre_ffn[l], w_out_b, w_router_p, b_router_p, tm=rows_s)

    t_p = nb_p * seq
    idx_all = jnp.concatenate([idx_p.reshape(t_p, LANES), idx_s.reshape(rows_s, LANES)], axis=0)
    gate_all = jnp.concatenate([gate_p.reshape(t_p, LANES), gate_s.reshape(rows_s, LANES)], axis=0)
    slot_of, visits = _moe_plan(idx_all)
    slot_tiles = slot_of.reshape((t_p + rows_s) // TOK_TILE, 1, TOK_TILE * TOP_K)
    xs = _moe_scatter(h2_p.reshape(t_p, D_MODEL), h2_s.reshape(rows_s, D_MODEL), slot_tiles)
    ys = _moe_experts(xs, visits, w1[l], b1[l], w2[l], b2[l])
    y_p, y_s = _moe_combine(ys, slot_tiles, gate_all, x1_p.reshape(t_p, D_MODEL), x1_s.reshape(rows_s, D_MODEL),
                            gate2_p, gate2_s.reshape(rows_s, D_MODEL), g_post_ffn[l], seq)
    y_prompt = y_p.reshape(nb_p, seq, D_MODEL)
    y_sample = y_s.reshape(nb_s, n_tok, D_MODEL)

    st = lambda t: t[None]
    conv_s = jnp.concatenate([state_conv[l].astype(F32), per_b(u_s)], axis=1)[:, -(CONV_W - 1):]
    return (y_prompt, y_sample, st(k_p), st(v_p), st(conv_p), st(c_p), st(n_p), st(m_p[:, :, 0]),
            st(k_new), st(v_new), st(conv_s), st(c_s), st(n_s), st(m_s[:, :, 0]))
```

```python
import functools
import math

import jax
import jax.numpy as jnp
import numpy as np
from jax import lax
from jax.experimental import pallas as pl
from jax.experimental.pallas import tpu as pltpu

F32 = jnp.float32
BF16 = jnp.bfloat16
U32 = jnp.uint32
NEG_INF = float("-inf")

D_MODEL = 1024
H_M = 4
D_M = 512
DK_M = 128
CONV_W = 4
H_A = 8
D_A = 512
HD_A = 64
BRANCHES = ((128, 1), (512, 4), (2048, 16))
WINDOW_MAX = 2048
QB = 128
N_BUCKETS = 32
MAX_DISTANCE = 2048
N_EXPERTS = 32
TOP_K = 4
D_FF = 1024
SWIGLU_LIMIT = 7.0
SWIGLU_ALPHA = 1.702
EPS = 1e-6

LANES = 128
C_U, C_V, C_O, C_QA, C_KA, C_VA, C_G = 0, 1024, 1536, 2048, 2560, 3072, 3584
D_IN_R = C_G + LANES

MLSTM_L_PROMPT = 256
MLSTM_L_SAMPLE = 32
ATT_QL = 512
MOE_M = 512
MOE_SUB = 256
TOK_TILE = 128
PACK_W = D_MODEL // 2
PACK_T = PACK_W // LANES
ROW_T = D_MODEL // LANES
HI_MASK = 0xFFFF0000
VMEM_LIMIT = 56 * 1024 * 1024


def _params(sem, vmem=None):
    return pltpu.CompilerParams(dimension_semantics=sem, vmem_limit_bytes=vmem or VMEM_LIMIT)


def _rms(x, g):
    return x * lax.rsqrt(jnp.mean(x * x, axis=-1, keepdims=True) + EPS) * g


def _log_sigmoid(t):
    return jnp.minimum(t, 0.0) - jnp.log1p(jnp.exp(-jnp.abs(t)))


def _ada_kernel(c_ref, w_ref, b_ref, o_ref):
    c = c_ref[...]
    a = (c * jax.nn.sigmoid(c)).astype(BF16)
    o_ref[...] = jnp.dot(a, w_ref[...].astype(BF16), preferred_element_type=F32) + b_ref[...]


def _adaln(c_all, w_ada, b_ada):
    rows = c_all.shape[0]
    n = w_ada.shape[1]
    tn = 1024
    return pl.pallas_call(
        _ada_kernel,
        grid=(n // tn,),
        in_specs=[
            pl.BlockSpec((rows, D_MODEL), lambda j: (0, 0)),
            pl.BlockSpec((D_MODEL, tn), lambda j: (0, j)),
            pl.BlockSpec((1, tn), lambda j: (0, j)),
        ],
        out_specs=pl.BlockSpec((rows, tn), lambda j: (0, j)),
        out_shape=jax.ShapeDtypeStruct((rows, n), F32),
        compiler_params=_params(("arbitrary",)),
        name="adaln",
    )(c_all, w_ada, b_ada.reshape(1, n))


def _to_dilated(x, dil):
    return pltpu.einshape("(lr)c->l(rc)", x, r=dil)


def _from_dilated(x, dil):
    return pltpu.einshape("l(rc)->(lr)c", x, r=dil)


def _inproj_kernel(x_ref, g_ref, sc_ref, sh_ref, w_ref, bg_ref,
                   u_ref, v_ref, o_ref, gt_ref, qa_ref, ka_ref, va_ref, *dil_refs, dilations):
    x = x_ref[0]
    h = _rms(x, g_ref[...]) * (1.0 + sc_ref[0]) + sh_ref[0]
    hb = h.astype(BF16)

    def seg(lo, width):
        return jnp.dot(hb, w_ref[:, lo:lo + width], preferred_element_type=F32)

    u_ref[0] = seg(C_U, 2 * D_M)
    v_ref[0] = seg(C_V, D_M)
    o_ref[0] = jax.nn.sigmoid(seg(C_O, D_M))
    qkv = (seg(C_QA, D_A) * (HD_A ** -0.5), seg(C_KA, D_A), seg(C_VA, D_A))
    for ref, val in zip((qa_ref, ka_ref, va_ref), qkv):
        ref[0] = val
    for n, dil in enumerate(dilations):
        for j, val in enumerate(qkv):
            dil_refs[3 * n + j][0] = _to_dilated(val, dil)
    t = seg(C_G, LANES) + bg_ref[...]
    lane = lax.broadcasted_iota(jnp.int32, t.shape, 1)
    gt_ref[0] = jnp.where(lane < H_M, t, _log_sigmoid(t))


def _inproj(x3, g_pre, scale, shift, w_r, bias_g, tm, dilations=()):
    nb, rows, _ = x3.shape
    per_row_mod = scale.shape[1] != 1
    mod_block = (1, tm, D_MODEL) if per_row_mod else (1, 1, D_MODEL)
    mod_map = (lambda b, i: (b, i, 0)) if per_row_mod else (lambda b, i: (b, 0, 0))
    row_map = lambda b, i: (b, i, 0)
    widths = (2 * D_M, D_M, D_M, LANES, D_A, D_A, D_A)
    dil_specs = [pl.BlockSpec((1, tm // d, d * D_A), row_map) for d in dilations for _ in range(3)]
    dil_shapes = [jax.ShapeDtypeStruct((nb, rows // d, d * D_A), F32) for d in dilations for _ in range(3)]
    return pl.pallas_call(
        functools.partial(_inproj_kernel, dilations=tuple(dilations)),
        grid=(nb, rows // tm),
        in_specs=[
            pl.BlockSpec((1, tm, D_MODEL), row_map),
            pl.BlockSpec((1, D_MODEL), lambda b, i: (0, 0)),
            pl.BlockSpec(mod_block, mod_map),
            pl.BlockSpec(mod_block, mod_map),
            pl.BlockSpec((D_MODEL, D_IN_R), lambda b, i: (0, 0)),
            pl.BlockSpec((1, LANES), lambda b, i: (0, 0)),
        ],
        out_specs=[pl.BlockSpec((1, tm, w), row_map) for w in widths] + dil_specs,
        out_shape=[jax.ShapeDtypeStruct((nb, rows, w), F32) for w in widths] + dil_shapes,
        compiler_params=_params(("arbitrary", "arbitrary")),
        name="inproj",
    )(x3, g_pre.reshape(1, D_MODEL), scale, shift, w_r, bias_g)


def _mlstm_kernel(u_ref, v_ref, og_ref, gc_ref, gr_ref, tail0_ref, c0_ref, n0_ref, m0_ref,
                  cw_ref, cb_ref, mh_ref,
                  mem_ref, c_out_ref, n_out_ref, m_out_ref,
                  c_s, n_s, m_s, tail_s, uext_s, *, chunk):
    L = chunk
    step = pl.program_id(1)

    @pl.when(step == 0)
    def _():
        c_s[...] = c0_ref[0]
        n_s[...] = n0_ref[0]
        m_s[...] = m0_ref[0]
        tail_s[...] = tail0_ref[0]

    u = u_ref[0]
    uext_s[0:8, :] = tail_s[...]
    uext_s[8:8 + L, :] = u
    acc = cb_ref[...] + uext_s[pl.ds(5, L), :] * cw_ref[0:1, :]
    acc = acc + uext_s[pl.ds(6, L), :] * cw_ref[1:2, :]
    acc = acc + uext_s[pl.ds(7, L), :] * cw_ref[2:3, :]
    acc = acc + u * cw_ref[3:4, :]
    tail_s[...] = uext_s[pl.ds(L, 8), :]
    qk = acc * jax.nn.sigmoid(acc)

    gc = gc_ref[0]
    gr = gr_ref[0]
    row = lax.broadcasted_iota(jnp.int32, (L, L), 0)
    col = lax.broadcasted_iota(jnp.int32, (L, L), 1)
    causal = col <= row
    tri = causal.astype(F32)
    tri_t = (row <= col).astype(F32)
    lane = lax.broadcasted_iota(jnp.int32, gc.shape, 1)
    srow = lax.broadcasted_iota(jnp.int32, gr.shape, 0)
    b_col_all = jnp.dot(tri, jnp.where(lane >= H_M, gc, 0.0), preferred_element_type=F32,
                        precision=lax.Precision.HIGHEST)
    b_row_all = jnp.dot(jnp.where(srow >= H_M, gr, 0.0), tri_t, preferred_element_type=F32,
                        precision=lax.Precision.HIGHEST)

    v_all = v_ref[0]
    og = og_ref[0]
    for h in range(H_M):
        q = qk[:, h * DK_M:(h + 1) * DK_M]
        k = qk[:, D_M + h * DK_M:D_M + (h + 1) * DK_M] * (DK_M ** -0.5)
        v = v_all[:, h * DK_M:(h + 1) * DK_M]
        li_row = gr[h:h + 1, :]
        li_col = gc[:, h:h + 1]
        b_row = b_row_all[H_M + h:H_M + h + 1, :]
        b_col = b_col_all[:, H_M + h:H_M + h + 1]
        m0 = m_s[h:h + 1, 0:1]
        c0 = c_s[h]
        n0 = n_s[h:h + 1, :]

        d = jnp.where(causal, b_col - b_row + li_row, NEG_INF)
        g_state = b_col + m0
        m = jnp.maximum(jnp.max(d, axis=-1, keepdims=True), g_state)
        w_state = jnp.exp(g_state - m)
        qb = q.astype(BF16)
        kb = k.astype(BF16)
        vb = v.astype(BF16)
        s = lax.dot_general(qb, kb, (((1,), (1,)), ((), ())), preferred_element_type=F32)
        s = s * jnp.exp(d - m)
        num = jnp.dot(s.astype(BF16), vb, preferred_element_type=F32)
        num = num + w_state * jnp.dot(qb, c0.astype(BF16), preferred_element_type=F32)
        den = jnp.sum(s, axis=-1, keepdims=True) + w_state * jnp.sum(q * n0, axis=-1, keepdims=True)
        hh = num / jnp.maximum(jnp.abs(den), jnp.exp(-m))

        b_last = b_col[L - 1:L, :]
        g_tok_row = b_last - b_row + li_row
        m_new = jnp.maximum(b_last + m0, jnp.max(g_tok_row, axis=-1, keepdims=True))
        w_tok_col = jnp.exp(b_last - b_col + li_col - m_new)
        decay = jnp.exp(b_last + m0 - m_new)
        kw = k * w_tok_col
        c_s[h] = decay * c0 + jnp.dot(kw.T.astype(BF16), vb, preferred_element_type=F32)
        n_s[h:h + 1, :] = decay * n0 + jnp.sum(kw, axis=0, keepdims=True)
        m_s[h:h + 1, :] = jnp.broadcast_to(m_new, (1, LANES))

        hn = hh * lax.rsqrt(jnp.mean(hh * hh, axis=-1, keepdims=True) + EPS)
        hn = hn * mh_ref[:, h * DK_M:(h + 1) * DK_M]
        mem_ref[0, :, h * DK_M:(h + 1) * DK_M] = og[:, h * DK_M:(h + 1) * DK_M] * hn

    @pl.when(step == pl.num_programs(1) - 1)
    def _():
        c_out_ref[0] = c_s[...]
        n_out_ref[0] = n_s[...]
        m_out_ref[0] = m_s[...]


def _mlstm(u, v, og, gates_col, gates_row, tail0, c0, n0, m0x, conv_w, conv_b, mh_norm, chunk):
    nb, rows, _ = u.shape
    nc = rows // chunk
    row_map = lambda b, c: (b, c, 0)
    bat3 = lambda b, c: (b, 0, 0)
    const2 = lambda b, c: (0, 0)
    return pl.pallas_call(
        functools.partial(_mlstm_kernel, chunk=chunk),
        grid=(nb, nc),
        in_specs=[
            pl.BlockSpec((1, chunk, 2 * D_M), row_map),
            pl.BlockSpec((1, chunk, D_M), row_map),
            pl.BlockSpec((1, chunk, D_M), row_map),
            pl.BlockSpec((1, chunk, LANES), row_map),
            pl.BlockSpec((1, 8, chunk), lambda b, c: (b, 0, c)),
            pl.BlockSpec((1, 8, 2 * D_M), bat3),
            pl.BlockSpec((1, H_M, DK_M, DK_M), lambda b, c: (b, 0, 0, 0)),
            pl.BlockSpec((1, H_M, DK_M), bat3),
            pl.BlockSpec((1, H_M, LANES), bat3),
            pl.BlockSpec((CONV_W, 2 * D_M), const2),
            pl.BlockSpec((1, 2 * D_M), const2),
            pl.BlockSpec((1, D_M), const2),
        ],
        out_specs=[
            pl.BlockSpec((1, chunk, D_M), row_map),
            pl.BlockSpec((1, H_M, DK_M, DK_M), lambda b, c: (b, 0, 0, 0)),
            pl.BlockSpec((1, H_M, DK_M), bat3),
            pl.BlockSpec((1, H_M, LANES), bat3),
        ],
        out_shape=[
            jax.ShapeDtypeStruct((nb, rows, D_M), F32),
            jax.ShapeDtypeStruct((nb, H_M, DK_M, DK_M), F32),
            jax.ShapeDtypeStruct((nb, H_M, DK_M), F32),
            jax.ShapeDtypeStruct((nb, H_M, LANES), F32),
        ],
        scratch_shapes=[
            pltpu.VMEM((H_M, DK_M, DK_M), F32),
            pltpu.VMEM((H_M, DK_M), F32),
            pltpu.VMEM((H_M, LANES), F32),
            pltpu.VMEM((8, 2 * D_M), F32),
            pltpu.VMEM((chunk + 8, 2 * D_M), F32),
        ],
        compiler_params=_params(("arbitrary", "arbitrary")),
        name="mlstm",
    )(u, v, og, gates_col, gates_row, tail0, c0, n0, m0x,
      conv_w, conv_b.reshape(1, 2 * D_M), mh_norm.reshape(1, D_M))


def _attn_kernel(q_ref, kc_ref, vc_ref, kp_ref, vp_ref, bias_ref, ones_ref, o_ref, l_ref, k_s, v_s, *, ql):
    first = pl.program_id(2) == 0
    k_s[0:QB, :] = kp_ref[0].astype(BF16)
    k_s[QB:QB + ql, :] = kc_ref[0].astype(BF16)
    v_s[0:QB, :] = vp_ref[0].astype(BF16)
    v_s[QB:QB + ql, :] = vc_ref[0].astype(BF16)
    col = lax.broadcasted_iota(jnp.int32, (QB, 2 * QB), 1)
    pad_keys = jnp.logical_and(first, col < QB)
    head0 = lax.broadcasted_iota(jnp.int32, (1, LANES), 1) < HD_A
    nt = (((1,), (1,)), ((), ()))
    for c in range(D_A // LANES):
        sl = slice(c * LANES, (c + 1) * LANES)
        v_all = v_s[:, sl]
        v_lo = jnp.where(head0, v_all, jnp.zeros_like(v_all))
        v_hi = jnp.where(head0, jnp.zeros_like(v_all), v_all)
        for j in range(ql // QB):
            rows = slice(j * QB, (j + 1) * QB)
            keys = slice(j * QB, (j + 2) * QB)
            q = q_ref[0, rows, sl].astype(BF16)
            kb = k_s[keys, sl]
            ps, ms = [], []
            for half in range(2):
                qh = jnp.where(head0, q, jnp.zeros_like(q)) if half == 0 else jnp.where(head0, jnp.zeros_like(q), q)
                logits = lax.dot_general(qh, kb, nt, preferred_element_type=F32) + bias_ref[2 * c + half]
                if j == 0:
                    logits = jnp.where(pad_keys, NEG_INF, logits)
                m = jnp.max(logits, axis=-1, keepdims=True)
                ps.append(jnp.exp(logits - m).astype(BF16))
                ms.append(m)
            p_cat = jnp.concatenate(ps, axis=1)
            v_bd = jnp.concatenate([v_lo[keys], v_hi[keys]], axis=0)
            o = jnp.dot(p_cat, v_bd, preferred_element_type=F32)
            s = jnp.dot(p_cat, ones_ref[...], preferred_element_type=F32)
            o_ref[0, rows, sl] = o / s
            l_ref[0, rows, sl] = jnp.where(head0, ms[0], ms[1]) + jnp.log(s)


def _attn_branch(qa, ka, va, bias, dil):
    nb, ld, _ = qa.shape
    ql = min(ATT_QL, ld)
    cur = lambda b, r, i: (b, i, r)
    prev = lambda b, r, i: (b, jnp.maximum(i * (ql // QB) - 1, 0), r)
    ones_bd = np.zeros((4 * QB, LANES), np.float32)
    ones_bd[:2 * QB, :HD_A] = 1.0
    ones_bd[2 * QB:, HD_A:] = 1.0
    return pl.pallas_call(
        functools.partial(_attn_kernel, ql=ql),
        grid=(nb, dil, ld // ql),
        in_specs=[
            pl.BlockSpec((1, ql, D_A), cur),
            pl.BlockSpec((1, ql, D_A), cur),
            pl.BlockSpec((1, ql, D_A), cur),
            pl.BlockSpec((1, QB, D_A), prev),
            pl.BlockSpec((1, QB, D_A), prev),
            pl.BlockSpec((H_A, QB, 2 * QB), lambda b, r, i: (0, 0, 0)),
            pl.BlockSpec((4 * QB, LANES), lambda b, r, i: (0, 0)),
        ],
        out_specs=[pl.BlockSpec((1, ql, D_A), cur), pl.BlockSpec((1, ql, D_A), cur)],
        out_shape=[jax.ShapeDtypeStruct((nb, ld, dil * D_A), F32)] * 2,
        scratch_shapes=[pltpu.VMEM((QB + ql, D_A), BF16), pltpu.VMEM((QB + ql, D_A), BF16)],
        compiler_params=_params(("arbitrary", "arbitrary", "arbitrary")),
        name=f"attn_d{dil}",
    )(qa, ka, va, ka, va, bias, jnp.asarray(ones_bd, BF16))


def _heads_kernel(k_ref, v_ref, ko_ref, vo_ref):
    for src, dst in ((k_ref, ko_ref), (v_ref, vo_ref)):
        for c in range(D_A // LANES):
            xt = src[0, :, c * LANES:(c + 1) * LANES].T
            dst[0, 2 * c] = xt[:HD_A]
            dst[0, 2 * c + 1] = xt[HD_A:]


def _window_heads(ka, va, keep, tm=512):
    nb, seq, _ = ka.shape
    first = (seq - keep) // tm
    in_spec = pl.BlockSpec((1, tm, D_A), lambda b, i: (b, first + i, 0))
    out_spec = pl.BlockSpec((1, H_A, HD_A, tm), lambda b, i: (b, 0, 0, i))
    kt, vt = pl.pallas_call(
        _heads_kernel,
        grid=(nb, keep // tm),
        in_specs=[in_spec, in_spec],
        out_specs=[out_spec, out_spec],
        out_shape=[jax.ShapeDtypeStruct((nb, H_A, HD_A, keep), F32)] * 2,
        compiler_params=_params(("arbitrary", "arbitrary")),
        name="window_heads",
    )(ka, va)
    return kt.transpose(0, 3, 1, 2), vt.transpose(0, 3, 1, 2)


def _t5_bucket(dist):
    max_exact = N_BUCKETS // 2
    n = jnp.maximum(dist, 1).astype(F32)
    large = max_exact + (jnp.log(n / max_exact) / math.log(MAX_DISTANCE / max_exact)
                         * (N_BUCKETS - max_exact)).astype(jnp.int32)
    large = jnp.minimum(large, N_BUCKETS - 1)
    return jnp.where(dist < max_exact, dist, large)


def _branch_bias_vec(rel_bias, window, dil):
    return rel_bias[_t5_bucket(dil * jnp.arange(window // dil + 1))].astype(F32)


def _prompt_bias(rel_bias, window, dil):
    assert window // dil == QB
    n = 2 * QB
    vec = _branch_bias_vec(rel_bias, window, dil)
    vecp = jnp.concatenate([vec[::-1], jnp.full((n - QB - 1, H_A), NEG_INF, F32)], axis=0)
    g = jnp.roll(vecp, QB - 1, axis=0)
    skew = jnp.tile(g, (QB + 1, 1))[:QB * (n + 1)].reshape(QB, n + 1, H_A)[:, :n]
    return skew[::-1].transpose(2, 0, 1)


TOK_PAD = 8


def _sattn_kernel(q_ref, kt_ref, vt_ref, knt_ref, vnt_ref, bc_ref, bn_ref, att_ref):
    nt = (((1,), (1,)), ((), ()))
    for h in range(H_A):
        q = q_ref[0, h].astype(BF16)
        lc = jnp.dot(q, kt_ref[0, h].astype(BF16), preferred_element_type=F32)
        ln = jnp.dot(q, knt_ref[0, h].astype(BF16), preferred_element_type=F32)
        lcs = [lc + bc_ref[d, h] for d in range(len(BRANCHES))]
        lns = [ln + bn_ref[d, h] for d in range(len(BRANCHES))]
        m = functools.reduce(jnp.maximum, [jnp.max(x, axis=-1, keepdims=True) for x in lcs + lns])
        pc = functools.reduce(jnp.add, [jnp.exp(x - m) for x in lcs])
        pn = functools.reduce(jnp.add, [jnp.exp(x - m) for x in lns])
        s = jnp.sum(pc, axis=-1, keepdims=True) + jnp.sum(pn, axis=-1, keepdims=True)
        o = lax.dot_general(pc.astype(BF16), vt_ref[0, h].astype(BF16), nt, preferred_element_type=F32)
        o = o + lax.dot_general(pn.astype(BF16), vnt_ref[0, h].astype(BF16), nt, preferred_element_type=F32)
        att_ref[0, h] = o / s


def _sample_attention(q_h, kt, vt, knt, vnt, bias_c, bias_n):
    nb, _, _, n_past = kt.shape
    bat = lambda b: (b, 0, 0, 0)
    const = lambda b: (0, 0, 0, 0)
    return pl.pallas_call(
        _sattn_kernel,
        grid=(nb,),
        in_specs=[
            pl.BlockSpec((1, H_A, TOK_PAD, HD_A), bat),
            pl.BlockSpec((1, H_A, HD_A, n_past), bat),
            pl.BlockSpec((1, H_A, HD_A, n_past), bat),
            pl.BlockSpec((1, H_A, HD_A, LANES), bat),
            pl.BlockSpec((1, H_A, HD_A, LANES), bat),
            pl.BlockSpec(bias_c.shape, const),
            pl.BlockSpec(bias_n.shape, const),
        ],
        out_specs=pl.BlockSpec((1, H_A, TOK_PAD, HD_A), bat),
        out_shape=jax.ShapeDtypeStruct((nb, H_A, TOK_PAD, HD_A), F32),
        compiler_params=_params(("arbitrary",)),
        name="sample_attn",
    )(q_h, kt, vt, knt, vnt, bias_c, bias_n)


def _sample_bias(rel_bias, n_tok, n_past):
    ninf = lambda n: jnp.full((n, H_A), NEG_INF, F32)
    cached, new = [], []
    for window, dil in BRANCHES:
        nk = window // dil
        rows = n_past // dil
        assert n_past % dil == 0 and rows >= nk
        vec = _branch_bias_vec(rel_bias, window, dil)
        res = jnp.arange(dil)[None, :, None]
        tc, tn = [], []
        for t in range(n_tok):
            near = t // dil + 1
            col = jnp.concatenate([ninf(rows - (nk + 1 - near)), vec[near:nk + 1][::-1]], axis=0)
            tab = jnp.where(res == t % dil, col[:, None, :], NEG_INF).reshape(n_past, H_A)
            tc.append(tab)
            keys = [vec[(t - j) // dil][None] if (t - j) % dil == 0 and 0 <= t - j <= nk * dil else ninf(1)
                    for j in range(n_tok)]
            tn.append(jnp.concatenate(keys + [ninf(LANES - n_tok)], axis=0))
        pad = lambda tabs: jnp.stack(tabs + [jnp.zeros_like(tabs[0])] * (TOK_PAD - n_tok))
        cached.append(pad(tc).transpose(2, 0, 1))
        new.append(pad(tn).transpose(2, 0, 1))
    return jnp.stack(cached), jnp.stack(new)


def _postmix_kernel(mem_ref, o1_ref, l1_ref, o2_ref, l2_ref, o3_ref, l3_ref, x_ref,
                    g1_ref, sc2_ref, sh2_ref, gpost_ref, gpre_ref, wo_ref, wr_ref, br_ref, cnt0_ref,
                    x1_ref, h2_ref, idx_ref, gate_ref, cnt_ref, cnt_s, *, dilations):
    first_step = jnp.logical_and(pl.program_id(0) == 0, pl.program_id(1) == 0)

    @pl.when(first_step)
    def _():
        cnt_s[...] = cnt0_ref[...]

    rows = lambda ref, dil: ref[0] if dil == 1 else _from_dilated(ref[0], dil)
    o1, o2, o3 = (rows(r, d) for r, d in zip((o1_ref, o2_ref, o3_ref), dilations))
    l1, l2, l3 = (rows(r, d) for r, d in zip((l1_ref, l2_ref, l3_ref), dilations))
    lmax = jnp.maximum(jnp.maximum(l1, l2), l3)
    w1 = jnp.exp(l1 - lmax)
    w2 = jnp.exp(l2 - lmax)
    w3 = jnp.exp(l3 - lmax)
    att = (w1 * o1 + w2 * o2 + w3 * o3) / (w1 + w2 + w3)
    y = jnp.dot(mem_ref[0].astype(BF16), wo_ref[0:D_M, :], preferred_element_type=F32)
    y = y + jnp.dot(att.astype(BF16), wo_ref[D_M:, :], preferred_element_type=F32)
    x1 = x_ref[0] + g1_ref[0] * _rms(y, gpost_ref[...])
    x1_ref[0] = x1
    h2 = _rms(x1, gpre_ref[...]) * (1.0 + sc2_ref[0]) + sh2_ref[0]
    h2_ref[0] = h2
    h_hi = h2.astype(BF16)
    h_lo = (h2 - h_hi.astype(F32)).astype(BF16)
    w_hi = wr_ref[0]
    logits = jnp.dot(h_hi, w_hi, preferred_element_type=F32)
    logits = logits + jnp.dot(h_lo, w_hi, preferred_element_type=F32)
    logits = logits + jnp.dot(h_hi, wr_ref[1], preferred_element_type=F32) + br_ref[...]
    lane = lax.broadcasted_iota(jnp.int32, logits.shape, 1)
    lane_f = lane.astype(F32)
    logits = jnp.where(lane < N_EXPERTS, logits, NEG_INF)
    vals, idxs = [], []
    for _ in range(TOP_K):
        mx = jnp.max(logits, axis=-1, keepdims=True)
        ix = jnp.min(jnp.where(logits == mx, lane_f, float(LANES)), axis=-1, keepdims=True)
        vals.append(mx)
        idxs.append(ix)
        logits = jnp.where(lane_f == ix, NEG_INF, logits)
    es = [jnp.exp(v - vals[0]) for v in vals]
    tot = es[0] + es[1] + es[2] + es[3]
    tm = logits.shape[0]
    picks = [(lane_f == ix).astype(F32) for ix in idxs]
    chosen = picks[0] + picks[1] + picks[2] + picks[3]
    r_i = lax.broadcasted_iota(jnp.int32, (tm, tm), 0)
    c_i = lax.broadcasted_iota(jnp.int32, (tm, tm), 1)
    earlier = jnp.dot((c_i < r_i).astype(BF16), chosen.astype(BF16), preferred_element_type=F32)
    before = earlier + cnt_s[...]
    cnt_s[...] = cnt_s[...] + jnp.sum(chosen, axis=0, keepdims=True)
    cnt_ref[...] = cnt_s[...]
    idx_tile = jnp.zeros(logits.shape, F32)
    gate_tile = jnp.zeros(logits.shape, F32)
    for k in range(TOP_K):
        rank = jnp.sum(picks[k] * before, axis=-1, keepdims=True)
        idx_tile = jnp.where(lane == k, idxs[k], idx_tile)
        idx_tile = jnp.where(lane == TOP_K + k, rank, idx_tile)
        gate_tile = jnp.where(lane == k, es[k] / tot, gate_tile)
    idx_ref[0] = idx_tile.astype(jnp.int32)
    gate_ref[0] = gate_tile


def _postmix(mem, branches, x3, gate1, scale2, shift2, g_post, g_pre, w_out_b, w_router_p, b_router_p, counts0, tm,
             dilations=(1, 1, 1)):
    nb, rows, _ = x3.shape
    per_row_mod = gate1.shape[1] != 1
    mod_block = (1, tm, D_MODEL) if per_row_mod else (1, 1, D_MODEL)
    mod_map = (lambda b, i: (b, i, 0)) if per_row_mod else (lambda b, i: (b, 0, 0))
    row_map = lambda b, i: (b, i, 0)
    const2 = lambda b, i: (0, 0)
    half = pl.BlockSpec((1, tm, D_M), row_map)
    flat = [a for pair in branches for a in pair]
    branch_specs = [pl.BlockSpec((1, tm // d, d * D_A), row_map) for d in dilations for _ in range(2)]
    return pl.pallas_call(
        functools.partial(_postmix_kernel, dilations=tuple(dilations)),
        grid=(nb, rows // tm),
        in_specs=[half] + branch_specs + [
            pl.BlockSpec((1, tm, D_MODEL), row_map),
            pl.BlockSpec(mod_block, mod_map),
            pl.BlockSpec(mod_block, mod_map),
            pl.BlockSpec(mod_block, mod_map),
            pl.BlockSpec((1, D_MODEL), const2),
            pl.BlockSpec((1, D_MODEL), const2),
            pl.BlockSpec((D_MODEL, D_MODEL), const2),
            pl.BlockSpec((2, D_MODEL, LANES), lambda b, i: (0, 0, 0)),
            pl.BlockSpec((1, LANES), const2),
            pl.BlockSpec((1, LANES), const2),
        ],
        out_specs=[
            pl.BlockSpec((1, tm, D_MODEL), row_map),
            pl.BlockSpec((1, tm, D_MODEL), row_map),
            pl.BlockSpec((1, tm, LANES), row_map),
            pl.BlockSpec((1, tm, LANES), row_map),
            pl.BlockSpec((1, LANES), const2),
        ],
        out_shape=[
            jax.ShapeDtypeStruct((nb, rows, D_MODEL), F32),
            jax.ShapeDtypeStruct((nb, rows, D_MODEL), F32),
            jax.ShapeDtypeStruct((nb, rows, LANES), jnp.int32),
            jax.ShapeDtypeStruct((nb, rows, LANES), F32),
            jax.ShapeDtypeStruct((1, LANES), F32),
        ],
        scratch_shapes=[pltpu.VMEM((1, LANES), F32)],
        compiler_params=_params(("arbitrary", "arbitrary")),
        name="postmix",
    )(mem, *flat, x3, gate1, scale2, shift2, g_post.reshape(1, D_MODEL), g_pre.reshape(1, D_MODEL),
      w_out_b, w_router_p, b_router_p, counts0)


def _row_wait(src_rows, dst_rows, sem):
    pltpu.make_async_copy(src_rows, dst_rows, sem).wait()


def _scatter_kernel(slot_ref, hp_ref, hs_ref, xs_hbm, buf, sem, *, n_prompt_tiles):
    h = jnp.where(pl.program_id(0) >= n_prompt_tiles, hs_ref[...], hp_ref[...])
    bits = pltpu.bitcast(h.astype(BF16).astype(F32), U32)
    packed = (bits[:, :PACK_W] & U32(HI_MASK)) | (bits[:, PACK_W:] >> 16)
    buf[...] = pltpu.einshape("m(jc)->mjc", packed, j=PACK_T)

    def issue(r, carry):
        src = buf.at[pl.ds(r, 1)]
        for k in range(TOP_K):
            pltpu.make_async_copy(src, xs_hbm.at[pl.ds(slot_ref[0, 0, r * TOP_K + k], 1)],
                                  sem).start(priority=k % 2)
        return carry

    lax.fori_loop(0, TOK_TILE, issue, 0, unroll=4)
    for _ in range(TOP_K):
        _row_wait(buf, xs_hbm.at[pl.ds(0, TOK_TILE)], sem)


def _moe_scatter(h_p, h_s, slot_tiles):
    tiles = slot_tiles.shape[0]
    n_p = h_p.shape[0] // TOK_TILE
    assert tiles == n_p + 1 and h_s.shape[0] == TOK_TILE
    n_rows = tiles * TOK_TILE * TOP_K
    return pl.pallas_call(
        functools.partial(_scatter_kernel, n_prompt_tiles=n_p),
        grid=(tiles,),
        in_specs=[
            pl.BlockSpec((1, 1, TOK_TILE * TOP_K), lambda i: (i, 0, 0), memory_space=pltpu.SMEM),
            pl.BlockSpec((TOK_TILE, D_MODEL), lambda i: (jnp.minimum(i, n_p - 1), 0)),
            pl.BlockSpec((TOK_TILE, D_MODEL), lambda i: (0, 0)),
        ],
        out_specs=pl.BlockSpec(memory_space=pl.ANY),
        out_shape=jax.ShapeDtypeStruct((n_rows, PACK_T, LANES), U32),
        scratch_shapes=[pltpu.VMEM((TOK_TILE, PACK_T, LANES), U32), pltpu.SemaphoreType.DMA],
        compiler_params=_params(("arbitrary",)),
        name="moe_scatter",
    )(slot_tiles, h_p, h_s)


def _expert_kernel(tile_ref, exp_ref, lo_ref, hi_ref, first_ref, nvis_ref,
                   xs_ref, w1_ref, b1_ref, w2_ref, b2_ref, ys_ref, w1b, w2b):
    v = pl.program_id(0)
    active = v < nvis_ref[0]
    changed = jnp.logical_or(v == 0, exp_ref[v] != exp_ref[jnp.maximum(v - 1, 0)])
    lo = lo_ref[v]
    hi = hi_ref[v]

    @pl.when(jnp.logical_and(active, changed))
    def _():
        w1b[...] = w1_ref[0].astype(BF16)
        w2b[...] = w2_ref[0].astype(BF16)

    def mlp_rows(r0):
        w = pltpu.einshape("mjc->m(jc)", xs_ref[r0:r0 + MOE_SUB])
        x_hi = pltpu.bitcast(w & U32(HI_MASK), F32).astype(BF16)
        x_lo = pltpu.bitcast(w << 16, F32).astype(BF16)
        hgu = jnp.dot(x_hi, w1b[0:PACK_W, :], preferred_element_type=F32)
        hgu = hgu + jnp.dot(x_lo, w1b[PACK_W:, :], preferred_element_type=F32) + b1_ref[0]
        x_glu = jnp.minimum(hgu[:, :D_FF], SWIGLU_LIMIT)
        x_lin = jnp.clip(hgu[:, D_FF:], -SWIGLU_LIMIT, SWIGLU_LIMIT)
        act = x_glu * jax.nn.sigmoid(SWIGLU_ALPHA * x_glu) * (x_lin + 1.0)
        y = jnp.dot(act.astype(BF16), w2b[...], preferred_element_type=F32) + b2_ref[0]
        return pltpu.einshape("m(jc)->mjc", y, j=ROW_T)

    subs = [r * MOE_SUB for r in range(MOE_M // MOE_SUB)]
    whole = jnp.logical_and(lo == 0, hi == MOE_M)

    @pl.when(jnp.logical_and(active, whole))
    def _():
        for r0 in subs:
            ys_ref[r0:r0 + MOE_SUB] = mlp_rows(r0)

    shared = jnp.logical_and(active, jnp.logical_not(whole))

    @pl.when(jnp.logical_and(shared, first_ref[v] == 1))
    def _():
        ys_ref[...] = jnp.zeros(ys_ref.shape, F32)

    for r0 in subs:
        @pl.when(jnp.logical_and(shared, jnp.logical_and(lo < r0 + MOE_SUB, hi > r0)))
        def _():
            rows = r0 + lax.broadcasted_iota(jnp.int32, (MOE_SUB, 1, 1), 0)
            mine = jnp.logical_and(rows >= lo, rows < hi)
            ys_ref[r0:r0 + MOE_SUB] = jnp.where(mine, mlp_rows(r0), ys_ref[r0:r0 + MOE_SUB])


def _moe_experts(xs, visits, w1, b1, w2, b2):
    n_rows = xs.shape[0]
    n_vis = visits[0].shape[0]
    tile_map = lambda v, tile, *_: (tile[v], 0, 0)
    exp_map = lambda v, tile, exp, *_: (exp[v], 0, 0)
    return pl.pallas_call(
        _expert_kernel,
        grid_spec=pltpu.PrefetchScalarGridSpec(
            num_scalar_prefetch=6,
            grid=(n_vis,),
            in_specs=[
                pl.BlockSpec((MOE_M, PACK_T, LANES), tile_map),
                pl.BlockSpec((1, D_MODEL, 2 * D_FF), exp_map),
                pl.BlockSpec((1, 1, 2 * D_FF), exp_map),
                pl.BlockSpec((1, D_FF, D_MODEL), exp_map),
                pl.BlockSpec((1, 1, D_MODEL), exp_map),
            ],
            out_specs=pl.BlockSpec((MOE_M, ROW_T, LANES), tile_map),
            scratch_shapes=[pltpu.VMEM((D_MODEL, 2 * D_FF), BF16), pltpu.VMEM((D_FF, D_MODEL), BF16)],
        ),
        out_shape=jax.ShapeDtypeStruct((n_rows, ROW_T, LANES), F32),
        compiler_params=_params(("arbitrary",)),
        name="moe_experts",
    )(*visits, xs, w1, b1.reshape(N_EXPERTS, 1, 2 * D_FF), w2, b2.reshape(N_EXPERTS, 1, D_MODEL))


def _combine_kernel(slot_ref, next_ref, gate_ref, x1p_ref, x1s_ref, g2p_ref, g2s_ref, gpost_ref, ys_hbm,
                    yp_ref, ysm_ref, buf, sem, *, n_prompt_tiles):
    i = pl.program_id(0)
    last = pl.num_programs(0) - 1
    cur = lax.rem(i, 2)

    def gather(slots, b):
        def issue(r, carry):
            for k in range(TOP_K):
                pltpu.make_async_copy(ys_hbm.at[pl.ds(slots[0, 0, r * TOP_K + k], 1)],
                                      buf.at[b, pl.ds(k * TOK_TILE + r, 1)], sem.at[b]).start(priority=k % 2)
            return carry

        lax.fori_loop(0, TOK_TILE, issue, 0, unroll=4)

    @pl.when(i == 0)
    def _():
        gather(slot_ref, cur)

    @pl.when(i < last)
    def _():
        gather(next_ref, 1 - cur)

    for k in range(TOP_K):
        _row_wait(ys_hbm.at[pl.ds(0, TOK_TILE)], buf.at[cur, pl.ds(k * TOK_TILE, TOK_TILE)], sem.at[cur])
    g = gate_ref[...]
    f = jnp.zeros((TOK_TILE, D_MODEL), F32)
    for k in range(TOP_K):
        rows = pltpu.einshape("mjc->m(jc)", buf[cur, pl.ds(k * TOK_TILE, TOK_TILE)])
        f = f + rows * g[:, k:k + 1]
    is_sample = i >= n_prompt_tiles
    x1 = jnp.where(is_sample, x1s_ref[...], x1p_ref[...])
    g2 = jnp.where(is_sample, g2s_ref[...], g2p_ref[0])
    y = x1 + g2 * _rms(f, gpost_ref[...])

    @pl.when(jnp.logical_not(is_sample))
    def _():
        yp_ref[...] = y

    @pl.when(is_sample)
    def _():
        ysm_ref[...] = y


def _moe_combine(ys, slot_tiles, gates, x1_p, x1_s, gate2_p, gate2_s, g_post, rows_per_batch):
    tiles = slot_tiles.shape[0]
    n_p = x1_p.shape[0] // TOK_TILE
    assert tiles == n_p + 1 and x1_s.shape[0] == TOK_TILE
    tiles_per_batch = rows_per_batch // TOK_TILE
    p_tile = lambda i: (jnp.minimum(i, n_p - 1), 0)
    const2 = lambda i: (0, 0)
    return pl.pallas_call(
        functools.partial(_combine_kernel, n_prompt_tiles=n_p),
        grid=(tiles,),
        in_specs=[
            pl.BlockSpec((1, 1, TOK_TILE * TOP_K), lambda i: (i, 0, 0), memory_space=pltpu.SMEM),
            pl.BlockSpec((1, 1, TOK_TILE * TOP_K), lambda i: (jnp.minimum(i + 1, tiles - 1), 0, 0),
                         memory_space=pltpu.SMEM),
            pl.BlockSpec((TOK_TILE, LANES), lambda i: (i, 0)),
            pl.BlockSpec((TOK_TILE, D_MODEL), p_tile),
            pl.BlockSpec((TOK_TILE, D_MODEL), const2),
            pl.BlockSpec((1, 1, D_MODEL), lambda i: (jnp.minimum(i, n_p - 1) // tiles_per_batch, 0, 0)),
            pl.BlockSpec((TOK_TILE, D_MODEL), const2),
            pl.BlockSpec((1, D_MODEL), const2),
            pl.BlockSpec(memory_space=pl.ANY),
        ],
        out_specs=[pl.BlockSpec((TOK_TILE, D_MODEL), p_tile), pl.BlockSpec((TOK_TILE, D_MODEL), const2)],
        out_shape=[jax.ShapeDtypeStruct(x1_p.shape, F32), jax.ShapeDtypeStruct(x1_s.shape, F32)],
        scratch_shapes=[pltpu.VMEM((2, TOP_K * TOK_TILE, ROW_T, LANES), F32), pltpu.SemaphoreType.DMA((2,))],
        compiler_params=_params(("arbitrary",)),
        name="moe_combine",
    )(slot_tiles, slot_tiles, gates, x1_p, x1_s, gate2_p, gate2_s, g_post.reshape(1, D_MODEL), ys)


def _moe_plan(route, counts):
    a = route.shape[0] * TOP_K
    assert a % MOE_M == 0
    flat_e = route[:, :TOP_K].reshape(a)
    rank = route[:, TOP_K:2 * TOP_K].reshape(a)
    g_end = jnp.cumsum(counts)
    g_start = g_end - counts
    onehot = flat_e[:, None] == jnp.arange(N_EXPERTS)[None, :]
    slot_of = (jnp.sum(jnp.where(onehot, g_start[None, :], 0), axis=1) + rank).astype(jnp.int32)

    n_tiles = a // MOE_M
    n_vis = n_tiles + N_EXPERTS - 1
    first_tile = g_start // MOE_M
    last_tile = jnp.where(counts > 0, (g_end - 1) // MOE_M, first_tile)
    nvis_e = jnp.where(counts > 0, last_tile - first_tile + 1, 0)
    v_end = jnp.cumsum(nvis_e)
    v_start = v_end - nvis_e
    total = v_end[-1]
    v = jnp.arange(n_vis)
    vc = jnp.minimum(v, total - 1)
    e_of = jnp.minimum(jnp.sum((vc[:, None] >= v_end[None, :]).astype(jnp.int32), axis=1), N_EXPERTS - 1)
    tile_of = first_tile[e_of] + (vc - v_start[e_of])
    lo = jnp.maximum(g_start[e_of], tile_of * MOE_M) - tile_of * MOE_M
    hi = jnp.minimum(g_end[e_of], (tile_of + 1) * MOE_M) - tile_of * MOE_M
    prev_tile = jnp.concatenate([jnp.full((1,), -1, tile_of.dtype), tile_of[:-1]])
    first = (tile_of != prev_tile).astype(jnp.int32)
    i32 = lambda t: t.astype(jnp.int32)
    visits = (i32(tile_of), i32(e_of), i32(lo), i32(hi), first, i32(total).reshape(1))
    return slot_of, visits


def _reorder_w_in(w_in):
    off_om_end = 4 * D_M
    off_g_end = off_om_end + 2 * H_M
    w = jnp.concatenate([
        w_in[:, :off_om_end],
        w_in[:, off_g_end:],
        w_in[:, off_om_end:off_g_end],
        jnp.zeros((D_MODEL, LANES - 2 * H_M), w_in.dtype),
    ], axis=1)
    return w.astype(BF16)


def kernel(x_prompt, x_sample, cache_k, cache_v, state_conv, state_C, state_n, state_m, c_prompt, c_sample, w_ada, b_ada, g_pre_mix, g_post_mix, g_pre_ffn, g_post_ffn, w_in, b_ig, b_fg, conv_w, conv_b, mh_norm, rel_bias, w_out, w_router, b_router, w1, b1, w2, b2):
    depth = w_in.shape[0]
    assert depth == 1
    l = 0
    nb_p, seq, _ = x_prompt.shape
    nb_s, n_tok, _ = x_sample.shape
    rows_s = nb_s * n_tok
    assert rows_s == TOK_TILE

    n_c = nb_p + nb_s
    c_pad = -(-n_c // 8) * 8
    c_all = jnp.concatenate([c_prompt, c_sample, jnp.zeros((c_pad - n_c, D_MODEL), F32)], axis=0)
    mod = _adaln(c_all, w_ada[l], b_ada[l])
    mods = [mod[:, i * D_MODEL:(i + 1) * D_MODEL] for i in range(6)]
    mods_p = [m[:nb_p].reshape(nb_p, 1, D_MODEL) for m in mods]
    mods_s = [jnp.repeat(m[nb_p:n_c], n_tok, axis=0).reshape(1, rows_s, D_MODEL) for m in mods]

    w_r = _reorder_w_in(w_in[l])
    bias_g = jnp.concatenate([b_ig[l], b_fg[l], jnp.zeros((LANES - 2 * H_M,), F32)]).reshape(1, LANES)
    w_out_b = w_out[l].astype(BF16)
    w_router_f = jnp.concatenate([w_router[l], jnp.zeros((D_MODEL, LANES - N_EXPERTS), F32)], axis=1)
    w_router_hi = w_router_f.astype(BF16)
    w_router_p = jnp.stack([w_router_hi, (w_router_f - w_router_hi.astype(F32)).astype(BF16)])
    b_router_p = jnp.concatenate([b_router[l].astype(F32), jnp.zeros((LANES - N_EXPERTS,), F32)]).reshape(1, LANES)

    shift1, scale1, gate1, shift2, scale2, gate2_p = mods_p
    dils = tuple(d for _, d in BRANCHES)
    assert dils[0] == 1
    u, v_m, og, gts, qa, ka, va, *dilated = _inproj(x_prompt, g_pre_mix[l], scale1, shift1, w_r, bias_g, tm=512,
                                                   dilations=dils[1:])
    qkv_d = [(qa, ka, va)] + [tuple(dilated[3 * n:3 * n + 3]) for n in range(len(dils) - 1)]
    gates_row = gts[:, :, :8].transpose(0, 2, 1)
    zeros = lambda *s: jnp.zeros(s, F32)
    mem, c_p, n_p, m_p = _mlstm(u, v_m, og, gts, gates_row, zeros(nb_p, 8, 2 * D_M),
                                zeros(nb_p, H_M, DK_M, DK_M), zeros(nb_p, H_M, DK_M), zeros(nb_p, H_M, LANES),
                                conv_w[l], conv_b[l], mh_norm[l], MLSTM_L_PROMPT)
    branches = [_attn_branch(*qkv, _prompt_bias(rel_bias, w, d), d) for qkv, (w, d) in zip(qkv_d, BRANCHES)]
    x1_p, h2_p, idx_p, gate_p, cnt_p = _postmix(mem, branches, x_prompt, gate1, scale2, shift2,
                                                g_post_mix[l], g_pre_ffn[l], w_out_b, w_router_p, b_router_p,
                                                zeros(1, LANES), tm=256, dilations=dils)
    keep = min(WINDOW_MAX, seq)
    k_p, v_p = _window_heads(ka, va, keep)
    conv_p = u[:, seq - (CONV_W - 1):]

    shift1, scale1, gate1, shift2, scale2, gate2_s = mods_s
    x_s = x_sample.reshape(1, rows_s, D_MODEL)
    u_s, v_s, og_s, gts_s, qa_s, ka_s, va_s = _inproj(x_s, g_pre_mix[l], scale1, shift1, w_r, bias_g, tm=rows_s)
    per_b = lambda t: t.reshape(nb_s, n_tok, t.shape[-1])
    ls = MLSTM_L_SAMPLE
    pad_rows = lambda t, n: jnp.concatenate([t, jnp.zeros((nb_s, n - t.shape[1], t.shape[2]), t.dtype)], axis=1)
    inert = jnp.concatenate([jnp.full((H_M,), NEG_INF, F32), jnp.zeros((LANES - H_M,), F32)])
    gts_pad = jnp.concatenate([per_b(gts_s), jnp.broadcast_to(inert, (nb_s, ls - n_tok, LANES))], axis=1)
    tail0 = jnp.concatenate([zeros(nb_s, 8 - (CONV_W - 1), 2 * D_M), state_conv[l].astype(F32)], axis=1)
    m0x = jnp.broadcast_to(state_m[l].astype(F32)[:, :, None], (nb_s, H_M, LANES))
    mem_s, c_s, n_s, m_s = _mlstm(pad_rows(per_b(u_s), ls), pad_rows(per_b(v_s), ls), pad_rows(per_b(og_s), ls),
                                  gts_pad, gts_pad[:, :, :8].transpose(0, 2, 1), tail0,
                                  state_C[l].astype(F32), state_n[l].astype(F32), m0x,
                                  conv_w[l], conv_b[l], mh_norm[l], ls)
    mem_s = mem_s[:, :n_tok].reshape(1, rows_s, D_M)
    heads = lambda t: t.reshape(nb_s, n_tok, H_A, HD_A)
    k_new, v_new = heads(ka_s), heads(va_s)
    n_past = cache_k.shape[2]
    assert n_tok <= TOK_PAD
    q_h = jnp.pad(heads(qa_s).transpose(0, 2, 1, 3), ((0, 0), (0, 0), (0, TOK_PAD - n_tok), (0, 0)))
    new_t = lambda t: jnp.pad(t.transpose(0, 2, 3, 1), ((0, 0), (0, 0), (0, 0), (0, LANES - n_tok)))
    cache_t = lambda c: c[l].astype(F32).transpose(0, 2, 3, 1)
    att_s = _sample_attention(q_h, cache_t(cache_k), cache_t(cache_v), new_t(k_new), new_t(v_new),
                              *_sample_bias(rel_bias, n_tok, n_past))
    att_s = att_s[:, :, :n_tok].transpose(0, 2, 1, 3).reshape(1, rows_s, D_A)
    zero_o = zeros(1, rows_s, D_A)
    ninf = jnp.full((1, rows_s, D_A), NEG_INF, F32)
    branches_s = [(att_s, zero_o), (zero_o, ninf), (zero_o, ninf)]
    x1_s, h2_s, idx_s, gate_s, cnt_all = _postmix(mem_s, branches_s, x_s, gate1, scale2, shift2,
                                                  g_post_mix[l], g_pre_ffn[l], w_out_b, w_router_p, b_router_p,
                                                  cnt_p, tm=rows_s)

    t_p = nb_p * seq
    idx_all = jnp.concatenate([idx_p.reshape(t_p, LANES), idx_s.reshape(rows_s, LANES)], axis=0)
    gate_all = jnp.concatenate([gate_p.reshape(t_p, LANES), gate_s.reshape(rows_s, LANES)], axis=0)
    slot_of, visits = _moe_plan(idx_all, cnt_all[0, :N_EXPERTS].astype(jnp.int32))
    slot_tiles = slot_of.reshape((t_p + rows_s) // TOK_TILE, 1, TOK_TILE * TOP_K)
    xs = _moe_scatter(h2_p.reshape(t_p, D_MODEL), h2_s.reshape(rows_s, D_MODEL), slot_tiles)
    ys = _moe_experts(xs, visits, w1[l], b1[l], w2[l], b2[l])
    y_p, y_s = _moe_combine(ys, slot_tiles, gate_all, x1_p.reshape(t_p, D_MODEL), x1_s.reshape(rows_s, D_MODEL),
                            gate2_p, gate2_s.reshape(rows_s, D_MODEL), g_post_ffn[l], seq)
    y_prompt = y_p.reshape(nb_p, seq, D_MODEL)
    y_sample = y_s.reshape(nb_s, n_tok, D_MODEL)

    st = lambda t: t[None]
    conv_s = jnp.concatenate([state_conv[l].astype(F32), per_b(u_s)], axis=1)[:, -(CONV_W - 1):]
    return (y_prompt, y_sample, st(k_p), st(v_p), st(conv_p), st(c_p), st(n_p), st(m_p[:, :, 0]),
            st(k_new), st(v_new), st(conv_s), st(c_s), st(n_s), st(m_s[:, :, 0]))
```

```python
import functools
import math

import jax
import jax.numpy as jnp
import numpy as np
from jax import lax
from jax.experimental import pallas as pl
from jax.experimental.pallas import tpu as pltpu

F32 = jnp.float32
BF16 = jnp.bfloat16
U32 = jnp.uint32
NEG_INF = float("-inf")

D_MODEL = 1024
H_M = 4
D_M = 512
DK_M = 128
CONV_W = 4
H_A = 8
D_A = 512
HD_A = 64
BRANCHES = ((128, 1), (512, 4), (2048, 16))
WINDOW_MAX = 2048
QB = 128
N_BUCKETS = 32
MAX_DISTANCE = 2048
N_EXPERTS = 32
TOP_K = 4
D_FF = 1024
SWIGLU_LIMIT = 7.0
SWIGLU_ALPHA = 1.702
EPS = 1e-6

LANES = 128
C_U, C_V, C_O, C_QA, C_KA, C_VA, C_G = 0, 1024, 1536, 2048, 2560, 3072, 3584
D_IN_R = C_G + LANES

MLSTM_L_PROMPT = 256
MLSTM_L_SAMPLE = 32
ATT_QL = 512
MOE_M = 512
MOE_SUB = 256
TOK_TILE = 128
PACK_W = D_MODEL // 2
PACK_T = PACK_W // LANES
ROW_T = D_MODEL // LANES
HI_MASK = 0xFFFF0000
VMEM_LIMIT = 56 * 1024 * 1024


def _params(sem, vmem=None):
    return pltpu.CompilerParams(dimension_semantics=sem, vmem_limit_bytes=vmem or VMEM_LIMIT)


def _rms(x, g):
    return x * lax.rsqrt(jnp.mean(x * x, axis=-1, keepdims=True) + EPS) * g


def _log_sigmoid(t):
    return jnp.minimum(t, 0.0) - jnp.log1p(jnp.exp(-jnp.abs(t)))


def _ada_kernel(c_ref, w_ref, b_ref, o_ref):
    c = c_ref[...]
    a = (c * jax.nn.sigmoid(c)).astype(BF16)
    o_ref[...] = jnp.dot(a, w_ref[...].astype(BF16), preferred_element_type=F32) + b_ref[...]


def _adaln(c_all, w_ada, b_ada):
    rows = c_all.shape[0]
    n = w_ada.shape[1]
    tn = 1024
    return pl.pallas_call(
        _ada_kernel,
        grid=(n // tn,),
        in_specs=[
            pl.BlockSpec((rows, D_MODEL), lambda j: (0, 0)),
            pl.BlockSpec((D_MODEL, tn), lambda j: (0, j)),
            pl.BlockSpec((1, tn), lambda j: (0, j)),
        ],
        out_specs=pl.BlockSpec((rows, tn), lambda j: (0, j)),
        out_shape=jax.ShapeDtypeStruct((rows, n), F32),
        compiler_params=_params(("arbitrary",)),
        name="adaln",
    )(c_all, w_ada, b_ada.reshape(1, n))


def _to_dilated(x, dil):
    return pltpu.einshape("(lr)c->l(rc)", x, r=dil)


def _from_dilated(x, dil):
    return pltpu.einshape("l(rc)->(lr)c", x, r=dil)


def _inproj_kernel(x_ref, g_ref, sc_ref, sh_ref, w_ref, bg_ref,
                   u_ref, v_ref, o_ref, gt_ref, qa_ref, ka_ref, va_ref, *dil_refs, dilations):
    x = x_ref[0]
    h = _rms(x, g_ref[...]) * (1.0 + sc_ref[0]) + sh_ref[0]
    hb = h.astype(BF16)

    def seg(lo, width):
        return jnp.dot(hb, w_ref[:, lo:lo + width], preferred_element_type=F32)

    u_ref[0] = seg(C_U, 2 * D_M)
    v_ref[0] = seg(C_V, D_M)
    o_ref[0] = jax.nn.sigmoid(seg(C_O, D_M))
    qkv = (seg(C_QA, D_A) * (HD_A ** -0.5), seg(C_KA, D_A), seg(C_VA, D_A))
    for ref, val in zip((qa_ref, ka_ref, va_ref), qkv):
        ref[0] = val
    for n, dil in enumerate(dilations):
        for j, val in enumerate(qkv):
            dil_refs[3 * n + j][0] = _to_dilated(val, dil)
    t = seg(C_G, LANES) + bg_ref[...]
    lane = lax.broadcasted_iota(jnp.int32, t.shape, 1)
    gt_ref[0] = jnp.where(lane < H_M, t, _log_sigmoid(t))


def _inproj(x3, g_pre, scale, shift, w_r, bias_g, tm, dilations=()):
    nb, rows, _ = x3.shape
    per_row_mod = scale.shape[1] != 1
    mod_block = (1, tm, D_MODEL) if per_row_mod else (1, 1, D_MODEL)
    mod_map = (lambda b, i: (b, i, 0)) if per_row_mod else (lambda b, i: (b, 0, 0))
    row_map = lambda b, i: (b, i, 0)
    widths = (2 * D_M, D_M, D_M, LANES, D_A, D_A, D_A)
    dil_specs = [pl.BlockSpec((1, tm // d, d * D_A), row_map) for d in dilations for _ in range(3)]
    dil_shapes = [jax.ShapeDtypeStruct((nb, rows // d, d * D_A), F32) for d in dilations for _ in range(3)]
    return pl.pallas_call(
        functools.partial(_inproj_kernel, dilations=tuple(dilations)),
        grid=(nb, rows // tm),
        in_specs=[
            pl.BlockSpec((1, tm, D_MODEL), row_map),
            pl.BlockSpec((1, D_MODEL), lambda b, i: (0, 0)),
            pl.BlockSpec(mod_block, mod_map),
            pl.BlockSpec(mod_block, mod_map),
            pl.BlockSpec((D_MODEL, D_IN_R), lambda b, i: (0, 0)),
            pl.BlockSpec((1, LANES), lambda b, i: (0, 0)),
        ],
        out_specs=[pl.BlockSpec((1, tm, w), row_map) for w in widths] + dil_specs,
        out_shape=[jax.ShapeDtypeStruct((nb, rows, w), F32) for w in widths] + dil_shapes,
        compiler_params=_params(("arbitrary", "arbitrary")),
        name="inproj",
    )(x3, g_pre.reshape(1, D_MODEL), scale, shift, w_r, bias_g)


def _mlstm_kernel(u_ref, v_ref, og_ref, gc_ref, gr_ref, tail0_ref, c0_ref, n0_ref, m0_ref,
                  cw_ref, cb_ref, mh_ref,
                  mem_ref, c_out_ref, n_out_ref, m_out_ref,
                  c_s, n_s, m_s, tail_s, uext_s, *, chunk):
    L = chunk
    step = pl.program_id(1)

    @pl.when(step == 0)
    def _():
        c_s[...] = c0_ref[0]
        n_s[...] = n0_ref[0]
        m_s[...] = m0_ref[0]
        tail_s[...] = tail0_ref[0]

    u = u_ref[0]
    uext_s[0:8, :] = tail_s[...]
    uext_s[8:8 + L, :] = u
    acc = cb_ref[...] + uext_s[pl.ds(5, L), :] * cw_ref[0:1, :]
    acc = acc + uext_s[pl.ds(6, L), :] * cw_ref[1:2, :]
    acc = acc + uext_s[pl.ds(7, L), :] * cw_ref[2:3, :]
    acc = acc + u * cw_ref[3:4, :]
    tail_s[...] = uext_s[pl.ds(L, 8), :]
    qk = acc * jax.nn.sigmoid(acc)

    gc = gc_ref[0]
    gr = gr_ref[0]
    row = lax.broadcasted_iota(jnp.int32, (L, L), 0)
    col = lax.broadcasted_iota(jnp.int32, (L, L), 1)
    causal = col <= row
    tri = causal.astype(BF16)
    tri_t = (row <= col).astype(BF16)
    lane = lax.broadcasted_iota(jnp.int32, gc.shape, 1)
    srow = lax.broadcasted_iota(jnp.int32, gr.shape, 0)

    def bf16_parts(x):
        hi = x.astype(BF16)
        r1 = x - hi.astype(F32)
        mid = r1.astype(BF16)
        return hi, mid, (r1 - mid.astype(F32)).astype(BF16)

    b_col_all = sum(jnp.dot(tri, p, preferred_element_type=F32)
                    for p in bf16_parts(jnp.where(lane >= H_M, gc, 0.0)))
    b_row_all = sum(jnp.dot(p, tri_t, preferred_element_type=F32)
                    for p in bf16_parts(jnp.where(srow >= H_M, gr, 0.0)))

    v_all = v_ref[0]
    og = og_ref[0]
    for h in range(H_M):
        q = qk[:, h * DK_M:(h + 1) * DK_M]
        k = qk[:, D_M + h * DK_M:D_M + (h + 1) * DK_M] * (DK_M ** -0.5)
        v = v_all[:, h * DK_M:(h + 1) * DK_M]
        li_row = gr[h:h + 1, :]
        li_col = gc[:, h:h + 1]
        b_row = b_row_all[H_M + h:H_M + h + 1, :]
        b_col = b_col_all[:, H_M + h:H_M + h + 1]
        m0 = m_s[h:h + 1, 0:1]
        c0 = c_s[h]
        n0 = n_s[h:h + 1, :]

        d = jnp.where(causal, b_col - b_row + li_row, NEG_INF)
        g_state = b_col + m0
        m = jnp.maximum(jnp.max(d, axis=-1, keepdims=True), g_state)
        w_state = jnp.exp(g_state - m)
        qb = q.astype(BF16)
        kb = k.astype(BF16)
        vb = v.astype(BF16)
        s = lax.dot_general(qb, kb, (((1,), (1,)), ((), ())), preferred_element_type=F32)
        s = s * jnp.exp(d - m)
        num = jnp.dot(s.astype(BF16), vb, preferred_element_type=F32)
        num = num + w_state * jnp.dot(qb, c0.astype(BF16), preferred_element_type=F32)
        den = jnp.sum(s, axis=-1, keepdims=True) + w_state * jnp.sum(q * n0, axis=-1, keepdims=True)
        hh = num / jnp.maximum(jnp.abs(den), jnp.exp(-m))

        b_last = b_col[L - 1:L, :]
        g_tok_row = b_last - b_row + li_row
        m_new = jnp.maximum(b_last + m0, jnp.max(g_tok_row, axis=-1, keepdims=True))
        w_tok_col = jnp.exp(b_last - b_col + li_col - m_new)
        decay = jnp.exp(b_last + m0 - m_new)
        kw = k * w_tok_col
        c_s[h] = decay * c0 + jnp.dot(kw.T.astype(BF16), vb, preferred_element_type=F32)
        n_s[h:h + 1, :] = decay * n0 + jnp.sum(kw, axis=0, keepdims=True)
        m_s[h:h + 1, :] = jnp.broadcast_to(m_new, (1, LANES))

        hn = hh * lax.rsqrt(jnp.mean(hh * hh, axis=-1, keepdims=True) + EPS)
        hn = hn * mh_ref[:, h * DK_M:(h + 1) * DK_M]
        mem_ref[0, :, h * DK_M:(h + 1) * DK_M] = og[:, h * DK_M:(h + 1) * DK_M] * hn

    @pl.when(step == pl.num_programs(1) - 1)
    def _():
        c_out_ref[0] = c_s[...]
        n_out_ref[0] = n_s[...]
        m_out_ref[0] = m_s[...]


def _mlstm(u, v, og, gates_col, gates_row, tail0, c0, n0, m0x, conv_w, conv_b, mh_norm, chunk):
    nb, rows, _ = u.shape
    nc = rows // chunk
    row_map = lambda b, c: (b, c, 0)
    bat3 = lambda b, c: (b, 0, 0)
    const2 = lambda b, c: (0, 0)
    return pl.pallas_call(
        functools.partial(_mlstm_kernel, chunk=chunk),
        grid=(nb, nc),
        in_specs=[
            pl.BlockSpec((1, chunk, 2 * D_M), row_map),
            pl.BlockSpec((1, chunk, D_M), row_map),
            pl.BlockSpec((1, chunk, D_M), row_map),
            pl.BlockSpec((1, chunk, LANES), row_map),
            pl.BlockSpec((1, 8, chunk), lambda b, c: (b, 0, c)),
            pl.BlockSpec((1, 8, 2 * D_M), bat3),
            pl.BlockSpec((1, H_M, DK_M, DK_M), lambda b, c: (b, 0, 0, 0)),
            pl.BlockSpec((1, H_M, DK_M), bat3),
            pl.BlockSpec((1, H_M, LANES), bat3),
            pl.BlockSpec((CONV_W, 2 * D_M), const2),
            pl.BlockSpec((1, 2 * D_M), const2),
            pl.BlockSpec((1, D_M), const2),
        ],
        out_specs=[
            pl.BlockSpec((1, chunk, D_M), row_map),
            pl.BlockSpec((1, H_M, DK_M, DK_M), lambda b, c: (b, 0, 0, 0)),
            pl.BlockSpec((1, H_M, DK_M), bat3),
            pl.BlockSpec((1, H_M, LANES), bat3),
        ],
        out_shape=[
            jax.ShapeDtypeStruct((nb, rows, D_M), F32),
            jax.ShapeDtypeStruct((nb, H_M, DK_M, DK_M), F32),
            jax.ShapeDtypeStruct((nb, H_M, DK_M), F32),
            jax.ShapeDtypeStruct((nb, H_M, LANES), F32),
        ],
        scratch_shapes=[
            pltpu.VMEM((H_M, DK_M, DK_M), F32),
            pltpu.VMEM((H_M, DK_M), F32),
            pltpu.VMEM((H_M, LANES), F32),
            pltpu.VMEM((8, 2 * D_M), F32),
            pltpu.VMEM((chunk + 8, 2 * D_M), F32),
        ],
        compiler_params=_params(("arbitrary", "arbitrary")),
        name="mlstm",
    )(u, v, og, gates_col, gates_row, tail0, c0, n0, m0x,
      conv_w, conv_b.reshape(1, 2 * D_M), mh_norm.reshape(1, D_M))


def _attn_kernel(q_ref, kc_ref, vc_ref, kp_ref, vp_ref, bias_ref, ones_ref, o_ref, l_ref, k_s, v_s, *, ql):
    first = pl.program_id(2) == 0
    k_s[0:QB, :] = kp_ref[0].astype(BF16)
    k_s[QB:QB + ql, :] = kc_ref[0].astype(BF16)
    v_s[0:QB, :] = vp_ref[0].astype(BF16)
    v_s[QB:QB + ql, :] = vc_ref[0].astype(BF16)
    col = lax.broadcasted_iota(jnp.int32, (QB, 2 * QB), 1)
    pad_keys = jnp.logical_and(first, col < QB)
    head0 = lax.broadcasted_iota(jnp.int32, (1, LANES), 1) < HD_A
    nt = (((1,), (1,)), ((), ()))
    for c in range(D_A // LANES):
        sl = slice(c * LANES, (c + 1) * LANES)
        v_all = v_s[:, sl]
        v_lo = jnp.where(head0, v_all, jnp.zeros_like(v_all))
        v_hi = jnp.where(head0, jnp.zeros_like(v_all), v_all)
        for j in range(ql // QB):
            rows = slice(j * QB, (j + 1) * QB)
            keys = slice(j * QB, (j + 2) * QB)
            q = q_ref[0, rows, sl].astype(BF16)
            kb = k_s[keys, sl]
            ps, ms = [], []
            for half in range(2):
                qh = jnp.where(head0, q, jnp.zeros_like(q)) if half == 0 else jnp.where(head0, jnp.zeros_like(q), q)
                logits = lax.dot_general(qh, kb, nt, preferred_element_type=F32) + bias_ref[2 * c + half]
                if j == 0:
                    logits = jnp.where(pad_keys, NEG_INF, logits)
                m = jnp.max(logits, axis=-1, keepdims=True)
                ps.append(jnp.exp(logits - m).astype(BF16))
                ms.append(m)
            p_cat = jnp.concatenate(ps, axis=1)
            v_bd = jnp.concatenate([v_lo[keys], v_hi[keys]], axis=0)
            o = jnp.dot(p_cat, v_bd, preferred_element_type=F32)
            s = jnp.dot(p_cat, ones_ref[...], preferred_element_type=F32)
            o_ref[0, rows, sl] = o / s
            l_ref[0, rows, sl] = jnp.where(head0, ms[0], ms[1]) + jnp.log(s)


def _attn_branch(qa, ka, va, bias, dil):
    nb, ld, _ = qa.shape
    ql = min(ATT_QL, ld)
    cur = lambda b, r, i: (b, i, r)
    prev = lambda b, r, i: (b, jnp.maximum(i * (ql // QB) - 1, 0), r)
    ones_bd = np.zeros((4 * QB, LANES), np.float32)
    ones_bd[:2 * QB, :HD_A] = 1.0
    ones_bd[2 * QB:, HD_A:] = 1.0
    return pl.pallas_call(
        functools.partial(_attn_kernel, ql=ql),
        grid=(nb, dil, ld // ql),
        in_specs=[
            pl.BlockSpec((1, ql, D_A), cur),
            pl.BlockSpec((1, ql, D_A), cur),
            pl.BlockSpec((1, ql, D_A), cur),
            pl.BlockSpec((1, QB, D_A), prev),
            pl.BlockSpec((1, QB, D_A), prev),
            pl.BlockSpec((H_A, QB, 2 * QB), lambda b, r, i: (0, 0, 0)),
            pl.BlockSpec((4 * QB, LANES), lambda b, r, i: (0, 0)),
        ],
        out_specs=[pl.BlockSpec((1, ql, D_A), cur), pl.BlockSpec((1, ql, D_A), cur)],
        out_shape=[jax.ShapeDtypeStruct((nb, ld, dil * D_A), F32)] * 2,
        scratch_shapes=[pltpu.VMEM((QB + ql, D_A), BF16), pltpu.VMEM((QB + ql, D_A), BF16)],
        compiler_params=_params(("arbitrary", "arbitrary", "arbitrary")),
        name=f"attn_d{dil}",
    )(qa, ka, va, ka, va, bias, jnp.asarray(ones_bd, BF16))


def _heads_kernel(k_ref, v_ref, ko_ref, vo_ref):
    for src, dst in ((k_ref, ko_ref), (v_ref, vo_ref)):
        for c in range(D_A // LANES):
            xt = src[0, :, c * LANES:(c + 1) * LANES].T
            dst[0, 2 * c] = xt[:HD_A]
            dst[0, 2 * c + 1] = xt[HD_A:]


def _window_heads(ka, va, keep, tm=512):
    nb, seq, _ = ka.shape
    first = (seq - keep) // tm
    in_spec = pl.BlockSpec((1, tm, D_A), lambda b, i: (b, first + i, 0))
    out_spec = pl.BlockSpec((1, H_A, HD_A, tm), lambda b, i: (b, 0, 0, i))
    kt, vt = pl.pallas_call(
        _heads_kernel,
        grid=(nb, keep // tm),
        in_specs=[in_spec, in_spec],
        out_specs=[out_spec, out_spec],
        out_shape=[jax.ShapeDtypeStruct((nb, H_A, HD_A, keep), F32)] * 2,
        compiler_params=_params(("arbitrary", "arbitrary")),
        name="window_heads",
    )(ka, va)
    return kt.transpose(0, 3, 1, 2), vt.transpose(0, 3, 1, 2)


def _t5_bucket(dist):
    max_exact = N_BUCKETS // 2
    n = jnp.maximum(dist, 1).astype(F32)
    large = max_exact + (jnp.log(n / max_exact) / math.log(MAX_DISTANCE / max_exact)
                         * (N_BUCKETS - max_exact)).astype(jnp.int32)
    large = jnp.minimum(large, N_BUCKETS - 1)
    return jnp.where(dist < max_exact, dist, large)


def _branch_bias_vec(rel_bias, window, dil):
    return rel_bias[_t5_bucket(dil * jnp.arange(window // dil + 1))].astype(F32)


def _prompt_bias(rel_bias, window, dil):
    assert window // dil == QB
    n = 2 * QB
    vec = _branch_bias_vec(rel_bias, window, dil)
    vecp = jnp.concatenate([vec[::-1], jnp.full((n - QB - 1, H_A), NEG_INF, F32)], axis=0)
    g = jnp.roll(vecp, QB - 1, axis=0)
    skew = jnp.tile(g, (QB + 1, 1))[:QB * (n + 1)].reshape(QB, n + 1, H_A)[:, :n]
    return skew[::-1].transpose(2, 0, 1)


TOK_PAD = 8


def _sattn_kernel(q_ref, kt_ref, vt_ref, knt_ref, vnt_ref, bc_ref, bn_ref, att_ref):
    nt = (((1,), (1,)), ((), ()))
    for h in range(H_A):
        q = q_ref[0, h].astype(BF16)
        lc = jnp.dot(q, kt_ref[0, h].astype(BF16), preferred_element_type=F32)
        ln = jnp.dot(q, knt_ref[0, h].astype(BF16), preferred_element_type=F32)
        lcs = [lc + bc_ref[d, h] for d in range(len(BRANCHES))]
        lns = [ln + bn_ref[d, h] for d in range(len(BRANCHES))]
        m = functools.reduce(jnp.maximum, [jnp.max(x, axis=-1, keepdims=True) for x in lcs + lns])
        pc = functools.reduce(jnp.add, [jnp.exp(x - m) for x in lcs])
        pn = functools.reduce(jnp.add, [jnp.exp(x - m) for x in lns])
        s = jnp.sum(pc, axis=-1, keepdims=True) + jnp.sum(pn, axis=-1, keepdims=True)
        o = lax.dot_general(pc.astype(BF16), vt_ref[0, h].astype(BF16), nt, preferred_element_type=F32)
        o = o + lax.dot_general(pn.astype(BF16), vnt_ref[0, h].astype(BF16), nt, preferred_element_type=F32)
        att_ref[0, h] = o / s


def _sample_attention(q_h, kt, vt, knt, vnt, bias_c, bias_n):
    nb, _, _, n_past = kt.shape
    bat = lambda b: (b, 0, 0, 0)
    const = lambda b: (0, 0, 0, 0)
    return pl.pallas_call(
        _sattn_kernel,
        grid=(nb,),
        in_specs=[
            pl.BlockSpec((1, H_A, TOK_PAD, HD_A), bat),
            pl.BlockSpec((1, H_A, HD_A, n_past), bat),
            pl.BlockSpec((1, H_A, HD_A, n_past), bat),
            pl.BlockSpec((1, H_A, HD_A, LANES), bat),
            pl.BlockSpec((1, H_A, HD_A, LANES), bat),
            pl.BlockSpec(bias_c.shape, const),
            pl.BlockSpec(bias_n.shape, const),
        ],
        out_specs=pl.BlockSpec((1, H_A, TOK_PAD, HD_A), bat),
        out_shape=jax.ShapeDtypeStruct((nb, H_A, TOK_PAD, HD_A), F32),
        compiler_params=_params(("arbitrary",)),
        name="sample_attn",
    )(q_h, kt, vt, knt, vnt, bias_c, bias_n)


def _sample_bias(rel_bias, n_tok, n_past):
    ninf = lambda n: jnp.full((n, H_A), NEG_INF, F32)
    cached, new = [], []
    for window, dil in BRANCHES:
        nk = window // dil
        rows = n_past // dil
        assert n_past % dil == 0 and rows >= nk
        vec = _branch_bias_vec(rel_bias, window, dil)
        res = jnp.arange(dil)[None, :, None]
        tc, tn = [], []
        for t in range(n_tok):
            near = t // dil + 1
            col = jnp.concatenate([ninf(rows - (nk + 1 - near)), vec[near:nk + 1][::-1]], axis=0)
            tab = jnp.where(res == t % dil, col[:, None, :], NEG_INF).reshape(n_past, H_A)
            tc.append(tab)
            keys = [vec[(t - j) // dil][None] if (t - j) % dil == 0 and 0 <= t - j <= nk * dil else ninf(1)
                    for j in range(n_tok)]
            tn.append(jnp.concatenate(keys + [ninf(LANES - n_tok)], axis=0))
        pad = lambda tabs: jnp.stack(tabs + [jnp.zeros_like(tabs[0])] * (TOK_PAD - n_tok))
        cached.append(pad(tc).transpose(2, 0, 1))
        new.append(pad(tn).transpose(2, 0, 1))
    return jnp.stack(cached), jnp.stack(new)


def _postmix_kernel(mem_ref, o1_ref, l1_ref, o2_ref, l2_ref, o3_ref, l3_ref, x_ref,
                    g1_ref, sc2_ref, sh2_ref, gpost_ref, gpre_ref, wo_ref, wr_ref, br_ref, cnt0_ref,
                    x1_ref, h2_ref, idx_ref, gate_ref, cnt_ref, cnt_s, *, dilations):
    first_step = jnp.logical_and(pl.program_id(0) == 0, pl.program_id(1) == 0)

    @pl.when(first_step)
    def _():
        cnt_s[...] = cnt0_ref[...]

    rows = lambda ref, dil: ref[0] if dil == 1 else _from_dilated(ref[0], dil)
    o1, o2, o3 = (rows(r, d) for r, d in zip((o1_ref, o2_ref, o3_ref), dilations))
    l1, l2, l3 = (rows(r, d) for r, d in zip((l1_ref, l2_ref, l3_ref), dilations))
    lmax = jnp.maximum(jnp.maximum(l1, l2), l3)
    w1 = jnp.exp(l1 - lmax)
    w2 = jnp.exp(l2 - lmax)
    w3 = jnp.exp(l3 - lmax)
    att = (w1 * o1 + w2 * o2 + w3 * o3) / (w1 + w2 + w3)
    y = jnp.dot(mem_ref[0].astype(BF16), wo_ref[0:D_M, :], preferred_element_type=F32)
    y = y + jnp.dot(att.astype(BF16), wo_ref[D_M:, :], preferred_element_type=F32)
    x1 = x_ref[0] + g1_ref[0] * _rms(y, gpost_ref[...])
    x1_ref[0] = x1
    h2 = _rms(x1, gpre_ref[...]) * (1.0 + sc2_ref[0]) + sh2_ref[0]
    h2_ref[0] = h2
    h_hi = h2.astype(BF16)
    h_lo = (h2 - h_hi.astype(F32)).astype(BF16)
    w_hi = wr_ref[0]
    logits = jnp.dot(h_hi, w_hi, preferred_element_type=F32)
    logits = logits + jnp.dot(h_lo, w_hi, preferred_element_type=F32)
    logits = logits + jnp.dot(h_hi, wr_ref[1], preferred_element_type=F32) + br_ref[...]
    lane = lax.broadcasted_iota(jnp.int32, logits.shape, 1)
    lane_f = lane.astype(F32)
    logits = jnp.where(lane < N_EXPERTS, logits, NEG_INF)
    vals, idxs = [], []
    for _ in range(TOP_K):
        mx = jnp.max(logits, axis=-1, keepdims=True)
        ix = jnp.min(jnp.where(logits == mx, lane_f, float(LANES)), axis=-1, keepdims=True)
        vals.append(mx)
        idxs.append(ix)
        logits = jnp.where(lane_f == ix, NEG_INF, logits)
    es = [jnp.exp(v - vals[0]) for v in vals]
    tot = es[0] + es[1] + es[2] + es[3]
    tm = logits.shape[0]
    picks = [(lane_f == ix).astype(F32) for ix in idxs]
    chosen = picks[0] + picks[1] + picks[2] + picks[3]
    r_i = lax.broadcasted_iota(jnp.int32, (tm, tm), 0)
    c_i = lax.broadcasted_iota(jnp.int32, (tm, tm), 1)
    earlier = jnp.dot((c_i < r_i).astype(BF16), chosen.astype(BF16), preferred_element_type=F32)
    before = earlier + cnt_s[...]
    cnt_s[...] = cnt_s[...] + jnp.sum(chosen, axis=0, keepdims=True)
    cnt_ref[...] = cnt_s[...]
    idx_tile = jnp.zeros(logits.shape, F32)
    gate_tile = jnp.zeros(logits.shape, F32)
    for k in range(TOP_K):
        rank = jnp.sum(picks[k] * before, axis=-1, keepdims=True)
        idx_tile = jnp.where(lane == k, idxs[k], idx_tile)
        idx_tile = jnp.where(lane == TOP_K + k, rank, idx_tile)
        gate_tile = jnp.where(lane == k, es[k] / tot, gate_tile)
    idx_ref[0] = idx_tile.astype(jnp.int32)
    gate_ref[0] = gate_tile


def _postmix(mem, branches, x3, gate1, scale2, shift2, g_post, g_pre, w_out_b, w_router_p, b_router_p, counts0, tm,
             dilations=(1, 1, 1)):
    nb, rows, _ = x3.shape
    per_row_mod = gate1.shape[1] != 1
    mod_block = (1, tm, D_MODEL) if per_row_mod else (1, 1, D_MODEL)
    mod_map = (lambda b, i: (b, i, 0)) if per_row_mod else (lambda b, i: (b, 0, 0))
    row_map = lambda b, i: (b, i, 0)
    const2 = lambda b, i: (0, 0)
    half = pl.BlockSpec((1, tm, D_M), row_map)
    flat = [a for pair in branches for a in pair]
    branch_specs = [pl.BlockSpec((1, tm // d, d * D_A), row_map) for d in dilations for _ in range(2)]
    return pl.pallas_call(
        functools.partial(_postmix_kernel, dilations=tuple(dilations)),
        grid=(nb, rows // tm),
        in_specs=[half] + branch_specs + [
            pl.BlockSpec((1, tm, D_MODEL), row_map),
            pl.BlockSpec(mod_block, mod_map),
            pl.BlockSpec(mod_block, mod_map),
            pl.BlockSpec(mod_block, mod_map),
            pl.BlockSpec((1, D_MODEL), const2),
            pl.BlockSpec((1, D_MODEL), const2),
            pl.BlockSpec((D_MODEL, D_MODEL), const2),
            pl.BlockSpec((2, D_MODEL, LANES), lambda b, i: (0, 0, 0)),
            pl.BlockSpec((1, LANES), const2),
            pl.BlockSpec((1, LANES), const2),
        ],
        out_specs=[
            pl.BlockSpec((1, tm, D_MODEL), row_map),
            pl.BlockSpec((1, tm, D_MODEL), row_map),
            pl.BlockSpec((1, tm, LANES), row_map),
            pl.BlockSpec((1, tm, LANES), row_map),
            pl.BlockSpec((1, LANES), const2),
        ],
        out_shape=[
            jax.ShapeDtypeStruct((nb, rows, D_MODEL), F32),
            jax.ShapeDtypeStruct((nb, rows, D_MODEL), F32),
            jax.ShapeDtypeStruct((nb, rows, LANES), jnp.int32),
            jax.ShapeDtypeStruct((nb, rows, LANES), F32),
            jax.ShapeDtypeStruct((1, LANES), F32),
        ],
        scratch_shapes=[pltpu.VMEM((1, LANES), F32)],
        compiler_params=_params(("arbitrary", "arbitrary")),
        name="postmix",
    )(mem, *flat, x3, gate1, scale2, shift2, g_post.reshape(1, D_MODEL), g_pre.reshape(1, D_MODEL),
      w_out_b, w_router_p, b_router_p, counts0)


def _row_wait(src_rows, dst_rows, sem):
    pltpu.make_async_copy(src_rows, dst_rows, sem).wait()


def _scatter_kernel(slot_ref, hp_ref, hs_ref, xs_hbm, buf, sem, *, n_prompt_tiles):
    h = jnp.where(pl.program_id(0) >= n_prompt_tiles, hs_ref[...], hp_ref[...])
    bits = pltpu.bitcast(h.astype(BF16).astype(F32), U32)
    packed = (bits[:, :PACK_W] & U32(HI_MASK)) | (bits[:, PACK_W:] >> 16)
    buf[...] = pltpu.einshape("m(jc)->mjc", packed, j=PACK_T)

    def issue(r, carry):
        src = buf.at[pl.ds(r, 1)]
        for k in range(TOP_K):
            pltpu.make_async_copy(src, xs_hbm.at[pl.ds(slot_ref[0, 0, r * TOP_K + k], 1)],
                                  sem).start(priority=k % 2)
        return carry

    lax.fori_loop(0, TOK_TILE, issue, 0, unroll=4)
    for _ in range(TOP_K):
        _row_wait(buf, xs_hbm.at[pl.ds(0, TOK_TILE)], sem)


def _moe_scatter(h_p, h_s, slot_tiles):
    tiles = slot_tiles.shape[0]
    n_p = h_p.shape[0] // TOK_TILE
    assert tiles == n_p + 1 and h_s.shape[0] == TOK_TILE
    n_rows = tiles * TOK_TILE * TOP_K
    return pl.pallas_call(
        functools.partial(_scatter_kernel, n_prompt_tiles=n_p),
        grid=(tiles,),
        in_specs=[
            pl.BlockSpec((1, 1, TOK_TILE * TOP_K), lambda i: (i, 0, 0), memory_space=pltpu.SMEM),
            pl.BlockSpec((TOK_TILE, D_MODEL), lambda i: (jnp.minimum(i, n_p - 1), 0)),
            pl.BlockSpec((TOK_TILE, D_MODEL), lambda i: (0, 0)),
        ],
        out_specs=pl.BlockSpec(memory_space=pl.ANY),
        out_shape=jax.ShapeDtypeStruct((n_rows, PACK_T, LANES), U32),
        scratch_shapes=[pltpu.VMEM((TOK_TILE, PACK_T, LANES), U32), pltpu.SemaphoreType.DMA],
        compiler_params=_params(("arbitrary",)),
        name="moe_scatter",
    )(slot_tiles, h_p, h_s)


def _expert_kernel(tile_ref, exp_ref, next_ref, lo_ref, hi_ref, first_ref, nvis_ref,
                   xs_ref, w1_hbm, b1_ref, w2_hbm, b2_ref, ys_ref, w1f, w2f, w1b, w2b, sem):
    v = pl.program_id(0)
    active = v < nvis_ref[0]
    changed = jnp.logical_or(v == 0, exp_ref[v] != exp_ref[jnp.maximum(v - 1, 0)])
    lo = lo_ref[v]
    hi = hi_ref[v]

    def weight_copies(e):
        return (pltpu.make_async_copy(w1_hbm.at[e], w1f, sem.at[0]),
                pltpu.make_async_copy(w2_hbm.at[e], w2f, sem.at[1]))

    @pl.when(v == 0)
    def _():
        for c in weight_copies(exp_ref[0]):
            c.start()

    @pl.when(jnp.logical_and(active, changed))
    def _():
        for c in weight_copies(exp_ref[v]):
            c.wait()
        w1b[...] = w1f[...].astype(BF16)
        w2b[...] = w2f[...].astype(BF16)

        @pl.when(next_ref[v] >= 0)
        def _():
            for c in weight_copies(next_ref[v]):
                c.start()

    def mlp_rows(r0):
        w = pltpu.einshape("mjc->m(jc)", xs_ref[r0:r0 + MOE_SUB])
        x_hi = pltpu.bitcast(w & U32(HI_MASK), F32).astype(BF16)
        x_lo = pltpu.bitcast(w << 16, F32).astype(BF16)
        hgu = jnp.dot(x_hi, w1b[0:PACK_W, :], preferred_element_type=F32)
        hgu = hgu + jnp.dot(x_lo, w1b[PACK_W:, :], preferred_element_type=F32) + b1_ref[0]
        x_glu = jnp.minimum(hgu[:, :D_FF], SWIGLU_LIMIT)
        x_lin = jnp.clip(hgu[:, D_FF:], -SWIGLU_LIMIT, SWIGLU_LIMIT)
        act = x_glu * jax.nn.sigmoid(SWIGLU_ALPHA * x_glu) * (x_lin + 1.0)
        y = jnp.dot(act.astype(BF16), w2b[...], preferred_element_type=F32) + b2_ref[0]
        return pltpu.einshape("m(jc)->mjc", y, j=ROW_T)

    subs = [r * MOE_SUB for r in range(MOE_M // MOE_SUB)]
    whole = jnp.logical_and(lo == 0, hi == MOE_M)

    @pl.when(jnp.logical_and(active, whole))
    def _():
        for r0 in subs:
            ys_ref[r0:r0 + MOE_SUB] = mlp_rows(r0)

    shared = jnp.logical_and(active, jnp.logical_not(whole))

    @pl.when(jnp.logical_and(shared, first_ref[v] == 1))
    def _():
        ys_ref[...] = jnp.zeros(ys_ref.shape, F32)

    for r0 in subs:
        @pl.when(jnp.logical_and(shared, jnp.logical_and(lo < r0 + MOE_SUB, hi > r0)))
        def _():
            rows = r0 + lax.broadcasted_iota(jnp.int32, (MOE_SUB, 1, 1), 0)
            mine = jnp.logical_and(rows >= lo, rows < hi)
            ys_ref[r0:r0 + MOE_SUB] = jnp.where(mine, mlp_rows(r0), ys_ref[r0:r0 + MOE_SUB])


def _moe_experts(xs, visits, w1, b1, w2, b2):
    n_rows = xs.shape[0]
    n_vis = visits[0].shape[0]
    tile_map = lambda v, tile, *_: (tile[v], 0, 0)
    exp_map = lambda v, tile, exp, *_: (exp[v], 0, 0)
    return pl.pallas_call(
        _expert_kernel,
        grid_spec=pltpu.PrefetchScalarGridSpec(
            num_scalar_prefetch=7,
            grid=(n_vis,),
            in_specs=[
                pl.BlockSpec((MOE_M, PACK_T, LANES), tile_map),
                pl.BlockSpec(memory_space=pl.ANY),
                pl.BlockSpec((1, 1, 2 * D_FF), exp_map),
                pl.BlockSpec(memory_space=pl.ANY),
                pl.BlockSpec((1, 1, D_MODEL), exp_map),
            ],
            out_specs=pl.BlockSpec((MOE_M, ROW_T, LANES), tile_map),
            scratch_shapes=[pltpu.VMEM((D_MODEL, 2 * D_FF), F32), pltpu.VMEM((D_FF, D_MODEL), F32),
                            pltpu.VMEM((D_MODEL, 2 * D_FF), BF16), pltpu.VMEM((D_FF, D_MODEL), BF16),
                            pltpu.SemaphoreType.DMA((2,))],
        ),
        out_shape=jax.ShapeDtypeStruct((n_rows, ROW_T, LANES), F32),
        compiler_params=_params(("arbitrary",)),
        name="moe_experts",
    )(*visits, xs, w1, b1.reshape(N_EXPERTS, 1, 2 * D_FF), w2, b2.reshape(N_EXPERTS, 1, D_MODEL))


def _combine_kernel(slot_ref, next_ref, gatep_ref, gates_ref, x1p_ref, x1s_ref, g2p_ref, g2s_ref, gpost_ref, ys_hbm,
                    yp_ref, ysm_ref, buf, sem, *, n_prompt_tiles):
    i = pl.program_id(0)
    last = pl.num_programs(0) - 1
    cur = lax.rem(i, 2)

    def gather(slots, b):
        def issue(r, carry):
            for k in range(TOP_K):
                pltpu.make_async_copy(ys_hbm.at[pl.ds(slots[0, 0, r * TOP_K + k], 1)],
                                      buf.at[b, pl.ds(k * TOK_TILE + r, 1)], sem.at[b]).start(priority=k % 2)
            return carry

        lax.fori_loop(0, TOK_TILE, issue, 0, unroll=4)

    @pl.when(i == 0)
    def _():
        gather(slot_ref, cur)

    @pl.when(i < last)
    def _():
        gather(next_ref, 1 - cur)

    for k in range(TOP_K):
        _row_wait(ys_hbm.at[pl.ds(0, TOK_TILE)], buf.at[cur, pl.ds(k * TOK_TILE, TOK_TILE)], sem.at[cur])
    is_sample = i >= n_prompt_tiles
    g = jnp.where(is_sample, gates_ref[...], gatep_ref[...])
    f = jnp.zeros((TOK_TILE, D_MODEL), F32)
    for k in range(TOP_K):
        rows = pltpu.einshape("mjc->m(jc)", buf[cur, pl.ds(k * TOK_TILE, TOK_TILE)])
        f = f + rows * g[:, k:k + 1]
    x1 = jnp.where(is_sample, x1s_ref[...], x1p_ref[...])
    g2 = jnp.where(is_sample, g2s_ref[...], g2p_ref[0])
    y = x1 + g2 * _rms(f, gpost_ref[...])

    @pl.when(jnp.logical_not(is_sample))
    def _():
        yp_ref[...] = y

    @pl.when(is_sample)
    def _():
        ysm_ref[...] = y


def _moe_combine(ys, slot_tiles, gates_p, gates_s, x1_p, x1_s, gate2_p, gate2_s, g_post, rows_per_batch):
    tiles = slot_tiles.shape[0]
    n_p = x1_p.shape[0] // TOK_TILE
    assert tiles == n_p + 1 and x1_s.shape[0] == TOK_TILE
    tiles_per_batch = rows_per_batch // TOK_TILE
    p_tile = lambda i: (jnp.minimum(i, n_p - 1), 0)
    const2 = lambda i: (0, 0)
    return pl.pallas_call(
        functools.partial(_combine_kernel, n_prompt_tiles=n_p),
        grid=(tiles,),
        in_specs=[
            pl.BlockSpec((1, 1, TOK_TILE * TOP_K), lambda i: (i, 0, 0), memory_space=pltpu.SMEM),
            pl.BlockSpec((1, 1, TOK_TILE * TOP_K), lambda i: (jnp.minimum(i + 1, tiles - 1), 0, 0),
                         memory_space=pltpu.SMEM),
            pl.BlockSpec((TOK_TILE, LANES), p_tile),
            pl.BlockSpec((TOK_TILE, LANES), const2),
            pl.BlockSpec((TOK_TILE, D_MODEL), p_tile),
            pl.BlockSpec((TOK_TILE, D_MODEL), const2),
            pl.BlockSpec((1, 1, D_MODEL), lambda i: (jnp.minimum(i, n_p - 1) // tiles_per_batch, 0, 0)),
            pl.BlockSpec((TOK_TILE, D_MODEL), const2),
            pl.BlockSpec((1, D_MODEL), const2),
            pl.BlockSpec(memory_space=pl.ANY),
        ],
        out_specs=[pl.BlockSpec((TOK_TILE, D_MODEL), p_tile), pl.BlockSpec((TOK_TILE, D_MODEL), const2)],
        out_shape=[jax.ShapeDtypeStruct(x1_p.shape, F32), jax.ShapeDtypeStruct(x1_s.shape, F32)],
        scratch_shapes=[pltpu.VMEM((2, TOP_K * TOK_TILE, ROW_T, LANES), F32), pltpu.SemaphoreType.DMA((2,))],
        compiler_params=_params(("arbitrary",)),
        name="moe_combine",
    )(slot_tiles, slot_tiles, gates_p, gates_s, x1_p, x1_s, gate2_p, gate2_s, g_post.reshape(1, D_MODEL), ys)


def _moe_plan(routes, counts):
    a = sum(r.shape[0] for r in routes) * TOP_K
    assert a % MOE_M == 0
    g_end = jnp.cumsum(counts)
    g_start = g_end - counts

    def slots(route):
        flat_e = route[:, :TOP_K].reshape(-1)
        rank = route[:, TOP_K:2 * TOP_K].reshape(-1)
        onehot = flat_e[:, None] == jnp.arange(N_EXPERTS)[None, :]
        return (jnp.sum(jnp.where(onehot, g_start[None, :], 0), axis=1) + rank).astype(jnp.int32)

    slot_of = jnp.concatenate([slots(r) for r in routes])

    n_tiles = a // MOE_M
    n_vis = n_tiles + N_EXPERTS - 1
    first_tile = g_start // MOE_M
    last_tile = jnp.where(counts > 0, (g_end - 1) // MOE_M, first_tile)
    nvis_e = jnp.where(counts > 0, last_tile - first_tile + 1, 0)
    v_end = jnp.cumsum(nvis_e)
    v_start = v_end - nvis_e
    total = v_end[-1]
    v = jnp.arange(n_vis)
    vc = jnp.minimum(v, total - 1)
    e_of = jnp.minimum(jnp.sum((vc[:, None] >= v_end[None, :]).astype(jnp.int32), axis=1), N_EXPERTS - 1)
    is_e = e_of[:, None] == jnp.arange(N_EXPERTS)[None, :]
    of_e = lambda t: jnp.sum(jnp.where(is_e, t[None, :], 0), axis=1)
    tile_of = of_e(first_tile) + (vc - of_e(v_start))
    lo = jnp.maximum(of_e(g_start), tile_of * MOE_M) - tile_of * MOE_M
    hi = jnp.minimum(of_e(g_end), (tile_of + 1) * MOE_M) - tile_of * MOE_M
    prev_tile = jnp.concatenate([jnp.full((1,), -1, tile_of.dtype), tile_of[:-1]])
    first = (tile_of != prev_tile).astype(jnp.int32)
    experts = jnp.arange(N_EXPERTS)
    later = (experts[None, :] > experts[:, None]) & (counts[None, :] > 0)
    next_e = jnp.min(jnp.where(later, experts[None, :], N_EXPERTS), axis=1)
    next_e = jnp.where(next_e < N_EXPERTS, next_e, -1)
    i32 = lambda t: t.astype(jnp.int32)
    visits = (i32(tile_of), i32(e_of), i32(of_e(next_e)), i32(lo), i32(hi), first, i32(total).reshape(1))
    return slot_of, visits


def _reorder_w_in(w_in):
    off_om_end = 4 * D_M
    off_g_end = off_om_end + 2 * H_M
    w = jnp.concatenate([
        w_in[:, :off_om_end],
        w_in[:, off_g_end:],
        w_in[:, off_om_end:off_g_end],
        jnp.zeros((D_MODEL, LANES - 2 * H_M), w_in.dtype),
    ], axis=1)
    return w.astype(BF16)


def kernel(x_prompt, x_sample, cache_k, cache_v, state_conv, state_C, state_n, state_m, c_prompt, c_sample, w_ada, b_ada, g_pre_mix, g_post_mix, g_pre_ffn, g_post_ffn, w_in, b_ig, b_fg, conv_w, conv_b, mh_norm, rel_bias, w_out, w_router, b_router, w1, b1, w2, b2):
    depth = w_in.shape[0]
    assert depth == 1
    l = 0
    nb_p, seq, _ = x_prompt.shape
    nb_s, n_tok, _ = x_sample.shape
    rows_s = nb_s * n_tok
    assert rows_s == TOK_TILE

    n_c = nb_p + nb_s
    c_pad = -(-n_c // 8) * 8
    c_all = jnp.concatenate([c_prompt, c_sample, jnp.zeros((c_pad - n_c, D_MODEL), F32)], axis=0)
    mod = _adaln(c_all, w_ada[l], b_ada[l])
    mods = [mod[:, i * D_MODEL:(i + 1) * D_MODEL] for i in range(6)]
    mods_p = [m[:nb_p].reshape(nb_p, 1, D_MODEL) for m in mods]
    mods_s = [jnp.repeat(m[nb_p:n_c], n_tok, axis=0).reshape(1, rows_s, D_MODEL) for m in mods]

    w_r = _reorder_w_in(w_in[l])
    bias_g = jnp.concatenate([b_ig[l], b_fg[l], jnp.zeros((LANES - 2 * H_M,), F32)]).reshape(1, LANES)
    w_out_b = w_out[l].astype(BF16)
    w_router_f = jnp.concatenate([w_router[l], jnp.zeros((D_MODEL, LANES - N_EXPERTS), F32)], axis=1)
    w_router_hi = w_router_f.astype(BF16)
    w_router_p = jnp.stack([w_router_hi, (w_router_f - w_router_hi.astype(F32)).astype(BF16)])
    b_router_p = jnp.concatenate([b_router[l].astype(F32), jnp.zeros((LANES - N_EXPERTS,), F32)]).reshape(1, LANES)

    shift1, scale1, gate1, shift2, scale2, gate2_p = mods_p
    dils = tuple(d for _, d in BRANCHES)
    assert dils[0] == 1
    u, v_m, og, gts, qa, ka, va, *dilated = _inproj(x_prompt, g_pre_mix[l], scale1, shift1, w_r, bias_g, tm=512,
                                                   dilations=dils[1:])
    qkv_d = [(qa, ka, va)] + [tuple(dilated[3 * n:3 * n + 3]) for n in range(len(dils) - 1)]
    gates_row = gts[:, :, :8].transpose(0, 2, 1)
    zeros = lambda *s: jnp.zeros(s, F32)
    mem, c_p, n_p, m_p = _mlstm(u, v_m, og, gts, gates_row, zeros(nb_p, 8, 2 * D_M),
                                zeros(nb_p, H_M, DK_M, DK_M), zeros(nb_p, H_M, DK_M), zeros(nb_p, H_M, LANES),
                                conv_w[l], conv_b[l], mh_norm[l], MLSTM_L_PROMPT)
    branches = [_attn_branch(*qkv, _prompt_bias(rel_bias, w, d), d) for qkv, (w, d) in zip(qkv_d, BRANCHES)]
    x1_p, h2_p, idx_p, gate_p, cnt_p = _postmix(mem, branches, x_prompt, gate1, scale2, shift2,
                                                g_post_mix[l], g_pre_ffn[l], w_out_b, w_router_p, b_router_p,
                                                zeros(1, LANES), tm=256, dilations=dils)
    keep = min(WINDOW_MAX, seq)
    k_p, v_p = _window_heads(ka, va, keep)
    conv_p = u[:, seq - (CONV_W - 1):]

    shift1, scale1, gate1, shift2, scale2, gate2_s = mods_s
    x_s = x_sample.reshape(1, rows_s, D_MODEL)
    u_s, v_s, og_s, gts_s, qa_s, ka_s, va_s = _inproj(x_s, g_pre_mix[l], scale1, shift1, w_r, bias_g, tm=rows_s)
    per_b = lambda t: t.reshape(nb_s, n_tok, t.shape[-1])
    ls = MLSTM_L_SAMPLE
    pad_rows = lambda t, n: jnp.concatenate([t, jnp.zeros((nb_s, n - t.shape[1], t.shape[2]), t.dtype)], axis=1)
    inert = jnp.concatenate([jnp.full((H_M,), NEG_INF, F32), jnp.zeros((LANES - H_M,), F32)])
    gts_pad = jnp.concatenate([per_b(gts_s), jnp.broadcast_to(inert, (nb_s, ls - n_tok, LANES))], axis=1)
    tail0 = jnp.concatenate([zeros(nb_s, 8 - (CONV_W - 1), 2 * D_M), state_conv[l].astype(F32)], axis=1)
    m0x = jnp.broadcast_to(state_m[l].astype(F32)[:, :, None], (nb_s, H_M, LANES))
    mem_s, c_s, n_s, m_s = _mlstm(pad_rows(per_b(u_s), ls), pad_rows(per_b(v_s), ls), pad_rows(per_b(og_s), ls),
                                  gts_pad, gts_pad[:, :, :8].transpose(0, 2, 1), tail0,
                                  state_C[l].astype(F32), state_n[l].astype(F32), m0x,
                                  conv_w[l], conv_b[l], mh_norm[l], ls)
    mem_s = mem_s[:, :n_tok].reshape(1, rows_s, D_M)
    heads = lambda t: t.reshape(nb_s, n_tok, H_A, HD_A)
    k_new, v_new = heads(ka_s), heads(va_s)
    n_past = cache_k.shape[2]
    assert n_tok <= TOK_PAD
    q_h = jnp.pad(heads(qa_s).transpose(0, 2, 1, 3), ((0, 0), (0, 0), (0, TOK_PAD - n_tok), (0, 0)))
    new_t = lambda t: jnp.pad(t.transpose(0, 2, 3, 1), ((0, 0), (0, 0), (0, 0), (0, LANES - n_tok)))
    cache_t = lambda c: c[l].astype(F32).transpose(0, 2, 3, 1)
    att_s = _sample_attention(q_h, cache_t(cache_k), cache_t(cache_v), new_t(k_new), new_t(v_new),
                              *_sample_bias(rel_bias, n_tok, n_past))
    att_s = att_s[:, :, :n_tok].transpose(0, 2, 1, 3).reshape(1, rows_s, D_A)
    zero_o = zeros(1, rows_s, D_A)
    ninf = jnp.full((1, rows_s, D_A), NEG_INF, F32)
    branches_s = [(att_s, zero_o), (zero_o, ninf), (zero_o, ninf)]
    x1_s, h2_s, idx_s, gate_s, cnt_all = _postmix(mem_s, branches_s, x_s, gate1, scale2, shift2,
                                                  g_post_mix[l], g_pre_ffn[l], w_out_b, w_router_p, b_router_p,
                                                  cnt_p, tm=rows_s)

    t_p = nb_p * seq
    slot_of, visits = _moe_plan([idx_p.reshape(t_p, LANES), idx_s.reshape(rows_s, LANES)],
                                cnt_all[0, :N_EXPERTS].astype(jnp.int32))
    slot_tiles = slot_of.reshape((t_p + rows_s) // TOK_TILE, 1, TOK_TILE * TOP_K)
    xs = _moe_scatter(h2_p.reshape(t_p, D_MODEL), h2_s.reshape(rows_s, D_MODEL), slot_tiles)
    ys = _moe_experts(xs, visits, w1[l], b1[l], w2[l], b2[l])
    y_p, y_s = _moe_combine(ys, slot_tiles, gate_p.reshape(t_p, LANES), gate_s.reshape(rows_s, LANES), x1_p.reshape(t_p, D_MODEL), x1_s.reshape(rows_s, D_MODEL),
                            gate2_p, gate2_s.reshape(rows_s, D_MODEL), g_post_ffn[l], seq)
    y_prompt = y_p.reshape(nb_p, seq, D_MODEL)
    y_sample = y_s.reshape(nb_s, n_tok, D_MODEL)

    st = lambda t: t[None]
    conv_s = jnp.concatenate([state_conv[l].astype(F32), per_b(u_s)], axis=1)[:, -(CONV_W - 1):]
    return (y_prompt, y_sample, st(k_p), st(v_p), st(conv_p), st(c_p), st(n_p), st(m_p[:, :, 0]),
            st(k_new), st(v_new), st(conv_s), st(c_s), st(n_s), st(m_s[:, :, 0]))
```

```python
import functools
import math

import jax
import jax.numpy as jnp
import numpy as np
from jax import lax
from jax.experimental import pallas as pl
from jax.experimental.pallas import tpu as pltpu

F32 = jnp.float32
BF16 = jnp.bfloat16
U32 = jnp.uint32
NEG_INF = float("-inf")

D_MODEL = 1024
H_M = 4
D_M = 512
DK_M = 128
CONV_W = 4
H_A = 8
D_A = 512
HD_A = 64
BRANCHES = ((128, 1), (512, 4), (2048, 16))
WINDOW_MAX = 2048
QB = 128
N_BUCKETS = 32
MAX_DISTANCE = 2048
N_EXPERTS = 32
TOP_K = 4
D_FF = 1024
SWIGLU_LIMIT = 7.0
SWIGLU_ALPHA = 1.702
EPS = 1e-6

LANES = 128
C_U, C_V, C_O, C_QA, C_KA, C_VA, C_G = 0, 1024, 1536, 2048, 2560, 3072, 3584
D_IN_R = C_G + LANES

MLSTM_L_PROMPT = 256
MLSTM_L_SAMPLE = 32
ATT_QL = 512
MOE_M = 512
MOE_SUB = 256
TOK_TILE = 128
PACK_W = D_MODEL // 2
PACK_T = PACK_W // LANES
ROW_T = D_MODEL // LANES
HI_MASK = 0xFFFF0000
VMEM_LIMIT = 56 * 1024 * 1024


def _params(sem, vmem=None):
    return pltpu.CompilerParams(dimension_semantics=sem, vmem_limit_bytes=vmem or VMEM_LIMIT)


def _rms(x, g):
    return x * lax.rsqrt(jnp.mean(x * x, axis=-1, keepdims=True) + EPS) * g


def _log_sigmoid(t):
    return jnp.minimum(t, 0.0) - jnp.log1p(jnp.exp(-jnp.abs(t)))


def _ada_kernel(c_ref, w_ref, b_ref, o_ref):
    c = c_ref[...]
    a = (c * jax.nn.sigmoid(c)).astype(BF16)
    o_ref[...] = jnp.dot(a, w_ref[...].astype(BF16), preferred_element_type=F32) + b_ref[...]


def _adaln(c_all, w_ada, b_ada):
    rows = c_all.shape[0]
    n = w_ada.shape[1]
    tn = 1024
    return pl.pallas_call(
        _ada_kernel,
        grid=(n // tn,),
        in_specs=[
            pl.BlockSpec((rows, D_MODEL), lambda j: (0, 0)),
            pl.BlockSpec((D_MODEL, tn), lambda j: (0, j)),
            pl.BlockSpec((1, tn), lambda j: (0, j)),
        ],
        out_specs=pl.BlockSpec((rows, tn), lambda j: (0, j)),
        out_shape=jax.ShapeDtypeStruct((rows, n), F32),
        compiler_params=_params(("arbitrary",)),
        name="adaln",
    )(c_all, w_ada, b_ada.reshape(1, n))


def _to_dilated(x, dil):
    return pltpu.einshape("(lr)c->l(rc)", x, r=dil)


def _from_dilated(x, dil):
    return pltpu.einshape("l(rc)->(lr)c", x, r=dil)


def _inproj_kernel(x_ref, g_ref, sc_ref, sh_ref, w_ref, bg_ref,
                   u_ref, v_ref, o_ref, gt_ref, qa_ref, ka_ref, va_ref, *dil_refs, dilations):
    x = x_ref[0]
    h = _rms(x, g_ref[...]) * (1.0 + sc_ref[0]) + sh_ref[0]
    hb = h.astype(BF16)

    def seg(lo, width):
        return jnp.dot(hb, w_ref[:, lo:lo + width], preferred_element_type=F32)

    u_ref[0] = seg(C_U, 2 * D_M)
    v_ref[0] = seg(C_V, D_M).astype(v_ref.dtype)
    o_ref[0] = jax.nn.sigmoid(seg(C_O, D_M))
    qkv = (seg(C_QA, D_A) * (HD_A ** -0.5), seg(C_KA, D_A), seg(C_VA, D_A))
    for ref, val in zip((qa_ref, ka_ref, va_ref), qkv):
        ref[0] = val.astype(ref.dtype)
    for n, dil in enumerate(dilations):
        for j, val in enumerate(qkv):
            dil_refs[3 * n + j][0] = _to_dilated(val, dil).astype(BF16)
    t = seg(C_G, LANES) + bg_ref[...]
    lane = lax.broadcasted_iota(jnp.int32, t.shape, 1)
    gt_ref[0] = jnp.where(lane < H_M, t, _log_sigmoid(t))


def _inproj(x3, g_pre, scale, shift, w_r, bias_g, tm, dilations=()):
    nb, rows, _ = x3.shape
    per_row_mod = scale.shape[1] != 1
    mod_block = (1, tm, D_MODEL) if per_row_mod else (1, 1, D_MODEL)
    mod_map = (lambda b, i: (b, i, 0)) if per_row_mod else (lambda b, i: (b, 0, 0))
    row_map = lambda b, i: (b, i, 0)
    widths = (2 * D_M, D_M, D_M, LANES, D_A, D_A, D_A)
    dil_specs = [pl.BlockSpec((1, tm // d, d * D_A), row_map) for d in dilations for _ in range(3)]
    dil_shapes = [jax.ShapeDtypeStruct((nb, rows // d, d * D_A), BF16) for d in dilations for _ in range(3)]
    dtypes = (F32, BF16, F32, F32, BF16, F32, F32)
    return pl.pallas_call(
        functools.partial(_inproj_kernel, dilations=tuple(dilations)),
        grid=(nb, rows // tm),
        in_specs=[
            pl.BlockSpec((1, tm, D_MODEL), row_map),
            pl.BlockSpec((1, D_MODEL), lambda b, i: (0, 0)),
            pl.BlockSpec(mod_block, mod_map),
            pl.BlockSpec(mod_block, mod_map),
            pl.BlockSpec((D_MODEL, D_IN_R), lambda b, i: (0, 0)),
            pl.BlockSpec((1, LANES), lambda b, i: (0, 0)),
        ],
        out_specs=[pl.BlockSpec((1, tm, w), row_map) for w in widths] + dil_specs,
        out_shape=[jax.ShapeDtypeStruct((nb, rows, w), t) for w, t in zip(widths, dtypes)] + dil_shapes,
        compiler_params=_params(("arbitrary", "arbitrary")),
        name="inproj",
    )(x3, g_pre.reshape(1, D_MODEL), scale, shift, w_r, bias_g)


def _mlstm_kernel(u_ref, v_ref, og_ref, gc_ref, gr_ref, tail0_ref, c0_ref, n0_ref, m0_ref,
                  cw_ref, cb_ref, mh_ref,
                  mem_ref, c_out_ref, n_out_ref, m_out_ref,
                  c_s, n_s, m_s, tail_s, uext_s, *, chunk):
    L = chunk
    step = pl.program_id(1)

    @pl.when(step == 0)
    def _():
        c_s[...] = c0_ref[0]
        n_s[...] = n0_ref[0]
        m_s[...] = m0_ref[0]
        tail_s[...] = tail0_ref[0]

    u = u_ref[0]
    uext_s[0:8, :] = tail_s[...]
    uext_s[8:8 + L, :] = u
    acc = cb_ref[...] + uext_s[pl.ds(5, L), :] * cw_ref[0:1, :]
    acc = acc + uext_s[pl.ds(6, L), :] * cw_ref[1:2, :]
    acc = acc + uext_s[pl.ds(7, L), :] * cw_ref[2:3, :]
    acc = acc + u * cw_ref[3:4, :]
    tail_s[...] = uext_s[pl.ds(L, 8), :]
    qk = acc * jax.nn.sigmoid(acc)

    gc = gc_ref[0]
    gr = gr_ref[0]
    row = lax.broadcasted_iota(jnp.int32, (L, L), 0)
    col = lax.broadcasted_iota(jnp.int32, (L, L), 1)
    causal = col <= row
    tri = causal.astype(BF16)
    tri_t = (row <= col).astype(BF16)
    lane = lax.broadcasted_iota(jnp.int32, gc.shape, 1)
    srow = lax.broadcasted_iota(jnp.int32, gr.shape, 0)

    def bf16_parts(x):
        hi = x.astype(BF16)
        r1 = x - hi.astype(F32)
        mid = r1.astype(BF16)
        return hi, mid, (r1 - mid.astype(F32)).astype(BF16)

    b_col_all = sum(jnp.dot(tri, p, preferred_element_type=F32)
                    for p in bf16_parts(jnp.where(lane >= H_M, gc, 0.0)))
    b_row_all = sum(jnp.dot(p, tri_t, preferred_element_type=F32)
                    for p in bf16_parts(jnp.where(srow >= H_M, gr, 0.0)))

    v_all = v_ref[0]
    og = og_ref[0]
    for h in range(H_M):
        q = qk[:, h * DK_M:(h + 1) * DK_M]
        k = qk[:, D_M + h * DK_M:D_M + (h + 1) * DK_M] * (DK_M ** -0.5)
        v = v_all[:, h * DK_M:(h + 1) * DK_M]
        li_row = gr[h:h + 1, :]
        li_col = gc[:, h:h + 1]
        b_row = b_row_all[H_M + h:H_M + h + 1, :]
        b_col = b_col_all[:, H_M + h:H_M + h + 1]
        m0 = m_s[h:h + 1, 0:1]
        c0 = c_s[h]
        n0 = n_s[h:h + 1, :]

        d = jnp.where(causal, b_col - b_row + li_row, NEG_INF)
        g_state = b_col + m0
        m = jnp.maximum(jnp.max(d, axis=-1, keepdims=True), g_state)
        w_state = jnp.exp(g_state - m)
        qb = q.astype(BF16)
        kb = k.astype(BF16)
        vb = v.astype(BF16)
        s = lax.dot_general(qb, kb, (((1,), (1,)), ((), ())), preferred_element_type=F32)
        s = s * jnp.exp(d - m)
        num = jnp.dot(s.astype(BF16), vb, preferred_element_type=F32)
        num = num + w_state * jnp.dot(qb, c0.astype(BF16), preferred_element_type=F32)
        den = jnp.sum(s, axis=-1, keepdims=True) + w_state * jnp.sum(q * n0, axis=-1, keepdims=True)
        hh = num / jnp.maximum(jnp.abs(den), jnp.exp(-m))

        b_last = b_col[L - 1:L, :]
        g_tok_row = b_last - b_row + li_row
        m_new = jnp.maximum(b_last + m0, jnp.max(g_tok_row, axis=-1, keepdims=True))
        w_tok_col = jnp.exp(b_last - b_col + li_col - m_new)
        decay = jnp.exp(b_last + m0 - m_new)
        kw = k * w_tok_col
        c_s[h] = decay * c0 + jnp.dot(kw.T.astype(BF16), vb, preferred_element_type=F32)
        n_s[h:h + 1, :] = decay * n0 + jnp.sum(kw, axis=0, keepdims=True)
        m_s[h:h + 1, :] = jnp.broadcast_to(m_new, (1, LANES))

        hn = hh * lax.rsqrt(jnp.mean(hh * hh, axis=-1, keepdims=True) + EPS)
        hn = hn * mh_ref[:, h * DK_M:(h + 1) * DK_M]
        mem_ref[0, :, h * DK_M:(h + 1) * DK_M] = og[:, h * DK_M:(h + 1) * DK_M] * hn

    @pl.when(step == pl.num_programs(1) - 1)
    def _():
        c_out_ref[0] = c_s[...]
        n_out_ref[0] = n_s[...]
        m_out_ref[0] = m_s[...]


def _mlstm(u, v, og, gates_col, gates_row, tail0, c0, n0, m0x, conv_w, conv_b, mh_norm, chunk):
    nb, rows, _ = u.shape
    nc = rows // chunk
    row_map = lambda b, c: (b, c, 0)
    bat3 = lambda b, c: (b, 0, 0)
    const2 = lambda b, c: (0, 0)
    return pl.pallas_call(
        functools.partial(_mlstm_kernel, chunk=chunk),
        grid=(nb, nc),
        in_specs=[
            pl.BlockSpec((1, chunk, 2 * D_M), row_map),
            pl.BlockSpec((1, chunk, D_M), row_map),
            pl.BlockSpec((1, chunk, D_M), row_map),
            pl.BlockSpec((1, chunk, LANES), row_map),
            pl.BlockSpec((1, 8, chunk), lambda b, c: (b, 0, c)),
            pl.BlockSpec((1, 8, 2 * D_M), bat3),
            pl.BlockSpec((1, H_M, DK_M, DK_M), lambda b, c: (b, 0, 0, 0)),
            pl.BlockSpec((1, H_M, DK_M), bat3),
            pl.BlockSpec((1, H_M, LANES), bat3),
            pl.BlockSpec((CONV_W, 2 * D_M), const2),
            pl.BlockSpec((1, 2 * D_M), const2),
            pl.BlockSpec((1, D_M), const2),
        ],
        out_specs=[
            pl.BlockSpec((1, chunk, D_M), row_map),
            pl.BlockSpec((1, H_M, DK_M, DK_M), lambda b, c: (b, 0, 0, 0)),
            pl.BlockSpec((1, H_M, DK_M), bat3),
            pl.BlockSpec((1, H_M, LANES), bat3),
        ],
        out_shape=[
            jax.ShapeDtypeStruct((nb, rows, D_M), F32),
            jax.ShapeDtypeStruct((nb, H_M, DK_M, DK_M), F32),
            jax.ShapeDtypeStruct((nb, H_M, DK_M), F32),
            jax.ShapeDtypeStruct((nb, H_M, LANES), F32),
        ],
        scratch_shapes=[
            pltpu.VMEM((H_M, DK_M, DK_M), F32),
            pltpu.VMEM((H_M, DK_M), F32),
            pltpu.VMEM((H_M, LANES), F32),
            pltpu.VMEM((8, 2 * D_M), F32),
            pltpu.VMEM((chunk + 8, 2 * D_M), F32),
        ],
        compiler_params=_params(("arbitrary", "arbitrary")),
        name="mlstm",
    )(u, v, og, gates_col, gates_row, tail0, c0, n0, m0x,
      conv_w, conv_b.reshape(1, 2 * D_M), mh_norm.reshape(1, D_M))


def _attn_kernel(q_ref, kc_ref, vc_ref, kp_ref, vp_ref, bias_ref, ones_ref, o_ref, l_ref, k_s, v_s, *, ql):
    first = pl.program_id(2) == 0
    k_s[0:QB, :] = kp_ref[0].astype(BF16)
    k_s[QB:QB + ql, :] = kc_ref[0].astype(BF16)
    v_s[0:QB, :] = vp_ref[0].astype(BF16)
    v_s[QB:QB + ql, :] = vc_ref[0].astype(BF16)
    col = lax.broadcasted_iota(jnp.int32, (QB, 2 * QB), 1)
    pad_keys = jnp.logical_and(first, col < QB)
    head0 = lax.broadcasted_iota(jnp.int32, (1, LANES), 1) < HD_A
    nt = (((1,), (1,)), ((), ()))
    for c in range(D_A // LANES):
        sl = slice(c * LANES, (c + 1) * LANES)
        v_all = v_s[:, sl]
        v_lo = jnp.where(head0, v_all, jnp.zeros_like(v_all))
        v_hi = jnp.where(head0, jnp.zeros_like(v_all), v_all)
        for j in range(ql // QB):
            rows = slice(j * QB, (j + 1) * QB)
            keys = slice(j * QB, (j + 2) * QB)
            q = q_ref[0, rows, sl].astype(BF16)
            kb = k_s[keys, sl]
            ps, ms = [], []
            for half in range(2):
                qh = jnp.where(head0, q, jnp.zeros_like(q)) if half == 0 else jnp.where(head0, jnp.zeros_like(q), q)
                logits = lax.dot_general(qh, kb, nt, preferred_element_type=F32) + bias_ref[2 * c + half]
                if j == 0:
                    logits = jnp.where(pad_keys, NEG_INF, logits)
                m = jnp.max(logits, axis=-1, keepdims=True)
                ps.append(jnp.exp(logits - m).astype(BF16))
                ms.append(m)
            p_cat = jnp.concatenate(ps, axis=1)
            v_bd = jnp.concatenate([v_lo[keys], v_hi[keys]], axis=0)
            o = jnp.dot(p_cat, v_bd, preferred_element_type=F32)
            s = jnp.dot(p_cat, ones_ref[...], preferred_element_type=F32)
            o_ref[0, rows, sl] = o / s
            l_ref[0, rows, sl] = jnp.where(head0, ms[0], ms[1]) + jnp.log(s)


def _attn_branch(qa, ka, va, bias, dil):
    nb, ld, _ = qa.shape
    ql = min(ATT_QL, ld)
    cur = lambda b, r, i: (b, i, r)
    prev = lambda b, r, i: (b, jnp.maximum(i * (ql // QB) - 1, 0), r)
    ones_bd = np.zeros((4 * QB, LANES), np.float32)
    ones_bd[:2 * QB, :HD_A] = 1.0
    ones_bd[2 * QB:, HD_A:] = 1.0
    return pl.pallas_call(
        functools.partial(_attn_kernel, ql=ql),
        grid=(nb, dil, ld // ql),
        in_specs=[
            pl.BlockSpec((1, ql, D_A), cur),
            pl.BlockSpec((1, ql, D_A), cur),
            pl.BlockSpec((1, ql, D_A), cur),
            pl.BlockSpec((1, QB, D_A), prev),
            pl.BlockSpec((1, QB, D_A), prev),
            pl.BlockSpec((H_A, QB, 2 * QB), lambda b, r, i: (0, 0, 0)),
            pl.BlockSpec((4 * QB, LANES), lambda b, r, i: (0, 0)),
        ],
        out_specs=[pl.BlockSpec((1, ql, D_A), cur), pl.BlockSpec((1, ql, D_A), cur)],
        out_shape=[jax.ShapeDtypeStruct((nb, ld, dil * D_A), F32)] * 2,
        scratch_shapes=[pltpu.VMEM((QB + ql, D_A), BF16), pltpu.VMEM((QB + ql, D_A), BF16)],
        compiler_params=_params(("arbitrary", "arbitrary", "arbitrary")),
        name=f"attn_d{dil}",
    )(qa, ka, va, ka, va, bias, jnp.asarray(ones_bd, BF16))


def _heads_kernel(k_ref, v_ref, ko_ref, vo_ref):
    for src, dst in ((k_ref, ko_ref), (v_ref, vo_ref)):
        for c in range(D_A // LANES):
            xt = src[0, :, c * LANES:(c + 1) * LANES].T
            dst[0, 2 * c] = xt[:HD_A]
            dst[0, 2 * c + 1] = xt[HD_A:]


def _window_heads(ka, va, keep, tm=512):
    nb, seq, _ = ka.shape
    first = (seq - keep) // tm
    in_spec = pl.BlockSpec((1, tm, D_A), lambda b, i: (b, first + i, 0))
    out_spec = pl.BlockSpec((1, H_A, HD_A, tm), lambda b, i: (b, 0, 0, i))
    kt, vt = pl.pallas_call(
        _heads_kernel,
        grid=(nb, keep // tm),
        in_specs=[in_spec, in_spec],
        out_specs=[out_spec, out_spec],
        out_shape=[jax.ShapeDtypeStruct((nb, H_A, HD_A, keep), F32)] * 2,
        compiler_params=_params(("arbitrary", "arbitrary")),
        name="window_heads",
    )(ka, va)
    return kt.transpose(0, 3, 1, 2), vt.transpose(0, 3, 1, 2)


def _t5_bucket(dist):
    max_exact = N_BUCKETS // 2
    n = jnp.maximum(dist, 1).astype(F32)
    large = max_exact + (jnp.log(n / max_exact) / math.log(MAX_DISTANCE / max_exact)
                         * (N_BUCKETS - max_exact)).astype(jnp.int32)
    large = jnp.minimum(large, N_BUCKETS - 1)
    return jnp.where(dist < max_exact, dist, large)


def _branch_bias_vec(rel_bias, window, dil):
    return rel_bias[_t5_bucket(dil * jnp.arange(window // dil + 1))].astype(F32)


def _prompt_bias(rel_bias, window, dil):
    assert window // dil == QB
    n = 2 * QB
    vec = _branch_bias_vec(rel_bias, window, dil)
    vecp = jnp.concatenate([vec[::-1], jnp.full((n - QB - 1, H_A), NEG_INF, F32)], axis=0)
    g = jnp.roll(vecp, QB - 1, axis=0)
    skew = jnp.tile(g, (QB + 1, 1))[:QB * (n + 1)].reshape(QB, n + 1, H_A)[:, :n]
    return skew[::-1].transpose(2, 0, 1)


TOK_PAD = 8


def _sattn_kernel(q_ref, kt_ref, vt_ref, knt_ref, vnt_ref, bc_ref, bn_ref, att_ref):
    nt = (((1,), (1,)), ((), ()))
    for h in range(H_A):
        q = q_ref[0, h].astype(BF16)
        lc = jnp.dot(q, kt_ref[0, h].astype(BF16), preferred_element_type=F32)
        ln = jnp.dot(q, knt_ref[0, h].astype(BF16), preferred_element_type=F32)
        lcs = [lc + bc_ref[d, h] for d in range(len(BRANCHES))]
        lns = [ln + bn_ref[d, h] for d in range(len(BRANCHES))]
        m = functools.reduce(jnp.maximum, [jnp.max(x, axis=-1, keepdims=True) for x in lcs + lns])
        pc = functools.reduce(jnp.add, [jnp.exp(x - m) for x in lcs])
        pn = functools.reduce(jnp.add, [jnp.exp(x - m) for x in lns])
        s = jnp.sum(pc, axis=-1, keepdims=True) + jnp.sum(pn, axis=-1, keepdims=True)
        o = lax.dot_general(pc.astype(BF16), vt_ref[0, h].astype(BF16), nt, preferred_element_type=F32)
        o = o + lax.dot_general(pn.astype(BF16), vnt_ref[0, h].astype(BF16), nt, preferred_element_type=F32)
        att_ref[0, h] = o / s


def _sample_attention(q_h, kt, vt, knt, vnt, bias_c, bias_n):
    nb, _, _, n_past = kt.shape
    bat = lambda b: (b, 0, 0, 0)
    const = lambda b: (0, 0, 0, 0)
    return pl.pallas_call(
        _sattn_kernel,
        grid=(nb,),
        in_specs=[
            pl.BlockSpec((1, H_A, TOK_PAD, HD_A), bat),
            pl.BlockSpec((1, H_A, HD_A, n_past), bat),
            pl.BlockSpec((1, H_A, HD_A, n_past), bat),
            pl.BlockSpec((1, H_A, HD_A, LANES), bat),
            pl.BlockSpec((1, H_A, HD_A, LANES), bat),
            pl.BlockSpec(bias_c.shape, const),
            pl.BlockSpec(bias_n.shape, const),
        ],
        out_specs=pl.BlockSpec((1, H_A, TOK_PAD, HD_A), bat),
        out_shape=jax.ShapeDtypeStruct((nb, H_A, TOK_PAD, HD_A), F32),
        compiler_params=_params(("arbitrary",)),
        name="sample_attn",
    )(q_h, kt, vt, knt, vnt, bias_c, bias_n)


def _sample_bias(rel_bias, n_tok, n_past):
    ninf = lambda n: jnp.full((n, H_A), NEG_INF, F32)
    cached, new = [], []
    for window, dil in BRANCHES:
        nk = window // dil
        rows = n_past // dil
        assert n_past % dil == 0 and rows >= nk
        vec = _branch_bias_vec(rel_bias, window, dil)
        res = jnp.arange(dil)[None, :, None]
        tc, tn = [], []
        for t in range(n_tok):
            near = t // dil + 1
            col = jnp.concatenate([ninf(rows - (nk + 1 - near)), vec[near:nk + 1][::-1]], axis=0)
            tab = jnp.where(res == t % dil, col[:, None, :], NEG_INF).reshape(n_past, H_A)
            tc.append(tab)
            keys = [vec[(t - j) // dil][None] if (t - j) % dil == 0 and 0 <= t - j <= nk * dil else ninf(1)
                    for j in range(n_tok)]
            tn.append(jnp.concatenate(keys + [ninf(LANES - n_tok)], axis=0))
        pad = lambda tabs: jnp.stack(tabs + [jnp.zeros_like(tabs[0])] * (TOK_PAD - n_tok))
        cached.append(pad(tc).transpose(2, 0, 1))
        new.append(pad(tn).transpose(2, 0, 1))
    return jnp.stack(cached), jnp.stack(new)


def _postmix_kernel(mem_ref, o1_ref, l1_ref, o2_ref, l2_ref, o3_ref, l3_ref, x_ref,
                    g1_ref, sc2_ref, sh2_ref, gpost_ref, gpre_ref, wo_ref, wr_ref, br_ref, cnt0_ref,
                    x1_ref, h2_ref, idx_ref, gate_ref, cnt_ref, cnt_s, *, dilations):
    first_step = jnp.logical_and(pl.program_id(0) == 0, pl.program_id(1) == 0)

    @pl.when(first_step)
    def _():
        cnt_s[...] = cnt0_ref[...]

    rows = lambda ref, dil: ref[0] if dil == 1 else _from_dilated(ref[0], dil)
    o1, o2, o3 = (rows(r, d) for r, d in zip((o1_ref, o2_ref, o3_ref), dilations))
    l1, l2, l3 = (rows(r, d) for r, d in zip((l1_ref, l2_ref, l3_ref), dilations))
    lmax = jnp.maximum(jnp.maximum(l1, l2), l3)
    w1 = jnp.exp(l1 - lmax)
    w2 = jnp.exp(l2 - lmax)
    w3 = jnp.exp(l3 - lmax)
    att = (w1 * o1 + w2 * o2 + w3 * o3) / (w1 + w2 + w3)
    y = jnp.dot(mem_ref[0].astype(BF16), wo_ref[0:D_M, :], preferred_element_type=F32)
    y = y + jnp.dot(att.astype(BF16), wo_ref[D_M:, :], preferred_element_type=F32)
    x1 = x_ref[0] + g1_ref[0] * _rms(y, gpost_ref[...])
    x1_ref[0] = x1
    h2 = _rms(x1, gpre_ref[...]) * (1.0 + sc2_ref[0]) + sh2_ref[0]
    h2_ref[0] = h2
    h_hi = h2.astype(BF16)
    h_lo = (h2 - h_hi.astype(F32)).astype(BF16)
    w_hi = wr_ref[0]
    logits = jnp.dot(h_hi, w_hi, preferred_element_type=F32)
    logits = logits + jnp.dot(h_lo, w_hi, preferred_element_type=F32)
    logits = logits + jnp.dot(h_hi, wr_ref[1], preferred_element_type=F32) + br_ref[...]
    lane = lax.broadcasted_iota(jnp.int32, logits.shape, 1)
    lane_f = lane.astype(F32)
    logits = jnp.where(lane < N_EXPERTS, logits, NEG_INF)
    vals, idxs = [], []
    for _ in range(TOP_K):
        mx = jnp.max(logits, axis=-1, keepdims=True)
        ix = jnp.min(jnp.where(logits == mx, lane_f, float(LANES)), axis=-1, keepdims=True)
        vals.append(mx)
        idxs.append(ix)
        logits = jnp.where(lane_f == ix, NEG_INF, logits)
    es = [jnp.exp(v - vals[0]) for v in vals]
    tot = es[0] + es[1] + es[2] + es[3]
    tm = logits.shape[0]
    picks = [(lane_f == ix).astype(F32) for ix in idxs]
    chosen = picks[0] + picks[1] + picks[2] + picks[3]
    r_i = lax.broadcasted_iota(jnp.int32, (tm, tm), 0)
    c_i = lax.broadcasted_iota(jnp.int32, (tm, tm), 1)
    earlier = jnp.dot((c_i < r_i).astype(BF16), chosen.astype(BF16), preferred_element_type=F32)
    before = earlier + cnt_s[...]
    cnt_s[...] = cnt_s[...] + jnp.sum(chosen, axis=0, keepdims=True)
    cnt_ref[...] = cnt_s[...]
    idx_tile = jnp.zeros(logits.shape, F32)
    gate_tile = jnp.zeros(logits.shape, F32)
    for k in range(TOP_K):
        rank = jnp.sum(picks[k] * before, axis=-1, keepdims=True)
        idx_tile = jnp.where(lane == k, idxs[k], idx_tile)
        idx_tile = jnp.where(lane == TOP_K + k, rank, idx_tile)
        gate_tile = jnp.where(lane == k, es[k] / tot, gate_tile)
    idx_ref[0] = idx_tile.astype(jnp.int32)
    gate_ref[0] = gate_tile


def _postmix(mem, branches, x3, gate1, scale2, shift2, g_post, g_pre, w_out_b, w_router_p, b_router_p, counts0, tm,
             dilations=(1, 1, 1)):
    nb, rows, _ = x3.shape
    per_row_mod = gate1.shape[1] != 1
    mod_block = (1, tm, D_MODEL) if per_row_mod else (1, 1, D_MODEL)
    mod_map = (lambda b, i: (b, i, 0)) if per_row_mod else (lambda b, i: (b, 0, 0))
    row_map = lambda b, i: (b, i, 0)
    const2 = lambda b, i: (0, 0)
    half = pl.BlockSpec((1, tm, D_M), row_map)
    flat = [a for pair in branches for a in pair]
    branch_specs = [pl.BlockSpec((1, tm // d, d * D_A), row_map) for d in dilations for _ in range(2)]
    return pl.pallas_call(
        functools.partial(_postmix_kernel, dilations=tuple(dilations)),
        grid=(nb, rows // tm),
        in_specs=[half] + branch_specs + [
            pl.BlockSpec((1, tm, D_MODEL), row_map),
            pl.BlockSpec(mod_block, mod_map),
            pl.BlockSpec(mod_block, mod_map),
            pl.BlockSpec(mod_block, mod_map),
            pl.BlockSpec((1, D_MODEL), const2),
            pl.BlockSpec((1, D_MODEL), const2),
            pl.BlockSpec((D_MODEL, D_MODEL), const2),
            pl.BlockSpec((2, D_MODEL, LANES), lambda b, i: (0, 0, 0)),
            pl.BlockSpec((1, LANES), const2),
            pl.BlockSpec((1, LANES), const2),
        ],
        out_specs=[
            pl.BlockSpec((1, tm, D_MODEL), row_map),
            pl.BlockSpec((1, tm, D_MODEL), row_map),
            pl.BlockSpec((1, tm, LANES), row_map),
            pl.BlockSpec((1, tm, LANES), row_map),
            pl.BlockSpec((1, LANES), const2),
        ],
        out_shape=[
            jax.ShapeDtypeStruct((nb, rows, D_MODEL), F32),
            jax.ShapeDtypeStruct((nb, rows, D_MODEL), F32),
            jax.ShapeDtypeStruct((nb, rows, LANES), jnp.int32),
            jax.ShapeDtypeStruct((nb, rows, LANES), F32),
            jax.ShapeDtypeStruct((1, LANES), F32),
        ],
        scratch_shapes=[pltpu.VMEM((1, LANES), F32)],
        compiler_params=_params(("arbitrary", "arbitrary")),
        name="postmix",
    )(mem, *flat, x3, gate1, scale2, shift2, g_post.reshape(1, D_MODEL), g_pre.reshape(1, D_MODEL),
      w_out_b, w_router_p, b_router_p, counts0)


def _row_wait(src_rows, dst_rows, sem):
    pltpu.make_async_copy(src_rows, dst_rows, sem).wait()


def _scatter_kernel(slot_ref, hp_ref, hs_ref, xs_hbm, buf, sem, *, n_prompt_tiles):
    h = jnp.where(pl.program_id(0) >= n_prompt_tiles, hs_ref[...], hp_ref[...])
    bits = pltpu.bitcast(h.astype(BF16).astype(F32), U32)
    packed = (bits[:, :PACK_W] & U32(HI_MASK)) | (bits[:, PACK_W:] >> 16)
    buf[...] = pltpu.einshape("m(jc)->mjc", packed, j=PACK_T)

    def issue(r, carry):
        src = buf.at[pl.ds(r, 1)]
        for k in range(TOP_K):
            pltpu.make_async_copy(src, xs_hbm.at[pl.ds(slot_ref[0, 0, r * TOP_K + k], 1)],
                                  sem).start(priority=k % 2)
        return carry

    lax.fori_loop(0, TOK_TILE, issue, 0, unroll=4)
    for _ in range(TOP_K):
        _row_wait(buf, xs_hbm.at[pl.ds(0, TOK_TILE)], sem)


def _moe_scatter(h_p, h_s, slot_tiles):
    tiles = slot_tiles.shape[0]
    n_p = h_p.shape[0] // TOK_TILE
    assert tiles == n_p + 1 and h_s.shape[0] == TOK_TILE
    n_rows = tiles * TOK_TILE * TOP_K
    return pl.pallas_call(
        functools.partial(_scatter_kernel, n_prompt_tiles=n_p),
        grid=(tiles,),
        in_specs=[
            pl.BlockSpec((1, 1, TOK_TILE * TOP_K), lambda i: (i, 0, 0), memory_space=pltpu.SMEM),
            pl.BlockSpec((TOK_TILE, D_MODEL), lambda i: (jnp.minimum(i, n_p - 1), 0)),
            pl.BlockSpec((TOK_TILE, D_MODEL), lambda i: (0, 0)),
        ],
        out_specs=pl.BlockSpec(memory_space=pl.ANY),
        out_shape=jax.ShapeDtypeStruct((n_rows, PACK_T, LANES), U32),
        scratch_shapes=[pltpu.VMEM((TOK_TILE, PACK_T, LANES), U32), pltpu.SemaphoreType.DMA],
        compiler_params=_params(("arbitrary",)),
        name="moe_scatter",
    )(slot_tiles, h_p, h_s)


def _expert_kernel(tile_ref, exp_ref, next_ref, lo_ref, hi_ref, first_ref, nvis_ref,
                   xs_ref, w1_hbm, b1_ref, w2_hbm, b2_ref, ys_ref, w1f, w2f, w1b, w2b, sem):
    v = pl.program_id(0)
    active = v < nvis_ref[0]
    changed = jnp.logical_or(v == 0, exp_ref[v] != exp_ref[jnp.maximum(v - 1, 0)])
    lo = lo_ref[v]
    hi = hi_ref[v]

    def weight_copies(e):
        return (pltpu.make_async_copy(w1_hbm.at[e], w1f, sem.at[0]),
                pltpu.make_async_copy(w2_hbm.at[e], w2f, sem.at[1]))

    @pl.when(v == 0)
    def _():
        for c in weight_copies(exp_ref[0]):
            c.start()

    @pl.when(jnp.logical_and(active, changed))
    def _():
        for c in weight_copies(exp_ref[v]):
            c.wait()
        w1b[...] = w1f[...].astype(BF16)
        w2b[...] = w2f[...].astype(BF16)

        @pl.when(next_ref[v] >= 0)
        def _():
            for c in weight_copies(next_ref[v]):
                c.start()

    def mlp_rows(r0):
        w = pltpu.einshape("mjc->m(jc)", xs_ref[r0:r0 + MOE_SUB])
        x_hi = pltpu.bitcast(w & U32(HI_MASK), F32).astype(BF16)
        x_lo = pltpu.bitcast(w << 16, F32).astype(BF16)
        hgu = jnp.dot(x_hi, w1b[0:PACK_W, :], preferred_element_type=F32)
        hgu = hgu + jnp.dot(x_lo, w1b[PACK_W:, :], preferred_element_type=F32) + b1_ref[0]
        x_glu = jnp.minimum(hgu[:, :D_FF], SWIGLU_LIMIT)
        x_lin = jnp.clip(hgu[:, D_FF:], -SWIGLU_LIMIT, SWIGLU_LIMIT)
        act = x_glu * jax.nn.sigmoid(SWIGLU_ALPHA * x_glu) * (x_lin + 1.0)
        y = jnp.dot(act.astype(BF16), w2b[...], preferred_element_type=F32) + b2_ref[0]
        return pltpu.einshape("m(jc)->mjc", y, j=ROW_T)

    subs = [r * MOE_SUB for r in range(MOE_M // MOE_SUB)]
    whole = jnp.logical_and(lo == 0, hi == MOE_M)

    @pl.when(jnp.logical_and(active, whole))
    def _():
        for r0 in subs:
            ys_ref[r0:r0 + MOE_SUB] = mlp_rows(r0)

    shared = jnp.logical_and(active, jnp.logical_not(whole))

    @pl.when(jnp.logical_and(shared, first_ref[v] == 1))
    def _():
        ys_ref[...] = jnp.zeros(ys_ref.shape, F32)

    for r0 in subs:
        @pl.when(jnp.logical_and(shared, jnp.logical_and(lo < r0 + MOE_SUB, hi > r0)))
        def _():
            rows = r0 + lax.broadcasted_iota(jnp.int32, (MOE_SUB, 1, 1), 0)
            mine = jnp.logical_and(rows >= lo, rows < hi)
            ys_ref[r0:r0 + MOE_SUB] = jnp.where(mine, mlp_rows(r0), ys_ref[r0:r0 + MOE_SUB])


def _moe_experts(xs, visits, w1, b1, w2, b2):
    n_rows = xs.shape[0]
    n_vis = visits[0].shape[0]
    tile_map = lambda v, tile, *_: (tile[v], 0, 0)
    exp_map = lambda v, tile, exp, *_: (exp[v], 0, 0)
    return pl.pallas_call(
        _expert_kernel,
        grid_spec=pltpu.PrefetchScalarGridSpec(
            num_scalar_prefetch=7,
            grid=(n_vis,),
            in_specs=[
                pl.BlockSpec((MOE_M, PACK_T, LANES), tile_map),
                pl.BlockSpec(memory_space=pl.ANY),
                pl.BlockSpec((1, 1, 2 * D_FF), exp_map),
                pl.BlockSpec(memory_space=pl.ANY),
                pl.BlockSpec((1, 1, D_MODEL), exp_map),
            ],
            out_specs=pl.BlockSpec((MOE_M, ROW_T, LANES), tile_map),
            scratch_shapes=[pltpu.VMEM((D_MODEL, 2 * D_FF), F32), pltpu.VMEM((D_FF, D_MODEL), F32),
                            pltpu.VMEM((D_MODEL, 2 * D_FF), BF16), pltpu.VMEM((D_FF, D_MODEL), BF16),
                            pltpu.SemaphoreType.DMA((2,))],
        ),
        out_shape=jax.ShapeDtypeStruct((n_rows, ROW_T, LANES), F32),
        compiler_params=_params(("arbitrary",)),
        name="moe_experts",
    )(*visits, xs, w1, b1.reshape(N_EXPERTS, 1, 2 * D_FF), w2, b2.reshape(N_EXPERTS, 1, D_MODEL))


def _combine_kernel(slot_ref, next_ref, gatep_ref, gates_ref, x1p_ref, x1s_ref, g2p_ref, g2s_ref, gpost_ref, ys_hbm,
                    yp_ref, ysm_ref, buf, sem, *, n_prompt_tiles):
    i = pl.program_id(0)
    last = pl.num_programs(0) - 1
    cur = lax.rem(i, 2)

    def gather(slots, b):
        def issue(r, carry):
            for k in range(TOP_K):
                pltpu.make_async_copy(ys_hbm.at[pl.ds(slots[0, 0, r * TOP_K + k], 1)],
                                      buf.at[b, pl.ds(k * TOK_TILE + r, 1)], sem.at[b]).start(priority=k % 2)
            return carry

        lax.fori_loop(0, TOK_TILE, issue, 0, unroll=4)

    @pl.when(i == 0)
    def _():
        gather(slot_ref, cur)

    @pl.when(i < last)
    def _():
        gather(next_ref, 1 - cur)

    for k in range(TOP_K):
        _row_wait(ys_hbm.at[pl.ds(0, TOK_TILE)], buf.at[cur, pl.ds(k * TOK_TILE, TOK_TILE)], sem.at[cur])
    is_sample = i >= n_prompt_tiles
    g = jnp.where(is_sample, gates_ref[...], gatep_ref[...])
    f = jnp.zeros((TOK_TILE, D_MODEL), F32)
    for k in range(TOP_K):
        rows = pltpu.einshape("mjc->m(jc)", buf[cur, pl.ds(k * TOK_TILE, TOK_TILE)])
        f = f + rows * g[:, k:k + 1]
    x1 = jnp.where(is_sample, x1s_ref[...], x1p_ref[...])
    g2 = jnp.where(is_sample, g2s_ref[...], g2p_ref[0])
    y = x1 + g2 * _rms(f, gpost_ref[...])

    @pl.when(jnp.logical_not(is_sample))
    def _():
        yp_ref[...] = y

    @pl.when(is_sample)
    def _():
        ysm_ref[...] = y


def _moe_combine(ys, slot_tiles, gates_p, gates_s, x1_p, x1_s, gate2_p, gate2_s, g_post, rows_per_batch):
    tiles = slot_tiles.shape[0]
    n_p = x1_p.shape[0] // TOK_TILE
    assert tiles == n_p + 1 and x1_s.shape[0] == TOK_TILE
    tiles_per_batch = rows_per_batch // TOK_TILE
    p_tile = lambda i: (jnp.minimum(i, n_p - 1), 0)
    const2 = lambda i: (0, 0)
    return pl.pallas_call(
        functools.partial(_combine_kernel, n_prompt_tiles=n_p),
        grid=(tiles,),
        in_specs=[
            pl.BlockSpec((1, 1, TOK_TILE * TOP_K), lambda i: (i, 0, 0), memory_space=pltpu.SMEM),
            pl.BlockSpec((1, 1, TOK_TILE * TOP_K), lambda i: (jnp.minimum(i + 1, tiles - 1), 0, 0),
                         memory_space=pltpu.SMEM),
            pl.BlockSpec((TOK_TILE, LANES), p_tile),
            pl.BlockSpec((TOK_TILE, LANES), const2),
            pl.BlockSpec((TOK_TILE, D_MODEL), p_tile),
            pl.BlockSpec((TOK_TILE, D_MODEL), const2),
            pl.BlockSpec((1, 1, D_MODEL), lambda i: (jnp.minimum(i, n_p - 1) // tiles_per_batch, 0, 0)),
            pl.BlockSpec((TOK_TILE, D_MODEL), const2),
            pl.BlockSpec((1, D_MODEL), const2),
            pl.BlockSpec(memory_space=pl.ANY),
        ],
        out_specs=[pl.BlockSpec((TOK_TILE, D_MODEL), p_tile), pl.BlockSpec((TOK_TILE, D_MODEL), const2)],
        out_shape=[jax.ShapeDtypeStruct(x1_p.shape, F32), jax.ShapeDtypeStruct(x1_s.shape, F32)],
        scratch_shapes=[pltpu.VMEM((2, TOP_K * TOK_TILE, ROW_T, LANES), F32), pltpu.SemaphoreType.DMA((2,))],
        compiler_params=_params(("arbitrary",)),
        name="moe_combine",
    )(slot_tiles, slot_tiles, gates_p, gates_s, x1_p, x1_s, gate2_p, gate2_s, g_post.reshape(1, D_MODEL), ys)


def _moe_plan(routes, counts):
    a = sum(r.shape[0] for r in routes) * TOP_K
    assert a % MOE_M == 0
    g_end = jnp.cumsum(counts)
    g_start = g_end - counts

    def slots(route):
        flat_e = route[:, :TOP_K].reshape(-1)
        rank = route[:, TOP_K:2 * TOP_K].reshape(-1)
        onehot = flat_e[:, None] == jnp.arange(N_EXPERTS)[None, :]
        return (jnp.sum(jnp.where(onehot, g_start[None, :], 0), axis=1) + rank).astype(jnp.int32)

    slot_of = jnp.concatenate([slots(r) for r in routes])

    n_tiles = a // MOE_M
    n_vis = n_tiles + N_EXPERTS - 1
    first_tile = g_start // MOE_M
    last_tile = jnp.where(counts > 0, (g_end - 1) // MOE_M, first_tile)
    nvis_e = jnp.where(counts > 0, last_tile - first_tile + 1, 0)
    v_end = jnp.cumsum(nvis_e)
    v_start = v_end - nvis_e
    total = v_end[-1]
    v = jnp.arange(n_vis)
    vc = jnp.minimum(v, total - 1)
    e_of = jnp.minimum(jnp.sum((vc[:, None] >= v_end[None, :]).astype(jnp.int32), axis=1), N_EXPERTS - 1)
    is_e = e_of[:, None] == jnp.arange(N_EXPERTS)[None, :]
    of_e = lambda t: jnp.sum(jnp.where(is_e, t[None, :], 0), axis=1)
    tile_of = of_e(first_tile) + (vc - of_e(v_start))
    lo = jnp.maximum(of_e(g_start), tile_of * MOE_M) - tile_of * MOE_M
    hi = jnp.minimum(of_e(g_end), (tile_of + 1) * MOE_M) - tile_of * MOE_M
    prev_tile = jnp.concatenate([jnp.full((1,), -1, tile_of.dtype), tile_of[:-1]])
    first = (tile_of != prev_tile).astype(jnp.int32)
    experts = jnp.arange(N_EXPERTS)
    later = (experts[None, :] > experts[:, None]) & (counts[None, :] > 0)
    next_e = jnp.min(jnp.where(later, experts[None, :], N_EXPERTS), axis=1)
    next_e = jnp.where(next_e < N_EXPERTS, next_e, -1)
    i32 = lambda t: t.astype(jnp.int32)
    visits = (i32(tile_of), i32(e_of), i32(of_e(next_e)), i32(lo), i32(hi), first, i32(total).reshape(1))
    return slot_of, visits


def _reorder_w_in(w_in):
    off_om_end = 4 * D_M
    off_g_end = off_om_end + 2 * H_M
    w = jnp.concatenate([
        w_in[:, :off_om_end],
        w_in[:, off_g_end:],
        w_in[:, off_om_end:off_g_end],
        jnp.zeros((D_MODEL, LANES - 2 * H_M), w_in.dtype),
    ], axis=1)
    return w.astype(BF16)


def kernel(x_prompt, x_sample, cache_k, cache_v, state_conv, state_C, state_n, state_m, c_prompt, c_sample, w_ada, b_ada, g_pre_mix, g_post_mix, g_pre_ffn, g_post_ffn, w_in, b_ig, b_fg, conv_w, conv_b, mh_norm, rel_bias, w_out, w_router, b_router, w1, b1, w2, b2):
    depth = w_in.shape[0]
    assert depth == 1
    l = 0
    nb_p, seq, _ = x_prompt.shape
    nb_s, n_tok, _ = x_sample.shape
    rows_s = nb_s * n_tok
    assert rows_s == TOK_TILE

    n_c = nb_p + nb_s
    c_pad = -(-n_c // 8) * 8
    c_all = jnp.concatenate([c_prompt, c_sample, jnp.zeros((c_pad - n_c, D_MODEL), F32)], axis=0)
    mod = _adaln(c_all, w_ada[l], b_ada[l])
    mods = [mod[:, i * D_MODEL:(i + 1) * D_MODEL] for i in range(6)]
    mods_p = [m[:nb_p].reshape(nb_p, 1, D_MODEL) for m in mods]
    mods_s = [jnp.repeat(m[nb_p:n_c], n_tok, axis=0).reshape(1, rows_s, D_MODEL) for m in mods]

    w_r = _reorder_w_in(w_in[l])
    bias_g = jnp.concatenate([b_ig[l], b_fg[l], jnp.zeros((LANES - 2 * H_M,), F32)]).reshape(1, LANES)
    w_out_b = w_out[l].astype(BF16)
    w_router_f = jnp.concatenate([w_router[l], jnp.zeros((D_MODEL, LANES - N_EXPERTS), F32)], axis=1)
    w_router_hi = w_router_f.astype(BF16)
    w_router_p = jnp.stack([w_router_hi, (w_router_f - w_router_hi.astype(F32)).astype(BF16)])
    b_router_p = jnp.concatenate([b_router[l].astype(F32), jnp.zeros((LANES - N_EXPERTS,), F32)]).reshape(1, LANES)

    shift1, scale1, gate1, shift2, scale2, gate2_p = mods_p
    dils = tuple(d for _, d in BRANCHES)
    assert dils[0] == 1
    u, v_m, og, gts, qa, ka, va, *dilated = _inproj(x_prompt, g_pre_mix[l], scale1, shift1, w_r, bias_g, tm=512,
                                                   dilations=dils[1:])
    qkv_d = [(qa, ka, va)] + [tuple(dilated[3 * n:3 * n + 3]) for n in range(len(dils) - 1)]
    gates_row = gts[:, :, :8].transpose(0, 2, 1)
    zeros = lambda *s: jnp.zeros(s, F32)
    mem, c_p, n_p, m_p = _mlstm(u, v_m, og, gts, gates_row, zeros(nb_p, 8, 2 * D_M),
                                zeros(nb_p, H_M, DK_M, DK_M), zeros(nb_p, H_M, DK_M), zeros(nb_p, H_M, LANES),
                                conv_w[l], conv_b[l], mh_norm[l], MLSTM_L_PROMPT)
    branches = [_attn_branch(*qkv, _prompt_bias(rel_bias, w, d), d) for qkv, (w, d) in zip(qkv_d, BRANCHES)]
    x1_p, h2_p, idx_p, gate_p, cnt_p = _postmix(mem, branches, x_prompt, gate1, scale2, shift2,
                                                g_post_mix[l], g_pre_ffn[l], w_out_b, w_router_p, b_router_p,
                                                zeros(1, LANES), tm=256, dilations=dils)
    keep = min(WINDOW_MAX, seq)
    k_p, v_p = _window_heads(ka, va, keep)
    conv_p = u[:, seq - (CONV_W - 1):]

    shift1, scale1, gate1, shift2, scale2, gate2_s = mods_s
    x_s = x_sample.reshape(1, rows_s, D_MODEL)
    u_s, v_s, og_s, gts_s, qa_s, ka_s, va_s = _inproj(x_s, g_pre_mix[l], scale1, shift1, w_r, bias_g, tm=rows_s)
    per_b = lambda t: t.reshape(nb_s, n_tok, t.shape[-1])
    ls = MLSTM_L_SAMPLE
    pad_rows = lambda t, n: jnp.concatenate([t, jnp.zeros((nb_s, n - t.shape[1], t.shape[2]), t.dtype)], axis=1)
    inert = jnp.concatenate([jnp.full((H_M,), NEG_INF, F32), jnp.zeros((LANES - H_M,), F32)])
    gts_pad = jnp.concatenate([per_b(gts_s), jnp.broadcast_to(inert, (nb_s, ls - n_tok, LANES))], axis=1)
    tail0 = jnp.concatenate([zeros(nb_s, 8 - (CONV_W - 1), 2 * D_M), state_conv[l].astype(F32)], axis=1)
    m0x = jnp.broadcast_to(state_m[l].astype(F32)[:, :, None], (nb_s, H_M, LANES))
    mem_s, c_s, n_s, m_s = _mlstm(pad_rows(per_b(u_s), ls), pad_rows(per_b(v_s), ls), pad_rows(per_b(og_s), ls),
                                  gts_pad, gts_pad[:, :, :8].transpose(0, 2, 1), tail0,
                                  state_C[l].astype(F32), state_n[l].astype(F32), m0x,
                                  conv_w[l], conv_b[l], mh_norm[l], ls)
    mem_s = mem_s[:, :n_tok].reshape(1, rows_s, D_M)
    heads = lambda t: t.reshape(nb_s, n_tok, H_A, HD_A)
    k_new, v_new = heads(ka_s), heads(va_s)
    n_past = cache_k.shape[2]
    assert n_tok <= TOK_PAD
    q_h = jnp.pad(heads(qa_s).transpose(0, 2, 1, 3), ((0, 0), (0, 0), (0, TOK_PAD - n_tok), (0, 0)))
    new_t = lambda t: jnp.pad(t.transpose(0, 2, 3, 1), ((0, 0), (0, 0), (0, 0), (0, LANES - n_tok)))
    cache_t = lambda c: c[l].astype(F32).transpose(0, 2, 3, 1)
    att_s = _sample_attention(q_h, cache_t(cache_k), cache_t(cache_v), new_t(k_new), new_t(v_new),
                              *_sample_bias(rel_bias, n_tok, n_past))
    att_s = att_s[:, :, :n_tok].transpose(0, 2, 1, 3).reshape(1, rows_s, D_A)
    zero_o = zeros(1, rows_s, D_A)
    ninf = jnp.full((1, rows_s, D_A), NEG_INF, F32)
    branches_s = [(att_s, zero_o), (zero_o, ninf), (zero_o, ninf)]
    x1_s, h2_s, idx_s, gate_s, cnt_all = _postmix(mem_s, branches_s, x_s, gate1, scale2, shift2,
                                                  g_post_mix[l], g_pre_ffn[l], w_out_b, w_router_p, b_router_p,
                                                  cnt_p, tm=rows_s)

    t_p = nb_p * seq
    slot_of, visits = _moe_plan([idx_p.reshape(t_p, LANES), idx_s.reshape(rows_s, LANES)],
                                cnt_all[0, :N_EXPERTS].astype(jnp.int32))
    slot_tiles = slot_of.reshape((t_p + rows_s) // TOK_TILE, 1, TOK_TILE * TOP_K)
    xs = _moe_scatter(h2_p.reshape(t_p, D_MODEL), h2_s.reshape(rows_s, D_MODEL), slot_tiles)
    ys = _moe_experts(xs, visits, w1[l], b1[l], w2[l], b2[l])
    y_p, y_s = _moe_combine(ys, slot_tiles, gate_p.reshape(t_p, LANES), gate_s.reshape(rows_s, LANES), x1_p.reshape(t_p, D_MODEL), x1_s.reshape(rows_s, D_MODEL),
                            gate2_p, gate2_s.reshape(rows_s, D_MODEL), g_post_ffn[l], seq)
    y_prompt = y_p.reshape(nb_p, seq, D_MODEL)
    y_sample = y_s.reshape(nb_s, n_tok, D_MODEL)

    st = lambda t: t[None]
    conv_s = jnp.concatenate([state_conv[l].astype(F32), per_b(u_s)], axis=1)[:, -(CONV_W - 1):]
    return (y_prompt, y_sample, st(k_p), st(v_p), st(conv_p), st(c_p), st(n_p), st(m_p[:, :, 0]),
            st(k_new), st(v_new), st(conv_s), st(c_s), st(n_s), st(m_s[:, :, 0]))
```

```python
import functools
import math

import jax
import jax.numpy as jnp
import numpy as np
from jax import lax
from jax.experimental import pallas as pl
from jax.experimental.pallas import tpu as pltpu

F32 = jnp.float32
BF16 = jnp.bfloat16
U32 = jnp.uint32
NEG_INF = float("-inf")

D_MODEL = 1024
H_M = 4
D_M = 512
DK_M = 128
CONV_W = 4
H_A = 8
D_A = 512
HD_A = 64
BRANCHES = ((128, 1), (512, 4), (2048, 16))
WINDOW_MAX = 2048
QB = 128
N_BUCKETS = 32
MAX_DISTANCE = 2048
N_EXPERTS = 32
TOP_K = 4
D_FF = 1024
SWIGLU_LIMIT = 7.0
SWIGLU_ALPHA = 1.702
EPS = 1e-6

LANES = 128
C_U, C_V, C_O, C_QA, C_KA, C_VA, C_G = 0, 1024, 1536, 2048, 2560, 3072, 3584
D_IN_R = C_G + LANES

MLSTM_L_PROMPT = 256
MLSTM_L_SAMPLE = 32
ATT_QL = 512
MOE_M = 512
MOE_SUB = 256
TOK_TILE = 128
PACK_W = D_MODEL // 2
PACK_T = PACK_W // LANES
ROW_T = D_MODEL // LANES
HI_MASK = 0xFFFF0000
VMEM_LIMIT = 56 * 1024 * 1024


def _params(sem, vmem=None):
    return pltpu.CompilerParams(dimension_semantics=sem, vmem_limit_bytes=vmem or VMEM_LIMIT)


def _rms(x, g):
    return x * lax.rsqrt(jnp.mean(x * x, axis=-1, keepdims=True) + EPS) * g


def _log_sigmoid(t):
    return jnp.minimum(t, 0.0) - jnp.log1p(jnp.exp(-jnp.abs(t)))


def _ada_kernel(c_ref, w_ref, b_ref, o_ref):
    c = c_ref[...]
    a = (c * jax.nn.sigmoid(c)).astype(BF16)
    o_ref[...] = jnp.dot(a, w_ref[...].astype(BF16), preferred_element_type=F32) + b_ref[...]


def _adaln(c_all, w_ada, b_ada):
    rows = c_all.shape[0]
    n = w_ada.shape[1]
    tn = 1024
    return pl.pallas_call(
        _ada_kernel,
        grid=(n // tn,),
        in_specs=[
            pl.BlockSpec((rows, D_MODEL), lambda j: (0, 0)),
            pl.BlockSpec((D_MODEL, tn), lambda j: (0, j)),
            pl.BlockSpec((1, tn), lambda j: (0, j)),
        ],
        out_specs=pl.BlockSpec((rows, tn), lambda j: (0, j)),
        out_shape=jax.ShapeDtypeStruct((rows, n), F32),
        compiler_params=_params(("arbitrary",)),
        name="adaln",
    )(c_all, w_ada, b_ada.reshape(1, n))


def _to_dilated(x, dil):
    return pltpu.einshape("(lr)c->l(rc)", x, r=dil)


def _from_dilated(x, dil):
    return pltpu.einshape("l(rc)->(lr)c", x, r=dil)


def _inproj_kernel(x_ref, g_ref, sc_ref, sh_ref, w_ref, bg_ref,
                   u_ref, v_ref, o_ref, gt_ref, qa_ref, ka_ref, va_ref, *dil_refs, dilations):
    x = x_ref[0]
    h = _rms(x, g_ref[...]) * (1.0 + sc_ref[0]) + sh_ref[0]
    hb = h.astype(BF16)

    def seg(lo, width):
        return jnp.dot(hb, w_ref[:, lo:lo + width], preferred_element_type=F32)

    u_ref[0] = seg(C_U, 2 * D_M)
    v_ref[0] = seg(C_V, D_M)
    o_ref[0] = jax.nn.sigmoid(seg(C_O, D_M))
    qkv = (seg(C_QA, D_A) * (HD_A ** -0.5), seg(C_KA, D_A), seg(C_VA, D_A))
    for ref, val in zip((qa_ref, ka_ref, va_ref), qkv):
        ref[0] = val
    for n, dil in enumerate(dilations):
        for j, val in enumerate(qkv):
            dil_refs[3 * n + j][0] = _to_dilated(val, dil)
    t = seg(C_G, LANES) + bg_ref[...]
    lane = lax.broadcasted_iota(jnp.int32, t.shape, 1)
    gt_ref[0] = jnp.where(lane < H_M, t, _log_sigmoid(t))


def _inproj(x3, g_pre, scale, shift, w_r, bias_g, tm, dilations=()):
    nb, rows, _ = x3.shape
    per_row_mod = scale.shape[1] != 1
    mod_block = (1, tm, D_MODEL) if per_row_mod else (1, 1, D_MODEL)
    mod_map = (lambda b, i: (b, i, 0)) if per_row_mod else (lambda b, i: (b, 0, 0))
    row_map = lambda b, i: (b, i, 0)
    widths = (2 * D_M, D_M, D_M, LANES, D_A, D_A, D_A)
    dil_specs = [pl.BlockSpec((1, tm // d, d * D_A), row_map) for d in dilations for _ in range(3)]
    dil_shapes = [jax.ShapeDtypeStruct((nb, rows // d, d * D_A), F32) for d in dilations for _ in range(3)]
    return pl.pallas_call(
        functools.partial(_inproj_kernel, dilations=tuple(dilations)),
        grid=(nb, rows // tm),
        in_specs=[
            pl.BlockSpec((1, tm, D_MODEL), row_map),
            pl.BlockSpec((1, D_MODEL), lambda b, i: (0, 0)),
            pl.BlockSpec(mod_block, mod_map),
            pl.BlockSpec(mod_block, mod_map),
            pl.BlockSpec((D_MODEL, D_IN_R), lambda b, i: (0, 0)),
            pl.BlockSpec((1, LANES), lambda b, i: (0, 0)),
        ],
        out_specs=[pl.BlockSpec((1, tm, w), row_map) for w in widths] + dil_specs,
        out_shape=[jax.ShapeDtypeStruct((nb, rows, w), F32) for w in widths] + dil_shapes,
        compiler_params=_params(("arbitrary", "arbitrary")),
        name="inproj",
    )(x3, g_pre.reshape(1, D_MODEL), scale, shift, w_r, bias_g)


def _mlstm_kernel(u_ref, v_ref, og_ref, gc_ref, gr_ref, tail0_ref, c0_ref, n0_ref, m0_ref,
                  cw_ref, cb_ref, mh_ref,
                  mem_ref, c_out_ref, n_out_ref, m_out_ref,
                  c_s, n_s, m_s, tail_s, uext_s, *, chunk):
    L = chunk
    step = pl.program_id(1)

    @pl.when(step == 0)
    def _():
        c_s[...] = c0_ref[0]
        n_s[...] = n0_ref[0]
        m_s[...] = m0_ref[0]
        tail_s[...] = tail0_ref[0]

    u = u_ref[0]
    uext_s[0:8, :] = tail_s[...]
    uext_s[8:8 + L, :] = u
    acc = cb_ref[...] + uext_s[pl.ds(5, L), :] * cw_ref[0:1, :]
    acc = acc + uext_s[pl.ds(6, L), :] * cw_ref[1:2, :]
    acc = acc + uext_s[pl.ds(7, L), :] * cw_ref[2:3, :]
    acc = acc + u * cw_ref[3:4, :]
    tail_s[...] = uext_s[pl.ds(L, 8), :]
    qk = acc * jax.nn.sigmoid(acc)

    gc = gc_ref[0]
    gr = gr_ref[0]
    row = lax.broadcasted_iota(jnp.int32, (L, L), 0)
    col = lax.broadcasted_iota(jnp.int32, (L, L), 1)
    causal = col <= row
    tri = causal.astype(BF16)
    tri_t = (row <= col).astype(BF16)
    lane = lax.broadcasted_iota(jnp.int32, gc.shape, 1)
    srow = lax.broadcasted_iota(jnp.int32, gr.shape, 0)

    def bf16_parts(x):
        hi = x.astype(BF16)
        r1 = x - hi.astype(F32)
        mid = r1.astype(BF16)
        return hi, mid, (r1 - mid.astype(F32)).astype(BF16)

    b_col_all = sum(jnp.dot(tri, p, preferred_element_type=F32)
                    for p in bf16_parts(jnp.where(lane >= H_M, gc, 0.0)))
    b_row_all = sum(jnp.dot(p, tri_t, preferred_element_type=F32)
                    for p in bf16_parts(jnp.where(srow >= H_M, gr, 0.0)))

    v_all = v_ref[0]
    og = og_ref[0]
    for h in range(H_M):
        q = qk[:, h * DK_M:(h + 1) * DK_M]
        k = qk[:, D_M + h * DK_M:D_M + (h + 1) * DK_M] * (DK_M ** -0.5)
        v = v_all[:, h * DK_M:(h + 1) * DK_M]
        li_row = gr[h:h + 1, :]
        li_col = gc[:, h:h + 1]
        b_row = b_row_all[H_M + h:H_M + h + 1, :]
        b_col = b_col_all[:, H_M + h:H_M + h + 1]
        m0 = m_s[h:h + 1, 0:1]
        c0 = c_s[h]
        n0 = n_s[h:h + 1, :]

        d = jnp.where(causal, b_col - b_row + li_row, NEG_INF)
        g_state = b_col + m0
        m = jnp.maximum(jnp.max(d, axis=-1, keepdims=True), g_state)
        w_state = jnp.exp(g_state - m)
        qb = q.astype(BF16)
        kb = k.astype(BF16)
        vb = v.astype(BF16)
        s = lax.dot_general(qb, kb, (((1,), (1,)), ((), ())), preferred_element_type=F32)
        s = s * jnp.exp(d - m)
        num = jnp.dot(s.astype(BF16), vb, preferred_element_type=F32)
        num = num + w_state * jnp.dot(qb, c0.astype(BF16), preferred_element_type=F32)
        den = jnp.sum(s, axis=-1, keepdims=True) + w_state * jnp.sum(q * n0, axis=-1, keepdims=True)
        hh = num / jnp.maximum(jnp.abs(den), jnp.exp(-m))

        b_last = b_col[L - 1:L, :]
        g_tok_row = b_last - b_row + li_row
        m_new = jnp.maximum(b_last + m0, jnp.max(g_tok_row, axis=-1, keepdims=True))
        w_tok_col = jnp.exp(b_last - b_col + li_col - m_new)
        decay = jnp.exp(b_last + m0 - m_new)
        kw = k * w_tok_col
        c_s[h] = decay * c0 + jnp.dot(kw.T.astype(BF16), vb, preferred_element_type=F32)
        n_s[h:h + 1, :] = decay * n0 + jnp.sum(kw, axis=0, keepdims=True)
        m_s[h:h + 1, :] = jnp.broadcast_to(m_new, (1, LANES))

        hn = hh * lax.rsqrt(jnp.mean(hh * hh, axis=-1, keepdims=True) + EPS)
        hn = hn * mh_ref[:, h * DK_M:(h + 1) * DK_M]
        mem_ref[0, :, h * DK_M:(h + 1) * DK_M] = og[:, h * DK_M:(h + 1) * DK_M] * hn

    @pl.when(step == pl.num_programs(1) - 1)
    def _():
        c_out_ref[0] = c_s[...]
        n_out_ref[0] = n_s[...]
        m_out_ref[0] = m_s[...]


def _mlstm(u, v, og, gates_col, gates_row, tail0, c0, n0, m0x, conv_w, conv_b, mh_norm, chunk):
    nb, rows, _ = u.shape
    nc = rows // chunk
    row_map = lambda b, c: (b, c, 0)
    bat3 = lambda b, c: (b, 0, 0)
    const2 = lambda b, c: (0, 0)
    return pl.pallas_call(
        functools.partial(_mlstm_kernel, chunk=chunk),
        grid=(nb, nc),
        in_specs=[
            pl.BlockSpec((1, chunk, 2 * D_M), row_map),
            pl.BlockSpec((1, chunk, D_M), row_map),
            pl.BlockSpec((1, chunk, D_M), row_map),
            pl.BlockSpec((1, chunk, LANES), row_map),
            pl.BlockSpec((1, 8, chunk), lambda b, c: (b, 0, c)),
            pl.BlockSpec((1, 8, 2 * D_M), bat3),
            pl.BlockSpec((1, H_M, DK_M, DK_M), lambda b, c: (b, 0, 0, 0)),
            pl.BlockSpec((1, H_M, DK_M), bat3),
            pl.BlockSpec((1, H_M, LANES), bat3),
            pl.BlockSpec((CONV_W, 2 * D_M), const2),
            pl.BlockSpec((1, 2 * D_M), const2),
            pl.BlockSpec((1, D_M), const2),
        ],
        out_specs=[
            pl.BlockSpec((1, chunk, D_M), row_map),
            pl.BlockSpec((1, H_M, DK_M, DK_M), lambda b, c: (b, 0, 0, 0)),
            pl.BlockSpec((1, H_M, DK_M), bat3),
            pl.BlockSpec((1, H_M, LANES), bat3),
        ],
        out_shape=[
            jax.ShapeDtypeStruct((nb, rows, D_M), F32),
            jax.ShapeDtypeStruct((nb, H_M, DK_M, DK_M), F32),
            jax.ShapeDtypeStruct((nb, H_M, DK_M), F32),
            jax.ShapeDtypeStruct((nb, H_M, LANES), F32),
        ],
        scratch_shapes=[
            pltpu.VMEM((H_M, DK_M, DK_M), F32),
            pltpu.VMEM((H_M, DK_M), F32),
            pltpu.VMEM((H_M, LANES), F32),
            pltpu.VMEM((8, 2 * D_M), F32),
            pltpu.VMEM((chunk + 8, 2 * D_M), F32),
        ],
        compiler_params=_params(("arbitrary", "arbitrary")),
        name="mlstm",
    )(u, v, og, gates_col, gates_row, tail0, c0, n0, m0x,
      conv_w, conv_b.reshape(1, 2 * D_M), mh_norm.reshape(1, D_M))


def _attn_kernel(q_ref, kc_ref, vc_ref, kp_ref, vp_ref, bias_ref, ones_ref, o_ref, l_ref, k_s, v_s, *, ql):
    first = pl.program_id(2) == 0
    k_s[0:QB, :] = kp_ref[0].astype(BF16)
    k_s[QB:QB + ql, :] = kc_ref[0].astype(BF16)
    v_s[0:QB, :] = vp_ref[0].astype(BF16)
    v_s[QB:QB + ql, :] = vc_ref[0].astype(BF16)
    col = lax.broadcasted_iota(jnp.int32, (QB, 2 * QB), 1)
    pad_keys = jnp.logical_and(first, col < QB)
    head0 = lax.broadcasted_iota(jnp.int32, (1, LANES), 1) < HD_A
    n_slab = D_A // LANES
    pad_all = jnp.concatenate([pad_keys] * H_A, axis=0)

    def raw_logits(c, j):
        sl = slice(c * LANES, (c + 1) * LANES)
        q = q_ref[0, j * QB:(j + 1) * QB, sl].astype(BF16)
        zero = jnp.zeros_like(q)
        q2 = jnp.concatenate([jnp.where(head0, q, zero), jnp.where(head0, zero, q)], axis=0)
        return lax.dot_general(q2, k_s[j * QB:(j + 2) * QB, sl], (((1,), (1,)), ((), ())),
                               preferred_element_type=F32)

    def softmax(js, logits):
        bias = bias_ref[...].reshape(H_A * QB, 2 * QB)
        logits = logits + jnp.concatenate([bias] * len(js), axis=0)
        if js[0] == 0:
            first_rows = lax.broadcasted_iota(jnp.int32, logits.shape, 0) < H_A * QB
            logits = jnp.where(jnp.logical_and(first_rows, jnp.concatenate([pad_all] * len(js), axis=0)),
                               NEG_INF, logits)
        m = jnp.max(logits, axis=-1, keepdims=True)
        return jnp.exp(logits - m).astype(BF16), m

    def finish(c, j, p, m):
        sl = slice(c * LANES, (c + 1) * LANES)
        rows = slice(j * QB, (j + 1) * QB)
        p_cat = jnp.concatenate([p[:QB], p[QB:]], axis=1)
        v_all = v_s[j * QB:(j + 2) * QB, sl]
        v_bd = jnp.concatenate([jnp.where(head0, v_all, jnp.zeros_like(v_all)),
                                jnp.where(head0, jnp.zeros_like(v_all), v_all)], axis=0)
        o = jnp.dot(p_cat, v_bd, preferred_element_type=F32)
        s = jnp.dot(p_cat, ones_ref[...], preferred_element_type=F32)
        o_ref[0, rows, sl] = o / s
        l_ref[0, rows, sl] = jnp.where(head0, m[:QB], m[QB:]) + jnp.log(s)

    group = 1
    for j0 in range(0, ql // QB, group):
        js = list(range(j0, min(j0 + group, ql // QB)))
        p, m = softmax(js, jnp.concatenate([raw_logits(c, j) for j in js for c in range(n_slab)], axis=0))
        for n, (j, c) in enumerate((j, c) for j in js for c in range(n_slab)):
            part = slice(2 * n * QB, 2 * (n + 1) * QB)
            finish(c, j, p[part], m[part])


def _attn_branch(qa, ka, va, bias, dil):
    nb, ld, _ = qa.shape
    ql = min(ATT_QL, ld)
    cur = lambda b, r, i: (b, i, r)
    prev = lambda b, r, i: (b, jnp.maximum(i * (ql // QB) - 1, 0), r)
    ones_bd = np.zeros((4 * QB, LANES), np.float32)
    ones_bd[:2 * QB, :HD_A] = 1.0
    ones_bd[2 * QB:, HD_A:] = 1.0
    return pl.pallas_call(
        functools.partial(_attn_kernel, ql=ql),
        grid=(nb, dil, ld // ql),
        in_specs=[
            pl.BlockSpec((1, ql, D_A), cur),
            pl.BlockSpec((1, ql, D_A), cur),
            pl.BlockSpec((1, ql, D_A), cur),
            pl.BlockSpec((1, QB, D_A), prev),
            pl.BlockSpec((1, QB, D_A), prev),
            pl.BlockSpec((H_A, QB, 2 * QB), lambda b, r, i: (0, 0, 0)),
            pl.BlockSpec((4 * QB, LANES), lambda b, r, i: (0, 0)),
        ],
        out_specs=[pl.BlockSpec((1, ql, D_A), cur), pl.BlockSpec((1, ql, D_A), cur)],
        out_shape=[jax.ShapeDtypeStruct((nb, ld, dil * D_A), F32)] * 2,
        scratch_shapes=[pltpu.VMEM((QB + ql, D_A), BF16), pltpu.VMEM((QB + ql, D_A), BF16)],
        compiler_params=_params(("arbitrary", "arbitrary", "arbitrary")),
        name=f"attn_d{dil}",
    )(qa, ka, va, ka, va, bias, jnp.asarray(ones_bd, BF16))


def _heads_kernel(k_ref, v_ref, ko_ref, vo_ref):
    for src, dst in ((k_ref, ko_ref), (v_ref, vo_ref)):
        for c in range(D_A // LANES):
            xt = src[0, :, c * LANES:(c + 1) * LANES].T
            dst[0, 2 * c] = xt[:HD_A]
            dst[0, 2 * c + 1] = xt[HD_A:]


def _window_heads(ka, va, keep, tm=512):
    nb, seq, _ = ka.shape
    first = (seq - keep) // tm
    in_spec = pl.BlockSpec((1, tm, D_A), lambda b, i: (b, first + i, 0))
    out_spec = pl.BlockSpec((1, H_A, HD_A, tm), lambda b, i: (b, 0, 0, i))
    kt, vt = pl.pallas_call(
        _heads_kernel,
        grid=(nb, keep // tm),
        in_specs=[in_spec, in_spec],
        out_specs=[out_spec, out_spec],
        out_shape=[jax.ShapeDtypeStruct((nb, H_A, HD_A, keep), F32)] * 2,
        compiler_params=_params(("arbitrary", "arbitrary")),
        name="window_heads",
    )(ka, va)
    return kt.transpose(0, 3, 1, 2), vt.transpose(0, 3, 1, 2)


def _t5_bucket(dist):
    max_exact = N_BUCKETS // 2
    n = jnp.maximum(dist, 1).astype(F32)
    large = max_exact + (jnp.log(n / max_exact) / math.log(MAX_DISTANCE / max_exact)
                         * (N_BUCKETS - max_exact)).astype(jnp.int32)
    large = jnp.minimum(large, N_BUCKETS - 1)
    return jnp.where(dist < max_exact, dist, large)


def _branch_bias_vec(rel_bias, window, dil):
    return rel_bias[_t5_bucket(dil * jnp.arange(window // dil + 1))].astype(F32)


def _prompt_bias(rel_bias, window, dil):
    assert window // dil == QB
    n = 2 * QB
    vec = _branch_bias_vec(rel_bias, window, dil)
    vecp = jnp.concatenate([vec[::-1], jnp.full((n - QB - 1, H_A), NEG_INF, F32)], axis=0)
    g = jnp.roll(vecp, QB - 1, axis=0)
    skew = jnp.tile(g, (QB + 1, 1))[:QB * (n + 1)].reshape(QB, n + 1, H_A)[:, :n]
    return skew[::-1].transpose(2, 0, 1)


TOK_PAD = 8


def _sattn_kernel(q_ref, kt_ref, vt_ref, knt_ref, vnt_ref, bc_ref, bn_ref, att_ref):
    nt = (((1,), (1,)), ((), ()))
    for h in range(H_A):
        q = q_ref[0, h].astype(BF16)
        lc = jnp.dot(q, kt_ref[0, h].astype(BF16), preferred_element_type=F32)
        ln = jnp.dot(q, knt_ref[0, h].astype(BF16), preferred_element_type=F32)
        lcs = [lc + bc_ref[d, h] for d in range(len(BRANCHES))]
        lns = [ln + bn_ref[d, h] for d in range(len(BRANCHES))]
        m = functools.reduce(jnp.maximum, [jnp.max(x, axis=-1, keepdims=True) for x in lcs + lns])
        pc = functools.reduce(jnp.add, [jnp.exp(x - m) for x in lcs])
        pn = functools.reduce(jnp.add, [jnp.exp(x - m) for x in lns])
        s = jnp.sum(pc, axis=-1, keepdims=True) + jnp.sum(pn, axis=-1, keepdims=True)
        o = lax.dot_general(pc.astype(BF16), vt_ref[0, h].astype(BF16), nt, preferred_element_type=F32)
        o = o + lax.dot_general(pn.astype(BF16), vnt_ref[0, h].astype(BF16), nt, preferred_element_type=F32)
        att_ref[0, h] = o / s


def _sample_attention(q_h, kt, vt, knt, vnt, bias_c, bias_n):
    nb, _, _, n_past = kt.shape
    bat = lambda b: (b, 0, 0, 0)
    const = lambda b: (0, 0, 0, 0)
    return pl.pallas_call(
        _sattn_kernel,
        grid=(nb,),
        in_specs=[
            pl.BlockSpec((1, H_A, TOK_PAD, HD_A), bat),
            pl.BlockSpec((1, H_A, HD_A, n_past), bat),
            pl.BlockSpec((1, H_A, HD_A, n_past), bat),
            pl.BlockSpec((1, H_A, HD_A, LANES), bat),
            pl.BlockSpec((1, H_A, HD_A, LANES), bat),
            pl.BlockSpec(bias_c.shape, const),
            pl.BlockSpec(bias_n.shape, const),
        ],
        out_specs=pl.BlockSpec((1, H_A, TOK_PAD, HD_A), bat),
        out_shape=jax.ShapeDtypeStruct((nb, H_A, TOK_PAD, HD_A), F32),
        compiler_params=_params(("arbitrary",)),
        name="sample_attn",
    )(q_h, kt, vt, knt, vnt, bias_c, bias_n)


def _sample_bias(rel_bias, n_tok, n_past):
    ninf = lambda n: jnp.full((n, H_A), NEG_INF, F32)
    cached, new = [], []
    for window, dil in BRANCHES:
        nk = window // dil
        rows = n_past // dil
        assert n_past % dil == 0 and rows >= nk
        vec = _branch_bias_vec(rel_bias, window, dil)
        res = jnp.arange(dil)[None, :, None]
        tc, tn = [], []
        for t in range(n_tok):
            near = t // dil + 1
            col = jnp.concatenate([ninf(rows - (nk + 1 - near)), vec[near:nk + 1][::-1]], axis=0)
            tab = jnp.where(res == t % dil, col[:, None, :], NEG_INF).reshape(n_past, H_A)
            tc.append(tab)
            keys = [vec[(t - j) // dil][None] if (t - j) % dil == 0 and 0 <= t - j <= nk * dil else ninf(1)
                    for j in range(n_tok)]
            tn.append(jnp.concatenate(keys + [ninf(LANES - n_tok)], axis=0))
        pad = lambda tabs: jnp.stack(tabs + [jnp.zeros_like(tabs[0])] * (TOK_PAD - n_tok))
        cached.append(pad(tc).transpose(2, 0, 1))
        new.append(pad(tn).transpose(2, 0, 1))
    return jnp.stack(cached), jnp.stack(new)


def _postmix_kernel(mem_ref, o1_ref, l1_ref, o2_ref, l2_ref, o3_ref, l3_ref, x_ref,
                    g1_ref, sc2_ref, sh2_ref, gpost_ref, gpre_ref, wo_ref, wr_ref, br_ref, cnt0_ref,
                    x1_ref, h2_ref, idx_ref, gate_ref, cnt_ref, cnt_s, *, dilations):
    first_step = jnp.logical_and(pl.program_id(0) == 0, pl.program_id(1) == 0)

    @pl.when(first_step)
    def _():
        cnt_s[...] = cnt0_ref[...]

    rows = lambda ref, dil: ref[0] if dil == 1 else _from_dilated(ref[0], dil)
    o1, o2, o3 = (rows(r, d) for r, d in zip((o1_ref, o2_ref, o3_ref), dilations))
    l1, l2, l3 = (rows(r, d) for r, d in zip((l1_ref, l2_ref, l3_ref), dilations))
    lmax = jnp.maximum(jnp.maximum(l1, l2), l3)
    w1 = jnp.exp(l1 - lmax)
    w2 = jnp.exp(l2 - lmax)
    w3 = jnp.exp(l3 - lmax)
    att = (w1 * o1 + w2 * o2 + w3 * o3) / (w1 + w2 + w3)
    y = jnp.dot(mem_ref[0].astype(BF16), wo_ref[0:D_M, :], preferred_element_type=F32)
    y = y + jnp.dot(att.astype(BF16), wo_ref[D_M:, :], preferred_element_type=F32)
    x1 = x_ref[0] + g1_ref[0] * _rms(y, gpost_ref[...])
    x1_ref[0] = x1
    h2 = _rms(x1, gpre_ref[...]) * (1.0 + sc2_ref[0]) + sh2_ref[0]
    h2_ref[0] = h2
    h_hi = h2.astype(BF16)
    h_lo = (h2 - h_hi.astype(F32)).astype(BF16)
    w_hi = wr_ref[0]
    logits = jnp.dot(h_hi, w_hi, preferred_element_type=F32)
    logits = logits + jnp.dot(h_lo, w_hi, preferred_element_type=F32)
    logits = logits + jnp.dot(h_hi, wr_ref[1], preferred_element_type=F32) + br_ref[...]
    lane = lax.broadcasted_iota(jnp.int32, logits.shape, 1)
    lane_f = lane.astype(F32)
    logits = jnp.where(lane < N_EXPERTS, logits, NEG_INF)
    vals, idxs = [], []
    for _ in range(TOP_K):
        mx = jnp.max(logits, axis=-1, keepdims=True)
        ix = jnp.min(jnp.where(logits == mx, lane_f, float(LANES)), axis=-1, keepdims=True)
        vals.append(mx)
        idxs.append(ix)
        logits = jnp.where(lane_f == ix, NEG_INF, logits)
    es = [jnp.exp(v - vals[0]) for v in vals]
    tot = es[0] + es[1] + es[2] + es[3]
    tm = logits.shape[0]
    picks = [(lane_f == ix).astype(F32) for ix in idxs]
    chosen = picks[0] + picks[1] + picks[2] + picks[3]
    r_i = lax.broadcasted_iota(jnp.int32, (tm, tm), 0)
    c_i = lax.broadcasted_iota(jnp.int32, (tm, tm), 1)
    earlier = jnp.dot((c_i < r_i).astype(BF16), chosen.astype(BF16), preferred_element_type=F32)
    before = earlier + cnt_s[...]
    cnt_s[...] = cnt_s[...] + jnp.sum(chosen, axis=0, keepdims=True)
    cnt_ref[...] = cnt_s[...]
    idx_tile = jnp.zeros(logits.shape, F32)
    gate_tile = jnp.zeros(logits.shape, F32)
    for k in range(TOP_K):
        rank = jnp.sum(picks[k] * before, axis=-1, keepdims=True)
        idx_tile = jnp.where(lane == k, idxs[k], idx_tile)
        idx_tile = jnp.where(lane == TOP_K + k, rank, idx_tile)
        gate_tile = jnp.where(lane == k, es[k] / tot, gate_tile)
    idx_ref[0] = idx_tile.astype(jnp.int32)
    gate_ref[0] = gate_tile


def _postmix(mem, branches, x3, gate1, scale2, shift2, g_post, g_pre, w_out_b, w_router_p, b_router_p, counts0, tm,
             dilations=(1, 1, 1)):
    nb, rows, _ = x3.shape
    per_row_mod = gate1.shape[1] != 1
    mod_block = (1, tm, D_MODEL) if per_row_mod else (1, 1, D_MODEL)
    mod_map = (lambda b, i: (b, i, 0)) if per_row_mod else (lambda b, i: (b, 0, 0))
    row_map = lambda b, i: (b, i, 0)
    const2 = lambda b, i: (0, 0)
    half = pl.BlockSpec((1, tm, D_M), row_map)
    flat = [a for pair in branches for a in pair]
    branch_specs = [pl.BlockSpec((1, tm // d, d * D_A), row_map) for d in dilations for _ in range(2)]
    return pl.pallas_call(
        functools.partial(_postmix_kernel, dilations=tuple(dilations)),
        grid=(nb, rows // tm),
        in_specs=[half] + branch_specs + [
            pl.BlockSpec((1, tm, D_MODEL), row_map),
            pl.BlockSpec(mod_block, mod_map),
            pl.BlockSpec(mod_block, mod_map),
            pl.BlockSpec(mod_block, mod_map),
            pl.BlockSpec((1, D_MODEL), const2),
            pl.BlockSpec((1, D_MODEL), const2),
            pl.BlockSpec((D_MODEL, D_MODEL), const2),
            pl.BlockSpec((2, D_MODEL, LANES), lambda b, i: (0, 0, 0)),
            pl.BlockSpec((1, LANES), const2),
            pl.BlockSpec((1, LANES), const2),
        ],
        out_specs=[
            pl.BlockSpec((1, tm, D_MODEL), row_map),
            pl.BlockSpec((1, tm, D_MODEL), row_map),
            pl.BlockSpec((1, tm, LANES), row_map),
            pl.BlockSpec((1, tm, LANES), row_map),
            pl.BlockSpec((1, LANES), const2),
        ],
        out_shape=[
            jax.ShapeDtypeStruct((nb, rows, D_MODEL), F32),
            jax.ShapeDtypeStruct((nb, rows, D_MODEL), F32),
            jax.ShapeDtypeStruct((nb, rows, LANES), jnp.int32),
            jax.ShapeDtypeStruct((nb, rows, LANES), F32),
            jax.ShapeDtypeStruct((1, LANES), F32),
        ],
        scratch_shapes=[pltpu.VMEM((1, LANES), F32)],
        compiler_params=_params(("arbitrary", "arbitrary")),
        name="postmix",
    )(mem, *flat, x3, gate1, scale2, shift2, g_post.reshape(1, D_MODEL), g_pre.reshape(1, D_MODEL),
      w_out_b, w_router_p, b_router_p, counts0)


def _row_wait(src_rows, dst_rows, sem):
    pltpu.make_async_copy(src_rows, dst_rows, sem).wait()


def _scatter_kernel(slot_ref, hp_ref, hs_ref, xs_hbm, buf, sem, *, n_prompt_tiles):
    h = jnp.where(pl.program_id(0) >= n_prompt_tiles, hs_ref[...], hp_ref[...])
    bits = pltpu.bitcast(h.astype(BF16).astype(F32), U32)
    packed = (bits[:, :PACK_W] & U32(HI_MASK)) | (bits[:, PACK_W:] >> 16)
    buf[...] = pltpu.einshape("m(jc)->mjc", packed, j=PACK_T)

    def issue(r, carry):
        src = buf.at[pl.ds(r, 1)]
        for k in range(TOP_K):
            pltpu.make_async_copy(src, xs_hbm.at[pl.ds(slot_ref[0, 0, r * TOP_K + k], 1)],
                                  sem).start(priority=k % 2)
        return carry

    lax.fori_loop(0, TOK_TILE, issue, 0, unroll=4)
    for _ in range(TOP_K):
        _row_wait(buf, xs_hbm.at[pl.ds(0, TOK_TILE)], sem)


def _moe_scatter(h_p, h_s, slot_tiles):
    tiles = slot_tiles.shape[0]
    n_p = h_p.shape[0] // TOK_TILE
    assert tiles == n_p + 1 and h_s.shape[0] == TOK_TILE
    n_rows = tiles * TOK_TILE * TOP_K
    return pl.pallas_call(
        functools.partial(_scatter_kernel, n_prompt_tiles=n_p),
        grid=(tiles,),
        in_specs=[
            pl.BlockSpec((1, 1, TOK_TILE * TOP_K), lambda i: (i, 0, 0), memory_space=pltpu.SMEM),
            pl.BlockSpec((TOK_TILE, D_MODEL), lambda i: (jnp.minimum(i, n_p - 1), 0)),
            pl.BlockSpec((TOK_TILE, D_MODEL), lambda i: (0, 0)),
        ],
        out_specs=pl.BlockSpec(memory_space=pl.ANY),
        out_shape=jax.ShapeDtypeStruct((n_rows, PACK_T, LANES), U32),
        scratch_shapes=[pltpu.VMEM((TOK_TILE, PACK_T, LANES), U32), pltpu.SemaphoreType.DMA],
        compiler_params=_params(("arbitrary",)),
        name="moe_scatter",
    )(slot_tiles, h_p, h_s)


def _expert_kernel(tile_ref, exp_ref, next_ref, lo_ref, hi_ref, first_ref, nvis_ref,
                   xs_ref, w1_hbm, b1_ref, w2_hbm, b2_ref, ys_ref, w1f, w2f, w1b, w2b, sem):
    v = pl.program_id(0)
    active = v < nvis_ref[0]
    changed = jnp.logical_or(v == 0, exp_ref[v] != exp_ref[jnp.maximum(v - 1, 0)])
    lo = lo_ref[v]
    hi = hi_ref[v]

    def weight_copies(e):
        return (pltpu.make_async_copy(w1_hbm.at[e], w1f, sem.at[0]),
                pltpu.make_async_copy(w2_hbm.at[e], w2f, sem.at[1]))

    @pl.when(v == 0)
    def _():
        for c in weight_copies(exp_ref[0]):
            c.start()

    @pl.when(jnp.logical_and(active, changed))
    def _():
        for c in weight_copies(exp_ref[v]):
            c.wait()
        w1b[...] = w1f[...].astype(BF16)
        w2b[...] = w2f[...].astype(BF16)

        @pl.when(next_ref[v] >= 0)
        def _():
            for c in weight_copies(next_ref[v]):
                c.start()

    def mlp_rows(r0):
        w = pltpu.einshape("mjc->m(jc)", xs_ref[r0:r0 + MOE_SUB])
        x_hi = pltpu.bitcast(w & U32(HI_MASK), F32).astype(BF16)
        x_lo = pltpu.bitcast(w << 16, F32).astype(BF16)
        hgu = jnp.dot(x_hi, w1b[0:PACK_W, :], preferred_element_type=F32)
        hgu = hgu + jnp.dot(x_lo, w1b[PACK_W:, :], preferred_element_type=F32) + b1_ref[0]
        x_glu = jnp.minimum(hgu[:, :D_FF], SWIGLU_LIMIT)
        x_lin = jnp.clip(hgu[:, D_FF:], -SWIGLU_LIMIT, SWIGLU_LIMIT)
        act = x_glu * jax.nn.sigmoid(SWIGLU_ALPHA * x_glu) * (x_lin + 1.0)
        y = jnp.dot(act.astype(BF16), w2b[...], preferred_element_type=F32) + b2_ref[0]
        return pltpu.einshape("m(jc)->mjc", y, j=ROW_T)

    subs = [r * MOE_SUB for r in range(MOE_M // MOE_SUB)]
    whole = jnp.logical_and(lo == 0, hi == MOE_M)

    @pl.when(jnp.logical_and(active, whole))
    def _():
        for r0 in subs:
            ys_ref[r0:r0 + MOE_SUB] = mlp_rows(r0)

    shared = jnp.logical_and(active, jnp.logical_not(whole))

    @pl.when(jnp.logical_and(shared, first_ref[v] == 1))
    def _():
        ys_ref[...] = jnp.zeros(ys_ref.shape, F32)

    for r0 in subs:
        @pl.when(jnp.logical_and(shared, jnp.logical_and(lo < r0 + MOE_SUB, hi > r0)))
        def _():
            rows = r0 + lax.broadcasted_iota(jnp.int32, (MOE_SUB, 1, 1), 0)
            mine = jnp.logical_and(rows >= lo, rows < hi)
            ys_ref[r0:r0 + MOE_SUB] = jnp.where(mine, mlp_rows(r0), ys_ref[r0:r0 + MOE_SUB])


def _moe_experts(xs, visits, w1, b1, w2, b2):
    n_rows = xs.shape[0]
    n_vis = visits[0].shape[0]
    tile_map = lambda v, tile, *_: (tile[v], 0, 0)
    exp_map = lambda v, tile, exp, *_: (exp[v], 0, 0)
    return pl.pallas_call(
        _expert_kernel,
        grid_spec=pltpu.PrefetchScalarGridSpec(
            num_scalar_prefetch=7,
            grid=(n_vis,),
            in_specs=[
                pl.BlockSpec((MOE_M, PACK_T, LANES), tile_map),
                pl.BlockSpec(memory_space=pl.ANY),
                pl.BlockSpec((1, 1, 2 * D_FF), exp_map),
                pl.BlockSpec(memory_space=pl.ANY),
                pl.BlockSpec((1, 1, D_MODEL), exp_map),
            ],
            out_specs=pl.BlockSpec((MOE_M, ROW_T, LANES), tile_map),
            scratch_shapes=[pltpu.VMEM((D_MODEL, 2 * D_FF), F32), pltpu.VMEM((D_FF, D_MODEL), F32),
                            pltpu.VMEM((D_MODEL, 2 * D_FF), BF16), pltpu.VMEM((D_FF, D_MODEL), BF16),
                            pltpu.SemaphoreType.DMA((2,))],
        ),
        out_shape=jax.ShapeDtypeStruct((n_rows, ROW_T, LANES), F32),
        compiler_params=_params(("arbitrary",)),
        name="moe_experts",
    )(*visits, xs, w1, b1.reshape(N_EXPERTS, 1, 2 * D_FF), w2, b2.reshape(N_EXPERTS, 1, D_MODEL))


def _combine_kernel(slot_ref, next_ref, gatep_ref, gates_ref, x1p_ref, x1s_ref, g2p_ref, g2s_ref, gpost_ref, ys_hbm,
                    yp_ref, ysm_ref, buf, sem, *, n_prompt_tiles):
    i = pl.program_id(0)
    last = pl.num_programs(0) - 1
    cur = lax.rem(i, 2)

    def gather(slots, b):
        def issue(r, carry):
            for k in range(TOP_K):
                pltpu.make_async_copy(ys_hbm.at[pl.ds(slots[0, 0, r * TOP_K + k], 1)],
                                      buf.at[b, pl.ds(k * TOK_TILE + r, 1)], sem.at[b]).start(priority=k % 2)
            return carry

        lax.fori_loop(0, TOK_TILE, issue, 0, unroll=4)

    @pl.when(i == 0)
    def _():
        gather(slot_ref, cur)

    @pl.when(i < last)
    def _():
        gather(next_ref, 1 - cur)

    for k in range(TOP_K):
        _row_wait(ys_hbm.at[pl.ds(0, TOK_TILE)], buf.at[cur, pl.ds(k * TOK_TILE, TOK_TILE)], sem.at[cur])
    is_sample = i >= n_prompt_tiles
    g = jnp.where(is_sample, gates_ref[...], gatep_ref[...])
    f = jnp.zeros((TOK_TILE, D_MODEL), F32)
    for k in range(TOP_K):
        rows = pltpu.einshape("mjc->m(jc)", buf[cur, pl.ds(k * TOK_TILE, TOK_TILE)])
        f = f + rows * g[:, k:k + 1]
    x1 = jnp.where(is_sample, x1s_ref[...], x1p_ref[...])
    g2 = jnp.where(is_sample, g2s_ref[...], g2p_ref[0])
    y = x1 + g2 * _rms(f, gpost_ref[...])

    @pl.when(jnp.logical_not(is_sample))
    def _():
        yp_ref[...] = y

    @pl.when(is_sample)
    def _():
        ysm_ref[...] = y


def _moe_combine(ys, slot_tiles, gates_p, gates_s, x1_p, x1_s, gate2_p, gate2_s, g_post, rows_per_batch):
    tiles = slot_tiles.shape[0]
    n_p = x1_p.shape[0] // TOK_TILE
    assert tiles == n_p + 1 and x1_s.shape[0] == TOK_TILE
    tiles_per_batch = rows_per_batch // TOK_TILE
    p_tile = lambda i: (jnp.minimum(i, n_p - 1), 0)
    const2 = lambda i: (0, 0)
    return pl.pallas_call(
        functools.partial(_combine_kernel, n_prompt_tiles=n_p),
        grid=(tiles,),
        in_specs=[
            pl.BlockSpec((1, 1, TOK_TILE * TOP_K), lambda i: (i, 0, 0), memory_space=pltpu.SMEM),
            pl.BlockSpec((1, 1, TOK_TILE * TOP_K), lambda i: (jnp.minimum(i + 1, tiles - 1), 0, 0),
                         memory_space=pltpu.SMEM),
            pl.BlockSpec((TOK_TILE, LANES), p_tile),
            pl.BlockSpec((TOK_TILE, LANES), const2),
            pl.BlockSpec((TOK_TILE, D_MODEL), p_tile),
            pl.BlockSpec((TOK_TILE, D_MODEL), const2),
            pl.BlockSpec((1, 1, D_MODEL), lambda i: (jnp.minimum(i, n_p - 1) // tiles_per_batch, 0, 0)),
            pl.BlockSpec((TOK_TILE, D_MODEL), const2),
            pl.BlockSpec((1, D_MODEL), const2),
            pl.BlockSpec(memory_space=pl.ANY),
        ],
        out_specs=[pl.BlockSpec((TOK_TILE, D_MODEL), p_tile), pl.BlockSpec((TOK_TILE, D_MODEL), const2)],
        out_shape=[jax.ShapeDtypeStruct(x1_p.shape, F32), jax.ShapeDtypeStruct(x1_s.shape, F32)],
        scratch_shapes=[pltpu.VMEM((2, TOP_K * TOK_TILE, ROW_T, LANES), F32), pltpu.SemaphoreType.DMA((2,))],
        compiler_params=_params(("arbitrary",)),
        name="moe_combine",
    )(slot_tiles, slot_tiles, gates_p, gates_s, x1_p, x1_s, gate2_p, gate2_s, g_post.reshape(1, D_MODEL), ys)


def _moe_plan(routes, counts):
    a = sum(r.shape[0] for r in routes) * TOP_K
    assert a % MOE_M == 0
    g_end = jnp.cumsum(counts)
    g_start = g_end - counts

    def slots(route):
        flat_e = route[:, :TOP_K].reshape(-1)
        rank = route[:, TOP_K:2 * TOP_K].reshape(-1)
        onehot = flat_e[:, None] == jnp.arange(N_EXPERTS)[None, :]
        return (jnp.sum(jnp.where(onehot, g_start[None, :], 0), axis=1) + rank).astype(jnp.int32)

    slot_of = jnp.concatenate([slots(r) for r in routes])

    n_tiles = a // MOE_M
    n_vis = n_tiles + N_EXPERTS - 1
    first_tile = g_start // MOE_M
    last_tile = jnp.where(counts > 0, (g_end - 1) // MOE_M, first_tile)
    nvis_e = jnp.where(counts > 0, last_tile - first_tile + 1, 0)
    v_end = jnp.cumsum(nvis_e)
    v_start = v_end - nvis_e
    total = v_end[-1]
    v = jnp.arange(n_vis)
    vc = jnp.minimum(v, total - 1)
    e_of = jnp.minimum(jnp.sum((vc[:, None] >= v_end[None, :]).astype(jnp.int32), axis=1), N_EXPERTS - 1)
    is_e = e_of[:, None] == jnp.arange(N_EXPERTS)[None, :]
    of_e = lambda t: jnp.sum(jnp.where(is_e, t[None, :], 0), axis=1)
    tile_of = of_e(first_tile) + (vc - of_e(v_start))
    lo = jnp.maximum(of_e(g_start), tile_of * MOE_M) - tile_of * MOE_M
    hi = jnp.minimum(of_e(g_end), (tile_of + 1) * MOE_M) - tile_of * MOE_M
    prev_tile = jnp.concatenate([jnp.full((1,), -1, tile_of.dtype), tile_of[:-1]])
    first = (tile_of != prev_tile).astype(jnp.int32)
    experts = jnp.arange(N_EXPERTS)
    later = (experts[None, :] > experts[:, None]) & (counts[None, :] > 0)
    next_e = jnp.min(jnp.where(later, experts[None, :], N_EXPERTS), axis=1)
    next_e = jnp.where(next_e < N_EXPERTS, next_e, -1)
    i32 = lambda t: t.astype(jnp.int32)
    visits = (i32(tile_of), i32(e_of), i32(of_e(next_e)), i32(lo), i32(hi), first, i32(total).reshape(1))
    return slot_of, visits


def _reorder_w_in(w_in):
    off_om_end = 4 * D_M
    off_g_end = off_om_end + 2 * H_M
    w = jnp.concatenate([
        w_in[:, :off_om_end],
        w_in[:, off_g_end:],
        w_in[:, off_om_end:off_g_end],
        jnp.zeros((D_MODEL, LANES - 2 * H_M), w_in.dtype),
    ], axis=1)
    return w.astype(BF16)


def kernel(x_prompt, x_sample, cache_k, cache_v, state_conv, state_C, state_n, state_m, c_prompt, c_sample, w_ada, b_ada, g_pre_mix, g_post_mix, g_pre_ffn, g_post_ffn, w_in, b_ig, b_fg, conv_w, conv_b, mh_norm, rel_bias, w_out, w_router, b_router, w1, b1, w2, b2):
    depth = w_in.shape[0]
    assert depth == 1
    l = 0
    nb_p, seq, _ = x_prompt.shape
    nb_s, n_tok, _ = x_sample.shape
    rows_s = nb_s * n_tok
    assert rows_s == TOK_TILE

    n_c = nb_p + nb_s
    c_pad = -(-n_c // 8) * 8
    c_all = jnp.concatenate([c_prompt, c_sample, jnp.zeros((c_pad - n_c, D_MODEL), F32)], axis=0)
    mod = _adaln(c_all, w_ada[l], b_ada[l])
    mods = [mod[:, i * D_MODEL:(i + 1) * D_MODEL] for i in range(6)]
    mods_p = [m[:nb_p].reshape(nb_p, 1, D_MODEL) for m in mods]
    mods_s = [jnp.repeat(m[nb_p:n_c], n_tok, axis=0).reshape(1, rows_s, D_MODEL) for m in mods]

    w_r = _reorder_w_in(w_in[l])
    bias_g = jnp.concatenate([b_ig[l], b_fg[l], jnp.zeros((LANES - 2 * H_M,), F32)]).reshape(1, LANES)
    w_out_b = w_out[l].astype(BF16)
    w_router_f = jnp.concatenate([w_router[l], jnp.zeros((D_MODEL, LANES - N_EXPERTS), F32)], axis=1)
    w_router_hi = w_router_f.astype(BF16)
    w_router_p = jnp.stack([w_router_hi, (w_router_f - w_router_hi.astype(F32)).astype(BF16)])
    b_router_p = jnp.concatenate([b_router[l].astype(F32), jnp.zeros((LANES - N_EXPERTS,), F32)]).reshape(1, LANES)

    shift1, scale1, gate1, shift2, scale2, gate2_p = mods_p
    dils = tuple(d for _, d in BRANCHES)
    assert dils[0] == 1
    u, v_m, og, gts, qa, ka, va, *dilated = _inproj(x_prompt, g_pre_mix[l], scale1, shift1, w_r, bias_g, tm=512,
                                                   dilations=dils[1:])
    qkv_d = [(qa, ka, va)] + [tuple(dilated[3 * n:3 * n + 3]) for n in range(len(dils) - 1)]
    gates_row = gts[:, :, :8].transpose(0, 2, 1)
    zeros = lambda *s: jnp.zeros(s, F32)
    mem, c_p, n_p, m_p = _mlstm(u, v_m, og, gts, gates_row, zeros(nb_p, 8, 2 * D_M),
                                zeros(nb_p, H_M, DK_M, DK_M), zeros(nb_p, H_M, DK_M), zeros(nb_p, H_M, LANES),
                                conv_w[l], conv_b[l], mh_norm[l], MLSTM_L_PROMPT)
    branches = [_attn_branch(*qkv, _prompt_bias(rel_bias, w, d), d) for qkv, (w, d) in zip(qkv_d, BRANCHES)]
    x1_p, h2_p, idx_p, gate_p, cnt_p = _postmix(mem, branches, x_prompt, gate1, scale2, shift2,
                                                g_post_mix[l], g_pre_ffn[l], w_out_b, w_router_p, b_router_p,
                                                zeros(1, LANES), tm=256, dilations=dils)
    keep = min(WINDOW_MAX, seq)
    k_p, v_p = _window_heads(ka, va, keep)
    conv_p = u[:, seq - (CONV_W - 1):]

    shift1, scale1, gate1, shift2, scale2, gate2_s = mods_s
    x_s = x_sample.reshape(1, rows_s, D_MODEL)
    u_s, v_s, og_s, gts_s, qa_s, ka_s, va_s = _inproj(x_s, g_pre_mix[l], scale1, shift1, w_r, bias_g, tm=rows_s)
    per_b = lambda t: t.reshape(nb_s, n_tok, t.shape[-1])
    ls = MLSTM_L_SAMPLE
    pad_rows = lambda t, n: jnp.concatenate([t, jnp.zeros((nb_s, n - t.shape[1], t.shape[2]), t.dtype)], axis=1)
    inert = jnp.concatenate([jnp.full((H_M,), NEG_INF, F32), jnp.zeros((LANES - H_M,), F32)])
    gts_pad = jnp.concatenate([per_b(gts_s), jnp.broadcast_to(inert, (nb_s, ls - n_tok, LANES))], axis=1)
    tail0 = jnp.concatenate([zeros(nb_s, 8 - (CONV_W - 1), 2 * D_M), state_conv[l].astype(F32)], axis=1)
    m0x = jnp.broadcast_to(state_m[l].astype(F32)[:, :, None], (nb_s, H_M, LANES))
    mem_s, c_s, n_s, m_s = _mlstm(pad_rows(per_b(u_s), ls), pad_rows(per_b(v_s), ls), pad_rows(per_b(og_s), ls),
                                  gts_pad, gts_pad[:, :, :8].transpose(0, 2, 1), tail0,
                                  state_C[l].astype(F32), state_n[l].astype(F32), m0x,
                                  conv_w[l], conv_b[l], mh_norm[l], ls)
    mem_s = mem_s[:, :n_tok].reshape(1, rows_s, D_M)
    heads = lambda t: t.reshape(nb_s, n_tok, H_A, HD_A)
    k_new, v_new = heads(ka_s), heads(va_s)
    n_past = cache_k.shape[2]
    assert n_tok <= TOK_PAD
    q_h = jnp.pad(heads(qa_s).transpose(0, 2, 1, 3), ((0, 0), (0, 0), (0, TOK_PAD - n_tok), (0, 0)))
    new_t = lambda t: jnp.pad(t.transpose(0, 2, 3, 1), ((0, 0), (0, 0), (0, 0), (0, LANES - n_tok)))
    cache_t = lambda c: c[l].astype(F32).transpose(0, 2, 3, 1)
    att_s = _sample_attention(q_h, cache_t(cache_k), cache_t(cache_v), new_t(k_new), new_t(v_new),
                              *_sample_bias(rel_bias, n_tok, n_past))
    att_s = att_s[:, :, :n_tok].transpose(0, 2, 1, 3).reshape(1, rows_s, D_A)
    zero_o = zeros(1, rows_s, D_A)
    ninf = jnp.full((1, rows_s, D_A), NEG_INF, F32)
    branches_s = [(att_s, zero_o), (zero_o, ninf), (zero_o, ninf)]
    x1_s, h2_s, idx_s, gate_s, cnt_all = _postmix(mem_s, branches_s, x_s, gate1, scale2, shift2,
                                                  g_post_mix[l], g_pre_ffn[l], w_out_b, w_router_p, b_router_p,
                                                  cnt_p, tm=rows_s)

    t_p = nb_p * seq
    slot_of, visits = _moe_plan([idx_p.reshape(t_p, LANES), idx_s.reshape(rows_s, LANES)],
                                cnt_all[0, :N_EXPERTS].astype(jnp.int32))
    slot_tiles = slot_of.reshape((t_p + rows_s) // TOK_TILE, 1, TOK_TILE * TOP_K)
    xs = _moe_scatter(h2_p.reshape(t_p, D_MODEL), h2_s.reshape(rows_s, D_MODEL), slot_tiles)
    ys = _moe_experts(xs, visits, w1[l], b1[l], w2[l], b2[l])
    y_p, y_s = _moe_combine(ys, slot_tiles, gate_p.reshape(t_p, LANES), gate_s.reshape(rows_s, LANES), x1_p.reshape(t_p, D_MODEL), x1_s.reshape(rows_s, D_MODEL),
                            gate2_p, gate2_s.reshape(rows_s, D_MODEL), g_post_ffn[l], seq)
    y_prompt = y_p.reshape(nb_p, seq, D_MODEL)
    y_sample = y_s.reshape(nb_s, n_tok, D_MODEL)

    st = lambda t: t[None]
    conv_s = jnp.concatenate([state_conv[l].astype(F32), per_b(u_s)], axis=1)[:, -(CONV_W - 1):]
    return (y_prompt, y_sample, st(k_p), st(v_p), st(conv_p), st(c_p), st(n_p), st(m_p[:, :, 0]),
            st(k_new), st(v_new), st(conv_s), st(c_s), st(n_s), st(m_s[:, :, 0]))
```

```python
import functools
import math

import jax
import jax.numpy as jnp
import numpy as np
from jax import lax
from jax.experimental import pallas as pl
from jax.experimental.pallas import tpu as pltpu

F32 = jnp.float32
BF16 = jnp.bfloat16
U32 = jnp.uint32
NEG_INF = float("-inf")

D_MODEL = 1024
H_M = 4
D_M = 512
DK_M = 128
CONV_W = 4
H_A = 8
D_A = 512
HD_A = 64
BRANCHES = ((128, 1), (512, 4), (2048, 16))
WINDOW_MAX = 2048
QB = 128
N_BUCKETS = 32
MAX_DISTANCE = 2048
N_EXPERTS = 32
TOP_K = 4
D_FF = 1024
SWIGLU_LIMIT = 7.0
SWIGLU_ALPHA = 1.702
EPS = 1e-6

LANES = 128
C_U, C_V, C_O, C_QA, C_KA, C_VA, C_G = 0, 1024, 1536, 2048, 2560, 3072, 3584
D_IN_R = C_G + LANES

MLSTM_L_PROMPT = 256
MLSTM_L_SAMPLE = 32
ATT_QL = 512
MOE_M = 512
MOE_SUB = 256
TOK_TILE = 128
PACK_W = D_MODEL // 2
PACK_T = PACK_W // LANES
ROW_T = D_MODEL // LANES
HI_MASK = 0xFFFF0000
VMEM_LIMIT = 56 * 1024 * 1024


def _params(sem, vmem=None):
    return pltpu.CompilerParams(dimension_semantics=sem, vmem_limit_bytes=vmem or VMEM_LIMIT)


def _rms(x, g):
    return x * lax.rsqrt(jnp.mean(x * x, axis=-1, keepdims=True) + EPS) * g


def _log_sigmoid(t):
    return jnp.minimum(t, 0.0) - jnp.log1p(jnp.exp(-jnp.abs(t)))


def _ada_kernel(c_ref, w_ref, b_ref, o_ref):
    c = c_ref[...]
    a = (c * jax.nn.sigmoid(c)).astype(BF16)
    o_ref[...] = jnp.dot(a, w_ref[...].astype(BF16), preferred_element_type=F32) + b_ref[...]


def _adaln(c_all, w_ada, b_ada):
    rows = c_all.shape[0]
    n = w_ada.shape[1]
    tn = 1024
    return pl.pallas_call(
        _ada_kernel,
        grid=(n // tn,),
        in_specs=[
            pl.BlockSpec((rows, D_MODEL), lambda j: (0, 0)),
            pl.BlockSpec((D_MODEL, tn), lambda j: (0, j)),
            pl.BlockSpec((1, tn), lambda j: (0, j)),
        ],
        out_specs=pl.BlockSpec((rows, tn), lambda j: (0, j)),
        out_shape=jax.ShapeDtypeStruct((rows, n), F32),
        compiler_params=_params(("arbitrary",)),
        name="adaln",
    )(c_all, w_ada, b_ada.reshape(1, n))


def _to_dilated(x, dil):
    return pltpu.einshape("(lr)c->l(rc)", x, r=dil)


def _from_dilated(x, dil):
    return pltpu.einshape("l(rc)->(lr)c", x, r=dil)


def _inproj_kernel(x_ref, g_ref, sc_ref, sh_ref, w_ref, bg_ref,
                   u_ref, v_ref, o_ref, gt_ref, qa_ref, ka_ref, va_ref, *dil_refs, dilations):
    x = x_ref[0]
    h = _rms(x, g_ref[...]) * (1.0 + sc_ref[0]) + sh_ref[0]
    hb = h.astype(BF16)

    def seg(lo, width):
        return jnp.dot(hb, w_ref[:, lo:lo + width], preferred_element_type=F32)

    u_ref[0] = seg(C_U, 2 * D_M)
    v_ref[0] = seg(C_V, D_M)
    o_ref[0] = jax.nn.sigmoid(seg(C_O, D_M))
    qkv = (seg(C_QA, D_A) * (HD_A ** -0.5), seg(C_KA, D_A), seg(C_VA, D_A))
    for ref, val in zip((qa_ref, ka_ref, va_ref), qkv):
        ref[0] = val
    for n, dil in enumerate(dilations):
        for j, val in enumerate(qkv):
            dil_refs[3 * n + j][0] = _to_dilated(val, dil)
    t = seg(C_G, LANES) + bg_ref[...]
    lane = lax.broadcasted_iota(jnp.int32, t.shape, 1)
    gt_ref[0] = jnp.where(lane < H_M, t, _log_sigmoid(t))


def _inproj(x3, g_pre, scale, shift, w_r, bias_g, tm, dilations=()):
    nb, rows, _ = x3.shape
    per_row_mod = scale.shape[1] != 1
    mod_block = (1, tm, D_MODEL) if per_row_mod else (1, 1, D_MODEL)
    mod_map = (lambda b, i: (b, i, 0)) if per_row_mod else (lambda b, i: (b, 0, 0))
    row_map = lambda b, i: (b, i, 0)
    widths = (2 * D_M, D_M, D_M, LANES, D_A, D_A, D_A)
    dil_specs = [pl.BlockSpec((1, tm // d, d * D_A), row_map) for d in dilations for _ in range(3)]
    dil_shapes = [jax.ShapeDtypeStruct((nb, rows // d, d * D_A), F32) for d in dilations for _ in range(3)]
    return pl.pallas_call(
        functools.partial(_inproj_kernel, dilations=tuple(dilations)),
        grid=(nb, rows // tm),
        in_specs=[
            pl.BlockSpec((1, tm, D_MODEL), row_map),
            pl.BlockSpec((1, D_MODEL), lambda b, i: (0, 0)),
            pl.BlockSpec(mod_block, mod_map),
            pl.BlockSpec(mod_block, mod_map),
            pl.BlockSpec((D_MODEL, D_IN_R), lambda b, i: (0, 0)),
            pl.BlockSpec((1, LANES), lambda b, i: (0, 0)),
        ],
        out_specs=[pl.BlockSpec((1, tm, w), row_map) for w in widths] + dil_specs,
        out_shape=[jax.ShapeDtypeStruct((nb, rows, w), F32) for w in widths] + dil_shapes,
        compiler_params=_params(("arbitrary", "arbitrary")),
        name="inproj",
    )(x3, g_pre.reshape(1, D_MODEL), scale, shift, w_r, bias_g)


def _mlstm_kernel(u_ref, v_ref, og_ref, gc_ref, gr_ref, tail0_ref, c0_ref, n0_ref, m0_ref,
                  cw_ref, cb_ref, mh_ref,
                  mem_ref, c_out_ref, n_out_ref, m_out_ref,
                  c_s, n_s, m_s, tail_s, uext_s, *, chunk):
    L = chunk
    step = pl.program_id(1)

    @pl.when(step == 0)
    def _():
        c_s[...] = c0_ref[0]
        n_s[...] = n0_ref[0]
        m_s[...] = m0_ref[0]
        tail_s[...] = tail0_ref[0]

    u = u_ref[0]
    uext_s[0:8, :] = tail_s[...]
    uext_s[8:8 + L, :] = u
    acc = cb_ref[...] + uext_s[pl.ds(5, L), :] * cw_ref[0:1, :]
    acc = acc + uext_s[pl.ds(6, L), :] * cw_ref[1:2, :]
    acc = acc + uext_s[pl.ds(7, L), :] * cw_ref[2:3, :]
    acc = acc + u * cw_ref[3:4, :]
    tail_s[...] = uext_s[pl.ds(L, 8), :]
    qk = acc * jax.nn.sigmoid(acc)

    gc = gc_ref[0]
    gr = gr_ref[0]
    row = lax.broadcasted_iota(jnp.int32, (L, L), 0)
    col = lax.broadcasted_iota(jnp.int32, (L, L), 1)
    causal = col <= row
    tri = causal.astype(BF16)
    tri_t = (row <= col).astype(BF16)
    lane = lax.broadcasted_iota(jnp.int32, gc.shape, 1)
    srow = lax.broadcasted_iota(jnp.int32, gr.shape, 0)

    def bf16_parts(x):
        hi = x.astype(BF16)
        r1 = x - hi.astype(F32)
        mid = r1.astype(BF16)
        return hi, mid, (r1 - mid.astype(F32)).astype(BF16)

    b_col_all = sum(jnp.dot(tri, p, preferred_element_type=F32)
                    for p in bf16_parts(jnp.where(lane >= H_M, gc, 0.0)))
    b_row_all = sum(jnp.dot(p, tri_t, preferred_element_type=F32)
                    for p in bf16_parts(jnp.where(srow >= H_M, gr, 0.0)))

    v_all = v_ref[0]
    og = og_ref[0]
    for h in range(H_M):
        q = qk[:, h * DK_M:(h + 1) * DK_M]
        k = qk[:, D_M + h * DK_M:D_M + (h + 1) * DK_M] * (DK_M ** -0.5)
        v = v_all[:, h * DK_M:(h + 1) * DK_M]
        li_row = gr[h:h + 1, :]
        li_col = gc[:, h:h + 1]
        b_row = b_row_all[H_M + h:H_M + h + 1, :]
        b_col = b_col_all[:, H_M + h:H_M + h + 1]
        m0 = m_s[h:h + 1, 0:1]
        c0 = c_s[h]
        n0 = n_s[h:h + 1, :]

        d = jnp.where(causal, b_col - b_row + li_row, NEG_INF)
        g_state = b_col + m0
        m = jnp.maximum(jnp.max(d, axis=-1, keepdims=True), g_state)
        w_state = jnp.exp(g_state - m)
        qb = q.astype(BF16)
        kb = k.astype(BF16)
        vb = v.astype(BF16)
        s = lax.dot_general(qb, kb, (((1,), (1,)), ((), ())), preferred_element_type=F32)
        s = s * jnp.exp(d - m)
        num = jnp.dot(s.astype(BF16), vb, preferred_element_type=F32)
        num = num + w_state * jnp.dot(qb, c0.astype(BF16), preferred_element_type=F32)
        den = jnp.sum(s, axis=-1, keepdims=True) + w_state * jnp.sum(q * n0, axis=-1, keepdims=True)
        hh = num / jnp.maximum(jnp.abs(den), jnp.exp(-m))

        b_last = b_col[L - 1:L, :]
        g_tok_row = b_last - b_row + li_row
        m_new = jnp.maximum(b_last + m0, jnp.max(g_tok_row, axis=-1, keepdims=True))
        w_tok_col = jnp.exp(b_last - b_col + li_col - m_new)
        decay = jnp.exp(b_last + m0 - m_new)
        kw = k * w_tok_col
        c_s[h] = decay * c0 + jnp.dot(kw.T.astype(BF16), vb, preferred_element_type=F32)
        n_s[h:h + 1, :] = decay * n0 + jnp.sum(kw, axis=0, keepdims=True)
        m_s[h:h + 1, :] = jnp.broadcast_to(m_new, (1, LANES))

        hn = hh * lax.rsqrt(jnp.mean(hh * hh, axis=-1, keepdims=True) + EPS)
        hn = hn * mh_ref[:, h * DK_M:(h + 1) * DK_M]
        mem_ref[0, :, h * DK_M:(h + 1) * DK_M] = og[:, h * DK_M:(h + 1) * DK_M] * hn

    @pl.when(step == pl.num_programs(1) - 1)
    def _():
        c_out_ref[0] = c_s[...]
        n_out_ref[0] = n_s[...]
        m_out_ref[0] = m_s[...]


def _mlstm(u, v, og, gates_col, gates_row, tail0, c0, n0, m0x, conv_w, conv_b, mh_norm, chunk):
    nb, rows, _ = u.shape
    nc = rows // chunk
    row_map = lambda b, c: (b, c, 0)
    bat3 = lambda b, c: (b, 0, 0)
    const2 = lambda b, c: (0, 0)
    return pl.pallas_call(
        functools.partial(_mlstm_kernel, chunk=chunk),
        grid=(nb, nc),
        in_specs=[
            pl.BlockSpec((1, chunk, 2 * D_M), row_map),
            pl.BlockSpec((1, chunk, D_M), row_map),
            pl.BlockSpec((1, chunk, D_M), row_map),
            pl.BlockSpec((1, chunk, LANES), row_map),
            pl.BlockSpec((1, 8, chunk), lambda b, c: (b, 0, c)),
            pl.BlockSpec((1, 8, 2 * D_M), bat3),
            pl.BlockSpec((1, H_M, DK_M, DK_M), lambda b, c: (b, 0, 0, 0)),
            pl.BlockSpec((1, H_M, DK_M), bat3),
            pl.BlockSpec((1, H_M, LANES), bat3),
            pl.BlockSpec((CONV_W, 2 * D_M), const2),
            pl.BlockSpec((1, 2 * D_M), const2),
            pl.BlockSpec((1, D_M), const2),
        ],
        out_specs=[
            pl.BlockSpec((1, chunk, D_M), row_map),
            pl.BlockSpec((1, H_M, DK_M, DK_M), lambda b, c: (b, 0, 0, 0)),
            pl.BlockSpec((1, H_M, DK_M), bat3),
            pl.BlockSpec((1, H_M, LANES), bat3),
        ],
        out_shape=[
            jax.ShapeDtypeStruct((nb, rows, D_M), F32),
            jax.ShapeDtypeStruct((nb, H_M, DK_M, DK_M), F32),
            jax.ShapeDtypeStruct((nb, H_M, DK_M), F32),
            jax.ShapeDtypeStruct((nb, H_M, LANES), F32),
        ],
        scratch_shapes=[
            pltpu.VMEM((H_M, DK_M, DK_M), F32),
            pltpu.VMEM((H_M, DK_M), F32),
            pltpu.VMEM((H_M, LANES), F32),
            pltpu.VMEM((8, 2 * D_M), F32),
            pltpu.VMEM((chunk + 8, 2 * D_M), F32),
        ],
        compiler_params=_params(("arbitrary", "arbitrary")),
        name="mlstm",
    )(u, v, og, gates_col, gates_row, tail0, c0, n0, m0x,
      conv_w, conv_b.reshape(1, 2 * D_M), mh_norm.reshape(1, D_M))


def _attn_kernel(q_ref, kc_ref, vc_ref, kp_ref, vp_ref, bias_ref, ones_ref, o_ref, l_ref, k_s, v_s, *, ql):
    first = pl.program_id(2) == 0
    k_s[0:QB, :] = kp_ref[0].astype(BF16)
    k_s[QB:QB + ql, :] = kc_ref[0].astype(BF16)
    v_s[0:QB, :] = vp_ref[0].astype(BF16)
    v_s[QB:QB + ql, :] = vc_ref[0].astype(BF16)
    col = lax.broadcasted_iota(jnp.int32, (QB, 2 * QB), 1)
    pad_keys = jnp.logical_and(first, col < QB)
    head0 = lax.broadcasted_iota(jnp.int32, (1, LANES), 1) < HD_A
    n_slab = D_A // LANES
    n_res = q_ref.shape[2] // D_A
    pad_all = jnp.concatenate([pad_keys] * H_A, axis=0)

    def raw_logits(c, j):
        sl = slice(c * LANES, (c + 1) * LANES)
        q = q_ref[0, j * QB:(j + 1) * QB, sl].astype(BF16)
        zero = jnp.zeros_like(q)
        q2 = jnp.concatenate([jnp.where(head0, q, zero), jnp.where(head0, zero, q)], axis=0)
        return lax.dot_general(q2, k_s[j * QB:(j + 2) * QB, sl], (((1,), (1,)), ((), ())),
                               preferred_element_type=F32)

    def softmax(j, logits):
        logits = logits + bias_ref[...].reshape(H_A * QB, 2 * QB)
        if j == 0:
            logits = jnp.where(pad_all, NEG_INF, logits)
        m = jnp.max(logits, axis=-1, keepdims=True)
        return jnp.exp(logits - m).astype(BF16), m

    def finish(c, j, p, m):
        sl = slice(c * LANES, (c + 1) * LANES)
        rows = slice(j * QB, (j + 1) * QB)
        p_cat = jnp.concatenate([p[:QB], p[QB:]], axis=1)
        v_all = v_s[j * QB:(j + 2) * QB, sl]
        v_bd = jnp.concatenate([jnp.where(head0, v_all, jnp.zeros_like(v_all)),
                                jnp.where(head0, jnp.zeros_like(v_all), v_all)], axis=0)
        o = jnp.dot(p_cat, v_bd, preferred_element_type=F32)
        s = jnp.dot(p_cat, ones_ref[...], preferred_element_type=F32)
        o_ref[0, rows, sl] = o / s
        l_ref[0, rows, sl] = jnp.where(head0, m[:QB], m[QB:]) + jnp.log(s)

    for res in range(n_res):
        slabs = range(res * n_slab, (res + 1) * n_slab)
        for j in range(ql // QB):
            p, m = softmax(j, jnp.concatenate([raw_logits(c, j) for c in slabs], axis=0))
            for n, c in enumerate(slabs):
                part = slice(2 * n * QB, 2 * (n + 1) * QB)
                finish(c, j, p[part], m[part])


def _attn_branch(qa, ka, va, bias, dil):
    nb, ld, _ = qa.shape
    ql = min(ATT_QL, ld)
    n_res = min(dil, ATT_QL // ql)
    width = n_res * D_A
    cur = lambda b, r, i: (b, i, r)
    prev = lambda b, r, i: (b, jnp.maximum(i * (ql // QB) - 1, 0), r)
    ones_bd = np.zeros((4 * QB, LANES), np.float32)
    ones_bd[:2 * QB, :HD_A] = 1.0
    ones_bd[2 * QB:, HD_A:] = 1.0
    return pl.pallas_call(
        functools.partial(_attn_kernel, ql=ql),
        grid=(nb, dil // n_res, ld // ql),
        in_specs=[
            pl.BlockSpec((1, ql, width), cur),
            pl.BlockSpec((1, ql, width), cur),
            pl.BlockSpec((1, ql, width), cur),
            pl.BlockSpec((1, QB, width), prev),
            pl.BlockSpec((1, QB, width), prev),
            pl.BlockSpec((H_A, QB, 2 * QB), lambda b, r, i: (0, 0, 0)),
            pl.BlockSpec((4 * QB, LANES), lambda b, r, i: (0, 0)),
        ],
        out_specs=[pl.BlockSpec((1, ql, width), cur), pl.BlockSpec((1, ql, width), cur)],
        out_shape=[jax.ShapeDtypeStruct((nb, ld, dil * D_A), F32)] * 2,
        scratch_shapes=[pltpu.VMEM((QB + ql, width), BF16), pltpu.VMEM((QB + ql, width), BF16)],
        compiler_params=_params(("arbitrary", "arbitrary", "arbitrary")),
        name=f"attn_d{dil}",
    )(qa, ka, va, ka, va, bias, jnp.asarray(ones_bd, BF16))


def _heads_kernel(k_ref, v_ref, ko_ref, vo_ref):
    for src, dst in ((k_ref, ko_ref), (v_ref, vo_ref)):
        for c in range(D_A // LANES):
            xt = src[0, :, c * LANES:(c + 1) * LANES].T
            dst[0, 2 * c] = xt[:HD_A]
            dst[0, 2 * c + 1] = xt[HD_A:]


def _window_heads(ka, va, keep, tm=512):
    nb, seq, _ = ka.shape
    first = (seq - keep) // tm
    in_spec = pl.BlockSpec((1, tm, D_A), lambda b, i: (b, first + i, 0))
    out_spec = pl.BlockSpec((1, H_A, HD_A, tm), lambda b, i: (b, 0, 0, i))
    kt, vt = pl.pallas_call(
        _heads_kernel,
        grid=(nb, keep // tm),
        in_specs=[in_spec, in_spec],
        out_specs=[out_spec, out_spec],
        out_shape=[jax.ShapeDtypeStruct((nb, H_A, HD_A, keep), F32)] * 2,
        compiler_params=_params(("arbitrary", "arbitrary")),
        name="window_heads",
    )(ka, va)
    return kt.transpose(0, 3, 1, 2), vt.transpose(0, 3, 1, 2)


def _t5_bucket(dist):
    max_exact = N_BUCKETS // 2
    n = jnp.maximum(dist, 1).astype(F32)
    large = max_exact + (jnp.log(n / max_exact) / math.log(MAX_DISTANCE / max_exact)
                         * (N_BUCKETS - max_exact)).astype(jnp.int32)
    large = jnp.minimum(large, N_BUCKETS - 1)
    return jnp.where(dist < max_exact, dist, large)


def _branch_bias_vec(rel_bias, window, dil):
    return rel_bias[_t5_bucket(dil * jnp.arange(window // dil + 1))].astype(F32)


def _prompt_bias(rel_bias, window, dil):
    assert window // dil == QB
    n = 2 * QB
    vec = _branch_bias_vec(rel_bias, window, dil)
    vecp = jnp.concatenate([vec[::-1], jnp.full((n - QB - 1, H_A), NEG_INF, F32)], axis=0)
    g = jnp.roll(vecp, QB - 1, axis=0)
    skew = jnp.tile(g, (QB + 1, 1))[:QB * (n + 1)].reshape(QB, n + 1, H_A)[:, :n]
    return skew[::-1].transpose(2, 0, 1)


TOK_PAD = 8


def _sattn_kernel(q_ref, kt_ref, vt_ref, knt_ref, vnt_ref, bc_ref, bn_ref, att_ref):
    nt = (((1,), (1,)), ((), ()))
    for h in range(H_A):
        q = q_ref[0, h].astype(BF16)
        lc = jnp.dot(q, kt_ref[0, h].astype(BF16), preferred_element_type=F32)
        ln = jnp.dot(q, knt_ref[0, h].astype(BF16), preferred_element_type=F32)
        lcs = [lc + bc_ref[d, h] for d in range(len(BRANCHES))]
        lns = [ln + bn_ref[d, h] for d in range(len(BRANCHES))]
        m = functools.reduce(jnp.maximum, [jnp.max(x, axis=-1, keepdims=True) for x in lcs + lns])
        pc = functools.reduce(jnp.add, [jnp.exp(x - m) for x in lcs])
        pn = functools.reduce(jnp.add, [jnp.exp(x - m) for x in lns])
        s = jnp.sum(pc, axis=-1, keepdims=True) + jnp.sum(pn, axis=-1, keepdims=True)
        o = lax.dot_general(pc.astype(BF16), vt_ref[0, h].astype(BF16), nt, preferred_element_type=F32)
        o = o + lax.dot_general(pn.astype(BF16), vnt_ref[0, h].astype(BF16), nt, preferred_element_type=F32)
        att_ref[0, h] = o / s


def _sample_attention(q_h, kt, vt, knt, vnt, bias_c, bias_n):
    nb, _, _, n_past = kt.shape
    bat = lambda b: (b, 0, 0, 0)
    const = lambda b: (0, 0, 0, 0)
    return pl.pallas_call(
        _sattn_kernel,
        grid=(nb,),
        in_specs=[
            pl.BlockSpec((1, H_A, TOK_PAD, HD_A), bat),
            pl.BlockSpec((1, H_A, HD_A, n_past), bat),
            pl.BlockSpec((1, H_A, HD_A, n_past), bat),
            pl.BlockSpec((1, H_A, HD_A, LANES), bat),
            pl.BlockSpec((1, H_A, HD_A, LANES), bat),
            pl.BlockSpec(bias_c.shape, const),
            pl.BlockSpec(bias_n.shape, const),
        ],
        out_specs=pl.BlockSpec((1, H_A, TOK_PAD, HD_A), bat),
        out_shape=jax.ShapeDtypeStruct((nb, H_A, TOK_PAD, HD_A), F32),
        compiler_params=_params(("arbitrary",)),
        name="sample_attn",
    )(q_h, kt, vt, knt, vnt, bias_c, bias_n)


def _sample_bias(rel_bias, n_tok, n_past):
    ninf = lambda n: jnp.full((n, H_A), NEG_INF, F32)
    cached, new = [], []
    for window, dil in BRANCHES:
        nk = window // dil
        rows = n_past // dil
        assert n_past % dil == 0 and rows >= nk
        vec = _branch_bias_vec(rel_bias, window, dil)
        res = jnp.arange(dil)[None, :, None]
        tc, tn = [], []
        for t in range(n_tok):
            near = t // dil + 1
            col = jnp.concatenate([ninf(rows - (nk + 1 - near)), vec[near:nk + 1][::-1]], axis=0)
            tab = jnp.where(res == t % dil, col[:, None, :], NEG_INF).reshape(n_past, H_A)
            tc.append(tab)
            keys = [vec[(t - j) // dil][None] if (t - j) % dil == 0 and 0 <= t - j <= nk * dil else ninf(1)
                    for j in range(n_tok)]
            tn.append(jnp.concatenate(keys + [ninf(LANES - n_tok)], axis=0))
        pad = lambda tabs: jnp.stack(tabs + [jnp.zeros_like(tabs[0])] * (TOK_PAD - n_tok))
        cached.append(pad(tc).transpose(2, 0, 1))
        new.append(pad(tn).transpose(2, 0, 1))
    return jnp.stack(cached), jnp.stack(new)


def _postmix_kernel(mem_ref, o1_ref, l1_ref, o2_ref, l2_ref, o3_ref, l3_ref, x_ref,
                    g1_ref, sc2_ref, sh2_ref, gpost_ref, gpre_ref, wo_ref, wr_ref, br_ref, cnt0_ref,
                    x1_ref, h2_ref, idx_ref, gate_ref, cnt_ref, cnt_s, *, dilations):
    first_step = jnp.logical_and(pl.program_id(0) == 0, pl.program_id(1) == 0)

    @pl.when(first_step)
    def _():
        cnt_s[...] = cnt0_ref[...]

    rows = lambda ref, dil: ref[0] if dil == 1 else _from_dilated(ref[0], dil)
    o1, o2, o3 = (rows(r, d) for r, d in zip((o1_ref, o2_ref, o3_ref), dilations))
    l1, l2, l3 = (rows(r, d) for r, d in zip((l1_ref, l2_ref, l3_ref), dilations))
    lmax = jnp.maximum(jnp.maximum(l1, l2), l3)
    w1 = jnp.exp(l1 - lmax)
    w2 = jnp.exp(l2 - lmax)
    w3 = jnp.exp(l3 - lmax)
    att = (w1 * o1 + w2 * o2 + w3 * o3) / (w1 + w2 + w3)
    y = jnp.dot(mem_ref[0].astype(BF16), wo_ref[0:D_M, :], preferred_element_type=F32)
    y = y + jnp.dot(att.astype(BF16), wo_ref[D_M:, :], preferred_element_type=F32)
    x1 = x_ref[0] + g1_ref[0] * _rms(y, gpost_ref[...])
    x1_ref[0] = x1
    h2 = _rms(x1, gpre_ref[...]) * (1.0 + sc2_ref[0]) + sh2_ref[0]
    h2_ref[0] = h2
    h_hi = h2.astype(BF16)
    h_lo = (h2 - h_hi.astype(F32)).astype(BF16)
    w_hi = wr_ref[0]
    logits = jnp.dot(h_hi, w_hi, preferred_element_type=F32)
    logits = logits + jnp.dot(h_lo, w_hi, preferred_element_type=F32)
    logits = logits + jnp.dot(h_hi, wr_ref[1], preferred_element_type=F32) + br_ref[...]
    lane = lax.broadcasted_iota(jnp.int32, logits.shape, 1)
    lane_f = lane.astype(F32)
    logits = jnp.where(lane < N_EXPERTS, logits, NEG_INF)
    vals, idxs = [], []
    for _ in range(TOP_K):
        mx = jnp.max(logits, axis=-1, keepdims=True)
        ix = jnp.min(jnp.where(logits == mx, lane_f, float(LANES)), axis=-1, keepdims=True)
        vals.append(mx)
        idxs.append(ix)
        logits = jnp.where(lane_f == ix, NEG_INF, logits)
    es = [jnp.exp(v - vals[0]) for v in vals]
    tot = es[0] + es[1] + es[2] + es[3]
    tm = logits.shape[0]
    picks = [(lane_f == ix).astype(F32) for ix in idxs]
    chosen = picks[0] + picks[1] + picks[2] + picks[3]
    r_i = lax.broadcasted_iota(jnp.int32, (tm, tm), 0)
    c_i = lax.broadcasted_iota(jnp.int32, (tm, tm), 1)
    earlier = jnp.dot((c_i < r_i).astype(BF16), chosen.astype(BF16), preferred_element_type=F32)
    before = earlier + cnt_s[...]
    cnt_s[...] = cnt_s[...] + jnp.sum(chosen, axis=0, keepdims=True)
    cnt_ref[...] = cnt_s[...]
    idx_tile = jnp.zeros(logits.shape, F32)
    gate_tile = jnp.zeros(logits.shape, F32)
    for k in range(TOP_K):
        rank = jnp.sum(picks[k] * before, axis=-1, keepdims=True)
        idx_tile = jnp.where(lane == k, idxs[k], idx_tile)
        idx_tile = jnp.where(lane == TOP_K + k, rank, idx_tile)
        gate_tile = jnp.where(lane == k, es[k] / tot, gate_tile)
    idx_ref[0] = idx_tile.astype(jnp.int32)
    gate_ref[0] = gate_tile


def _postmix(mem, branches, x3, gate1, scale2, shift2, g_post, g_pre, w_out_b, w_router_p, b_router_p, counts0, tm,
             dilations=(1, 1, 1)):
    nb, rows, _ = x3.shape
    per_row_mod = gate1.shape[1] != 1
    mod_block = (1, tm, D_MODEL) if per_row_mod else (1, 1, D_MODEL)
    mod_map = (lambda b, i: (b, i, 0)) if per_row_mod else (lambda b, i: (b, 0, 0))
    row_map = lambda b, i: (b, i, 0)
    const2 = lambda b, i: (0, 0)
    half = pl.BlockSpec((1, tm, D_M), row_map)
    flat = [a for pair in branches for a in pair]
    branch_specs = [pl.BlockSpec((1, tm // d, d * D_A), row_map) for d in dilations for _ in range(2)]
    return pl.pallas_call(
        functools.partial(_postmix_kernel, dilations=tuple(dilations)),
        grid=(nb, rows // tm),
        in_specs=[half] + branch_specs + [
            pl.BlockSpec((1, tm, D_MODEL), row_map),
            pl.BlockSpec(mod_block, mod_map),
            pl.BlockSpec(mod_block, mod_map),
            pl.BlockSpec(mod_block, mod_map),
            pl.BlockSpec((1, D_MODEL), const2),
            pl.BlockSpec((1, D_MODEL), const2),
            pl.BlockSpec((D_MODEL, D_MODEL), const2),
            pl.BlockSpec((2, D_MODEL, LANES), lambda b, i: (0, 0, 0)),
            pl.BlockSpec((1, LANES), const2),
            pl.BlockSpec((1, LANES), const2),
        ],
        out_specs=[
            pl.BlockSpec((1, tm, D_MODEL), row_map),
            pl.BlockSpec((1, tm, D_MODEL), row_map),
            pl.BlockSpec((1, tm, LANES), row_map),
            pl.BlockSpec((1, tm, LANES), row_map),
            pl.BlockSpec((1, LANES), const2),
        ],
        out_shape=[
            jax.ShapeDtypeStruct((nb, rows, D_MODEL), F32),
            jax.ShapeDtypeStruct((nb, rows, D_MODEL), F32),
            jax.ShapeDtypeStruct((nb, rows, LANES), jnp.int32),
            jax.ShapeDtypeStruct((nb, rows, LANES), F32),
            jax.ShapeDtypeStruct((1, LANES), F32),
        ],
        scratch_shapes=[pltpu.VMEM((1, LANES), F32)],
        compiler_params=_params(("arbitrary", "arbitrary")),
        name="postmix",
    )(mem, *flat, x3, gate1, scale2, shift2, g_post.reshape(1, D_MODEL), g_pre.reshape(1, D_MODEL),
      w_out_b, w_router_p, b_router_p, counts0)


def _row_wait(src_rows, dst_rows, sem):
    pltpu.make_async_copy(src_rows, dst_rows, sem).wait()


def _scatter_kernel(slot_ref, hp_ref, hs_ref, xs_hbm, buf, sem, *, n_prompt_tiles):
    h = jnp.where(pl.program_id(0) >= n_prompt_tiles, hs_ref[...], hp_ref[...])
    bits = pltpu.bitcast(h.astype(BF16).astype(F32), U32)
    packed = (bits[:, :PACK_W] & U32(HI_MASK)) | (bits[:, PACK_W:] >> 16)
    buf[...] = pltpu.einshape("m(jc)->mjc", packed, j=PACK_T)

    def issue(r, carry):
        src = buf.at[pl.ds(r, 1)]
        for k in range(TOP_K):
            pltpu.make_async_copy(src, xs_hbm.at[pl.ds(slot_ref[0, 0, r * TOP_K + k], 1)],
                                  sem).start(priority=k % 2)
        return carry

    lax.fori_loop(0, TOK_TILE, issue, 0, unroll=4)
    for _ in range(TOP_K):
        _row_wait(buf, xs_hbm.at[pl.ds(0, TOK_TILE)], sem)


def _moe_scatter(h_p, h_s, slot_tiles):
    tiles = slot_tiles.shape[0]
    n_p = h_p.shape[0] // TOK_TILE
    assert tiles == n_p + 1 and h_s.shape[0] == TOK_TILE
    n_rows = tiles * TOK_TILE * TOP_K
    return pl.pallas_call(
        functools.partial(_scatter_kernel, n_prompt_tiles=n_p),
        grid=(tiles,),
        in_specs=[
            pl.BlockSpec((1, 1, TOK_TILE * TOP_K), lambda i: (i, 0, 0), memory_space=pltpu.SMEM),
            pl.BlockSpec((TOK_TILE, D_MODEL), lambda i: (jnp.minimum(i, n_p - 1), 0)),
            pl.BlockSpec((TOK_TILE, D_MODEL), lambda i: (0, 0)),
        ],
        out_specs=pl.BlockSpec(memory_space=pl.ANY),
        out_shape=jax.ShapeDtypeStruct((n_rows, PACK_T, LANES), U32),
        scratch_shapes=[pltpu.VMEM((TOK_TILE, PACK_T, LANES), U32), pltpu.SemaphoreType.DMA],
        compiler_params=_params(("arbitrary",)),
        name="moe_scatter",
    )(slot_tiles, h_p, h_s)


def _expert_kernel(tile_ref, exp_ref, next_ref, lo_ref, hi_ref, first_ref, nvis_ref,
                   xs_ref, w1_hbm, b1_ref, w2_hbm, b2_ref, ys_ref, w1f, w2f, w1b, w2b, sem):
    v = pl.program_id(0)
    active = v < nvis_ref[0]
    changed = jnp.logical_or(v == 0, exp_ref[v] != exp_ref[jnp.maximum(v - 1, 0)])
    lo = lo_ref[v]
    hi = hi_ref[v]

    def weight_copies(e):
        return (pltpu.make_async_copy(w1_hbm.at[e], w1f, sem.at[0]),
                pltpu.make_async_copy(w2_hbm.at[e], w2f, sem.at[1]))

    @pl.when(v == 0)
    def _():
        for c in weight_copies(exp_ref[0]):
            c.start()

    @pl.when(jnp.logical_and(active, changed))
    def _():
        for c in weight_copies(exp_ref[v]):
            c.wait()
        w1b[...] = w1f[...].astype(BF16)
        w2b[...] = w2f[...].astype(BF16)

        @pl.when(next_ref[v] >= 0)
        def _():
            for c in weight_copies(next_ref[v]):
                c.start()

    def mlp_rows(r0):
        w = pltpu.einshape("mjc->m(jc)", xs_ref[r0:r0 + MOE_SUB])
        x_hi = pltpu.bitcast(w & U32(HI_MASK), F32).astype(BF16)
        x_lo = pltpu.bitcast(w << 16, F32).astype(BF16)
        hgu = jnp.dot(x_hi, w1b[0:PACK_W, :], preferred_element_type=F32)
        hgu = hgu + jnp.dot(x_lo, w1b[PACK_W:, :], preferred_element_type=F32) + b1_ref[0]
        x_glu = jnp.minimum(hgu[:, :D_FF], SWIGLU_LIMIT)
        x_lin = jnp.clip(hgu[:, D_FF:], -SWIGLU_LIMIT, SWIGLU_LIMIT)
        act = x_glu * jax.nn.sigmoid(SWIGLU_ALPHA * x_glu) * (x_lin + 1.0)
        y = jnp.dot(act.astype(BF16), w2b[...], preferred_element_type=F32) + b2_ref[0]
        return pltpu.einshape("m(jc)->mjc", y, j=ROW_T)

    subs = [r * MOE_SUB for r in range(MOE_M // MOE_SUB)]
    whole = jnp.logical_and(lo == 0, hi == MOE_M)

    @pl.when(jnp.logical_and(active, whole))
    def _():
        for r0 in subs:
            ys_ref[r0:r0 + MOE_SUB] = mlp_rows(r0)

    shared = jnp.logical_and(active, jnp.logical_not(whole))

    @pl.when(jnp.logical_and(shared, first_ref[v] == 1))
    def _():
        ys_ref[...] = jnp.zeros(ys_ref.shape, F32)

    for r0 in subs:
        @pl.when(jnp.logical_and(shared, jnp.logical_and(lo < r0 + MOE_SUB, hi > r0)))
        def _():
            rows = r0 + lax.broadcasted_iota(jnp.int32, (MOE_SUB, 1, 1), 0)
            mine = jnp.logical_and(rows >= lo, rows < hi)
            ys_ref[r0:r0 + MOE_SUB] = jnp.where(mine, mlp_rows(r0), ys_ref[r0:r0 + MOE_SUB])


def _moe_experts(xs, visits, w1, b1, w2, b2):
    n_rows = xs.shape[0]
    n_vis = visits[0].shape[0]
    tile_map = lambda v, tile, *_: (tile[v], 0, 0)
    exp_map = lambda v, tile, exp, *_: (exp[v], 0, 0)
    return pl.pallas_call(
        _expert_kernel,
        grid_spec=pltpu.PrefetchScalarGridSpec(
            num_scalar_prefetch=7,
            grid=(n_vis,),
            in_specs=[
                pl.BlockSpec((MOE_M, PACK_T, LANES), tile_map),
                pl.BlockSpec(memory_space=pl.ANY),
                pl.BlockSpec((1, 1, 2 * D_FF), exp_map),
                pl.BlockSpec(memory_space=pl.ANY),
                pl.BlockSpec((1, 1, D_MODEL), exp_map),
            ],
            out_specs=pl.BlockSpec((MOE_M, ROW_T, LANES), tile_map),
            scratch_shapes=[pltpu.VMEM((D_MODEL, 2 * D_FF), F32), pltpu.VMEM((D_FF, D_MODEL), F32),
                            pltpu.VMEM((D_MODEL, 2 * D_FF), BF16), pltpu.VMEM((D_FF, D_MODEL), BF16),
                            pltpu.SemaphoreType.DMA((2,))],
        ),
        out_shape=jax.ShapeDtypeStruct((n_rows, ROW_T, LANES), F32),
        compiler_params=_params(("arbitrary",)),
        name="moe_experts",
    )(*visits, xs, w1, b1.reshape(N_EXPERTS, 1, 2 * D_FF), w2, b2.reshape(N_EXPERTS, 1, D_MODEL))


def _combine_kernel(slot_ref, next_ref, gatep_ref, gates_ref, x1p_ref, x1s_ref, g2p_ref, g2s_ref, gpost_ref, ys_hbm,
                    yp_ref, ysm_ref, buf, sem, *, n_prompt_tiles):
    i = pl.program_id(0)
    last = pl.num_programs(0) - 1
    cur = lax.rem(i, 2)

    def gather(slots, b):
        def issue(r, carry):
            for k in range(TOP_K):
                pltpu.make_async_copy(ys_hbm.at[pl.ds(slots[0, 0, r * TOP_K + k], 1)],
                                      buf.at[b, pl.ds(k * TOK_TILE + r, 1)], sem.at[b]).start(priority=k % 2)
            return carry

        lax.fori_loop(0, TOK_TILE, issue, 0, unroll=4)

    @pl.when(i == 0)
    def _():
        gather(slot_ref, cur)

    @pl.when(i < last)
    def _():
        gather(next_ref, 1 - cur)

    for k in range(TOP_K):
        _row_wait(ys_hbm.at[pl.ds(0, TOK_TILE)], buf.at[cur, pl.ds(k * TOK_TILE, TOK_TILE)], sem.at[cur])
    is_sample = i >= n_prompt_tiles
    g = jnp.where(is_sample, gates_ref[...], gatep_ref[...])
    f = jnp.zeros((TOK_TILE, D_MODEL), F32)
    for k in range(TOP_K):
        rows = pltpu.einshape("mjc->m(jc)", buf[cur, pl.ds(k * TOK_TILE, TOK_TILE)])
        f = f + rows * g[:, k:k + 1]
    x1 = jnp.where(is_sample, x1s_ref[...], x1p_ref[...])
    g2 = jnp.where(is_sample, g2s_ref[...], g2p_ref[0])
    y = x1 + g2 * _rms(f, gpost_ref[...])

    @pl.when(jnp.logical_not(is_sample))
    def _():
        yp_ref[...] = y

    @pl.when(is_sample)
    def _():
        ysm_ref[...] = y


def _moe_combine(ys, slot_tiles, gates_p, gates_s, x1_p, x1_s, gate2_p, gate2_s, g_post, rows_per_batch):
    tiles = slot_tiles.shape[0]
    n_p = x1_p.shape[0] // TOK_TILE
    assert tiles == n_p + 1 and x1_s.shape[0] == TOK_TILE
    tiles_per_batch = rows_per_batch // TOK_TILE
    p_tile = lambda i: (jnp.minimum(i, n_p - 1), 0)
    const2 = lambda i: (0, 0)
    return pl.pallas_call(
        functools.partial(_combine_kernel, n_prompt_tiles=n_p),
        grid=(tiles,),
        in_specs=[
            pl.BlockSpec((1, 1, TOK_TILE * TOP_K), lambda i: (i, 0, 0), memory_space=pltpu.SMEM),
            pl.BlockSpec((1, 1, TOK_TILE * TOP_K), lambda i: (jnp.minimum(i + 1, tiles - 1), 0, 0),
                         memory_space=pltpu.SMEM),
            pl.BlockSpec((TOK_TILE, LANES), p_tile),
            pl.BlockSpec((TOK_TILE, LANES), const2),
            pl.BlockSpec((TOK_TILE, D_MODEL), p_tile),
            pl.BlockSpec((TOK_TILE, D_MODEL), const2),
            pl.BlockSpec((1, 1, D_MODEL), lambda i: (jnp.minimum(i, n_p - 1) // tiles_per_batch, 0, 0)),
            pl.BlockSpec((TOK_TILE, D_MODEL), const2),
            pl.BlockSpec((1, D_MODEL), const2),
            pl.BlockSpec(memory_space=pl.ANY),
        ],
        out_specs=[pl.BlockSpec((TOK_TILE, D_MODEL), p_tile), pl.BlockSpec((TOK_TILE, D_MODEL), const2)],
        out_shape=[jax.ShapeDtypeStruct(x1_p.shape, F32), jax.ShapeDtypeStruct(x1_s.shape, F32)],
        scratch_shapes=[pltpu.VMEM((2, TOP_K * TOK_TILE, ROW_T, LANES), F32), pltpu.SemaphoreType.DMA((2,))],
        compiler_params=_params(("arbitrary",)),
        name="moe_combine",
    )(slot_tiles, slot_tiles, gates_p, gates_s, x1_p, x1_s, gate2_p, gate2_s, g_post.reshape(1, D_MODEL), ys)


def _moe_plan(routes, counts):
    a = sum(r.shape[0] for r in routes) * TOP_K
    assert a % MOE_M == 0
    g_end = jnp.cumsum(counts)
    g_start = g_end - counts

    def slots(route):
        flat_e = route[:, :TOP_K].reshape(-1)
        rank = route[:, TOP_K:2 * TOP_K].reshape(-1)
        onehot = flat_e[:, None] == jnp.arange(N_EXPERTS)[None, :]
        return (jnp.sum(jnp.where(onehot, g_start[None, :], 0), axis=1) + rank).astype(jnp.int32)

    slot_of = jnp.concatenate([slots(r) for r in routes])

    n_tiles = a // MOE_M
    n_vis = n_tiles + N_EXPERTS - 1
    first_tile = g_start // MOE_M
    last_tile = jnp.where(counts > 0, (g_end - 1) // MOE_M, first_tile)
    nvis_e = jnp.where(counts > 0, last_tile - first_tile + 1, 0)
    v_end = jnp.cumsum(nvis_e)
    v_start = v_end - nvis_e
    total = v_end[-1]
    v = jnp.arange(n_vis)
    vc = jnp.minimum(v, total - 1)
    e_of = jnp.minimum(jnp.sum((vc[:, None] >= v_end[None, :]).astype(jnp.int32), axis=1), N_EXPERTS - 1)
    is_e = e_of[:, None] == jnp.arange(N_EXPERTS)[None, :]
    of_e = lambda t: jnp.sum(jnp.where(is_e, t[None, :], 0), axis=1)
    tile_of = of_e(first_tile) + (vc - of_e(v_start))
    lo = jnp.maximum(of_e(g_start), tile_of * MOE_M) - tile_of * MOE_M
    hi = jnp.minimum(of_e(g_end), (tile_of + 1) * MOE_M) - tile_of * MOE_M
    prev_tile = jnp.concatenate([jnp.full((1,), -1, tile_of.dtype), tile_of[:-1]])
    first = (tile_of != prev_tile).astype(jnp.int32)
    experts = jnp.arange(N_EXPERTS)
    later = (experts[None, :] > experts[:, None]) & (counts[None, :] > 0)
    next_e = jnp.min(jnp.where(later, experts[None, :], N_EXPERTS), axis=1)
    next_e = jnp.where(next_e < N_EXPERTS, next_e, -1)
    i32 = lambda t: t.astype(jnp.int32)
    visits = (i32(tile_of), i32(e_of), i32(of_e(next_e)), i32(lo), i32(hi), first, i32(total).reshape(1))
    return slot_of, visits


def _reorder_w_in(w_in):
    off_om_end = 4 * D_M
    off_g_end = off_om_end + 2 * H_M
    w = jnp.concatenate([
        w_in[:, :off_om_end],
        w_in[:, off_g_end:],
        w_in[:, off_om_end:off_g_end],
        jnp.zeros((D_MODEL, LANES - 2 * H_M), w_in.dtype),
    ], axis=1)
    return w.astype(BF16)


def kernel(x_prompt, x_sample, cache_k, cache_v, state_conv, state_C, state_n, state_m, c_prompt, c_sample, w_ada, b_ada, g_pre_mix, g_post_mix, g_pre_ffn, g_post_ffn, w_in, b_ig, b_fg, conv_w, conv_b, mh_norm, rel_bias, w_out, w_router, b_router, w1, b1, w2, b2):
    depth = w_in.shape[0]
    assert depth == 1
    l = 0
    nb_p, seq, _ = x_prompt.shape
    nb_s, n_tok, _ = x_sample.shape
    rows_s = nb_s * n_tok
    assert rows_s == TOK_TILE

    n_c = nb_p + nb_s
    c_pad = -(-n_c // 8) * 8
    c_all = jnp.concatenate([c_prompt, c_sample, jnp.zeros((c_pad - n_c, D_MODEL), F32)], axis=0)
    mod = _adaln(c_all, w_ada[l], b_ada[l])
    mods = [mod[:, i * D_MODEL:(i + 1) * D_MODEL] for i in range(6)]
    mods_p = [m[:nb_p].reshape(nb_p, 1, D_MODEL) for m in mods]
    mods_s = [jnp.repeat(m[nb_p:n_c], n_tok, axis=0).reshape(1, rows_s, D_MODEL) for m in mods]

    w_r = _reorder_w_in(w_in[l])
    bias_g = jnp.concatenate([b_ig[l], b_fg[l], jnp.zeros((LANES - 2 * H_M,), F32)]).reshape(1, LANES)
    w_out_b = w_out[l].astype(BF16)
    w_router_f = jnp.concatenate([w_router[l], jnp.zeros((D_MODEL, LANES - N_EXPERTS), F32)], axis=1)
    w_router_hi = w_router_f.astype(BF16)
    w_router_p = jnp.stack([w_router_hi, (w_router_f - w_router_hi.astype(F32)).astype(BF16)])
    b_router_p = jnp.concatenate([b_router[l].astype(F32), jnp.zeros((LANES - N_EXPERTS,), F32)]).reshape(1, LANES)

    shift1, scale1, gate1, shift2, scale2, gate2_p = mods_p
    dils = tuple(d for _, d in BRANCHES)
    assert dils[0] == 1
    u, v_m, og, gts, qa, ka, va, *dilated = _inproj(x_prompt, g_pre_mix[l], scale1, shift1, w_r, bias_g, tm=512,
                                                   dilations=dils[1:])
    qkv_d = [(qa, ka, va)] + [tuple(dilated[3 * n:3 * n + 3]) for n in range(len(dils) - 1)]
    gates_row = gts[:, :, :8].transpose(0, 2, 1)
    zeros = lambda *s: jnp.zeros(s, F32)
    mem, c_p, n_p, m_p = _mlstm(u, v_m, og, gts, gates_row, zeros(nb_p, 8, 2 * D_M),
                                zeros(nb_p, H_M, DK_M, DK_M), zeros(nb_p, H_M, DK_M), zeros(nb_p, H_M, LANES),
                                conv_w[l], conv_b[l], mh_norm[l], MLSTM_L_PROMPT)
    branches = [_attn_branch(*qkv, _prompt_bias(rel_bias, w, d), d) for qkv, (w, d) in zip(qkv_d, BRANCHES)]
    x1_p, h2_p, idx_p, gate_p, cnt_p = _postmix(mem, branches, x_prompt, gate1, scale2, shift2,
                                                g_post_mix[l], g_pre_ffn[l], w_out_b, w_router_p, b_router_p,
                                                zeros(1, LANES), tm=512, dilations=dils)
    keep = min(WINDOW_MAX, seq)
    k_p, v_p = _window_heads(ka, va, keep)
    conv_p = u[:, seq - (CONV_W - 1):]

    shift1, scale1, gate1, shift2, scale2, gate2_s = mods_s
    x_s = x_sample.reshape(1, rows_s, D_MODEL)
    u_s, v_s, og_s, gts_s, qa_s, ka_s, va_s = _inproj(x_s, g_pre_mix[l], scale1, shift1, w_r, bias_g, tm=rows_s)
    per_b = lambda t: t.reshape(nb_s, n_tok, t.shape[-1])
    ls = MLSTM_L_SAMPLE
    pad_rows = lambda t, n: jnp.concatenate([t, jnp.zeros((nb_s, n - t.shape[1], t.shape[2]), t.dtype)], axis=1)
    inert = jnp.concatenate([jnp.full((H_M,), NEG_INF, F32), jnp.zeros((LANES - H_M,), F32)])
    gts_pad = jnp.concatenate([per_b(gts_s), jnp.broadcast_to(inert, (nb_s, ls - n_tok, LANES))], axis=1)
    tail0 = jnp.concatenate([zeros(nb_s, 8 - (CONV_W - 1), 2 * D_M), state_conv[l].astype(F32)], axis=1)
    m0x = jnp.broadcast_to(state_m[l].astype(F32)[:, :, None], (nb_s, H_M, LANES))
    mem_s, c_s, n_s, m_s = _mlstm(pad_rows(per_b(u_s), ls), pad_rows(per_b(v_s), ls), pad_rows(per_b(og_s), ls),
                                  gts_pad, gts_pad[:, :, :8].transpose(0, 2, 1), tail0,
                                  state_C[l].astype(F32), state_n[l].astype(F32), m0x,
                                  conv_w[l], conv_b[l], mh_norm[l], ls)
    mem_s = mem_s[:, :n_tok].reshape(1, rows_s, D_M)
    heads = lambda t: t.reshape(nb_s, n_tok, H_A, HD_A)
    k_new, v_new = heads(ka_s), heads(va_s)
    n_past = cache_k.shape[2]
    assert n_tok <= TOK_PAD
    q_h = jnp.pad(heads(qa_s).transpose(0, 2, 1, 3), ((0, 0), (0, 0), (0, TOK_PAD - n_tok), (0, 0)))
    new_t = lambda t: jnp.pad(t.transpose(0, 2, 3, 1), ((0, 0), (0, 0), (0, 0), (0, LANES - n_tok)))
    cache_t = lambda c: c[l].astype(F32).transpose(0, 2, 3, 1)
    att_s = _sample_attention(q_h, cache_t(cache_k), cache_t(cache_v), new_t(k_new), new_t(v_new),
                              *_sample_bias(rel_bias, n_tok, n_past))
    att_s = att_s[:, :, :n_tok].transpose(0, 2, 1, 3).reshape(1, rows_s, D_A)
    zero_o = zeros(1, rows_s, D_A)
    ninf = jnp.full((1, rows_s, D_A), NEG_INF, F32)
    branches_s = [(att_s, zero_o), (zero_o, ninf), (zero_o, ninf)]
    x1_s, h2_s, idx_s, gate_s, cnt_all = _postmix(mem_s, branches_s, x_s, gate1, scale2, shift2,
                                                  g_post_mix[l], g_pre_ffn[l], w_out_b, w_router_p, b_router_p,
                                                  cnt_p, tm=rows_s)

    t_p = nb_p * seq
    slot_of, visits = _moe_plan([idx_p.reshape(t_p, LANES), idx_s.reshape(rows_s, LANES)],
                                cnt_all[0, :N_EXPERTS].astype(jnp.int32))
    slot_tiles = slot_of.reshape((t_p + rows_s) // TOK_TILE, 1, TOK_TILE * TOP_K)
    xs = _moe_scatter(h2_p.reshape(t_p, D_MODEL), h2_s.reshape(rows_s, D_MODEL), slot_tiles)
    ys = _moe_experts(xs, visits, w1[l], b1[l], w2[l], b2[l])
    y_p, y_s = _moe_combine(ys, slot_tiles, gate_p.reshape(t_p, LANES), gate_s.reshape(rows_s, LANES), x1_p.reshape(t_p, D_MODEL), x1_s.reshape(rows_s, D_MODEL),
                            gate2_p, gate2_s.reshape(rows_s, D_MODEL), g_post_ffn[l], seq)
    y_prompt = y_p.reshape(nb_p, seq, D_MODEL)
    y_sample = y_s.reshape(nb_s, n_tok, D_MODEL)

    st = lambda t: t[None]
    conv_s = jnp.concatenate([state_conv[l].astype(F32), per_b(u_s)], axis=1)[:, -(CONV_W - 1):]
    return (y_prompt, y_sample, st(k_p), st(v_p), st(conv_p), st(c_p), st(n_p), st(m_p[:, :, 0]),
            st(k_new), st(v_new), st(conv_s), st(c_s), st(n_s), st(m_s[:, :, 0]))
```

```python
import functools
import math

import jax
import jax.numpy as jnp
import numpy as np
from jax import lax
from jax.experimental import pallas as pl
from jax.experimental.pallas import tpu as pltpu

F32 = jnp.float32
BF16 = jnp.bfloat16
U32 = jnp.uint32
NEG_INF = float("-inf")

D_MODEL = 1024
H_M = 4
D_M = 512
DK_M = 128
CONV_W = 4
H_A = 8
D_A = 512
HD_A = 64
BRANCHES = ((128, 1), (512, 4), (2048, 16))
WINDOW_MAX = 2048
QB = 128
N_BUCKETS = 32
MAX_DISTANCE = 2048
N_EXPERTS = 32
TOP_K = 4
D_FF = 1024
SWIGLU_LIMIT = 7.0
SWIGLU_ALPHA = 1.702
EPS = 1e-6

LANES = 128
C_U, C_V, C_O, C_QA, C_KA, C_VA, C_G = 0, 1024, 1536, 2048, 2560, 3072, 3584
D_IN_R = C_G + LANES

MLSTM_L_PROMPT = 256
MLSTM_L_SAMPLE = 32
ATT_QL = 1024
MOE_M = 512
MOE_SUB = 256
TOK_TILE = 128
PACK_W = D_MODEL // 2
PACK_T = PACK_W // LANES
ROW_T = D_MODEL // LANES
HI_MASK = 0xFFFF0000
VMEM_LIMIT = 56 * 1024 * 1024


def _params(sem, vmem=None):
    return pltpu.CompilerParams(dimension_semantics=sem, vmem_limit_bytes=vmem or VMEM_LIMIT)


def _rms(x, g):
    return x * lax.rsqrt(jnp.mean(x * x, axis=-1, keepdims=True) + EPS) * g


def _log_sigmoid(t):
    return jnp.minimum(t, 0.0) - jnp.log1p(jnp.exp(-jnp.abs(t)))


def _ada_kernel(c_ref, w_ref, b_ref, o_ref):
    c = c_ref[...]
    a = (c * jax.nn.sigmoid(c)).astype(BF16)
    o_ref[...] = jnp.dot(a, w_ref[...].astype(BF16), preferred_element_type=F32) + b_ref[...]


def _adaln(c_all, w_ada, b_ada):
    rows = c_all.shape[0]
    n = w_ada.shape[1]
    tn = 1024
    return pl.pallas_call(
        _ada_kernel,
        grid=(n // tn,),
        in_specs=[
            pl.BlockSpec((rows, D_MODEL), lambda j: (0, 0)),
            pl.BlockSpec((D_MODEL, tn), lambda j: (0, j)),
            pl.BlockSpec((1, tn), lambda j: (0, j)),
        ],
        out_specs=pl.BlockSpec((rows, tn), lambda j: (0, j)),
        out_shape=jax.ShapeDtypeStruct((rows, n), F32),
        compiler_params=_params(("arbitrary",)),
        name="adaln",
    )(c_all, w_ada, b_ada.reshape(1, n))


def _to_dilated(x, dil):
    return pltpu.einshape("(lr)c->l(rc)", x, r=dil)


def _from_dilated(x, dil):
    return pltpu.einshape("l(rc)->(lr)c", x, r=dil)


def _inproj_kernel(x_ref, g_ref, sc_ref, sh_ref, w_ref, bg_ref,
                   u_ref, v_ref, o_ref, gt_ref, qa_ref, ka_ref, va_ref, *dil_refs, dilations):
    x = x_ref[0]
    h = _rms(x, g_ref[...]) * (1.0 + sc_ref[0]) + sh_ref[0]
    hb = h.astype(BF16)

    def seg(lo, width):
        return jnp.dot(hb, w_ref[:, lo:lo + width], preferred_element_type=F32)

    u_ref[0] = seg(C_U, 2 * D_M)
    v_ref[0] = seg(C_V, D_M)
    o_ref[0] = jax.nn.sigmoid(seg(C_O, D_M))
    qkv = (seg(C_QA, D_A) * (HD_A ** -0.5), seg(C_KA, D_A), seg(C_VA, D_A))
    for ref, val in zip((qa_ref, ka_ref, va_ref), qkv):
        ref[0] = val
    for n, dil in enumerate(dilations):
        for j, val in enumerate(qkv):
            dil_refs[3 * n + j][0] = _to_dilated(val, dil)
    t = seg(C_G, LANES) + bg_ref[...]
    lane = lax.broadcasted_iota(jnp.int32, t.shape, 1)
    gt_ref[0] = jnp.where(lane < H_M, t, _log_sigmoid(t))


def _inproj(x3, g_pre, scale, shift, w_r, bias_g, tm, dilations=()):
    nb, rows, _ = x3.shape
    per_row_mod = scale.shape[1] != 1
    mod_block = (1, tm, D_MODEL) if per_row_mod else (1, 1, D_MODEL)
    mod_map = (lambda b, i: (b, i, 0)) if per_row_mod else (lambda b, i: (b, 0, 0))
    row_map = lambda b, i: (b, i, 0)
    widths = (2 * D_M, D_M, D_M, LANES, D_A, D_A, D_A)
    dil_specs = [pl.BlockSpec((1, tm // d, d * D_A), row_map) for d in dilations for _ in range(3)]
    dil_shapes = [jax.ShapeDtypeStruct((nb, rows // d, d * D_A), F32) for d in dilations for _ in range(3)]
    return pl.pallas_call(
        functools.partial(_inproj_kernel, dilations=tuple(dilations)),
        grid=(nb, rows // tm),
        in_specs=[
            pl.BlockSpec((1, tm, D_MODEL), row_map),
            pl.BlockSpec((1, D_MODEL), lambda b, i: (0, 0)),
            pl.BlockSpec(mod_block, mod_map),
            pl.BlockSpec(mod_block, mod_map),
            pl.BlockSpec((D_MODEL, D_IN_R), lambda b, i: (0, 0)),
            pl.BlockSpec((1, LANES), lambda b, i: (0, 0)),
        ],
        out_specs=[pl.BlockSpec((1, tm, w), row_map) for w in widths] + dil_specs,
        out_shape=[jax.ShapeDtypeStruct((nb, rows, w), F32) for w in widths] + dil_shapes,
        compiler_params=_params(("arbitrary", "arbitrary")),
        name="inproj",
    )(x3, g_pre.reshape(1, D_MODEL), scale, shift, w_r, bias_g)


def _mlstm_kernel(u_ref, v_ref, og_ref, gc_ref, gr_ref, tail0_ref, c0_ref, n0_ref, m0_ref,
                  cw_ref, cb_ref, mh_ref,
                  mem_ref, c_out_ref, n_out_ref, m_out_ref,
                  c_s, n_s, m_s, tail_s, uext_s, *, chunk):
    L = chunk
    step = pl.program_id(1)

    @pl.when(step == 0)
    def _():
        c_s[...] = c0_ref[0]
        n_s[...] = n0_ref[0]
        m_s[...] = m0_ref[0]
        tail_s[...] = tail0_ref[0]

    u = u_ref[0]
    uext_s[0:8, :] = tail_s[...]
    uext_s[8:8 + L, :] = u
    acc = cb_ref[...] + uext_s[pl.ds(5, L), :] * cw_ref[0:1, :]
    acc = acc + uext_s[pl.ds(6, L), :] * cw_ref[1:2, :]
    acc = acc + uext_s[pl.ds(7, L), :] * cw_ref[2:3, :]
    acc = acc + u * cw_ref[3:4, :]
    tail_s[...] = uext_s[pl.ds(L, 8), :]
    qk = acc * jax.nn.sigmoid(acc)

    gc = gc_ref[0]
    gr = gr_ref[0]
    row = lax.broadcasted_iota(jnp.int32, (L, L), 0)
    col = lax.broadcasted_iota(jnp.int32, (L, L), 1)
    causal = col <= row
    tri = causal.astype(BF16)
    tri_t = (row <= col).astype(BF16)
    lane = lax.broadcasted_iota(jnp.int32, gc.shape, 1)
    srow = lax.broadcasted_iota(jnp.int32, gr.shape, 0)

    def bf16_parts(x):
        hi = x.astype(BF16)
        r1 = x - hi.astype(F32)
        mid = r1.astype(BF16)
        return hi, mid, (r1 - mid.astype(F32)).astype(BF16)

    b_col_all = sum(jnp.dot(tri, p, preferred_element_type=F32)
                    for p in bf16_parts(jnp.where(lane >= H_M, gc, 0.0)))
    b_row_all = sum(jnp.dot(p, tri_t, preferred_element_type=F32)
                    for p in bf16_parts(jnp.where(srow >= H_M, gr, 0.0)))

    v_all = v_ref[0]
    og = og_ref[0]
    for h in range(H_M):
        q = qk[:, h * DK_M:(h + 1) * DK_M]
        k = qk[:, D_M + h * DK_M:D_M + (h + 1) * DK_M] * (DK_M ** -0.5)
        v = v_all[:, h * DK_M:(h + 1) * DK_M]
        li_row = gr[h:h + 1, :]
        li_col = gc[:, h:h + 1]
        b_row = b_row_all[H_M + h:H_M + h + 1, :]
        b_col = b_col_all[:, H_M + h:H_M + h + 1]
        m0 = m_s[h:h + 1, 0:1]
        c0 = c_s[h]
        n0 = n_s[h:h + 1, :]

        d = jnp.where(causal, b_col - b_row + li_row, NEG_INF)
        g_state = b_col + m0
        m = jnp.maximum(jnp.max(d, axis=-1, keepdims=True), g_state)
        w_state = jnp.exp(g_state - m)
        qb = q.astype(BF16)
        kb = k.astype(BF16)
        vb = v.astype(BF16)
        s = lax.dot_general(qb, kb, (((1,), (1,)), ((), ())), preferred_element_type=F32)
        s = s * jnp.exp(d - m)
        num = jnp.dot(s.astype(BF16), vb, preferred_element_type=F32)
        num = num + w_state * jnp.dot(qb, c0.astype(BF16), preferred_element_type=F32)
        den = jnp.sum(s, axis=-1, keepdims=True) + w_state * jnp.sum(q * n0, axis=-1, keepdims=True)
        hh = num / jnp.maximum(jnp.abs(den), jnp.exp(-m))

        b_last = b_col[L - 1:L, :]
        g_tok_row = b_last - b_row + li_row
        m_new = jnp.maximum(b_last + m0, jnp.max(g_tok_row, axis=-1, keepdims=True))
        w_tok_col = jnp.exp(b_last - b_col + li_col - m_new)
        decay = jnp.exp(b_last + m0 - m_new)
        kw = k * w_tok_col
        c_s[h] = decay * c0 + jnp.dot(kw.T.astype(BF16), vb, preferred_element_type=F32)
        n_s[h:h + 1, :] = decay * n0 + jnp.sum(kw, axis=0, keepdims=True)
        m_s[h:h + 1, :] = jnp.broadcast_to(m_new, (1, LANES))

        hn = hh * lax.rsqrt(jnp.mean(hh * hh, axis=-1, keepdims=True) + EPS)
        hn = hn * mh_ref[:, h * DK_M:(h + 1) * DK_M]
        mem_ref[0, :, h * DK_M:(h + 1) * DK_M] = og[:, h * DK_M:(h + 1) * DK_M] * hn

    @pl.when(step == pl.num_programs(1) - 1)
    def _():
        c_out_ref[0] = c_s[...]
        n_out_ref[0] = n_s[...]
        m_out_ref[0] = m_s[...]


def _mlstm(u, v, og, gates_col, gates_row, tail0, c0, n0, m0x, conv_w, conv_b, mh_norm, chunk):
    nb, rows, _ = u.shape
    nc = rows // chunk
    row_map = lambda b, c: (b, c, 0)
    bat3 = lambda b, c: (b, 0, 0)
    const2 = lambda b, c: (0, 0)
    return pl.pallas_call(
        functools.partial(_mlstm_kernel, chunk=chunk),
        grid=(nb, nc),
        in_specs=[
            pl.BlockSpec((1, chunk, 2 * D_M), row_map),
            pl.BlockSpec((1, chunk, D_M), row_map),
            pl.BlockSpec((1, chunk, D_M), row_map),
            pl.BlockSpec((1, chunk, LANES), row_map),
            pl.BlockSpec((1, 8, chunk), lambda b, c: (b, 0, c)),
            pl.BlockSpec((1, 8, 2 * D_M), bat3),
            pl.BlockSpec((1, H_M, DK_M, DK_M), lambda b, c: (b, 0, 0, 0)),
            pl.BlockSpec((1, H_M, DK_M), bat3),
            pl.BlockSpec((1, H_M, LANES), bat3),
            pl.BlockSpec((CONV_W, 2 * D_M), const2),
            pl.BlockSpec((1, 2 * D_M), const2),
            pl.BlockSpec((1, D_M), const2),
        ],
        out_specs=[
            pl.BlockSpec((1, chunk, D_M), row_map),
            pl.BlockSpec((1, H_M, DK_M, DK_M), lambda b, c: (b, 0, 0, 0)),
            pl.BlockSpec((1, H_M, DK_M), bat3),
            pl.BlockSpec((1, H_M, LANES), bat3),
        ],
        out_shape=[
            jax.ShapeDtypeStruct((nb, rows, D_M), F32),
            jax.ShapeDtypeStruct((nb, H_M, DK_M, DK_M), F32),
            jax.ShapeDtypeStruct((nb, H_M, DK_M), F32),
            jax.ShapeDtypeStruct((nb, H_M, LANES), F32),
        ],
        scratch_shapes=[
            pltpu.VMEM((H_M, DK_M, DK_M), F32),
            pltpu.VMEM((H_M, DK_M), F32),
            pltpu.VMEM((H_M, LANES), F32),
            pltpu.VMEM((8, 2 * D_M), F32),
            pltpu.VMEM((chunk + 8, 2 * D_M), F32),
        ],
        compiler_params=_params(("arbitrary", "arbitrary")),
        name="mlstm",
    )(u, v, og, gates_col, gates_row, tail0, c0, n0, m0x,
      conv_w, conv_b.reshape(1, 2 * D_M), mh_norm.reshape(1, D_M))


def _attn_kernel(q_ref, kc_ref, vc_ref, kp_ref, vp_ref, bias_ref, ones_ref, o_ref, l_ref, k_s, v_s, *, ql):
    first = pl.program_id(2) == 0
    k_s[0:QB, :] = kp_ref[0].astype(BF16)
    k_s[QB:QB + ql, :] = kc_ref[0].astype(BF16)
    v_s[0:QB, :] = vp_ref[0].astype(BF16)
    v_s[QB:QB + ql, :] = vc_ref[0].astype(BF16)
    col = lax.broadcasted_iota(jnp.int32, (QB, 2 * QB), 1)
    pad_keys = jnp.logical_and(first, col < QB)
    head0 = lax.broadcasted_iota(jnp.int32, (1, LANES), 1) < HD_A
    n_slab = D_A // LANES
    n_res = q_ref.shape[2] // D_A
    pad_all = jnp.concatenate([pad_keys] * H_A, axis=0)

    def raw_logits(c, j):
        sl = slice(c * LANES, (c + 1) * LANES)
        q = q_ref[0, j * QB:(j + 1) * QB, sl].astype(BF16)
        zero = jnp.zeros_like(q)
        q2 = jnp.concatenate([jnp.where(head0, q, zero), jnp.where(head0, zero, q)], axis=0)
        return lax.dot_general(q2, k_s[j * QB:(j + 2) * QB, sl], (((1,), (1,)), ((), ())),
                               preferred_element_type=F32)

    def softmax(j, logits):
        logits = logits + bias_ref[...].reshape(H_A * QB, 2 * QB)
        if j == 0:
            logits = jnp.where(pad_all, NEG_INF, logits)
        m = jnp.max(logits, axis=-1, keepdims=True)
        return jnp.exp(logits - m).astype(BF16), m

    def finish(c, j, p, m):
        sl = slice(c * LANES, (c + 1) * LANES)
        rows = slice(j * QB, (j + 1) * QB)
        p_cat = jnp.concatenate([p[:QB], p[QB:]], axis=1)
        v_all = v_s[j * QB:(j + 2) * QB, sl]
        v_bd = jnp.concatenate([jnp.where(head0, v_all, jnp.zeros_like(v_all)),
                                jnp.where(head0, jnp.zeros_like(v_all), v_all)], axis=0)
        o = jnp.dot(p_cat, v_bd, preferred_element_type=F32)
        s = jnp.dot(p_cat, ones_ref[...], preferred_element_type=F32)
        o_ref[0, rows, sl] = o / s
        l_ref[0, rows, sl] = jnp.where(head0, m[:QB], m[QB:]) + jnp.log(s)

    for res in range(n_res):
        slabs = range(res * n_slab, (res + 1) * n_slab)
        for j in range(ql // QB):
            p, m = softmax(j, jnp.concatenate([raw_logits(c, j) for c in slabs], axis=0))
            for n, c in enumerate(slabs):
                part = slice(2 * n * QB, 2 * (n + 1) * QB)
                finish(c, j, p[part], m[part])


def _attn_branch(qa, ka, va, bias, dil):
    nb, ld, _ = qa.shape
    ql = min(ATT_QL, ld)
    n_res = min(dil, ATT_QL // ql)
    width = n_res * D_A
    cur = lambda b, r, i: (b, i, r)
    prev = lambda b, r, i: (b, jnp.maximum(i * (ql // QB) - 1, 0), r)
    ones_bd = np.zeros((4 * QB, LANES), np.float32)
    ones_bd[:2 * QB, :HD_A] = 1.0
    ones_bd[2 * QB:, HD_A:] = 1.0
    return pl.pallas_call(
        functools.partial(_attn_kernel, ql=ql),
        grid=(nb, dil // n_res, ld // ql),
        in_specs=[
            pl.BlockSpec((1, ql, width), cur),
            pl.BlockSpec((1, ql, width), cur),
            pl.BlockSpec((1, ql, width), cur),
            pl.BlockSpec((1, QB, width), prev),
            pl.BlockSpec((1, QB, width), prev),
            pl.BlockSpec((H_A, QB, 2 * QB), lambda b, r, i: (0, 0, 0)),
            pl.BlockSpec((4 * QB, LANES), lambda b, r, i: (0, 0)),
        ],
        out_specs=[pl.BlockSpec((1, ql, width), cur), pl.BlockSpec((1, ql, width), cur)],
        out_shape=[jax.ShapeDtypeStruct((nb, ld, dil * D_A), F32)] * 2,
        scratch_shapes=[pltpu.VMEM((QB + ql, width), BF16), pltpu.VMEM((QB + ql, width), BF16)],
        compiler_params=_params(("arbitrary", "arbitrary", "arbitrary")),
        name=f"attn_d{dil}",
    )(qa, ka, va, ka, va, bias, jnp.asarray(ones_bd, BF16))


def _heads_kernel(k_ref, v_ref, ko_ref, vo_ref):
    for src, dst in ((k_ref, ko_ref), (v_ref, vo_ref)):
        for c in range(D_A // LANES):
            xt = src[0, :, c * LANES:(c + 1) * LANES].T
            dst[0, 2 * c] = xt[:HD_A]
            dst[0, 2 * c + 1] = xt[HD_A:]


def _window_heads(ka, va, keep, tm=512):
    nb, seq, _ = ka.shape
    first = (seq - keep) // tm
    in_spec = pl.BlockSpec((1, tm, D_A), lambda b, i: (b, first + i, 0))
    out_spec = pl.BlockSpec((1, H_A, HD_A, tm), lambda b, i: (b, 0, 0, i))
    kt, vt = pl.pallas_call(
        _heads_kernel,
        grid=(nb, keep // tm),
        in_specs=[in_spec, in_spec],
        out_specs=[out_spec, out_spec],
        out_shape=[jax.ShapeDtypeStruct((nb, H_A, HD_A, keep), F32)] * 2,
        compiler_params=_params(("arbitrary", "arbitrary")),
        name="window_heads",
    )(ka, va)
    return kt.transpose(0, 3, 1, 2), vt.transpose(0, 3, 1, 2)


def _t5_bucket(dist):
    max_exact = N_BUCKETS // 2
    n = jnp.maximum(dist, 1).astype(F32)
    large = max_exact + (jnp.log(n / max_exact) / math.log(MAX_DISTANCE / max_exact)
                         * (N_BUCKETS - max_exact)).astype(jnp.int32)
    large = jnp.minimum(large, N_BUCKETS - 1)
    return jnp.where(dist < max_exact, dist, large)


def _branch_bias_vec(rel_bias, window, dil):
    return rel_bias[_t5_bucket(dil * jnp.arange(window // dil + 1))].astype(F32)


def _prompt_bias(rel_bias, window, dil):
    assert window // dil == QB
    n = 2 * QB
    vec = _branch_bias_vec(rel_bias, window, dil)
    vecp = jnp.concatenate([vec[::-1], jnp.full((n - QB - 1, H_A), NEG_INF, F32)], axis=0)
    g = jnp.roll(vecp, QB - 1, axis=0)
    skew = jnp.tile(g, (QB + 1, 1))[:QB * (n + 1)].reshape(QB, n + 1, H_A)[:, :n]
    return skew[::-1].transpose(2, 0, 1)


TOK_PAD = 8


def _sattn_kernel(q_ref, kt_ref, vt_ref, knt_ref, vnt_ref, bc_ref, bn_ref, att_ref):
    nt = (((1,), (1,)), ((), ()))
    for h in range(H_A):
        q = q_ref[0, h].astype(BF16)
        lc = jnp.dot(q, kt_ref[0, h].astype(BF16), preferred_element_type=F32)
        ln = jnp.dot(q, knt_ref[0, h].astype(BF16), preferred_element_type=F32)
        lcs = [lc + bc_ref[d, h] for d in range(len(BRANCHES))]
        lns = [ln + bn_ref[d, h] for d in range(len(BRANCHES))]
        m = functools.reduce(jnp.maximum, [jnp.max(x, axis=-1, keepdims=True) for x in lcs + lns])
        pc = functools.reduce(jnp.add, [jnp.exp(x - m) for x in lcs])
        pn = functools.reduce(jnp.add, [jnp.exp(x - m) for x in lns])
        s = jnp.sum(pc, axis=-1, keepdims=True) + jnp.sum(pn, axis=-1, keepdims=True)
        o = lax.dot_general(pc.astype(BF16), vt_ref[0, h].astype(BF16), nt, preferred_element_type=F32)
        o = o + lax.dot_general(pn.astype(BF16), vnt_ref[0, h].astype(BF16), nt, preferred_element_type=F32)
        att_ref[0, h] = o / s


def _sample_attention(q_h, kt, vt, knt, vnt, bias_c, bias_n):
    nb, _, _, n_past = kt.shape
    bat = lambda b: (b, 0, 0, 0)
    const = lambda b: (0, 0, 0, 0)
    return pl.pallas_call(
        _sattn_kernel,
        grid=(nb,),
        in_specs=[
            pl.BlockSpec((1, H_A, TOK_PAD, HD_A), bat),
            pl.BlockSpec((1, H_A, HD_A, n_past), bat),
            pl.BlockSpec((1, H_A, HD_A, n_past), bat),
            pl.BlockSpec((1, H_A, HD_A, LANES), bat),
            pl.BlockSpec((1, H_A, HD_A, LANES), bat),
            pl.BlockSpec(bias_c.shape, const),
            pl.BlockSpec(bias_n.shape, const),
        ],
        out_specs=pl.BlockSpec((1, H_A, TOK_PAD, HD_A), bat),
        out_shape=jax.ShapeDtypeStruct((nb, H_A, TOK_PAD, HD_A), F32),
        compiler_params=_params(("arbitrary",)),
        name="sample_attn",
    )(q_h, kt, vt, knt, vnt, bias_c, bias_n)


def _sample_bias(rel_bias, n_tok, n_past):
    ninf = lambda n: jnp.full((n, H_A), NEG_INF, F32)
    cached, new = [], []
    for window, dil in BRANCHES:
        nk = window // dil
        rows = n_past // dil
        assert n_past % dil == 0 and rows >= nk
        vec = _branch_bias_vec(rel_bias, window, dil)
        res = jnp.arange(dil)[None, :, None]
        tc, tn = [], []
        for t in range(n_tok):
            near = t // dil + 1
            col = jnp.concatenate([ninf(rows - (nk + 1 - near)), vec[near:nk + 1][::-1]], axis=0)
            tab = jnp.where(res == t % dil, col[:, None, :], NEG_INF).reshape(n_past, H_A)
            tc.append(tab)
            keys = [vec[(t - j) // dil][None] if (t - j) % dil == 0 and 0 <= t - j <= nk * dil else ninf(1)
                    for j in range(n_tok)]
            tn.append(jnp.concatenate(keys + [ninf(LANES - n_tok)], axis=0))
        pad = lambda tabs: jnp.stack(tabs + [jnp.zeros_like(tabs[0])] * (TOK_PAD - n_tok))
        cached.append(pad(tc).transpose(2, 0, 1))
        new.append(pad(tn).transpose(2, 0, 1))
    return jnp.stack(cached), jnp.stack(new)


def _postmix_kernel(mem_ref, o1_ref, l1_ref, o2_ref, l2_ref, o3_ref, l3_ref, x_ref,
                    g1_ref, sc2_ref, sh2_ref, gpost_ref, gpre_ref, wo_ref, wr_ref, br_ref, cnt0_ref,
                    x1_ref, h2_ref, idx_ref, gate_ref, cnt_ref, cnt_s, *, dilations):
    first_step = jnp.logical_and(pl.program_id(0) == 0, pl.program_id(1) == 0)

    @pl.when(first_step)
    def _():
        cnt_s[...] = cnt0_ref[...]

    rows = lambda ref, dil: ref[0] if dil == 1 else _from_dilated(ref[0], dil)
    o1, o2, o3 = (rows(r, d) for r, d in zip((o1_ref, o2_ref, o3_ref), dilations))
    l1, l2, l3 = (rows(r, d) for r, d in zip((l1_ref, l2_ref, l3_ref), dilations))
    lmax = jnp.maximum(jnp.maximum(l1, l2), l3)
    w1 = jnp.exp(l1 - lmax)
    w2 = jnp.exp(l2 - lmax)
    w3 = jnp.exp(l3 - lmax)
    att = (w1 * o1 + w2 * o2 + w3 * o3) / (w1 + w2 + w3)
    y = jnp.dot(mem_ref[0].astype(BF16), wo_ref[0:D_M, :], preferred_element_type=F32)
    y = y + jnp.dot(att.astype(BF16), wo_ref[D_M:, :], preferred_element_type=F32)
    x1 = x_ref[0] + g1_ref[0] * _rms(y, gpost_ref[...])
    x1_ref[0] = x1
    h2 = _rms(x1, gpre_ref[...]) * (1.0 + sc2_ref[0]) + sh2_ref[0]
    h2_ref[0] = h2
    h_hi = h2.astype(BF16)
    h_lo = (h2 - h_hi.astype(F32)).astype(BF16)
    w_hi = wr_ref[0]
    logits = jnp.dot(h_hi, w_hi, preferred_element_type=F32)
    logits = logits + jnp.dot(h_lo, w_hi, preferred_element_type=F32)
    logits = logits + jnp.dot(h_hi, wr_ref[1], preferred_element_type=F32) + br_ref[...]
    lane = lax.broadcasted_iota(jnp.int32, logits.shape, 1)
    lane_f = lane.astype(F32)
    logits = jnp.where(lane < N_EXPERTS, logits, NEG_INF)
    vals, idxs = [], []
    for _ in range(TOP_K):
        mx = jnp.max(logits, axis=-1, keepdims=True)
        ix = jnp.min(jnp.where(logits == mx, lane_f, float(LANES)), axis=-1, keepdims=True)
        vals.append(mx)
        idxs.append(ix)
        logits = jnp.where(lane_f == ix, NEG_INF, logits)
    es = [jnp.exp(v - vals[0]) for v in vals]
    tot = es[0] + es[1] + es[2] + es[3]
    tm = logits.shape[0]
    picks = [(lane_f == ix).astype(F32) for ix in idxs]
    chosen = picks[0] + picks[1] + picks[2] + picks[3]
    r_i = lax.broadcasted_iota(jnp.int32, (tm, tm), 0)
    c_i = lax.broadcasted_iota(jnp.int32, (tm, tm), 1)
    earlier = jnp.dot((c_i < r_i).astype(BF16), chosen.astype(BF16), preferred_element_type=F32)
    before = earlier + cnt_s[...]
    cnt_s[...] = cnt_s[...] + jnp.sum(chosen, axis=0, keepdims=True)
    cnt_ref[...] = cnt_s[...]
    idx_tile = jnp.zeros(logits.shape, F32)
    gate_tile = jnp.zeros(logits.shape, F32)
    for k in range(TOP_K):
        rank = jnp.sum(picks[k] * before, axis=-1, keepdims=True)
        idx_tile = jnp.where(lane == k, idxs[k], idx_tile)
        idx_tile = jnp.where(lane == TOP_K + k, rank, idx_tile)
        gate_tile = jnp.where(lane == k, es[k] / tot, gate_tile)
    idx_ref[0] = idx_tile.astype(jnp.int32)
    gate_ref[0] = gate_tile


def _postmix(mem, branches, x3, gate1, scale2, shift2, g_post, g_pre, w_out_b, w_router_p, b_router_p, counts0, tm,
             dilations=(1, 1, 1)):
    nb, rows, _ = x3.shape
    per_row_mod = gate1.shape[1] != 1
    mod_block = (1, tm, D_MODEL) if per_row_mod else (1, 1, D_MODEL)
    mod_map = (lambda b, i: (b, i, 0)) if per_row_mod else (lambda b, i: (b, 0, 0))
    row_map = lambda b, i: (b, i, 0)
    const2 = lambda b, i: (0, 0)
    half = pl.BlockSpec((1, tm, D_M), row_map)
    flat = [a for pair in branches for a in pair]
    branch_specs = [pl.BlockSpec((1, tm // d, d * D_A), row_map) for d in dilations for _ in range(2)]
    return pl.pallas_call(
        functools.partial(_postmix_kernel, dilations=tuple(dilations)),
        grid=(nb, rows // tm),
        in_specs=[half] + branch_specs + [
            pl.BlockSpec((1, tm, D_MODEL), row_map),
            pl.BlockSpec(mod_block, mod_map),
            pl.BlockSpec(mod_block, mod_map),
            pl.BlockSpec(mod_block, mod_map),
            pl.BlockSpec((1, D_MODEL), const2),
            pl.BlockSpec((1, D_MODEL), const2),
            pl.BlockSpec((D_MODEL, D_MODEL), const2),
            pl.BlockSpec((2, D_MODEL, LANES), lambda b, i: (0, 0, 0)),
            pl.BlockSpec((1, LANES), const2),
            pl.BlockSpec((1, LANES), const2),
        ],
        out_specs=[
            pl.BlockSpec((1, tm, D_MODEL), row_map),
            pl.BlockSpec((1, tm, D_MODEL), row_map),
            pl.BlockSpec((1, tm, LANES), row_map),
            pl.BlockSpec((1, tm, LANES), row_map),
            pl.BlockSpec((1, LANES), const2),
        ],
        out_shape=[
            jax.ShapeDtypeStruct((nb, rows, D_MODEL), F32),
            jax.ShapeDtypeStruct((nb, rows, D_MODEL), F32),
            jax.ShapeDtypeStruct((nb, rows, LANES), jnp.int32),
            jax.ShapeDtypeStruct((nb, rows, LANES), F32),
            jax.ShapeDtypeStruct((1, LANES), F32),
        ],
        scratch_shapes=[pltpu.VMEM((1, LANES), F32)],
        compiler_params=_params(("arbitrary", "arbitrary")),
        name="postmix",
    )(mem, *flat, x3, gate1, scale2, shift2, g_post.reshape(1, D_MODEL), g_pre.reshape(1, D_MODEL),
      w_out_b, w_router_p, b_router_p, counts0)


def _row_wait(src_rows, dst_rows, sem):
    pltpu.make_async_copy(src_rows, dst_rows, sem).wait()


def _scatter_kernel(slot_ref, hp_ref, hs_ref, xs_hbm, buf, sem, *, n_prompt_tiles):
    h = jnp.where(pl.program_id(0) >= n_prompt_tiles, hs_ref[...], hp_ref[...])
    bits = pltpu.bitcast(h.astype(BF16).astype(F32), U32)
    packed = (bits[:, :PACK_W] & U32(HI_MASK)) | (bits[:, PACK_W:] >> 16)
    buf[...] = pltpu.einshape("m(jc)->mjc", packed, j=PACK_T)

    def issue(r, carry):
        src = buf.at[pl.ds(r, 1)]
        for k in range(TOP_K):
            pltpu.make_async_copy(src, xs_hbm.at[pl.ds(slot_ref[0, 0, r * TOP_K + k], 1)],
                                  sem).start(priority=k % 2)
        return carry

    lax.fori_loop(0, TOK_TILE, issue, 0, unroll=4)
    for _ in range(TOP_K):
        _row_wait(buf, xs_hbm.at[pl.ds(0, TOK_TILE)], sem)


def _moe_scatter(h_p, h_s, slot_tiles):
    tiles = slot_tiles.shape[0]
    n_p = h_p.shape[0] // TOK_TILE
    assert tiles == n_p + 1 and h_s.shape[0] == TOK_TILE
    n_rows = tiles * TOK_TILE * TOP_K
    return pl.pallas_call(
        functools.partial(_scatter_kernel, n_prompt_tiles=n_p),
        grid=(tiles,),
        in_specs=[
            pl.BlockSpec((1, 1, TOK_TILE * TOP_K), lambda i: (i, 0, 0), memory_space=pltpu.SMEM),
            pl.BlockSpec((TOK_TILE, D_MODEL), lambda i: (jnp.minimum(i, n_p - 1), 0)),
            pl.BlockSpec((TOK_TILE, D_MODEL), lambda i: (0, 0)),
        ],
        out_specs=pl.BlockSpec(memory_space=pl.ANY),
        out_shape=jax.ShapeDtypeStruct((n_rows, PACK_T, LANES), U32),
        scratch_shapes=[pltpu.VMEM((TOK_TILE, PACK_T, LANES), U32), pltpu.SemaphoreType.DMA],
        compiler_params=_params(("arbitrary",)),
        name="moe_scatter",
    )(slot_tiles, h_p, h_s)


def _expert_kernel(tile_ref, exp_ref, next_ref, lo_ref, hi_ref, first_ref, nvis_ref,
                   xs_ref, w1_hbm, b1_ref, w2_hbm, b2_ref, ys_ref, w1f, w2f, w1b, w2b, sem):
    v = pl.program_id(0)
    active = v < nvis_ref[0]
    changed = jnp.logical_or(v == 0, exp_ref[v] != exp_ref[jnp.maximum(v - 1, 0)])
    lo = lo_ref[v]
    hi = hi_ref[v]

    def weight_copies(e):
        return (pltpu.make_async_copy(w1_hbm.at[e], w1f, sem.at[0]),
                pltpu.make_async_copy(w2_hbm.at[e], w2f, sem.at[1]))

    @pl.when(v == 0)
    def _():
        for c in weight_copies(exp_ref[0]):
            c.start()

    @pl.when(jnp.logical_and(active, changed))
    def _():
        for c in weight_copies(exp_ref[v]):
            c.wait()
        w1b[...] = w1f[...].astype(BF16)
        w2b[...] = w2f[...].astype(BF16)

        @pl.when(next_ref[v] >= 0)
        def _():
            for c in weight_copies(next_ref[v]):
                c.start()

    def mlp_rows(r0):
        w = pltpu.einshape("mjc->m(jc)", xs_ref[r0:r0 + MOE_SUB])
        x_hi = pltpu.bitcast(w & U32(HI_MASK), F32).astype(BF16)
        x_lo = pltpu.bitcast(w << 16, F32).astype(BF16)
        hgu = jnp.dot(x_hi, w1b[0:PACK_W, :], preferred_element_type=F32)
        hgu = hgu + jnp.dot(x_lo, w1b[PACK_W:, :], preferred_element_type=F32) + b1_ref[0]
        x_glu = jnp.minimum(hgu[:, :D_FF], SWIGLU_LIMIT)
        x_lin = jnp.clip(hgu[:, D_FF:], -SWIGLU_LIMIT, SWIGLU_LIMIT)
        act = x_glu * jax.nn.sigmoid(SWIGLU_ALPHA * x_glu) * (x_lin + 1.0)
        y = jnp.dot(act.astype(BF16), w2b[...], preferred_element_type=F32) + b2_ref[0]
        return pltpu.einshape("m(jc)->mjc", y, j=ROW_T)

    subs = [r * MOE_SUB for r in range(MOE_M // MOE_SUB)]
    whole = jnp.logical_and(lo == 0, hi == MOE_M)

    @pl.when(jnp.logical_and(active, whole))
    def _():
        for r0 in subs:
            ys_ref[r0:r0 + MOE_SUB] = mlp_rows(r0)

    shared = jnp.logical_and(active, jnp.logical_not(whole))

    @pl.when(jnp.logical_and(shared, first_ref[v] == 1))
    def _():
        ys_ref[...] = jnp.zeros(ys_ref.shape, F32)

    for r0 in subs:
        @pl.when(jnp.logical_and(shared, jnp.logical_and(lo < r0 + MOE_SUB, hi > r0)))
        def _():
            rows = r0 + lax.broadcasted_iota(jnp.int32, (MOE_SUB, 1, 1), 0)
            mine = jnp.logical_and(rows >= lo, rows < hi)
            ys_ref[r0:r0 + MOE_SUB] = jnp.where(mine, mlp_rows(r0), ys_ref[r0:r0 + MOE_SUB])


def _moe_experts(xs, visits, w1, b1, w2, b2):
    n_rows = xs.shape[0]
    n_vis = visits[0].shape[0]
    tile_map = lambda v, tile, *_: (tile[v], 0, 0)
    exp_map = lambda v, tile, exp, *_: (exp[v], 0, 0)
    return pl.pallas_call(
        _expert_kernel,
        grid_spec=pltpu.PrefetchScalarGridSpec(
            num_scalar_prefetch=7,
            grid=(n_vis,),
            in_specs=[
                pl.BlockSpec((MOE_M, PACK_T, LANES), tile_map),
                pl.BlockSpec(memory_space=pl.ANY),
                pl.BlockSpec((1, 1, 2 * D_FF), exp_map),
                pl.BlockSpec(memory_space=pl.ANY),
                pl.BlockSpec((1, 1, D_MODEL), exp_map),
            ],
            out_specs=pl.BlockSpec((MOE_M, ROW_T, LANES), tile_map),
            scratch_shapes=[pltpu.VMEM((D_MODEL, 2 * D_FF), F32), pltpu.VMEM((D_FF, D_MODEL), F32),
                            pltpu.VMEM((D_MODEL, 2 * D_FF), BF16), pltpu.VMEM((D_FF, D_MODEL), BF16),
                            pltpu.SemaphoreType.DMA((2,))],
        ),
        out_shape=jax.ShapeDtypeStruct((n_rows, ROW_T, LANES), F32),
        compiler_params=_params(("arbitrary",)),
        name="moe_experts",
    )(*visits, xs, w1, b1.reshape(N_EXPERTS, 1, 2 * D_FF), w2, b2.reshape(N_EXPERTS, 1, D_MODEL))


def _combine_kernel(slot_ref, next_ref, gatep_ref, gates_ref, x1p_ref, x1s_ref, g2p_ref, g2s_ref, gpost_ref, ys_hbm,
                    yp_ref, ysm_ref, buf, sem, *, n_prompt_tiles):
    i = pl.program_id(0)
    last = pl.num_programs(0) - 1
    cur = lax.rem(i, 2)

    def gather(slots, b):
        def issue(r, carry):
            for k in range(TOP_K):
                pltpu.make_async_copy(ys_hbm.at[pl.ds(slots[0, 0, r * TOP_K + k], 1)],
                                      buf.at[b, pl.ds(k * TOK_TILE + r, 1)], sem.at[b]).start(priority=k % 2)
            return carry

        lax.fori_loop(0, TOK_TILE, issue, 0, unroll=4)

    @pl.when(i == 0)
    def _():
        gather(slot_ref, cur)

    @pl.when(i < last)
    def _():
        gather(next_ref, 1 - cur)

    for k in range(TOP_K):
        _row_wait(ys_hbm.at[pl.ds(0, TOK_TILE)], buf.at[cur, pl.ds(k * TOK_TILE, TOK_TILE)], sem.at[cur])
    is_sample = i >= n_prompt_tiles
    g = jnp.where(is_sample, gates_ref[...], gatep_ref[...])
    f = jnp.zeros((TOK_TILE, D_MODEL), F32)
    for k in range(TOP_K):
        rows = pltpu.einshape("mjc->m(jc)", buf[cur, pl.ds(k * TOK_TILE, TOK_TILE)])
        f = f + rows * g[:, k:k + 1]
    x1 = jnp.where(is_sample, x1s_ref[...], x1p_ref[...])
    g2 = jnp.where(is_sample, g2s_ref[...], g2p_ref[0])
    y = x1 + g2 * _rms(f, gpost_ref[...])

    @pl.when(jnp.logical_not(is_sample))
    def _():
        yp_ref[...] = y

    @pl.when(is_sample)
    def _():
        ysm_ref[...] = y


def _moe_combine(ys, slot_tiles, gates_p, gates_s, x1_p, x1_s, gate2_p, gate2_s, g_post, rows_per_batch):
    tiles = slot_tiles.shape[0]
    n_p = x1_p.shape[0] // TOK_TILE
    assert tiles == n_p + 1 and x1_s.shape[0] == TOK_TILE
    tiles_per_batch = rows_per_batch // TOK_TILE
    p_tile = lambda i: (jnp.minimum(i, n_p - 1), 0)
    const2 = lambda i: (0, 0)
    return pl.pallas_call(
        functools.partial(_combine_kernel, n_prompt_tiles=n_p),
        grid=(tiles,),
        in_specs=[
            pl.BlockSpec((1, 1, TOK_TILE * TOP_K), lambda i: (i, 0, 0), memory_space=pltpu.SMEM),
            pl.BlockSpec((1, 1, TOK_TILE * TOP_K), lambda i: (jnp.minimum(i + 1, tiles - 1), 0, 0),
                         memory_space=pltpu.SMEM),
            pl.BlockSpec((TOK_TILE, LANES), p_tile),
            pl.BlockSpec((TOK_TILE, LANES), const2),
            pl.BlockSpec((TOK_TILE, D_MODEL), p_tile),
            pl.BlockSpec((TOK_TILE, D_MODEL), const2),
            pl.BlockSpec((1, 1, D_MODEL), lambda i: (jnp.minimum(i, n_p - 1) // tiles_per_batch, 0, 0)),
            pl.BlockSpec((TOK_TILE, D_MODEL), const2),
            pl.BlockSpec((1, D_MODEL), const2),
            pl.BlockSpec(memory_space=pl.ANY),
        ],
        out_specs=[pl.BlockSpec((TOK_TILE, D_MODEL), p_tile), pl.BlockSpec((TOK_TILE, D_MODEL), const2)],
        out_shape=[jax.ShapeDtypeStruct(x1_p.shape, F32), jax.ShapeDtypeStruct(x1_s.shape, F32)],
        scratch_shapes=[pltpu.VMEM((2, TOP_K * TOK_TILE, ROW_T, LANES), F32), pltpu.SemaphoreType.DMA((2,))],
        compiler_params=_params(("arbitrary",)),
        name="moe_combine",
    )(slot_tiles, slot_tiles, gates_p, gates_s, x1_p, x1_s, gate2_p, gate2_s, g_post.reshape(1, D_MODEL), ys)


def _moe_plan(routes, counts):
    a = sum(r.shape[0] for r in routes) * TOP_K
    assert a % MOE_M == 0
    g_end = jnp.cumsum(counts)
    g_start = g_end - counts

    def slots(route):
        flat_e = route[:, :TOP_K].reshape(-1)
        rank = route[:, TOP_K:2 * TOP_K].reshape(-1)
        onehot = flat_e[:, None] == jnp.arange(N_EXPERTS)[None, :]
        return (jnp.sum(jnp.where(onehot, g_start[None, :], 0), axis=1) + rank).astype(jnp.int32)

    slot_of = jnp.concatenate([slots(r) for r in routes])

    n_tiles = a // MOE_M
    n_vis = n_tiles + N_EXPERTS - 1
    first_tile = g_start // MOE_M
    last_tile = jnp.where(counts > 0, (g_end - 1) // MOE_M, first_tile)
    nvis_e = jnp.where(counts > 0, last_tile - first_tile + 1, 0)
    v_end = jnp.cumsum(nvis_e)
    v_start = v_end - nvis_e
    total = v_end[-1]
    v = jnp.arange(n_vis)
    vc = jnp.minimum(v, total - 1)
    e_of = jnp.minimum(jnp.sum((vc[:, None] >= v_end[None, :]).astype(jnp.int32), axis=1), N_EXPERTS - 1)
    is_e = e_of[:, None] == jnp.arange(N_EXPERTS)[None, :]
    of_e = lambda t: jnp.sum(jnp.where(is_e, t[None, :], 0), axis=1)
    tile_of = of_e(first_tile) + (vc - of_e(v_start))
    lo = jnp.maximum(of_e(g_start), tile_of * MOE_M) - tile_of * MOE_M
    hi = jnp.minimum(of_e(g_end), (tile_of + 1) * MOE_M) - tile_of * MOE_M
    prev_tile = jnp.concatenate([jnp.full((1,), -1, tile_of.dtype), tile_of[:-1]])
    first = (tile_of != prev_tile).astype(jnp.int32)
    experts = jnp.arange(N_EXPERTS)
    later = (experts[None, :] > experts[:, None]) & (counts[None, :] > 0)
    next_e = jnp.min(jnp.where(later, experts[None, :], N_EXPERTS), axis=1)
    next_e = jnp.where(next_e < N_EXPERTS, next_e, -1)
    i32 = lambda t: t.astype(jnp.int32)
    visits = (i32(tile_of), i32(e_of), i32(of_e(next_e)), i32(lo), i32(hi), first, i32(total).reshape(1))
    return slot_of, visits


def _reorder_w_in(w_in):
    off_om_end = 4 * D_M
    off_g_end = off_om_end + 2 * H_M
    w = jnp.concatenate([
        w_in[:, :off_om_end],
        w_in[:, off_g_end:],
        w_in[:, off_om_end:off_g_end],
        jnp.zeros((D_MODEL, LANES - 2 * H_M), w_in.dtype),
    ], axis=1)
    return w.astype(BF16)


def kernel(x_prompt, x_sample, cache_k, cache_v, state_conv, state_C, state_n, state_m, c_prompt, c_sample, w_ada, b_ada, g_pre_mix, g_post_mix, g_pre_ffn, g_post_ffn, w_in, b_ig, b_fg, conv_w, conv_b, mh_norm, rel_bias, w_out, w_router, b_router, w1, b1, w2, b2):
    depth = w_in.shape[0]
    assert depth == 1
    l = 0
    nb_p, seq, _ = x_prompt.shape
    nb_s, n_tok, _ = x_sample.shape
    rows_s = nb_s * n_tok
    assert rows_s == TOK_TILE

    n_c = nb_p + nb_s
    c_pad = -(-n_c // 8) * 8
    c_all = jnp.concatenate([c_prompt, c_sample, jnp.zeros((c_pad - n_c, D_MODEL), F32)], axis=0)
    mod = _adaln(c_all, w_ada[l], b_ada[l])
    mods = [mod[:, i * D_MODEL:(i + 1) * D_MODEL] for i in range(6)]
    mods_p = [m[:nb_p].reshape(nb_p, 1, D_MODEL) for m in mods]
    mods_s = [jnp.repeat(m[nb_p:n_c], n_tok, axis=0).reshape(1, rows_s, D_MODEL) for m in mods]

    w_r = _reorder_w_in(w_in[l])
    bias_g = jnp.concatenate([b_ig[l], b_fg[l], jnp.zeros((LANES - 2 * H_M,), F32)]).reshape(1, LANES)
    w_out_b = w_out[l].astype(BF16)
    w_router_f = jnp.concatenate([w_router[l], jnp.zeros((D_MODEL, LANES - N_EXPERTS), F32)], axis=1)
    w_router_hi = w_router_f.astype(BF16)
    w_router_p = jnp.stack([w_router_hi, (w_router_f - w_router_hi.astype(F32)).astype(BF16)])
    b_router_p = jnp.concatenate([b_router[l].astype(F32), jnp.zeros((LANES - N_EXPERTS,), F32)]).reshape(1, LANES)

    shift1, scale1, gate1, shift2, scale2, gate2_p = mods_p
    dils = tuple(d for _, d in BRANCHES)
    assert dils[0] == 1
    u, v_m, og, gts, qa, ka, va, *dilated = _inproj(x_prompt, g_pre_mix[l], scale1, shift1, w_r, bias_g, tm=512,
                                                   dilations=dils[1:])
    qkv_d = [(qa, ka, va)] + [tuple(dilated[3 * n:3 * n + 3]) for n in range(len(dils) - 1)]
    gates_row = gts[:, :, :8].transpose(0, 2, 1)
    zeros = lambda *s: jnp.zeros(s, F32)
    mem, c_p, n_p, m_p = _mlstm(u, v_m, og, gts, gates_row, zeros(nb_p, 8, 2 * D_M),
                                zeros(nb_p, H_M, DK_M, DK_M), zeros(nb_p, H_M, DK_M), zeros(nb_p, H_M, LANES),
                                conv_w[l], conv_b[l], mh_norm[l], MLSTM_L_PROMPT)
    branches = [_attn_branch(*qkv, _prompt_bias(rel_bias, w, d), d) for qkv, (w, d) in zip(qkv_d, BRANCHES)]
    x1_p, h2_p, idx_p, gate_p, cnt_p = _postmix(mem, branches, x_prompt, gate1, scale2, shift2,
                                                g_post_mix[l], g_pre_ffn[l], w_out_b, w_router_p, b_router_p,
                                                zeros(1, LANES), tm=512, dilations=dils)
    keep = min(WINDOW_MAX, seq)
    k_p, v_p = _window_heads(ka, va, keep)
    conv_p = u[:, seq - (CONV_W - 1):]

    shift1, scale1, gate1, shift2, scale2, gate2_s = mods_s
    x_s = x_sample.reshape(1, rows_s, D_MODEL)
    u_s, v_s, og_s, gts_s, qa_s, ka_s, va_s = _inproj(x_s, g_pre_mix[l], scale1, shift1, w_r, bias_g, tm=rows_s)
    per_b = lambda t: t.reshape(nb_s, n_tok, t.shape[-1])
    ls = MLSTM_L_SAMPLE
    pad_rows = lambda t, n: jnp.concatenate([t, jnp.zeros((nb_s, n - t.shape[1], t.shape[2]), t.dtype)], axis=1)
    inert = jnp.concatenate([jnp.full((H_M,), NEG_INF, F32), jnp.zeros((LANES - H_M,), F32)])
    gts_pad = jnp.concatenate([per_b(gts_s), jnp.broadcast_to(inert, (nb_s, ls - n_tok, LANES))], axis=1)
    tail0 = jnp.concatenate([zeros(nb_s, 8 - (CONV_W - 1), 2 * D_M), state_conv[l].astype(F32)], axis=1)
    m0x = jnp.broadcast_to(state_m[l].astype(F32)[:, :, None], (nb_s, H_M, LANES))
    mem_s, c_s, n_s, m_s = _mlstm(pad_rows(per_b(u_s), ls), pad_rows(per_b(v_s), ls), pad_rows(per_b(og_s), ls),
                                  gts_pad, gts_pad[:, :, :8].transpose(0, 2, 1), tail0,
                                  state_C[l].astype(F32), state_n[l].astype(F32), m0x,
                                  conv_w[l], conv_b[l], mh_norm[l], ls)
    mem_s = mem_s[:, :n_tok].reshape(1, rows_s, D_M)
    heads = lambda t: t.reshape(nb_s, n_tok, H_A, HD_A)
    k_new, v_new = heads(ka_s), heads(va_s)
    n_past = cache_k.shape[2]
    assert n_tok <= TOK_PAD
    q_h = jnp.pad(heads(qa_s).transpose(0, 2, 1, 3), ((0, 0), (0, 0), (0, TOK_PAD - n_tok), (0, 0)))
    new_t = lambda t: jnp.pad(t.transpose(0, 2, 3, 1), ((0, 0), (0, 0), (0, 0), (0, LANES - n_tok)))
    cache_t = lambda c: c[l].astype(F32).transpose(0, 2, 3, 1)
    att_s = _sample_attention(q_h, cache_t(cache_k), cache_t(cache_v), new_t(k_new), new_t(v_new),
                              *_sample_bias(rel_bias, n_tok, n_past))
    att_s = att_s[:, :, :n_tok].transpose(0, 2, 1, 3).reshape(1, rows_s, D_A)
    zero_o = zeros(1, rows_s, D_A)
    ninf = jnp.full((1, rows_s, D_A), NEG_INF, F32)
    branches_s = [(att_s, zero_o), (zero_o, ninf), (zero_o, ninf)]
    x1_s, h2_s, idx_s, gate_s, cnt_all = _postmix(mem_s, branches_s, x_s, gate1, scale2, shift2,
                                                  g_post_mix[l], g_pre_ffn[l], w_out_b, w_router_p, b_router_p,
                                                  cnt_p, tm=rows_s)

    t_p = nb_p * seq
    slot_of, visits = _moe_plan([idx_p.reshape(t_p, LANES), idx_s.reshape(rows_s, LANES)],
                                cnt_all[0, :N_EXPERTS].astype(jnp.int32))
    slot_tiles = slot_of.reshape((t_p + rows_s) // TOK_TILE, 1, TOK_TILE * TOP_K)
    xs = _moe_scatter(h2_p.reshape(t_p, D_MODEL), h2_s.reshape(rows_s, D_MODEL), slot_tiles)
    ys = _moe_experts(xs, visits, w1[l], b1[l], w2[l], b2[l])
    y_p, y_s = _moe_combine(ys, slot_tiles, gate_p.reshape(t_p, LANES), gate_s.reshape(rows_s, LANES), x1_p.reshape(t_p, D_MODEL), x1_s.reshape(rows_s, D_MODEL),
                            gate2_p, gate2_s.reshape(rows_s, D_MODEL), g_post_ffn[l], seq)
    y_prompt = y_p.reshape(nb_p, seq, D_MODEL)
    y_sample = y_s.reshape(nb_s, n_tok, D_MODEL)

    st = lambda t: t[None]
    conv_s = jnp.concatenate([state_conv[l].astype(F32), per_b(u_s)], axis=1)[:, -(CONV_W - 1):]
    return (y_prompt, y_sample, st(k_p), st(v_p), st(conv_p), st(c_p), st(n_p), st(m_p[:, :, 0]),
            st(k_new), st(v_new), st(conv_s), st(c_s), st(n_s), st(m_s[:, :, 0]))
```

```python
import functools
import math

import jax
import jax.numpy as jnp
import numpy as np
from jax import lax
from jax.experimental import pallas as pl
from jax.experimental.pallas import tpu as pltpu

F32 = jnp.float32
BF16 = jnp.bfloat16
U32 = jnp.uint32
NEG_INF = float("-inf")

D_MODEL = 1024
H_M = 4
D_M = 512
DK_M = 128
CONV_W = 4
H_A = 8
D_A = 512
HD_A = 64
BRANCHES = ((128, 1), (512, 4), (2048, 16))
WINDOW_MAX = 2048
QB = 128
N_BUCKETS = 32
MAX_DISTANCE = 2048
N_EXPERTS = 32
TOP_K = 4
D_FF = 1024
SWIGLU_LIMIT = 7.0
SWIGLU_ALPHA = 1.702
EPS = 1e-6

LANES = 128
C_U, C_V, C_O, C_QA, C_KA, C_VA, C_G = 0, 1024, 1536, 2048, 2560, 3072, 3584
D_IN_R = C_G + LANES

MLSTM_L_PROMPT = 256
MLSTM_L_SAMPLE = 32
ATT_QL = 1024
MOE_M = 512
MOE_SUB = 256
TOK_TILE = 128
PACK_W = D_MODEL // 2
PACK_T = PACK_W // LANES
ROW_T = D_MODEL // LANES
HI_MASK = 0xFFFF0000
VMEM_LIMIT = 56 * 1024 * 1024


def _params(sem, vmem=None):
    return pltpu.CompilerParams(dimension_semantics=sem, vmem_limit_bytes=vmem or VMEM_LIMIT)


def _rms(x, g):
    return x * lax.rsqrt(jnp.mean(x * x, axis=-1, keepdims=True) + EPS) * g


def _log_sigmoid(t):
    return jnp.minimum(t, 0.0) - jnp.log1p(jnp.exp(-jnp.abs(t)))


def _ada_kernel(c_ref, w_ref, b_ref, o_ref):
    c = c_ref[...]
    a = (c * jax.nn.sigmoid(c)).astype(BF16)
    o_ref[...] = jnp.dot(a, w_ref[...].astype(BF16), preferred_element_type=F32) + b_ref[...]


def _adaln(c_all, w_ada, b_ada):
    rows = c_all.shape[0]
    n = w_ada.shape[1]
    tn = 1024
    return pl.pallas_call(
        _ada_kernel,
        grid=(n // tn,),
        in_specs=[
            pl.BlockSpec((rows, D_MODEL), lambda j: (0, 0)),
            pl.BlockSpec((D_MODEL, tn), lambda j: (0, j)),
            pl.BlockSpec((1, tn), lambda j: (0, j)),
        ],
        out_specs=pl.BlockSpec((rows, tn), lambda j: (0, j)),
        out_shape=jax.ShapeDtypeStruct((rows, n), F32),
        compiler_params=_params(("arbitrary",)),
        name="adaln",
    )(c_all, w_ada, b_ada.reshape(1, n))


def _inproj_kernel(x_ref, g_ref, sc_ref, sh_ref, w_ref, bg_ref,
                   u_ref, v_ref, o_ref, gt_ref, qa_ref, ka_ref, va_ref, *rest, dilations):
    dil_refs, slab_s = (rest[:-1], rest[-1]) if dilations else (rest, None)
    x = x_ref[0]
    h = _rms(x, g_ref[...]) * (1.0 + sc_ref[0]) + sh_ref[0]
    hb = h.astype(BF16)

    def seg(lo, width):
        return jnp.dot(hb, w_ref[:, lo:lo + width], preferred_element_type=F32)

    u_ref[0] = seg(C_U, 2 * D_M)
    v_ref[0] = seg(C_V, D_M)
    o_ref[0] = jax.nn.sigmoid(seg(C_O, D_M))
    qkv = (seg(C_QA, D_A) * (HD_A ** -0.5), seg(C_KA, D_A), seg(C_VA, D_A))
    for ref, val in zip((qa_ref, ka_ref, va_ref), qkv):
        ref[0] = val
    n_slab = D_A // LANES
    rows = x.shape[0]
    for j, val in enumerate(qkv if dilations else ()):
        for c in range(n_slab):
            slab_s[j, c] = val[:, c * LANES:(c + 1) * LANES]
        for n, dil in enumerate(dilations):
            for r in range(dil):
                for c in range(n_slab):
                    lo = r * D_A + c * LANES
                    dil_refs[3 * n + j][0, :, lo:lo + LANES] = slab_s[j, c, pl.ds(r, rows // dil, stride=dil), :]
    t = seg(C_G, LANES) + bg_ref[...]
    lane = lax.broadcasted_iota(jnp.int32, t.shape, 1)
    gt_ref[0] = jnp.where(lane < H_M, t, _log_sigmoid(t))


def _inproj(x3, g_pre, scale, shift, w_r, bias_g, tm, dilations=()):
    nb, rows, _ = x3.shape
    per_row_mod = scale.shape[1] != 1
    mod_block = (1, tm, D_MODEL) if per_row_mod else (1, 1, D_MODEL)
    mod_map = (lambda b, i: (b, i, 0)) if per_row_mod else (lambda b, i: (b, 0, 0))
    row_map = lambda b, i: (b, i, 0)
    widths = (2 * D_M, D_M, D_M, LANES, D_A, D_A, D_A)
    dil_specs = [pl.BlockSpec((1, tm // d, d * D_A), row_map) for d in dilations for _ in range(3)]
    dil_shapes = [jax.ShapeDtypeStruct((nb, rows // d, d * D_A), F32) for d in dilations for _ in range(3)]
    return pl.pallas_call(
        functools.partial(_inproj_kernel, dilations=tuple(dilations)),
        grid=(nb, rows // tm),
        in_specs=[
            pl.BlockSpec((1, tm, D_MODEL), row_map),
            pl.BlockSpec((1, D_MODEL), lambda b, i: (0, 0)),
            pl.BlockSpec(mod_block, mod_map),
            pl.BlockSpec(mod_block, mod_map),
            pl.BlockSpec((D_MODEL, D_IN_R), lambda b, i: (0, 0)),
            pl.BlockSpec((1, LANES), lambda b, i: (0, 0)),
        ],
        out_specs=[pl.BlockSpec((1, tm, w), row_map) for w in widths] + dil_specs,
        out_shape=[jax.ShapeDtypeStruct((nb, rows, w), F32) for w in widths] + dil_shapes,
        scratch_shapes=[pltpu.VMEM((3, D_A // LANES, tm, LANES), F32)] if dilations else [],
        compiler_params=_params(("arbitrary", "arbitrary")),
        name="inproj",
    )(x3, g_pre.reshape(1, D_MODEL), scale, shift, w_r, bias_g)


def _mlstm_kernel(u_ref, v_ref, og_ref, gc_ref, gr_ref, tail0_ref, c0_ref, n0_ref, m0_ref,
                  cw_ref, cb_ref, mh_ref,
                  mem_ref, c_out_ref, n_out_ref, m_out_ref,
                  c_s, n_s, m_s, tail_s, uext_s, *, chunk):
    L = chunk
    step = pl.program_id(1)

    @pl.when(step == 0)
    def _():
        c_s[...] = c0_ref[0]
        n_s[...] = n0_ref[0]
        m_s[...] = m0_ref[0]
        tail_s[...] = tail0_ref[0]

    u = u_ref[0]
    uext_s[0:8, :] = tail_s[...]
    uext_s[8:8 + L, :] = u
    acc = cb_ref[...] + uext_s[pl.ds(5, L), :] * cw_ref[0:1, :]
    acc = acc + uext_s[pl.ds(6, L), :] * cw_ref[1:2, :]
    acc = acc + uext_s[pl.ds(7, L), :] * cw_ref[2:3, :]
    acc = acc + u * cw_ref[3:4, :]
    tail_s[...] = uext_s[pl.ds(L, 8), :]
    qk = acc * jax.nn.sigmoid(acc)

    gc = gc_ref[0]
    gr = gr_ref[0]
    row = lax.broadcasted_iota(jnp.int32, (L, L), 0)
    col = lax.broadcasted_iota(jnp.int32, (L, L), 1)
    causal = col <= row
    tri = causal.astype(BF16)
    tri_t = (row <= col).astype(BF16)
    lane = lax.broadcasted_iota(jnp.int32, gc.shape, 1)
    srow = lax.broadcasted_iota(jnp.int32, gr.shape, 0)

    def bf16_parts(x):
        hi = x.astype(BF16)
        r1 = x - hi.astype(F32)
        mid = r1.astype(BF16)
        return hi, mid, (r1 - mid.astype(F32)).astype(BF16)

    b_col_all = sum(jnp.dot(tri, p, preferred_element_type=F32)
                    for p in bf16_parts(jnp.where(lane >= H_M, gc, 0.0)))
    b_row_all = sum(jnp.dot(p, tri_t, preferred_element_type=F32)
                    for p in bf16_parts(jnp.where(srow >= H_M, gr, 0.0)))

    v_all = v_ref[0]
    og = og_ref[0]
    for h in range(H_M):
        q = qk[:, h * DK_M:(h + 1) * DK_M]
        k = qk[:, D_M + h * DK_M:D_M + (h + 1) * DK_M] * (DK_M ** -0.5)
        v = v_all[:, h * DK_M:(h + 1) * DK_M]
        li_row = gr[h:h + 1, :]
        li_col = gc[:, h:h + 1]
        b_row = b_row_all[H_M + h:H_M + h + 1, :]
        b_col = b_col_all[:, H_M + h:H_M + h + 1]
        m0 = m_s[h:h + 1, 0:1]
        c0 = c_s[h]
        n0 = n_s[h:h + 1, :]

        d = jnp.where(causal, b_col - b_row + li_row, NEG_INF)
        g_state = b_col + m0
        m = jnp.maximum(jnp.max(d, axis=-1, keepdims=True), g_state)
        w_state = jnp.exp(g_state - m)
        qb = q.astype(BF16)
        kb = k.astype(BF16)
        vb = v.astype(BF16)
        s = lax.dot_general(qb, kb, (((1,), (1,)), ((), ())), preferred_element_type=F32)
        s = s * jnp.exp(d - m)
        num = jnp.dot(s.astype(BF16), vb, preferred_element_type=F32)
        num = num + w_state * jnp.dot(qb, c0.astype(BF16), preferred_element_type=F32)
        den = jnp.sum(s, axis=-1, keepdims=True) + w_state * jnp.sum(q * n0, axis=-1, keepdims=True)
        hh = num / jnp.maximum(jnp.abs(den), jnp.exp(-m))

        b_last = b_col[L - 1:L, :]
        g_tok_row = b_last - b_row + li_row
        m_new = jnp.maximum(b_last + m0, jnp.max(g_tok_row, axis=-1, keepdims=True))
        w_tok_col = jnp.exp(b_last - b_col + li_col - m_new)
        decay = jnp.exp(b_last + m0 - m_new)
        kw = k * w_tok_col
        c_s[h] = decay * c0 + jnp.dot(kw.T.astype(BF16), vb, preferred_element_type=F32)
        n_s[h:h + 1, :] = decay * n0 + jnp.sum(kw, axis=0, keepdims=True)
        m_s[h:h + 1, :] = jnp.broadcast_to(m_new, (1, LANES))

        hn = hh * lax.rsqrt(jnp.mean(hh * hh, axis=-1, keepdims=True) + EPS)
        hn = hn * mh_ref[:, h * DK_M:(h + 1) * DK_M]
        mem_ref[0, :, h * DK_M:(h + 1) * DK_M] = og[:, h * DK_M:(h + 1) * DK_M] * hn

    @pl.when(step == pl.num_programs(1) - 1)
    def _():
        c_out_ref[0] = c_s[...]
        n_out_ref[0] = n_s[...]
        m_out_ref[0] = m_s[...]


def _mlstm(u, v, og, gates_col, gates_row, tail0, c0, n0, m0x, conv_w, conv_b, mh_norm, chunk):
    nb, rows, _ = u.shape
    nc = rows // chunk
    row_map = lambda b, c: (b, c, 0)
    bat3 = lambda b, c: (b, 0, 0)
    const2 = lambda b, c: (0, 0)
    return pl.pallas_call(
        functools.partial(_mlstm_kernel, chunk=chunk),
        grid=(nb, nc),
        in_specs=[
            pl.BlockSpec((1, chunk, 2 * D_M), row_map),
            pl.BlockSpec((1, chunk, D_M), row_map),
            pl.BlockSpec((1, chunk, D_M), row_map),
            pl.BlockSpec((1, chunk, LANES), row_map),
            pl.BlockSpec((1, 8, chunk), lambda b, c: (b, 0, c)),
            pl.BlockSpec((1, 8, 2 * D_M), bat3),
            pl.BlockSpec((1, H_M, DK_M, DK_M), lambda b, c: (b, 0, 0, 0)),
            pl.BlockSpec((1, H_M, DK_M), bat3),
            pl.BlockSpec((1, H_M, LANES), bat3),
            pl.BlockSpec((CONV_W, 2 * D_M), const2),
            pl.BlockSpec((1, 2 * D_M), const2),
            pl.BlockSpec((1, D_M), const2),
        ],
        out_specs=[
            pl.BlockSpec((1, chunk, D_M), row_map),
            pl.BlockSpec((1, H_M, DK_M, DK_M), lambda b, c: (b, 0, 0, 0)),
            pl.BlockSpec((1, H_M, DK_M), bat3),
            pl.BlockSpec((1, H_M, LANES), bat3),
        ],
        out_shape=[
            jax.ShapeDtypeStruct((nb, rows, D_M), F32),
            jax.ShapeDtypeStruct((nb, H_M, DK_M, DK_M), F32),
            jax.ShapeDtypeStruct((nb, H_M, DK_M), F32),
            jax.ShapeDtypeStruct((nb, H_M, LANES), F32),
        ],
        scratch_shapes=[
            pltpu.VMEM((H_M, DK_M, DK_M), F32),
            pltpu.VMEM((H_M, DK_M), F32),
            pltpu.VMEM((H_M, LANES), F32),
            pltpu.VMEM((8, 2 * D_M), F32),
            pltpu.VMEM((chunk + 8, 2 * D_M), F32),
        ],
        compiler_params=_params(("arbitrary", "arbitrary")),
        name="mlstm",
    )(u, v, og, gates_col, gates_row, tail0, c0, n0, m0x,
      conv_w, conv_b.reshape(1, 2 * D_M), mh_norm.reshape(1, D_M))


def _attn_kernel(q_ref, kc_ref, vc_ref, kp_ref, vp_ref, bias_ref, ones_ref, o_ref, l_ref, k_s, v_s, *, ql):
    first = pl.program_id(2) == 0
    k_s[0:QB, :] = kp_ref[0].astype(BF16)
    k_s[QB:QB + ql, :] = kc_ref[0].astype(BF16)
    v_s[0:QB, :] = vp_ref[0].astype(BF16)
    v_s[QB:QB + ql, :] = vc_ref[0].astype(BF16)
    col = lax.broadcasted_iota(jnp.int32, (QB, 2 * QB), 1)
    pad_keys = jnp.logical_and(first, col < QB)
    head0 = lax.broadcasted_iota(jnp.int32, (1, LANES), 1) < HD_A
    n_slab = D_A // LANES
    n_res = q_ref.shape[2] // D_A
    pad_all = jnp.concatenate([pad_keys] * H_A, axis=0)

    def raw_logits(c, j):
        sl = slice(c * LANES, (c + 1) * LANES)
        q = q_ref[0, j * QB:(j + 1) * QB, sl].astype(BF16)
        zero = jnp.zeros_like(q)
        q2 = jnp.concatenate([jnp.where(head0, q, zero), jnp.where(head0, zero, q)], axis=0)
        return lax.dot_general(q2, k_s[j * QB:(j + 2) * QB, sl], (((1,), (1,)), ((), ())),
                               preferred_element_type=F32)

    def softmax(j, logits):
        logits = logits + bias_ref[...].reshape(H_A * QB, 2 * QB)
        if j == 0:
            logits = jnp.where(pad_all, NEG_INF, logits)
        m = jnp.max(logits, axis=-1, keepdims=True)
        return jnp.exp(logits - m).astype(BF16), m

    def finish(c, j, p, m):
        sl = slice(c * LANES, (c + 1) * LANES)
        rows = slice(j * QB, (j + 1) * QB)
        p_cat = jnp.concatenate([p[:QB], p[QB:]], axis=1)
        v_all = v_s[j * QB:(j + 2) * QB, sl]
        v_bd = jnp.concatenate([jnp.where(head0, v_all, jnp.zeros_like(v_all)),
                                jnp.where(head0, jnp.zeros_like(v_all), v_all)], axis=0)
        o = jnp.dot(p_cat, v_bd, preferred_element_type=F32)
        s = jnp.dot(p_cat, ones_ref[...], preferred_element_type=F32)
        o_ref[0, rows, sl] = o / s
        l_ref[0, rows, sl] = jnp.where(head0, m[:QB], m[QB:]) + jnp.log(s)

    for res in range(n_res):
        slabs = range(res * n_slab, (res + 1) * n_slab)
        for j in range(ql // QB):
            p, m = softmax(j, jnp.concatenate([raw_logits(c, j) for c in slabs], axis=0))
            for n, c in enumerate(slabs):
                part = slice(2 * n * QB, 2 * (n + 1) * QB)
                finish(c, j, p[part], m[part])


def _attn_branch(qa, ka, va, bias, dil):
    nb, ld, _ = qa.shape
    ql = min(ATT_QL, ld)
    n_res = min(dil, ATT_QL // ql)
    width = n_res * D_A
    cur = lambda b, r, i: (b, i, r)
    prev = lambda b, r, i: (b, jnp.maximum(i * (ql // QB) - 1, 0), r)
    ones_bd = np.zeros((4 * QB, LANES), np.float32)
    ones_bd[:2 * QB, :HD_A] = 1.0
    ones_bd[2 * QB:, HD_A:] = 1.0
    return pl.pallas_call(
        functools.partial(_attn_kernel, ql=ql),
        grid=(nb, dil // n_res, ld // ql),
        in_specs=[
            pl.BlockSpec((1, ql, width), cur),
            pl.BlockSpec((1, ql, width), cur),
            pl.BlockSpec((1, ql, width), cur),
            pl.BlockSpec((1, QB, width), prev),
            pl.BlockSpec((1, QB, width), prev),
            pl.BlockSpec((H_A, QB, 2 * QB), lambda b, r, i: (0, 0, 0)),
            pl.BlockSpec((4 * QB, LANES), lambda b, r, i: (0, 0)),
        ],
        out_specs=[pl.BlockSpec((1, ql, width), cur), pl.BlockSpec((1, ql, width), cur)],
        out_shape=[jax.ShapeDtypeStruct((nb, ld, dil * D_A), F32)] * 2,
        scratch_shapes=[pltpu.VMEM((QB + ql, width), BF16), pltpu.VMEM((QB + ql, width), BF16)],
        compiler_params=_params(("arbitrary", "arbitrary", "arbitrary")),
        name=f"attn_d{dil}",
    )(qa, ka, va, ka, va, bias, jnp.asarray(ones_bd, BF16))


def _heads_kernel(k_ref, v_ref, ko_ref, vo_ref):
    for src, dst in ((k_ref, ko_ref), (v_ref, vo_ref)):
        for c in range(D_A // LANES):
            xt = src[0, :, c * LANES:(c + 1) * LANES].T
            dst[0, 2 * c] = xt[:HD_A]
            dst[0, 2 * c + 1] = xt[HD_A:]


def _window_heads(ka, va, keep, tm=512):
    nb, seq, _ = ka.shape
    first = (seq - keep) // tm
    in_spec = pl.BlockSpec((1, tm, D_A), lambda b, i: (b, first + i, 0))
    out_spec = pl.BlockSpec((1, H_A, HD_A, tm), lambda b, i: (b, 0, 0, i))
    kt, vt = pl.pallas_call(
        _heads_kernel,
        grid=(nb, keep // tm),
        in_specs=[in_spec, in_spec],
        out_specs=[out_spec, out_spec],
        out_shape=[jax.ShapeDtypeStruct((nb, H_A, HD_A, keep), F32)] * 2,
        compiler_params=_params(("arbitrary", "arbitrary")),
        name="window_heads",
    )(ka, va)
    return kt.transpose(0, 3, 1, 2), vt.transpose(0, 3, 1, 2)


def _t5_bucket(dist):
    max_exact = N_BUCKETS // 2
    n = jnp.maximum(dist, 1).astype(F32)
    large = max_exact + (jnp.log(n / max_exact) / math.log(MAX_DISTANCE / max_exact)
                         * (N_BUCKETS - max_exact)).astype(jnp.int32)
    large = jnp.minimum(large, N_BUCKETS - 1)
    return jnp.where(dist < max_exact, dist, large)


def _branch_bias_vec(rel_bias, window, dil):
    return rel_bias[_t5_bucket(dil * jnp.arange(window // dil + 1))].astype(F32)


def _prompt_bias(rel_bias, window, dil):
    assert window // dil == QB
    n = 2 * QB
    vec = _branch_bias_vec(rel_bias, window, dil)
    vecp = jnp.concatenate([vec[::-1], jnp.full((n - QB - 1, H_A), NEG_INF, F32)], axis=0)
    g = jnp.roll(vecp, QB - 1, axis=0)
    skew = jnp.tile(g, (QB + 1, 1))[:QB * (n + 1)].reshape(QB, n + 1, H_A)[:, :n]
    return skew[::-1].transpose(2, 0, 1)


TOK_PAD = 8


def _sattn_kernel(q_ref, kt_ref, vt_ref, knt_ref, vnt_ref, bc_ref, bn_ref, att_ref):
    nt = (((1,), (1,)), ((), ()))
    for h in range(H_A):
        q = q_ref[0, h].astype(BF16)
        lc = jnp.dot(q, kt_ref[0, h].astype(BF16), preferred_element_type=F32)
        ln = jnp.dot(q, knt_ref[0, h].astype(BF16), preferred_element_type=F32)
        lcs = [lc + bc_ref[d, h] for d in range(len(BRANCHES))]
        lns = [ln + bn_ref[d, h] for d in range(len(BRANCHES))]
        m = functools.reduce(jnp.maximum, [jnp.max(x, axis=-1, keepdims=True) for x in lcs + lns])
        pc = functools.reduce(jnp.add, [jnp.exp(x - m) for x in lcs])
        pn = functools.reduce(jnp.add, [jnp.exp(x - m) for x in lns])
        s = jnp.sum(pc, axis=-1, keepdims=True) + jnp.sum(pn, axis=-1, keepdims=True)
        o = lax.dot_general(pc.astype(BF16), vt_ref[0, h].astype(BF16), nt, preferred_element_type=F32)
        o = o + lax.dot_general(pn.astype(BF16), vnt_ref[0, h].astype(BF16), nt, preferred_element_type=F32)
        att_ref[0, h] = o / s


def _sample_attention(q_h, kt, vt, knt, vnt, bias_c, bias_n):
    nb, _, _, n_past = kt.shape
    bat = lambda b: (b, 0, 0, 0)
    const = lambda b: (0, 0, 0, 0)
    return pl.pallas_call(
        _sattn_kernel,
        grid=(nb,),
        in_specs=[
            pl.BlockSpec((1, H_A, TOK_PAD, HD_A), bat),
            pl.BlockSpec((1, H_A, HD_A, n_past), bat),
            pl.BlockSpec((1, H_A, HD_A, n_past), bat),
            pl.BlockSpec((1, H_A, HD_A, LANES), bat),
            pl.BlockSpec((1, H_A, HD_A, LANES), bat),
            pl.BlockSpec(bias_c.shape, const),
            pl.BlockSpec(bias_n.shape, const),
        ],
        out_specs=pl.BlockSpec((1, H_A, TOK_PAD, HD_A), bat),
        out_shape=jax.ShapeDtypeStruct((nb, H_A, TOK_PAD, HD_A), F32),
        compiler_params=_params(("arbitrary",)),
        name="sample_attn",
    )(q_h, kt, vt, knt, vnt, bias_c, bias_n)


def _sample_bias(rel_bias, n_tok, n_past):
    ninf = lambda n: jnp.full((n, H_A), NEG_INF, F32)
    cached, new = [], []
    for window, dil in BRANCHES:
        nk = window // dil
        rows = n_past // dil
        assert n_past % dil == 0 and rows >= nk
        vec = _branch_bias_vec(rel_bias, window, dil)
        res = jnp.arange(dil)[None, :, None]
        tc, tn = [], []
        for t in range(n_tok):
            near = t // dil + 1
            col = jnp.concatenate([ninf(rows - (nk + 1 - near)), vec[near:nk + 1][::-1]], axis=0)
            tab = jnp.where(res == t % dil, col[:, None, :], NEG_INF).reshape(n_past, H_A)
            tc.append(tab)
            keys = [vec[(t - j) // dil][None] if (t - j) % dil == 0 and 0 <= t - j <= nk * dil else ninf(1)
                    for j in range(n_tok)]
            tn.append(jnp.concatenate(keys + [ninf(LANES - n_tok)], axis=0))
        pad = lambda tabs: jnp.stack(tabs + [jnp.zeros_like(tabs[0])] * (TOK_PAD - n_tok))
        cached.append(pad(tc).transpose(2, 0, 1))
        new.append(pad(tn).transpose(2, 0, 1))
    return jnp.stack(cached), jnp.stack(new)


def _postmix_kernel(mem_ref, o1_ref, l1_ref, o2_ref, l2_ref, o3_ref, l3_ref, x_ref,
                    g1_ref, sc2_ref, sh2_ref, gpost_ref, gpre_ref, wo_ref, wr_ref, br_ref, cnt0_ref,
                    x1_ref, h2_ref, idx_ref, gate_ref, cnt_ref, cnt_s, slab_s, *, dilations):
    first_step = jnp.logical_and(pl.program_id(0) == 0, pl.program_id(1) == 0)

    @pl.when(first_step)
    def _():
        cnt_s[...] = cnt0_ref[...]

    n_slab = D_A // LANES
    tm = x_ref.shape[1]

    def rows(ref, dil, slot):
        if dil == 1:
            return ref[0]
        for r in range(dil):
            for c in range(n_slab):
                lo = r * D_A + c * LANES
                slab_s[slot, c, pl.ds(r, tm // dil, stride=dil), :] = ref[0, :, lo:lo + LANES]
        return jnp.concatenate([slab_s[slot, c] for c in range(n_slab)], axis=1)

    o1, o2, o3 = (rows(r, d, n) for n, (r, d) in enumerate(zip((o1_ref, o2_ref, o3_ref), dilations)))
    l1, l2, l3 = (rows(r, d, 3 + n) for n, (r, d) in enumerate(zip((l1_ref, l2_ref, l3_ref), dilations)))
    lmax = jnp.maximum(jnp.maximum(l1, l2), l3)
    w1 = jnp.exp(l1 - lmax)
    w2 = jnp.exp(l2 - lmax)
    w3 = jnp.exp(l3 - lmax)
    att = (w1 * o1 + w2 * o2 + w3 * o3) / (w1 + w2 + w3)
    y = jnp.dot(mem_ref[0].astype(BF16), wo_ref[0:D_M, :], preferred_element_type=F32)
    y = y + jnp.dot(att.astype(BF16), wo_ref[D_M:, :], preferred_element_type=F32)
    x1 = x_ref[0] + g1_ref[0] * _rms(y, gpost_ref[...])
    x1_ref[0] = x1
    h2 = _rms(x1, gpre_ref[...]) * (1.0 + sc2_ref[0]) + sh2_ref[0]
    h2_ref[0] = h2
    h_hi = h2.astype(BF16)
    h_lo = (h2 - h_hi.astype(F32)).astype(BF16)
    w_hi = wr_ref[0]
    logits = jnp.dot(h_hi, w_hi, preferred_element_type=F32)
    logits = logits + jnp.dot(h_lo, w_hi, preferred_element_type=F32)
    logits = logits + jnp.dot(h_hi, wr_ref[1], preferred_element_type=F32) + br_ref[...]
    lane = lax.broadcasted_iota(jnp.int32, logits.shape, 1)
    lane_f = lane.astype(F32)
    logits = jnp.where(lane < N_EXPERTS, logits, NEG_INF)
    vals, idxs = [], []
    for _ in range(TOP_K):
        mx = jnp.max(logits, axis=-1, keepdims=True)
        ix = jnp.min(jnp.where(logits == mx, lane_f, float(LANES)), axis=-1, keepdims=True)
        vals.append(mx)
        idxs.append(ix)
        logits = jnp.where(lane_f == ix, NEG_INF, logits)
    es = [jnp.exp(v - vals[0]) for v in vals]
    tot = es[0] + es[1] + es[2] + es[3]
    tm = logits.shape[0]
    picks = [(lane_f == ix).astype(F32) for ix in idxs]
    chosen = picks[0] + picks[1] + picks[2] + picks[3]
    r_i = lax.broadcasted_iota(jnp.int32, (tm, tm), 0)
    c_i = lax.broadcasted_iota(jnp.int32, (tm, tm), 1)
    earlier = jnp.dot((c_i < r_i).astype(BF16), chosen.astype(BF16), preferred_element_type=F32)
    before = earlier + cnt_s[...]
    cnt_s[...] = cnt_s[...] + jnp.sum(chosen, axis=0, keepdims=True)
    cnt_ref[...] = cnt_s[...]
    idx_tile = jnp.zeros(logits.shape, F32)
    gate_tile = jnp.zeros(logits.shape, F32)
    for k in range(TOP_K):
        rank = jnp.sum(picks[k] * before, axis=-1, keepdims=True)
        idx_tile = jnp.where(lane == k, idxs[k], idx_tile)
        idx_tile = jnp.where(lane == TOP_K + k, rank, idx_tile)
        gate_tile = jnp.where(lane == k, es[k] / tot, gate_tile)
    idx_ref[0] = idx_tile.astype(jnp.int32)
    gate_ref[0] = gate_tile


def _postmix(mem, branches, x3, gate1, scale2, shift2, g_post, g_pre, w_out_b, w_router_p, b_router_p, counts0, tm,
             dilations=(1, 1, 1)):
    nb, rows, _ = x3.shape
    per_row_mod = gate1.shape[1] != 1
    mod_block = (1, tm, D_MODEL) if per_row_mod else (1, 1, D_MODEL)
    mod_map = (lambda b, i: (b, i, 0)) if per_row_mod else (lambda b, i: (b, 0, 0))
    row_map = lambda b, i: (b, i, 0)
    const2 = lambda b, i: (0, 0)
    half = pl.BlockSpec((1, tm, D_M), row_map)
    flat = [a for pair in branches for a in pair]
    branch_specs = [pl.BlockSpec((1, tm // d, d * D_A), row_map) for d in dilations for _ in range(2)]
    return pl.pallas_call(
        functools.partial(_postmix_kernel, dilations=tuple(dilations)),
        grid=(nb, rows // tm),
        in_specs=[half] + branch_specs + [
            pl.BlockSpec((1, tm, D_MODEL), row_map),
            pl.BlockSpec(mod_block, mod_map),
            pl.BlockSpec(mod_block, mod_map),
            pl.BlockSpec(mod_block, mod_map),
            pl.BlockSpec((1, D_MODEL), const2),
            pl.BlockSpec((1, D_MODEL), const2),
            pl.BlockSpec((D_MODEL, D_MODEL), const2),
            pl.BlockSpec((2, D_MODEL, LANES), lambda b, i: (0, 0, 0)),
            pl.BlockSpec((1, LANES), const2),
            pl.BlockSpec((1, LANES), const2),
        ],
        out_specs=[
            pl.BlockSpec((1, tm, D_MODEL), row_map),
            pl.BlockSpec((1, tm, D_MODEL), row_map),
            pl.BlockSpec((1, tm, LANES), row_map),
            pl.BlockSpec((1, tm, LANES), row_map),
            pl.BlockSpec((1, LANES), const2),
        ],
        out_shape=[
            jax.ShapeDtypeStruct((nb, rows, D_MODEL), F32),
            jax.ShapeDtypeStruct((nb, rows, D_MODEL), F32),
            jax.ShapeDtypeStruct((nb, rows, LANES), jnp.int32),
            jax.ShapeDtypeStruct((nb, rows, LANES), F32),
            jax.ShapeDtypeStruct((1, LANES), F32),
        ],
        scratch_shapes=[pltpu.VMEM((1, LANES), F32), pltpu.VMEM((6, D_A // LANES, tm, LANES), F32)],
        compiler_params=_params(("arbitrary", "arbitrary")),
        name="postmix",
    )(mem, *flat, x3, gate1, scale2, shift2, g_post.reshape(1, D_MODEL), g_pre.reshape(1, D_MODEL),
      w_out_b, w_router_p, b_router_p, counts0)


def _row_wait(src_rows, dst_rows, sem):
    pltpu.make_async_copy(src_rows, dst_rows, sem).wait()


def _scatter_kernel(slot_ref, hp_ref, hs_ref, xs_hbm, buf, sem, *, n_prompt_tiles):
    h = jnp.where(pl.program_id(0) >= n_prompt_tiles, hs_ref[...], hp_ref[...])
    bits = pltpu.bitcast(h.astype(BF16).astype(F32), U32)
    packed = (bits[:, :PACK_W] & U32(HI_MASK)) | (bits[:, PACK_W:] >> 16)
    buf[...] = pltpu.einshape("m(jc)->mjc", packed, j=PACK_T)

    def issue(r, carry):
        src = buf.at[pl.ds(r, 1)]
        for k in range(TOP_K):
            pltpu.make_async_copy(src, xs_hbm.at[pl.ds(slot_ref[0, 0, r * TOP_K + k], 1)],
                                  sem).start(priority=k % 2)
        return carry

    lax.fori_loop(0, TOK_TILE, issue, 0, unroll=4)
    for _ in range(TOP_K):
        _row_wait(buf, xs_hbm.at[pl.ds(0, TOK_TILE)], sem)


def _moe_scatter(h_p, h_s, slot_tiles):
    tiles = slot_tiles.shape[0]
    n_p = h_p.shape[0] // TOK_TILE
    assert tiles == n_p + 1 and h_s.shape[0] == TOK_TILE
    n_rows = tiles * TOK_TILE * TOP_K
    return pl.pallas_call(
        functools.partial(_scatter_kernel, n_prompt_tiles=n_p),
        grid=(tiles,),
        in_specs=[
            pl.BlockSpec((1, 1, TOK_TILE * TOP_K), lambda i: (i, 0, 0), memory_space=pltpu.SMEM),
            pl.BlockSpec((TOK_TILE, D_MODEL), lambda i: (jnp.minimum(i, n_p - 1), 0)),
            pl.BlockSpec((TOK_TILE, D_MODEL), lambda i: (0, 0)),
        ],
        out_specs=pl.BlockSpec(memory_space=pl.ANY),
        out_shape=jax.ShapeDtypeStruct((n_rows, PACK_T, LANES), U32),
        scratch_shapes=[pltpu.VMEM((TOK_TILE, PACK_T, LANES), U32), pltpu.SemaphoreType.DMA],
        compiler_params=_params(("arbitrary",)),
        name="moe_scatter",
    )(slot_tiles, h_p, h_s)


def _expert_kernel(tile_ref, exp_ref, next_ref, lo_ref, hi_ref, first_ref, nvis_ref,
                   xs_ref, w1_hbm, b1_ref, w2_hbm, b2_ref, ys_ref, w1f, w2f, w1b, w2b, sem):
    v = pl.program_id(0)
    active = v < nvis_ref[0]
    changed = jnp.logical_or(v == 0, exp_ref[v] != exp_ref[jnp.maximum(v - 1, 0)])
    lo = lo_ref[v]
    hi = hi_ref[v]

    def weight_copies(e):
        return (pltpu.make_async_copy(w1_hbm.at[e], w1f, sem.at[0]),
                pltpu.make_async_copy(w2_hbm.at[e], w2f, sem.at[1]))

    @pl.when(v == 0)
    def _():
        for c in weight_copies(exp_ref[0]):
            c.start()

    @pl.when(jnp.logical_and(active, changed))
    def _():
        for c in weight_copies(exp_ref[v]):
            c.wait()
        w1b[...] = w1f[...].astype(BF16)
        w2b[...] = w2f[...].astype(BF16)

        @pl.when(next_ref[v] >= 0)
        def _():
            for c in weight_copies(next_ref[v]):
                c.start()

    def mlp_rows(r0):
        w = pltpu.einshape("mjc->m(jc)", xs_ref[r0:r0 + MOE_SUB])
        x_hi = pltpu.bitcast(w & U32(HI_MASK), F32).astype(BF16)
        x_lo = pltpu.bitcast(w << 16, F32).astype(BF16)
        hgu = jnp.dot(x_hi, w1b[0:PACK_W, :], preferred_element_type=F32)
        hgu = hgu + jnp.dot(x_lo, w1b[PACK_W:, :], preferred_element_type=F32) + b1_ref[0]
        x_glu = jnp.minimum(hgu[:, :D_FF], SWIGLU_LIMIT)
        x_lin = jnp.clip(hgu[:, D_FF:], -SWIGLU_LIMIT, SWIGLU_LIMIT)
        act = x_glu * jax.nn.sigmoid(SWIGLU_ALPHA * x_glu) * (x_lin + 1.0)
        y = jnp.dot(act.astype(BF16), w2b[...], preferred_element_type=F32) + b2_ref[0]
        return pltpu.einshape("m(jc)->mjc", y, j=ROW_T)

    subs = [r * MOE_SUB for r in range(MOE_M // MOE_SUB)]
    whole = jnp.logical_and(lo == 0, hi == MOE_M)

    @pl.when(jnp.logical_and(active, whole))
    def _():
        for r0 in subs:
            ys_ref[r0:r0 + MOE_SUB] = mlp_rows(r0)

    shared = jnp.logical_and(active, jnp.logical_not(whole))

    @pl.when(jnp.logical_and(shared, first_ref[v] == 1))
    def _():
        ys_ref[...] = jnp.zeros(ys_ref.shape, F32)

    for r0 in subs:
        @pl.when(jnp.logical_and(shared, jnp.logical_and(lo < r0 + MOE_SUB, hi > r0)))
        def _():
            rows = r0 + lax.broadcasted_iota(jnp.int32, (MOE_SUB, 1, 1), 0)
            mine = jnp.logical_and(rows >= lo, rows < hi)
            ys_ref[r0:r0 + MOE_SUB] = jnp.where(mine, mlp_rows(r0), ys_ref[r0:r0 + MOE_SUB])


def _moe_experts(xs, visits, w1, b1, w2, b2):
    n_rows = xs.shape[0]
    n_vis = visits[0].shape[0]
    tile_map = lambda v, tile, *_: (tile[v], 0, 0)
    exp_map = lambda v, tile, exp, *_: (exp[v], 0, 0)
    return pl.pallas_call(
        _expert_kernel,
        grid_spec=pltpu.PrefetchScalarGridSpec(
            num_scalar_prefetch=7,
            grid=(n_vis,),
            in_specs=[
                pl.BlockSpec((MOE_M, PACK_T, LANES), tile_map),
                pl.BlockSpec(memory_space=pl.ANY),
                pl.BlockSpec((1, 1, 2 * D_FF), exp_map),
                pl.BlockSpec(memory_space=pl.ANY),
                pl.BlockSpec((1, 1, D_MODEL), exp_map),
            ],
            out_specs=pl.BlockSpec((MOE_M, ROW_T, LANES), tile_map),
            scratch_shapes=[pltpu.VMEM((D_MODEL, 2 * D_FF), F32), pltpu.VMEM((D_FF, D_MODEL), F32),
                            pltpu.VMEM((D_MODEL, 2 * D_FF), BF16), pltpu.VMEM((D_FF, D_MODEL), BF16),
                            pltpu.SemaphoreType.DMA((2,))],
        ),
        out_shape=jax.ShapeDtypeStruct((n_rows, ROW_T, LANES), F32),
        compiler_params=_params(("arbitrary",)),
        name="moe_experts",
    )(*visits, xs, w1, b1.reshape(N_EXPERTS, 1, 2 * D_FF), w2, b2.reshape(N_EXPERTS, 1, D_MODEL))


def _combine_kernel(slot_ref, next_ref, gatep_ref, gates_ref, x1p_ref, x1s_ref, g2p_ref, g2s_ref, gpost_ref, ys_hbm,
                    yp_ref, ysm_ref, buf, sem, *, n_prompt_tiles):
    i = pl.program_id(0)
    last = pl.num_programs(0) - 1
    cur = lax.rem(i, 2)

    def gather(slots, b):
        def issue(r, carry):
            for k in range(TOP_K):
                pltpu.make_async_copy(ys_hbm.at[pl.ds(slots[0, 0, r * TOP_K + k], 1)],
                                      buf.at[b, pl.ds(k * TOK_TILE + r, 1)], sem.at[b]).start(priority=k % 2)
            return carry

        lax.fori_loop(0, TOK_TILE, issue, 0, unroll=4)

    @pl.when(i == 0)
    def _():
        gather(slot_ref, cur)

    @pl.when(i < last)
    def _():
        gather(next_ref, 1 - cur)

    for k in range(TOP_K):
        _row_wait(ys_hbm.at[pl.ds(0, TOK_TILE)], buf.at[cur, pl.ds(k * TOK_TILE, TOK_TILE)], sem.at[cur])
    is_sample = i >= n_prompt_tiles
    g = jnp.where(is_sample, gates_ref[...], gatep_ref[...])
    f = jnp.zeros((TOK_TILE, D_MODEL), F32)
    for k in range(TOP_K):
        rows = pltpu.einshape("mjc->m(jc)", buf[cur, pl.ds(k * TOK_TILE, TOK_TILE)])
        f = f + rows * g[:, k:k + 1]
    x1 = jnp.where(is_sample, x1s_ref[...], x1p_ref[...])
    g2 = jnp.where(is_sample, g2s_ref[...], g2p_ref[0])
    y = x1 + g2 * _rms(f, gpost_ref[...])

    @pl.when(jnp.logical_not(is_sample))
    def _():
        yp_ref[...] = y

    @pl.when(is_sample)
    def _():
        ysm_ref[...] = y


def _moe_combine(ys, slot_tiles, gates_p, gates_s, x1_p, x1_s, gate2_p, gate2_s, g_post, rows_per_batch):
    tiles = slot_tiles.shape[0]
    n_p = x1_p.shape[0] // TOK_TILE
    assert tiles == n_p + 1 and x1_s.shape[0] == TOK_TILE
    tiles_per_batch = rows_per_batch // TOK_TILE
    p_tile = lambda i: (jnp.minimum(i, n_p - 1), 0)
    const2 = lambda i: (0, 0)
    return pl.pallas_call(
        functools.partial(_combine_kernel, n_prompt_tiles=n_p),
        grid=(tiles,),
        in_specs=[
            pl.BlockSpec((1, 1, TOK_TILE * TOP_K), lambda i: (i, 0, 0), memory_space=pltpu.SMEM),
            pl.BlockSpec((1, 1, TOK_TILE * TOP_K), lambda i: (jnp.minimum(i + 1, tiles - 1), 0, 0),
                         memory_space=pltpu.SMEM),
            pl.BlockSpec((TOK_TILE, LANES), p_tile),
            pl.BlockSpec((TOK_TILE, LANES), const2),
            pl.BlockSpec((TOK_TILE, D_MODEL), p_tile),
            pl.BlockSpec((TOK_TILE, D_MODEL), const2),
            pl.BlockSpec((1, 1, D_MODEL), lambda i: (jnp.minimum(i, n_p - 1) // tiles_per_batch, 0, 0)),
            pl.BlockSpec((TOK_TILE, D_MODEL), const2),
            pl.BlockSpec((1, D_MODEL), const2),
            pl.BlockSpec(memory_space=pl.ANY),
        ],
        out_specs=[pl.BlockSpec((TOK_TILE, D_MODEL), p_tile), pl.BlockSpec((TOK_TILE, D_MODEL), const2)],
        out_shape=[jax.ShapeDtypeStruct(x1_p.shape, F32), jax.ShapeDtypeStruct(x1_s.shape, F32)],
        scratch_shapes=[pltpu.VMEM((2, TOP_K * TOK_TILE, ROW_T, LANES), F32), pltpu.SemaphoreType.DMA((2,))],
        compiler_params=_params(("arbitrary",)),
        name="moe_combine",
    )(slot_tiles, slot_tiles, gates_p, gates_s, x1_p, x1_s, gate2_p, gate2_s, g_post.reshape(1, D_MODEL), ys)


def _moe_plan(routes, counts):
    a = sum(r.shape[0] for r in routes) * TOP_K
    assert a % MOE_M == 0
    g_end = jnp.cumsum(counts)
    g_start = g_end - counts

    def slots(route):
        flat_e = route[:, :TOP_K].reshape(-1)
        rank = route[:, TOP_K:2 * TOP_K].reshape(-1)
        onehot = flat_e[:, None] == jnp.arange(N_EXPERTS)[None, :]
        return (jnp.sum(jnp.where(onehot, g_start[None, :], 0), axis=1) + rank).astype(jnp.int32)

    slot_of = jnp.concatenate([slots(r) for r in routes])

    n_tiles = a // MOE_M
    n_vis = n_tiles + N_EXPERTS - 1
    first_tile = g_start // MOE_M
    last_tile = jnp.where(counts > 0, (g_end - 1) // MOE_M, first_tile)
    nvis_e = jnp.where(counts > 0, last_tile - first_tile + 1, 0)
    v_end = jnp.cumsum(nvis_e)
    v_start = v_end - nvis_e
    total = v_end[-1]
    v = jnp.arange(n_vis)
    vc = jnp.minimum(v, total - 1)
    e_of = jnp.minimum(jnp.sum((vc[:, None] >= v_end[None, :]).astype(jnp.int32), axis=1), N_EXPERTS - 1)
    is_e = e_of[:, None] == jnp.arange(N_EXPERTS)[None, :]
    of_e = lambda t: jnp.sum(jnp.where(is_e, t[None, :], 0), axis=1)
    tile_of = of_e(first_tile) + (vc - of_e(v_start))
    lo = jnp.maximum(of_e(g_start), tile_of * MOE_M) - tile_of * MOE_M
    hi = jnp.minimum(of_e(g_end), (tile_of + 1) * MOE_M) - tile_of * MOE_M
    prev_tile = jnp.concatenate([jnp.full((1,), -1, tile_of.dtype), tile_of[:-1]])
    first = (tile_of != prev_tile).astype(jnp.int32)
    experts = jnp.arange(N_EXPERTS)
    later = (experts[None, :] > experts[:, None]) & (counts[None, :] > 0)
    next_e = jnp.min(jnp.where(later, experts[None, :], N_EXPERTS), axis=1)
    next_e = jnp.where(next_e < N_EXPERTS, next_e, -1)
    i32 = lambda t: t.astype(jnp.int32)
    visits = (i32(tile_of), i32(e_of), i32(of_e(next_e)), i32(lo), i32(hi), first, i32(total).reshape(1))
    return slot_of, visits


def _reorder_w_in(w_in):
    off_om_end = 4 * D_M
    off_g_end = off_om_end + 2 * H_M
    w = jnp.concatenate([
        w_in[:, :off_om_end],
        w_in[:, off_g_end:],
        w_in[:, off_om_end:off_g_end],
        jnp.zeros((D_MODEL, LANES - 2 * H_M), w_in.dtype),
    ], axis=1)
    return w.astype(BF16)


def kernel(x_prompt, x_sample, cache_k, cache_v, state_conv, state_C, state_n, state_m, c_prompt, c_sample, w_ada, b_ada, g_pre_mix, g_post_mix, g_pre_ffn, g_post_ffn, w_in, b_ig, b_fg, conv_w, conv_b, mh_norm, rel_bias, w_out, w_router, b_router, w1, b1, w2, b2):
    depth = w_in.shape[0]
    assert depth == 1
    l = 0
    nb_p, seq, _ = x_prompt.shape
    nb_s, n_tok, _ = x_sample.shape
    rows_s = nb_s * n_tok
    assert rows_s == TOK_TILE

    n_c = nb_p + nb_s
    c_pad = -(-n_c // 8) * 8
    c_all = jnp.concatenate([c_prompt, c_sample, jnp.zeros((c_pad - n_c, D_MODEL), F32)], axis=0)
    mod = _adaln(c_all, w_ada[l], b_ada[l])
    mods = [mod[:, i * D_MODEL:(i + 1) * D_MODEL] for i in range(6)]
    mods_p = [m[:nb_p].reshape(nb_p, 1, D_MODEL) for m in mods]
    mods_s = [jnp.repeat(m[nb_p:n_c], n_tok, axis=0).reshape(1, rows_s, D_MODEL) for m in mods]

    w_r = _reorder_w_in(w_in[l])
    bias_g = jnp.concatenate([b_ig[l], b_fg[l], jnp.zeros((LANES - 2 * H_M,), F32)]).reshape(1, LANES)
    w_out_b = w_out[l].astype(BF16)
    w_router_f = jnp.concatenate([w_router[l], jnp.zeros((D_MODEL, LANES - N_EXPERTS), F32)], axis=1)
    w_router_hi = w_router_f.astype(BF16)
    w_router_p = jnp.stack([w_router_hi, (w_router_f - w_router_hi.astype(F32)).astype(BF16)])
    b_router_p = jnp.concatenate([b_router[l].astype(F32), jnp.zeros((LANES - N_EXPERTS,), F32)]).reshape(1, LANES)

    shift1, scale1, gate1, shift2, scale2, gate2_p = mods_p
    dils = tuple(d for _, d in BRANCHES)
    assert dils[0] == 1
    u, v_m, og, gts, qa, ka, va, *dilated = _inproj(x_prompt, g_pre_mix[l], scale1, shift1, w_r, bias_g, tm=512,
                                                   dilations=dils[1:])
    qkv_d = [(qa, ka, va)] + [tuple(dilated[3 * n:3 * n + 3]) for n in range(len(dils) - 1)]
    gates_row = gts[:, :, :8].transpose(0, 2, 1)
    zeros = lambda *s: jnp.zeros(s, F32)
    mem, c_p, n_p, m_p = _mlstm(u, v_m, og, gts, gates_row, zeros(nb_p, 8, 2 * D_M),
                                zeros(nb_p, H_M, DK_M, DK_M), zeros(nb_p, H_M, DK_M), zeros(nb_p, H_M, LANES),
                                conv_w[l], conv_b[l], mh_norm[l], MLSTM_L_PROMPT)
    branches = [_attn_branch(*qkv, _prompt_bias(rel_bias, w, d), d) for qkv, (w, d) in zip(qkv_d, BRANCHES)]
    x1_p, h2_p, idx_p, gate_p, cnt_p = _postmix(mem, branches, x_prompt, gate1, scale2, shift2,
                                                g_post_mix[l], g_pre_ffn[l], w_out_b, w_router_p, b_router_p,
                                                zeros(1, LANES), tm=512, dilations=dils)
    keep = min(WINDOW_MAX, seq)
    k_p, v_p = _window_heads(ka, va, keep)
    conv_p = u[:, seq - (CONV_W - 1):]

    shift1, scale1, gate1, shift2, scale2, gate2_s = mods_s
    x_s = x_sample.reshape(1, rows_s, D_MODEL)
    u_s, v_s, og_s, gts_s, qa_s, ka_s, va_s = _inproj(x_s, g_pre_mix[l], scale1, shift1, w_r, bias_g, tm=rows_s)
    per_b = lambda t: t.reshape(nb_s, n_tok, t.shape[-1])
    ls = MLSTM_L_SAMPLE
    pad_rows = lambda t, n: jnp.concatenate([t, jnp.zeros((nb_s, n - t.shape[1], t.shape[2]), t.dtype)], axis=1)
    inert = jnp.concatenate([jnp.full((H_M,), NEG_INF, F32), jnp.zeros((LANES - H_M,), F32)])
    gts_pad = jnp.concatenate([per_b(gts_s), jnp.broadcast_to(inert, (nb_s, ls - n_tok, LANES))], axis=1)
    tail0 = jnp.concatenate([zeros(nb_s, 8 - (CONV_W - 1), 2 * D_M), state_conv[l].astype(F32)], axis=1)
    m0x = jnp.broadcast_to(state_m[l].astype(F32)[:, :, None], (nb_s, H_M, LANES))
    mem_s, c_s, n_s, m_s = _mlstm(pad_rows(per_b(u_s), ls), pad_rows(per_b(v_s), ls), pad_rows(per_b(og_s), ls),
                                  gts_pad, gts_pad[:, :, :8].transpose(0, 2, 1), tail0,
                                  state_C[l].astype(F32), state_n[l].astype(F32), m0x,
                                  conv_w[l], conv_b[l], mh_norm[l], ls)
    mem_s = mem_s[:, :n_tok].reshape(1, rows_s, D_M)
    heads = lambda t: t.reshape(nb_s, n_tok, H_A, HD_A)
    k_new, v_new = heads(ka_s), heads(va_s)
    n_past = cache_k.shape[2]
    assert n_tok <= TOK_PAD
    q_h = jnp.pad(heads(qa_s).transpose(0, 2, 1, 3), ((0, 0), (0, 0), (0, TOK_PAD - n_tok), (0, 0)))
    new_t = lambda t: jnp.pad(t.transpose(0, 2, 3, 1), ((0, 0), (0, 0), (0, 0), (0, LANES - n_tok)))
    cache_t = lambda c: c[l].astype(F32).transpose(0, 2, 3, 1)
    att_s = _sample_attention(q_h, cache_t(cache_k), cache_t(cache_v), new_t(k_new), new_t(v_new),
                              *_sample_bias(rel_bias, n_tok, n_past))
    att_s = att_s[:, :, :n_tok].transpose(0, 2, 1, 3).reshape(1, rows_s, D_A)
    zero_o = zeros(1, rows_s, D_A)
    ninf = jnp.full((1, rows_s, D_A), NEG_INF, F32)
    branches_s = [(att_s, zero_o), (zero_o, ninf), (zero_o, ninf)]
    x1_s, h2_s, idx_s, gate_s, cnt_all = _postmix(mem_s, branches_s, x_s, gate1, scale2, shift2,
                                                  g_post_mix[l], g_pre_ffn[l], w_out_b, w_router_p, b_router_p,
                                                  cnt_p, tm=rows_s)

    t_p = nb_p * seq
    slot_of, visits = _moe_plan([idx_p.reshape(t_p, LANES), idx_s.reshape(rows_s, LANES)],
                                cnt_all[0, :N_EXPERTS].astype(jnp.int32))
    slot_tiles = slot_of.reshape((t_p + rows_s) // TOK_TILE, 1, TOK_TILE * TOP_K)
    xs = _moe_scatter(h2_p.reshape(t_p, D_MODEL), h2_s.reshape(rows_s, D_MODEL), slot_tiles)
    ys = _moe_experts(xs, visits, w1[l], b1[l], w2[l], b2[l])
    y_p, y_s = _moe_combine(ys, slot_tiles, gate_p.reshape(t_p, LANES), gate_s.reshape(rows_s, LANES), x1_p.reshape(t_p, D_MODEL), x1_s.reshape(rows_s, D_MODEL),
                            gate2_p, gate2_s.reshape(rows_s, D_MODEL), g_post_ffn[l], seq)
    y_prompt = y_p.reshape(nb_p, seq, D_MODEL)
    y_sample = y_s.reshape(nb_s, n_tok, D_MODEL)

    st = lambda t: t[None]
    conv_s = jnp.concatenate([state_conv[l].astype(F32), per_b(u_s)], axis=1)[:, -(CONV_W - 1):]
    return (y_prompt, y_sample, st(k_p), st(v_p), st(conv_p), st(c_p), st(n_p), st(m_p[:, :, 0]),
            st(k_new), st(v_new), st(conv_s), st(c_s), st(n_s), st(m_s[:, :, 0]))
```

```python
import functools
import math

import jax
import jax.numpy as jnp
import numpy as np
from jax import lax
from jax.experimental import pallas as pl
from jax.experimental.pallas import tpu as pltpu

F32 = jnp.float32
BF16 = jnp.bfloat16
U32 = jnp.uint32
NEG_INF = float("-inf")

D_MODEL = 1024
H_M = 4
D_M = 512
DK_M = 128
CONV_W = 4
H_A = 8
D_A = 512
HD_A = 64
BRANCHES = ((128, 1), (512, 4), (2048, 16))
WINDOW_MAX = 2048
QB = 128
N_BUCKETS = 32
MAX_DISTANCE = 2048
N_EXPERTS = 32
TOP_K = 4
D_FF = 1024
SWIGLU_LIMIT = 7.0
SWIGLU_ALPHA = 1.702
EPS = 1e-6

LANES = 128
C_U, C_V, C_O, C_QA, C_KA, C_VA, C_G = 0, 1024, 1536, 2048, 2560, 3072, 3584
D_IN_R = C_G + LANES

MLSTM_L_PROMPT = 256
MLSTM_L_SAMPLE = 32
ATT_QL = 1024
MOE_M = 512
MOE_SUB = 256
TOK_TILE = 128
PACK_W = D_MODEL // 2
PACK_T = PACK_W // LANES
ROW_T = D_MODEL // LANES
HI_MASK = 0xFFFF0000
VMEM_LIMIT = 56 * 1024 * 1024


def _params(sem, vmem=None, fuse_inputs=0):
    return pltpu.CompilerParams(dimension_semantics=sem, vmem_limit_bytes=vmem or VMEM_LIMIT,
                                allow_input_fusion=[True] * fuse_inputs if fuse_inputs else None)


def _rms(x, g):
    return x * lax.rsqrt(jnp.mean(x * x, axis=-1, keepdims=True) + EPS) * g


def _log_sigmoid(t):
    return jnp.minimum(t, 0.0) - jnp.log1p(jnp.exp(-jnp.abs(t)))


def _ada_kernel(c_ref, w_ref, b_ref, o_ref):
    c = c_ref[...]
    a = (c * jax.nn.sigmoid(c)).astype(BF16)
    o_ref[...] = jnp.dot(a, w_ref[...].astype(BF16), preferred_element_type=F32) + b_ref[...]


def _adaln(c_all, w_ada, b_ada):
    rows = c_all.shape[0]
    n = w_ada.shape[1]
    tn = 1024
    return pl.pallas_call(
        _ada_kernel,
        grid=(n // tn,),
        in_specs=[
            pl.BlockSpec((rows, D_MODEL), lambda j: (0, 0)),
            pl.BlockSpec((D_MODEL, tn), lambda j: (0, j)),
            pl.BlockSpec((1, tn), lambda j: (0, j)),
        ],
        out_specs=pl.BlockSpec((rows, tn), lambda j: (0, j)),
        out_shape=jax.ShapeDtypeStruct((rows, n), F32),
        compiler_params=_params(("arbitrary",)),
        name="adaln",
    )(c_all, w_ada, b_ada.reshape(1, n))


def _inproj_kernel(x_ref, g_ref, sc_ref, sh_ref, w_ref, bg_ref,
                   u_ref, v_ref, o_ref, gt_ref, qa_ref, ka_ref, va_ref, *rest, dilations):
    dil_refs, slab_s = (rest[:-1], rest[-1]) if dilations else (rest, None)
    x = x_ref[0]
    h = _rms(x, g_ref[...]) * (1.0 + sc_ref[0]) + sh_ref[0]
    hb = h.astype(BF16)

    def seg(lo, width):
        return jnp.dot(hb, w_ref[:, lo:lo + width], preferred_element_type=F32)

    u_ref[0] = seg(C_U, 2 * D_M)
    v_ref[0] = seg(C_V, D_M)
    o_ref[0] = jax.nn.sigmoid(seg(C_O, D_M))
    qkv = (seg(C_QA, D_A) * (HD_A ** -0.5), seg(C_KA, D_A), seg(C_VA, D_A))
    for ref, val in zip((qa_ref, ka_ref, va_ref), qkv):
        ref[0] = val
    n_slab = D_A // LANES
    rows = x.shape[0]
    for j, val in enumerate(qkv if dilations else ()):
        for c in range(n_slab):
            slab_s[j, c] = val[:, c * LANES:(c + 1) * LANES]
        for n, dil in enumerate(dilations):
            for r in range(dil):
                for c in range(n_slab):
                    lo = r * D_A + c * LANES
                    dil_refs[3 * n + j][0, :, lo:lo + LANES] = slab_s[j, c, pl.ds(r, rows // dil, stride=dil), :]
    t = seg(C_G, LANES) + bg_ref[...]
    lane = lax.broadcasted_iota(jnp.int32, t.shape, 1)
    gt_ref[0] = jnp.where(lane < H_M, t, _log_sigmoid(t))


def _inproj(x3, g_pre, scale, shift, w_r, bias_g, tm, dilations=()):
    nb, rows, _ = x3.shape
    per_row_mod = scale.shape[1] != 1
    mod_block = (1, tm, D_MODEL) if per_row_mod else (1, 1, D_MODEL)
    mod_map = (lambda b, i: (b, i, 0)) if per_row_mod else (lambda b, i: (b, 0, 0))
    row_map = lambda b, i: (b, i, 0)
    widths = (2 * D_M, D_M, D_M, LANES, D_A, D_A, D_A)
    dil_specs = [pl.BlockSpec((1, tm // d, d * D_A), row_map) for d in dilations for _ in range(3)]
    dil_shapes = [jax.ShapeDtypeStruct((nb, rows // d, d * D_A), F32) for d in dilations for _ in range(3)]
    return pl.pallas_call(
        functools.partial(_inproj_kernel, dilations=tuple(dilations)),
        grid=(nb, rows // tm),
        in_specs=[
            pl.BlockSpec((1, tm, D_MODEL), row_map),
            pl.BlockSpec((1, D_MODEL), lambda b, i: (0, 0)),
            pl.BlockSpec(mod_block, mod_map),
            pl.BlockSpec(mod_block, mod_map),
            pl.BlockSpec((D_MODEL, D_IN_R), lambda b, i: (0, 0)),
            pl.BlockSpec((1, LANES), lambda b, i: (0, 0)),
        ],
        out_specs=[pl.BlockSpec((1, tm, w), row_map) for w in widths] + dil_specs,
        out_shape=[jax.ShapeDtypeStruct((nb, rows, w), F32) for w in widths] + dil_shapes,
        scratch_shapes=[pltpu.VMEM((3, D_A // LANES, tm, LANES), F32)] if dilations else [],
        compiler_params=_params(("arbitrary", "arbitrary")),
        name="inproj",
    )(x3, g_pre.reshape(1, D_MODEL), scale, shift, w_r, bias_g)


def _mlstm_kernel(u_ref, v_ref, og_ref, gc_ref, gr_ref, tail0_ref, c0_ref, n0_ref, m0_ref,
                  cw_ref, cb_ref, mh_ref,
                  mem_ref, c_out_ref, n_out_ref, m_out_ref,
                  c_s, n_s, m_s, tail_s, uext_s, *, chunk):
    L = chunk
    step = pl.program_id(1)

    @pl.when(step == 0)
    def _():
        c_s[...] = c0_ref[0]
        n_s[...] = n0_ref[0]
        m_s[...] = m0_ref[0]
        tail_s[...] = tail0_ref[0]

    u = u_ref[0]
    uext_s[0:8, :] = tail_s[...]
    uext_s[8:8 + L, :] = u
    acc = cb_ref[...] + uext_s[pl.ds(5, L), :] * cw_ref[0:1, :]
    acc = acc + uext_s[pl.ds(6, L), :] * cw_ref[1:2, :]
    acc = acc + uext_s[pl.ds(7, L), :] * cw_ref[2:3, :]
    acc = acc + u * cw_ref[3:4, :]
    tail_s[...] = uext_s[pl.ds(L, 8), :]
    qk = acc * jax.nn.sigmoid(acc)

    gc = gc_ref[0]
    gr = gr_ref[0]
    row = lax.broadcasted_iota(jnp.int32, (L, L), 0)
    col = lax.broadcasted_iota(jnp.int32, (L, L), 1)
    causal = col <= row
    tri = causal.astype(BF16)
    tri_t = (row <= col).astype(BF16)
    lane = lax.broadcasted_iota(jnp.int32, gc.shape, 1)
    srow = lax.broadcasted_iota(jnp.int32, gr.shape, 0)

    def bf16_parts(x):
        hi = x.astype(BF16)
        r1 = x - hi.astype(F32)
        mid = r1.astype(BF16)
        return hi, mid, (r1 - mid.astype(F32)).astype(BF16)

    b_col_all = sum(jnp.dot(tri, p, preferred_element_type=F32)
                    for p in bf16_parts(jnp.where(lane >= H_M, gc, 0.0)))
    b_row_all = sum(jnp.dot(p, tri_t, preferred_element_type=F32)
                    for p in bf16_parts(jnp.where(srow >= H_M, gr, 0.0)))

    v_all = v_ref[0]
    og = og_ref[0]
    for h in range(H_M):
        q = qk[:, h * DK_M:(h + 1) * DK_M]
        k = qk[:, D_M + h * DK_M:D_M + (h + 1) * DK_M] * (DK_M ** -0.5)
        v = v_all[:, h * DK_M:(h + 1) * DK_M]
        li_row = gr[h:h + 1, :]
        li_col = gc[:, h:h + 1]
        b_row = b_row_all[H_M + h:H_M + h + 1, :]
        b_col = b_col_all[:, H_M + h:H_M + h + 1]
        m0 = m_s[h:h + 1, 0:1]
        c0 = c_s[h]
        n0 = n_s[h:h + 1, :]

        d = jnp.where(causal, b_col - b_row + li_row, NEG_INF)
        g_state = b_col + m0
        m = jnp.maximum(jnp.max(d, axis=-1, keepdims=True), g_state)
        w_state = jnp.exp(g_state - m)
        qb = q.astype(BF16)
        kb = k.astype(BF16)
        vb = v.astype(BF16)
        s = lax.dot_general(qb, kb, (((1,), (1,)), ((), ())), preferred_element_type=F32)
        s = s * jnp.exp(d - m)
        num = jnp.dot(s.astype(BF16), vb, preferred_element_type=F32)
        num = num + w_state * jnp.dot(qb, c0.astype(BF16), preferred_element_type=F32)
        den = jnp.sum(s, axis=-1, keepdims=True) + w_state * jnp.sum(q * n0, axis=-1, keepdims=True)
        hh = num / jnp.maximum(jnp.abs(den), jnp.exp(-m))

        b_last = b_col[L - 1:L, :]
        g_tok_row = b_last - b_row + li_row
        m_new = jnp.maximum(b_last + m0, jnp.max(g_tok_row, axis=-1, keepdims=True))
        w_tok_col = jnp.exp(b_last - b_col + li_col - m_new)
        decay = jnp.exp(b_last + m0 - m_new)
        kw = k * w_tok_col
        c_s[h] = decay * c0 + jnp.dot(kw.T.astype(BF16), vb, preferred_element_type=F32)
        n_s[h:h + 1, :] = decay * n0 + jnp.sum(kw, axis=0, keepdims=True)
        m_s[h:h + 1, :] = jnp.broadcast_to(m_new, (1, LANES))

        hn = hh * lax.rsqrt(jnp.mean(hh * hh, axis=-1, keepdims=True) + EPS)
        hn = hn * mh_ref[:, h * DK_M:(h + 1) * DK_M]
        mem_ref[0, :, h * DK_M:(h + 1) * DK_M] = og[:, h * DK_M:(h + 1) * DK_M] * hn

    @pl.when(step == pl.num_programs(1) - 1)
    def _():
        c_out_ref[0] = c_s[...]
        n_out_ref[0] = n_s[...]
        m_out_ref[0] = m_s[...]


def _mlstm(u, v, og, gates_col, gates_row, tail0, c0, n0, m0x, conv_w, conv_b, mh_norm, chunk):
    nb, rows, _ = u.shape
    nc = rows // chunk
    row_map = lambda b, c: (b, c, 0)
    bat3 = lambda b, c: (b, 0, 0)
    const2 = lambda b, c: (0, 0)
    return pl.pallas_call(
        functools.partial(_mlstm_kernel, chunk=chunk),
        grid=(nb, nc),
        in_specs=[
            pl.BlockSpec((1, chunk, 2 * D_M), row_map),
            pl.BlockSpec((1, chunk, D_M), row_map),
            pl.BlockSpec((1, chunk, D_M), row_map),
            pl.BlockSpec((1, chunk, LANES), row_map),
            pl.BlockSpec((1, 8, chunk), lambda b, c: (b, 0, c)),
            pl.BlockSpec((1, 8, 2 * D_M), bat3),
            pl.BlockSpec((1, H_M, DK_M, DK_M), lambda b, c: (b, 0, 0, 0)),
            pl.BlockSpec((1, H_M, DK_M), bat3),
            pl.BlockSpec((1, H_M, LANES), bat3),
            pl.BlockSpec((CONV_W, 2 * D_M), const2),
            pl.BlockSpec((1, 2 * D_M), const2),
            pl.BlockSpec((1, D_M), const2),
        ],
        out_specs=[
            pl.BlockSpec((1, chunk, D_M), row_map),
            pl.BlockSpec((1, H_M, DK_M, DK_M), lambda b, c: (b, 0, 0, 0)),
            pl.BlockSpec((1, H_M, DK_M), bat3),
            pl.BlockSpec((1, H_M, LANES), bat3),
        ],
        out_shape=[
            jax.ShapeDtypeStruct((nb, rows, D_M), F32),
            jax.ShapeDtypeStruct((nb, H_M, DK_M, DK_M), F32),
            jax.ShapeDtypeStruct((nb, H_M, DK_M), F32),
            jax.ShapeDtypeStruct((nb, H_M, LANES), F32),
        ],
        scratch_shapes=[
            pltpu.VMEM((H_M, DK_M, DK_M), F32),
            pltpu.VMEM((H_M, DK_M), F32),
            pltpu.VMEM((H_M, LANES), F32),
            pltpu.VMEM((8, 2 * D_M), F32),
            pltpu.VMEM((chunk + 8, 2 * D_M), F32),
        ],
        compiler_params=_params(("arbitrary", "arbitrary"), fuse_inputs=12),
        name="mlstm",
    )(u, v, og, gates_col, gates_row, tail0, c0, n0, m0x,
      conv_w, conv_b.reshape(1, 2 * D_M), mh_norm.reshape(1, D_M))


def _attn_kernel(q_ref, kc_ref, vc_ref, kp_ref, vp_ref, bias_ref, ones_ref, o_ref, l_ref, k_s, v_s, *, ql):
    first = pl.program_id(2) == 0
    k_s[0:QB, :] = kp_ref[0].astype(BF16)
    k_s[QB:QB + ql, :] = kc_ref[0].astype(BF16)
    v_s[0:QB, :] = vp_ref[0].astype(BF16)
    v_s[QB:QB + ql, :] = vc_ref[0].astype(BF16)
    col = lax.broadcasted_iota(jnp.int32, (QB, 2 * QB), 1)
    pad_keys = jnp.logical_and(first, col < QB)
    head0 = lax.broadcasted_iota(jnp.int32, (1, LANES), 1) < HD_A
    n_slab = D_A // LANES
    n_res = q_ref.shape[2] // D_A
    pad_all = jnp.concatenate([pad_keys] * H_A, axis=0)

    def raw_logits(c, j):
        sl = slice(c * LANES, (c + 1) * LANES)
        q = q_ref[0, j * QB:(j + 1) * QB, sl].astype(BF16)
        zero = jnp.zeros_like(q)
        q2 = jnp.concatenate([jnp.where(head0, q, zero), jnp.where(head0, zero, q)], axis=0)
        return lax.dot_general(q2, k_s[j * QB:(j + 2) * QB, sl], (((1,), (1,)), ((), ())),
                               preferred_element_type=F32)

    def softmax(j, logits):
        logits = logits + bias_ref[...].reshape(H_A * QB, 2 * QB)
        if j == 0:
            logits = jnp.where(pad_all, NEG_INF, logits)
        m = jnp.max(logits, axis=-1, keepdims=True)
        return jnp.exp(logits - m).astype(BF16), m

    def finish(c, j, p, m):
        sl = slice(c * LANES, (c + 1) * LANES)
        rows = slice(j * QB, (j + 1) * QB)
        p_cat = jnp.concatenate([p[:QB], p[QB:]], axis=1)
        v_all = v_s[j * QB:(j + 2) * QB, sl]
        v_bd = jnp.concatenate([jnp.where(head0, v_all, jnp.zeros_like(v_all)),
                                jnp.where(head0, jnp.zeros_like(v_all), v_all)], axis=0)
        o = jnp.dot(p_cat, v_bd, preferred_element_type=F32)
        s = jnp.dot(p_cat, ones_ref[...], preferred_element_type=F32)
        o_ref[0, rows, sl] = o / s
        l_ref[0, rows, sl] = jnp.where(head0, m[:QB], m[QB:]) + jnp.log(s)

    for res in range(n_res):
        slabs = range(res * n_slab, (res + 1) * n_slab)
        for j in range(ql // QB):
            p, m = softmax(j, jnp.concatenate([raw_logits(c, j) for c in slabs], axis=0))
            for n, c in enumerate(slabs):
                part = slice(2 * n * QB, 2 * (n + 1) * QB)
                finish(c, j, p[part], m[part])


def _attn_branch(qa, ka, va, bias, dil):
    nb, ld, _ = qa.shape
    ql = min(ATT_QL, ld)
    n_res = min(dil, ATT_QL // ql)
    width = n_res * D_A
    cur = lambda b, r, i: (b, i, r)
    prev = lambda b, r, i: (b, jnp.maximum(i * (ql // QB) - 1, 0), r)
    ones_bd = np.zeros((4 * QB, LANES), np.float32)
    ones_bd[:2 * QB, :HD_A] = 1.0
    ones_bd[2 * QB:, HD_A:] = 1.0
    return pl.pallas_call(
        functools.partial(_attn_kernel, ql=ql),
        grid=(nb, dil // n_res, ld // ql),
        in_specs=[
            pl.BlockSpec((1, ql, width), cur),
            pl.BlockSpec((1, ql, width), cur),
            pl.BlockSpec((1, ql, width), cur),
            pl.BlockSpec((1, QB, width), prev),
            pl.BlockSpec((1, QB, width), prev),
            pl.BlockSpec((H_A, QB, 2 * QB), lambda b, r, i: (0, 0, 0)),
            pl.BlockSpec((4 * QB, LANES), lambda b, r, i: (0, 0)),
        ],
        out_specs=[pl.BlockSpec((1, ql, width), cur), pl.BlockSpec((1, ql, width), cur)],
        out_shape=[jax.ShapeDtypeStruct((nb, ld, dil * D_A), F32)] * 2,
        scratch_shapes=[pltpu.VMEM((QB + ql, width), BF16), pltpu.VMEM((QB + ql, width), BF16)],
        compiler_params=_params(("arbitrary", "arbitrary", "arbitrary"), fuse_inputs=7),
        name=f"attn_d{dil}",
    )(qa, ka, va, ka, va, bias, jnp.asarray(ones_bd, BF16))


def _heads_kernel(k_ref, v_ref, ko_ref, vo_ref):
    for src, dst in ((k_ref, ko_ref), (v_ref, vo_ref)):
        for c in range(D_A // LANES):
            xt = src[0, :, c * LANES:(c + 1) * LANES].T
            dst[0, 2 * c] = xt[:HD_A]
            dst[0, 2 * c + 1] = xt[HD_A:]


def _window_heads(ka, va, keep, tm=512):
    nb, seq, _ = ka.shape
    first = (seq - keep) // tm
    in_spec = pl.BlockSpec((1, tm, D_A), lambda b, i: (b, first + i, 0))
    out_spec = pl.BlockSpec((1, H_A, HD_A, tm), lambda b, i: (b, 0, 0, i))
    kt, vt = pl.pallas_call(
        _heads_kernel,
        grid=(nb, keep // tm),
        in_specs=[in_spec, in_spec],
        out_specs=[out_spec, out_spec],
        out_shape=[jax.ShapeDtypeStruct((nb, H_A, HD_A, keep), F32)] * 2,
        compiler_params=_params(("arbitrary", "arbitrary")),
        name="window_heads",
    )(ka, va)
    return kt.transpose(0, 3, 1, 2), vt.transpose(0, 3, 1, 2)


def _t5_bucket(dist):
    max_exact = N_BUCKETS // 2
    n = jnp.maximum(dist, 1).astype(F32)
    large = max_exact + (jnp.log(n / max_exact) / math.log(MAX_DISTANCE / max_exact)
                         * (N_BUCKETS - max_exact)).astype(jnp.int32)
    large = jnp.minimum(large, N_BUCKETS - 1)
    return jnp.where(dist < max_exact, dist, large)


def _branch_bias_vec(rel_bias, window, dil):
    return rel_bias[_t5_bucket(dil * jnp.arange(window // dil + 1))].astype(F32)


def _prompt_bias(rel_bias, window, dil):
    assert window // dil == QB
    n = 2 * QB
    vec = _branch_bias_vec(rel_bias, window, dil)
    vecp = jnp.concatenate([vec[::-1], jnp.full((n - QB - 1, H_A), NEG_INF, F32)], axis=0)
    g = jnp.roll(vecp, QB - 1, axis=0)
    skew = jnp.tile(g, (QB + 1, 1))[:QB * (n + 1)].reshape(QB, n + 1, H_A)[:, :n]
    return skew[::-1].transpose(2, 0, 1)


TOK_PAD = 8


def _sattn_kernel(q_ref, kt_ref, vt_ref, knt_ref, vnt_ref, bc_ref, bn_ref, att_ref):
    nt = (((1,), (1,)), ((), ()))
    for h in range(H_A):
        q = q_ref[0, h].astype(BF16)
        lc = jnp.dot(q, kt_ref[0, h].astype(BF16), preferred_element_type=F32)
        ln = jnp.dot(q, knt_ref[0, h].astype(BF16), preferred_element_type=F32)
        lcs = [lc + bc_ref[d, h] for d in range(len(BRANCHES))]
        lns = [ln + bn_ref[d, h] for d in range(len(BRANCHES))]
        m = functools.reduce(jnp.maximum, [jnp.max(x, axis=-1, keepdims=True) for x in lcs + lns])
        pc = functools.reduce(jnp.add, [jnp.exp(x - m) for x in lcs])
        pn = functools.reduce(jnp.add, [jnp.exp(x - m) for x in lns])
        s = jnp.sum(pc, axis=-1, keepdims=True) + jnp.sum(pn, axis=-1, keepdims=True)
        o = lax.dot_general(pc.astype(BF16), vt_ref[0, h].astype(BF16), nt, preferred_element_type=F32)
        o = o + lax.dot_general(pn.astype(BF16), vnt_ref[0, h].astype(BF16), nt, preferred_element_type=F32)
        att_ref[0, h] = o / s


def _sample_attention(q_h, kt, vt, knt, vnt, bias_c, bias_n):
    nb, _, _, n_past = kt.shape
    bat = lambda b: (b, 0, 0, 0)
    const = lambda b: (0, 0, 0, 0)
    return pl.pallas_call(
        _sattn_kernel,
        grid=(nb,),
        in_specs=[
            pl.BlockSpec((1, H_A, TOK_PAD, HD_A), bat),
            pl.BlockSpec((1, H_A, HD_A, n_past), bat),
            pl.BlockSpec((1, H_A, HD_A, n_past), bat),
            pl.BlockSpec((1, H_A, HD_A, LANES), bat),
            pl.BlockSpec((1, H_A, HD_A, LANES), bat),
            pl.BlockSpec(bias_c.shape, const),
            pl.BlockSpec(bias_n.shape, const),
        ],
        out_specs=pl.BlockSpec((1, H_A, TOK_PAD, HD_A), bat),
        out_shape=jax.ShapeDtypeStruct((nb, H_A, TOK_PAD, HD_A), F32),
        compiler_params=_params(("arbitrary",), fuse_inputs=7),
        name="sample_attn",
    )(q_h, kt, vt, knt, vnt, bias_c, bias_n)


def _sample_bias(rel_bias, n_tok, n_past):
    ninf = lambda n: jnp.full((n, H_A), NEG_INF, F32)
    cached, new = [], []
    for window, dil in BRANCHES:
        nk = window // dil
        rows = n_past // dil
        assert n_past % dil == 0 and rows >= nk
        vec = _branch_bias_vec(rel_bias, window, dil)
        res = jnp.arange(dil)[None, :, None]
        tc, tn = [], []
        for t in range(n_tok):
            near = t // dil + 1
            col = jnp.concatenate([ninf(rows - (nk + 1 - near)), vec[near:nk + 1][::-1]], axis=0)
            tab = jnp.where(res == t % dil, col[:, None, :], NEG_INF).reshape(n_past, H_A)
            tc.append(tab)
            keys = [vec[(t - j) // dil][None] if (t - j) % dil == 0 and 0 <= t - j <= nk * dil else ninf(1)
                    for j in range(n_tok)]
            tn.append(jnp.concatenate(keys + [ninf(LANES - n_tok)], axis=0))
        pad = lambda tabs: jnp.stack(tabs + [jnp.zeros_like(tabs[0])] * (TOK_PAD - n_tok))
        cached.append(pad(tc).transpose(2, 0, 1))
        new.append(pad(tn).transpose(2, 0, 1))
    return jnp.stack(cached), jnp.stack(new)


def _postmix_kernel(mem_ref, o1_ref, l1_ref, o2_ref, l2_ref, o3_ref, l3_ref, x_ref,
                    g1_ref, sc2_ref, sh2_ref, gpost_ref, gpre_ref, wo_ref, wr_ref, br_ref, cnt0_ref,
                    x1_ref, h2_ref, idx_ref, gate_ref, cnt_ref, cnt_s, slab_s, *, dilations):
    first_step = jnp.logical_and(pl.program_id(0) == 0, pl.program_id(1) == 0)

    @pl.when(first_step)
    def _():
        cnt_s[...] = cnt0_ref[...]

    n_slab = D_A // LANES
    tm = x_ref.shape[1]

    def rows(ref, dil, slot):
        if dil == 1:
            return ref[0]
        for r in range(dil):
            for c in range(n_slab):
                lo = r * D_A + c * LANES
                slab_s[slot, c, pl.ds(r, tm // dil, stride=dil), :] = ref[0, :, lo:lo + LANES]
        return jnp.concatenate([slab_s[slot, c] for c in range(n_slab)], axis=1)

    o1, o2, o3 = (rows(r, d, n) for n, (r, d) in enumerate(zip((o1_ref, o2_ref, o3_ref), dilations)))
    l1, l2, l3 = (rows(r, d, 3 + n) for n, (r, d) in enumerate(zip((l1_ref, l2_ref, l3_ref), dilations)))
    lmax = jnp.maximum(jnp.maximum(l1, l2), l3)
    w1 = jnp.exp(l1 - lmax)
    w2 = jnp.exp(l2 - lmax)
    w3 = jnp.exp(l3 - lmax)
    att = (w1 * o1 + w2 * o2 + w3 * o3) / (w1 + w2 + w3)
    y = jnp.dot(mem_ref[0].astype(BF16), wo_ref[0:D_M, :], preferred_element_type=F32)
    y = y + jnp.dot(att.astype(BF16), wo_ref[D_M:, :], preferred_element_type=F32)
    x1 = x_ref[0] + g1_ref[0] * _rms(y, gpost_ref[...])
    x1_ref[0] = x1
    h2 = _rms(x1, gpre_ref[...]) * (1.0 + sc2_ref[0]) + sh2_ref[0]
    h2_ref[0] = h2
    h_hi = h2.astype(BF16)
    h_lo = (h2 - h_hi.astype(F32)).astype(BF16)
    w_hi = wr_ref[0]
    logits = jnp.dot(h_hi, w_hi, preferred_element_type=F32)
    logits = logits + jnp.dot(h_lo, w_hi, preferred_element_type=F32)
    logits = logits + jnp.dot(h_hi, wr_ref[1], preferred_element_type=F32) + br_ref[...]
    lane = lax.broadcasted_iota(jnp.int32, logits.shape, 1)
    lane_f = lane.astype(F32)
    logits = jnp.where(lane < N_EXPERTS, logits, NEG_INF)
    vals, idxs = [], []
    for _ in range(TOP_K):
        mx = jnp.max(logits, axis=-1, keepdims=True)
        ix = jnp.min(jnp.where(logits == mx, lane_f, float(LANES)), axis=-1, keepdims=True)
        vals.append(mx)
        idxs.append(ix)
        logits = jnp.where(lane_f == ix, NEG_INF, logits)
    es = [jnp.exp(v - vals[0]) for v in vals]
    tot = es[0] + es[1] + es[2] + es[3]
    tm = logits.shape[0]
    picks = [(lane_f == ix).astype(F32) for ix in idxs]
    chosen = picks[0] + picks[1] + picks[2] + picks[3]
    r_i = lax.broadcasted_iota(jnp.int32, (tm, tm), 0)
    c_i = lax.broadcasted_iota(jnp.int32, (tm, tm), 1)
    earlier = jnp.dot((c_i < r_i).astype(BF16), chosen.astype(BF16), preferred_element_type=F32)
    before = earlier + cnt_s[...]
    cnt_s[...] = cnt_s[...] + jnp.sum(chosen, axis=0, keepdims=True)
    cnt_ref[...] = cnt_s[...]
    idx_tile = jnp.zeros(logits.shape, F32)
    gate_tile = jnp.zeros(logits.shape, F32)
    for k in range(TOP_K):
        rank = jnp.sum(picks[k] * before, axis=-1, keepdims=True)
        idx_tile = jnp.where(lane == k, idxs[k], idx_tile)
        idx_tile = jnp.where(lane == TOP_K + k, rank, idx_tile)
        gate_tile = jnp.where(lane == k, es[k] / tot, gate_tile)
    idx_ref[0] = idx_tile.astype(jnp.int32)
    gate_ref[0] = gate_tile


def _postmix(mem, branches, x3, gate1, scale2, shift2, g_post, g_pre, w_out_b, w_router_p, b_router_p, counts0, tm,
             dilations=(1, 1, 1)):
    nb, rows, _ = x3.shape
    per_row_mod = gate1.shape[1] != 1
    mod_block = (1, tm, D_MODEL) if per_row_mod else (1, 1, D_MODEL)
    mod_map = (lambda b, i: (b, i, 0)) if per_row_mod else (lambda b, i: (b, 0, 0))
    row_map = lambda b, i: (b, i, 0)
    const2 = lambda b, i: (0, 0)
    half = pl.BlockSpec((1, tm, D_M), row_map)
    flat = [a for pair in branches for a in pair]
    branch_specs = [pl.BlockSpec((1, tm // d, d * D_A), row_map) for d in dilations for _ in range(2)]
    return pl.pallas_call(
        functools.partial(_postmix_kernel, dilations=tuple(dilations)),
        grid=(nb, rows // tm),
        in_specs=[half] + branch_specs + [
            pl.BlockSpec((1, tm, D_MODEL), row_map),
            pl.BlockSpec(mod_block, mod_map),
            pl.BlockSpec(mod_block, mod_map),
            pl.BlockSpec(mod_block, mod_map),
            pl.BlockSpec((1, D_MODEL), const2),
            pl.BlockSpec((1, D_MODEL), const2),
            pl.BlockSpec((D_MODEL, D_MODEL), const2),
            pl.BlockSpec((2, D_MODEL, LANES), lambda b, i: (0, 0, 0)),
            pl.BlockSpec((1, LANES), const2),
            pl.BlockSpec((1, LANES), const2),
        ],
        out_specs=[
            pl.BlockSpec((1, tm, D_MODEL), row_map),
            pl.BlockSpec((1, tm, D_MODEL), row_map),
            pl.BlockSpec((1, tm, LANES), row_map),
            pl.BlockSpec((1, tm, LANES), row_map),
            pl.BlockSpec((1, LANES), const2),
        ],
        out_shape=[
            jax.ShapeDtypeStruct((nb, rows, D_MODEL), F32),
            jax.ShapeDtypeStruct((nb, rows, D_MODEL), F32),
            jax.ShapeDtypeStruct((nb, rows, LANES), jnp.int32),
            jax.ShapeDtypeStruct((nb, rows, LANES), F32),
            jax.ShapeDtypeStruct((1, LANES), F32),
        ],
        scratch_shapes=[pltpu.VMEM((1, LANES), F32), pltpu.VMEM((6, D_A // LANES, tm, LANES), F32)],
        compiler_params=_params(("arbitrary", "arbitrary")),
        name="postmix",
    )(mem, *flat, x3, gate1, scale2, shift2, g_post.reshape(1, D_MODEL), g_pre.reshape(1, D_MODEL),
      w_out_b, w_router_p, b_router_p, counts0)


def _row_wait(src_rows, dst_rows, sem):
    pltpu.make_async_copy(src_rows, dst_rows, sem).wait()


def _scatter_kernel(slot_ref, hp_ref, hs_ref, xs_hbm, buf, sem, *, n_prompt_tiles):
    h = jnp.where(pl.program_id(0) >= n_prompt_tiles, hs_ref[...], hp_ref[...])
    bits = pltpu.bitcast(h.astype(BF16).astype(F32), U32)
    packed = (bits[:, :PACK_W] & U32(HI_MASK)) | (bits[:, PACK_W:] >> 16)
    buf[...] = pltpu.einshape("m(jc)->mjc", packed, j=PACK_T)

    def issue(r, carry):
        src = buf.at[pl.ds(r, 1)]
        for k in range(TOP_K):
            pltpu.make_async_copy(src, xs_hbm.at[pl.ds(slot_ref[0, 0, r * TOP_K + k], 1)],
                                  sem).start(priority=k % 2)
        return carry

    lax.fori_loop(0, TOK_TILE, issue, 0, unroll=4)
    for _ in range(TOP_K):
        _row_wait(buf, xs_hbm.at[pl.ds(0, TOK_TILE)], sem)


def _moe_scatter(h_p, h_s, slot_tiles):
    tiles = slot_tiles.shape[0]
    n_p = h_p.shape[0] // TOK_TILE
    assert tiles == n_p + 1 and h_s.shape[0] == TOK_TILE
    n_rows = tiles * TOK_TILE * TOP_K
    return pl.pallas_call(
        functools.partial(_scatter_kernel, n_prompt_tiles=n_p),
        grid=(tiles,),
        in_specs=[
            pl.BlockSpec((1, 1, TOK_TILE * TOP_K), lambda i: (i, 0, 0), memory_space=pltpu.SMEM),
            pl.BlockSpec((TOK_TILE, D_MODEL), lambda i: (jnp.minimum(i, n_p - 1), 0)),
            pl.BlockSpec((TOK_TILE, D_MODEL), lambda i: (0, 0)),
        ],
        out_specs=pl.BlockSpec(memory_space=pl.ANY),
        out_shape=jax.ShapeDtypeStruct((n_rows, PACK_T, LANES), U32),
        scratch_shapes=[pltpu.VMEM((TOK_TILE, PACK_T, LANES), U32), pltpu.SemaphoreType.DMA],
        compiler_params=_params(("arbitrary",)),
        name="moe_scatter",
    )(slot_tiles, h_p, h_s)


def _expert_kernel(tile_ref, exp_ref, next_ref, lo_ref, hi_ref, first_ref, nvis_ref,
                   xs_ref, w1_hbm, b1_ref, w2_hbm, b2_ref, ys_ref, w1f, w2f, w1b, w2b, sem):
    v = pl.program_id(0)
    active = v < nvis_ref[0]
    changed = jnp.logical_or(v == 0, exp_ref[v] != exp_ref[jnp.maximum(v - 1, 0)])
    lo = lo_ref[v]
    hi = hi_ref[v]

    def weight_copies(e):
        return (pltpu.make_async_copy(w1_hbm.at[e], w1f, sem.at[0]),
                pltpu.make_async_copy(w2_hbm.at[e], w2f, sem.at[1]))

    @pl.when(v == 0)
    def _():
        for c in weight_copies(exp_ref[0]):
            c.start()

    @pl.when(jnp.logical_and(active, changed))
    def _():
        for c in weight_copies(exp_ref[v]):
            c.wait()
        w1b[...] = w1f[...].astype(BF16)
        w2b[...] = w2f[...].astype(BF16)

        @pl.when(next_ref[v] >= 0)
        def _():
            for c in weight_copies(next_ref[v]):
                c.start()

    def mlp_rows(r0):
        w = pltpu.einshape("mjc->m(jc)", xs_ref[r0:r0 + MOE_SUB])
        x_hi = pltpu.bitcast(w & U32(HI_MASK), F32).astype(BF16)
        x_lo = pltpu.bitcast(w << 16, F32).astype(BF16)
        hgu = jnp.dot(x_hi, w1b[0:PACK_W, :], preferred_element_type=F32)
        hgu = hgu + jnp.dot(x_lo, w1b[PACK_W:, :], preferred_element_type=F32) + b1_ref[0]
        x_glu = jnp.minimum(hgu[:, :D_FF], SWIGLU_LIMIT)
        x_lin = jnp.clip(hgu[:, D_FF:], -SWIGLU_LIMIT, SWIGLU_LIMIT)
        act = x_glu * jax.nn.sigmoid(SWIGLU_ALPHA * x_glu) * (x_lin + 1.0)
        y = jnp.dot(act.astype(BF16), w2b[...], preferred_element_type=F32) + b2_ref[0]
        return pltpu.einshape("m(jc)->mjc", y, j=ROW_T)

    subs = [r * MOE_SUB for r in range(MOE_M // MOE_SUB)]
    whole = jnp.logical_and(lo == 0, hi == MOE_M)

    @pl.when(jnp.logical_and(active, whole))
    def _():
        for r0 in subs:
            ys_ref[r0:r0 + MOE_SUB] = mlp_rows(r0)

    shared = jnp.logical_and(active, jnp.logical_not(whole))

    @pl.when(jnp.logical_and(shared, first_ref[v] == 1))
    def _():
        ys_ref[...] = jnp.zeros(ys_ref.shape, F32)

    for r0 in subs:
        @pl.when(jnp.logical_and(shared, jnp.logical_and(lo < r0 + MOE_SUB, hi > r0)))
        def _():
            rows = r0 + lax.broadcasted_iota(jnp.int32, (MOE_SUB, 1, 1), 0)
            mine = jnp.logical_and(rows >= lo, rows < hi)
            ys_ref[r0:r0 + MOE_SUB] = jnp.where(mine, mlp_rows(r0), ys_ref[r0:r0 + MOE_SUB])


def _moe_experts(xs, visits, w1, b1, w2, b2):
    n_rows = xs.shape[0]
    n_vis = visits[0].shape[0]
    tile_map = lambda v, tile, *_: (tile[v], 0, 0)
    exp_map = lambda v, tile, exp, *_: (exp[v], 0, 0)
    return pl.pallas_call(
        _expert_kernel,
        grid_spec=pltpu.PrefetchScalarGridSpec(
            num_scalar_prefetch=7,
            grid=(n_vis,),
            in_specs=[
                pl.BlockSpec((MOE_M, PACK_T, LANES), tile_map),
                pl.BlockSpec(memory_space=pl.ANY),
                pl.BlockSpec((1, 1, 2 * D_FF), exp_map),
                pl.BlockSpec(memory_space=pl.ANY),
                pl.BlockSpec((1, 1, D_MODEL), exp_map),
            ],
            out_specs=pl.BlockSpec((MOE_M, ROW_T, LANES), tile_map),
            scratch_shapes=[pltpu.VMEM((D_MODEL, 2 * D_FF), F32), pltpu.VMEM((D_FF, D_MODEL), F32),
                            pltpu.VMEM((D_MODEL, 2 * D_FF), BF16), pltpu.VMEM((D_FF, D_MODEL), BF16),
                            pltpu.SemaphoreType.DMA((2,))],
        ),
        out_shape=jax.ShapeDtypeStruct((n_rows, ROW_T, LANES), F32),
        compiler_params=_params(("arbitrary",)),
        name="moe_experts",
    )(*visits, xs, w1, b1.reshape(N_EXPERTS, 1, 2 * D_FF), w2, b2.reshape(N_EXPERTS, 1, D_MODEL))


def _combine_kernel(slot_ref, next_ref, gatep_ref, gates_ref, x1p_ref, x1s_ref, g2p_ref, g2s_ref, gpost_ref, ys_hbm,
                    yp_ref, ysm_ref, buf, sem, *, n_prompt_tiles):
    i = pl.program_id(0)
    last = pl.num_programs(0) - 1
    cur = lax.rem(i, 2)

    def gather(slots, b):
        def issue(r, carry):
            for k in range(TOP_K):
                pltpu.make_async_copy(ys_hbm.at[pl.ds(slots[0, 0, r * TOP_K + k], 1)],
                                      buf.at[b, pl.ds(k * TOK_TILE + r, 1)], sem.at[b]).start(priority=k % 2)
            return carry

        lax.fori_loop(0, TOK_TILE, issue, 0, unroll=4)

    @pl.when(i == 0)
    def _():
        gather(slot_ref, cur)

    @pl.when(i < last)
    def _():
        gather(next_ref, 1 - cur)

    for k in range(TOP_K):
        _row_wait(ys_hbm.at[pl.ds(0, TOK_TILE)], buf.at[cur, pl.ds(k * TOK_TILE, TOK_TILE)], sem.at[cur])
    is_sample = i >= n_prompt_tiles
    g = jnp.where(is_sample, gates_ref[...], gatep_ref[...])
    f = jnp.zeros((TOK_TILE, D_MODEL), F32)
    for k in range(TOP_K):
        rows = pltpu.einshape("mjc->m(jc)", buf[cur, pl.ds(k * TOK_TILE, TOK_TILE)])
        f = f + rows * g[:, k:k + 1]
    x1 = jnp.where(is_sample, x1s_ref[...], x1p_ref[...])
    g2 = jnp.where(is_sample, g2s_ref[...], g2p_ref[0])
    y = x1 + g2 * _rms(f, gpost_ref[...])

    @pl.when(jnp.logical_not(is_sample))
    def _():
        yp_ref[...] = y

    @pl.when(is_sample)
    def _():
        ysm_ref[...] = y


def _moe_combine(ys, slot_tiles, gates_p, gates_s, x1_p, x1_s, gate2_p, gate2_s, g_post, rows_per_batch):
    tiles = slot_tiles.shape[0]
    n_p = x1_p.shape[0] // TOK_TILE
    assert tiles == n_p + 1 and x1_s.shape[0] == TOK_TILE
    tiles_per_batch = rows_per_batch // TOK_TILE
    p_tile = lambda i: (jnp.minimum(i, n_p - 1), 0)
    const2 = lambda i: (0, 0)
    return pl.pallas_call(
        functools.partial(_combine_kernel, n_prompt_tiles=n_p),
        grid=(tiles,),
        in_specs=[
            pl.BlockSpec((1, 1, TOK_TILE * TOP_K), lambda i: (i, 0, 0), memory_space=pltpu.SMEM),
            pl.BlockSpec((1, 1, TOK_TILE * TOP_K), lambda i: (jnp.minimum(i + 1, tiles - 1), 0, 0),
                         memory_space=pltpu.SMEM),
            pl.BlockSpec((TOK_TILE, LANES), p_tile),
            pl.BlockSpec((TOK_TILE, LANES), const2),
            pl.BlockSpec((TOK_TILE, D_MODEL), p_tile),
            pl.BlockSpec((TOK_TILE, D_MODEL), const2),
            pl.BlockSpec((1, 1, D_MODEL), lambda i: (jnp.minimum(i, n_p - 1) // tiles_per_batch, 0, 0)),
            pl.BlockSpec((TOK_TILE, D_MODEL), const2),
            pl.BlockSpec((1, D_MODEL), const2),
            pl.BlockSpec(memory_space=pl.ANY),
        ],
        out_specs=[pl.BlockSpec((TOK_TILE, D_MODEL), p_tile), pl.BlockSpec((TOK_TILE, D_MODEL), const2)],
        out_shape=[jax.ShapeDtypeStruct(x1_p.shape, F32), jax.ShapeDtypeStruct(x1_s.shape, F32)],
        scratch_shapes=[pltpu.VMEM((2, TOP_K * TOK_TILE, ROW_T, LANES), F32), pltpu.SemaphoreType.DMA((2,))],
        compiler_params=_params(("arbitrary",)),
        name="moe_combine",
    )(slot_tiles, slot_tiles, gates_p, gates_s, x1_p, x1_s, gate2_p, gate2_s, g_post.reshape(1, D_MODEL), ys)


def _moe_plan(routes, counts):
    a = sum(r.shape[0] for r in routes) * TOP_K
    assert a % MOE_M == 0
    g_end = jnp.cumsum(counts)
    g_start = g_end - counts

    def slots(route):
        flat_e = route[:, :TOP_K].reshape(-1)
        rank = route[:, TOP_K:2 * TOP_K].reshape(-1)
        onehot = flat_e[:, None] == jnp.arange(N_EXPERTS)[None, :]
        return (jnp.sum(jnp.where(onehot, g_start[None, :], 0), axis=1) + rank).astype(jnp.int32)

    slot_of = jnp.concatenate([slots(r) for r in routes])

    n_tiles = a // MOE_M
    n_vis = n_tiles + N_EXPERTS - 1
    first_tile = g_start // MOE_M
    last_tile = jnp.where(counts > 0, (g_end - 1) // MOE_M, first_tile)
    nvis_e = jnp.where(counts > 0, last_tile - first_tile + 1, 0)
    v_end = jnp.cumsum(nvis_e)
    v_start = v_end - nvis_e
    total = v_end[-1]
    v = jnp.arange(n_vis)
    vc = jnp.minimum(v, total - 1)
    e_of = jnp.minimum(jnp.sum((vc[:, None] >= v_end[None, :]).astype(jnp.int32), axis=1), N_EXPERTS - 1)
    is_e = e_of[:, None] == jnp.arange(N_EXPERTS)[None, :]
    of_e = lambda t: jnp.sum(jnp.where(is_e, t[None, :], 0), axis=1)
    tile_of = of_e(first_tile) + (vc - of_e(v_start))
    lo = jnp.maximum(of_e(g_start), tile_of * MOE_M) - tile_of * MOE_M
    hi = jnp.minimum(of_e(g_end), (tile_of + 1) * MOE_M) - tile_of * MOE_M
    prev_tile = jnp.concatenate([jnp.full((1,), -1, tile_of.dtype), tile_of[:-1]])
    first = (tile_of != prev_tile).astype(jnp.int32)
    experts = jnp.arange(N_EXPERTS)
    later = (experts[None, :] > experts[:, None]) & (counts[None, :] > 0)
    next_e = jnp.min(jnp.where(later, experts[None, :], N_EXPERTS), axis=1)
    next_e = jnp.where(next_e < N_EXPERTS, next_e, -1)
    i32 = lambda t: t.astype(jnp.int32)
    visits = (i32(tile_of), i32(e_of), i32(of_e(next_e)), i32(lo), i32(hi), first, i32(total).reshape(1))
    return slot_of, visits


def _reorder_w_in(w_in):
    off_om_end = 4 * D_M
    off_g_end = off_om_end + 2 * H_M
    w = jnp.concatenate([
        w_in[:, :off_om_end],
        w_in[:, off_g_end:],
        w_in[:, off_om_end:off_g_end],
        jnp.zeros((D_MODEL, LANES - 2 * H_M), w_in.dtype),
    ], axis=1)
    return w.astype(BF16)


def kernel(x_prompt, x_sample, cache_k, cache_v, state_conv, state_C, state_n, state_m, c_prompt, c_sample, w_ada, b_ada, g_pre_mix, g_post_mix, g_pre_ffn, g_post_ffn, w_in, b_ig, b_fg, conv_w, conv_b, mh_norm, rel_bias, w_out, w_router, b_router, w1, b1, w2, b2):
    depth = w_in.shape[0]
    assert depth == 1
    l = 0
    nb_p, seq, _ = x_prompt.shape
    nb_s, n_tok, _ = x_sample.shape
    rows_s = nb_s * n_tok
    assert rows_s == TOK_TILE

    n_c = nb_p + nb_s
    c_pad = -(-n_c // 8) * 8
    c_all = jnp.concatenate([c_prompt, c_sample, jnp.zeros((c_pad - n_c, D_MODEL), F32)], axis=0)
    mod = _adaln(c_all, w_ada[l], b_ada[l])
    mods = [mod[:, i * D_MODEL:(i + 1) * D_MODEL] for i in range(6)]
    mods_p = [m[:nb_p].reshape(nb_p, 1, D_MODEL) for m in mods]
    mods_s = [jnp.repeat(m[nb_p:n_c], n_tok, axis=0).reshape(1, rows_s, D_MODEL) for m in mods]

    w_r = _reorder_w_in(w_in[l])
    bias_g = jnp.concatenate([b_ig[l], b_fg[l], jnp.zeros((LANES - 2 * H_M,), F32)]).reshape(1, LANES)
    w_out_b = w_out[l].astype(BF16)
    w_router_f = jnp.concatenate([w_router[l], jnp.zeros((D_MODEL, LANES - N_EXPERTS), F32)], axis=1)
    w_router_hi = w_router_f.astype(BF16)
    w_router_p = jnp.stack([w_router_hi, (w_router_f - w_router_hi.astype(F32)).astype(BF16)])
    b_router_p = jnp.concatenate([b_router[l].astype(F32), jnp.zeros((LANES - N_EXPERTS,), F32)]).reshape(1, LANES)

    shift1, scale1, gate1, shift2, scale2, gate2_p = mods_p
    dils = tuple(d for _, d in BRANCHES)
    assert dils[0] == 1
    u, v_m, og, gts, qa, ka, va, *dilated = _inproj(x_prompt, g_pre_mix[l], scale1, shift1, w_r, bias_g, tm=512,
                                                   dilations=dils[1:])
    qkv_d = [(qa, ka, va)] + [tuple(dilated[3 * n:3 * n + 3]) for n in range(len(dils) - 1)]
    gates_row = gts[:, :, :8].transpose(0, 2, 1)
    zeros = lambda *s: jnp.zeros(s, F32)
    mem, c_p, n_p, m_p = _mlstm(u, v_m, og, gts, gates_row, zeros(nb_p, 8, 2 * D_M),
                                zeros(nb_p, H_M, DK_M, DK_M), zeros(nb_p, H_M, DK_M), zeros(nb_p, H_M, LANES),
                                conv_w[l], conv_b[l], mh_norm[l], MLSTM_L_PROMPT)
    branches = [_attn_branch(*qkv, _prompt_bias(rel_bias, w, d), d) for qkv, (w, d) in zip(qkv_d, BRANCHES)]
    x1_p, h2_p, idx_p, gate_p, cnt_p = _postmix(mem, branches, x_prompt, gate1, scale2, shift2,
                                                g_post_mix[l], g_pre_ffn[l], w_out_b, w_router_p, b_router_p,
                                                zeros(1, LANES), tm=512, dilations=dils)
    keep = min(WINDOW_MAX, seq)
    k_p, v_p = _window_heads(ka, va, keep)
    conv_p = u[:, seq - (CONV_W - 1):]

    shift1, scale1, gate1, shift2, scale2, gate2_s = mods_s
    x_s = x_sample.reshape(1, rows_s, D_MODEL)
    u_s, v_s, og_s, gts_s, qa_s, ka_s, va_s = _inproj(x_s, g_pre_mix[l], scale1, shift1, w_r, bias_g, tm=rows_s)
    per_b = lambda t: t.reshape(nb_s, n_tok, t.shape[-1])
    ls = MLSTM_L_SAMPLE
    pad_rows = lambda t, n: jnp.concatenate([t, jnp.zeros((nb_s, n - t.shape[1], t.shape[2]), t.dtype)], axis=1)
    inert = jnp.concatenate([jnp.full((H_M,), NEG_INF, F32), jnp.zeros((LANES - H_M,), F32)])
    gts_pad = jnp.concatenate([per_b(gts_s), jnp.broadcast_to(inert, (nb_s, ls - n_tok, LANES))], axis=1)
    tail0 = jnp.concatenate([zeros(nb_s, 8 - (CONV_W - 1), 2 * D_M), state_conv[l].astype(F32)], axis=1)
    m0x = jnp.broadcast_to(state_m[l].astype(F32)[:, :, None], (nb_s, H_M, LANES))
    mem_s, c_s, n_s, m_s = _mlstm(pad_rows(per_b(u_s), ls), pad_rows(per_b(v_s), ls), pad_rows(per_b(og_s), ls),
                                  gts_pad, gts_pad[:, :, :8].transpose(0, 2, 1), tail0,
                                  state_C[l].astype(F32), state_n[l].astype(F32), m0x,
                                  conv_w[l], conv_b[l], mh_norm[l], ls)
    mem_s = mem_s[:, :n_tok].reshape(1, rows_s, D_M)
    heads = lambda t: t.reshape(nb_s, n_tok, H_A, HD_A)
    k_new, v_new = heads(ka_s), heads(va_s)
    n_past = cache_k.shape[2]
    assert n_tok <= TOK_PAD
    q_h = jnp.pad(heads(qa_s).transpose(0, 2, 1, 3), ((0, 0), (0, 0), (0, TOK_PAD - n_tok), (0, 0)))
    new_t = lambda t: jnp.pad(t.transpose(0, 2, 3, 1), ((0, 0), (0, 0), (0, 0), (0, LANES - n_tok)))
    cache_t = lambda c: c[l].astype(F32).transpose(0, 2, 3, 1)
    att_s = _sample_attention(q_h, cache_t(cache_k), cache_t(cache_v), new_t(k_new), new_t(v_new),
                              *_sample_bias(rel_bias, n_tok, n_past))
    att_s = att_s[:, :, :n_tok].transpose(0, 2, 1, 3).reshape(1, rows_s, D_A)
    zero_o = zeros(1, rows_s, D_A)
    ninf = jnp.full((1, rows_s, D_A), NEG_INF, F32)
    branches_s = [(att_s, zero_o), (zero_o, ninf), (zero_o, ninf)]
    x1_s, h2_s, idx_s, gate_s, cnt_all = _postmix(mem_s, branches_s, x_s, gate1, scale2, shift2,
                                                  g_post_mix[l], g_pre_ffn[l], w_out_b, w_router_p, b_router_p,
                                                  cnt_p, tm=rows_s)

    t_p = nb_p * seq
    slot_of, visits = _moe_plan([idx_p.reshape(t_p, LANES), idx_s.reshape(rows_s, LANES)],
                                cnt_all[0, :N_EXPERTS].astype(jnp.int32))
    slot_tiles = slot_of.reshape((t_p + rows_s) // TOK_TILE, 1, TOK_TILE * TOP_K)
    xs = _moe_scatter(h2_p.reshape(t_p, D_MODEL), h2_s.reshape(rows_s, D_MODEL), slot_tiles)
    ys = _moe_experts(xs, visits, w1[l], b1[l], w2[l], b2[l])
    y_p, y_s = _moe_combine(ys, slot_tiles, gate_p.reshape(t_p, LANES), gate_s.reshape(rows_s, LANES), x1_p.reshape(t_p, D_MODEL), x1_s.reshape(rows_s, D_MODEL),
                            gate2_p, gate2_s.reshape(rows_s, D_MODEL), g_post_ffn[l], seq)
    y_prompt = y_p.reshape(nb_p, seq, D_MODEL)
    y_sample = y_s.reshape(nb_s, n_tok, D_MODEL)

    st = lambda t: t[None]
    conv_s = jnp.concatenate([state_conv[l].astype(F32), per_b(u_s)], axis=1)[:, -(CONV_W - 1):]
    return (y_prompt, y_sample, st(k_p), st(v_p), st(conv_p), st(c_p), st(n_p), st(m_p[:, :, 0]),
            st(k_new), st(v_new), st(conv_s), st(c_s), st(n_s), st(m_s[:, :, 0]))
```

```python
import functools
import math

import jax
import jax.numpy as jnp
import numpy as np
from jax import lax
from jax.experimental import pallas as pl
from jax.experimental.pallas import tpu as pltpu

F32 = jnp.float32
BF16 = jnp.bfloat16
U32 = jnp.uint32
NEG_INF = float("-inf")

D_MODEL = 1024
H_M = 4
D_M = 512
DK_M = 128
CONV_W = 4
H_A = 8
D_A = 512
HD_A = 64
BRANCHES = ((128, 1), (512, 4), (2048, 16))
WINDOW_MAX = 2048
QB = 128
N_BUCKETS = 32
MAX_DISTANCE = 2048
N_EXPERTS = 32
TOP_K = 4
D_FF = 1024
SWIGLU_LIMIT = 7.0
SWIGLU_ALPHA = 1.702
EPS = 1e-6

LANES = 128
C_U, C_V, C_O, C_QA, C_KA, C_VA, C_G = 0, 1024, 1536, 2048, 2560, 3072, 3584
D_IN_R = C_G + LANES

MLSTM_L_PROMPT = 256
MLSTM_L_SAMPLE = 32
ATT_QL = 1024
MOE_M = 512
MOE_SUB = 256
TOK_TILE = 128
PACK_W = D_MODEL // 2
PACK_T = PACK_W // LANES
ROW_T = D_MODEL // LANES
HI_MASK = 0xFFFF0000
VMEM_LIMIT = 56 * 1024 * 1024


def _params(sem, vmem=None, fuse_inputs=0):
    return pltpu.CompilerParams(dimension_semantics=sem, vmem_limit_bytes=vmem or VMEM_LIMIT,
                                allow_input_fusion=[True] * fuse_inputs if fuse_inputs else None)


def _rms(x, g):
    return x * lax.rsqrt(jnp.mean(x * x, axis=-1, keepdims=True) + EPS) * g


def _log_sigmoid(t):
    return jnp.minimum(t, 0.0) - jnp.log1p(jnp.exp(-jnp.abs(t)))


def _ada_kernel(c_ref, w_ref, b_ref, o_ref):
    c = c_ref[...]
    a = (c * jax.nn.sigmoid(c)).astype(BF16)
    o_ref[...] = jnp.dot(a, w_ref[...].astype(BF16), preferred_element_type=F32) + b_ref[...]


def _adaln(c_all, w_ada, b_ada):
    rows = c_all.shape[0]
    n = w_ada.shape[1]
    tn = 1024
    return pl.pallas_call(
        _ada_kernel,
        grid=(n // tn,),
        in_specs=[
            pl.BlockSpec((rows, D_MODEL), lambda j: (0, 0)),
            pl.BlockSpec((D_MODEL, tn), lambda j: (0, j)),
            pl.BlockSpec((1, tn), lambda j: (0, j)),
        ],
        out_specs=pl.BlockSpec((rows, tn), lambda j: (0, j)),
        out_shape=jax.ShapeDtypeStruct((rows, n), F32),
        compiler_params=_params(("arbitrary",), fuse_inputs=3),
        name="adaln",
    )(c_all, w_ada, b_ada.reshape(1, n))


def _inproj_kernel(x_ref, g_ref, sc_ref, sh_ref, w_ref, bg_ref,
                   u_ref, v_ref, o_ref, gt_ref, qa_ref, ka_ref, va_ref, *rest, dilations):
    dil_refs, slab_s = (rest[:-1], rest[-1]) if dilations else (rest, None)
    x = x_ref[0]
    h = _rms(x, g_ref[...]) * (1.0 + sc_ref[0]) + sh_ref[0]
    hb = h.astype(BF16)

    def seg(lo, width):
        return jnp.dot(hb, w_ref[:, lo:lo + width], preferred_element_type=F32)

    u_ref[0] = seg(C_U, 2 * D_M)
    v_ref[0] = seg(C_V, D_M)
    o_ref[0] = jax.nn.sigmoid(seg(C_O, D_M))
    qkv = (seg(C_QA, D_A) * (HD_A ** -0.5), seg(C_KA, D_A), seg(C_VA, D_A))
    for ref, val in zip((qa_ref, ka_ref, va_ref), qkv):
        ref[0] = val
    n_slab = D_A // LANES
    rows = x.shape[0]
    for j, val in enumerate(qkv if dilations else ()):
        for c in range(n_slab):
            slab_s[j, c] = val[:, c * LANES:(c + 1) * LANES]
        for n, dil in enumerate(dilations):
            for r in range(dil):
                for c in range(n_slab):
                    lo = r * D_A + c * LANES
                    dil_refs[3 * n + j][0, :, lo:lo + LANES] = slab_s[j, c, pl.ds(r, rows // dil, stride=dil), :]
    t = seg(C_G, LANES) + bg_ref[...]
    lane = lax.broadcasted_iota(jnp.int32, t.shape, 1)
    gt_ref[0] = jnp.where(lane < H_M, t, _log_sigmoid(t))


def _inproj(x3, g_pre, scale, shift, w_r, bias_g, tm, dilations=()):
    nb, rows, _ = x3.shape
    per_row_mod = scale.shape[1] != 1
    mod_block = (1, tm, D_MODEL) if per_row_mod else (1, 1, D_MODEL)
    mod_map = (lambda b, i: (b, i, 0)) if per_row_mod else (lambda b, i: (b, 0, 0))
    row_map = lambda b, i: (b, i, 0)
    widths = (2 * D_M, D_M, D_M, LANES, D_A, D_A, D_A)
    dil_specs = [pl.BlockSpec((1, tm // d, d * D_A), row_map) for d in dilations for _ in range(3)]
    dil_shapes = [jax.ShapeDtypeStruct((nb, rows // d, d * D_A), F32) for d in dilations for _ in range(3)]
    return pl.pallas_call(
        functools.partial(_inproj_kernel, dilations=tuple(dilations)),
        grid=(nb, rows // tm),
        in_specs=[
            pl.BlockSpec((1, tm, D_MODEL), row_map),
            pl.BlockSpec((1, D_MODEL), lambda b, i: (0, 0)),
            pl.BlockSpec(mod_block, mod_map),
            pl.BlockSpec(mod_block, mod_map),
            pl.BlockSpec((D_MODEL, D_IN_R), lambda b, i: (0, 0)),
            pl.BlockSpec((1, LANES), lambda b, i: (0, 0)),
        ],
        out_specs=[pl.BlockSpec((1, tm, w), row_map) for w in widths] + dil_specs,
        out_shape=[jax.ShapeDtypeStruct((nb, rows, w), F32) for w in widths] + dil_shapes,
        scratch_shapes=[pltpu.VMEM((3, D_A // LANES, tm, LANES), F32)] if dilations else [],
        compiler_params=_params(("arbitrary", "arbitrary"), fuse_inputs=6),
        name="inproj",
    )(x3, g_pre.reshape(1, D_MODEL), scale, shift, w_r, bias_g)


def _mlstm_kernel(u_ref, v_ref, og_ref, gc_ref, gr_ref, tail0_ref, c0_ref, n0_ref, m0_ref,
                  cw_ref, cb_ref, mh_ref,
                  mem_ref, c_out_ref, n_out_ref, m_out_ref,
                  c_s, n_s, m_s, tail_s, uext_s, *, chunk):
    L = chunk
    step = pl.program_id(1)

    @pl.when(step == 0)
    def _():
        c_s[...] = c0_ref[0]
        n_s[...] = n0_ref[0]
        m_s[...] = m0_ref[0]
        tail_s[...] = tail0_ref[0]

    u = u_ref[0]
    uext_s[0:8, :] = tail_s[...]
    uext_s[8:8 + L, :] = u
    acc = cb_ref[...] + uext_s[pl.ds(5, L), :] * cw_ref[0:1, :]
    acc = acc + uext_s[pl.ds(6, L), :] * cw_ref[1:2, :]
    acc = acc + uext_s[pl.ds(7, L), :] * cw_ref[2:3, :]
    acc = acc + u * cw_ref[3:4, :]
    tail_s[...] = uext_s[pl.ds(L, 8), :]
    qk = acc * jax.nn.sigmoid(acc)

    gc = gc_ref[0]
    gr = gr_ref[0]
    row = lax.broadcasted_iota(jnp.int32, (L, L), 0)
    col = lax.broadcasted_iota(jnp.int32, (L, L), 1)
    causal = col <= row
    tri = causal.astype(BF16)
    tri_t = (row <= col).astype(BF16)
    lane = lax.broadcasted_iota(jnp.int32, gc.shape, 1)
    srow = lax.broadcasted_iota(jnp.int32, gr.shape, 0)

    def bf16_parts(x):
        hi = x.astype(BF16)
        r1 = x - hi.astype(F32)
        mid = r1.astype(BF16)
        return hi, mid, (r1 - mid.astype(F32)).astype(BF16)

    b_col_all = sum(jnp.dot(tri, p, preferred_element_type=F32)
                    for p in bf16_parts(jnp.where(lane >= H_M, gc, 0.0)))
    b_row_all = sum(jnp.dot(p, tri_t, preferred_element_type=F32)
                    for p in bf16_parts(jnp.where(srow >= H_M, gr, 0.0)))

    v_all = v_ref[0]
    og = og_ref[0]
    for h in range(H_M):
        q = qk[:, h * DK_M:(h + 1) * DK_M]
        k = qk[:, D_M + h * DK_M:D_M + (h + 1) * DK_M] * (DK_M ** -0.5)
        v = v_all[:, h * DK_M:(h + 1) * DK_M]
        li_row = gr[h:h + 1, :]
        li_col = gc[:, h:h + 1]
        b_row = b_row_all[H_M + h:H_M + h + 1, :]
        b_col = b_col_all[:, H_M + h:H_M + h + 1]
        m0 = m_s[h:h + 1, 0:1]
        c0 = c_s[h]
        n0 = n_s[h:h + 1, :]

        d = jnp.where(causal, b_col - b_row + li_row, NEG_INF)
        g_state = b_col + m0
        m = jnp.maximum(jnp.max(d, axis=-1, keepdims=True), g_state)
        w_state = jnp.exp(g_state - m)
        qb = q.astype(BF16)
        kb = k.astype(BF16)
        vb = v.astype(BF16)
        s = lax.dot_general(qb, kb, (((1,), (1,)), ((), ())), preferred_element_type=F32)
        s = s * jnp.exp(d - m)
        num = jnp.dot(s.astype(BF16), vb, preferred_element_type=F32)
        num = num + w_state * jnp.dot(qb, c0.astype(BF16), preferred_element_type=F32)
        den = jnp.sum(s, axis=-1, keepdims=True) + w_state * jnp.sum(q * n0, axis=-1, keepdims=True)
        hh = num / jnp.maximum(jnp.abs(den), jnp.exp(-m))

        b_last = b_col[L - 1:L, :]
        g_tok_row = b_last - b_row + li_row
        m_new = jnp.maximum(b_last + m0, jnp.max(g_tok_row, axis=-1, keepdims=True))
        w_tok_col = jnp.exp(b_last - b_col + li_col - m_new)
        decay = jnp.exp(b_last + m0 - m_new)
        kw = k * w_tok_col
        c_s[h] = decay * c0 + jnp.dot(kw.T.astype(BF16), vb, preferred_element_type=F32)
        n_s[h:h + 1, :] = decay * n0 + jnp.sum(kw, axis=0, keepdims=True)
        m_s[h:h + 1, :] = jnp.broadcast_to(m_new, (1, LANES))

        hn = hh * lax.rsqrt(jnp.mean(hh * hh, axis=-1, keepdims=True) + EPS)
        hn = hn * mh_ref[:, h * DK_M:(h + 1) * DK_M]
        mem_ref[0, :, h * DK_M:(h + 1) * DK_M] = og[:, h * DK_M:(h + 1) * DK_M] * hn

    @pl.when(step == pl.num_programs(1) - 1)
    def _():
        c_out_ref[0] = c_s[...]
        n_out_ref[0] = n_s[...]
        m_out_ref[0] = m_s[...]


def _mlstm(u, v, og, gates_col, gates_row, tail0, c0, n0, m0x, conv_w, conv_b, mh_norm, chunk):
    nb, rows, _ = u.shape
    nc = rows // chunk
    row_map = lambda b, c: (b, c, 0)
    bat3 = lambda b, c: (b, 0, 0)
    const2 = lambda b, c: (0, 0)
    return pl.pallas_call(
        functools.partial(_mlstm_kernel, chunk=chunk),
        grid=(nb, nc),
        in_specs=[
            pl.BlockSpec((1, chunk, 2 * D_M), row_map),
            pl.BlockSpec((1, chunk, D_M), row_map),
            pl.BlockSpec((1, chunk, D_M), row_map),
            pl.BlockSpec((1, chunk, LANES), row_map),
            pl.BlockSpec((1, 8, chunk), lambda b, c: (b, 0, c)),
            pl.BlockSpec((1, 8, 2 * D_M), bat3),
            pl.BlockSpec((1, H_M, DK_M, DK_M), lambda b, c: (b, 0, 0, 0)),
            pl.BlockSpec((1, H_M, DK_M), bat3),
            pl.BlockSpec((1, H_M, LANES), bat3),
            pl.BlockSpec((CONV_W, 2 * D_M), const2),
            pl.BlockSpec((1, 2 * D_M), const2),
            pl.BlockSpec((1, D_M), const2),
        ],
        out_specs=[
            pl.BlockSpec((1, chunk, D_M), row_map),
            pl.BlockSpec((1, H_M, DK_M, DK_M), lambda b, c: (b, 0, 0, 0)),
            pl.BlockSpec((1, H_M, DK_M), bat3),
            pl.BlockSpec((1, H_M, LANES), bat3),
        ],
        out_shape=[
            jax.ShapeDtypeStruct((nb, rows, D_M), F32),
            jax.ShapeDtypeStruct((nb, H_M, DK_M, DK_M), F32),
            jax.ShapeDtypeStruct((nb, H_M, DK_M), F32),
            jax.ShapeDtypeStruct((nb, H_M, LANES), F32),
        ],
        scratch_shapes=[
            pltpu.VMEM((H_M, DK_M, DK_M), F32),
            pltpu.VMEM((H_M, DK_M), F32),
            pltpu.VMEM((H_M, LANES), F32),
            pltpu.VMEM((8, 2 * D_M), F32),
            pltpu.VMEM((chunk + 8, 2 * D_M), F32),
        ],
        compiler_params=_params(("arbitrary", "arbitrary"), fuse_inputs=12),
        name="mlstm",
    )(u, v, og, gates_col, gates_row, tail0, c0, n0, m0x,
      conv_w, conv_b.reshape(1, 2 * D_M), mh_norm.reshape(1, D_M))


def _attn_kernel(q_ref, kc_ref, vc_ref, kp_ref, vp_ref, bias_ref, ones_ref, o_ref, l_ref, k_s, v_s, *, ql):
    first = pl.program_id(2) == 0
    k_s[0:QB, :] = kp_ref[0].astype(BF16)
    k_s[QB:QB + ql, :] = kc_ref[0].astype(BF16)
    v_s[0:QB, :] = vp_ref[0].astype(BF16)
    v_s[QB:QB + ql, :] = vc_ref[0].astype(BF16)
    col = lax.broadcasted_iota(jnp.int32, (QB, 2 * QB), 1)
    pad_keys = jnp.logical_and(first, col < QB)
    head0 = lax.broadcasted_iota(jnp.int32, (1, LANES), 1) < HD_A
    n_slab = D_A // LANES
    n_res = q_ref.shape[2] // D_A
    pad_all = jnp.concatenate([pad_keys] * H_A, axis=0)

    def raw_logits(c, j):
        sl = slice(c * LANES, (c + 1) * LANES)
        q = q_ref[0, j * QB:(j + 1) * QB, sl].astype(BF16)
        zero = jnp.zeros_like(q)
        q2 = jnp.concatenate([jnp.where(head0, q, zero), jnp.where(head0, zero, q)], axis=0)
        return lax.dot_general(q2, k_s[j * QB:(j + 2) * QB, sl], (((1,), (1,)), ((), ())),
                               preferred_element_type=F32)

    def softmax(j, logits):
        logits = logits + bias_ref[...].reshape(H_A * QB, 2 * QB)
        if j == 0:
            logits = jnp.where(pad_all, NEG_INF, logits)
        m = jnp.max(logits, axis=-1, keepdims=True)
        return jnp.exp(logits - m).astype(BF16), m

    def finish(c, j, p, m):
        sl = slice(c * LANES, (c + 1) * LANES)
        rows = slice(j * QB, (j + 1) * QB)
        p_cat = jnp.concatenate([p[:QB], p[QB:]], axis=1)
        v_all = v_s[j * QB:(j + 2) * QB, sl]
        v_bd = jnp.concatenate([jnp.where(head0, v_all, jnp.zeros_like(v_all)),
                                jnp.where(head0, jnp.zeros_like(v_all), v_all)], axis=0)
        o = jnp.dot(p_cat, v_bd, preferred_element_type=F32)
        s = jnp.dot(p_cat, ones_ref[...], preferred_element_type=F32)
        o_ref[0, rows, sl] = o / s
        l_ref[0, rows, sl] = jnp.where(head0, m[:QB], m[QB:]) + jnp.log(s)

    for res in range(n_res):
        slabs = range(res * n_slab, (res + 1) * n_slab)
        for j in range(ql // QB):
            p, m = softmax(j, jnp.concatenate([raw_logits(c, j) for c in slabs], axis=0))
            for n, c in enumerate(slabs):
                part = slice(2 * n * QB, 2 * (n + 1) * QB)
                finish(c, j, p[part], m[part])


def _attn_branch(qa, ka, va, bias, dil):
    nb, ld, _ = qa.shape
    ql = min(ATT_QL, ld)
    n_res = min(dil, ATT_QL // ql)
    width = n_res * D_A
    cur = lambda b, r, i: (b, i, r)
    prev = lambda b, r, i: (b, jnp.maximum(i * (ql // QB) - 1, 0), r)
    ones_bd = np.zeros((4 * QB, LANES), np.float32)
    ones_bd[:2 * QB, :HD_A] = 1.0
    ones_bd[2 * QB:, HD_A:] = 1.0
    return pl.pallas_call(
        functools.partial(_attn_kernel, ql=ql),
        grid=(nb, dil // n_res, ld // ql),
        in_specs=[
            pl.BlockSpec((1, ql, width), cur),
            pl.BlockSpec((1, ql, width), cur),
            pl.BlockSpec((1, ql, width), cur),
            pl.BlockSpec((1, QB, width), prev),
            pl.BlockSpec((1, QB, width), prev),
            pl.BlockSpec((H_A, QB, 2 * QB), lambda b, r, i: (0, 0, 0)),
            pl.BlockSpec((4 * QB, LANES), lambda b, r, i: (0, 0)),
        ],
        out_specs=[pl.BlockSpec((1, ql, width), cur), pl.BlockSpec((1, ql, width), cur)],
        out_shape=[jax.ShapeDtypeStruct((nb, ld, dil * D_A), F32)] * 2,
        scratch_shapes=[pltpu.VMEM((QB + ql, width), BF16), pltpu.VMEM((QB + ql, width), BF16)],
        compiler_params=_params(("arbitrary", "arbitrary", "arbitrary"), fuse_inputs=7),
        name=f"attn_d{dil}",
    )(qa, ka, va, ka, va, bias, jnp.asarray(ones_bd, BF16))


def _heads_kernel(k_ref, v_ref, ko_ref, vo_ref):
    for src, dst in ((k_ref, ko_ref), (v_ref, vo_ref)):
        for c in range(D_A // LANES):
            xt = src[0, :, c * LANES:(c + 1) * LANES].T
            dst[0, 2 * c] = xt[:HD_A]
            dst[0, 2 * c + 1] = xt[HD_A:]


def _window_heads(ka, va, keep, tm=512):
    nb, seq, _ = ka.shape
    first = (seq - keep) // tm
    in_spec = pl.BlockSpec((1, tm, D_A), lambda b, i: (b, first + i, 0))
    out_spec = pl.BlockSpec((1, H_A, HD_A, tm), lambda b, i: (b, 0, 0, i))
    kt, vt = pl.pallas_call(
        _heads_kernel,
        grid=(nb, keep // tm),
        in_specs=[in_spec, in_spec],
        out_specs=[out_spec, out_spec],
        out_shape=[jax.ShapeDtypeStruct((nb, H_A, HD_A, keep), F32)] * 2,
        compiler_params=_params(("arbitrary", "arbitrary"), fuse_inputs=2),
        name="window_heads",
    )(ka, va)
    return kt.transpose(0, 3, 1, 2), vt.transpose(0, 3, 1, 2)


def _t5_bucket(dist):
    max_exact = N_BUCKETS // 2
    n = jnp.maximum(dist, 1).astype(F32)
    large = max_exact + (jnp.log(n / max_exact) / math.log(MAX_DISTANCE / max_exact)
                         * (N_BUCKETS - max_exact)).astype(jnp.int32)
    large = jnp.minimum(large, N_BUCKETS - 1)
    return jnp.where(dist < max_exact, dist, large)


def _branch_bias_vec(rel_bias, window, dil):
    return rel_bias[_t5_bucket(dil * jnp.arange(window // dil + 1))].astype(F32)


def _prompt_bias(rel_bias, window, dil):
    assert window // dil == QB
    n = 2 * QB
    vec = _branch_bias_vec(rel_bias, window, dil)
    vecp = jnp.concatenate([vec[::-1], jnp.full((n - QB - 1, H_A), NEG_INF, F32)], axis=0)
    g = jnp.roll(vecp, QB - 1, axis=0)
    skew = jnp.tile(g, (QB + 1, 1))[:QB * (n + 1)].reshape(QB, n + 1, H_A)[:, :n]
    return skew[::-1].transpose(2, 0, 1)


TOK_PAD = 8


def _sattn_kernel(q_ref, kt_ref, vt_ref, knt_ref, vnt_ref, bc_ref, bn_ref, att_ref):
    nt = (((1,), (1,)), ((), ()))
    for h in range(H_A):
        q = q_ref[0, h].astype(BF16)
        lc = jnp.dot(q, kt_ref[0, h].astype(BF16), preferred_element_type=F32)
        ln = jnp.dot(q, knt_ref[0, h].astype(BF16), preferred_element_type=F32)
        lcs = [lc + bc_ref[d, h] for d in range(len(BRANCHES))]
        lns = [ln + bn_ref[d, h] for d in range(len(BRANCHES))]
        m = functools.reduce(jnp.maximum, [jnp.max(x, axis=-1, keepdims=True) for x in lcs + lns])
        pc = functools.reduce(jnp.add, [jnp.exp(x - m) for x in lcs])
        pn = functools.reduce(jnp.add, [jnp.exp(x - m) for x in lns])
        s = jnp.sum(pc, axis=-1, keepdims=True) + jnp.sum(pn, axis=-1, keepdims=True)
        o = lax.dot_general(pc.astype(BF16), vt_ref[0, h].astype(BF16), nt, preferred_element_type=F32)
        o = o + lax.dot_general(pn.astype(BF16), vnt_ref[0, h].astype(BF16), nt, preferred_element_type=F32)
        att_ref[0, h] = o / s


def _sample_attention(q_h, kt, vt, knt, vnt, bias_c, bias_n):
    nb, _, _, n_past = kt.shape
    bat = lambda b: (b, 0, 0, 0)
    const = lambda b: (0, 0, 0, 0)
    return pl.pallas_call(
        _sattn_kernel,
        grid=(nb,),
        in_specs=[
            pl.BlockSpec((1, H_A, TOK_PAD, HD_A), bat),
            pl.BlockSpec((1, H_A, HD_A, n_past), bat),
            pl.BlockSpec((1, H_A, HD_A, n_past), bat),
            pl.BlockSpec((1, H_A, HD_A, LANES), bat),
            pl.BlockSpec((1, H_A, HD_A, LANES), bat),
            pl.BlockSpec(bias_c.shape, const),
            pl.BlockSpec(bias_n.shape, const),
        ],
        out_specs=pl.BlockSpec((1, H_A, TOK_PAD, HD_A), bat),
        out_shape=jax.ShapeDtypeStruct((nb, H_A, TOK_PAD, HD_A), F32),
        compiler_params=_params(("arbitrary",), fuse_inputs=7),
        name="sample_attn",
    )(q_h, kt, vt, knt, vnt, bias_c, bias_n)


def _sample_bias(rel_bias, n_tok, n_past):
    ninf = lambda n: jnp.full((n, H_A), NEG_INF, F32)
    cached, new = [], []
    for window, dil in BRANCHES:
        nk = window // dil
        rows = n_past // dil
        assert n_past % dil == 0 and rows >= nk
        vec = _branch_bias_vec(rel_bias, window, dil)
        res = jnp.arange(dil)[None, :, None]
        tc, tn = [], []
        for t in range(n_tok):
            near = t // dil + 1
            col = jnp.concatenate([ninf(rows - (nk + 1 - near)), vec[near:nk + 1][::-1]], axis=0)
            tab = jnp.where(res == t % dil, col[:, None, :], NEG_INF).reshape(n_past, H_A)
            tc.append(tab)
            keys = [vec[(t - j) // dil][None] if (t - j) % dil == 0 and 0 <= t - j <= nk * dil else ninf(1)
                    for j in range(n_tok)]
            tn.append(jnp.concatenate(keys + [ninf(LANES - n_tok)], axis=0))
        pad = lambda tabs: jnp.stack(tabs + [jnp.zeros_like(tabs[0])] * (TOK_PAD - n_tok))
        cached.append(pad(tc).transpose(2, 0, 1))
        new.append(pad(tn).transpose(2, 0, 1))
    return jnp.stack(cached), jnp.stack(new)


def _postmix_kernel(mem_ref, o1_ref, l1_ref, o2_ref, l2_ref, o3_ref, l3_ref, x_ref,
                    g1_ref, sc2_ref, sh2_ref, gpost_ref, gpre_ref, wo_ref, wr_ref, br_ref, cnt0_ref,
                    x1_ref, h2_ref, idx_ref, gate_ref, cnt_ref, cnt_s, slab_s, *, dilations):
    first_step = jnp.logical_and(pl.program_id(0) == 0, pl.program_id(1) == 0)

    @pl.when(first_step)
    def _():
        cnt_s[...] = cnt0_ref[...]

    n_slab = D_A // LANES
    tm = x_ref.shape[1]

    def rows(ref, dil, slot):
        if dil == 1:
            return ref[0]
        for r in range(dil):
            for c in range(n_slab):
                lo = r * D_A + c * LANES
                slab_s[slot, c, pl.ds(r, tm // dil, stride=dil), :] = ref[0, :, lo:lo + LANES]
        return jnp.concatenate([slab_s[slot, c] for c in range(n_slab)], axis=1)

    o1, o2, o3 = (rows(r, d, n) for n, (r, d) in enumerate(zip((o1_ref, o2_ref, o3_ref), dilations)))
    l1, l2, l3 = (rows(r, d, 3 + n) for n, (r, d) in enumerate(zip((l1_ref, l2_ref, l3_ref), dilations)))
    lmax = jnp.maximum(jnp.maximum(l1, l2), l3)
    w1 = jnp.exp(l1 - lmax)
    w2 = jnp.exp(l2 - lmax)
    w3 = jnp.exp(l3 - lmax)
    att = (w1 * o1 + w2 * o2 + w3 * o3) / (w1 + w2 + w3)
    y = jnp.dot(mem_ref[0].astype(BF16), wo_ref[0:D_M, :], preferred_element_type=F32)
    y = y + jnp.dot(att.astype(BF16), wo_ref[D_M:, :], preferred_element_type=F32)
    x1 = x_ref[0] + g1_ref[0] * _rms(y, gpost_ref[...])
    x1_ref[0] = x1
    h2 = _rms(x1, gpre_ref[...]) * (1.0 + sc2_ref[0]) + sh2_ref[0]
    h2_ref[0] = h2
    h_hi = h2.astype(BF16)
    h_lo = (h2 - h_hi.astype(F32)).astype(BF16)
    w_hi = wr_ref[0]
    logits = jnp.dot(h_hi, w_hi, preferred_element_type=F32)
    logits = logits + jnp.dot(h_lo, w_hi, preferred_element_type=F32)
    logits = logits + jnp.dot(h_hi, wr_ref[1], preferred_element_type=F32) + br_ref[...]
    lane = lax.broadcasted_iota(jnp.int32, logits.shape, 1)
    lane_f = lane.astype(F32)
    logits = jnp.where(lane < N_EXPERTS, logits, NEG_INF)
    vals, idxs = [], []
    for _ in range(TOP_K):
        mx = jnp.max(logits, axis=-1, keepdims=True)
        ix = jnp.min(jnp.where(logits == mx, lane_f, float(LANES)), axis=-1, keepdims=True)
        vals.append(mx)
        idxs.append(ix)
        logits = jnp.where(lane_f == ix, NEG_INF, logits)
    es = [jnp.exp(v - vals[0]) for v in vals]
    tot = es[0] + es[1] + es[2] + es[3]
    tm = logits.shape[0]
    picks = [(lane_f == ix).astype(F32) for ix in idxs]
    chosen = picks[0] + picks[1] + picks[2] + picks[3]
    r_i = lax.broadcasted_iota(jnp.int32, (tm, tm), 0)
    c_i = lax.broadcasted_iota(jnp.int32, (tm, tm), 1)
    earlier = jnp.dot((c_i < r_i).astype(BF16), chosen.astype(BF16), preferred_element_type=F32)
    before = earlier + cnt_s[...]
    cnt_s[...] = cnt_s[...] + jnp.sum(chosen, axis=0, keepdims=True)
    cnt_ref[...] = cnt_s[...]
    idx_tile = jnp.zeros(logits.shape, F32)
    gate_tile = jnp.zeros(logits.shape, F32)
    for k in range(TOP_K):
        rank = jnp.sum(picks[k] * before, axis=-1, keepdims=True)
        idx_tile = jnp.where(lane == k, idxs[k], idx_tile)
        idx_tile = jnp.where(lane == TOP_K + k, rank, idx_tile)
        gate_tile = jnp.where(lane == k, es[k] / tot, gate_tile)
    idx_ref[0] = idx_tile.astype(jnp.int32)
    gate_ref[0] = gate_tile


def _postmix(mem, branches, x3, gate1, scale2, shift2, g_post, g_pre, w_out_b, w_router_p, b_router_p, counts0, tm,
             dilations=(1, 1, 1)):
    nb, rows, _ = x3.shape
    per_row_mod = gate1.shape[1] != 1
    mod_block = (1, tm, D_MODEL) if per_row_mod else (1, 1, D_MODEL)
    mod_map = (lambda b, i: (b, i, 0)) if per_row_mod else (lambda b, i: (b, 0, 0))
    row_map = lambda b, i: (b, i, 0)
    const2 = lambda b, i: (0, 0)
    half = pl.BlockSpec((1, tm, D_M), row_map)
    flat = [a for pair in branches for a in pair]
    branch_specs = [pl.BlockSpec((1, tm // d, d * D_A), row_map) for d in dilations for _ in range(2)]
    return pl.pallas_call(
        functools.partial(_postmix_kernel, dilations=tuple(dilations)),
        grid=(nb, rows // tm),
        in_specs=[half] + branch_specs + [
            pl.BlockSpec((1, tm, D_MODEL), row_map),
            pl.BlockSpec(mod_block, mod_map),
            pl.BlockSpec(mod_block, mod_map),
            pl.BlockSpec(mod_block, mod_map),
            pl.BlockSpec((1, D_MODEL), const2),
            pl.BlockSpec((1, D_MODEL), const2),
            pl.BlockSpec((D_MODEL, D_MODEL), const2),
            pl.BlockSpec((2, D_MODEL, LANES), lambda b, i: (0, 0, 0)),
            pl.BlockSpec((1, LANES), const2),
            pl.BlockSpec((1, LANES), const2),
        ],
        out_specs=[
            pl.BlockSpec((1, tm, D_MODEL), row_map),
            pl.BlockSpec((1, tm, D_MODEL), row_map),
            pl.BlockSpec((1, tm, LANES), row_map),
            pl.BlockSpec((1, tm, LANES), row_map),
            pl.BlockSpec((1, LANES), const2),
        ],
        out_shape=[
            jax.ShapeDtypeStruct((nb, rows, D_MODEL), F32),
            jax.ShapeDtypeStruct((nb, rows, D_MODEL), F32),
            jax.ShapeDtypeStruct((nb, rows, LANES), jnp.int32),
            jax.ShapeDtypeStruct((nb, rows, LANES), F32),
            jax.ShapeDtypeStruct((1, LANES), F32),
        ],
        scratch_shapes=[pltpu.VMEM((1, LANES), F32), pltpu.VMEM((6, D_A // LANES, tm, LANES), F32)],
        compiler_params=_params(("arbitrary", "arbitrary"), fuse_inputs=17),
        name="postmix",
    )(mem, *flat, x3, gate1, scale2, shift2, g_post.reshape(1, D_MODEL), g_pre.reshape(1, D_MODEL),
      w_out_b, w_router_p, b_router_p, counts0)


def _row_wait(src_rows, dst_rows, sem):
    pltpu.make_async_copy(src_rows, dst_rows, sem).wait()


def _scatter_kernel(slot_ref, hp_ref, hs_ref, xs_hbm, buf, sem, *, n_prompt_tiles):
    h = jnp.where(pl.program_id(0) >= n_prompt_tiles, hs_ref[...], hp_ref[...])
    bits = pltpu.bitcast(h.astype(BF16).astype(F32), U32)
    packed = (bits[:, :PACK_W] & U32(HI_MASK)) | (bits[:, PACK_W:] >> 16)
    buf[...] = pltpu.einshape("m(jc)->mjc", packed, j=PACK_T)

    def issue(r, carry):
        src = buf.at[pl.ds(r, 1)]
        for k in range(TOP_K):
            pltpu.make_async_copy(src, xs_hbm.at[pl.ds(slot_ref[0, 0, r * TOP_K + k], 1)],
                                  sem).start(priority=k % 2)
        return carry

    lax.fori_loop(0, TOK_TILE, issue, 0, unroll=4)
    for _ in range(TOP_K):
        _row_wait(buf, xs_hbm.at[pl.ds(0, TOK_TILE)], sem)


def _moe_scatter(h_p, h_s, slot_tiles):
    tiles = slot_tiles.shape[0]
    n_p = h_p.shape[0] // TOK_TILE
    assert tiles == n_p + 1 and h_s.shape[0] == TOK_TILE
    n_rows = tiles * TOK_TILE * TOP_K
    return pl.pallas_call(
        functools.partial(_scatter_kernel, n_prompt_tiles=n_p),
        grid=(tiles,),
        in_specs=[
            pl.BlockSpec((1, 1, TOK_TILE * TOP_K), lambda i: (i, 0, 0), memory_space=pltpu.SMEM),
            pl.BlockSpec((TOK_TILE, D_MODEL), lambda i: (jnp.minimum(i, n_p - 1), 0)),
            pl.BlockSpec((TOK_TILE, D_MODEL), lambda i: (0, 0)),
        ],
        out_specs=pl.BlockSpec(memory_space=pl.ANY),
        out_shape=jax.ShapeDtypeStruct((n_rows, PACK_T, LANES), U32),
        scratch_shapes=[pltpu.VMEM((TOK_TILE, PACK_T, LANES), U32), pltpu.SemaphoreType.DMA],
        compiler_params=_params(("arbitrary",)),
        name="moe_scatter",
    )(slot_tiles, h_p, h_s)


def _expert_kernel(tile_ref, exp_ref, next_ref, lo_ref, hi_ref, first_ref, nvis_ref,
                   xs_ref, w1_hbm, b1_ref, w2_hbm, b2_ref, ys_ref, w1f, w2f, w1b, w2b, sem):
    v = pl.program_id(0)
    active = v < nvis_ref[0]
    changed = jnp.logical_or(v == 0, exp_ref[v] != exp_ref[jnp.maximum(v - 1, 0)])
    lo = lo_ref[v]
    hi = hi_ref[v]

    def weight_copies(e):
        return (pltpu.make_async_copy(w1_hbm.at[e], w1f, sem.at[0]),
                pltpu.make_async_copy(w2_hbm.at[e], w2f, sem.at[1]))

    @pl.when(v == 0)
    def _():
        for c in weight_copies(exp_ref[0]):
            c.start()

    @pl.when(jnp.logical_and(active, changed))
    def _():
        for c in weight_copies(exp_ref[v]):
            c.wait()
        w1b[...] = w1f[...].astype(BF16)
        w2b[...] = w2f[...].astype(BF16)

        @pl.when(next_ref[v] >= 0)
        def _():
            for c in weight_copies(next_ref[v]):
                c.start()

    def mlp_rows(r0):
        w = pltpu.einshape("mjc->m(jc)", xs_ref[r0:r0 + MOE_SUB])
        x_hi = pltpu.bitcast(w & U32(HI_MASK), F32).astype(BF16)
        x_lo = pltpu.bitcast(w << 16, F32).astype(BF16)
        hgu = jnp.dot(x_hi, w1b[0:PACK_W, :], preferred_element_type=F32)
        hgu = hgu + jnp.dot(x_lo, w1b[PACK_W:, :], preferred_element_type=F32) + b1_ref[0]
        x_glu = jnp.minimum(hgu[:, :D_FF], SWIGLU_LIMIT)
        x_lin = jnp.clip(hgu[:, D_FF:], -SWIGLU_LIMIT, SWIGLU_LIMIT)
        act = x_glu * jax.nn.sigmoid(SWIGLU_ALPHA * x_glu) * (x_lin + 1.0)
        y = jnp.dot(act.astype(BF16), w2b[...], preferred_element_type=F32) + b2_ref[0]
        return pltpu.einshape("m(jc)->mjc", y, j=ROW_T)

    subs = [r * MOE_SUB for r in range(MOE_M // MOE_SUB)]
    whole = jnp.logical_and(lo == 0, hi == MOE_M)

    @pl.when(jnp.logical_and(active, whole))
    def _():
        for r0 in subs:
            ys_ref[r0:r0 + MOE_SUB] = mlp_rows(r0)

    shared = jnp.logical_and(active, jnp.logical_not(whole))

    @pl.when(jnp.logical_and(shared, first_ref[v] == 1))
    def _():
        ys_ref[...] = jnp.zeros(ys_ref.shape, F32)

    for r0 in subs:
        @pl.when(jnp.logical_and(shared, jnp.logical_and(lo < r0 + MOE_SUB, hi > r0)))
        def _():
            rows = r0 + lax.broadcasted_iota(jnp.int32, (MOE_SUB, 1, 1), 0)
            mine = jnp.logical_and(rows >= lo, rows < hi)
            ys_ref[r0:r0 + MOE_SUB] = jnp.where(mine, mlp_rows(r0), ys_ref[r0:r0 + MOE_SUB])


def _moe_experts(xs, visits, w1, b1, w2, b2):
    n_rows = xs.shape[0]
    n_vis = visits[0].shape[0]
    tile_map = lambda v, tile, *_: (tile[v], 0, 0)
    exp_map = lambda v, tile, exp, *_: (exp[v], 0, 0)
    return pl.pallas_call(
        _expert_kernel,
        grid_spec=pltpu.PrefetchScalarGridSpec(
            num_scalar_prefetch=7,
            grid=(n_vis,),
            in_specs=[
                pl.BlockSpec((MOE_M, PACK_T, LANES), tile_map),
                pl.BlockSpec(memory_space=pl.ANY),
                pl.BlockSpec((1, 1, 2 * D_FF), exp_map),
                pl.BlockSpec(memory_space=pl.ANY),
                pl.BlockSpec((1, 1, D_MODEL), exp_map),
            ],
            out_specs=pl.BlockSpec((MOE_M, ROW_T, LANES), tile_map),
            scratch_shapes=[pltpu.VMEM((D_MODEL, 2 * D_FF), F32), pltpu.VMEM((D_FF, D_MODEL), F32),
                            pltpu.VMEM((D_MODEL, 2 * D_FF), BF16), pltpu.VMEM((D_FF, D_MODEL), BF16),
                            pltpu.SemaphoreType.DMA((2,))],
        ),
        out_shape=jax.ShapeDtypeStruct((n_rows, ROW_T, LANES), F32),
        compiler_params=_params(("arbitrary",)),
        name="moe_experts",
    )(*visits, xs, w1, b1.reshape(N_EXPERTS, 1, 2 * D_FF), w2, b2.reshape(N_EXPERTS, 1, D_MODEL))


def _combine_kernel(slot_ref, next_ref, gatep_ref, gates_ref, x1p_ref, x1s_ref, g2p_ref, g2s_ref, gpost_ref, ys_hbm,
                    yp_ref, ysm_ref, buf, sem, *, n_prompt_tiles):
    i = pl.program_id(0)
    last = pl.num_programs(0) - 1
    cur = lax.rem(i, 2)

    def gather(slots, b):
        def issue(r, carry):
            for k in range(TOP_K):
                pltpu.make_async_copy(ys_hbm.at[pl.ds(slots[0, 0, r * TOP_K + k], 1)],
                                      buf.at[b, pl.ds(k * TOK_TILE + r, 1)], sem.at[b]).start(priority=k % 2)
            return carry

        lax.fori_loop(0, TOK_TILE, issue, 0, unroll=4)

    @pl.when(i == 0)
    def _():
        gather(slot_ref, cur)

    @pl.when(i < last)
    def _():
        gather(next_ref, 1 - cur)

    for k in range(TOP_K):
        _row_wait(ys_hbm.at[pl.ds(0, TOK_TILE)], buf.at[cur, pl.ds(k * TOK_TILE, TOK_TILE)], sem.at[cur])
    is_sample = i >= n_prompt_tiles
    g = jnp.where(is_sample, gates_ref[...], gatep_ref[...])
    f = jnp.zeros((TOK_TILE, D_MODEL), F32)
    for k in range(TOP_K):
        rows = pltpu.einshape("mjc->m(jc)", buf[cur, pl.ds(k * TOK_TILE, TOK_TILE)])
        f = f + rows * g[:, k:k + 1]
    x1 = jnp.where(is_sample, x1s_ref[...], x1p_ref[...])
    g2 = jnp.where(is_sample, g2s_ref[...], g2p_ref[0])
    y = x1 + g2 * _rms(f, gpost_ref[...])

    @pl.when(jnp.logical_not(is_sample))
    def _():
        yp_ref[...] = y

    @pl.when(is_sample)
    def _():
        ysm_ref[...] = y


def _moe_combine(ys, slot_tiles, gates_p, gates_s, x1_p, x1_s, gate2_p, gate2_s, g_post, rows_per_batch):
    tiles = slot_tiles.shape[0]
    n_p = x1_p.shape[0] // TOK_TILE
    assert tiles == n_p + 1 and x1_s.shape[0] == TOK_TILE
    tiles_per_batch = rows_per_batch // TOK_TILE
    p_tile = lambda i: (jnp.minimum(i, n_p - 1), 0)
    const2 = lambda i: (0, 0)
    return pl.pallas_call(
        functools.partial(_combine_kernel, n_prompt_tiles=n_p),
        grid=(tiles,),
        in_specs=[
            pl.BlockSpec((1, 1, TOK_TILE * TOP_K), lambda i: (i, 0, 0), memory_space=pltpu.SMEM),
            pl.BlockSpec((1, 1, TOK_TILE * TOP_K), lambda i: (jnp.minimum(i + 1, tiles - 1), 0, 0),
                         memory_space=pltpu.SMEM),
            pl.BlockSpec((TOK_TILE, LANES), p_tile),
            pl.BlockSpec((TOK_TILE, LANES), const2),
            pl.BlockSpec((TOK_TILE, D_MODEL), p_tile),
            pl.BlockSpec((TOK_TILE, D_MODEL), const2),
            pl.BlockSpec((1, 1, D_MODEL), lambda i: (jnp.minimum(i, n_p - 1) // tiles_per_batch, 0, 0)),
            pl.BlockSpec((TOK_TILE, D_MODEL), const2),
            pl.BlockSpec((1, D_MODEL), const2),
            pl.BlockSpec(memory_space=pl.ANY),
        ],
        out_specs=[pl.BlockSpec((TOK_TILE, D_MODEL), p_tile), pl.BlockSpec((TOK_TILE, D_MODEL), const2)],
        out_shape=[jax.ShapeDtypeStruct(x1_p.shape, F32), jax.ShapeDtypeStruct(x1_s.shape, F32)],
        scratch_shapes=[pltpu.VMEM((2, TOP_K * TOK_TILE, ROW_T, LANES), F32), pltpu.SemaphoreType.DMA((2,))],
        compiler_params=_params(("arbitrary",)),
        name="moe_combine",
    )(slot_tiles, slot_tiles, gates_p, gates_s, x1_p, x1_s, gate2_p, gate2_s, g_post.reshape(1, D_MODEL), ys)


def _moe_plan(routes, counts):
    a = sum(r.shape[0] for r in routes) * TOP_K
    assert a % MOE_M == 0
    g_end = jnp.cumsum(counts)
    g_start = g_end - counts

    def slots(route):
        flat_e = route[:, :TOP_K].reshape(-1)
        rank = route[:, TOP_K:2 * TOP_K].reshape(-1)
        onehot = flat_e[:, None] == jnp.arange(N_EXPERTS)[None, :]
        return (jnp.sum(jnp.where(onehot, g_start[None, :], 0), axis=1) + rank).astype(jnp.int32)

    slot_of = jnp.concatenate([slots(r) for r in routes])

    n_tiles = a // MOE_M
    n_vis = n_tiles + N_EXPERTS - 1
    first_tile = g_start // MOE_M
    last_tile = jnp.where(counts > 0, (g_end - 1) // MOE_M, first_tile)
    nvis_e = jnp.where(counts > 0, last_tile - first_tile + 1, 0)
    v_end = jnp.cumsum(nvis_e)
    v_start = v_end - nvis_e
    total = v_end[-1]
    v = jnp.arange(n_vis)
    vc = jnp.minimum(v, total - 1)
    e_of = jnp.minimum(jnp.sum((vc[:, None] >= v_end[None, :]).astype(jnp.int32), axis=1), N_EXPERTS - 1)
    is_e = e_of[:, None] == jnp.arange(N_EXPERTS)[None, :]
    of_e = lambda t: jnp.sum(jnp.where(is_e, t[None, :], 0), axis=1)
    tile_of = of_e(first_tile) + (vc - of_e(v_start))
    lo = jnp.maximum(of_e(g_start), tile_of * MOE_M) - tile_of * MOE_M
    hi = jnp.minimum(of_e(g_end), (tile_of + 1) * MOE_M) - tile_of * MOE_M
    prev_tile = jnp.concatenate([jnp.full((1,), -1, tile_of.dtype), tile_of[:-1]])
    first = (tile_of != prev_tile).astype(jnp.int32)
    experts = jnp.arange(N_EXPERTS)
    later = (experts[None, :] > experts[:, None]) & (counts[None, :] > 0)
    next_e = jnp.min(jnp.where(later, experts[None, :], N_EXPERTS), axis=1)
    next_e = jnp.where(next_e < N_EXPERTS, next_e, -1)
    i32 = lambda t: t.astype(jnp.int32)
    visits = (i32(tile_of), i32(e_of), i32(of_e(next_e)), i32(lo), i32(hi), first, i32(total).reshape(1))
    return slot_of, visits


def _reorder_w_in(w_in):
    off_om_end = 4 * D_M
    off_g_end = off_om_end + 2 * H_M
    w = jnp.concatenate([
        w_in[:, :off_om_end],
        w_in[:, off_g_end:],
        w_in[:, off_om_end:off_g_end],
        jnp.zeros((D_MODEL, LANES - 2 * H_M), w_in.dtype),
    ], axis=1)
    return w.astype(BF16)


def kernel(x_prompt, x_sample, cache_k, cache_v, state_conv, state_C, state_n, state_m, c_prompt, c_sample, w_ada, b_ada, g_pre_mix, g_post_mix, g_pre_ffn, g_post_ffn, w_in, b_ig, b_fg, conv_w, conv_b, mh_norm, rel_bias, w_out, w_router, b_router, w1, b1, w2, b2):
    depth = w_in.shape[0]
    assert depth == 1
    l = 0
    nb_p, seq, _ = x_prompt.shape
    nb_s, n_tok, _ = x_sample.shape
    rows_s = nb_s * n_tok
    assert rows_s == TOK_TILE

    n_c = nb_p + nb_s
    c_pad = -(-n_c // 8) * 8
    c_all = jnp.concatenate([c_prompt, c_sample, jnp.zeros((c_pad - n_c, D_MODEL), F32)], axis=0)
    mod = _adaln(c_all, w_ada[l], b_ada[l])
    mods = [mod[:, i * D_MODEL:(i + 1) * D_MODEL] for i in range(6)]
    mods_p = [m[:nb_p].reshape(nb_p, 1, D_MODEL) for m in mods]
    mods_s = [jnp.repeat(m[nb_p:n_c], n_tok, axis=0).reshape(1, rows_s, D_MODEL) for m in mods]

    w_r = _reorder_w_in(w_in[l])
    bias_g = jnp.concatenate([b_ig[l], b_fg[l], jnp.zeros((LANES - 2 * H_M,), F32)]).reshape(1, LANES)
    w_out_b = w_out[l].astype(BF16)
    w_router_f = jnp.concatenate([w_router[l], jnp.zeros((D_MODEL, LANES - N_EXPERTS), F32)], axis=1)
    w_router_hi = w_router_f.astype(BF16)
    w_router_p = jnp.stack([w_router_hi, (w_router_f - w_router_hi.astype(F32)).astype(BF16)])
    b_router_p = jnp.concatenate([b_router[l].astype(F32), jnp.zeros((LANES - N_EXPERTS,), F32)]).reshape(1, LANES)

    shift1, scale1, gate1, shift2, scale2, gate2_p = mods_p
    dils = tuple(d for _, d in BRANCHES)
    assert dils[0] == 1
    u, v_m, og, gts, qa, ka, va, *dilated = _inproj(x_prompt, g_pre_mix[l], scale1, shift1, w_r, bias_g, tm=512,
                                                   dilations=dils[1:])
    qkv_d = [(qa, ka, va)] + [tuple(dilated[3 * n:3 * n + 3]) for n in range(len(dils) - 1)]
    gates_row = gts[:, :, :8].transpose(0, 2, 1)
    zeros = lambda *s: jnp.zeros(s, F32)
    mem, c_p, n_p, m_p = _mlstm(u, v_m, og, gts, gates_row, zeros(nb_p, 8, 2 * D_M),
                                zeros(nb_p, H_M, DK_M, DK_M), zeros(nb_p, H_M, DK_M), zeros(nb_p, H_M, LANES),
                                conv_w[l], conv_b[l], mh_norm[l], MLSTM_L_PROMPT)
    branches = [_attn_branch(*qkv, _prompt_bias(rel_bias, w, d), d) for qkv, (w, d) in zip(qkv_d, BRANCHES)]
    x1_p, h2_p, idx_p, gate_p, cnt_p = _postmix(mem, branches, x_prompt, gate1, scale2, shift2,
                                                g_post_mix[l], g_pre_ffn[l], w_out_b, w_router_p, b_router_p,
                                                zeros(1, LANES), tm=512, dilations=dils)
    keep = min(WINDOW_MAX, seq)
    k_p, v_p = _window_heads(ka, va, keep)
    conv_p = u[:, seq - (CONV_W - 1):]

    shift1, scale1, gate1, shift2, scale2, gate2_s = mods_s
    x_s = x_sample.reshape(1, rows_s, D_MODEL)
    u_s, v_s, og_s, gts_s, qa_s, ka_s, va_s = _inproj(x_s, g_pre_mix[l], scale1, shift1, w_r, bias_g, tm=rows_s)
    per_b = lambda t: t.reshape(nb_s, n_tok, t.shape[-1])
    ls = MLSTM_L_SAMPLE
    pad_rows = lambda t, n: jnp.concatenate([t, jnp.zeros((nb_s, n - t.shape[1], t.shape[2]), t.dtype)], axis=1)
    inert = jnp.concatenate([jnp.full((H_M,), NEG_INF, F32), jnp.zeros((LANES - H_M,), F32)])
    gts_pad = jnp.concatenate([per_b(gts_s), jnp.broadcast_to(inert, (nb_s, ls - n_tok, LANES))], axis=1)
    tail0 = jnp.concatenate([zeros(nb_s, 8 - (CONV_W - 1), 2 * D_M), state_conv[l].astype(F32)], axis=1)
    m0x = jnp.broadcast_to(state_m[l].astype(F32)[:, :, None], (nb_s, H_M, LANES))
    mem_s, c_s, n_s, m_s = _mlstm(pad_rows(per_b(u_s), ls), pad_rows(per_b(v_s), ls), pad_rows(per_b(og_s), ls),
                                  gts_pad, gts_pad[:, :, :8].transpose(0, 2, 1), tail0,
                                  state_C[l].astype(F32), state_n[l].astype(F32), m0x,
                                  conv_w[l], conv_b[l], mh_norm[l], ls)
    mem_s = mem_s[:, :n_tok].reshape(1, rows_s, D_M)
    heads = lambda t: t.reshape(nb_s, n_tok, H_A, HD_A)
    k_new, v_new = heads(ka_s), heads(va_s)
    n_past = cache_k.shape[2]
    assert n_tok <= TOK_PAD
    q_h = jnp.pad(heads(qa_s).transpose(0, 2, 1, 3), ((0, 0), (0, 0), (0, TOK_PAD - n_tok), (0, 0)))
    new_t = lambda t: jnp.pad(t.transpose(0, 2, 3, 1), ((0, 0), (0, 0), (0, 0), (0, LANES - n_tok)))
    cache_t = lambda c: c[l].astype(F32).transpose(0, 2, 3, 1)
    att_s = _sample_attention(q_h, cache_t(cache_k), cache_t(cache_v), new_t(k_new), new_t(v_new),
                              *_sample_bias(rel_bias, n_tok, n_past))
    att_s = att_s[:, :, :n_tok].transpose(0, 2, 1, 3).reshape(1, rows_s, D_A)
    zero_o = zeros(1, rows_s, D_A)
    ninf = jnp.full((1, rows_s, D_A), NEG_INF, F32)
    branches_s = [(att_s, zero_o), (zero_o, ninf), (zero_o, ninf)]
    x1_s, h2_s, idx_s, gate_s, cnt_all = _postmix(mem_s, branches_s, x_s, gate1, scale2, shift2,
                                                  g_post_mix[l], g_pre_ffn[l], w_out_b, w_router_p, b_router_p,
                                                  cnt_p, tm=rows_s)

    t_p = nb_p * seq
    slot_of, visits = _moe_plan([idx_p.reshape(t_p, LANES), idx_s.reshape(rows_s, LANES)],
                                cnt_all[0, :N_EXPERTS].astype(jnp.int32))
    slot_tiles = slot_of.reshape((t_p + rows_s) // TOK_TILE, 1, TOK_TILE * TOP_K)
    xs = _moe_scatter(h2_p.reshape(t_p, D_MODEL), h2_s.reshape(rows_s, D_MODEL), slot_tiles)
    ys = _moe_experts(xs, visits, w1[l], b1[l], w2[l], b2[l])
    y_p, y_s = _moe_combine(ys, slot_tiles, gate_p.reshape(t_p, LANES), gate_s.reshape(rows_s, LANES), x1_p.reshape(t_p, D_MODEL), x1_s.reshape(rows_s, D_MODEL),
                            gate2_p, gate2_s.reshape(rows_s, D_MODEL), g_post_ffn[l], seq)
    y_prompt = y_p.reshape(nb_p, seq, D_MODEL)
    y_sample = y_s.reshape(nb_s, n_tok, D_MODEL)

    st = lambda t: t[None]
    conv_s = jnp.concatenate([state_conv[l].astype(F32), per_b(u_s)], axis=1)[:, -(CONV_W - 1):]
    return (y_prompt, y_sample, st(k_p), st(v_p), st(conv_p), st(c_p), st(n_p), st(m_p[:, :, 0]),
            st(k_new), st(v_new), st(conv_s), st(c_s), st(n_s), st(m_s[:, :, 0]))
```

```python
import functools
import math

import jax
import jax.numpy as jnp
import numpy as np
from jax import lax
from jax.experimental import pallas as pl
from jax.experimental.pallas import tpu as pltpu

F32 = jnp.float32
BF16 = jnp.bfloat16
U32 = jnp.uint32
NEG_INF = float("-inf")

D_MODEL = 1024
H_M = 4
D_M = 512
DK_M = 128
CONV_W = 4
H_A = 8
D_A = 512
HD_A = 64
BRANCHES = ((128, 1), (512, 4), (2048, 16))
WINDOW_MAX = 2048
QB = 128
N_BUCKETS = 32
MAX_DISTANCE = 2048
N_EXPERTS = 32
TOP_K = 4
D_FF = 1024
SWIGLU_LIMIT = 7.0
SWIGLU_ALPHA = 1.702
EPS = 1e-6

LANES = 128
C_U, C_V, C_O, C_QA, C_KA, C_VA, C_G = 0, 1024, 1536, 2048, 2560, 3072, 3584
D_IN_R = C_G + LANES

MLSTM_L_PROMPT = 256
MLSTM_L_SAMPLE = 32
ATT_QL = 1024
MOE_M = 512
MOE_SUB = 256
TOK_TILE = 128
PACK_W = D_MODEL // 2
PACK_T = PACK_W // LANES
ROW_T = D_MODEL // LANES
HI_MASK = 0xFFFF0000
VMEM_LIMIT = 56 * 1024 * 1024


def _params(sem, vmem=None, fuse_inputs=0):
    return pltpu.CompilerParams(dimension_semantics=sem, vmem_limit_bytes=vmem or VMEM_LIMIT,
                                allow_input_fusion=[True] * fuse_inputs if fuse_inputs else None)


def _rms(x, g):
    return x * lax.rsqrt(jnp.mean(x * x, axis=-1, keepdims=True) + EPS) * g


def _log_sigmoid(t):
    return jnp.minimum(t, 0.0) - jnp.log1p(jnp.exp(-jnp.abs(t)))


def _ada_kernel(c_ref, w_ref, b_ref, o_ref):
    c = c_ref[...]
    a = (c * jax.nn.sigmoid(c)).astype(BF16)
    o_ref[...] = jnp.dot(a, w_ref[...].astype(BF16), preferred_element_type=F32) + b_ref[...]


def _adaln(c_all, w_ada, b_ada):
    rows = c_all.shape[0]
    n = w_ada.shape[1]
    tn = 1024
    return pl.pallas_call(
        _ada_kernel,
        grid=(n // tn,),
        in_specs=[
            pl.BlockSpec((rows, D_MODEL), lambda j: (0, 0)),
            pl.BlockSpec((D_MODEL, tn), lambda j: (0, j)),
            pl.BlockSpec((1, tn), lambda j: (0, j)),
        ],
        out_specs=pl.BlockSpec((rows, tn), lambda j: (0, j)),
        out_shape=jax.ShapeDtypeStruct((rows, n), F32),
        compiler_params=_params(("arbitrary",), vmem=16 * 1024 * 1024, fuse_inputs=3),
        name="adaln",
    )(c_all, w_ada, b_ada.reshape(1, n))


def _inproj_kernel(x_ref, g_ref, sc_ref, sh_ref, w_ref, bg_ref,
                   u_ref, v_ref, o_ref, gt_ref, qa_ref, ka_ref, va_ref, *rest, dilations):
    dil_refs, slab_s = (rest[:-1], rest[-1]) if dilations else (rest, None)
    x = x_ref[0]
    h = _rms(x, g_ref[...]) * (1.0 + sc_ref[0]) + sh_ref[0]
    hb = h.astype(BF16)

    def seg(lo, width):
        return jnp.dot(hb, w_ref[:, lo:lo + width], preferred_element_type=F32)

    u_ref[0] = seg(C_U, 2 * D_M)
    v_ref[0] = seg(C_V, D_M)
    o_ref[0] = jax.nn.sigmoid(seg(C_O, D_M))
    qkv = (seg(C_QA, D_A) * (HD_A ** -0.5), seg(C_KA, D_A), seg(C_VA, D_A))
    for ref, val in zip((qa_ref, ka_ref, va_ref), qkv):
        ref[0] = val
    n_slab = D_A // LANES
    rows = x.shape[0]
    for j, val in enumerate(qkv if dilations else ()):
        for c in range(n_slab):
            slab_s[j, c] = val[:, c * LANES:(c + 1) * LANES]
        for n, dil in enumerate(dilations):
            for r in range(dil):
                for c in range(n_slab):
                    lo = r * D_A + c * LANES
                    dil_refs[3 * n + j][0, :, lo:lo + LANES] = slab_s[j, c, pl.ds(r, rows // dil, stride=dil), :]
    t = seg(C_G, LANES) + bg_ref[...]
    lane = lax.broadcasted_iota(jnp.int32, t.shape, 1)
    gt_ref[0] = jnp.where(lane < H_M, t, _log_sigmoid(t))


def _inproj(x3, g_pre, scale, shift, w_r, bias_g, tm, dilations=()):
    nb, rows, _ = x3.shape
    per_row_mod = scale.shape[1] != 1
    mod_block = (1, tm, D_MODEL) if per_row_mod else (1, 1, D_MODEL)
    mod_map = (lambda b, i: (b, i, 0)) if per_row_mod else (lambda b, i: (b, 0, 0))
    row_map = lambda b, i: (b, i, 0)
    widths = (2 * D_M, D_M, D_M, LANES, D_A, D_A, D_A)
    dil_specs = [pl.BlockSpec((1, tm // d, d * D_A), row_map) for d in dilations for _ in range(3)]
    dil_shapes = [jax.ShapeDtypeStruct((nb, rows // d, d * D_A), F32) for d in dilations for _ in range(3)]
    return pl.pallas_call(
        functools.partial(_inproj_kernel, dilations=tuple(dilations)),
        grid=(nb, rows // tm),
        in_specs=[
            pl.BlockSpec((1, tm, D_MODEL), row_map),
            pl.BlockSpec((1, D_MODEL), lambda b, i: (0, 0)),
            pl.BlockSpec(mod_block, mod_map),
            pl.BlockSpec(mod_block, mod_map),
            pl.BlockSpec((D_MODEL, D_IN_R), lambda b, i: (0, 0)),
            pl.BlockSpec((1, LANES), lambda b, i: (0, 0)),
        ],
        out_specs=[pl.BlockSpec((1, tm, w), row_map) for w in widths] + dil_specs,
        out_shape=[jax.ShapeDtypeStruct((nb, rows, w), F32) for w in widths] + dil_shapes,
        scratch_shapes=[pltpu.VMEM((3, D_A // LANES, tm, LANES), F32)] if dilations else [],
        compiler_params=_params(("arbitrary", "arbitrary"), fuse_inputs=6),
        name="inproj",
    )(x3, g_pre.reshape(1, D_MODEL), scale, shift, w_r, bias_g)


def _mlstm_kernel(u_ref, v_ref, og_ref, gc_ref, gr_ref, tail0_ref, c0_ref, n0_ref, m0_ref,
                  cw_ref, cb_ref, mh_ref,
                  mem_ref, c_out_ref, n_out_ref, m_out_ref,
                  c_s, n_s, m_s, tail_s, uext_s, *, chunk):
    L = chunk
    step = pl.program_id(1)

    @pl.when(step == 0)
    def _():
        c_s[...] = c0_ref[0]
        n_s[...] = n0_ref[0]
        m_s[...] = m0_ref[0]
        tail_s[...] = tail0_ref[0]

    u = u_ref[0]
    uext_s[0:8, :] = tail_s[...]
    uext_s[8:8 + L, :] = u
    acc = cb_ref[...] + uext_s[pl.ds(5, L), :] * cw_ref[0:1, :]
    acc = acc + uext_s[pl.ds(6, L), :] * cw_ref[1:2, :]
    acc = acc + uext_s[pl.ds(7, L), :] * cw_ref[2:3, :]
    acc = acc + u * cw_ref[3:4, :]
    tail_s[...] = uext_s[pl.ds(L, 8), :]
    qk = acc * jax.nn.sigmoid(acc)

    gc = gc_ref[0]
    gr = gr_ref[0]
    row = lax.broadcasted_iota(jnp.int32, (L, L), 0)
    col = lax.broadcasted_iota(jnp.int32, (L, L), 1)
    causal = col <= row
    tri = causal.astype(BF16)
    tri_t = (row <= col).astype(BF16)
    lane = lax.broadcasted_iota(jnp.int32, gc.shape, 1)
    srow = lax.broadcasted_iota(jnp.int32, gr.shape, 0)

    def bf16_parts(x):
        hi = x.astype(BF16)
        r1 = x - hi.astype(F32)
        mid = r1.astype(BF16)
        return hi, mid, (r1 - mid.astype(F32)).astype(BF16)

    b_col_all = sum(jnp.dot(tri, p, preferred_element_type=F32)
                    for p in bf16_parts(jnp.where(lane >= H_M, gc, 0.0)))
    b_row_all = sum(jnp.dot(p, tri_t, preferred_element_type=F32)
                    for p in bf16_parts(jnp.where(srow >= H_M, gr, 0.0)))

    v_all = v_ref[0]
    og = og_ref[0]
    for h in range(H_M):
        q = qk[:, h * DK_M:(h + 1) * DK_M]
        k = qk[:, D_M + h * DK_M:D_M + (h + 1) * DK_M] * (DK_M ** -0.5)
        v = v_all[:, h * DK_M:(h + 1) * DK_M]
        li_row = gr[h:h + 1, :]
        li_col = gc[:, h:h + 1]
        b_row = b_row_all[H_M + h:H_M + h + 1, :]
        b_col = b_col_all[:, H_M + h:H_M + h + 1]
        m0 = m_s[h:h + 1, 0:1]
        c0 = c_s[h]
        n0 = n_s[h:h + 1, :]

        d = jnp.where(causal, b_col - b_row + li_row, NEG_INF)
        g_state = b_col + m0
        m = jnp.maximum(jnp.max(d, axis=-1, keepdims=True), g_state)
        w_state = jnp.exp(g_state - m)
        qb = q.astype(BF16)
        kb = k.astype(BF16)
        vb = v.astype(BF16)
        s = lax.dot_general(qb, kb, (((1,), (1,)), ((), ())), preferred_element_type=F32)
        s = s * jnp.exp(d - m)
        num = jnp.dot(s.astype(BF16), vb, preferred_element_type=F32)
        num = num + w_state * jnp.dot(qb, c0.astype(BF16), preferred_element_type=F32)
        den = jnp.sum(s, axis=-1, keepdims=True) + w_state * jnp.sum(q * n0, axis=-1, keepdims=True)
        hh = num / jnp.maximum(jnp.abs(den), jnp.exp(-m))

        b_last = b_col[L - 1:L, :]
        g_tok_row = b_last - b_row + li_row
        m_new = jnp.maximum(b_last + m0, jnp.max(g_tok_row, axis=-1, keepdims=True))
        w_tok_col = jnp.exp(b_last - b_col + li_col - m_new)
        decay = jnp.exp(b_last + m0 - m_new)
        kw = k * w_tok_col
        c_s[h] = decay * c0 + jnp.dot(kw.T.astype(BF16), vb, preferred_element_type=F32)
        n_s[h:h + 1, :] = decay * n0 + jnp.sum(kw, axis=0, keepdims=True)
        m_s[h:h + 1, :] = jnp.broadcast_to(m_new, (1, LANES))

        hn = hh * lax.rsqrt(jnp.mean(hh * hh, axis=-1, keepdims=True) + EPS)
        hn = hn * mh_ref[:, h * DK_M:(h + 1) * DK_M]
        mem_ref[0, :, h * DK_M:(h + 1) * DK_M] = og[:, h * DK_M:(h + 1) * DK_M] * hn

    @pl.when(step == pl.num_programs(1) - 1)
    def _():
        c_out_ref[0] = c_s[...]
        n_out_ref[0] = n_s[...]
        m_out_ref[0] = m_s[...]


def _mlstm(u, v, og, gates_col, gates_row, tail0, c0, n0, m0x, conv_w, conv_b, mh_norm, chunk):
    nb, rows, _ = u.shape
    nc = rows // chunk
    row_map = lambda b, c: (b, c, 0)
    bat3 = lambda b, c: (b, 0, 0)
    const2 = lambda b, c: (0, 0)
    return pl.pallas_call(
        functools.partial(_mlstm_kernel, chunk=chunk),
        grid=(nb, nc),
        in_specs=[
            pl.BlockSpec((1, chunk, 2 * D_M), row_map),
            pl.BlockSpec((1, chunk, D_M), row_map),
            pl.BlockSpec((1, chunk, D_M), row_map),
            pl.BlockSpec((1, chunk, LANES), row_map),
            pl.BlockSpec((1, 8, chunk), lambda b, c: (b, 0, c)),
            pl.BlockSpec((1, 8, 2 * D_M), bat3),
            pl.BlockSpec((1, H_M, DK_M, DK_M), lambda b, c: (b, 0, 0, 0)),
            pl.BlockSpec((1, H_M, DK_M), bat3),
            pl.BlockSpec((1, H_M, LANES), bat3),
            pl.BlockSpec((CONV_W, 2 * D_M), const2),
            pl.BlockSpec((1, 2 * D_M), const2),
            pl.BlockSpec((1, D_M), const2),
        ],
        out_specs=[
            pl.BlockSpec((1, chunk, D_M), row_map),
            pl.BlockSpec((1, H_M, DK_M, DK_M), lambda b, c: (b, 0, 0, 0)),
            pl.BlockSpec((1, H_M, DK_M), bat3),
            pl.BlockSpec((1, H_M, LANES), bat3),
        ],
        out_shape=[
            jax.ShapeDtypeStruct((nb, rows, D_M), F32),
            jax.ShapeDtypeStruct((nb, H_M, DK_M, DK_M), F32),
            jax.ShapeDtypeStruct((nb, H_M, DK_M), F32),
            jax.ShapeDtypeStruct((nb, H_M, LANES), F32),
        ],
        scratch_shapes=[
            pltpu.VMEM((H_M, DK_M, DK_M), F32),
            pltpu.VMEM((H_M, DK_M), F32),
            pltpu.VMEM((H_M, LANES), F32),
            pltpu.VMEM((8, 2 * D_M), F32),
            pltpu.VMEM((chunk + 8, 2 * D_M), F32),
        ],
        compiler_params=_params(("arbitrary", "arbitrary"), vmem=16 * 1024 * 1024, fuse_inputs=12),
        name="mlstm",
    )(u, v, og, gates_col, gates_row, tail0, c0, n0, m0x,
      conv_w, conv_b.reshape(1, 2 * D_M), mh_norm.reshape(1, D_M))


def _attn_kernel(q_ref, kc_ref, vc_ref, kp_ref, vp_ref, bias_ref, ones_ref, o_ref, l_ref, k_s, v_s, *, ql):
    first = pl.program_id(2) == 0
    k_s[0:QB, :] = kp_ref[0].astype(BF16)
    k_s[QB:QB + ql, :] = kc_ref[0].astype(BF16)
    v_s[0:QB, :] = vp_ref[0].astype(BF16)
    v_s[QB:QB + ql, :] = vc_ref[0].astype(BF16)
    col = lax.broadcasted_iota(jnp.int32, (QB, 2 * QB), 1)
    pad_keys = jnp.logical_and(first, col < QB)
    head0 = lax.broadcasted_iota(jnp.int32, (1, LANES), 1) < HD_A
    n_slab = D_A // LANES
    n_res = q_ref.shape[2] // D_A
    pad_all = jnp.concatenate([pad_keys] * H_A, axis=0)

    def raw_logits(c, j):
        sl = slice(c * LANES, (c + 1) * LANES)
        q = q_ref[0, j * QB:(j + 1) * QB, sl].astype(BF16)
        zero = jnp.zeros_like(q)
        q2 = jnp.concatenate([jnp.where(head0, q, zero), jnp.where(head0, zero, q)], axis=0)
        return lax.dot_general(q2, k_s[j * QB:(j + 2) * QB, sl], (((1,), (1,)), ((), ())),
                               preferred_element_type=F32)

    def softmax(j, logits):
        logits = logits + bias_ref[...].reshape(H_A * QB, 2 * QB)
        if j == 0:
            logits = jnp.where(pad_all, NEG_INF, logits)
        m = jnp.max(logits, axis=-1, keepdims=True)
        return jnp.exp(logits - m).astype(BF16), m

    def finish(c, j, p, m):
        sl = slice(c * LANES, (c + 1) * LANES)
        rows = slice(j * QB, (j + 1) * QB)
        p_cat = jnp.concatenate([p[:QB], p[QB:]], axis=1)
        v_all = v_s[j * QB:(j + 2) * QB, sl]
        v_bd = jnp.concatenate([jnp.where(head0, v_all, jnp.zeros_like(v_all)),
                                jnp.where(head0, jnp.zeros_like(v_all), v_all)], axis=0)
        o = jnp.dot(p_cat, v_bd, preferred_element_type=F32)
        s = jnp.dot(p_cat, ones_ref[...], preferred_element_type=F32)
        o_ref[0, rows, sl] = o / s
        l_ref[0, rows, sl] = jnp.where(head0, m[:QB], m[QB:]) + jnp.log(s)

    for res in range(n_res):
        slabs = range(res * n_slab, (res + 1) * n_slab)
        for j in range(ql // QB):
            p, m = softmax(j, jnp.concatenate([raw_logits(c, j) for c in slabs], axis=0))
            for n, c in enumerate(slabs):
                part = slice(2 * n * QB, 2 * (n + 1) * QB)
                finish(c, j, p[part], m[part])


def _attn_branch(qa, ka, va, bias, dil):
    nb, ld, _ = qa.shape
    ql = min(ATT_QL, ld)
    n_res = min(dil, ATT_QL // ql)
    width = n_res * D_A
    cur = lambda b, r, i: (b, i, r)
    prev = lambda b, r, i: (b, jnp.maximum(i * (ql // QB) - 1, 0), r)
    ones_bd = np.zeros((4 * QB, LANES), np.float32)
    ones_bd[:2 * QB, :HD_A] = 1.0
    ones_bd[2 * QB:, HD_A:] = 1.0
    return pl.pallas_call(
        functools.partial(_attn_kernel, ql=ql),
        grid=(nb, dil // n_res, ld // ql),
        in_specs=[
            pl.BlockSpec((1, ql, width), cur),
            pl.BlockSpec((1, ql, width), cur),
            pl.BlockSpec((1, ql, width), cur),
            pl.BlockSpec((1, QB, width), prev),
            pl.BlockSpec((1, QB, width), prev),
            pl.BlockSpec((H_A, QB, 2 * QB), lambda b, r, i: (0, 0, 0)),
            pl.BlockSpec((4 * QB, LANES), lambda b, r, i: (0, 0)),
        ],
        out_specs=[pl.BlockSpec((1, ql, width), cur), pl.BlockSpec((1, ql, width), cur)],
        out_shape=[jax.ShapeDtypeStruct((nb, ld, dil * D_A), F32)] * 2,
        scratch_shapes=[pltpu.VMEM((QB + ql, width), BF16), pltpu.VMEM((QB + ql, width), BF16)],
        compiler_params=_params(("arbitrary", "arbitrary", "arbitrary"), vmem=40 * 1024 * 1024, fuse_inputs=7),
        name=f"attn_d{dil}",
    )(qa, ka, va, ka, va, bias, jnp.asarray(ones_bd, BF16))


def _heads_kernel(k_ref, v_ref, ko_ref, vo_ref):
    for src, dst in ((k_ref, ko_ref), (v_ref, vo_ref)):
        for c in range(D_A // LANES):
            xt = src[0, :, c * LANES:(c + 1) * LANES].T
            dst[0, 2 * c] = xt[:HD_A]
            dst[0, 2 * c + 1] = xt[HD_A:]


def _window_heads(ka, va, keep, tm=512):
    nb, seq, _ = ka.shape
    first = (seq - keep) // tm
    in_spec = pl.BlockSpec((1, tm, D_A), lambda b, i: (b, first + i, 0))
    out_spec = pl.BlockSpec((1, H_A, HD_A, tm), lambda b, i: (b, 0, 0, i))
    kt, vt = pl.pallas_call(
        _heads_kernel,
        grid=(nb, keep // tm),
        in_specs=[in_spec, in_spec],
        out_specs=[out_spec, out_spec],
        out_shape=[jax.ShapeDtypeStruct((nb, H_A, HD_A, keep), F32)] * 2,
        compiler_params=_params(("arbitrary", "arbitrary"), vmem=16 * 1024 * 1024, fuse_inputs=2),
        name="window_heads",
    )(ka, va)
    return kt.transpose(0, 3, 1, 2), vt.transpose(0, 3, 1, 2)


def _t5_bucket(dist):
    max_exact = N_BUCKETS // 2
    n = jnp.maximum(dist, 1).astype(F32)
    large = max_exact + (jnp.log(n / max_exact) / math.log(MAX_DISTANCE / max_exact)
                         * (N_BUCKETS - max_exact)).astype(jnp.int32)
    large = jnp.minimum(large, N_BUCKETS - 1)
    return jnp.where(dist < max_exact, dist, large)


def _branch_bias_vec(rel_bias, window, dil):
    return rel_bias[_t5_bucket(dil * jnp.arange(window // dil + 1))].astype(F32)


def _prompt_bias(rel_bias, window, dil):
    assert window // dil == QB
    n = 2 * QB
    vec = _branch_bias_vec(rel_bias, window, dil)
    vecp = jnp.concatenate([vec[::-1], jnp.full((n - QB - 1, H_A), NEG_INF, F32)], axis=0)
    g = jnp.roll(vecp, QB - 1, axis=0)
    skew = jnp.tile(g, (QB + 1, 1))[:QB * (n + 1)].reshape(QB, n + 1, H_A)[:, :n]
    return skew[::-1].transpose(2, 0, 1)


TOK_PAD = 8


def _sattn_kernel(q_ref, kt_ref, vt_ref, knt_ref, vnt_ref, bc_ref, bn_ref, att_ref):
    nt = (((1,), (1,)), ((), ()))
    for h in range(H_A):
        q = q_ref[0, h].astype(BF16)
        lc = jnp.dot(q, kt_ref[0, h].astype(BF16), preferred_element_type=F32)
        ln = jnp.dot(q, knt_ref[0, h].astype(BF16), preferred_element_type=F32)
        lcs = [lc + bc_ref[d, h] for d in range(len(BRANCHES))]
        lns = [ln + bn_ref[d, h] for d in range(len(BRANCHES))]
        m = functools.reduce(jnp.maximum, [jnp.max(x, axis=-1, keepdims=True) for x in lcs + lns])
        pc = functools.reduce(jnp.add, [jnp.exp(x - m) for x in lcs])
        pn = functools.reduce(jnp.add, [jnp.exp(x - m) for x in lns])
        s = jnp.sum(pc, axis=-1, keepdims=True) + jnp.sum(pn, axis=-1, keepdims=True)
        o = lax.dot_general(pc.astype(BF16), vt_ref[0, h].astype(BF16), nt, preferred_element_type=F32)
        o = o + lax.dot_general(pn.astype(BF16), vnt_ref[0, h].astype(BF16), nt, preferred_element_type=F32)
        att_ref[0, h] = o / s


def _sample_attention(q_h, kt, vt, knt, vnt, bias_c, bias_n):
    nb, _, _, n_past = kt.shape
    bat = lambda b: (b, 0, 0, 0)
    const = lambda b: (0, 0, 0, 0)
    return pl.pallas_call(
        _sattn_kernel,
        grid=(nb,),
        in_specs=[
            pl.BlockSpec((1, H_A, TOK_PAD, HD_A), bat),
            pl.BlockSpec((1, H_A, HD_A, n_past), bat),
            pl.BlockSpec((1, H_A, HD_A, n_past), bat),
            pl.BlockSpec((1, H_A, HD_A, LANES), bat),
            pl.BlockSpec((1, H_A, HD_A, LANES), bat),
            pl.BlockSpec(bias_c.shape, const),
            pl.BlockSpec(bias_n.shape, const),
        ],
        out_specs=pl.BlockSpec((1, H_A, TOK_PAD, HD_A), bat),
        out_shape=jax.ShapeDtypeStruct((nb, H_A, TOK_PAD, HD_A), F32),
        compiler_params=_params(("arbitrary",), vmem=28 * 1024 * 1024, fuse_inputs=7),
        name="sample_attn",
    )(q_h, kt, vt, knt, vnt, bias_c, bias_n)


def _sample_bias(rel_bias, n_tok, n_past):
    ninf = lambda n: jnp.full((n, H_A), NEG_INF, F32)
    cached, new = [], []
    for window, dil in BRANCHES:
        nk = window // dil
        rows = n_past // dil
        assert n_past % dil == 0 and rows >= nk
        vec = _branch_bias_vec(rel_bias, window, dil)
        res = jnp.arange(dil)[None, :, None]
        tc, tn = [], []
        for t in range(n_tok):
            near = t // dil + 1
            col = jnp.concatenate([ninf(rows - (nk + 1 - near)), vec[near:nk + 1][::-1]], axis=0)
            tab = jnp.where(res == t % dil, col[:, None, :], NEG_INF).reshape(n_past, H_A)
            tc.append(tab)
            keys = [vec[(t - j) // dil][None] if (t - j) % dil == 0 and 0 <= t - j <= nk * dil else ninf(1)
                    for j in range(n_tok)]
            tn.append(jnp.concatenate(keys + [ninf(LANES - n_tok)], axis=0))
        pad = lambda tabs: jnp.stack(tabs + [jnp.zeros_like(tabs[0])] * (TOK_PAD - n_tok))
        cached.append(pad(tc).transpose(2, 0, 1))
        new.append(pad(tn).transpose(2, 0, 1))
    return jnp.stack(cached), jnp.stack(new)


def _postmix_kernel(mem_ref, o1_ref, l1_ref, o2_ref, l2_ref, o3_ref, l3_ref, x_ref,
                    g1_ref, sc2_ref, sh2_ref, gpost_ref, gpre_ref, wo_ref, wr_ref, br_ref, cnt0_ref,
                    x1_ref, h2_ref, idx_ref, gate_ref, cnt_ref, cnt_s, slab_s, *, dilations):
    first_step = jnp.logical_and(pl.program_id(0) == 0, pl.program_id(1) == 0)

    @pl.when(first_step)
    def _():
        cnt_s[...] = cnt0_ref[...]

    n_slab = D_A // LANES
    tm = x_ref.shape[1]

    def rows(ref, dil, slot):
        if dil == 1:
            return ref[0]
        for r in range(dil):
            for c in range(n_slab):
                lo = r * D_A + c * LANES
                slab_s[slot, c, pl.ds(r, tm // dil, stride=dil), :] = ref[0, :, lo:lo + LANES]
        return jnp.concatenate([slab_s[slot, c] for c in range(n_slab)], axis=1)

    o1, o2, o3 = (rows(r, d, n) for n, (r, d) in enumerate(zip((o1_ref, o2_ref, o3_ref), dilations)))
    l1, l2, l3 = (rows(r, d, 3 + n) for n, (r, d) in enumerate(zip((l1_ref, l2_ref, l3_ref), dilations)))
    lmax = jnp.maximum(jnp.maximum(l1, l2), l3)
    w1 = jnp.exp(l1 - lmax)
    w2 = jnp.exp(l2 - lmax)
    w3 = jnp.exp(l3 - lmax)
    att = (w1 * o1 + w2 * o2 + w3 * o3) / (w1 + w2 + w3)
    y = jnp.dot(mem_ref[0].astype(BF16), wo_ref[0:D_M, :], preferred_element_type=F32)
    y = y + jnp.dot(att.astype(BF16), wo_ref[D_M:, :], preferred_element_type=F32)
    x1 = x_ref[0] + g1_ref[0] * _rms(y, gpost_ref[...])
    x1_ref[0] = x1
    h2 = _rms(x1, gpre_ref[...]) * (1.0 + sc2_ref[0]) + sh2_ref[0]
    h2_ref[0] = h2
    h_hi = h2.astype(BF16)
    h_lo = (h2 - h_hi.astype(F32)).astype(BF16)
    w_hi = wr_ref[0]
    logits = jnp.dot(h_hi, w_hi, preferred_element_type=F32)
    logits = logits + jnp.dot(h_lo, w_hi, preferred_element_type=F32)
    logits = logits + jnp.dot(h_hi, wr_ref[1], preferred_element_type=F32) + br_ref[...]
    lane = lax.broadcasted_iota(jnp.int32, logits.shape, 1)
    lane_f = lane.astype(F32)
    logits = jnp.where(lane < N_EXPERTS, logits, NEG_INF)
    vals, idxs = [], []
    for _ in range(TOP_K):
        mx = jnp.max(logits, axis=-1, keepdims=True)
        ix = jnp.min(jnp.where(logits == mx, lane_f, float(LANES)), axis=-1, keepdims=True)
        vals.append(mx)
        idxs.append(ix)
        logits = jnp.where(lane_f == ix, NEG_INF, logits)
    es = [jnp.exp(v - vals[0]) for v in vals]
    tot = es[0] + es[1] + es[2] + es[3]
    tm = logits.shape[0]
    picks = [(lane_f == ix).astype(F32) for ix in idxs]
    chosen = picks[0] + picks[1] + picks[2] + picks[3]
    r_i = lax.broadcasted_iota(jnp.int32, (tm, tm), 0)
    c_i = lax.broadcasted_iota(jnp.int32, (tm, tm), 1)
    earlier = jnp.dot((c_i < r_i).astype(BF16), chosen.astype(BF16), preferred_element_type=F32)
    before = earlier + cnt_s[...]
    cnt_s[...] = cnt_s[...] + jnp.sum(chosen, axis=0, keepdims=True)
    cnt_ref[...] = cnt_s[...]
    idx_tile = jnp.zeros(logits.shape, F32)
    gate_tile = jnp.zeros(logits.shape, F32)
    for k in range(TOP_K):
        rank = jnp.sum(picks[k] * before, axis=-1, keepdims=True)
        idx_tile = jnp.where(lane == k, idxs[k], idx_tile)
        idx_tile = jnp.where(lane == TOP_K + k, rank, idx_tile)
        gate_tile = jnp.where(lane == k, es[k] / tot, gate_tile)
    idx_ref[0] = idx_tile.astype(jnp.int32)
    gate_ref[0] = gate_tile


def _postmix(mem, branches, x3, gate1, scale2, shift2, g_post, g_pre, w_out_b, w_router_p, b_router_p, counts0, tm,
             dilations=(1, 1, 1)):
    nb, rows, _ = x3.shape
    per_row_mod = gate1.shape[1] != 1
    mod_block = (1, tm, D_MODEL) if per_row_mod else (1, 1, D_MODEL)
    mod_map = (lambda b, i: (b, i, 0)) if per_row_mod else (lambda b, i: (b, 0, 0))
    row_map = lambda b, i: (b, i, 0)
    const2 = lambda b, i: (0, 0)
    half = pl.BlockSpec((1, tm, D_M), row_map)
    flat = [a for pair in branches for a in pair]
    branch_specs = [pl.BlockSpec((1, tm // d, d * D_A), row_map) for d in dilations for _ in range(2)]
    return pl.pallas_call(
        functools.partial(_postmix_kernel, dilations=tuple(dilations)),
        grid=(nb, rows // tm),
        in_specs=[half] + branch_specs + [
            pl.BlockSpec((1, tm, D_MODEL), row_map),
            pl.BlockSpec(mod_block, mod_map),
            pl.BlockSpec(mod_block, mod_map),
            pl.BlockSpec(mod_block, mod_map),
            pl.BlockSpec((1, D_MODEL), const2),
            pl.BlockSpec((1, D_MODEL), const2),
            pl.BlockSpec((D_MODEL, D_MODEL), const2),
            pl.BlockSpec((2, D_MODEL, LANES), lambda b, i: (0, 0, 0)),
            pl.BlockSpec((1, LANES), const2),
            pl.BlockSpec((1, LANES), const2),
        ],
        out_specs=[
            pl.BlockSpec((1, tm, D_MODEL), row_map),
            pl.BlockSpec((1, tm, D_MODEL), row_map),
            pl.BlockSpec((1, tm, LANES), row_map),
            pl.BlockSpec((1, tm, LANES), row_map),
            pl.BlockSpec((1, LANES), const2),
        ],
        out_shape=[
            jax.ShapeDtypeStruct((nb, rows, D_MODEL), F32),
            jax.ShapeDtypeStruct((nb, rows, D_MODEL), F32),
            jax.ShapeDtypeStruct((nb, rows, LANES), jnp.int32),
            jax.ShapeDtypeStruct((nb, rows, LANES), F32),
            jax.ShapeDtypeStruct((1, LANES), F32),
        ],
        scratch_shapes=[pltpu.VMEM((1, LANES), F32), pltpu.VMEM((6, D_A // LANES, tm, LANES), F32)],
        compiler_params=_params(("arbitrary", "arbitrary"), fuse_inputs=17),
        name="postmix",
    )(mem, *flat, x3, gate1, scale2, shift2, g_post.reshape(1, D_MODEL), g_pre.reshape(1, D_MODEL),
      w_out_b, w_router_p, b_router_p, counts0)


def _row_wait(src_rows, dst_rows, sem):
    pltpu.make_async_copy(src_rows, dst_rows, sem).wait()


def _scatter_kernel(slot_ref, hp_ref, hs_ref, xs_hbm, buf, sem, *, n_prompt_tiles):
    h = jnp.where(pl.program_id(0) >= n_prompt_tiles, hs_ref[...], hp_ref[...])
    bits = pltpu.bitcast(h.astype(BF16).astype(F32), U32)
    packed = (bits[:, :PACK_W] & U32(HI_MASK)) | (bits[:, PACK_W:] >> 16)
    buf[...] = pltpu.einshape("m(jc)->mjc", packed, j=PACK_T)

    def issue(r, carry):
        src = buf.at[pl.ds(r, 1)]
        for k in range(TOP_K):
            pltpu.make_async_copy(src, xs_hbm.at[pl.ds(slot_ref[0, 0, r * TOP_K + k], 1)],
                                  sem).start(priority=k % 2)
        return carry

    lax.fori_loop(0, TOK_TILE, issue, 0, unroll=4)
    for _ in range(TOP_K):
        _row_wait(buf, xs_hbm.at[pl.ds(0, TOK_TILE)], sem)


def _moe_scatter(h_p, h_s, slot_tiles):
    tiles = slot_tiles.shape[0]
    n_p = h_p.shape[0] // TOK_TILE
    assert tiles == n_p + 1 and h_s.shape[0] == TOK_TILE
    n_rows = tiles * TOK_TILE * TOP_K
    return pl.pallas_call(
        functools.partial(_scatter_kernel, n_prompt_tiles=n_p),
        grid=(tiles,),
        in_specs=[
            pl.BlockSpec((1, 1, TOK_TILE * TOP_K), lambda i: (i, 0, 0), memory_space=pltpu.SMEM),
            pl.BlockSpec((TOK_TILE, D_MODEL), lambda i: (jnp.minimum(i, n_p - 1), 0)),
            pl.BlockSpec((TOK_TILE, D_MODEL), lambda i: (0, 0)),
        ],
        out_specs=pl.BlockSpec(memory_space=pl.ANY),
        out_shape=jax.ShapeDtypeStruct((n_rows, PACK_T, LANES), U32),
        scratch_shapes=[pltpu.VMEM((TOK_TILE, PACK_T, LANES), U32), pltpu.SemaphoreType.DMA],
        compiler_params=_params(("arbitrary",), vmem=8 * 1024 * 1024),
        name="moe_scatter",
    )(slot_tiles, h_p, h_s)


def _expert_kernel(tile_ref, exp_ref, next_ref, lo_ref, hi_ref, first_ref, nvis_ref,
                   xs_ref, w1_hbm, b1_ref, w2_hbm, b2_ref, ys_ref, w1f, w2f, w1b, w2b, sem):
    v = pl.program_id(0)
    active = v < nvis_ref[0]
    changed = jnp.logical_or(v == 0, exp_ref[v] != exp_ref[jnp.maximum(v - 1, 0)])
    lo = lo_ref[v]
    hi = hi_ref[v]

    def weight_copies(e):
        return (pltpu.make_async_copy(w1_hbm.at[e], w1f, sem.at[0]),
                pltpu.make_async_copy(w2_hbm.at[e], w2f, sem.at[1]))

    @pl.when(v == 0)
    def _():
        for c in weight_copies(exp_ref[0]):
            c.start()

    @pl.when(jnp.logical_and(active, changed))
    def _():
        for c in weight_copies(exp_ref[v]):
            c.wait()
        w1b[...] = w1f[...].astype(BF16)
        w2b[...] = w2f[...].astype(BF16)

        @pl.when(next_ref[v] >= 0)
        def _():
            for c in weight_copies(next_ref[v]):
                c.start()

    def mlp_rows(r0):
        w = pltpu.einshape("mjc->m(jc)", xs_ref[r0:r0 + MOE_SUB])
        x_hi = pltpu.bitcast(w & U32(HI_MASK), F32).astype(BF16)
        x_lo = pltpu.bitcast(w << 16, F32).astype(BF16)
        hgu = jnp.dot(x_hi, w1b[0:PACK_W, :], preferred_element_type=F32)
        hgu = hgu + jnp.dot(x_lo, w1b[PACK_W:, :], preferred_element_type=F32) + b1_ref[0]
        x_glu = jnp.minimum(hgu[:, :D_FF], SWIGLU_LIMIT)
        x_lin = jnp.clip(hgu[:, D_FF:], -SWIGLU_LIMIT, SWIGLU_LIMIT)
        act = x_glu * jax.nn.sigmoid(SWIGLU_ALPHA * x_glu) * (x_lin + 1.0)
        y = jnp.dot(act.astype(BF16), w2b[...], preferred_element_type=F32) + b2_ref[0]
        return pltpu.einshape("m(jc)->mjc", y, j=ROW_T)

    subs = [r * MOE_SUB for r in range(MOE_M // MOE_SUB)]
    whole = jnp.logical_and(lo == 0, hi == MOE_M)

    @pl.when(jnp.logical_and(active, whole))
    def _():
        for r0 in subs:
            ys_ref[r0:r0 + MOE_SUB] = mlp_rows(r0)

    shared = jnp.logical_and(active, jnp.logical_not(whole))

    @pl.when(jnp.logical_and(shared, first_ref[v] == 1))
    def _():
        ys_ref[...] = jnp.zeros(ys_ref.shape, F32)

    for r0 in subs:
        @pl.when(jnp.logical_and(shared, jnp.logical_and(lo < r0 + MOE_SUB, hi > r0)))
        def _():
            rows = r0 + lax.broadcasted_iota(jnp.int32, (MOE_SUB, 1, 1), 0)
            mine = jnp.logical_and(rows >= lo, rows < hi)
            ys_ref[r0:r0 + MOE_SUB] = jnp.where(mine, mlp_rows(r0), ys_ref[r0:r0 + MOE_SUB])


def _moe_experts(xs, visits, w1, b1, w2, b2):
    n_rows = xs.shape[0]
    n_vis = visits[0].shape[0]
    tile_map = lambda v, tile, *_: (tile[v], 0, 0)
    exp_map = lambda v, tile, exp, *_: (exp[v], 0, 0)
    return pl.pallas_call(
        _expert_kernel,
        grid_spec=pltpu.PrefetchScalarGridSpec(
            num_scalar_prefetch=7,
            grid=(n_vis,),
            in_specs=[
                pl.BlockSpec((MOE_M, PACK_T, LANES), tile_map),
                pl.BlockSpec(memory_space=pl.ANY),
                pl.BlockSpec((1, 1, 2 * D_FF), exp_map),
                pl.BlockSpec(memory_space=pl.ANY),
                pl.BlockSpec((1, 1, D_MODEL), exp_map),
            ],
            out_specs=pl.BlockSpec((MOE_M, ROW_T, LANES), tile_map),
            scratch_shapes=[pltpu.VMEM((D_MODEL, 2 * D_FF), F32), pltpu.VMEM((D_FF, D_MODEL), F32),
                            pltpu.VMEM((D_MODEL, 2 * D_FF), BF16), pltpu.VMEM((D_FF, D_MODEL), BF16),
                            pltpu.SemaphoreType.DMA((2,))],
        ),
        out_shape=jax.ShapeDtypeStruct((n_rows, ROW_T, LANES), F32),
        compiler_params=_params(("arbitrary",), vmem=44 * 1024 * 1024),
        name="moe_experts",
    )(*visits, xs, w1, b1.reshape(N_EXPERTS, 1, 2 * D_FF), w2, b2.reshape(N_EXPERTS, 1, D_MODEL))


def _combine_kernel(slot_ref, next_ref, gatep_ref, gates_ref, x1p_ref, x1s_ref, g2p_ref, g2s_ref, gpost_ref, ys_hbm,
                    yp_ref, ysm_ref, buf, sem, *, n_prompt_tiles):
    i = pl.program_id(0)
    last = pl.num_programs(0) - 1
    cur = lax.rem(i, 2)

    def gather(slots, b):
        def issue(r, carry):
            for k in range(TOP_K):
                pltpu.make_async_copy(ys_hbm.at[pl.ds(slots[0, 0, r * TOP_K + k], 1)],
                                      buf.at[b, pl.ds(k * TOK_TILE + r, 1)], sem.at[b]).start(priority=k % 2)
            return carry

        lax.fori_loop(0, TOK_TILE, issue, 0, unroll=4)

    @pl.when(i == 0)
    def _():
        gather(slot_ref, cur)

    @pl.when(i < last)
    def _():
        gather(next_ref, 1 - cur)

    for k in range(TOP_K):
        _row_wait(ys_hbm.at[pl.ds(0, TOK_TILE)], buf.at[cur, pl.ds(k * TOK_TILE, TOK_TILE)], sem.at[cur])
    is_sample = i >= n_prompt_tiles
    g = jnp.where(is_sample, gates_ref[...], gatep_ref[...])
    f = jnp.zeros((TOK_TILE, D_MODEL), F32)
    for k in range(TOP_K):
        rows = pltpu.einshape("mjc->m(jc)", buf[cur, pl.ds(k * TOK_TILE, TOK_TILE)])
        f = f + rows * g[:, k:k + 1]
    x1 = jnp.where(is_sample, x1s_ref[...], x1p_ref[...])
    g2 = jnp.where(is_sample, g2s_ref[...], g2p_ref[0])
    y = x1 + g2 * _rms(f, gpost_ref[...])

    @pl.when(jnp.logical_not(is_sample))
    def _():
        yp_ref[...] = y

    @pl.when(is_sample)
    def _():
        ysm_ref[...] = y


def _moe_combine(ys, slot_tiles, gates_p, gates_s, x1_p, x1_s, gate2_p, gate2_s, g_post, rows_per_batch):
    tiles = slot_tiles.shape[0]
    n_p = x1_p.shape[0] // TOK_TILE
    assert tiles == n_p + 1 and x1_s.shape[0] == TOK_TILE
    tiles_per_batch = rows_per_batch // TOK_TILE
    p_tile = lambda i: (jnp.minimum(i, n_p - 1), 0)
    const2 = lambda i: (0, 0)
    return pl.pallas_call(
        functools.partial(_combine_kernel, n_prompt_tiles=n_p),
        grid=(tiles,),
        in_specs=[
            pl.BlockSpec((1, 1, TOK_TILE * TOP_K), lambda i: (i, 0, 0), memory_space=pltpu.SMEM),
            pl.BlockSpec((1, 1, TOK_TILE * TOP_K), lambda i: (jnp.minimum(i + 1, tiles - 1), 0, 0),
                         memory_space=pltpu.SMEM),
            pl.BlockSpec((TOK_TILE, LANES), p_tile),
            pl.BlockSpec((TOK_TILE, LANES), const2),
            pl.BlockSpec((TOK_TILE, D_MODEL), p_tile),
            pl.BlockSpec((TOK_TILE, D_MODEL), const2),
            pl.BlockSpec((1, 1, D_MODEL), lambda i: (jnp.minimum(i, n_p - 1) // tiles_per_batch, 0, 0)),
            pl.BlockSpec((TOK_TILE, D_MODEL), const2),
            pl.BlockSpec((1, D_MODEL), const2),
            pl.BlockSpec(memory_space=pl.ANY),
        ],
        out_specs=[pl.BlockSpec((TOK_TILE, D_MODEL), p_tile), pl.BlockSpec((TOK_TILE, D_MODEL), const2)],
        out_shape=[jax.ShapeDtypeStruct(x1_p.shape, F32), jax.ShapeDtypeStruct(x1_s.shape, F32)],
        scratch_shapes=[pltpu.VMEM((2, TOP_K * TOK_TILE, ROW_T, LANES), F32), pltpu.SemaphoreType.DMA((2,))],
        compiler_params=_params(("arbitrary",), vmem=16 * 1024 * 1024),
        name="moe_combine",
    )(slot_tiles, slot_tiles, gates_p, gates_s, x1_p, x1_s, gate2_p, gate2_s, g_post.reshape(1, D_MODEL), ys)


def _moe_plan(routes, counts):
    a = sum(r.shape[0] for r in routes) * TOP_K
    assert a % MOE_M == 0
    g_end = jnp.cumsum(counts)
    g_start = g_end - counts

    def slots(route):
        flat_e = route[:, :TOP_K].reshape(-1)
        rank = route[:, TOP_K:2 * TOP_K].reshape(-1)
        onehot = flat_e[:, None] == jnp.arange(N_EXPERTS)[None, :]
        return (jnp.sum(jnp.where(onehot, g_start[None, :], 0), axis=1) + rank).astype(jnp.int32)

    slot_of = jnp.concatenate([slots(r) for r in routes])

    n_tiles = a // MOE_M
    n_vis = n_tiles + N_EXPERTS - 1
    first_tile = g_start // MOE_M
    last_tile = jnp.where(counts > 0, (g_end - 1) // MOE_M, first_tile)
    nvis_e = jnp.where(counts > 0, last_tile - first_tile + 1, 0)
    v_end = jnp.cumsum(nvis_e)
    v_start = v_end - nvis_e
    total = v_end[-1]
    v = jnp.arange(n_vis)
    vc = jnp.minimum(v, total - 1)
    e_of = jnp.minimum(jnp.sum((vc[:, None] >= v_end[None, :]).astype(jnp.int32), axis=1), N_EXPERTS - 1)
    is_e = e_of[:, None] == jnp.arange(N_EXPERTS)[None, :]
    of_e = lambda t: jnp.sum(jnp.where(is_e, t[None, :], 0), axis=1)
    tile_of = of_e(first_tile) + (vc - of_e(v_start))
    lo = jnp.maximum(of_e(g_start), tile_of * MOE_M) - tile_of * MOE_M
    hi = jnp.minimum(of_e(g_end), (tile_of + 1) * MOE_M) - tile_of * MOE_M
    prev_tile = jnp.concatenate([jnp.full((1,), -1, tile_of.dtype), tile_of[:-1]])
    first = (tile_of != prev_tile).astype(jnp.int32)
    experts = jnp.arange(N_EXPERTS)
    later = (experts[None, :] > experts[:, None]) & (counts[None, :] > 0)
    next_e = jnp.min(jnp.where(later, experts[None, :], N_EXPERTS), axis=1)
    next_e = jnp.where(next_e < N_EXPERTS, next_e, -1)
    i32 = lambda t: t.astype(jnp.int32)
    visits = (i32(tile_of), i32(e_of), i32(of_e(next_e)), i32(lo), i32(hi), first, i32(total).reshape(1))
    return slot_of, visits


def _reorder_w_in(w_in):
    off_om_end = 4 * D_M
    off_g_end = off_om_end + 2 * H_M
    w = jnp.concatenate([
        w_in[:, :off_om_end],
        w_in[:, off_g_end:],
        w_in[:, off_om_end:off_g_end],
        jnp.zeros((D_MODEL, LANES - 2 * H_M), w_in.dtype),
    ], axis=1)
    return w.astype(BF16)


def kernel(x_prompt, x_sample, cache_k, cache_v, state_conv, state_C, state_n, state_m, c_prompt, c_sample, w_ada, b_ada, g_pre_mix, g_post_mix, g_pre_ffn, g_post_ffn, w_in, b_ig, b_fg, conv_w, conv_b, mh_norm, rel_bias, w_out, w_router, b_router, w1, b1, w2, b2):
    depth = w_in.shape[0]
    assert depth == 1
    l = 0
    nb_p, seq, _ = x_prompt.shape
    nb_s, n_tok, _ = x_sample.shape
    rows_s = nb_s * n_tok
    assert rows_s == TOK_TILE

    n_c = nb_p + nb_s
    c_pad = -(-n_c // 8) * 8
    c_all = jnp.concatenate([c_prompt, c_sample, jnp.zeros((c_pad - n_c, D_MODEL), F32)], axis=0)
    mod = _adaln(c_all, w_ada[l], b_ada[l])
    mods = [mod[:, i * D_MODEL:(i + 1) * D_MODEL] for i in range(6)]
    mods_p = [m[:nb_p].reshape(nb_p, 1, D_MODEL) for m in mods]
    mods_s = [jnp.repeat(m[nb_p:n_c], n_tok, axis=0).reshape(1, rows_s, D_MODEL) for m in mods]

    w_r = _reorder_w_in(w_in[l])
    bias_g = jnp.concatenate([b_ig[l], b_fg[l], jnp.zeros((LANES - 2 * H_M,), F32)]).reshape(1, LANES)
    w_out_b = w_out[l].astype(BF16)
    w_router_f = jnp.concatenate([w_router[l], jnp.zeros((D_MODEL, LANES - N_EXPERTS), F32)], axis=1)
    w_router_hi = w_router_f.astype(BF16)
    w_router_p = jnp.stack([w_router_hi, (w_router_f - w_router_hi.astype(F32)).astype(BF16)])
    b_router_p = jnp.concatenate([b_router[l].astype(F32), jnp.zeros((LANES - N_EXPERTS,), F32)]).reshape(1, LANES)

    shift1, scale1, gate1, shift2, scale2, gate2_p = mods_p
    dils = tuple(d for _, d in BRANCHES)
    assert dils[0] == 1
    u, v_m, og, gts, qa, ka, va, *dilated = _inproj(x_prompt, g_pre_mix[l], scale1, shift1, w_r, bias_g, tm=512,
                                                   dilations=dils[1:])
    qkv_d = [(qa, ka, va)] + [tuple(dilated[3 * n:3 * n + 3]) for n in range(len(dils) - 1)]
    gates_row = gts[:, :, :8].transpose(0, 2, 1)
    zeros = lambda *s: jnp.zeros(s, F32)
    mem, c_p, n_p, m_p = _mlstm(u, v_m, og, gts, gates_row, zeros(nb_p, 8, 2 * D_M),
                                zeros(nb_p, H_M, DK_M, DK_M), zeros(nb_p, H_M, DK_M), zeros(nb_p, H_M, LANES),
                                conv_w[l], conv_b[l], mh_norm[l], MLSTM_L_PROMPT)
    branches = [_attn_branch(*qkv, _prompt_bias(rel_bias, w, d), d) for qkv, (w, d) in zip(qkv_d, BRANCHES)]
    x1_p, h2_p, idx_p, gate_p, cnt_p = _postmix(mem, branches, x_prompt, gate1, scale2, shift2,
                                                g_post_mix[l], g_pre_ffn[l], w_out_b, w_router_p, b_router_p,
                                                zeros(1, LANES), tm=512, dilations=dils)
    keep = min(WINDOW_MAX, seq)
    k_p, v_p = _window_heads(ka, va, keep)
    conv_p = u[:, seq - (CONV_W - 1):]

    shift1, scale1, gate1, shift2, scale2, gate2_s = mods_s
    x_s = x_sample.reshape(1, rows_s, D_MODEL)
    u_s, v_s, og_s, gts_s, qa_s, ka_s, va_s = _inproj(x_s, g_pre_mix[l], scale1, shift1, w_r, bias_g, tm=rows_s)
    per_b = lambda t: t.reshape(nb_s, n_tok, t.shape[-1])
    ls = MLSTM_L_SAMPLE
    pad_rows = lambda t, n: jnp.concatenate([t, jnp.zeros((nb_s, n - t.shape[1], t.shape[2]), t.dtype)], axis=1)
    inert = jnp.concatenate([jnp.full((H_M,), NEG_INF, F32), jnp.zeros((LANES - H_M,), F32)])
    gts_pad = jnp.concatenate([per_b(gts_s), jnp.broadcast_to(inert, (nb_s, ls - n_tok, LANES))], axis=1)
    tail0 = jnp.concatenate([zeros(nb_s, 8 - (CONV_W - 1), 2 * D_M), state_conv[l].astype(F32)], axis=1)
    m0x = jnp.broadcast_to(state_m[l].astype(F32)[:, :, None], (nb_s, H_M, LANES))
    mem_s, c_s, n_s, m_s = _mlstm(pad_rows(per_b(u_s), ls), pad_rows(per_b(v_s), ls), pad_rows(per_b(og_s), ls),
                                  gts_pad, gts_pad[:, :, :8].transpose(0, 2, 1), tail0,
                                  state_C[l].astype(F32), state_n[l].astype(F32), m0x,
                                  conv_w[l], conv_b[l], mh_norm[l], ls)
    mem_s = mem_s[:, :n_tok].reshape(1, rows_s, D_M)
    heads = lambda t: t.reshape(nb_s, n_tok, H_A, HD_A)
    k_new, v_new = heads(ka_s), heads(va_s)
    n_past = cache_k.shape[2]
    assert n_tok <= TOK_PAD
    q_h = jnp.pad(heads(qa_s).transpose(0, 2, 1, 3), ((0, 0), (0, 0), (0, TOK_PAD - n_tok), (0, 0)))
    new_t = lambda t: jnp.pad(t.transpose(0, 2, 3, 1), ((0, 0), (0, 0), (0, 0), (0, LANES - n_tok)))
    cache_t = lambda c: c[l].astype(F32).transpose(0, 2, 3, 1)
    att_s = _sample_attention(q_h, cache_t(cache_k), cache_t(cache_v), new_t(k_new), new_t(v_new),
                              *_sample_bias(rel_bias, n_tok, n_past))
    att_s = att_s[:, :, :n_tok].transpose(0, 2, 1, 3).reshape(1, rows_s, D_A)
    zero_o = zeros(1, rows_s, D_A)
    ninf = jnp.full((1, rows_s, D_A), NEG_INF, F32)
    branches_s = [(att_s, zero_o), (zero_o, ninf), (zero_o, ninf)]
    x1_s, h2_s, idx_s, gate_s, cnt_all = _postmix(mem_s, branches_s, x_s, gate1, scale2, shift2,
                                                  g_post_mix[l], g_pre_ffn[l], w_out_b, w_router_p, b_router_p,
                                                  cnt_p, tm=rows_s)

    t_p = nb_p * seq
    slot_of, visits = _moe_plan([idx_p.reshape(t_p, LANES), idx_s.reshape(rows_s, LANES)],
                                cnt_all[0, :N_EXPERTS].astype(jnp.int32))
    slot_tiles = slot_of.reshape((t_p + rows_s) // TOK_TILE, 1, TOK_TILE * TOP_K)
    xs = _moe_scatter(h2_p.reshape(t_p, D_MODEL), h2_s.reshape(rows_s, D_MODEL), slot_tiles)
    ys = _moe_experts(xs, visits, w1[l], b1[l], w2[l], b2[l])
    y_p, y_s = _moe_combine(ys, slot_tiles, gate_p.reshape(t_p, LANES), gate_s.reshape(rows_s, LANES), x1_p.reshape(t_p, D_MODEL), x1_s.reshape(rows_s, D_MODEL),
                            gate2_p, gate2_s.reshape(rows_s, D_MODEL), g_post_ffn[l], seq)
    y_prompt = y_p.reshape(nb_p, seq, D_MODEL)
    y_sample = y_s.reshape(nb_s, n_tok, D_MODEL)

    st = lambda t: t[None]
    conv_s = jnp.concatenate([state_conv[l].astype(F32), per_b(u_s)], axis=1)[:, -(CONV_W - 1):]
    return (y_prompt, y_sample, st(k_p), st(v_p), st(conv_p), st(c_p), st(n_p), st(m_p[:, :, 0]),
            st(k_new), st(v_new), st(conv_s), st(c_s), st(n_s), st(m_s[:, :, 0]))
```

```python
import functools
import math

import jax
import jax.numpy as jnp
import numpy as np
from jax import lax
from jax.experimental import pallas as pl
from jax.experimental.pallas import tpu as pltpu

F32 = jnp.float32
BF16 = jnp.bfloat16
U32 = jnp.uint32
NEG_INF = float("-inf")

D_MODEL = 1024
H_M = 4
D_M = 512
DK_M = 128
CONV_W = 4
H_A = 8
D_A = 512
HD_A = 64
BRANCHES = ((128, 1), (512, 4), (2048, 16))
WINDOW_MAX = 2048
QB = 128
N_BUCKETS = 32
MAX_DISTANCE = 2048
N_EXPERTS = 32
TOP_K = 4
D_FF = 1024
SWIGLU_LIMIT = 7.0
SWIGLU_ALPHA = 1.702
EPS = 1e-6

LANES = 128
C_U, C_V, C_O, C_QA, C_KA, C_VA, C_G = 0, 1024, 1536, 2048, 2560, 3072, 3584
D_IN_R = C_G + LANES

MLSTM_L_PROMPT = 256
MLSTM_L_SAMPLE = 32
ATT_QL = 1024
MOE_M = 512
MOE_SUB = 256
TOK_TILE = 128
PACK_W = D_MODEL // 2
PACK_T = PACK_W // LANES
ROW_T = D_MODEL // LANES
HI_MASK = 0xFFFF0000
VMEM_LIMIT = 56 * 1024 * 1024


def _params(sem, vmem=None, fuse_inputs=0):
    fuse = [True] * fuse_inputs if isinstance(fuse_inputs, int) else list(fuse_inputs)
    return pltpu.CompilerParams(dimension_semantics=sem, vmem_limit_bytes=vmem or VMEM_LIMIT,
                                allow_input_fusion=fuse or None)


def _rms(x, g):
    return x * lax.rsqrt(jnp.mean(x * x, axis=-1, keepdims=True) + EPS) * g


def _log_sigmoid(t):
    return jnp.minimum(t, 0.0) - jnp.log1p(jnp.exp(-jnp.abs(t)))


def _ada_kernel(c_ref, w_ref, b_ref, o_ref):
    c = c_ref[...]
    a = (c * jax.nn.sigmoid(c)).astype(BF16)
    o_ref[...] = jnp.dot(a, w_ref[...].astype(BF16), preferred_element_type=F32) + b_ref[...]


def _adaln(c_all, w_ada, b_ada):
    rows = c_all.shape[0]
    n = w_ada.shape[1]
    tn = 1024
    return pl.pallas_call(
        _ada_kernel,
        grid=(n // tn,),
        in_specs=[
            pl.BlockSpec((rows, D_MODEL), lambda j: (0, 0)),
            pl.BlockSpec((D_MODEL, tn), lambda j: (0, j)),
            pl.BlockSpec((1, tn), lambda j: (0, j)),
        ],
        out_specs=pl.BlockSpec((rows, tn), lambda j: (0, j)),
        out_shape=jax.ShapeDtypeStruct((rows, n), F32),
        compiler_params=_params(("arbitrary",), vmem=16 * 1024 * 1024, fuse_inputs=3),
        name="adaln",
    )(c_all, w_ada, b_ada.reshape(1, n))


def _inproj_kernel(x_ref, g_ref, sc_ref, sh_ref, w_ref, bg_ref,
                   u_ref, v_ref, o_ref, gt_ref, qa_ref, ka_ref, va_ref, *rest, dilations):
    dil_refs, slab_s = (rest[:-1], rest[-1]) if dilations else (rest, None)
    x = x_ref[0]
    h = _rms(x, g_ref[...]) * (1.0 + sc_ref[0]) + sh_ref[0]
    hb = h.astype(BF16)

    def seg(lo, width):
        return jnp.dot(hb, w_ref[:, lo:lo + width], preferred_element_type=F32)

    u_ref[0] = seg(C_U, 2 * D_M)
    v_ref[0] = seg(C_V, D_M)
    o_ref[0] = jax.nn.sigmoid(seg(C_O, D_M))
    qkv = (seg(C_QA, D_A) * (HD_A ** -0.5), seg(C_KA, D_A), seg(C_VA, D_A))
    for ref, val in zip((qa_ref, ka_ref, va_ref), qkv):
        ref[0] = val
    n_slab = D_A // LANES
    rows = x.shape[0]
    for j, val in enumerate(qkv if dilations else ()):
        for c in range(n_slab):
            slab_s[j, c] = val[:, c * LANES:(c + 1) * LANES]
        for n, dil in enumerate(dilations):
            for r in range(dil):
                for c in range(n_slab):
                    lo = r * D_A + c * LANES
                    dil_refs[3 * n + j][0, :, lo:lo + LANES] = slab_s[j, c, pl.ds(r, rows // dil, stride=dil), :]
    t = seg(C_G, LANES) + bg_ref[...]
    lane = lax.broadcasted_iota(jnp.int32, t.shape, 1)
    gt_ref[0] = jnp.where(lane < H_M, t, _log_sigmoid(t))


def _inproj(x3, g_pre, scale, shift, w_r, bias_g, tm, dilations=()):
    nb, rows, _ = x3.shape
    per_row_mod = scale.shape[1] != 1
    mod_block = (1, tm, D_MODEL) if per_row_mod else (1, 1, D_MODEL)
    mod_map = (lambda b, i: (b, i, 0)) if per_row_mod else (lambda b, i: (b, 0, 0))
    row_map = lambda b, i: (b, i, 0)
    widths = (2 * D_M, D_M, D_M, LANES, D_A, D_A, D_A)
    dil_specs = [pl.BlockSpec((1, tm // d, d * D_A), row_map) for d in dilations for _ in range(3)]
    dil_shapes = [jax.ShapeDtypeStruct((nb, rows // d, d * D_A), F32) for d in dilations for _ in range(3)]
    return pl.pallas_call(
        functools.partial(_inproj_kernel, dilations=tuple(dilations)),
        grid=(nb, rows // tm),
        in_specs=[
            pl.BlockSpec((1, tm, D_MODEL), row_map),
            pl.BlockSpec((1, D_MODEL), lambda b, i: (0, 0)),
            pl.BlockSpec(mod_block, mod_map),
            pl.BlockSpec(mod_block, mod_map),
            pl.BlockSpec((D_MODEL, D_IN_R), lambda b, i: (0, 0)),
            pl.BlockSpec((1, LANES), lambda b, i: (0, 0)),
        ],
        out_specs=[pl.BlockSpec((1, tm, w), row_map) for w in widths] + dil_specs,
        out_shape=[jax.ShapeDtypeStruct((nb, rows, w), F32) for w in widths] + dil_shapes,
        scratch_shapes=[pltpu.VMEM((3, D_A // LANES, tm, LANES), F32)] if dilations else [],
        compiler_params=_params(("arbitrary", "arbitrary"), fuse_inputs=6),
        name="inproj",
    )(x3, g_pre.reshape(1, D_MODEL), scale, shift, w_r, bias_g)


def _mlstm_kernel(u_ref, v_ref, og_ref, gc_ref, gr_ref, tail0_ref, c0_ref, n0_ref, m0_ref,
                  cw_ref, cb_ref, mh_ref,
                  mem_ref, c_out_ref, n_out_ref, m_out_ref,
                  c_s, n_s, m_s, tail_s, uext_s, *, chunk):
    L = chunk
    step = pl.program_id(1)

    @pl.when(step == 0)
    def _():
        c_s[...] = c0_ref[0]
        n_s[...] = n0_ref[0]
        m_s[...] = m0_ref[0]
        tail_s[...] = tail0_ref[0]

    u = u_ref[0]
    uext_s[0:8, :] = tail_s[...]
    uext_s[8:8 + L, :] = u
    acc = cb_ref[...] + uext_s[pl.ds(5, L), :] * cw_ref[0:1, :]
    acc = acc + uext_s[pl.ds(6, L), :] * cw_ref[1:2, :]
    acc = acc + uext_s[pl.ds(7, L), :] * cw_ref[2:3, :]
    acc = acc + u * cw_ref[3:4, :]
    tail_s[...] = uext_s[pl.ds(L, 8), :]
    qk = acc * jax.nn.sigmoid(acc)

    gc = gc_ref[0]
    gr = gr_ref[0]
    row = lax.broadcasted_iota(jnp.int32, (L, L), 0)
    col = lax.broadcasted_iota(jnp.int32, (L, L), 1)
    causal = col <= row
    tri = causal.astype(BF16)
    tri_t = (row <= col).astype(BF16)
    lane = lax.broadcasted_iota(jnp.int32, gc.shape, 1)
    srow = lax.broadcasted_iota(jnp.int32, gr.shape, 0)

    def bf16_parts(x):
        hi = x.astype(BF16)
        r1 = x - hi.astype(F32)
        mid = r1.astype(BF16)
        return hi, mid, (r1 - mid.astype(F32)).astype(BF16)

    b_col_all = sum(jnp.dot(tri, p, preferred_element_type=F32)
                    for p in bf16_parts(jnp.where(lane >= H_M, gc, 0.0)))
    b_row_all = sum(jnp.dot(p, tri_t, preferred_element_type=F32)
                    for p in bf16_parts(jnp.where(srow >= H_M, gr, 0.0)))

    v_all = v_ref[0]
    og = og_ref[0]
    for h in range(H_M):
        q = qk[:, h * DK_M:(h + 1) * DK_M]
        k = qk[:, D_M + h * DK_M:D_M + (h + 1) * DK_M] * (DK_M ** -0.5)
        v = v_all[:, h * DK_M:(h + 1) * DK_M]
        li_row = gr[h:h + 1, :]
        li_col = gc[:, h:h + 1]
        b_row = b_row_all[H_M + h:H_M + h + 1, :]
        b_col = b_col_all[:, H_M + h:H_M + h + 1]
        m0 = m_s[h:h + 1, 0:1]
        c0 = c_s[h]
        n0 = n_s[h:h + 1, :]

        d = jnp.where(causal, b_col - b_row + li_row, NEG_INF)
        g_state = b_col + m0
        m = jnp.maximum(jnp.max(d, axis=-1, keepdims=True), g_state)
        w_state = jnp.exp(g_state - m)
        qb = q.astype(BF16)
        kb = k.astype(BF16)
        vb = v.astype(BF16)
        s = lax.dot_general(qb, kb, (((1,), (1,)), ((), ())), preferred_element_type=F32)
        s = s * jnp.exp(d - m)
        num = jnp.dot(s.astype(BF16), vb, preferred_element_type=F32)
        num = num + w_state * jnp.dot(qb, c0.astype(BF16), preferred_element_type=F32)
        den = jnp.sum(s, axis=-1, keepdims=True) + w_state * jnp.sum(q * n0, axis=-1, keepdims=True)
        hh = num / jnp.maximum(jnp.abs(den), jnp.exp(-m))

        b_last = b_col[L - 1:L, :]
        g_tok_row = b_last - b_row + li_row
        m_new = jnp.maximum(b_last + m0, jnp.max(g_tok_row, axis=-1, keepdims=True))
        w_tok_col = jnp.exp(b_last - b_col + li_col - m_new)
        decay = jnp.exp(b_last + m0 - m_new)
        kw = k * w_tok_col
        c_s[h] = decay * c0 + jnp.dot(kw.T.astype(BF16), vb, preferred_element_type=F32)
        n_s[h:h + 1, :] = decay * n0 + jnp.sum(kw, axis=0, keepdims=True)
        m_s[h:h + 1, :] = jnp.broadcast_to(m_new, (1, LANES))

        hn = hh * lax.rsqrt(jnp.mean(hh * hh, axis=-1, keepdims=True) + EPS)
        hn = hn * mh_ref[:, h * DK_M:(h + 1) * DK_M]
        mem_ref[0, :, h * DK_M:(h + 1) * DK_M] = og[:, h * DK_M:(h + 1) * DK_M] * hn

    @pl.when(step == pl.num_programs(1) - 1)
    def _():
        c_out_ref[0] = c_s[...]
        n_out_ref[0] = n_s[...]
        m_out_ref[0] = m_s[...]


def _mlstm(u, v, og, gates_col, gates_row, tail0, c0, n0, m0x, conv_w, conv_b, mh_norm, chunk):
    nb, rows, _ = u.shape
    nc = rows // chunk
    row_map = lambda b, c: (b, c, 0)
    bat3 = lambda b, c: (b, 0, 0)
    const2 = lambda b, c: (0, 0)
    return pl.pallas_call(
        functools.partial(_mlstm_kernel, chunk=chunk),
        grid=(nb, nc),
        in_specs=[
            pl.BlockSpec((1, chunk, 2 * D_M), row_map),
            pl.BlockSpec((1, chunk, D_M), row_map),
            pl.BlockSpec((1, chunk, D_M), row_map),
            pl.BlockSpec((1, chunk, LANES), row_map),
            pl.BlockSpec((1, 8, chunk), lambda b, c: (b, 0, c)),
            pl.BlockSpec((1, 8, 2 * D_M), bat3),
            pl.BlockSpec((1, H_M, DK_M, DK_M), lambda b, c: (b, 0, 0, 0)),
            pl.BlockSpec((1, H_M, DK_M), bat3),
            pl.BlockSpec((1, H_M, LANES), bat3),
            pl.BlockSpec((CONV_W, 2 * D_M), const2),
            pl.BlockSpec((1, 2 * D_M), const2),
            pl.BlockSpec((1, D_M), const2),
        ],
        out_specs=[
            pl.BlockSpec((1, chunk, D_M), row_map),
            pl.BlockSpec((1, H_M, DK_M, DK_M), lambda b, c: (b, 0, 0, 0)),
            pl.BlockSpec((1, H_M, DK_M), bat3),
            pl.BlockSpec((1, H_M, LANES), bat3),
        ],
        out_shape=[
            jax.ShapeDtypeStruct((nb, rows, D_M), F32),
            jax.ShapeDtypeStruct((nb, H_M, DK_M, DK_M), F32),
            jax.ShapeDtypeStruct((nb, H_M, DK_M), F32),
            jax.ShapeDtypeStruct((nb, H_M, LANES), F32),
        ],
        scratch_shapes=[
            pltpu.VMEM((H_M, DK_M, DK_M), F32),
            pltpu.VMEM((H_M, DK_M), F32),
            pltpu.VMEM((H_M, LANES), F32),
            pltpu.VMEM((8, 2 * D_M), F32),
            pltpu.VMEM((chunk + 8, 2 * D_M), F32),
        ],
        compiler_params=_params(("arbitrary", "arbitrary"), vmem=16 * 1024 * 1024, fuse_inputs=12),
        name="mlstm",
    )(u, v, og, gates_col, gates_row, tail0, c0, n0, m0x,
      conv_w, conv_b.reshape(1, 2 * D_M), mh_norm.reshape(1, D_M))


def _attn_kernel(q_ref, kc_ref, vc_ref, kp_ref, vp_ref, bias_ref, ones_ref, o_ref, l_ref, k_s, v_s, *, ql):
    first = pl.program_id(2) == 0
    k_s[0:QB, :] = kp_ref[0].astype(BF16)
    k_s[QB:QB + ql, :] = kc_ref[0].astype(BF16)
    v_s[0:QB, :] = vp_ref[0].astype(BF16)
    v_s[QB:QB + ql, :] = vc_ref[0].astype(BF16)
    col = lax.broadcasted_iota(jnp.int32, (QB, 2 * QB), 1)
    pad_keys = jnp.logical_and(first, col < QB)
    head0 = lax.broadcasted_iota(jnp.int32, (1, LANES), 1) < HD_A
    n_slab = D_A // LANES
    n_res = q_ref.shape[2] // D_A
    pad_all = jnp.concatenate([pad_keys] * H_A, axis=0)

    def raw_logits(c, j):
        sl = slice(c * LANES, (c + 1) * LANES)
        q = q_ref[0, j * QB:(j + 1) * QB, sl].astype(BF16)
        zero = jnp.zeros_like(q)
        q2 = jnp.concatenate([jnp.where(head0, q, zero), jnp.where(head0, zero, q)], axis=0)
        return lax.dot_general(q2, k_s[j * QB:(j + 2) * QB, sl], (((1,), (1,)), ((), ())),
                               preferred_element_type=F32)

    def softmax(j, logits):
        logits = logits + bias_ref[...].reshape(H_A * QB, 2 * QB)
        if j == 0:
            logits = jnp.where(pad_all, NEG_INF, logits)
        m = jnp.max(logits, axis=-1, keepdims=True)
        return jnp.exp(logits - m).astype(BF16), m

    def finish(c, j, p, m):
        sl = slice(c * LANES, (c + 1) * LANES)
        rows = slice(j * QB, (j + 1) * QB)
        p_cat = jnp.concatenate([p[:QB], p[QB:]], axis=1)
        v_all = v_s[j * QB:(j + 2) * QB, sl]
        v_bd = jnp.concatenate([jnp.where(head0, v_all, jnp.zeros_like(v_all)),
                                jnp.where(head0, jnp.zeros_like(v_all), v_all)], axis=0)
        o = jnp.dot(p_cat, v_bd, preferred_element_type=F32)
        s = jnp.dot(p_cat, ones_ref[...], preferred_element_type=F32)
        o_ref[0, rows, sl] = o / s
        l_ref[0, rows, sl] = jnp.where(head0, m[:QB], m[QB:]) + jnp.log(s)

    for res in range(n_res):
        slabs = range(res * n_slab, (res + 1) * n_slab)
        for j in range(ql // QB):
            p, m = softmax(j, jnp.concatenate([raw_logits(c, j) for c in slabs], axis=0))
            for n, c in enumerate(slabs):
                part = slice(2 * n * QB, 2 * (n + 1) * QB)
                finish(c, j, p[part], m[part])


def _attn_branch(qa, ka, va, bias, dil):
    nb, ld, _ = qa.shape
    ql = min(ATT_QL, ld)
    n_res = min(dil, ATT_QL // ql)
    width = n_res * D_A
    cur = lambda b, r, i: (b, i, r)
    prev = lambda b, r, i: (b, jnp.maximum(i * (ql // QB) - 1, 0), r)
    ones_bd = np.zeros((4 * QB, LANES), np.float32)
    ones_bd[:2 * QB, :HD_A] = 1.0
    ones_bd[2 * QB:, HD_A:] = 1.0
    return pl.pallas_call(
        functools.partial(_attn_kernel, ql=ql),
        grid=(nb, dil // n_res, ld // ql),
        in_specs=[
            pl.BlockSpec((1, ql, width), cur),
            pl.BlockSpec((1, ql, width), cur),
            pl.BlockSpec((1, ql, width), cur),
            pl.BlockSpec((1, QB, width), prev),
            pl.BlockSpec((1, QB, width), prev),
            pl.BlockSpec((H_A, QB, 2 * QB), lambda b, r, i: (0, 0, 0)),
            pl.BlockSpec((4 * QB, LANES), lambda b, r, i: (0, 0)),
        ],
        out_specs=[pl.BlockSpec((1, ql, width), cur), pl.BlockSpec((1, ql, width), cur)],
        out_shape=[jax.ShapeDtypeStruct((nb, ld, dil * D_A), F32)] * 2,
        scratch_shapes=[pltpu.VMEM((QB + ql, width), BF16), pltpu.VMEM((QB + ql, width), BF16)],
        compiler_params=_params(("arbitrary", "arbitrary", "arbitrary"), vmem=40 * 1024 * 1024, fuse_inputs=7),
        name=f"attn_d{dil}",
    )(qa, ka, va, ka, va, bias, jnp.asarray(ones_bd, BF16))


def _heads_kernel(k_ref, v_ref, ko_ref, vo_ref):
    for src, dst in ((k_ref, ko_ref), (v_ref, vo_ref)):
        for c in range(D_A // LANES):
            xt = src[0, :, c * LANES:(c + 1) * LANES].T
            dst[0, 2 * c] = xt[:HD_A]
            dst[0, 2 * c + 1] = xt[HD_A:]


def _window_heads(ka, va, keep, tm=512):
    nb, seq, _ = ka.shape
    first = (seq - keep) // tm
    in_spec = pl.BlockSpec((1, tm, D_A), lambda b, i: (b, first + i, 0))
    out_spec = pl.BlockSpec((1, H_A, HD_A, tm), lambda b, i: (b, 0, 0, i))
    kt, vt = pl.pallas_call(
        _heads_kernel,
        grid=(nb, keep // tm),
        in_specs=[in_spec, in_spec],
        out_specs=[out_spec, out_spec],
        out_shape=[jax.ShapeDtypeStruct((nb, H_A, HD_A, keep), F32)] * 2,
        compiler_params=_params(("arbitrary", "arbitrary"), vmem=16 * 1024 * 1024, fuse_inputs=2),
        name="window_heads",
    )(ka, va)
    return kt.transpose(0, 3, 1, 2), vt.transpose(0, 3, 1, 2)


def _t5_bucket(dist):
    max_exact = N_BUCKETS // 2
    n = jnp.maximum(dist, 1).astype(F32)
    large = max_exact + (jnp.log(n / max_exact) / math.log(MAX_DISTANCE / max_exact)
                         * (N_BUCKETS - max_exact)).astype(jnp.int32)
    large = jnp.minimum(large, N_BUCKETS - 1)
    return jnp.where(dist < max_exact, dist, large)


def _branch_bias_vec(rel_bias, window, dil):
    return rel_bias[_t5_bucket(dil * jnp.arange(window // dil + 1))].astype(F32)


def _prompt_bias(rel_bias, window, dil):
    assert window // dil == QB
    n = 2 * QB
    vec = _branch_bias_vec(rel_bias, window, dil)
    vecp = jnp.concatenate([vec[::-1], jnp.full((n - QB - 1, H_A), NEG_INF, F32)], axis=0)
    g = jnp.roll(vecp, QB - 1, axis=0)
    skew = jnp.tile(g, (QB + 1, 1))[:QB * (n + 1)].reshape(QB, n + 1, H_A)[:, :n]
    return skew[::-1].transpose(2, 0, 1)


TOK_PAD = 8


CACHE_SLOTS = 3


def _sattn_kernel(q_ref, kt_hbm, vt_hbm, knt_ref, vnt_ref, bc_ref, bn_ref, att_ref, kbuf, vbuf, sem):
    b = pl.program_id(0)
    nb = pl.num_programs(0)

    def copies(entry, slot):
        return (pltpu.make_async_copy(kt_hbm.at[entry], kbuf.at[slot], sem.at[0, slot]),
                pltpu.make_async_copy(vt_hbm.at[entry], vbuf.at[slot], sem.at[1, slot]))

    def request(entry):
        for c in copies(entry, lax.rem(entry, CACHE_SLOTS)):
            c.start()

    @pl.when(b == 0)
    def _():
        for first in range(CACHE_SLOTS - 1):
            @pl.when(first < nb)
            def _():
                request(jnp.int32(first))

    ahead = b + (CACHE_SLOTS - 1)

    @pl.when(ahead < nb)
    def _():
        request(ahead)

    slot = lax.rem(b, CACHE_SLOTS)
    for c in copies(b, slot):
        c.wait()
    nt = (((1,), (1,)), ((), ()))
    for h in range(H_A):
        q = q_ref[0, h].astype(BF16)
        lc = jnp.dot(q, kbuf[slot, h].astype(BF16), preferred_element_type=F32)
        ln = jnp.dot(q, knt_ref[0, h].astype(BF16), preferred_element_type=F32)
        lcs = [lc + bc_ref[d, h] for d in range(len(BRANCHES))]
        lns = [ln + bn_ref[d, h] for d in range(len(BRANCHES))]
        m = functools.reduce(jnp.maximum, [jnp.max(x, axis=-1, keepdims=True) for x in lcs + lns])
        pc = functools.reduce(jnp.add, [jnp.exp(x - m) for x in lcs])
        pn = functools.reduce(jnp.add, [jnp.exp(x - m) for x in lns])
        s = jnp.sum(pc, axis=-1, keepdims=True) + jnp.sum(pn, axis=-1, keepdims=True)
        o = lax.dot_general(pc.astype(BF16), vbuf[slot, h].astype(BF16), nt, preferred_element_type=F32)
        o = o + lax.dot_general(pn.astype(BF16), vnt_ref[0, h].astype(BF16), nt, preferred_element_type=F32)
        att_ref[0, h] = o / s


def _sample_attention(q_h, kt, vt, knt, vnt, bias_c, bias_n):
    nb, _, _, n_past = kt.shape
    bat = lambda b: (b, 0, 0, 0)
    const = lambda b: (0, 0, 0, 0)
    return pl.pallas_call(
        _sattn_kernel,
        grid=(nb,),
        in_specs=[
            pl.BlockSpec((1, H_A, TOK_PAD, HD_A), bat),
            pl.BlockSpec(memory_space=pl.ANY),
            pl.BlockSpec(memory_space=pl.ANY),
            pl.BlockSpec((1, H_A, HD_A, LANES), bat),
            pl.BlockSpec((1, H_A, HD_A, LANES), bat),
            pl.BlockSpec(bias_c.shape, const),
            pl.BlockSpec(bias_n.shape, const),
        ],
        out_specs=pl.BlockSpec((1, H_A, TOK_PAD, HD_A), bat),
        out_shape=jax.ShapeDtypeStruct((nb, H_A, TOK_PAD, HD_A), F32),
        scratch_shapes=[pltpu.VMEM((CACHE_SLOTS, H_A, HD_A, n_past), F32),
                        pltpu.VMEM((CACHE_SLOTS, H_A, HD_A, n_past), F32),
                        pltpu.SemaphoreType.DMA((2, CACHE_SLOTS))],
        compiler_params=_params(("arbitrary",), vmem=36 * 1024 * 1024,
                                fuse_inputs=(True, False, False, True, True, True, True)),
        name="sample_attn",
    )(q_h, kt, vt, knt, vnt, bias_c, bias_n)


def _sample_bias(rel_bias, n_tok, n_past):
    ninf = lambda n: jnp.full((n, H_A), NEG_INF, F32)
    cached, new = [], []
    for window, dil in BRANCHES:
        nk = window // dil
        rows = n_past // dil
        assert n_past % dil == 0 and rows >= nk
        vec = _branch_bias_vec(rel_bias, window, dil)
        res = jnp.arange(dil)[None, :, None]
        tc, tn = [], []
        for t in range(n_tok):
            near = t // dil + 1
            col = jnp.concatenate([ninf(rows - (nk + 1 - near)), vec[near:nk + 1][::-1]], axis=0)
            tab = jnp.where(res == t % dil, col[:, None, :], NEG_INF).reshape(n_past, H_A)
            tc.append(tab)
            keys = [vec[(t - j) // dil][None] if (t - j) % dil == 0 and 0 <= t - j <= nk * dil else ninf(1)
                    for j in range(n_tok)]
            tn.append(jnp.concatenate(keys + [ninf(LANES - n_tok)], axis=0))
        pad = lambda tabs: jnp.stack(tabs + [jnp.zeros_like(tabs[0])] * (TOK_PAD - n_tok))
        cached.append(pad(tc).transpose(2, 0, 1))
        new.append(pad(tn).transpose(2, 0, 1))
    return jnp.stack(cached), jnp.stack(new)


def _postmix_kernel(mem_ref, o1_ref, l1_ref, o2_ref, l2_ref, o3_ref, l3_ref, x_ref,
                    g1_ref, sc2_ref, sh2_ref, gpost_ref, gpre_ref, wo_ref, wr_ref, br_ref, cnt0_ref,
                    x1_ref, h2_ref, idx_ref, gate_ref, cnt_ref, cnt_s, slab_s, *, dilations):
    first_step = jnp.logical_and(pl.program_id(0) == 0, pl.program_id(1) == 0)

    @pl.when(first_step)
    def _():
        cnt_s[...] = cnt0_ref[...]

    n_slab = D_A // LANES
    tm = x_ref.shape[1]

    def rows(ref, dil, slot):
        if dil == 1:
            return ref[0]
        for r in range(dil):
            for c in range(n_slab):
                lo = r * D_A + c * LANES
                slab_s[slot, c, pl.ds(r, tm // dil, stride=dil), :] = ref[0, :, lo:lo + LANES]
        return jnp.concatenate([slab_s[slot, c] for c in range(n_slab)], axis=1)

    o1, o2, o3 = (rows(r, d, n) for n, (r, d) in enumerate(zip((o1_ref, o2_ref, o3_ref), dilations)))
    l1, l2, l3 = (rows(r, d, 3 + n) for n, (r, d) in enumerate(zip((l1_ref, l2_ref, l3_ref), dilations)))
    lmax = jnp.maximum(jnp.maximum(l1, l2), l3)
    w1 = jnp.exp(l1 - lmax)
    w2 = jnp.exp(l2 - lmax)
    w3 = jnp.exp(l3 - lmax)
    att = (w1 * o1 + w2 * o2 + w3 * o3) / (w1 + w2 + w3)
    y = jnp.dot(mem_ref[0].astype(BF16), wo_ref[0:D_M, :], preferred_element_type=F32)
    y = y + jnp.dot(att.astype(BF16), wo_ref[D_M:, :], preferred_element_type=F32)
    x1 = x_ref[0] + g1_ref[0] * _rms(y, gpost_ref[...])
    x1_ref[0] = x1
    h2 = _rms(x1, gpre_ref[...]) * (1.0 + sc2_ref[0]) + sh2_ref[0]
    h2_ref[0] = h2
    h_hi = h2.astype(BF16)
    h_lo = (h2 - h_hi.astype(F32)).astype(BF16)
    w_hi = wr_ref[0]
    logits = jnp.dot(h_hi, w_hi, preferred_element_type=F32)
    logits = logits + jnp.dot(h_lo, w_hi, preferred_element_type=F32)
    logits = logits + jnp.dot(h_hi, wr_ref[1], preferred_element_type=F32) + br_ref[...]
    lane = lax.broadcasted_iota(jnp.int32, logits.shape, 1)
    lane_f = lane.astype(F32)
    logits = jnp.where(lane < N_EXPERTS, logits, NEG_INF)
    vals, idxs = [], []
    for _ in range(TOP_K):
        mx = jnp.max(logits, axis=-1, keepdims=True)
        ix = jnp.min(jnp.where(logits == mx, lane_f, float(LANES)), axis=-1, keepdims=True)
        vals.append(mx)
        idxs.append(ix)
        logits = jnp.where(lane_f == ix, NEG_INF, logits)
    es = [jnp.exp(v - vals[0]) for v in vals]
    tot = es[0] + es[1] + es[2] + es[3]
    tm = logits.shape[0]
    picks = [(lane_f == ix).astype(F32) for ix in idxs]
    chosen = picks[0] + picks[1] + picks[2] + picks[3]
    r_i = lax.broadcasted_iota(jnp.int32, (tm, tm), 0)
    c_i = lax.broadcasted_iota(jnp.int32, (tm, tm), 1)
    earlier = jnp.dot((c_i < r_i).astype(BF16), chosen.astype(BF16), preferred_element_type=F32)
    before = earlier + cnt_s[...]
    cnt_s[...] = cnt_s[...] + jnp.sum(chosen, axis=0, keepdims=True)
    cnt_ref[...] = cnt_s[...]
    idx_tile = jnp.zeros(logits.shape, F32)
    gate_tile = jnp.zeros(logits.shape, F32)
    for k in range(TOP_K):
        rank = jnp.sum(picks[k] * before, axis=-1, keepdims=True)
        idx_tile = jnp.where(lane == k, idxs[k], idx_tile)
        idx_tile = jnp.where(lane == TOP_K + k, rank, idx_tile)
        gate_tile = jnp.where(lane == k, es[k] / tot, gate_tile)
    idx_ref[0] = idx_tile.astype(jnp.int32)
    gate_ref[0] = gate_tile


def _postmix(mem, branches, x3, gate1, scale2, shift2, g_post, g_pre, w_out_b, w_router_p, b_router_p, counts0, tm,
             dilations=(1, 1, 1)):
    nb, rows, _ = x3.shape
    per_row_mod = gate1.shape[1] != 1
    mod_block = (1, tm, D_MODEL) if per_row_mod else (1, 1, D_MODEL)
    mod_map = (lambda b, i: (b, i, 0)) if per_row_mod else (lambda b, i: (b, 0, 0))
    row_map = lambda b, i: (b, i, 0)
    const2 = lambda b, i: (0, 0)
    half = pl.BlockSpec((1, tm, D_M), row_map)
    flat = [a for pair in branches for a in pair]
    branch_specs = [pl.BlockSpec((1, tm // d, d * D_A), row_map) for d in dilations for _ in range(2)]
    return pl.pallas_call(
        functools.partial(_postmix_kernel, dilations=tuple(dilations)),
        grid=(nb, rows // tm),
        in_specs=[half] + branch_specs + [
            pl.BlockSpec((1, tm, D_MODEL), row_map),
            pl.BlockSpec(mod_block, mod_map),
            pl.BlockSpec(mod_block, mod_map),
            pl.BlockSpec(mod_block, mod_map),
            pl.BlockSpec((1, D_MODEL), const2),
            pl.BlockSpec((1, D_MODEL), const2),
            pl.BlockSpec((D_MODEL, D_MODEL), const2),
            pl.BlockSpec((2, D_MODEL, LANES), lambda b, i: (0, 0, 0)),
            pl.BlockSpec((1, LANES), const2),
            pl.BlockSpec((1, LANES), const2),
        ],
        out_specs=[
            pl.BlockSpec((1, tm, D_MODEL), row_map),
            pl.BlockSpec((1, tm, D_MODEL), row_map),
            pl.BlockSpec((1, tm, LANES), row_map),
            pl.BlockSpec((1, tm, LANES), row_map),
            pl.BlockSpec((1, LANES), const2),
        ],
        out_shape=[
            jax.ShapeDtypeStruct((nb, rows, D_MODEL), F32),
            jax.ShapeDtypeStruct((nb, rows, D_MODEL), F32),
            jax.ShapeDtypeStruct((nb, rows, LANES), jnp.int32),
            jax.ShapeDtypeStruct((nb, rows, LANES), F32),
            jax.ShapeDtypeStruct((1, LANES), F32),
        ],
        scratch_shapes=[pltpu.VMEM((1, LANES), F32), pltpu.VMEM((6, D_A // LANES, tm, LANES), F32)],
        compiler_params=_params(("arbitrary", "arbitrary"), fuse_inputs=17),
        name="postmix",
    )(mem, *flat, x3, gate1, scale2, shift2, g_post.reshape(1, D_MODEL), g_pre.reshape(1, D_MODEL),
      w_out_b, w_router_p, b_router_p, counts0)


def _row_wait(src_rows, dst_rows, sem):
    pltpu.make_async_copy(src_rows, dst_rows, sem).wait()


def _scatter_kernel(slot_ref, hp_ref, hs_ref, xs_hbm, buf, sem, *, n_prompt_tiles):
    h = jnp.where(pl.program_id(0) >= n_prompt_tiles, hs_ref[...], hp_ref[...])
    bits = pltpu.bitcast(h.astype(BF16).astype(F32), U32)
    packed = (bits[:, :PACK_W] & U32(HI_MASK)) | (bits[:, PACK_W:] >> 16)
    buf[...] = pltpu.einshape("m(jc)->mjc", packed, j=PACK_T)

    def issue(r, carry):
        src = buf.at[pl.ds(r, 1)]
        for k in range(TOP_K):
            pltpu.make_async_copy(src, xs_hbm.at[pl.ds(slot_ref[0, 0, r * TOP_K + k], 1)],
                                  sem).start(priority=k % 2)
        return carry

    lax.fori_loop(0, TOK_TILE, issue, 0, unroll=4)
    for _ in range(TOP_K):
        _row_wait(buf, xs_hbm.at[pl.ds(0, TOK_TILE)], sem)


def _moe_scatter(h_p, h_s, slot_tiles):
    tiles = slot_tiles.shape[0]
    n_p = h_p.shape[0] // TOK_TILE
    assert tiles == n_p + 1 and h_s.shape[0] == TOK_TILE
    n_rows = tiles * TOK_TILE * TOP_K
    return pl.pallas_call(
        functools.partial(_scatter_kernel, n_prompt_tiles=n_p),
        grid=(tiles,),
        in_specs=[
            pl.BlockSpec((1, 1, TOK_TILE * TOP_K), lambda i: (i, 0, 0), memory_space=pltpu.SMEM),
            pl.BlockSpec((TOK_TILE, D_MODEL), lambda i: (jnp.minimum(i, n_p - 1), 0)),
            pl.BlockSpec((TOK_TILE, D_MODEL), lambda i: (0, 0)),
        ],
        out_specs=pl.BlockSpec(memory_space=pl.ANY),
        out_shape=jax.ShapeDtypeStruct((n_rows, PACK_T, LANES), U32),
        scratch_shapes=[pltpu.VMEM((TOK_TILE, PACK_T, LANES), U32), pltpu.SemaphoreType.DMA],
        compiler_params=_params(("arbitrary",), vmem=8 * 1024 * 1024),
        name="moe_scatter",
    )(slot_tiles, h_p, h_s)


def _expert_kernel(tile_ref, exp_ref, next_ref, lo_ref, hi_ref, first_ref, nvis_ref,
                   xs_ref, w1_hbm, b1_ref, w2_hbm, b2_ref, ys_ref, w1f, w2f, w1b, w2b, sem):
    v = pl.program_id(0)
    active = v < nvis_ref[0]
    changed = jnp.logical_or(v == 0, exp_ref[v] != exp_ref[jnp.maximum(v - 1, 0)])
    lo = lo_ref[v]
    hi = hi_ref[v]

    def weight_copies(e):
        return (pltpu.make_async_copy(w1_hbm.at[e], w1f, sem.at[0]),
                pltpu.make_async_copy(w2_hbm.at[e], w2f, sem.at[1]))

    @pl.when(v == 0)
    def _():
        for c in weight_copies(exp_ref[0]):
            c.start()

    @pl.when(jnp.logical_and(active, changed))
    def _():
        for c in weight_copies(exp_ref[v]):
            c.wait()
        w1b[...] = w1f[...].astype(BF16)
        w2b[...] = w2f[...].astype(BF16)

        @pl.when(next_ref[v] >= 0)
        def _():
            for c in weight_copies(next_ref[v]):
                c.start()

    def mlp_rows(r0):
        w = pltpu.einshape("mjc->m(jc)", xs_ref[r0:r0 + MOE_SUB])
        x_hi = pltpu.bitcast(w & U32(HI_MASK), F32).astype(BF16)
        x_lo = pltpu.bitcast(w << 16, F32).astype(BF16)
        hgu = jnp.dot(x_hi, w1b[0:PACK_W, :], preferred_element_type=F32)
        hgu = hgu + jnp.dot(x_lo, w1b[PACK_W:, :], preferred_element_type=F32) + b1_ref[0]
        x_glu = jnp.minimum(hgu[:, :D_FF], SWIGLU_LIMIT)
        x_lin = jnp.clip(hgu[:, D_FF:], -SWIGLU_LIMIT, SWIGLU_LIMIT)
        act = x_glu * jax.nn.sigmoid(SWIGLU_ALPHA * x_glu) * (x_lin + 1.0)
        y = jnp.dot(act.astype(BF16), w2b[...], preferred_element_type=F32) + b2_ref[0]
        return pltpu.einshape("m(jc)->mjc", y, j=ROW_T)

    subs = [r * MOE_SUB for r in range(MOE_M // MOE_SUB)]
    whole = jnp.logical_and(lo == 0, hi == MOE_M)

    @pl.when(jnp.logical_and(active, whole))
    def _():
        for r0 in subs:
            ys_ref[r0:r0 + MOE_SUB] = mlp_rows(r0)

    shared = jnp.logical_and(active, jnp.logical_not(whole))

    @pl.when(jnp.logical_and(shared, first_ref[v] == 1))
    def _():
        ys_ref[...] = jnp.zeros(ys_ref.shape, F32)

    for r0 in subs:
        @pl.when(jnp.logical_and(shared, jnp.logical_and(lo < r0 + MOE_SUB, hi > r0)))
        def _():
            rows = r0 + lax.broadcasted_iota(jnp.int32, (MOE_SUB, 1, 1), 0)
            mine = jnp.logical_and(rows >= lo, rows < hi)
            ys_ref[r0:r0 + MOE_SUB] = jnp.where(mine, mlp_rows(r0), ys_ref[r0:r0 + MOE_SUB])


def _moe_experts(xs, visits, w1, b1, w2, b2):
    n_rows = xs.shape[0]
    n_vis = visits[0].shape[0]
    tile_map = lambda v, tile, *_: (tile[v], 0, 0)
    exp_map = lambda v, tile, exp, *_: (exp[v], 0, 0)
    return pl.pallas_call(
        _expert_kernel,
        grid_spec=pltpu.PrefetchScalarGridSpec(
            num_scalar_prefetch=7,
            grid=(n_vis,),
            in_specs=[
                pl.BlockSpec((MOE_M, PACK_T, LANES), tile_map),
                pl.BlockSpec(memory_space=pl.ANY),
                pl.BlockSpec((1, 1, 2 * D_FF), exp_map),
                pl.BlockSpec(memory_space=pl.ANY),
                pl.BlockSpec((1, 1, D_MODEL), exp_map),
            ],
            out_specs=pl.BlockSpec((MOE_M, ROW_T, LANES), tile_map),
            scratch_shapes=[pltpu.VMEM((D_MODEL, 2 * D_FF), F32), pltpu.VMEM((D_FF, D_MODEL), F32),
                            pltpu.VMEM((D_MODEL, 2 * D_FF), BF16), pltpu.VMEM((D_FF, D_MODEL), BF16),
                            pltpu.SemaphoreType.DMA((2,))],
        ),
        out_shape=jax.ShapeDtypeStruct((n_rows, ROW_T, LANES), F32),
        compiler_params=_params(("arbitrary",), vmem=44 * 1024 * 1024),
        name="moe_experts",
    )(*visits, xs, w1, b1.reshape(N_EXPERTS, 1, 2 * D_FF), w2, b2.reshape(N_EXPERTS, 1, D_MODEL))


def _combine_kernel(slot_ref, next_ref, gatep_ref, gates_ref, x1p_ref, x1s_ref, g2p_ref, g2s_ref, gpost_ref, ys_hbm,
                    yp_ref, ysm_ref, buf, sem, *, n_prompt_tiles):
    i = pl.program_id(0)
    last = pl.num_programs(0) - 1
    cur = lax.rem(i, 2)

    def gather(slots, b):
        def issue(r, carry):
            for k in range(TOP_K):
                pltpu.make_async_copy(ys_hbm.at[pl.ds(slots[0, 0, r * TOP_K + k], 1)],
                                      buf.at[b, pl.ds(k * TOK_TILE + r, 1)], sem.at[b]).start(priority=k % 2)
            return carry

        lax.fori_loop(0, TOK_TILE, issue, 0, unroll=4)

    @pl.when(i == 0)
    def _():
        gather(slot_ref, cur)

    @pl.when(i < last)
    def _():
        gather(next_ref, 1 - cur)

    for k in range(TOP_K):
        _row_wait(ys_hbm.at[pl.ds(0, TOK_TILE)], buf.at[cur, pl.ds(k * TOK_TILE, TOK_TILE)], sem.at[cur])
    is_sample = i >= n_prompt_tiles
    g = jnp.where(is_sample, gates_ref[...], gatep_ref[...])
    f = jnp.zeros((TOK_TILE, D_MODEL), F32)
    for k in range(TOP_K):
        rows = pltpu.einshape("mjc->m(jc)", buf[cur, pl.ds(k * TOK_TILE, TOK_TILE)])
        f = f + rows * g[:, k:k + 1]
    x1 = jnp.where(is_sample, x1s_ref[...], x1p_ref[...])
    g2 = jnp.where(is_sample, g2s_ref[...], g2p_ref[0])
    y = x1 + g2 * _rms(f, gpost_ref[...])

    @pl.when(jnp.logical_not(is_sample))
    def _():
        yp_ref[...] = y

    @pl.when(is_sample)
    def _():
        ysm_ref[...] = y


def _moe_combine(ys, slot_tiles, gates_p, gates_s, x1_p, x1_s, gate2_p, gate2_s, g_post, rows_per_batch):
    tiles = slot_tiles.shape[0]
    n_p = x1_p.shape[0] // TOK_TILE
    assert tiles == n_p + 1 and x1_s.shape[0] == TOK_TILE
    tiles_per_batch = rows_per_batch // TOK_TILE
    p_tile = lambda i: (jnp.minimum(i, n_p - 1), 0)
    const2 = lambda i: (0, 0)
    return pl.pallas_call(
        functools.partial(_combine_kernel, n_prompt_tiles=n_p),
        grid=(tiles,),
        in_specs=[
            pl.BlockSpec((1, 1, TOK_TILE * TOP_K), lambda i: (i, 0, 0), memory_space=pltpu.SMEM),
            pl.BlockSpec((1, 1, TOK_TILE * TOP_K), lambda i: (jnp.minimum(i + 1, tiles - 1), 0, 0),
                         memory_space=pltpu.SMEM),
            pl.BlockSpec((TOK_TILE, LANES), p_tile),
            pl.BlockSpec((TOK_TILE, LANES), const2),
            pl.BlockSpec((TOK_TILE, D_MODEL), p_tile),
            pl.BlockSpec((TOK_TILE, D_MODEL), const2),
            pl.BlockSpec((1, 1, D_MODEL), lambda i: (jnp.minimum(i, n_p - 1) // tiles_per_batch, 0, 0)),
            pl.BlockSpec((TOK_TILE, D_MODEL), const2),
            pl.BlockSpec((1, D_MODEL), const2),
            pl.BlockSpec(memory_space=pl.ANY),
        ],
        out_specs=[pl.BlockSpec((TOK_TILE, D_MODEL), p_tile), pl.BlockSpec((TOK_TILE, D_MODEL), const2)],
        out_shape=[jax.ShapeDtypeStruct(x1_p.shape, F32), jax.ShapeDtypeStruct(x1_s.shape, F32)],
        scratch_shapes=[pltpu.VMEM((2, TOP_K * TOK_TILE, ROW_T, LANES), F32), pltpu.SemaphoreType.DMA((2,))],
        compiler_params=_params(("arbitrary",), vmem=16 * 1024 * 1024),
        name="moe_combine",
    )(slot_tiles, slot_tiles, gates_p, gates_s, x1_p, x1_s, gate2_p, gate2_s, g_post.reshape(1, D_MODEL), ys)


def _moe_plan(routes, counts):
    a = sum(r.shape[0] for r in routes) * TOP_K
    assert a % MOE_M == 0
    g_end = jnp.cumsum(counts)
    g_start = g_end - counts

    def slots(route):
        flat_e = route[:, :TOP_K].reshape(-1)
        rank = route[:, TOP_K:2 * TOP_K].reshape(-1)
        onehot = flat_e[:, None] == jnp.arange(N_EXPERTS)[None, :]
        return (jnp.sum(jnp.where(onehot, g_start[None, :], 0), axis=1) + rank).astype(jnp.int32)

    slot_of = jnp.concatenate([slots(r) for r in routes])

    n_tiles = a // MOE_M
    n_vis = n_tiles + N_EXPERTS - 1
    first_tile = g_start // MOE_M
    last_tile = jnp.where(counts > 0, (g_end - 1) // MOE_M, first_tile)
    nvis_e = jnp.where(counts > 0, last_tile - first_tile + 1, 0)
    v_end = jnp.cumsum(nvis_e)
    v_start = v_end - nvis_e
    total = v_end[-1]
    v = jnp.arange(n_vis)
    vc = jnp.minimum(v, total - 1)
    e_of = jnp.minimum(jnp.sum((vc[:, None] >= v_end[None, :]).astype(jnp.int32), axis=1), N_EXPERTS - 1)
    is_e = e_of[:, None] == jnp.arange(N_EXPERTS)[None, :]
    of_e = lambda t: jnp.sum(jnp.where(is_e, t[None, :], 0), axis=1)
    tile_of = of_e(first_tile) + (vc - of_e(v_start))
    lo = jnp.maximum(of_e(g_start), tile_of * MOE_M) - tile_of * MOE_M
    hi = jnp.minimum(of_e(g_end), (tile_of + 1) * MOE_M) - tile_of * MOE_M
    prev_tile = jnp.concatenate([jnp.full((1,), -1, tile_of.dtype), tile_of[:-1]])
    first = (tile_of != prev_tile).astype(jnp.int32)
    experts = jnp.arange(N_EXPERTS)
    later = (experts[None, :] > experts[:, None]) & (counts[None, :] > 0)
    next_e = jnp.min(jnp.where(later, experts[None, :], N_EXPERTS), axis=1)
    next_e = jnp.where(next_e < N_EXPERTS, next_e, -1)
    i32 = lambda t: t.astype(jnp.int32)
    visits = (i32(tile_of), i32(e_of), i32(of_e(next_e)), i32(lo), i32(hi), first, i32(total).reshape(1))
    return slot_of, visits


def _reorder_w_in(w_in):
    off_om_end = 4 * D_M
    off_g_end = off_om_end + 2 * H_M
    w = jnp.concatenate([
        w_in[:, :off_om_end],
        w_in[:, off_g_end:],
        w_in[:, off_om_end:off_g_end],
        jnp.zeros((D_MODEL, LANES - 2 * H_M), w_in.dtype),
    ], axis=1)
    return w.astype(BF16)


def kernel(x_prompt, x_sample, cache_k, cache_v, state_conv, state_C, state_n, state_m, c_prompt, c_sample, w_ada, b_ada, g_pre_mix, g_post_mix, g_pre_ffn, g_post_ffn, w_in, b_ig, b_fg, conv_w, conv_b, mh_norm, rel_bias, w_out, w_router, b_router, w1, b1, w2, b2):
    depth = w_in.shape[0]
    assert depth == 1
    l = 0
    nb_p, seq, _ = x_prompt.shape
    nb_s, n_tok, _ = x_sample.shape
    rows_s = nb_s * n_tok
    assert rows_s == TOK_TILE

    n_c = nb_p + nb_s
    c_pad = -(-n_c // 8) * 8
    c_all = jnp.concatenate([c_prompt, c_sample, jnp.zeros((c_pad - n_c, D_MODEL), F32)], axis=0)
    mod = _adaln(c_all, w_ada[l], b_ada[l])
    mods = [mod[:, i * D_MODEL:(i + 1) * D_MODEL] for i in range(6)]
    mods_p = [m[:nb_p].reshape(nb_p, 1, D_MODEL) for m in mods]
    mods_s = [jnp.repeat(m[nb_p:n_c], n_tok, axis=0).reshape(1, rows_s, D_MODEL) for m in mods]

    w_r = _reorder_w_in(w_in[l])
    bias_g = jnp.concatenate([b_ig[l], b_fg[l], jnp.zeros((LANES - 2 * H_M,), F32)]).reshape(1, LANES)
    w_out_b = w_out[l].astype(BF16)
    w_router_f = jnp.concatenate([w_router[l], jnp.zeros((D_MODEL, LANES - N_EXPERTS), F32)], axis=1)
    w_router_hi = w_router_f.astype(BF16)
    w_router_p = jnp.stack([w_router_hi, (w_router_f - w_router_hi.astype(F32)).astype(BF16)])
    b_router_p = jnp.concatenate([b_router[l].astype(F32), jnp.zeros((LANES - N_EXPERTS,), F32)]).reshape(1, LANES)

    shift1, scale1, gate1, shift2, scale2, gate2_p = mods_p
    dils = tuple(d for _, d in BRANCHES)
    assert dils[0] == 1
    u, v_m, og, gts, qa, ka, va, *dilated = _inproj(x_prompt, g_pre_mix[l], scale1, shift1, w_r, bias_g, tm=512,
                                                   dilations=dils[1:])
    qkv_d = [(qa, ka, va)] + [tuple(dilated[3 * n:3 * n + 3]) for n in range(len(dils) - 1)]
    gates_row = gts[:, :, :8].transpose(0, 2, 1)
    zeros = lambda *s: jnp.zeros(s, F32)
    mem, c_p, n_p, m_p = _mlstm(u, v_m, og, gts, gates_row, zeros(nb_p, 8, 2 * D_M),
                                zeros(nb_p, H_M, DK_M, DK_M), zeros(nb_p, H_M, DK_M), zeros(nb_p, H_M, LANES),
                                conv_w[l], conv_b[l], mh_norm[l], MLSTM_L_PROMPT)
    branches = [_attn_branch(*qkv, _prompt_bias(rel_bias, w, d), d) for qkv, (w, d) in zip(qkv_d, BRANCHES)]
    x1_p, h2_p, idx_p, gate_p, cnt_p = _postmix(mem, branches, x_prompt, gate1, scale2, shift2,
                                                g_post_mix[l], g_pre_ffn[l], w_out_b, w_router_p, b_router_p,
                                                zeros(1, LANES), tm=512, dilations=dils)
    keep = min(WINDOW_MAX, seq)
    k_p, v_p = _window_heads(ka, va, keep)
    conv_p = u[:, seq - (CONV_W - 1):]

    shift1, scale1, gate1, shift2, scale2, gate2_s = mods_s
    x_s = x_sample.reshape(1, rows_s, D_MODEL)
    u_s, v_s, og_s, gts_s, qa_s, ka_s, va_s = _inproj(x_s, g_pre_mix[l], scale1, shift1, w_r, bias_g, tm=rows_s)
    per_b = lambda t: t.reshape(nb_s, n_tok, t.shape[-1])
    ls = MLSTM_L_SAMPLE
    pad_rows = lambda t, n: jnp.concatenate([t, jnp.zeros((nb_s, n - t.shape[1], t.shape[2]), t.dtype)], axis=1)
    inert = jnp.concatenate([jnp.full((H_M,), NEG_INF, F32), jnp.zeros((LANES - H_M,), F32)])
    gts_pad = jnp.concatenate([per_b(gts_s), jnp.broadcast_to(inert, (nb_s, ls - n_tok, LANES))], axis=1)
    tail0 = jnp.concatenate([zeros(nb_s, 8 - (CONV_W - 1), 2 * D_M), state_conv[l].astype(F32)], axis=1)
    m0x = jnp.broadcast_to(state_m[l].astype(F32)[:, :, None], (nb_s, H_M, LANES))
    mem_s, c_s, n_s, m_s = _mlstm(pad_rows(per_b(u_s), ls), pad_rows(per_b(v_s), ls), pad_rows(per_b(og_s), ls),
                                  gts_pad, gts_pad[:, :, :8].transpose(0, 2, 1), tail0,
                                  state_C[l].astype(F32), state_n[l].astype(F32), m0x,
                                  conv_w[l], conv_b[l], mh_norm[l], ls)
    mem_s = mem_s[:, :n_tok].reshape(1, rows_s, D_M)
    heads = lambda t: t.reshape(nb_s, n_tok, H_A, HD_A)
    k_new, v_new = heads(ka_s), heads(va_s)
    n_past = cache_k.shape[2]
    assert n_tok <= TOK_PAD
    q_h = jnp.pad(heads(qa_s).transpose(0, 2, 1, 3), ((0, 0), (0, 0), (0, TOK_PAD - n_tok), (0, 0)))
    new_t = lambda t: jnp.pad(t.transpose(0, 2, 3, 1), ((0, 0), (0, 0), (0, 0), (0, LANES - n_tok)))
    cache_t = lambda c: c[l].astype(F32).transpose(0, 2, 3, 1)
    att_s = _sample_attention(q_h, cache_t(cache_k), cache_t(cache_v), new_t(k_new), new_t(v_new),
                              *_sample_bias(rel_bias, n_tok, n_past))
    att_s = att_s[:, :, :n_tok].transpose(0, 2, 1, 3).reshape(1, rows_s, D_A)
    zero_o = zeros(1, rows_s, D_A)
    ninf = jnp.full((1, rows_s, D_A), NEG_INF, F32)
    branches_s = [(att_s, zero_o), (zero_o, ninf), (zero_o, ninf)]
    x1_s, h2_s, idx_s, gate_s, cnt_all = _postmix(mem_s, branches_s, x_s, gate1, scale2, shift2,
                                                  g_post_mix[l], g_pre_ffn[l], w_out_b, w_router_p, b_router_p,
                                                  cnt_p, tm=rows_s)

    t_p = nb_p * seq
    slot_of, visits = _moe_plan([idx_p.reshape(t_p, LANES), idx_s.reshape(rows_s, LANES)],
                                cnt_all[0, :N_EXPERTS].astype(jnp.int32))
    slot_tiles = slot_of.reshape((t_p + rows_s) // TOK_TILE, 1, TOK_TILE * TOP_K)
    xs = _moe_scatter(h2_p.reshape(t_p, D_MODEL), h2_s.reshape(rows_s, D_MODEL), slot_tiles)
    ys = _moe_experts(xs, visits, w1[l], b1[l], w2[l], b2[l])
    y_p, y_s = _moe_combine(ys, slot_tiles, gate_p.reshape(t_p, LANES), gate_s.reshape(rows_s, LANES), x1_p.reshape(t_p, D_MODEL), x1_s.reshape(rows_s, D_MODEL),
                            gate2_p, gate2_s.reshape(rows_s, D_MODEL), g_post_ffn[l], seq)
    y_prompt = y_p.reshape(nb_p, seq, D_MODEL)
    y_sample = y_s.reshape(nb_s, n_tok, D_MODEL)

    st = lambda t: t[None]
    conv_s = jnp.concatenate([state_conv[l].astype(F32), per_b(u_s)], axis=1)[:, -(CONV_W - 1):]
    return (y_prompt, y_sample, st(k_p), st(v_p), st(conv_p), st(c_p), st(n_p), st(m_p[:, :, 0]),
            st(k_new), st(v_new), st(conv_s), st(c_s), st(n_s), st(m_s[:, :, 0]))
```

```python
import functools
import math

import jax
import jax.numpy as jnp
import numpy as np
from jax import lax
from jax.experimental import pallas as pl
from jax.experimental.pallas import tpu as pltpu

F32 = jnp.float32
BF16 = jnp.bfloat16
U32 = jnp.uint32
NEG_INF = float("-inf")

D_MODEL = 1024
H_M = 4
D_M = 512
DK_M = 128
CONV_W = 4
H_A = 8
D_A = 512
HD_A = 64
BRANCHES = ((128, 1), (512, 4), (2048, 16))
WINDOW_MAX = 2048
QB = 128
N_BUCKETS = 32
MAX_DISTANCE = 2048
N_EXPERTS = 32
TOP_K = 4
D_FF = 1024
SWIGLU_LIMIT = 7.0
SWIGLU_ALPHA = 1.702
EPS = 1e-6

LANES = 128
C_U, C_V, C_O, C_QA, C_KA, C_VA, C_G = 0, 1024, 1536, 2048, 2560, 3072, 3584
D_IN_R = C_G + LANES

MLSTM_L_PROMPT = 256
MLSTM_L_SAMPLE = 32
ATT_QL = 1024
MOE_M = 512
MOE_SUB = 256
TOK_TILE = 128
PACK_W = D_MODEL // 2
PACK_T = PACK_W // LANES
ROW_T = D_MODEL // LANES
HI_MASK = 0xFFFF0000
VMEM_LIMIT = 56 * 1024 * 1024


def _params(sem, vmem=None, fuse_inputs=0):
    return pltpu.CompilerParams(dimension_semantics=sem, vmem_limit_bytes=vmem or VMEM_LIMIT,
                                allow_input_fusion=[True] * fuse_inputs if fuse_inputs else None)


def _rms(x, g):
    return x * lax.rsqrt(jnp.mean(x * x, axis=-1, keepdims=True) + EPS) * g


def _log_sigmoid(t):
    return jnp.minimum(t, 0.0) - jnp.log1p(jnp.exp(-jnp.abs(t)))


def _ada_kernel(c_ref, w_ref, b_ref, o_ref):
    c = c_ref[...]
    a = (c * jax.nn.sigmoid(c)).astype(BF16)
    o_ref[...] = jnp.dot(a, w_ref[...].astype(BF16), preferred_element_type=F32) + b_ref[...]


def _adaln(c_all, w_ada, b_ada):
    rows = c_all.shape[0]
    n = w_ada.shape[1]
    tn = 1024
    return pl.pallas_call(
        _ada_kernel,
        grid=(n // tn,),
        in_specs=[
            pl.BlockSpec((rows, D_MODEL), lambda j: (0, 0)),
            pl.BlockSpec((D_MODEL, tn), lambda j: (0, j)),
            pl.BlockSpec((1, tn), lambda j: (0, j)),
        ],
        out_specs=pl.BlockSpec((rows, tn), lambda j: (0, j)),
        out_shape=jax.ShapeDtypeStruct((rows, n), F32),
        compiler_params=_params(("arbitrary",), vmem=16 * 1024 * 1024, fuse_inputs=3),
        name="adaln",
    )(c_all, w_ada, b_ada.reshape(1, n))


def _inproj_kernel(x_ref, g_ref, sc_ref, sh_ref, w_ref, bg_ref,
                   u_ref, v_ref, o_ref, gt_ref, qa_ref, ka_ref, va_ref, *rest, dilations):
    dil_refs, slab_s = (rest[:-1], rest[-1]) if dilations else (rest, None)
    x = x_ref[0]
    h = _rms(x, g_ref[...]) * (1.0 + sc_ref[0]) + sh_ref[0]
    hb = h.astype(BF16)

    def seg(lo, width):
        return jnp.dot(hb, w_ref[:, lo:lo + width], preferred_element_type=F32)

    u_ref[0] = seg(C_U, 2 * D_M)
    v_ref[0] = seg(C_V, D_M)
    o_ref[0] = jax.nn.sigmoid(seg(C_O, D_M))
    qkv = (seg(C_QA, D_A) * (HD_A ** -0.5), seg(C_KA, D_A), seg(C_VA, D_A))
    for ref, val in zip((qa_ref, ka_ref, va_ref), qkv):
        ref[0] = val
    n_slab = D_A // LANES
    rows = x.shape[0]
    for j, val in enumerate(qkv if dilations else ()):
        for c in range(n_slab):
            slab_s[j, c] = val[:, c * LANES:(c + 1) * LANES]
        for n, dil in enumerate(dilations):
            for r in range(dil):
                for c in range(n_slab):
                    lo = r * D_A + c * LANES
                    dil_refs[3 * n + j][0, :, lo:lo + LANES] = slab_s[j, c, pl.ds(r, rows // dil, stride=dil), :]
    t = seg(C_G, LANES) + bg_ref[...]
    lane = lax.broadcasted_iota(jnp.int32, t.shape, 1)
    gt_ref[0] = jnp.where(lane < H_M, t, _log_sigmoid(t))


def _inproj(x3, g_pre, scale, shift, w_r, bias_g, tm, dilations=()):
    nb, rows, _ = x3.shape
    per_row_mod = scale.shape[1] != 1
    mod_block = (1, tm, D_MODEL) if per_row_mod else (1, 1, D_MODEL)
    mod_map = (lambda b, i: (b, i, 0)) if per_row_mod else (lambda b, i: (b, 0, 0))
    row_map = lambda b, i: (b, i, 0)
    widths = (2 * D_M, D_M, D_M, LANES, D_A, D_A, D_A)
    dil_specs = [pl.BlockSpec((1, tm // d, d * D_A), row_map) for d in dilations for _ in range(3)]
    dil_shapes = [jax.ShapeDtypeStruct((nb, rows // d, d * D_A), F32) for d in dilations for _ in range(3)]
    return pl.pallas_call(
        functools.partial(_inproj_kernel, dilations=tuple(dilations)),
        grid=(nb, rows // tm),
        in_specs=[
            pl.BlockSpec((1, tm, D_MODEL), row_map),
            pl.BlockSpec((1, D_MODEL), lambda b, i: (0, 0)),
            pl.BlockSpec(mod_block, mod_map),
            pl.BlockSpec(mod_block, mod_map),
            pl.BlockSpec((D_MODEL, D_IN_R), lambda b, i: (0, 0)),
            pl.BlockSpec((1, LANES), lambda b, i: (0, 0)),
        ],
        out_specs=[pl.BlockSpec((1, tm, w), row_map) for w in widths] + dil_specs,
        out_shape=[jax.ShapeDtypeStruct((nb, rows, w), F32) for w in widths] + dil_shapes,
        scratch_shapes=[pltpu.VMEM((3, D_A // LANES, tm, LANES), F32)] if dilations else [],
        compiler_params=_params(("arbitrary", "arbitrary"), fuse_inputs=6),
        name="inproj",
    )(x3, g_pre.reshape(1, D_MODEL), scale, shift, w_r, bias_g)


def _mlstm_kernel(u_ref, v_ref, og_ref, gc_ref, gr_ref, tail0_ref, c0_ref, n0_ref, m0_ref,
                  cw_ref, cb_ref, mh_ref,
                  mem_ref, c_out_ref, n_out_ref, m_out_ref,
                  c_s, n_s, m_s, tail_s, uext_s, *, chunk):
    L = chunk
    step = pl.program_id(1)

    @pl.when(step == 0)
    def _():
        c_s[...] = c0_ref[0]
        n_s[...] = n0_ref[0]
        m_s[...] = m0_ref[0]
        tail_s[...] = tail0_ref[0]

    u = u_ref[0]
    uext_s[0:8, :] = tail_s[...]
    uext_s[8:8 + L, :] = u
    acc = cb_ref[...] + uext_s[pl.ds(5, L), :] * cw_ref[0:1, :]
    acc = acc + uext_s[pl.ds(6, L), :] * cw_ref[1:2, :]
    acc = acc + uext_s[pl.ds(7, L), :] * cw_ref[2:3, :]
    acc = acc + u * cw_ref[3:4, :]
    tail_s[...] = uext_s[pl.ds(L, 8), :]
    qk = acc * jax.nn.sigmoid(acc)

    gc = gc_ref[0]
    gr = gr_ref[0]
    row = lax.broadcasted_iota(jnp.int32, (L, L), 0)
    col = lax.broadcasted_iota(jnp.int32, (L, L), 1)
    causal = col <= row
    tri = causal.astype(BF16)
    tri_t = (row <= col).astype(BF16)
    lane = lax.broadcasted_iota(jnp.int32, gc.shape, 1)
    srow = lax.broadcasted_iota(jnp.int32, gr.shape, 0)

    def bf16_parts(x):
        hi = x.astype(BF16)
        r1 = x - hi.astype(F32)
        mid = r1.astype(BF16)
        return hi, mid, (r1 - mid.astype(F32)).astype(BF16)

    b_col_all = sum(jnp.dot(tri, p, preferred_element_type=F32)
                    for p in bf16_parts(jnp.where(lane >= H_M, gc, 0.0)))
    b_row_all = sum(jnp.dot(p, tri_t, preferred_element_type=F32)
                    for p in bf16_parts(jnp.where(srow >= H_M, gr, 0.0)))

    v_all = v_ref[0]
    og = og_ref[0]
    for h in range(H_M):
        q = qk[:, h * DK_M:(h + 1) * DK_M]
        k = qk[:, D_M + h * DK_M:D_M + (h + 1) * DK_M] * (DK_M ** -0.5)
        v = v_all[:, h * DK_M:(h + 1) * DK_M]
        li_row = gr[h:h + 1, :]
        li_col = gc[:, h:h + 1]
        b_row = b_row_all[H_M + h:H_M + h + 1, :]
        b_col = b_col_all[:, H_M + h:H_M + h + 1]
        m0 = m_s[h:h + 1, 0:1]
        c0 = c_s[h]
        n0 = n_s[h:h + 1, :]

        d = jnp.where(causal, b_col - b_row + li_row, NEG_INF)
        g_state = b_col + m0
        m = jnp.maximum(jnp.max(d, axis=-1, keepdims=True), g_state)
        w_state = jnp.exp(g_state - m)
        qb = q.astype(BF16)
        kb = k.astype(BF16)
        vb = v.astype(BF16)
        s = lax.dot_general(qb, kb, (((1,), (1,)), ((), ())), preferred_element_type=F32)
        s = s * jnp.exp(d - m)
        num = jnp.dot(s.astype(BF16), vb, preferred_element_type=F32)
        num = num + w_state * jnp.dot(qb, c0.astype(BF16), preferred_element_type=F32)
        den = jnp.sum(s, axis=-1, keepdims=True) + w_state * jnp.sum(q * n0, axis=-1, keepdims=True)
        hh = num / jnp.maximum(jnp.abs(den), jnp.exp(-m))

        b_last = b_col[L - 1:L, :]
        g_tok_row = b_last - b_row + li_row
        m_new = jnp.maximum(b_last + m0, jnp.max(g_tok_row, axis=-1, keepdims=True))
        w_tok_col = jnp.exp(b_last - b_col + li_col - m_new)
        decay = jnp.exp(b_last + m0 - m_new)
        kw = k * w_tok_col
        c_s[h] = decay * c0 + jnp.dot(kw.T.astype(BF16), vb, preferred_element_type=F32)
        n_s[h:h + 1, :] = decay * n0 + jnp.sum(kw, axis=0, keepdims=True)
        m_s[h:h + 1, :] = jnp.broadcast_to(m_new, (1, LANES))

        hn = hh * lax.rsqrt(jnp.mean(hh * hh, axis=-1, keepdims=True) + EPS)
        hn = hn * mh_ref[:, h * DK_M:(h + 1) * DK_M]
        mem_ref[0, :, h * DK_M:(h + 1) * DK_M] = og[:, h * DK_M:(h + 1) * DK_M] * hn

    @pl.when(step == pl.num_programs(1) - 1)
    def _():
        c_out_ref[0] = c_s[...]
        n_out_ref[0] = n_s[...]
        m_out_ref[0] = m_s[...]


def _mlstm(u, v, og, gates_col, gates_row, tail0, c0, n0, m0x, conv_w, conv_b, mh_norm, chunk):
    nb, rows, _ = u.shape
    nc = rows // chunk
    row_map = lambda b, c: (b, c, 0)
    bat3 = lambda b, c: (b, 0, 0)
    const2 = lambda b, c: (0, 0)
    return pl.pallas_call(
        functools.partial(_mlstm_kernel, chunk=chunk),
        grid=(nb, nc),
        in_specs=[
            pl.BlockSpec((1, chunk, 2 * D_M), row_map),
            pl.BlockSpec((1, chunk, D_M), row_map),
            pl.BlockSpec((1, chunk, D_M), row_map),
            pl.BlockSpec((1, chunk, LANES), row_map),
            pl.BlockSpec((1, 8, chunk), lambda b, c: (b, 0, c)),
            pl.BlockSpec((1, 8, 2 * D_M), bat3),
            pl.BlockSpec((1, H_M, DK_M, DK_M), lambda b, c: (b, 0, 0, 0)),
            pl.BlockSpec((1, H_M, DK_M), bat3),
            pl.BlockSpec((1, H_M, LANES), bat3),
            pl.BlockSpec((CONV_W, 2 * D_M), const2),
            pl.BlockSpec((1, 2 * D_M), const2),
            pl.BlockSpec((1, D_M), const2),
        ],
        out_specs=[
            pl.BlockSpec((1, chunk, D_M), row_map),
            pl.BlockSpec((1, H_M, DK_M, DK_M), lambda b, c: (b, 0, 0, 0)),
            pl.BlockSpec((1, H_M, DK_M), bat3),
            pl.BlockSpec((1, H_M, LANES), bat3),
        ],
        out_shape=[
            jax.ShapeDtypeStruct((nb, rows, D_M), F32),
            jax.ShapeDtypeStruct((nb, H_M, DK_M, DK_M), F32),
            jax.ShapeDtypeStruct((nb, H_M, DK_M), F32),
            jax.ShapeDtypeStruct((nb, H_M, LANES), F32),
        ],
        scratch_shapes=[
            pltpu.VMEM((H_M, DK_M, DK_M), F32),
            pltpu.VMEM((H_M, DK_M), F32),
            pltpu.VMEM((H_M, LANES), F32),
            pltpu.VMEM((8, 2 * D_M), F32),
            pltpu.VMEM((chunk + 8, 2 * D_M), F32),
        ],
        compiler_params=_params(("arbitrary", "arbitrary"), vmem=16 * 1024 * 1024, fuse_inputs=12),
        name="mlstm",
    )(u, v, og, gates_col, gates_row, tail0, c0, n0, m0x,
      conv_w, conv_b.reshape(1, 2 * D_M), mh_norm.reshape(1, D_M))


def _attn_kernel(q_ref, kc_ref, vc_ref, kp_ref, vp_ref, bias_ref, ones_ref, o_ref, l_ref, k_s, v_s, *, ql):
    first = pl.program_id(2) == 0
    k_s[0:QB, :] = kp_ref[0].astype(BF16)
    k_s[QB:QB + ql, :] = kc_ref[0].astype(BF16)
    v_s[0:QB, :] = vp_ref[0].astype(BF16)
    v_s[QB:QB + ql, :] = vc_ref[0].astype(BF16)
    col = lax.broadcasted_iota(jnp.int32, (QB, 2 * QB), 1)
    pad_keys = jnp.logical_and(first, col < QB)
    head0 = lax.broadcasted_iota(jnp.int32, (1, LANES), 1) < HD_A
    n_slab = D_A // LANES
    n_res = q_ref.shape[2] // D_A
    pad_all = jnp.concatenate([pad_keys] * H_A, axis=0)

    def raw_logits(c, j):
        sl = slice(c * LANES, (c + 1) * LANES)
        q = q_ref[0, j * QB:(j + 1) * QB, sl].astype(BF16)
        zero = jnp.zeros_like(q)
        q2 = jnp.concatenate([jnp.where(head0, q, zero), jnp.where(head0, zero, q)], axis=0)
        return lax.dot_general(q2, k_s[j * QB:(j + 2) * QB, sl], (((1,), (1,)), ((), ())),
                               preferred_element_type=F32)

    def softmax(j, logits):
        logits = logits + bias_ref[...].reshape(H_A * QB, 2 * QB)
        if j == 0:
            logits = jnp.where(pad_all, NEG_INF, logits)
        m = jnp.max(logits, axis=-1, keepdims=True)
        return jnp.exp(logits - m).astype(BF16), m

    def finish(c, j, p, m):
        sl = slice(c * LANES, (c + 1) * LANES)
        rows = slice(j * QB, (j + 1) * QB)
        p_cat = jnp.concatenate([p[:QB], p[QB:]], axis=1)
        v_all = v_s[j * QB:(j + 2) * QB, sl]
        v_bd = jnp.concatenate([jnp.where(head0, v_all, jnp.zeros_like(v_all)),
                                jnp.where(head0, jnp.zeros_like(v_all), v_all)], axis=0)
        o = jnp.dot(p_cat, v_bd, preferred_element_type=F32)
        s = jnp.dot(p_cat, ones_ref[...], preferred_element_type=F32)
        o_ref[0, rows, sl] = o / s
        l_ref[0, rows, sl] = jnp.where(head0, m[:QB], m[QB:]) + jnp.log(s)

    for res in range(n_res):
        slabs = range(res * n_slab, (res + 1) * n_slab)
        for j in range(ql // QB):
            p, m = softmax(j, jnp.concatenate([raw_logits(c, j) for c in slabs], axis=0))
            for n, c in enumerate(slabs):
                part = slice(2 * n * QB, 2 * (n + 1) * QB)
                finish(c, j, p[part], m[part])


def _attn_branch(qa, ka, va, bias, dil):
    nb, ld, _ = qa.shape
    ql = min(ATT_QL, ld)
    n_res = min(dil, ATT_QL // ql)
    width = n_res * D_A
    cur = lambda b, r, i: (b, i, r)
    prev = lambda b, r, i: (b, jnp.maximum(i * (ql // QB) - 1, 0), r)
    ones_bd = np.zeros((4 * QB, LANES), np.float32)
    ones_bd[:2 * QB, :HD_A] = 1.0
    ones_bd[2 * QB:, HD_A:] = 1.0
    return pl.pallas_call(
        functools.partial(_attn_kernel, ql=ql),
        grid=(nb, dil // n_res, ld // ql),
        in_specs=[
            pl.BlockSpec((1, ql, width), cur),
            pl.BlockSpec((1, ql, width), cur),
            pl.BlockSpec((1, ql, width), cur),
            pl.BlockSpec((1, QB, width), prev),
            pl.BlockSpec((1, QB, width), prev),
            pl.BlockSpec((H_A, QB, 2 * QB), lambda b, r, i: (0, 0, 0)),
            pl.BlockSpec((4 * QB, LANES), lambda b, r, i: (0, 0)),
        ],
        out_specs=[pl.BlockSpec((1, ql, width), cur), pl.BlockSpec((1, ql, width), cur)],
        out_shape=[jax.ShapeDtypeStruct((nb, ld, dil * D_A), F32)] * 2,
        scratch_shapes=[pltpu.VMEM((QB + ql, width), BF16), pltpu.VMEM((QB + ql, width), BF16)],
        compiler_params=_params(("arbitrary", "arbitrary", "arbitrary"), vmem=40 * 1024 * 1024, fuse_inputs=7),
        name=f"attn_d{dil}",
    )(qa, ka, va, ka, va, bias, jnp.asarray(ones_bd, BF16))


def _heads_kernel(k_ref, v_ref, ko_ref, vo_ref):
    for src, dst in ((k_ref, ko_ref), (v_ref, vo_ref)):
        for c in range(D_A // LANES):
            xt = src[0, :, c * LANES:(c + 1) * LANES].T
            dst[0, 2 * c] = xt[:HD_A]
            dst[0, 2 * c + 1] = xt[HD_A:]


def _window_heads(ka, va, keep, tm=512):
    nb, seq, _ = ka.shape
    first = (seq - keep) // tm
    in_spec = pl.BlockSpec((1, tm, D_A), lambda b, i: (b, first + i, 0))
    out_spec = pl.BlockSpec((1, H_A, HD_A, tm), lambda b, i: (b, 0, 0, i))
    kt, vt = pl.pallas_call(
        _heads_kernel,
        grid=(nb, keep // tm),
        in_specs=[in_spec, in_spec],
        out_specs=[out_spec, out_spec],
        out_shape=[jax.ShapeDtypeStruct((nb, H_A, HD_A, keep), F32)] * 2,
        compiler_params=_params(("arbitrary", "arbitrary"), vmem=16 * 1024 * 1024, fuse_inputs=2),
        name="window_heads",
    )(ka, va)
    return kt.transpose(0, 3, 1, 2), vt.transpose(0, 3, 1, 2)


def _t5_bucket(dist):
    max_exact = N_BUCKETS // 2
    n = jnp.maximum(dist, 1).astype(F32)
    large = max_exact + (jnp.log(n / max_exact) / math.log(MAX_DISTANCE / max_exact)
                         * (N_BUCKETS - max_exact)).astype(jnp.int32)
    large = jnp.minimum(large, N_BUCKETS - 1)
    return jnp.where(dist < max_exact, dist, large)


def _branch_bias_vec(rel_bias, window, dil):
    return rel_bias[_t5_bucket(dil * jnp.arange(window // dil + 1))].astype(F32)


def _prompt_bias(rel_bias, window, dil):
    assert window // dil == QB
    n = 2 * QB
    vec = _branch_bias_vec(rel_bias, window, dil)
    vecp = jnp.concatenate([vec[::-1], jnp.full((n - QB - 1, H_A), NEG_INF, F32)], axis=0)
    g = jnp.roll(vecp, QB - 1, axis=0)
    skew = jnp.tile(g, (QB + 1, 1))[:QB * (n + 1)].reshape(QB, n + 1, H_A)[:, :n]
    return skew[::-1].transpose(2, 0, 1)


TOK_PAD = 8


CACHE_SLOTS = 3


def _sattn_kernel(q_ref, kt_hbm, vt_hbm, knt_ref, vnt_ref, bc_ref, bn_ref, att_ref, kbuf, vbuf, sem):
    b = pl.program_id(0)
    nb = pl.num_programs(0)

    def copies(entry, slot):
        return (pltpu.make_async_copy(kt_hbm.at[entry], kbuf.at[slot], sem.at[0, slot]),
                pltpu.make_async_copy(vt_hbm.at[entry], vbuf.at[slot], sem.at[1, slot]))

    def request(entry):
        for c in copies(entry, lax.rem(entry, CACHE_SLOTS)):
            c.start()

    @pl.when(b == 0)
    def _():
        for first in range(CACHE_SLOTS - 1):
            @pl.when(first < nb)
            def _():
                request(jnp.int32(first))

    ahead = b + (CACHE_SLOTS - 1)

    @pl.when(ahead < nb)
    def _():
        request(ahead)

    slot = lax.rem(b, CACHE_SLOTS)
    for c in copies(b, slot):
        c.wait()
    nt = (((1,), (1,)), ((), ()))
    for h in range(H_A):
        q = q_ref[0, h].astype(BF16)
        lc = jnp.dot(q, kbuf[slot, h].astype(BF16), preferred_element_type=F32)
        ln = jnp.dot(q, knt_ref[0, h].astype(BF16), preferred_element_type=F32)
        lcs = [lc + bc_ref[d, h] for d in range(len(BRANCHES))]
        lns = [ln + bn_ref[d, h] for d in range(len(BRANCHES))]
        m = functools.reduce(jnp.maximum, [jnp.max(x, axis=-1, keepdims=True) for x in lcs + lns])
        pc = functools.reduce(jnp.add, [jnp.exp(x - m) for x in lcs])
        pn = functools.reduce(jnp.add, [jnp.exp(x - m) for x in lns])
        s = jnp.sum(pc, axis=-1, keepdims=True) + jnp.sum(pn, axis=-1, keepdims=True)
        o = lax.dot_general(pc.astype(BF16), vbuf[slot, h].astype(BF16), nt, preferred_element_type=F32)
        o = o + lax.dot_general(pn.astype(BF16), vnt_ref[0, h].astype(BF16), nt, preferred_element_type=F32)
        att_ref[0, h] = o / s


def _sample_attention(q_h, kt, vt, knt, vnt, bias_c, bias_n):
    nb, _, _, n_past = kt.shape
    bat = lambda b: (b, 0, 0, 0)
    const = lambda b: (0, 0, 0, 0)
    return pl.pallas_call(
        _sattn_kernel,
        grid=(nb,),
        in_specs=[
            pl.BlockSpec((1, H_A, TOK_PAD, HD_A), bat),
            pl.BlockSpec(memory_space=pl.ANY),
            pl.BlockSpec(memory_space=pl.ANY),
            pl.BlockSpec((1, H_A, HD_A, LANES), bat),
            pl.BlockSpec((1, H_A, HD_A, LANES), bat),
            pl.BlockSpec(bias_c.shape, const),
            pl.BlockSpec(bias_n.shape, const),
        ],
        out_specs=pl.BlockSpec((1, H_A, TOK_PAD, HD_A), bat),
        out_shape=jax.ShapeDtypeStruct((nb, H_A, TOK_PAD, HD_A), F32),
        scratch_shapes=[pltpu.VMEM((CACHE_SLOTS, H_A, HD_A, n_past), F32),
                        pltpu.VMEM((CACHE_SLOTS, H_A, HD_A, n_past), F32),
                        pltpu.SemaphoreType.DMA((2, CACHE_SLOTS))],
        compiler_params=_params(("arbitrary",), vmem=36 * 1024 * 1024, fuse_inputs=7),
        name="sample_attn",
    )(q_h, kt, vt, knt, vnt, bias_c, bias_n)


def _sample_bias(rel_bias, n_tok, n_past):
    ninf = lambda n: jnp.full((n, H_A), NEG_INF, F32)
    cached, new = [], []
    for window, dil in BRANCHES:
        nk = window // dil
        rows = n_past // dil
        assert n_past % dil == 0 and rows >= nk
        vec = _branch_bias_vec(rel_bias, window, dil)
        res = jnp.arange(dil)[None, :, None]
        tc, tn = [], []
        for t in range(n_tok):
            near = t // dil + 1
            col = jnp.concatenate([ninf(rows - (nk + 1 - near)), vec[near:nk + 1][::-1]], axis=0)
            tab = jnp.where(res == t % dil, col[:, None, :], NEG_INF).reshape(n_past, H_A)
            tc.append(tab)
            keys = [vec[(t - j) // dil][None] if (t - j) % dil == 0 and 0 <= t - j <= nk * dil else ninf(1)
                    for j in range(n_tok)]
            tn.append(jnp.concatenate(keys + [ninf(LANES - n_tok)], axis=0))
        pad = lambda tabs: jnp.stack(tabs + [jnp.zeros_like(tabs[0])] * (TOK_PAD - n_tok))
        cached.append(pad(tc).transpose(2, 0, 1))
        new.append(pad(tn).transpose(2, 0, 1))
    return jnp.stack(cached), jnp.stack(new)


def _postmix_kernel(mem_ref, o1_ref, l1_ref, o2_ref, l2_ref, o3_ref, l3_ref, x_ref,
                    g1_ref, sc2_ref, sh2_ref, gpost_ref, gpre_ref, wo_ref, wr_ref, br_ref, cnt0_ref,
                    x1_ref, h2_ref, idx_ref, gate_ref, cnt_ref, cnt_s, slab_s, *, dilations):
    first_step = jnp.logical_and(pl.program_id(0) == 0, pl.program_id(1) == 0)

    @pl.when(first_step)
    def _():
        cnt_s[...] = cnt0_ref[...]

    n_slab = D_A // LANES
    tm = x_ref.shape[1]

    def rows(ref, dil, slot):
        if dil == 1:
            return ref[0]
        for r in range(dil):
            for c in range(n_slab):
                lo = r * D_A + c * LANES
                slab_s[slot, c, pl.ds(r, tm // dil, stride=dil), :] = ref[0, :, lo:lo + LANES]
        return jnp.concatenate([slab_s[slot, c] for c in range(n_slab)], axis=1)

    o1, o2, o3 = (rows(r, d, n) for n, (r, d) in enumerate(zip((o1_ref, o2_ref, o3_ref), dilations)))
    l1, l2, l3 = (rows(r, d, 3 + n) for n, (r, d) in enumerate(zip((l1_ref, l2_ref, l3_ref), dilations)))
    lmax = jnp.maximum(jnp.maximum(l1, l2), l3)
    w1 = jnp.exp(l1 - lmax)
    w2 = jnp.exp(l2 - lmax)
    w3 = jnp.exp(l3 - lmax)
    att = (w1 * o1 + w2 * o2 + w3 * o3) / (w1 + w2 + w3)
    y = jnp.dot(mem_ref[0].astype(BF16), wo_ref[0:D_M, :], preferred_element_type=F32)
    y = y + jnp.dot(att.astype(BF16), wo_ref[D_M:, :], preferred_element_type=F32)
    x1 = x_ref[0] + g1_ref[0] * _rms(y, gpost_ref[...])
    x1_ref[0] = x1
    h2 = _rms(x1, gpre_ref[...]) * (1.0 + sc2_ref[0]) + sh2_ref[0]
    h2_ref[0] = h2
    h_hi = h2.astype(BF16)
    h_lo = (h2 - h_hi.astype(F32)).astype(BF16)
    w_hi = wr_ref[0]
    logits = jnp.dot(h_hi, w_hi, preferred_element_type=F32)
    logits = logits + jnp.dot(h_lo, w_hi, preferred_element_type=F32)
    logits = logits + jnp.dot(h_hi, wr_ref[1], preferred_element_type=F32) + br_ref[...]
    lane = lax.broadcasted_iota(jnp.int32, logits.shape, 1)
    lane_f = lane.astype(F32)
    logits = jnp.where(lane < N_EXPERTS, logits, NEG_INF)
    vals, idxs = [], []
    for _ in range(TOP_K):
        mx = jnp.max(logits, axis=-1, keepdims=True)
        ix = jnp.min(jnp.where(logits == mx, lane_f, float(LANES)), axis=-1, keepdims=True)
        vals.append(mx)
        idxs.append(ix)
        logits = jnp.where(lane_f == ix, NEG_INF, logits)
    es = [jnp.exp(v - vals[0]) for v in vals]
    tot = es[0] + es[1] + es[2] + es[3]
    tm = logits.shape[0]
    picks = [(lane_f == ix).astype(F32) for ix in idxs]
    chosen = picks[0] + picks[1] + picks[2] + picks[3]
    r_i = lax.broadcasted_iota(jnp.int32, (tm, tm), 0)
    c_i = lax.broadcasted_iota(jnp.int32, (tm, tm), 1)
    earlier = jnp.dot((c_i < r_i).astype(BF16), chosen.astype(BF16), preferred_element_type=F32)
    before = earlier + cnt_s[...]
    cnt_s[...] = cnt_s[...] + jnp.sum(chosen, axis=0, keepdims=True)
    cnt_ref[...] = cnt_s[...]
    idx_tile = jnp.zeros(logits.shape, F32)
    gate_tile = jnp.zeros(logits.shape, F32)
    for k in range(TOP_K):
        rank = jnp.sum(picks[k] * before, axis=-1, keepdims=True)
        idx_tile = jnp.where(lane == k, idxs[k], idx_tile)
        idx_tile = jnp.where(lane == TOP_K + k, rank, idx_tile)
        gate_tile = jnp.where(lane == k, es[k] / tot, gate_tile)
    idx_ref[0] = idx_tile.astype(jnp.int32)
    gate_ref[0] = gate_tile


def _postmix(mem, branches, x3, gate1, scale2, shift2, g_post, g_pre, w_out_b, w_router_p, b_router_p, counts0, tm,
             dilations=(1, 1, 1)):
    nb, rows, _ = x3.shape
    per_row_mod = gate1.shape[1] != 1
    mod_block = (1, tm, D_MODEL) if per_row_mod else (1, 1, D_MODEL)
    mod_map = (lambda b, i: (b, i, 0)) if per_row_mod else (lambda b, i: (b, 0, 0))
    row_map = lambda b, i: (b, i, 0)
    const2 = lambda b, i: (0, 0)
    half = pl.BlockSpec((1, tm, D_M), row_map)
    flat = [a for pair in branches for a in pair]
    branch_specs = [pl.BlockSpec((1, tm // d, d * D_A), row_map) for d in dilations for _ in range(2)]
    return pl.pallas_call(
        functools.partial(_postmix_kernel, dilations=tuple(dilations)),
        grid=(nb, rows // tm),
        in_specs=[half] + branch_specs + [
            pl.BlockSpec((1, tm, D_MODEL), row_map),
            pl.BlockSpec(mod_block, mod_map),
            pl.BlockSpec(mod_block, mod_map),
            pl.BlockSpec(mod_block, mod_map),
            pl.BlockSpec((1, D_MODEL), const2),
            pl.BlockSpec((1, D_MODEL), const2),
            pl.BlockSpec((D_MODEL, D_MODEL), const2),
            pl.BlockSpec((2, D_MODEL, LANES), lambda b, i: (0, 0, 0)),
            pl.BlockSpec((1, LANES), const2),
            pl.BlockSpec((1, LANES), const2),
        ],
        out_specs=[
            pl.BlockSpec((1, tm, D_MODEL), row_map),
            pl.BlockSpec((1, tm, D_MODEL), row_map),
            pl.BlockSpec((1, tm, LANES), row_map),
            pl.BlockSpec((1, tm, LANES), row_map),
            pl.BlockSpec((1, LANES), const2),
        ],
        out_shape=[
            jax.ShapeDtypeStruct((nb, rows, D_MODEL), F32),
            jax.ShapeDtypeStruct((nb, rows, D_MODEL), F32),
            jax.ShapeDtypeStruct((nb, rows, LANES), jnp.int32),
            jax.ShapeDtypeStruct((nb, rows, LANES), F32),
            jax.ShapeDtypeStruct((1, LANES), F32),
        ],
        scratch_shapes=[pltpu.VMEM((1, LANES), F32), pltpu.VMEM((6, D_A // LANES, tm, LANES), F32)],
        compiler_params=_params(("arbitrary", "arbitrary"), fuse_inputs=17),
        name="postmix",
    )(mem, *flat, x3, gate1, scale2, shift2, g_post.reshape(1, D_MODEL), g_pre.reshape(1, D_MODEL),
      w_out_b, w_router_p, b_router_p, counts0)


def _row_wait(src_rows, dst_rows, sem):
    pltpu.make_async_copy(src_rows, dst_rows, sem).wait()


def _scatter_kernel(slot_ref, hp_ref, hs_ref, xs_hbm, buf, sem, *, n_prompt_tiles):
    h = jnp.where(pl.program_id(0) >= n_prompt_tiles, hs_ref[...], hp_ref[...])
    bits = pltpu.bitcast(h.astype(BF16).astype(F32), U32)
    packed = (bits[:, :PACK_W] & U32(HI_MASK)) | (bits[:, PACK_W:] >> 16)
    buf[...] = pltpu.einshape("m(jc)->mjc", packed, j=PACK_T)

    def issue(r, carry):
        src = buf.at[pl.ds(r, 1)]
        for k in range(TOP_K):
            pltpu.make_async_copy(src, xs_hbm.at[pl.ds(slot_ref[0, 0, r * TOP_K + k], 1)],
                                  sem).start(priority=k % 2)
        return carry

    lax.fori_loop(0, TOK_TILE, issue, 0, unroll=4)
    for _ in range(TOP_K):
        _row_wait(buf, xs_hbm.at[pl.ds(0, TOK_TILE)], sem)


def _moe_scatter(h_p, h_s, slot_tiles):
    tiles = slot_tiles.shape[0]
    n_p = h_p.shape[0] // TOK_TILE
    assert tiles == n_p + 1 and h_s.shape[0] == TOK_TILE
    n_rows = tiles * TOK_TILE * TOP_K
    return pl.pallas_call(
        functools.partial(_scatter_kernel, n_prompt_tiles=n_p),
        grid=(tiles,),
        in_specs=[
            pl.BlockSpec((1, 1, TOK_TILE * TOP_K), lambda i: (i, 0, 0), memory_space=pltpu.SMEM),
            pl.BlockSpec((TOK_TILE, D_MODEL), lambda i: (jnp.minimum(i, n_p - 1), 0)),
            pl.BlockSpec((TOK_TILE, D_MODEL), lambda i: (0, 0)),
        ],
        out_specs=pl.BlockSpec(memory_space=pl.ANY),
        out_shape=jax.ShapeDtypeStruct((n_rows, PACK_T, LANES), U32),
        scratch_shapes=[pltpu.VMEM((TOK_TILE, PACK_T, LANES), U32), pltpu.SemaphoreType.DMA],
        compiler_params=_params(("arbitrary",), vmem=8 * 1024 * 1024),
        name="moe_scatter",
    )(slot_tiles, h_p, h_s)


def _expert_kernel(tile_ref, exp_ref, next_ref, lo_ref, hi_ref, first_ref, nvis_ref,
                   xs_ref, w1_hbm, b1_ref, w2_hbm, b2_ref, ys_ref, w1f, w2f, w1b, w2b, sem):
    v = pl.program_id(0)
    active = v < nvis_ref[0]
    changed = jnp.logical_or(v == 0, exp_ref[v] != exp_ref[jnp.maximum(v - 1, 0)])
    lo = lo_ref[v]
    hi = hi_ref[v]

    def weight_copies(e):
        return (pltpu.make_async_copy(w1_hbm.at[e], w1f, sem.at[0]),
                pltpu.make_async_copy(w2_hbm.at[e], w2f, sem.at[1]))

    @pl.when(v == 0)
    def _():
        for c in weight_copies(exp_ref[0]):
            c.start()

    @pl.when(jnp.logical_and(active, changed))
    def _():
        for c in weight_copies(exp_ref[v]):
            c.wait()
        w1b[...] = w1f[...].astype(BF16)
        w2b[...] = w2f[...].astype(BF16)

        @pl.when(next_ref[v] >= 0)
        def _():
            for c in weight_copies(next_ref[v]):
                c.start()

    def mlp_rows(r0):
        w = pltpu.einshape("mjc->m(jc)", xs_ref[r0:r0 + MOE_SUB])
        x_hi = pltpu.bitcast(w & U32(HI_MASK), F32).astype(BF16)
        x_lo = pltpu.bitcast(w << 16, F32).astype(BF16)
        hgu = jnp.dot(x_hi, w1b[0:PACK_W, :], preferred_element_type=F32)
        hgu = hgu + jnp.dot(x_lo, w1b[PACK_W:, :], preferred_element_type=F32) + b1_ref[0]
        x_glu = jnp.minimum(hgu[:, :D_FF], SWIGLU_LIMIT)
        x_lin = jnp.clip(hgu[:, D_FF:], -SWIGLU_LIMIT, SWIGLU_LIMIT)
        act = x_glu * jax.nn.sigmoid(SWIGLU_ALPHA * x_glu) * (x_lin + 1.0)
        y = jnp.dot(act.astype(BF16), w2b[...], preferred_element_type=F32) + b2_ref[0]
        return pltpu.einshape("m(jc)->mjc", y, j=ROW_T)

    subs = [r * MOE_SUB for r in range(MOE_M // MOE_SUB)]
    whole = jnp.logical_and(lo == 0, hi == MOE_M)

    @pl.when(jnp.logical_and(active, whole))
    def _():
        for r0 in subs:
            ys_ref[r0:r0 + MOE_SUB] = mlp_rows(r0)

    shared = jnp.logical_and(active, jnp.logical_not(whole))

    @pl.when(jnp.logical_and(shared, first_ref[v] == 1))
    def _():
        ys_ref[...] = jnp.zeros(ys_ref.shape, F32)

    for r0 in subs:
        @pl.when(jnp.logical_and(shared, jnp.logical_and(lo < r0 + MOE_SUB, hi > r0)))
        def _():
            rows = r0 + lax.broadcasted_iota(jnp.int32, (MOE_SUB, 1, 1), 0)
            mine = jnp.logical_and(rows >= lo, rows < hi)
            ys_ref[r0:r0 + MOE_SUB] = jnp.where(mine, mlp_rows(r0), ys_ref[r0:r0 + MOE_SUB])


def _moe_experts(xs, visits, w1, b1, w2, b2):
    n_rows = xs.shape[0]
    n_vis = visits[0].shape[0]
    tile_map = lambda v, tile, *_: (tile[v], 0, 0)
    exp_map = lambda v, tile, exp, *_: (exp[v], 0, 0)
    return pl.pallas_call(
        _expert_kernel,
        grid_spec=pltpu.PrefetchScalarGridSpec(
            num_scalar_prefetch=7,
            grid=(n_vis,),
            in_specs=[
                pl.BlockSpec((MOE_M, PACK_T, LANES), tile_map),
                pl.BlockSpec(memory_space=pl.ANY),
                pl.BlockSpec((1, 1, 2 * D_FF), exp_map),
                pl.BlockSpec(memory_space=pl.ANY),
                pl.BlockSpec((1, 1, D_MODEL), exp_map),
            ],
            out_specs=pl.BlockSpec((MOE_M, ROW_T, LANES), tile_map),
            scratch_shapes=[pltpu.VMEM((D_MODEL, 2 * D_FF), F32), pltpu.VMEM((D_FF, D_MODEL), F32),
                            pltpu.VMEM((D_MODEL, 2 * D_FF), BF16), pltpu.VMEM((D_FF, D_MODEL), BF16),
                            pltpu.SemaphoreType.DMA((2,))],
        ),
        out_shape=jax.ShapeDtypeStruct((n_rows, ROW_T, LANES), F32),
        compiler_params=_params(("arbitrary",), vmem=44 * 1024 * 1024),
        name="moe_experts",
    )(*visits, xs, w1, b1.reshape(N_EXPERTS, 1, 2 * D_FF), w2, b2.reshape(N_EXPERTS, 1, D_MODEL))


def _combine_kernel(slot_ref, next_ref, gatep_ref, gates_ref, x1p_ref, x1s_ref, g2p_ref, g2s_ref, gpost_ref, ys_hbm,
                    yp_ref, ysm_ref, buf, sem, *, n_prompt_tiles):
    i = pl.program_id(0)
    last = pl.num_programs(0) - 1
    cur = lax.rem(i, 2)

    def gather(slots, b):
        def issue(r, carry):
            for k in range(TOP_K):
                pltpu.make_async_copy(ys_hbm.at[pl.ds(slots[0, 0, r * TOP_K + k], 1)],
                                      buf.at[b, pl.ds(k * TOK_TILE + r, 1)], sem.at[b]).start(priority=k % 2)
            return carry

        lax.fori_loop(0, TOK_TILE, issue, 0, unroll=4)

    @pl.when(i == 0)
    def _():
        gather(slot_ref, cur)

    @pl.when(i < last)
    def _():
        gather(next_ref, 1 - cur)

    for k in range(TOP_K):
        _row_wait(ys_hbm.at[pl.ds(0, TOK_TILE)], buf.at[cur, pl.ds(k * TOK_TILE, TOK_TILE)], sem.at[cur])
    is_sample = i >= n_prompt_tiles
    g = jnp.where(is_sample, gates_ref[...], gatep_ref[...])
    f = jnp.zeros((TOK_TILE, D_MODEL), F32)
    for k in range(TOP_K):
        rows = pltpu.einshape("mjc->m(jc)", buf[cur, pl.ds(k * TOK_TILE, TOK_TILE)])
        f = f + rows * g[:, k:k + 1]
    x1 = jnp.where(is_sample, x1s_ref[...], x1p_ref[...])
    g2 = jnp.where(is_sample, g2s_ref[...], g2p_ref[0])
    y = x1 + g2 * _rms(f, gpost_ref[...])

    @pl.when(jnp.logical_not(is_sample))
    def _():
        yp_ref[...] = y

    @pl.when(is_sample)
    def _():
        ysm_ref[...] = y


def _moe_combine(ys, slot_tiles, gates_p, gates_s, x1_p, x1_s, gate2_p, gate2_s, g_post, rows_per_batch):
    tiles = slot_tiles.shape[0]
    n_p = x1_p.shape[0] // TOK_TILE
    assert tiles == n_p + 1 and x1_s.shape[0] == TOK_TILE
    tiles_per_batch = rows_per_batch // TOK_TILE
    p_tile = lambda i: (jnp.minimum(i, n_p - 1), 0)
    const2 = lambda i: (0, 0)
    return pl.pallas_call(
        functools.partial(_combine_kernel, n_prompt_tiles=n_p),
        grid=(tiles,),
        in_specs=[
            pl.BlockSpec((1, 1, TOK_TILE * TOP_K), lambda i: (i, 0, 0), memory_space=pltpu.SMEM),
            pl.BlockSpec((1, 1, TOK_TILE * TOP_K), lambda i: (jnp.minimum(i + 1, tiles - 1), 0, 0),
                         memory_space=pltpu.SMEM),
            pl.BlockSpec((TOK_TILE, LANES), p_tile),
            pl.BlockSpec((TOK_TILE, LANES), const2),
            pl.BlockSpec((TOK_TILE, D_MODEL), p_tile),
            pl.BlockSpec((TOK_TILE, D_MODEL), const2),
            pl.BlockSpec((1, 1, D_MODEL), lambda i: (jnp.minimum(i, n_p - 1) // tiles_per_batch, 0, 0)),
            pl.BlockSpec((TOK_TILE, D_MODEL), const2),
            pl.BlockSpec((1, D_MODEL), const2),
            pl.BlockSpec(memory_space=pl.ANY),
        ],
        out_specs=[pl.BlockSpec((TOK_TILE, D_MODEL), p_tile), pl.BlockSpec((TOK_TILE, D_MODEL), const2)],
        out_shape=[jax.ShapeDtypeStruct(x1_p.shape, F32), jax.ShapeDtypeStruct(x1_s.shape, F32)],
        scratch_shapes=[pltpu.VMEM((2, TOP_K * TOK_TILE, ROW_T, LANES), F32), pltpu.SemaphoreType.DMA((2,))],
        compiler_params=_params(("arbitrary",), vmem=16 * 1024 * 1024),
        name="moe_combine",
    )(slot_tiles, slot_tiles, gates_p, gates_s, x1_p, x1_s, gate2_p, gate2_s, g_post.reshape(1, D_MODEL), ys)


def _moe_plan(routes, counts):
    a = sum(r.shape[0] for r in routes) * TOP_K
    assert a % MOE_M == 0
    g_end = jnp.cumsum(counts)
    g_start = g_end - counts

    def slots(route):
        flat_e = route[:, :TOP_K].reshape(-1)
        rank = route[:, TOP_K:2 * TOP_K].reshape(-1)
        onehot = flat_e[:, None] == jnp.arange(N_EXPERTS)[None, :]
        return (jnp.sum(jnp.where(onehot, g_start[None, :], 0), axis=1) + rank).astype(jnp.int32)

    slot_of = jnp.concatenate([slots(r) for r in routes])

    n_tiles = a // MOE_M
    n_vis = n_tiles + N_EXPERTS - 1
    first_tile = g_start // MOE_M
    last_tile = jnp.where(counts > 0, (g_end - 1) // MOE_M, first_tile)
    nvis_e = jnp.where(counts > 0, last_tile - first_tile + 1, 0)
    v_end = jnp.cumsum(nvis_e)
    v_start = v_end - nvis_e
    total = v_end[-1]
    v = jnp.arange(n_vis)
    vc = jnp.minimum(v, total - 1)
    e_of = jnp.minimum(jnp.sum((vc[:, None] >= v_end[None, :]).astype(jnp.int32), axis=1), N_EXPERTS - 1)
    is_e = e_of[:, None] == jnp.arange(N_EXPERTS)[None, :]
    of_e = lambda t: jnp.sum(jnp.where(is_e, t[None, :], 0), axis=1)
    tile_of = of_e(first_tile) + (vc - of_e(v_start))
    lo = jnp.maximum(of_e(g_start), tile_of * MOE_M) - tile_of * MOE_M
    hi = jnp.minimum(of_e(g_end), (tile_of + 1) * MOE_M) - tile_of * MOE_M
    prev_tile = jnp.concatenate([jnp.full((1,), -1, tile_of.dtype), tile_of[:-1]])
    first = (tile_of != prev_tile).astype(jnp.int32)
    experts = jnp.arange(N_EXPERTS)
    later = (experts[None, :] > experts[:, None]) & (counts[None, :] > 0)
    next_e = jnp.min(jnp.where(later, experts[None, :], N_EXPERTS), axis=1)
    next_e = jnp.where(next_e < N_EXPERTS, next_e, -1)
    i32 = lambda t: t.astype(jnp.int32)
    visits = (i32(tile_of), i32(e_of), i32(of_e(next_e)), i32(lo), i32(hi), first, i32(total).reshape(1))
    return slot_of, visits


def _reorder_w_in(w_in):
    off_om_end = 4 * D_M
    off_g_end = off_om_end + 2 * H_M
    w = jnp.concatenate([
        w_in[:, :off_om_end],
        w_in[:, off_g_end:],
        w_in[:, off_om_end:off_g_end],
        jnp.zeros((D_MODEL, LANES - 2 * H_M), w_in.dtype),
    ], axis=1)
    return w.astype(BF16)


def kernel(x_prompt, x_sample, cache_k, cache_v, state_conv, state_C, state_n, state_m, c_prompt, c_sample, w_ada, b_ada, g_pre_mix, g_post_mix, g_pre_ffn, g_post_ffn, w_in, b_ig, b_fg, conv_w, conv_b, mh_norm, rel_bias, w_out, w_router, b_router, w1, b1, w2, b2):
    depth = w_in.shape[0]
    assert depth == 1
    l = 0
    nb_p, seq, _ = x_prompt.shape
    nb_s, n_tok, _ = x_sample.shape
    rows_s = nb_s * n_tok
    assert rows_s == TOK_TILE

    n_c = nb_p + nb_s
    c_pad = -(-n_c // 8) * 8
    c_all = jnp.concatenate([c_prompt, c_sample, jnp.zeros((c_pad - n_c, D_MODEL), F32)], axis=0)
    mod = _adaln(c_all, w_ada[l], b_ada[l])
    mods = [mod[:, i * D_MODEL:(i + 1) * D_MODEL] for i in range(6)]
    mods_p = [m[:nb_p].reshape(nb_p, 1, D_MODEL) for m in mods]
    mods_s = [jnp.repeat(m[nb_p:n_c], n_tok, axis=0).reshape(1, rows_s, D_MODEL) for m in mods]

    w_r = _reorder_w_in(w_in[l])
    bias_g = jnp.concatenate([b_ig[l], b_fg[l], jnp.zeros((LANES - 2 * H_M,), F32)]).reshape(1, LANES)
    w_out_b = w_out[l].astype(BF16)
    w_router_f = jnp.concatenate([w_router[l], jnp.zeros((D_MODEL, LANES - N_EXPERTS), F32)], axis=1)
    w_router_hi = w_router_f.astype(BF16)
    w_router_p = jnp.stack([w_router_hi, (w_router_f - w_router_hi.astype(F32)).astype(BF16)])
    b_router_p = jnp.concatenate([b_router[l].astype(F32), jnp.zeros((LANES - N_EXPERTS,), F32)]).reshape(1, LANES)

    shift1, scale1, gate1, shift2, scale2, gate2_p = mods_p
    dils = tuple(d for _, d in BRANCHES)
    assert dils[0] == 1
    u, v_m, og, gts, qa, ka, va, *dilated = _inproj(x_prompt, g_pre_mix[l], scale1, shift1, w_r, bias_g, tm=512,
                                                   dilations=dils[1:])
    qkv_d = [(qa, ka, va)] + [tuple(dilated[3 * n:3 * n + 3]) for n in range(len(dils) - 1)]
    gates_row = gts[:, :, :8].transpose(0, 2, 1)
    zeros = lambda *s: jnp.zeros(s, F32)
    mem, c_p, n_p, m_p = _mlstm(u, v_m, og, gts, gates_row, zeros(nb_p, 8, 2 * D_M),
                                zeros(nb_p, H_M, DK_M, DK_M), zeros(nb_p, H_M, DK_M), zeros(nb_p, H_M, LANES),
                                conv_w[l], conv_b[l], mh_norm[l], MLSTM_L_PROMPT)
    branches = [_attn_branch(*qkv, _prompt_bias(rel_bias, w, d), d) for qkv, (w, d) in zip(qkv_d, BRANCHES)]
    x1_p, h2_p, idx_p, gate_p, cnt_p = _postmix(mem, branches, x_prompt, gate1, scale2, shift2,
                                                g_post_mix[l], g_pre_ffn[l], w_out_b, w_router_p, b_router_p,
                                                zeros(1, LANES), tm=512, dilations=dils)
    keep = min(WINDOW_MAX, seq)
    k_p, v_p = _window_heads(ka, va, keep)
    conv_p = u[:, seq - (CONV_W - 1):]

    shift1, scale1, gate1, shift2, scale2, gate2_s = mods_s
    x_s = x_sample.reshape(1, rows_s, D_MODEL)
    u_s, v_s, og_s, gts_s, qa_s, ka_s, va_s = _inproj(x_s, g_pre_mix[l], scale1, shift1, w_r, bias_g, tm=rows_s)
    per_b = lambda t: t.reshape(nb_s, n_tok, t.shape[-1])
    ls = MLSTM_L_SAMPLE
    pad_rows = lambda t, n: jnp.concatenate([t, jnp.zeros((nb_s, n - t.shape[1], t.shape[2]), t.dtype)], axis=1)
    inert = jnp.concatenate([jnp.full((H_M,), NEG_INF, F32), jnp.zeros((LANES - H_M,), F32)])
    gts_pad = jnp.concatenate([per_b(gts_s), jnp.broadcast_to(inert, (nb_s, ls - n_tok, LANES))], axis=1)
    tail0 = jnp.concatenate([zeros(nb_s, 8 - (CONV_W - 1), 2 * D_M), state_conv[l].astype(F32)], axis=1)
    m0x = jnp.broadcast_to(state_m[l].astype(F32)[:, :, None], (nb_s, H_M, LANES))
    mem_s, c_s, n_s, m_s = _mlstm(pad_rows(per_b(u_s), ls), pad_rows(per_b(v_s), ls), pad_rows(per_b(og_s), ls),
                                  gts_pad, gts_pad[:, :, :8].transpose(0, 2, 1), tail0,
                                  state_C[l].astype(F32), state_n[l].astype(F32), m0x,
                                  conv_w[l], conv_b[l], mh_norm[l], ls)
    mem_s = mem_s[:, :n_tok].reshape(1, rows_s, D_M)
    heads = lambda t: t.reshape(nb_s, n_tok, H_A, HD_A)
    k_new, v_new = heads(ka_s), heads(va_s)
    n_past = cache_k.shape[2]
    assert n_tok <= TOK_PAD
    q_h = jnp.pad(heads(qa_s).transpose(0, 2, 1, 3), ((0, 0), (0, 0), (0, TOK_PAD - n_tok), (0, 0)))
    new_t = lambda t: jnp.pad(t.transpose(0, 2, 3, 1), ((0, 0), (0, 0), (0, 0), (0, LANES - n_tok)))
    cache_t = lambda c: c[l].astype(F32).transpose(0, 2, 3, 1)
    att_s = _sample_attention(q_h, cache_t(cache_k), cache_t(cache_v), new_t(k_new), new_t(v_new),
                              *_sample_bias(rel_bias, n_tok, n_past))
    att_s = att_s[:, :, :n_tok].transpose(0, 2, 1, 3).reshape(1, rows_s, D_A)
    zero_o = zeros(1, rows_s, D_A)
    ninf = jnp.full((1, rows_s, D_A), NEG_INF, F32)
    branches_s = [(att_s, zero_o), (zero_o, ninf), (zero_o, ninf)]
    x1_s, h2_s, idx_s, gate_s, cnt_all = _postmix(mem_s, branches_s, x_s, gate1, scale2, shift2,
                                                  g_post_mix[l], g_pre_ffn[l], w_out_b, w_router_p, b_router_p,
                                                  cnt_p, tm=rows_s)

    t_p = nb_p * seq
    slot_of, visits = _moe_plan([idx_p.reshape(t_p, LANES), idx_s.reshape(rows_s, LANES)],
                                cnt_all[0, :N_EXPERTS].astype(jnp.int32))
    slot_tiles = slot_of.reshape((t_p + rows_s) // TOK_TILE, 1, TOK_TILE * TOP_K)
    xs = _moe_scatter(h2_p.reshape(t_p, D_MODEL), h2_s.reshape(rows_s, D_MODEL), slot_tiles)
    ys = _moe_experts(xs, visits, w1[l], b1[l], w2[l], b2[l])
    y_p, y_s = _moe_combine(ys, slot_tiles, gate_p.reshape(t_p, LANES), gate_s.reshape(rows_s, LANES), x1_p.reshape(t_p, D_MODEL), x1_s.reshape(rows_s, D_MODEL),
                            gate2_p, gate2_s.reshape(rows_s, D_MODEL), g_post_ffn[l], seq)
    y_prompt = y_p.reshape(nb_p, seq, D_MODEL)
    y_sample = y_s.reshape(nb_s, n_tok, D_MODEL)

    st = lambda t: t[None]
    conv_s = jnp.concatenate([state_conv[l].astype(F32), per_b(u_s)], axis=1)[:, -(CONV_W - 1):]
    return (y_prompt, y_sample, st(k_p), st(v_p), st(conv_p), st(c_p), st(n_p), st(m_p[:, :, 0]),
            st(k_new), st(v_new), st(conv_s), st(c_s), st(n_s), st(m_s[:, :, 0]))
```
